```python
import math
import jax
import jax.numpy as jnp
from jax import lax
import numpy as np

D_MODEL = 1024
BATCH = 1
SEQ = 16384
DEPTH = 2
DEC_BATCH = 32
DEC_SEQ = 1
PAST_LEN = 16384
PAGE_SIZE = 128

HEAD_DIM = 64
ROPE_THETA = 10000.0
N_A_LAYERS = (DEPTH + 1) // 2
N_C_LAYERS = DEPTH // 2
NSA_HEADS = 8
NSA_KV_HEADS = 2
NSA_GROUP = NSA_HEADS // NSA_KV_HEADS
CMP_BLOCK = 32
SEL_BLOCK = 64
CMP_PER_SEL = SEL_BLOCK // CMP_BLOCK
CMP_HIDDEN = 2 * HEAD_DIM
NSA_TOPN = 16
NSA_WINDOW = 512
NSA_QBLOCK = 128
NSA_FORCE = 1.0e4
GDN_HEADS = 8
GDN_CONV = 4
GDN_CHUNK = 64
DIL_HEADS = 16
DIL_GROUPS = ((128, 1), (512, 4), (2048, 16))
DIL_SPAN = 128
DIL_BLOCK = 128
DIL_MAX_WINDOW = 2048
D_FF = 2816
FFN_CONV = 3
DEEPNORM_ALPHA = (2.0 * DEPTH) ** 0.25
DEEPNORM_BETA = (8.0 * DEPTH) ** -0.25
LN_EPS = 1e-5
NORM_EPS = 1e-6
NSA_Q_W = NSA_HEADS * HEAD_DIM
NSA_KV_W = NSA_KV_HEADS * HEAD_DIM
GDN_W = GDN_HEADS * HEAD_DIM
DIL_W = DIL_HEADS * HEAD_DIM
PROJ_A_W = NSA_Q_W + 6 * NSA_KV_W + 3 * NSA_HEADS + 4 * GDN_W + 2 * GDN_HEADS

kernel_name = 'hybrid_nsa_gdn_dilated_convffn_step'


def _split_cols(h, widths):
    parts, start = [], 0
    for w in widths:
        parts.append(h[..., start:start + w])
        start += w
    return parts


def _even_widths():
    return (NSA_Q_W,) + (NSA_KV_W,) * 6 + (3 * NSA_HEADS, 3 * GDN_W, GDN_HEADS, GDN_HEADS, GDN_W)


def _layer_norm(x, g, b):
    xf = x.astype(jnp.float32)
    mu = jnp.mean(xf, axis=-1, keepdims=True)
    var = jnp.mean(jnp.square(xf - mu), axis=-1, keepdims=True)
    return (xf - mu) * lax.rsqrt(var + LN_EPS) * g + b


def _rms_norm(x, w):
    return x * lax.rsqrt(jnp.mean(jnp.square(x), axis=-1, keepdims=True) + NORM_EPS) * w


def _l2_norm(x):
    return x * lax.rsqrt(jnp.sum(jnp.square(x), axis=-1, keepdims=True) + NORM_EPS)


def _rope(x, pos):
    half = HEAD_DIM // 2
    inv_freq = ROPE_THETA ** (-2.0 * jnp.arange(half, dtype=jnp.float32) / HEAD_DIM)
    ang = pos.astype(jnp.float32)[:, None] * inv_freq[None, :]
    cos, sin = jnp.cos(ang)[:, None, :], jnp.sin(ang)[:, None, :]
    xf = x.astype(jnp.float32)
    x1, x2 = xf[..., :half], xf[..., half:]
    return jnp.concatenate([x1 * cos - x2 * sin, x2 * cos + x1 * sin], axis=-1)


def _causal_dwconv(hist, u, w):
    width, s = w.shape[0], u.shape[1]
    ext = jnp.concatenate([hist.astype(u.dtype), u], axis=1)
    out = w[0] * ext[:, :s]
    for j in range(1, width):
        out = out + w[j] * ext[:, j:j + s]
    return out, ext[:, s:]


def _masked_softmax(s, mask):
    s = jnp.where(mask, s, -jnp.inf)
    m = jnp.max(s, axis=-1, keepdims=True)
    m = jnp.where(jnp.isfinite(m), m, 0.0)
    e = jnp.where(mask, jnp.exp(s - m), 0.0)
    den = jnp.sum(e, axis=-1, keepdims=True)
    return e / jnp.where(den > 0.0, den, 1.0)


def _gather_pages(pool, page_table, layer_idx):
    rows = pool[page_table, layer_idx]
    return rows.reshape(rows.shape[0], -1, *rows.shape[3:])


def _nsa_compress(rows, w1, b1, w2, pe):
    b, l, g, dh = rows.shape
    nc = l // CMP_BLOCK
    blk = rows[:, :nc * CMP_BLOCK].astype(jnp.float32).reshape(b, nc, CMP_BLOCK, g, dh) + pe[:, None, :]
    flat = blk.transpose(0, 1, 3, 2, 4).reshape(b, nc, g, CMP_BLOCK * dh)
    return jax.nn.gelu(flat @ w1 + b1) @ w2


def _nsa_compressed_kv(k_rows, v_rows, cw1, cb1, cw2, cpe):
    ck = _nsa_compress(k_rows, cw1[0], cb1[0], cw2[0], cpe[0])
    cv = _nsa_compress(v_rows, cw1[1], cb1[1], cw2[1], cpe[1])
    nc = ck.shape[1]
    ck = _rope(ck, (jnp.arange(nc) + 1) * CMP_BLOCK - 1)
    return ck, cv


def _nsa_attend(q, qpos, ck, cv, sk, sv, wk, wv, wpos):
    b, nq = q.shape[:2]
    scale = HEAD_DIM ** -0.5
    nc, ns = ck.shape[1], sk.shape[2]
    cend = (jnp.arange(nc) + 1) * CMP_BLOCK - 1
    s = jnp.einsum('bqghd,bcgd->bghqc', q, ck) * scale
    p_cmp = _masked_softmax(s, cend[None, :] <= qpos[:, None])
    o_cmp = jnp.einsum('bghqc,bcgd->bqghd', p_cmp, cv)
    imp = jnp.sum(p_cmp, axis=2)
    imp = jnp.pad(imp, ((0, 0), (0, 0), (0, 0), (0, ns * CMP_PER_SEL - nc)))
    imp = imp.reshape(b, NSA_KV_HEADS, nq, ns, CMP_PER_SEL).sum(-1)
    blk = jnp.arange(ns)[None, :]
    cur = (qpos // SEL_BLOCK)[:, None]
    forced = (blk == 0) | (blk == cur) | (blk == cur - 1)
    imp = jnp.where(blk <= cur, jnp.where(forced, NSA_FORCE, imp), -1.0)
    _, idx = lax.top_k(imp, min(NSA_TOPN, ns))
    n = idx.shape[-1]
    pick = jax.vmap(jax.vmap(lambda kb, ix: kb[ix]))
    ksel = pick(sk, idx).reshape(b, NSA_KV_HEADS, nq, n * SEL_BLOCK, HEAD_DIM)
    vsel = pick(sv, idx).reshape(b, NSA_KV_HEADS, nq, n * SEL_BLOCK, HEAD_DIM)
    kpos = (idx[..., None] * SEL_BLOCK + jnp.arange(SEL_BLOCK)).reshape(b, NSA_KV_HEADS, nq, n * SEL_BLOCK)
    s = jnp.einsum('bqghd,bgqkd->bghqk', q, ksel) * scale
    p = _masked_softmax(s, (kpos <= qpos[:, None])[:, :, None])
    o_slc = jnp.einsum('bghqk,bgqkd->bqghd', p, vsel)
    dist = qpos[:, None] - wpos[None, :]
    wmask = (dist >= 0) & (dist <= NSA_WINDOW) & (wpos[None, :] >= 0)
    s = jnp.einsum('bqghd,bkgd->bghqk', q, wk) * scale
    p = _masked_softmax(s, wmask)
    o_win = jnp.einsum('bghqk,bkgd->bqghd', p, wv)
    return o_cmp, o_slc, o_win


def _nsa_prompt(q, kc, vc, ks, vs, kw, vw, cw1, cb1, cw2, cpe):
    b, l = q.shape[:2]
    pos = jnp.arange(l)
    qr = _rope(q, pos).reshape(b, l, NSA_KV_HEADS, NSA_GROUP, HEAD_DIM)
    ck, cv = _nsa_compressed_kv(kc, vc, cw1, cb1, cw2, cpe)
    ksr = _rope(ks, pos)
    vsf = vs.astype(jnp.float32)
    ns = l // SEL_BLOCK
    to_blocks = lambda t: t.reshape(b, ns, SEL_BLOCK, NSA_KV_HEADS, HEAD_DIM).transpose(0, 3, 1, 2, 4)
    sk, sv = to_blocks(ksr), to_blocks(vsf)
    kwr = _rope(kw, pos)
    vwf = vw.astype(jnp.float32)
    pad = ((0, 0), (NSA_WINDOW, 0), (0, 0), (0, 0))
    wk, wv = jnp.pad(kwr, pad), jnp.pad(vwf, pad)
    span = NSA_WINDOW + NSA_QBLOCK

    def block(bi):
        s0 = bi * NSA_QBLOCK
        qpos = s0 + jnp.arange(NSA_QBLOCK)
        wpos = s0 - NSA_WINDOW + jnp.arange(span)
        return _nsa_attend(lax.dynamic_slice_in_dim(qr, s0, NSA_QBLOCK, axis=1), qpos, ck, cv, sk, sv,
                           lax.dynamic_slice_in_dim(wk, s0, span, axis=1),
                           lax.dynamic_slice_in_dim(wv, s0, span, axis=1), wpos)

    o_cmp, o_slc, o_win = lax.map(block, jnp.arange(l // NSA_QBLOCK))
    unblock = lambda o: jnp.moveaxis(o, 0, 1).reshape(b, l, NSA_HEADS, HEAD_DIM)
    keep = min(NSA_WINDOW, l)
    rows_cmp = jnp.stack([kc, vc], axis=2)
    rows_slc = jnp.stack([ksr, vsf], axis=2)
    rows_win = jnp.stack([kwr[:, l - keep:], vwf[:, l - keep:]], axis=2)
    return unblock(o_cmp), unblock(o_slc), unblock(o_win), rows_cmp, rows_slc, rows_win


def _nsa_sample(q, kc, vc, ks, vs, kw, vw, cmp_pool, slc_pool, layer_idx, win_buf, page_table, cw1, cb1, cw2, cpe):
    db, s = q.shape[:2]
    past = page_table.shape[1] * PAGE_SIZE
    l = past + s
    qpos = past + jnp.arange(s)
    qr = _rope(q, qpos).reshape(db, s, NSA_KV_HEADS, NSA_GROUP, HEAD_DIM)
    past_c = _gather_pages(cmp_pool, page_table, layer_idx)
    ck, cv = _nsa_compressed_kv(jnp.concatenate([past_c[:, :, 0], kc], axis=1),
                                jnp.concatenate([past_c[:, :, 1], vc], axis=1), cw1, cb1, cw2, cpe)
    ksr = _rope(ks, qpos)
    vsf = vs.astype(jnp.float32)
    past_s = _gather_pages(slc_pool, page_table, layer_idx)
    ns = -(-l // SEL_BLOCK)
    pad = ((0, 0), (0, ns * SEL_BLOCK - l), (0, 0), (0, 0))
    to_blocks = lambda t: jnp.pad(t.astype(jnp.float32), pad).reshape(
        db, ns, SEL_BLOCK, NSA_KV_HEADS, HEAD_DIM).transpose(0, 3, 1, 2, 4)
    sk = to_blocks(jnp.concatenate([past_s[:, :, 0], ksr], axis=1))
    sv = to_blocks(jnp.concatenate([past_s[:, :, 1], vsf], axis=1))
    kwr = _rope(kw, qpos)
    wb = win_buf.shape[1]
    wk = jnp.concatenate([win_buf[:, :, 0].astype(jnp.float32), kwr], axis=1)
    wv = jnp.concatenate([win_buf[:, :, 1].astype(jnp.float32), vw.astype(jnp.float32)], axis=1)
    wpos = past - wb + jnp.arange(wb + s)
    o_cmp, o_slc, o_win = _nsa_attend(qr, qpos, ck, cv, sk, sv, wk, wv, wpos)
    flat = lambda o: o.reshape(db, s, NSA_HEADS, HEAD_DIM)
    keep = min(NSA_WINDOW, l)
    rows_cmp = jnp.stack([kc, vc], axis=2)
    rows_slc = jnp.stack([ksr, vsf], axis=2)
    rows_win = jnp.stack([wk[:, wb + s - keep:], wv[:, wb + s - keep:]], axis=2)
    return flat(o_cmp), flat(o_slc), flat(o_win), rows_cmp, rows_slc, rows_win


def _gdn_chunked(q, k, v, g, beta, s0):
    b, l, h, dk = q.shape
    dv = v.shape[-1]
    c = GDN_CHUNK
    nch = l // c
    r = lambda a: jnp.moveaxis(a.reshape(b, nch, c, h, *a.shape[3:]), 3, 2)
    q, k, v, g, beta = r(q), r(k), r(v), r(g), r(beta)
    gc = jnp.cumsum(g, axis=-1)
    ii = jnp.arange(c)
    tri = ii[:, None] >= ii[None, :]
    strict = ii[:, None] > ii[None, :]
    diff = gc[..., :, None] - gc[..., None, :]
    gamma = jnp.where(tri, jnp.exp(jnp.where(tri, diff, 0.0)), 0.0)
    kb = k * beta[..., None]
    a_mat = jnp.where(strict, jnp.einsum('bnhik,bnhjk->bnhij', kb, k) * gamma, 0.0)
    eye = jnp.eye(c, dtype=jnp.float32)
    t_inv = lax.linalg.triangular_solve(eye + a_mat, jnp.broadcast_to(eye, a_mat.shape),
                                        left_side=True, lower=True, unit_diagonal=True)
    u = t_inv @ (v * beta[..., None])
    w = t_inv @ (kb * jnp.exp(gc)[..., None])
    qk = jnp.where(tri, jnp.einsum('bnhik,bnhjk->bnhij', q, k) * gamma, 0.0)
    qg = q * jnp.exp(gc)[..., None]
    kd = k * jnp.exp(gc[..., -1:] - gc)[..., None]
    glast = jnp.exp(gc[..., -1])

    def step(state, xs):
        qg_c, kd_c, u_c, w_c, qk_c, gl_c = xs
        v_new = u_c - jnp.einsum('bhck,bhkv->bhcv', w_c, state)
        o = jnp.einsum('bhck,bhkv->bhcv', qg_c, state) + jnp.einsum('bhij,bhjv->bhiv', qk_c, v_new)
        state = state * gl_c[..., None, None] + jnp.einsum('bhck,bhcv->bhkv', kd_c, v_new)
        return state, o

    xs = tuple(jnp.moveaxis(a, 1, 0) for a in (qg, kd, u, w, qk, glast))
    s_fin, o = lax.scan(step, s0, xs)
    o = jnp.moveaxis(jnp.moveaxis(o, 0, 1), 2, 3).reshape(b, l, h, dv)
    return o, s_fin


def _gdn_recurrent(q, k, v, g, beta, s0):
    def step(state, xs):
        q_t, k_t, v_t, g_t, b_t = xs
        state = state * jnp.exp(g_t)[..., None, None]
        v_t = (v_t - jnp.einsum('bhk,bhkv->bhv', k_t, state)) * b_t[..., None]
        state = state + jnp.einsum('bhk,bhv->bhkv', k_t, v_t)
        return state, jnp.einsum('bhk,bhkv->bhv', q_t, state)

    xs = tuple(jnp.moveaxis(a, 1, 0) for a in (q, k, v, g, beta))
    s_fin, o = lax.scan(step, s0, xs)
    return jnp.moveaxis(o, 0, 1), s_fin


def _gdn_mix(qkv, a, bt, z, conv_hist, s0, conv_w, a_log, dt_bias, norm_w, chunked):
    b, s = qkv.shape[:2]
    c, new_hist = _causal_dwconv(conv_hist, qkv, conv_w)
    c = jax.nn.silu(c.astype(jnp.float32))
    q, k, v = [t.reshape(b, s, GDN_HEADS, HEAD_DIM) for t in jnp.split(c, 3, axis=-1)]
    q = _l2_norm(q) * HEAD_DIM ** -0.5
    k = _l2_norm(k)
    beta = jax.nn.sigmoid(bt.astype(jnp.float32))
    g = -jnp.exp(a_log) * jax.nn.softplus(a.astype(jnp.float32) + dt_bias)
    s0 = s0.astype(jnp.float32)
    if chunked:
        o, s_fin = _gdn_chunked(q, k, v, g, beta, s0)
    else:
        o, s_fin = _gdn_recurrent(q, k, v, g, beta, s0)
    o = _rms_norm(o, norm_w) * jax.nn.silu(z.astype(jnp.float32).reshape(b, s, GDN_HEADS, HEAD_DIM))
    return o.reshape(b, s, GDN_W), new_hist, s_fin


def _even_merge(o_cmp, o_slc, o_win, gate, o_gdn, w_out):
    b, s = gate.shape[:2]
    gt = jax.nn.sigmoid(gate.astype(jnp.float32)).reshape(b, s, 3, NSA_HEADS, 1)
    o_nsa = gt[:, :, 0] * o_cmp + gt[:, :, 1] * o_slc + gt[:, :, 2] * o_win
    return jnp.concatenate([o_nsa.reshape(b, s, NSA_Q_W), o_gdn], axis=-1) @ w_out


def _even_prompt(x, w_in, cw1, cb1, cw2, cpe, conv_w, a_log, dt_bias, norm_w, w_out):
    b, l, _ = x.shape
    q, kc, vc, ks, vs, kw, vw, gate, qkv, a, bt, z = _split_cols(x @ w_in, _even_widths())
    heads = lambda t: t.reshape(b, l, -1, HEAD_DIM)
    o_cmp, o_slc, o_win, r_cmp, r_slc, r_win = _nsa_prompt(
        heads(q), heads(kc), heads(vc), heads(ks), heads(vs), heads(kw), heads(vw), cw1, cb1, cw2, cpe)
    hist0 = jnp.zeros((b, GDN_CONV - 1, 3 * GDN_W), qkv.dtype)
    s0 = jnp.zeros((b, GDN_HEADS, HEAD_DIM, HEAD_DIM), jnp.float32)
    o_gdn, conv_hist, s_fin = _gdn_mix(qkv, a, bt, z, hist0, s0, conv_w, a_log, dt_bias, norm_w, True)
    out = _even_merge(o_cmp, o_slc, o_win, gate, o_gdn, w_out)
    return out, r_cmp, r_slc, r_win, conv_hist, s_fin


def _even_sample(x, cmp_pool, slc_pool, layer_idx, win_buf, conv_hist, s0, page_table,
                 w_in, cw1, cb1, cw2, cpe, conv_w, a_log, dt_bias, norm_w, w_out):
    b, s, _ = x.shape
    q, kc, vc, ks, vs, kw, vw, gate, qkv, a, bt, z = _split_cols(x @ w_in, _even_widths())
    heads = lambda t: t.reshape(b, s, -1, HEAD_DIM)
    o_cmp, o_slc, o_win, r_cmp, r_slc, r_win = _nsa_sample(
        heads(q), heads(kc), heads(vc), heads(ks), heads(vs), heads(kw), heads(vw),
        cmp_pool, slc_pool, layer_idx, win_buf, page_table, cw1, cb1, cw2, cpe)
    o_gdn, new_hist, s_fin = _gdn_mix(qkv, a, bt, z, conv_hist, s0, conv_w, a_log, dt_bias, norm_w, False)
    out = _even_merge(o_cmp, o_slc, o_win, gate, o_gdn, w_out)
    return out, r_cmp, r_slc, r_win, new_hist, s_fin


def _dilated_band_stats(q, k, v, d):
    b, l, h, dh = q.shape
    unit = d * DIL_BLOCK
    lp = -(-l // unit) * unit
    nb = lp // unit
    to_sub = lambda a: jnp.pad(a, ((0, 0), (0, lp - l), (0, 0), (0, 0))).reshape(b, nb, DIL_BLOCK, d, h, dh)
    qs, ks, vs = to_sub(q), to_sub(k), to_sub(v)
    prev = lambda a: jnp.concatenate([jnp.zeros_like(a[:, :1]), a[:, :-1]], axis=1)
    kk = jnp.concatenate([prev(ks), ks], axis=2)
    vv = jnp.concatenate([prev(vs), vs], axis=2)
    s = jnp.einsum('bnirhd,bnjrhd->bnrhij', qs, kk) * HEAD_DIM ** -0.5
    i = jnp.arange(DIL_BLOCK)
    j = jnp.arange(2 * DIL_BLOCK) - DIL_BLOCK
    dist = i[:, None] - j[None, :]
    sub_k = jnp.arange(nb)[:, None, None] * DIL_BLOCK + j[None, None, :]
    mask = (dist >= 0)[None] & (dist <= DIL_SPAN)[None] & (sub_k >= 0)
    s = jnp.where(mask[None, :, None, None], s, -jnp.inf)
    m = jnp.max(s, axis=-1)
    e = jnp.exp(s - m[..., None])
    den = jnp.sum(e, axis=-1)
    num = jnp.einsum('bnrhij,bnjrhd->bnrhid', e, vv)
    m = jnp.transpose(m, (0, 1, 4, 2, 3)).reshape(b, lp, h)[:, :l]
    den = jnp.transpose(den, (0, 1, 4, 2, 3)).reshape(b, lp, h)[:, :l]
    num = jnp.transpose(num, (0, 1, 4, 2, 3, 5)).reshape(b, lp, h, dh)[:, :l]
    return m, den, num


def _dilated_gather_stats(q, k_all, v_all, qpos, kpos0, d):
    kp = qpos[:, None] - jnp.arange(DIL_SPAN + 1)[None, :] * d
    idx = kp - kpos0
    valid = idx >= 0
    idxc = jnp.clip(idx, 0)
    kg, vg = k_all[:, idxc], v_all[:, idxc]
    s = jnp.einsum('bshd,bsmhd->bshm', q, kg) * HEAD_DIM ** -0.5
    s = jnp.where(valid[None, :, None, :], s, -jnp.inf)
    m = jnp.max(s, axis=-1)
    e = jnp.exp(s - m[..., None])
    return m, jnp.sum(e, axis=-1), jnp.einsum('bshm,bsmhd->bshd', e, vg)


def _combine_by_denominators(stats):
    m_all = stats[0][0]
    for m, _, _ in stats[1:]:
        m_all = jnp.maximum(m_all, m)
    num, den = None, None
    for m, dn, nm in stats:
        w = jnp.exp(m - m_all)
        num = w[..., None] * nm if num is None else num + w[..., None] * nm
        den = w * dn if den is None else den + w * dn
    return num / den[..., None]


def _dil_prompt(x, w_in, w_out):
    b, l, _ = x.shape
    q, k, v = [t.reshape(b, l, DIL_HEADS, HEAD_DIM) for t in jnp.split(x @ w_in, 3, axis=-1)]
    pos = jnp.arange(l)
    qr, kr, vf = _rope(q, pos), _rope(k, pos), v.astype(jnp.float32)
    o = _combine_by_denominators([_dilated_band_stats(qr, kr, vf, d) for _, d in DIL_GROUPS])
    keep = min(DIL_MAX_WINDOW, l)
    buf = jnp.stack([kr[:, l - keep:], vf[:, l - keep:]], axis=2)
    return o.reshape(b, l, DIL_W) @ w_out, buf


def _dil_sample(x, buf, past, w_in, w_out):
    db, s, _ = x.shape
    q, k, v = [t.reshape(db, s, DIL_HEADS, HEAD_DIM) for t in jnp.split(x @ w_in, 3, axis=-1)]
    qpos = past + jnp.arange(s)
    qr, kr = _rope(q, qpos), _rope(k, qpos)
    wb = buf.shape[1]
    k_all = jnp.concatenate([buf[:, :, 0].astype(jnp.float32), kr], axis=1)
    v_all = jnp.concatenate([buf[:, :, 1].astype(jnp.float32), v.astype(jnp.float32)], axis=1)
    o = _combine_by_denominators(
        [_dilated_gather_stats(qr, k_all, v_all, qpos, past - wb, d) for _, d in DIL_GROUPS])
    keep = min(DIL_MAX_WINDOW, past + s)
    new_buf = jnp.stack([k_all[:, wb + s - keep:], v_all[:, wb + s - keep:]], axis=2)
    return o.reshape(db, s, DIL_W) @ w_out, new_buf


def _conv_ffn(x, hist, w_in, conv_w, conv_b, w_out):
    u = x @ w_in
    c, new_hist = _causal_dwconv(hist, u, conv_w)
    a, g = jnp.split(c + conv_b, 2, axis=-1)
    return (jax.nn.silu(a) * g) @ w_out, new_hist


def setup_inputs(seed: int = 0) -> dict:
    key = jax.random.key(seed)
    keys = iter(jax.random.split(key, 40))

    def nrm(shape, scale):
        return jax.random.normal(next(keys), shape, jnp.float32) * scale

    n_pages = PAST_LEN // PAGE_SIZE
    n_pool = (DEC_BATCH * n_pages * 5) // 4
    win_buf = min(NSA_WINDOW, PAST_LEN)
    dil_buf = min(DIL_MAX_WINDOW, PAST_LEN)
    x_prompt = nrm((BATCH, SEQ, D_MODEL), 1.0)
    x_sample = nrm((DEC_BATCH, DEC_SEQ, D_MODEL), 1.0)
    cache_nsa_cmp_kv = nrm((n_pool, N_A_LAYERS, PAGE_SIZE, 2, NSA_KV_HEADS, HEAD_DIM), 1.0)
    cache_nsa_slc_kv = nrm((n_pool, N_A_LAYERS, PAGE_SIZE, 2, NSA_KV_HEADS, HEAD_DIM), 1.0)
    state_nsa_win_kv = nrm((DEC_BATCH, N_A_LAYERS, win_buf, 2, NSA_KV_HEADS, HEAD_DIM), 1.0)
    state_gdn_conv = nrm((DEC_BATCH, N_A_LAYERS, GDN_CONV - 1, 3 * GDN_W), 1.0)
    state_gdn_S = nrm((DEC_BATCH, N_A_LAYERS, GDN_HEADS, HEAD_DIM, HEAD_DIM), 0.1)
    state_dil_kv = nrm((DEC_BATCH, N_C_LAYERS, dil_buf, 2, DIL_HEADS, HEAD_DIM), 1.0)
    state_ffn_conv = nrm((DEC_BATCH, DEPTH, FFN_CONV - 1, 2 * D_FF), 1.0)
    page_table = jax.random.permutation(next(keys), n_pool)[:DEC_BATCH * n_pages].reshape(
        DEC_BATCH, n_pages).astype(jnp.int32)
    w_in_a = nrm((N_A_LAYERS, D_MODEL, PROJ_A_W), D_MODEL ** -0.5)
    nsa_cmp_w1 = nrm((N_A_LAYERS, 2, CMP_BLOCK * HEAD_DIM, CMP_HIDDEN), (CMP_BLOCK * HEAD_DIM) ** -0.5)
    nsa_cmp_b1 = nrm((N_A_LAYERS, 2, CMP_HIDDEN), 0.02)
    nsa_cmp_w2 = nrm((N_A_LAYERS, 2, CMP_HIDDEN, HEAD_DIM), CMP_HIDDEN ** -0.5)
    nsa_cmp_pe = nrm((N_A_LAYERS, 2, CMP_BLOCK, HEAD_DIM), 0.1)
    gdn_conv_w = nrm((N_A_LAYERS, GDN_CONV, 3 * GDN_W), GDN_CONV ** -0.5)
    gdn_A_log = jnp.log(jax.random.uniform(next(keys), (N_A_LAYERS, GDN_HEADS), jnp.float32, 1.0, 16.0))
    dt = jnp.exp(jax.random.uniform(next(keys), (N_A_LAYERS, GDN_HEADS), jnp.float32,
                                    math.log(1e-3), math.log(1e-1)))
    gdn_dt_bias = dt + jnp.log(-jnp.expm1(-dt))
    gdn_norm_w = 1.0 + nrm((N_A_LAYERS, HEAD_DIM), 0.02)
    w_out_a = nrm((N_A_LAYERS, NSA_Q_W + GDN_W, D_MODEL), (NSA_Q_W + GDN_W) ** -0.5 * DEEPNORM_BETA)
    w_in_c = nrm((N_C_LAYERS, D_MODEL, 3 * DIL_W), D_MODEL ** -0.5)
    w_out_c = nrm((N_C_LAYERS, DIL_W, D_MODEL), DIL_W ** -0.5 * DEEPNORM_BETA)
    ln_mix_g = 1.0 + nrm((DEPTH, D_MODEL), 0.02)
    ln_mix_b = nrm((DEPTH, D_MODEL), 0.02)
    ffn_w_in = nrm((DEPTH, D_MODEL, 2 * D_FF), D_MODEL ** -0.5)
    ffn_conv_w = nrm((DEPTH, FFN_CONV, 2 * D_FF), FFN_CONV ** -0.5)
    ffn_conv_b = nrm((DEPTH, 2 * D_FF), 0.02)
    ffn_w_out = nrm((DEPTH, D_FF, D_MODEL), D_FF ** -0.5 * DEEPNORM_BETA)
    ln_ffn_g = 1.0 + nrm((DEPTH, D_MODEL), 0.02)
    ln_ffn_b = nrm((DEPTH, D_MODEL), 0.02)
    return {'x_prompt': x_prompt, 'x_sample': x_sample,
            'cache_nsa_cmp_kv': cache_nsa_cmp_kv, 'cache_nsa_slc_kv': cache_nsa_slc_kv,
            'state_nsa_win_kv': state_nsa_win_kv, 'state_gdn_conv': state_gdn_conv, 'state_gdn_S': state_gdn_S,
            'state_dil_kv': state_dil_kv, 'state_ffn_conv': state_ffn_conv, 'page_table': page_table,
            'w_in_a': w_in_a, 'nsa_cmp_w1': nsa_cmp_w1, 'nsa_cmp_b1': nsa_cmp_b1, 'nsa_cmp_w2': nsa_cmp_w2,
            'nsa_cmp_pe': nsa_cmp_pe, 'gdn_conv_w': gdn_conv_w, 'gdn_A_log': gdn_A_log, 'gdn_dt_bias': gdn_dt_bias,
            'gdn_norm_w': gdn_norm_w, 'w_out_a': w_out_a, 'w_in_c': w_in_c, 'w_out_c': w_out_c,
            'ln_mix_g': ln_mix_g, 'ln_mix_b': ln_mix_b, 'ffn_w_in': ffn_w_in, 'ffn_conv_w': ffn_conv_w,
            'ffn_conv_b': ffn_conv_b, 'ffn_w_out': ffn_w_out, 'ln_ffn_g': ln_ffn_g, 'ln_ffn_b': ln_ffn_b}


def reference(x_prompt, x_sample, cache_nsa_cmp_kv, cache_nsa_slc_kv, state_nsa_win_kv, state_gdn_conv,
              state_gdn_S, state_dil_kv, state_ffn_conv, page_table, w_in_a, nsa_cmp_w1, nsa_cmp_b1, nsa_cmp_w2,
              nsa_cmp_pe, gdn_conv_w, gdn_A_log, gdn_dt_bias, gdn_norm_w, w_out_a, w_in_c, w_out_c,
              ln_mix_g, ln_mix_b, ffn_w_in, ffn_conv_w, ffn_conv_b, ffn_w_out, ln_ffn_g, ln_ffn_b):
    past = page_table.shape[1] * PAGE_SIZE
    xp, xs = x_prompt, x_sample
    cmp_p, cmp_s, slc_p, slc_s, win_p, win_s = [], [], [], [], [], []
    gconv_p, gconv_s, gstate_p, gstate_s = [], [], [], []
    dil_p, dil_s, ffn_p, ffn_s = [], [], [], []
    for layer in range(DEPTH):
        if layer % 2 == 0:
            la = layer // 2
            wa = (w_in_a[la], nsa_cmp_w1[la], nsa_cmp_b1[la], nsa_cmp_w2[la], nsa_cmp_pe[la],
                  gdn_conv_w[la], gdn_A_log[la], gdn_dt_bias[la], gdn_norm_w[la], w_out_a[la])
            mp, rc, rs, rw, hc, hs_ = _even_prompt(xp, *wa)
            cmp_p.append(rc); slc_p.append(rs); win_p.append(rw); gconv_p.append(hc); gstate_p.append(hs_)
            ms, rc, rs, rw, hc, hs_ = _even_sample(xs, cache_nsa_cmp_kv, cache_nsa_slc_kv, la,
                                                   state_nsa_win_kv[:, la], state_gdn_conv[:, la],
                                                   state_gdn_S[:, la], page_table, *wa)
            cmp_s.append(rc); slc_s.append(rs); win_s.append(rw); gconv_s.append(hc); gstate_s.append(hs_)
        else:
            lc = layer // 2
            mp, bp = _dil_prompt(xp, w_in_c[lc], w_out_c[lc])
            ms, bs = _dil_sample(xs, state_dil_kv[:, lc], past, w_in_c[lc], w_out_c[lc])
            dil_p.append(bp); dil_s.append(bs)
        xp = _layer_norm(DEEPNORM_ALPHA * xp + mp, ln_mix_g[layer], ln_mix_b[layer])
        xs = _layer_norm(DEEPNORM_ALPHA * xs + ms, ln_mix_g[layer], ln_mix_b[layer])
        hist0 = jnp.zeros((xp.shape[0], FFN_CONV - 1, 2 * D_FF), xp.dtype)
        fp, hp = _conv_ffn(xp, hist0, ffn_w_in[layer], ffn_conv_w[layer], ffn_conv_b[layer], ffn_w_out[layer])
        fs, hs = _conv_ffn(xs, state_ffn_conv[:, layer], ffn_w_in[layer], ffn_conv_w[layer], ffn_conv_b[layer],
                           ffn_w_out[layer])
        xp = _layer_norm(DEEPNORM_ALPHA * xp + fp, ln_ffn_g[layer], ln_ffn_b[layer])
        xs = _layer_norm(DEEPNORM_ALPHA * xs + fs, ln_ffn_g[layer], ln_ffn_b[layer])
        ffn_p.append(hp); ffn_s.append(hs)

    def stk(lst):
        return jnp.stack(lst, axis=1)

    return (xp, xs, stk(cmp_p), stk(cmp_s), stk(slc_p), stk(slc_s), stk(win_p), stk(win_s),
            stk(gconv_p), stk(gconv_s), stk(gstate_p), stk(gstate_s), stk(dil_p), stk(dil_s),
            stk(ffn_p), stk(ffn_s))
```

```python
import functools
import math

import jax
import jax.numpy as jnp
from jax import lax
from jax.experimental import pallas as pl
from jax.experimental.pallas import tpu as pltpu

D_MODEL = 1024
DEPTH = 2
PAGE_SIZE = 128
HEAD_DIM = 64
ROPE_THETA = 10000.0
NSA_HEADS = 8
NSA_KV_HEADS = 2
NSA_GROUP = NSA_HEADS // NSA_KV_HEADS
CMP_BLOCK = 32
SEL_BLOCK = 64
CMP_PER_SEL = SEL_BLOCK // CMP_BLOCK
NSA_TOPN = 16
NSA_WINDOW = 512
NSA_QBLOCK = 128
NSA_FORCE = 1.0e4
GDN_HEADS = 8
GDN_CONV = 4
GDN_CHUNK = 64
DIL_HEADS = 16
DIL_GROUPS = ((128, 1), (512, 4), (2048, 16))
DIL_SPAN = 128
DIL_BLOCK = 128
DIL_MAX_WINDOW = 2048
D_FF = 2816
FFN_CONV = 3
DEEPNORM_ALPHA = (2.0 * DEPTH) ** 0.25
LN_EPS = 1e-5
NORM_EPS = 1e-6
NSA_Q_W = NSA_HEADS * HEAD_DIM
NSA_KV_W = NSA_KV_HEADS * HEAD_DIM
GDN_W = GDN_HEADS * HEAD_DIM
DIL_W = DIL_HEADS * HEAD_DIM

LANES = 128
VMEM_LIMIT_BYTES = 56 * 1024 * 1024


def _layer_norm_rows(r, g, b):
    mu = jnp.mean(r, axis=-1, keepdims=True)
    d = r - mu
    var = jnp.mean(d * d, axis=-1, keepdims=True)
    return d * lax.rsqrt(var + LN_EPS) * g + b


def _mm_kernel(x_ref, w_ref, o_ref):
    o_ref[...] = jnp.dot(x_ref[...].astype(jnp.bfloat16), w_ref[...], preferred_element_type=jnp.float32)


def _mm_ln_kernel(x_ref, w_ref, res_ref, g_ref, b_ref, o_ref):
    acc = jnp.dot(x_ref[...].astype(jnp.bfloat16), w_ref[...], preferred_element_type=jnp.float32)
    o_ref[...] = _layer_norm_rows(DEEPNORM_ALPHA * res_ref[...] + acc, g_ref[...], b_ref[...])


def _row_tile(m):
    return 512 if m % 512 == 0 else m


def _matmul(x, w):
    m, k = x.shape
    n = w.shape[1]
    tm = _row_tile(m)
    tn = n
    for cand in (1152, 1024, 768, 512):
        if n % cand == 0:
            tn = cand
            break
    return pl.pallas_call(
        _mm_kernel,
        grid=(m // tm, n // tn),
        in_specs=[pl.BlockSpec((tm, k), lambda i, j: (i, 0)),
                  pl.BlockSpec((k, tn), lambda i, j: (0, j))],
        out_specs=pl.BlockSpec((tm, tn), lambda i, j: (i, j)),
        out_shape=jax.ShapeDtypeStruct((m, n), jnp.float32),
        compiler_params=pltpu.CompilerParams(dimension_semantics=("parallel", "arbitrary"),
                                             vmem_limit_bytes=VMEM_LIMIT_BYTES),
        name="matmul",
    )(x, w.astype(jnp.bfloat16))


def _matmul_ln(x, w, res, g, b):
    m, k = x.shape
    n = w.shape[1]
    tm = _row_tile(m)
    return pl.pallas_call(
        _mm_ln_kernel,
        grid=(m // tm,),
        in_specs=[pl.BlockSpec((tm, k), lambda i: (i, 0)),
                  pl.BlockSpec((k, n), lambda i: (0, 0)),
                  pl.BlockSpec((tm, n), lambda i: (i, 0)),
                  pl.BlockSpec((1, n), lambda i: (0, 0)),
                  pl.BlockSpec((1, n), lambda i: (0, 0))],
        out_specs=pl.BlockSpec((tm, n), lambda i: (i, 0)),
        out_shape=jax.ShapeDtypeStruct((m, n), jnp.float32),
        compiler_params=pltpu.CompilerParams(dimension_semantics=("arbitrary",),
                                             vmem_limit_bytes=VMEM_LIMIT_BYTES),
        name="matmul_ln",
    )(x, w.astype(jnp.bfloat16), res, g.reshape(1, n), b.reshape(1, n))


FFN_CHUNK = D_FF // 2
FFN_NCHUNK = D_FF // FFN_CHUNK


def _ffn_seq_kernel(x_ref, wa_ref, wg_ref, cwa_ref, cwg_ref, cba_ref, cbg_ref, wo_ref, lg_ref, lb_ref,
                    y_ref, ha_ref, hg_ref, acc_ref, carry_ref):
    i, j = pl.program_id(0), pl.program_id(1)
    tm = x_ref.shape[0]
    x = x_ref[...]
    xb = x.astype(jnp.bfloat16)

    @pl.when(i == 0)
    def _():
        carry_ref[j] = jnp.zeros(carry_ref.shape[1:], jnp.float32)

    def conv(u, cw_ref, cb_ref, slot):
        prev = carry_ref[j, slot]
        p2, p1 = prev[6:7], prev[7:8]
        row = lax.broadcasted_iota(jnp.int32, u.shape, 0)
        u1 = jnp.where(row == 0, p1, pltpu.roll(u, 1, 0))
        u2 = jnp.where(row == 0, p2, jnp.where(row == 1, p1, pltpu.roll(u, 2, 0)))
        carry_ref[j, slot] = u[tm - 8:]
        cw = cw_ref[...]
        return cw[0:1] * u2 + cw[1:2] * u1 + cw[2:3] * u + cb_ref[...]

    ua = jnp.dot(xb, wa_ref[...], preferred_element_type=jnp.float32)
    ug = jnp.dot(xb, wg_ref[...], preferred_element_type=jnp.float32)
    ha_ref[...] = ua[tm - 8:]
    hg_ref[...] = ug[tm - 8:]
    a = conv(ua, cwa_ref, cba_ref, 0)
    g = conv(ug, cwg_ref, cbg_ref, 1)
    h = (a * jax.nn.sigmoid(a) * g).astype(jnp.bfloat16)
    part = jnp.dot(h, wo_ref[...], preferred_element_type=jnp.float32)

    @pl.when(j == 0)
    def _():
        acc_ref[...] = part

    @pl.when(j > 0)
    def _():
        acc_ref[...] += part

    @pl.when(j == pl.num_programs(1) - 1)
    def _():
        y_ref[...] = _layer_norm_rows(DEEPNORM_ALPHA * x + acc_ref[...], lg_ref[...], lb_ref[...])


def _ffn_seq(x, w_in, conv_w, conv_b, w_out, ln_g, ln_b):
    l, d = x.shape
    tm = 512
    c, nc = FFN_CHUNK, FFN_NCHUNK
    w_in = w_in.astype(jnp.bfloat16)
    cw8 = jnp.zeros((8, 2 * D_FF), jnp.float32).at[:FFN_CONV].set(conv_w)
    cb = conv_b.reshape(1, 2 * D_FF)
    y, ha, hg = pl.pallas_call(
        _ffn_seq_kernel,
        grid=(l // tm, nc),
        in_specs=[pl.BlockSpec((tm, d), lambda i, j: (i, 0)),
                  pl.BlockSpec((d, c), lambda i, j: (0, j)),
                  pl.BlockSpec((d, c), lambda i, j: (0, j + nc)),
                  pl.BlockSpec((8, c), lambda i, j: (0, j)),
                  pl.BlockSpec((8, c), lambda i, j: (0, j + nc)),
                  pl.BlockSpec((1, c), lambda i, j: (0, j)),
                  pl.BlockSpec((1, c), lambda i, j: (0, j + nc)),
                  pl.BlockSpec((c, d), lambda i, j: (j, 0)),
                  pl.BlockSpec((1, d), lambda i, j: (0, 0)),
                  pl.BlockSpec((1, d), lambda i, j: (0, 0))],
        out_specs=[pl.BlockSpec((tm, d), lambda i, j: (i, 0)),
                   pl.BlockSpec((8, c), lambda i, j: (i, j)),
                   pl.BlockSpec((8, c), lambda i, j: (i, j))],
        out_shape=[jax.ShapeDtypeStruct((l, d), jnp.float32),
                   jax.ShapeDtypeStruct((l // tm * 8, D_FF), jnp.float32),
                   jax.ShapeDtypeStruct((l // tm * 8, D_FF), jnp.float32)],
        scratch_shapes=[pltpu.VMEM((tm, d), jnp.float32),
                        pltpu.VMEM((nc, 2, 8, c), jnp.float32)],
        compiler_params=pltpu.CompilerParams(dimension_semantics=("arbitrary", "arbitrary"),
                                             vmem_limit_bytes=VMEM_LIMIT_BYTES),
        name="ffn_seq",
    )(x, w_in, w_in, cw8, cw8, cb, cb, w_out.astype(jnp.bfloat16), ln_g.reshape(1, d), ln_b.reshape(1, d))
    hist = jnp.concatenate([ha[-(FFN_CONV - 1):], hg[-(FFN_CONV - 1):]], axis=-1)
    return y, hist


def _ffn_step_kernel(x_ref, h_ref, wa_ref, wg_ref, cwa_ref, cwg_ref, cba_ref, cbg_ref, wo_ref, lg_ref, lb_ref,
                     y_ref, ua_ref, ug_ref, acc_ref):
    j = pl.program_id(0)
    x = x_ref[...]
    xb = x.astype(jnp.bfloat16)
    ua = jnp.dot(xb, wa_ref[...], preferred_element_type=jnp.float32)
    ug = jnp.dot(xb, wg_ref[...], preferred_element_type=jnp.float32)
    ua_ref[...] = ua
    ug_ref[...] = ug
    cwa, cwg = cwa_ref[...], cwg_ref[...]
    a = cwa[0:1] * h_ref[0, 0] + cwa[1:2] * h_ref[1, 0] + cwa[2:3] * ua + cba_ref[...]
    g = cwg[0:1] * h_ref[0, 1] + cwg[1:2] * h_ref[1, 1] + cwg[2:3] * ug + cbg_ref[...]
    h = (a * jax.nn.sigmoid(a) * g).astype(jnp.bfloat16)
    part = jnp.dot(h, wo_ref[...], preferred_element_type=jnp.float32)

    @pl.when(j == 0)
    def _():
        acc_ref[...] = part

    @pl.when(j > 0)
    def _():
        acc_ref[...] += part

    @pl.when(j == pl.num_programs(0) - 1)
    def _():
        y_ref[...] = _layer_norm_rows(DEEPNORM_ALPHA * x + acc_ref[...], lg_ref[...], lb_ref[...])


def _ffn_step(x, hist, w_in, conv_w, conv_b, w_out, ln_g, ln_b):
    b, d = x.shape
    c, nc = FFN_CHUNK, FFN_NCHUNK
    w_in = w_in.astype(jnp.bfloat16)
    cw8 = jnp.zeros((8, 2 * D_FF), jnp.float32).at[:FFN_CONV].set(conv_w)
    cb = conv_b.reshape(1, 2 * D_FF)
    h4 = jnp.transpose(hist, (1, 0, 2)).reshape(2, b, 2, D_FF).transpose(0, 2, 1, 3)
    y, ua, ug = pl.pallas_call(
        _ffn_step_kernel,
        grid=(nc,),
        in_specs=[pl.BlockSpec((b, d), lambda j: (0, 0)),
                  pl.BlockSpec((2, 2, b, c), lambda j: (0, 0, 0, j)),
                  pl.BlockSpec((d, c), lambda j: (0, j)),
                  pl.BlockSpec((d, c), lambda j: (0, j + nc)),
                  pl.BlockSpec((8, c), lambda j: (0, j)),
                  pl.BlockSpec((8, c), lambda j: (0, j + nc)),
                  pl.BlockSpec((1, c), lambda j: (0, j)),
                  pl.BlockSpec((1, c), lambda j: (0, j + nc)),
                  pl.BlockSpec((c, d), lambda j: (j, 0)),
                  pl.BlockSpec((1, d), lambda j: (0, 0)),
                  pl.BlockSpec((1, d), lambda j: (0, 0))],
        out_specs=[pl.BlockSpec((b, d), lambda j: (0, 0)),
                   pl.BlockSpec((b, c), lambda j: (0, j)),
                   pl.BlockSpec((b, c), lambda j: (0, j))],
        out_shape=[jax.ShapeDtypeStruct((b, d), jnp.float32),
                   jax.ShapeDtypeStruct((b, D_FF), jnp.float32),
                   jax.ShapeDtypeStruct((b, D_FF), jnp.float32)],
        scratch_shapes=[pltpu.VMEM((b, d), jnp.float32)],
        compiler_params=pltpu.CompilerParams(dimension_semantics=("arbitrary",),
                                             vmem_limit_bytes=VMEM_LIMIT_BYTES),
        name="ffn_step",
    )(x, h4, w_in, w_in, cw8, cw8, cb, cb, w_out.astype(jnp.bfloat16), ln_g.reshape(1, d), ln_b.reshape(1, d))
    u = jnp.concatenate([ua, ug], axis=-1)
    return y, jnp.concatenate([hist[:, 1:], u[:, None]], axis=1)


def _split_cols(h, widths):
    parts, start = [], 0
    for w in widths:
        parts.append(h[..., start:start + w])
        start += w
    return parts


def _even_widths():
    return (NSA_Q_W,) + (NSA_KV_W,) * 6 + (3 * NSA_HEADS, 3 * GDN_W, GDN_HEADS, GDN_HEADS, GDN_W)


def _rms_norm(x, w):
    return x * lax.rsqrt(jnp.mean(jnp.square(x), axis=-1, keepdims=True) + NORM_EPS) * w


def _l2_norm(x):
    return x * lax.rsqrt(jnp.sum(jnp.square(x), axis=-1, keepdims=True) + NORM_EPS)


def _rope(x, pos):
    half = HEAD_DIM // 2
    inv_freq = ROPE_THETA ** (-2.0 * jnp.arange(half, dtype=jnp.float32) / HEAD_DIM)
    ang = pos.astype(jnp.float32)[:, None] * inv_freq[None, :]
    cos, sin = jnp.cos(ang)[:, None, :], jnp.sin(ang)[:, None, :]
    xf = x.astype(jnp.float32)
    x1, x2 = xf[..., :half], xf[..., half:]
    return jnp.concatenate([x1 * cos - x2 * sin, x2 * cos + x1 * sin], axis=-1)


def _causal_dwconv(hist, u, w):
    width, s = w.shape[0], u.shape[1]
    ext = jnp.concatenate([hist.astype(u.dtype), u], axis=1)
    out = w[0] * ext[:, :s]
    for j in range(1, width):
        out = out + w[j] * ext[:, j:j + s]
    return out, ext[:, s:]


def _masked_softmax(s, mask):
    s = jnp.where(mask, s, -jnp.inf)
    m = jnp.max(s, axis=-1, keepdims=True)
    m = jnp.where(jnp.isfinite(m), m, 0.0)
    e = jnp.where(mask, jnp.exp(s - m), 0.0)
    den = jnp.sum(e, axis=-1, keepdims=True)
    return e / jnp.where(den > 0.0, den, 1.0)


def _gather_pages(pool, page_table, layer_idx):
    rows = pool[page_table, layer_idx]
    return rows.reshape(rows.shape[0], -1, *rows.shape[3:])


def _nsa_compress(rows, w1, b1, w2, pe):
    b, l, g, dh = rows.shape
    nc = l // CMP_BLOCK
    blk = rows[:, :nc * CMP_BLOCK].astype(jnp.float32).reshape(b, nc, CMP_BLOCK, g, dh) + pe[:, None, :]
    flat = blk.transpose(0, 1, 3, 2, 4).reshape(b, nc, g, CMP_BLOCK * dh)
    return jax.nn.gelu(flat @ w1 + b1) @ w2


def _nsa_compressed_kv(k_rows, v_rows, cw1, cb1, cw2, cpe):
    ck = _nsa_compress(k_rows, cw1[0], cb1[0], cw2[0], cpe[0])
    cv = _nsa_compress(v_rows, cw1[1], cb1[1], cw2[1], cpe[1])
    nc = ck.shape[1]
    ck = _rope(ck, (jnp.arange(nc) + 1) * CMP_BLOCK - 1)
    return ck, cv


def _nsa_attend(q, qpos, ck, cv, sk, sv, wk, wv, wpos):
    b, nq = q.shape[:2]
    scale = HEAD_DIM ** -0.5
    nc, ns = ck.shape[1], sk.shape[2]
    cend = (jnp.arange(nc) + 1) * CMP_BLOCK - 1
    s = jnp.einsum('bqghd,bcgd->bghqc', q, ck) * scale
    p_cmp = _masked_softmax(s, cend[None, :] <= qpos[:, None])
    o_cmp = jnp.einsum('bghqc,bcgd->bqghd', p_cmp, cv)
    imp = jnp.sum(p_cmp, axis=2)
    imp = jnp.pad(imp, ((0, 0), (0, 0), (0, 0), (0, ns * CMP_PER_SEL - nc)))
    imp = imp.reshape(b, NSA_KV_HEADS, nq, ns, CMP_PER_SEL).sum(-1)
    blk = jnp.arange(ns)[None, :]
    cur = (qpos // SEL_BLOCK)[:, None]
    forced = (blk == 0) | (blk == cur) | (blk == cur - 1)
    imp = jnp.where(blk <= cur, jnp.where(forced, NSA_FORCE, imp), -1.0)
    _, idx = lax.top_k(imp, min(NSA_TOPN, ns))
    n = idx.shape[-1]
    pick = jax.vmap(jax.vmap(lambda kb, ix: kb[ix]))
    ksel = pick(sk, idx).reshape(b, NSA_KV_HEADS, nq, n * SEL_BLOCK, HEAD_DIM)
    vsel = pick(sv, idx).reshape(b, NSA_KV_HEADS, nq, n * SEL_BLOCK, HEAD_DIM)
    kpos = (idx[..., None] * SEL_BLOCK + jnp.arange(SEL_BLOCK)).reshape(b, NSA_KV_HEADS, nq, n * SEL_BLOCK)
    s = jnp.einsum('bqghd,bgqkd->bghqk', q, ksel) * scale
    p = _masked_softmax(s, (kpos <= qpos[:, None])[:, :, None])
    o_slc = jnp.einsum('bghqk,bgqkd->bqghd', p, vsel)
    dist = qpos[:, None] - wpos[None, :]
    wmask = (dist >= 0) & (dist <= NSA_WINDOW) & (wpos[None, :] >= 0)
    s = jnp.einsum('bqghd,bkgd->bghqk', q, wk) * scale
    p = _masked_softmax(s, wmask)
    o_win = jnp.einsum('bghqk,bkgd->bqghd', p, wv)
    return o_cmp, o_slc, o_win


def _nsa_prompt(q, kc, vc, ks, vs, kw, vw, cw1, cb1, cw2, cpe):
    b, l = q.shape[:2]
    pos = jnp.arange(l)
    qr = _rope(q, pos).reshape(b, l, NSA_KV_HEADS, NSA_GROUP, HEAD_DIM)
    ck, cv = _nsa_compressed_kv(kc, vc, cw1, cb1, cw2, cpe)
    ksr = _rope(ks, pos)
    vsf = vs.astype(jnp.float32)
    ns = l // SEL_BLOCK
    to_blocks = lambda t: t.reshape(b, ns, SEL_BLOCK, NSA_KV_HEADS, HEAD_DIM).transpose(0, 3, 1, 2, 4)
    sk, sv = to_blocks(ksr), to_blocks(vsf)
    kwr = _rope(kw, pos)
    vwf = vw.astype(jnp.float32)
    pad = ((0, 0), (NSA_WINDOW, 0), (0, 0), (0, 0))
    wk, wv = jnp.pad(kwr, pad), jnp.pad(vwf, pad)
    span = NSA_WINDOW + NSA_QBLOCK

    def block(bi):
        s0 = bi * NSA_QBLOCK
        qpos = s0 + jnp.arange(NSA_QBLOCK)
        wpos = s0 - NSA_WINDOW + jnp.arange(span)
        return _nsa_attend(lax.dynamic_slice_in_dim(qr, s0, NSA_QBLOCK, axis=1), qpos, ck, cv, sk, sv,
                           lax.dynamic_slice_in_dim(wk, s0, span, axis=1),
                           lax.dynamic_slice_in_dim(wv, s0, span, axis=1), wpos)

    o_cmp, o_slc, o_win = lax.map(block, jnp.arange(l // NSA_QBLOCK))
    unblock = lambda o: jnp.moveaxis(o, 0, 1).reshape(b, l, NSA_HEADS, HEAD_DIM)
    keep = min(NSA_WINDOW, l)
    rows_cmp = jnp.stack([kc, vc], axis=2)
    rows_slc = jnp.stack([ksr, vsf], axis=2)
    rows_win = jnp.stack([kwr[:, l - keep:], vwf[:, l - keep:]], axis=2)
    return unblock(o_cmp), unblock(o_slc), unblock(o_win), rows_cmp, rows_slc, rows_win


def _nsa_sample(q, kc, vc, ks, vs, kw, vw, cmp_pool, slc_pool, layer_idx, win_buf, page_table, cw1, cb1, cw2, cpe):
    db, s = q.shape[:2]
    past = page_table.shape[1] * PAGE_SIZE
    l = past + s
    qpos = past + jnp.arange(s)
    qr = _rope(q, qpos).reshape(db, s, NSA_KV_HEADS, NSA_GROUP, HEAD_DIM)
    past_c = _gather_pages(cmp_pool, page_table, layer_idx)
    ck, cv = _nsa_compressed_kv(jnp.concatenate([past_c[:, :, 0], kc], axis=1),
                                jnp.concatenate([past_c[:, :, 1], vc], axis=1), cw1, cb1, cw2, cpe)
    ksr = _rope(ks, qpos)
    vsf = vs.astype(jnp.float32)
    past_s = _gather_pages(slc_pool, page_table, layer_idx)
    ns = -(-l // SEL_BLOCK)
    pad = ((0, 0), (0, ns * SEL_BLOCK - l), (0, 0), (0, 0))
    to_blocks = lambda t: jnp.pad(t.astype(jnp.float32), pad).reshape(
        db, ns, SEL_BLOCK, NSA_KV_HEADS, HEAD_DIM).transpose(0, 3, 1, 2, 4)
    sk = to_blocks(jnp.concatenate([past_s[:, :, 0], ksr], axis=1))
    sv = to_blocks(jnp.concatenate([past_s[:, :, 1], vsf], axis=1))
    kwr = _rope(kw, qpos)
    wb = win_buf.shape[1]
    wk = jnp.concatenate([win_buf[:, :, 0].astype(jnp.float32), kwr], axis=1)
    wv = jnp.concatenate([win_buf[:, :, 1].astype(jnp.float32), vw.astype(jnp.float32)], axis=1)
    wpos = past - wb + jnp.arange(wb + s)
    o_cmp, o_slc, o_win = _nsa_attend(qr, qpos, ck, cv, sk, sv, wk, wv, wpos)
    flat = lambda o: o.reshape(db, s, NSA_HEADS, HEAD_DIM)
    keep = min(NSA_WINDOW, l)
    rows_cmp = jnp.stack([kc, vc], axis=2)
    rows_slc = jnp.stack([ksr, vsf], axis=2)
    rows_win = jnp.stack([wk[:, wb + s - keep:], wv[:, wb + s - keep:]], axis=2)
    return flat(o_cmp), flat(o_slc), flat(o_win), rows_cmp, rows_slc, rows_win


def _gdn_chunked(q, k, v, g, beta, s0):
    b, l, h, dk = q.shape
    dv = v.shape[-1]
    c = GDN_CHUNK
    nch = l // c
    r = lambda a: jnp.moveaxis(a.reshape(b, nch, c, h, *a.shape[3:]), 3, 2)
    q, k, v, g, beta = r(q), r(k), r(v), r(g), r(beta)
    gc = jnp.cumsum(g, axis=-1)
    ii = jnp.arange(c)
    tri = ii[:, None] >= ii[None, :]
    strict = ii[:, None] > ii[None, :]
    diff = gc[..., :, None] - gc[..., None, :]
    gamma = jnp.where(tri, jnp.exp(jnp.where(tri, diff, 0.0)), 0.0)
    kb = k * beta[..., None]
    a_mat = jnp.where(strict, jnp.einsum('bnhik,bnhjk->bnhij', kb, k) * gamma, 0.0)
    eye = jnp.eye(c, dtype=jnp.float32)
    t_inv = lax.linalg.triangular_solve(eye + a_mat, jnp.broadcast_to(eye, a_mat.shape),
                                        left_side=True, lower=True, unit_diagonal=True)
    u = t_inv @ (v * beta[..., None])
    w = t_inv @ (kb * jnp.exp(gc)[..., None])
    qk = jnp.where(tri, jnp.einsum('bnhik,bnhjk->bnhij', q, k) * gamma, 0.0)
    qg = q * jnp.exp(gc)[..., None]
    kd = k * jnp.exp(gc[..., -1:] - gc)[..., None]
    glast = jnp.exp(gc[..., -1])

    def step(state, xs):
        qg_c, kd_c, u_c, w_c, qk_c, gl_c = xs
        v_new = u_c - jnp.einsum('bhck,bhkv->bhcv', w_c, state)
        o = jnp.einsum('bhck,bhkv->bhcv', qg_c, state) + jnp.einsum('bhij,bhjv->bhiv', qk_c, v_new)
        state = state * gl_c[..., None, None] + jnp.einsum('bhck,bhcv->bhkv', kd_c, v_new)
        return state, o

    xs = tuple(jnp.moveaxis(a, 1, 0) for a in (qg, kd, u, w, qk, glast))
    s_fin, o = lax.scan(step, s0, xs)
    o = jnp.moveaxis(jnp.moveaxis(o, 0, 1), 2, 3).reshape(b, l, h, dv)
    return o, s_fin


def _gdn_recurrent(q, k, v, g, beta, s0):
    def step(state, xs):
        q_t, k_t, v_t, g_t, b_t = xs
        state = state * jnp.exp(g_t)[..., None, None]
        v_t = (v_t - jnp.einsum('bhk,bhkv->bhv', k_t, state)) * b_t[..., None]
        state = state + jnp.einsum('bhk,bhv->bhkv', k_t, v_t)
        return state, jnp.einsum('bhk,bhkv->bhv', q_t, state)

    xs = tuple(jnp.moveaxis(a, 1, 0) for a in (q, k, v, g, beta))
    s_fin, o = lax.scan(step, s0, xs)
    return jnp.moveaxis(o, 0, 1), s_fin


def _gdn_mix(qkv, a, bt, z, conv_hist, s0, conv_w, a_log, dt_bias, norm_w, chunked):
    b, s = qkv.shape[:2]
    c, new_hist = _causal_dwconv(conv_hist, qkv, conv_w)
    c = jax.nn.silu(c.astype(jnp.float32))
    q, k, v = [t.reshape(b, s, GDN_HEADS, HEAD_DIM) for t in jnp.split(c, 3, axis=-1)]
    q = _l2_norm(q) * HEAD_DIM ** -0.5
    k = _l2_norm(k)
    beta = jax.nn.sigmoid(bt.astype(jnp.float32))
    g = -jnp.exp(a_log) * jax.nn.softplus(a.astype(jnp.float32) + dt_bias)
    s0 = s0.astype(jnp.float32)
    if chunked:
        o, s_fin = _gdn_chunked(q, k, v, g, beta, s0)
    else:
        o, s_fin = _gdn_recurrent(q, k, v, g, beta, s0)
    o = _rms_norm(o, norm_w) * jax.nn.silu(z.astype(jnp.float32).reshape(b, s, GDN_HEADS, HEAD_DIM))
    return o.reshape(b, s, GDN_W), new_hist, s_fin


def _even_merge(o_cmp, o_slc, o_win, gate, o_gdn):
    b, s = gate.shape[:2]
    gt = jax.nn.sigmoid(gate.astype(jnp.float32)).reshape(b, s, 3, NSA_HEADS, 1)
    o_nsa = gt[:, :, 0] * o_cmp + gt[:, :, 1] * o_slc + gt[:, :, 2] * o_win
    return jnp.concatenate([o_nsa.reshape(b, s, NSA_Q_W), o_gdn], axis=-1)


def _proj(x, w):
    b, s, d = x.shape
    n = w.shape[1]
    npad = -(-n // LANES) * LANES
    wp = jnp.pad(w, ((0, 0), (0, npad - n)))
    return _matmul(x.reshape(b * s, d), wp)[:, :n].reshape(b, s, n)


def _even_prompt(x, w_in, cw1, cb1, cw2, cpe, conv_w, a_log, dt_bias, norm_w):
    b, l, _ = x.shape
    q, kc, vc, ks, vs, kw, vw, gate, qkv, a, bt, z = _split_cols(_proj(x, w_in), _even_widths())
    heads = lambda t: t.reshape(b, l, -1, HEAD_DIM)
    o_cmp, o_slc, o_win, r_cmp, r_slc, r_win = _nsa_prompt(
        heads(q), heads(kc), heads(vc), heads(ks), heads(vs), heads(kw), heads(vw), cw1, cb1, cw2, cpe)
    hist0 = jnp.zeros((b, GDN_CONV - 1, 3 * GDN_W), qkv.dtype)
    s0 = jnp.zeros((b, GDN_HEADS, HEAD_DIM, HEAD_DIM), jnp.float32)
    o_gdn, conv_hist, s_fin = _gdn_mix(qkv, a, bt, z, hist0, s0, conv_w, a_log, dt_bias, norm_w, True)
    return _even_merge(o_cmp, o_slc, o_win, gate, o_gdn), r_cmp, r_slc, r_win, conv_hist, s_fin


def _even_sample(x, cmp_pool, slc_pool, layer_idx, win_buf, conv_hist, s0, page_table,
                 w_in, cw1, cb1, cw2, cpe, conv_w, a_log, dt_bias, norm_w):
    b, s, _ = x.shape
    q, kc, vc, ks, vs, kw, vw, gate, qkv, a, bt, z = _split_cols(_proj(x, w_in), _even_widths())
    heads = lambda t: t.reshape(b, s, -1, HEAD_DIM)
    o_cmp, o_slc, o_win, r_cmp, r_slc, r_win = _nsa_sample(
        heads(q), heads(kc), heads(vc), heads(ks), heads(vs), heads(kw), heads(vw),
        cmp_pool, slc_pool, layer_idx, win_buf, page_table, cw1, cb1, cw2, cpe)
    o_gdn, new_hist, s_fin = _gdn_mix(qkv, a, bt, z, conv_hist, s0, conv_w, a_log, dt_bias, norm_w, False)
    return _even_merge(o_cmp, o_slc, o_win, gate, o_gdn), r_cmp, r_slc, r_win, new_hist, s_fin


def _dilated_band_stats(q, k, v, d):
    b, l, h, dh = q.shape
    unit = d * DIL_BLOCK
    lp = -(-l // unit) * unit
    nb = lp // unit
    to_sub = lambda a: jnp.pad(a, ((0, 0), (0, lp - l), (0, 0), (0, 0))).reshape(b, nb, DIL_BLOCK, d, h, dh)
    qs, ks, vs = to_sub(q), to_sub(k), to_sub(v)
    prev = lambda a: jnp.concatenate([jnp.zeros_like(a[:, :1]), a[:, :-1]], axis=1)
    kk = jnp.concatenate([prev(ks), ks], axis=2)
    vv = jnp.concatenate([prev(vs), vs], axis=2)
    s = jnp.einsum('bnirhd,bnjrhd->bnrhij', qs, kk) * HEAD_DIM ** -0.5
    i = jnp.arange(DIL_BLOCK)
    j = jnp.arange(2 * DIL_BLOCK) - DIL_BLOCK
    dist = i[:, None] - j[None, :]
    sub_k = jnp.arange(nb)[:, None, None] * DIL_BLOCK + j[None, None, :]
    mask = (dist >= 0)[None] & (dist <= DIL_SPAN)[None] & (sub_k >= 0)
    s = jnp.where(mask[None, :, None, None], s, -jnp.inf)
    m = jnp.max(s, axis=-1)
    e = jnp.exp(s - m[..., None])
    den = jnp.sum(e, axis=-1)
    num = jnp.einsum('bnrhij,bnjrhd->bnrhid', e, vv)
    m = jnp.transpose(m, (0, 1, 4, 2, 3)).reshape(b, lp, h)[:, :l]
    den = jnp.transpose(den, (0, 1, 4, 2, 3)).reshape(b, lp, h)[:, :l]
    num = jnp.transpose(num, (0, 1, 4, 2, 3, 5)).reshape(b, lp, h, dh)[:, :l]
    return m, den, num


def _dilated_gather_stats(q, k_all, v_all, qpos, kpos0, d):
    kp = qpos[:, None] - jnp.arange(DIL_SPAN + 1)[None, :] * d
    idx = kp - kpos0
    valid = idx >= 0
    idxc = jnp.clip(idx, 0)
    kg, vg = k_all[:, idxc], v_all[:, idxc]
    s = jnp.einsum('bshd,bsmhd->bshm', q, kg) * HEAD_DIM ** -0.5
    s = jnp.where(valid[None, :, None, :], s, -jnp.inf)
    m = jnp.max(s, axis=-1)
    e = jnp.exp(s - m[..., None])
    return m, jnp.sum(e, axis=-1), jnp.einsum('bshm,bsmhd->bshd', e, vg)


def _combine_by_denominators(stats):
    m_all = stats[0][0]
    for m, _, _ in stats[1:]:
        m_all = jnp.maximum(m_all, m)
    num, den = None, None
    for m, dn, nm in stats:
        w = jnp.exp(m - m_all)
        num = w[..., None] * nm if num is None else num + w[..., None] * nm
        den = w * dn if den is None else den + w * dn
    return num / den[..., None]


def _dil_prompt(x, w_in):
    b, l, _ = x.shape
    q, k, v = [t.reshape(b, l, DIL_HEADS, HEAD_DIM) for t in jnp.split(_proj(x, w_in), 3, axis=-1)]
    pos = jnp.arange(l)
    qr, kr, vf = _rope(q, pos), _rope(k, pos), v.astype(jnp.float32)
    o = _combine_by_denominators([_dilated_band_stats(qr, kr, vf, d) for _, d in DIL_GROUPS])
    keep = min(DIL_MAX_WINDOW, l)
    buf = jnp.stack([kr[:, l - keep:], vf[:, l - keep:]], axis=2)
    return o.reshape(b, l, DIL_W), buf


def _dil_sample(x, buf, past, w_in):
    db, s, _ = x.shape
    q, k, v = [t.reshape(db, s, DIL_HEADS, HEAD_DIM) for t in jnp.split(_proj(x, w_in), 3, axis=-1)]
    qpos = past + jnp.arange(s)
    qr, kr = _rope(q, qpos), _rope(k, qpos)
    wb = buf.shape[1]
    k_all = jnp.concatenate([buf[:, :, 0].astype(jnp.float32), kr], axis=1)
    v_all = jnp.concatenate([buf[:, :, 1].astype(jnp.float32), v.astype(jnp.float32)], axis=1)
    o = _combine_by_denominators(
        [_dilated_gather_stats(qr, k_all, v_all, qpos, past - wb, d) for _, d in DIL_GROUPS])
    keep = min(DIL_MAX_WINDOW, past + s)
    new_buf = jnp.stack([k_all[:, wb + s - keep:], v_all[:, wb + s - keep:]], axis=2)
    return o.reshape(db, s, DIL_W), new_buf


def kernel(x_prompt, x_sample, cache_nsa_cmp_kv, cache_nsa_slc_kv, state_nsa_win_kv, state_gdn_conv,
           state_gdn_S, state_dil_kv, state_ffn_conv, page_table, w_in_a, nsa_cmp_w1, nsa_cmp_b1, nsa_cmp_w2,
           nsa_cmp_pe, gdn_conv_w, gdn_A_log, gdn_dt_bias, gdn_norm_w, w_out_a, w_in_c, w_out_c,
           ln_mix_g, ln_mix_b, ffn_w_in, ffn_conv_w, ffn_conv_b, ffn_w_out, ln_ffn_g, ln_ffn_b):
    past = page_table.shape[1] * PAGE_SIZE
    bp, lp, d = x_prompt.shape
    bs, ls, _ = x_sample.shape
    assert bp == 1 and ls == 1
    xp, xs = x_prompt, x_sample
    cmp_p, cmp_s, slc_p, slc_s, win_p, win_s = [], [], [], [], [], []
    gconv_p, gconv_s, gstate_p, gstate_s = [], [], [], []
    dil_p, dil_s, ffn_p, ffn_s = [], [], [], []
    for layer in range(DEPTH):
        if layer % 2 == 0:
            la = layer // 2
            wa = (w_in_a[la], nsa_cmp_w1[la], nsa_cmp_b1[la], nsa_cmp_w2[la], nsa_cmp_pe[la],
                  gdn_conv_w[la], gdn_A_log[la], gdn_dt_bias[la], gdn_norm_w[la])
            mp, rc, rs, rw, hc, hs_ = _even_prompt(xp, *wa)
            cmp_p.append(rc); slc_p.append(rs); win_p.append(rw); gconv_p.append(hc); gstate_p.append(hs_)
            ms, rc, rs, rw, hc, hs_ = _even_sample(xs, cache_nsa_cmp_kv, cache_nsa_slc_kv, la,
                                                   state_nsa_win_kv[:, la], state_gdn_conv[:, la],
                                                   state_gdn_S[:, la], page_table, *wa)
            cmp_s.append(rc); slc_s.append(rs); win_s.append(rw); gconv_s.append(hc); gstate_s.append(hs_)
            w_out = w_out_a[la]
        else:
            lc = layer // 2
            mp, bpf = _dil_prompt(xp, w_in_c[lc])
            ms, bsf = _dil_sample(xs, state_dil_kv[:, lc], past, w_in_c[lc])
            dil_p.append(bpf); dil_s.append(bsf)
            w_out = w_out_c[lc]
        xp2 = _matmul_ln(mp.reshape(lp, -1), w_out, xp.reshape(lp, d), ln_mix_g[layer], ln_mix_b[layer])
        xs2 = _matmul_ln(ms.reshape(bs, -1), w_out, xs.reshape(bs, d), ln_mix_g[layer], ln_mix_b[layer])
        fargs = (ffn_w_in[layer], ffn_conv_w[layer], ffn_conv_b[layer], ffn_w_out[layer],
                 ln_ffn_g[layer], ln_ffn_b[layer])
        xp3, hp = _ffn_seq(xp2, *fargs)
        xs3, hs = _ffn_step(xs2, state_ffn_conv[:, layer], *fargs)
        xp, xs = xp3.reshape(1, lp, d), xs3.reshape(bs, 1, d)
        ffn_p.append(hp[None]); ffn_s.append(hs)

    def stk(lst):
        return jnp.stack(lst, axis=1)

    return (xp, xs, stk(cmp_p), stk(cmp_s), stk(slc_p), stk(slc_s), stk(win_p), stk(win_s),
            stk(gconv_p), stk(gconv_s), stk(gstate_p), stk(gstate_s), stk(dil_p), stk(dil_s),
            stk(ffn_p), stk(ffn_s))
```

```python
import functools
import math

import jax
import jax.numpy as jnp
from jax import lax
from jax.experimental import pallas as pl
from jax.experimental.pallas import tpu as pltpu

D_MODEL = 1024
DEPTH = 2
PAGE_SIZE = 128
HEAD_DIM = 64
ROPE_THETA = 10000.0
NSA_HEADS = 8
NSA_KV_HEADS = 2
NSA_GROUP = NSA_HEADS // NSA_KV_HEADS
CMP_BLOCK = 32
SEL_BLOCK = 64
CMP_PER_SEL = SEL_BLOCK // CMP_BLOCK
NSA_TOPN = 16
NSA_WINDOW = 512
NSA_QBLOCK = 128
NSA_FORCE = 1.0e4
GDN_HEADS = 8
GDN_CONV = 4
GDN_CHUNK = 64
DIL_HEADS = 16
DIL_GROUPS = ((128, 1), (512, 4), (2048, 16))
DIL_SPAN = 128
DIL_BLOCK = 128
DIL_MAX_WINDOW = 2048
D_FF = 2816
FFN_CONV = 3
DEEPNORM_ALPHA = (2.0 * DEPTH) ** 0.25
LN_EPS = 1e-5
NORM_EPS = 1e-6
NSA_Q_W = NSA_HEADS * HEAD_DIM
NSA_KV_W = NSA_KV_HEADS * HEAD_DIM
GDN_W = GDN_HEADS * HEAD_DIM
DIL_W = DIL_HEADS * HEAD_DIM

LANES = 128
VMEM_LIMIT_BYTES = 56 * 1024 * 1024


def _layer_norm_rows(r, g, b):
    mu = jnp.mean(r, axis=-1, keepdims=True)
    d = r - mu
    var = jnp.mean(d * d, axis=-1, keepdims=True)
    return d * lax.rsqrt(var + LN_EPS) * g + b


def _mm_kernel(x_ref, w_ref, o_ref):
    o_ref[...] = jnp.dot(x_ref[...].astype(jnp.bfloat16), w_ref[...], preferred_element_type=jnp.float32)


def _mm_ln_kernel(x_ref, w_ref, res_ref, g_ref, b_ref, o_ref):
    acc = jnp.dot(x_ref[...].astype(jnp.bfloat16), w_ref[...], preferred_element_type=jnp.float32)
    o_ref[...] = _layer_norm_rows(DEEPNORM_ALPHA * res_ref[...] + acc, g_ref[...], b_ref[...])


def _row_tile(m):
    return 512 if m % 512 == 0 else m


def _matmul(x, w):
    m, k = x.shape
    n = w.shape[1]
    tm = _row_tile(m)
    tn = n
    for cand in (1152, 1024, 768, 512):
        if n % cand == 0:
            tn = cand
            break
    return pl.pallas_call(
        _mm_kernel,
        grid=(m // tm, n // tn),
        in_specs=[pl.BlockSpec((tm, k), lambda i, j: (i, 0)),
                  pl.BlockSpec((k, tn), lambda i, j: (0, j))],
        out_specs=pl.BlockSpec((tm, tn), lambda i, j: (i, j)),
        out_shape=jax.ShapeDtypeStruct((m, n), jnp.float32),
        compiler_params=pltpu.CompilerParams(dimension_semantics=("parallel", "arbitrary"),
                                             vmem_limit_bytes=VMEM_LIMIT_BYTES),
        name="matmul",
    )(x, w.astype(jnp.bfloat16))


def _matmul_ln(x, w, res, g, b):
    m, k = x.shape
    n = w.shape[1]
    tm = _row_tile(m)
    return pl.pallas_call(
        _mm_ln_kernel,
        grid=(m // tm,),
        in_specs=[pl.BlockSpec((tm, k), lambda i: (i, 0)),
                  pl.BlockSpec((k, n), lambda i: (0, 0)),
                  pl.BlockSpec((tm, n), lambda i: (i, 0)),
                  pl.BlockSpec((1, n), lambda i: (0, 0)),
                  pl.BlockSpec((1, n), lambda i: (0, 0))],
        out_specs=pl.BlockSpec((tm, n), lambda i: (i, 0)),
        out_shape=jax.ShapeDtypeStruct((m, n), jnp.float32),
        compiler_params=pltpu.CompilerParams(dimension_semantics=("arbitrary",),
                                             vmem_limit_bytes=VMEM_LIMIT_BYTES),
        name="matmul_ln",
    )(x, w.astype(jnp.bfloat16), res, g.reshape(1, n), b.reshape(1, n))


FFN_CHUNK = D_FF // 2
FFN_NCHUNK = D_FF // FFN_CHUNK


def _ffn_seq_kernel(x_ref, wa_ref, wg_ref, cwa_ref, cwg_ref, cba_ref, cbg_ref, wo_ref, lg_ref, lb_ref,
                    y_ref, ha_ref, hg_ref, acc_ref, carry_ref):
    i, j = pl.program_id(0), pl.program_id(1)
    tm = x_ref.shape[0]
    x = x_ref[...]
    xb = x.astype(jnp.bfloat16)

    @pl.when(i == 0)
    def _():
        carry_ref[j] = jnp.zeros(carry_ref.shape[1:], jnp.float32)

    def conv(u, cw_ref, cb_ref, slot):
        prev = carry_ref[j, slot]
        p2, p1 = prev[6:7], prev[7:8]
        row = lax.broadcasted_iota(jnp.int32, u.shape, 0)
        u1 = jnp.where(row == 0, p1, pltpu.roll(u, 1, 0))
        u2 = jnp.where(row == 0, p2, jnp.where(row == 1, p1, pltpu.roll(u, 2, 0)))
        carry_ref[j, slot] = u[tm - 8:]
        cw = cw_ref[...]
        return cw[0:1] * u2 + cw[1:2] * u1 + cw[2:3] * u + cb_ref[...]

    ua = jnp.dot(xb, wa_ref[...], preferred_element_type=jnp.float32)
    ug = jnp.dot(xb, wg_ref[...], preferred_element_type=jnp.float32)
    ha_ref[...] = ua[tm - 8:]
    hg_ref[...] = ug[tm - 8:]
    a = conv(ua, cwa_ref, cba_ref, 0)
    g = conv(ug, cwg_ref, cbg_ref, 1)
    h = (a * jax.nn.sigmoid(a) * g).astype(jnp.bfloat16)
    part = jnp.dot(h, wo_ref[...], preferred_element_type=jnp.float32)

    @pl.when(j == 0)
    def _():
        acc_ref[...] = part

    @pl.when(j > 0)
    def _():
        acc_ref[...] += part

    @pl.when(j == pl.num_programs(1) - 1)
    def _():
        y_ref[...] = _layer_norm_rows(DEEPNORM_ALPHA * x + acc_ref[...], lg_ref[...], lb_ref[...])


def _ffn_seq(x, w_in, conv_w, conv_b, w_out, ln_g, ln_b):
    l, d = x.shape
    tm = 512
    c, nc = FFN_CHUNK, FFN_NCHUNK
    w_in = w_in.astype(jnp.bfloat16)
    cw8 = jnp.zeros((8, 2 * D_FF), jnp.float32).at[:FFN_CONV].set(conv_w)
    cb = conv_b.reshape(1, 2 * D_FF)
    y, ha, hg = pl.pallas_call(
        _ffn_seq_kernel,
        grid=(l // tm, nc),
        in_specs=[pl.BlockSpec((tm, d), lambda i, j: (i, 0)),
                  pl.BlockSpec((d, c), lambda i, j: (0, j)),
                  pl.BlockSpec((d, c), lambda i, j: (0, j + nc)),
                  pl.BlockSpec((8, c), lambda i, j: (0, j)),
                  pl.BlockSpec((8, c), lambda i, j: (0, j + nc)),
                  pl.BlockSpec((1, c), lambda i, j: (0, j)),
                  pl.BlockSpec((1, c), lambda i, j: (0, j + nc)),
                  pl.BlockSpec((c, d), lambda i, j: (j, 0)),
                  pl.BlockSpec((1, d), lambda i, j: (0, 0)),
                  pl.BlockSpec((1, d), lambda i, j: (0, 0))],
        out_specs=[pl.BlockSpec((tm, d), lambda i, j: (i, 0)),
                   pl.BlockSpec((8, c), lambda i, j: (i, j)),
                   pl.BlockSpec((8, c), lambda i, j: (i, j))],
        out_shape=[jax.ShapeDtypeStruct((l, d), jnp.float32),
                   jax.ShapeDtypeStruct((l // tm * 8, D_FF), jnp.float32),
                   jax.ShapeDtypeStruct((l // tm * 8, D_FF), jnp.float32)],
        scratch_shapes=[pltpu.VMEM((tm, d), jnp.float32),
                        pltpu.VMEM((nc, 2, 8, c), jnp.float32)],
        compiler_params=pltpu.CompilerParams(dimension_semantics=("arbitrary", "arbitrary"),
                                             vmem_limit_bytes=VMEM_LIMIT_BYTES),
        name="ffn_seq",
    )(x, w_in, w_in, cw8, cw8, cb, cb, w_out.astype(jnp.bfloat16), ln_g.reshape(1, d), ln_b.reshape(1, d))
    hist = jnp.concatenate([ha[-(FFN_CONV - 1):], hg[-(FFN_CONV - 1):]], axis=-1)
    return y, hist


def _ffn_step_kernel(x_ref, h_ref, wa_ref, wg_ref, cwa_ref, cwg_ref, cba_ref, cbg_ref, wo_ref, lg_ref, lb_ref,
                     y_ref, ua_ref, ug_ref, acc_ref):
    j = pl.program_id(0)
    x = x_ref[...]
    xb = x.astype(jnp.bfloat16)
    ua = jnp.dot(xb, wa_ref[...], preferred_element_type=jnp.float32)
    ug = jnp.dot(xb, wg_ref[...], preferred_element_type=jnp.float32)
    ua_ref[...] = ua
    ug_ref[...] = ug
    cwa, cwg = cwa_ref[...], cwg_ref[...]
    a = cwa[0:1] * h_ref[0, 0] + cwa[1:2] * h_ref[1, 0] + cwa[2:3] * ua + cba_ref[...]
    g = cwg[0:1] * h_ref[0, 1] + cwg[1:2] * h_ref[1, 1] + cwg[2:3] * ug + cbg_ref[...]
    h = (a * jax.nn.sigmoid(a) * g).astype(jnp.bfloat16)
    part = jnp.dot(h, wo_ref[...], preferred_element_type=jnp.float32)

    @pl.when(j == 0)
    def _():
        acc_ref[...] = part

    @pl.when(j > 0)
    def _():
        acc_ref[...] += part

    @pl.when(j == pl.num_programs(0) - 1)
    def _():
        y_ref[...] = _layer_norm_rows(DEEPNORM_ALPHA * x + acc_ref[...], lg_ref[...], lb_ref[...])


def _ffn_step(x, hist, w_in, conv_w, conv_b, w_out, ln_g, ln_b):
    b, d = x.shape
    c, nc = FFN_CHUNK, FFN_NCHUNK
    w_in = w_in.astype(jnp.bfloat16)
    cw8 = jnp.zeros((8, 2 * D_FF), jnp.float32).at[:FFN_CONV].set(conv_w)
    cb = conv_b.reshape(1, 2 * D_FF)
    h4 = jnp.transpose(hist, (1, 0, 2)).reshape(2, b, 2, D_FF).transpose(0, 2, 1, 3)
    y, ua, ug = pl.pallas_call(
        _ffn_step_kernel,
        grid=(nc,),
        in_specs=[pl.BlockSpec((b, d), lambda j: (0, 0)),
                  pl.BlockSpec((2, 2, b, c), lambda j: (0, 0, 0, j)),
                  pl.BlockSpec((d, c), lambda j: (0, j)),
                  pl.BlockSpec((d, c), lambda j: (0, j + nc)),
                  pl.BlockSpec((8, c), lambda j: (0, j)),
                  pl.BlockSpec((8, c), lambda j: (0, j + nc)),
                  pl.BlockSpec((1, c), lambda j: (0, j)),
                  pl.BlockSpec((1, c), lambda j: (0, j + nc)),
                  pl.BlockSpec((c, d), lambda j: (j, 0)),
                  pl.BlockSpec((1, d), lambda j: (0, 0)),
                  pl.BlockSpec((1, d), lambda j: (0, 0))],
        out_specs=[pl.BlockSpec((b, d), lambda j: (0, 0)),
                   pl.BlockSpec((b, c), lambda j: (0, j)),
                   pl.BlockSpec((b, c), lambda j: (0, j))],
        out_shape=[jax.ShapeDtypeStruct((b, d), jnp.float32),
                   jax.ShapeDtypeStruct((b, D_FF), jnp.float32),
                   jax.ShapeDtypeStruct((b, D_FF), jnp.float32)],
        scratch_shapes=[pltpu.VMEM((b, d), jnp.float32)],
        compiler_params=pltpu.CompilerParams(dimension_semantics=("arbitrary",),
                                             vmem_limit_bytes=VMEM_LIMIT_BYTES),
        name="ffn_step",
    )(x, h4, w_in, w_in, cw8, cw8, cb, cb, w_out.astype(jnp.bfloat16), ln_g.reshape(1, d), ln_b.reshape(1, d))
    u = jnp.concatenate([ua, ug], axis=-1)
    return y, jnp.concatenate([hist[:, 1:], u[:, None]], axis=1)


NEG = -1e30
NSA_KT = 512
NSA_COLS = NSA_HEADS * NSA_QBLOCK
NSA_WSPAN = NSA_WINDOW + NSA_QBLOCK


def _lane_tile(x, n):
    return jnp.concatenate([x] * n, axis=1)


def _nsa_prompt_kernel(q_ref, sm_ref, ck_ref, cvt_ref, ks_ref, vst_ref, kw_ref, vwt_ref, tri_ref, o_ref,
                       selb_ref, m_ref, l_ref, acc_ref, *, ns):
    f32, bf16 = jnp.float32, jnp.bfloat16
    qb = NSA_QBLOCK
    i = pl.program_id(0)
    s0 = i * qb
    half = NSA_COLS // 2

    qt = (q_ref[...] * (HEAD_DIM ** -0.5)).T
    zero = jnp.zeros((HEAD_DIM, qb), f32)
    top = jnp.concatenate([qt[h * HEAD_DIM:(h + 1) * HEAD_DIM] for h in range(NSA_GROUP)] + [zero] * NSA_GROUP, axis=1)
    bot = jnp.concatenate([zero] * NSA_GROUP + [qt[h * HEAD_DIM:(h + 1) * HEAD_DIM]
                                                for h in range(NSA_GROUP, NSA_HEADS)], axis=1)
    qbd = jnp.concatenate([top, bot], axis=0).astype(bf16)

    def softmax_cols(s):
        m = jnp.max(s, axis=0, keepdims=True)
        p = jnp.exp(s - m)
        return m, p, jnp.sum(p, axis=0, keepdims=True)

    def pv(vt, p):
        pb = p.astype(bf16)
        return [jnp.dot(vt[g * HEAD_DIM:(g + 1) * HEAD_DIM], pb[:, g * half:(g + 1) * half],
                        preferred_element_type=f32) for g in range(NSA_KV_HEADS)]

    nc = 2 * ns
    r = lax.broadcasted_iota(jnp.int32, (nc, qb), 0)
    lane = lax.broadcasted_iota(jnp.int32, (nc, qb), 1)
    cidx = jnp.where(r < ns, 2 * r, 2 * (r - ns) + 1)
    cbias = jnp.where((cidx + 1) * CMP_BLOCK - 1 <= s0 + lane, 0.0, NEG)
    sc = jnp.dot(ck_ref[...], qbd, preferred_element_type=f32) + _lane_tile(cbias, NSA_HEADS)
    m, p, l = softmax_cols(sc)
    pn = p * jnp.where(m > 0.5 * NEG, 1.0 / l, 0.0)
    o_cmp = pv(cvt_ref[...], pn)

    blk = lax.broadcasted_iota(jnp.int32, (ns, qb), 0)
    qpos = s0 + lax.broadcasted_iota(jnp.int32, (ns, qb), 1)
    cur = qpos // SEL_BLOCK
    forced = (blk == 0) | (blk == cur) | (blk == cur - 1)
    force_key = lax.bitcast_convert_type(jnp.full((ns, qb), NSA_FORCE, f32), jnp.int32)
    for g in range(NSA_KV_HEADS):
        imp = pn[:, g * half:g * half + qb]
        for h in range(1, NSA_GROUP):
            imp = imp + pn[:, g * half + h * qb:g * half + (h + 1) * qb]
        imp = imp[:ns] + imp[ns:]
        key = jnp.where(blk > cur, -1, jnp.where(forced, force_key, lax.bitcast_convert_type(imp, jnp.int32)))
        thr = jnp.zeros((1, qb), jnp.int32)
        for bit in range(30, -1, -1):
            cand = thr | (1 << bit)
            cnt = jnp.sum(jnp.where(key >= cand, 1.0, 0.0), axis=0, keepdims=True)
            thr = jnp.where(cnt >= NSA_TOPN, cand, thr)
        above = key > thr
        n_above = jnp.sum(jnp.where(above, 1.0, 0.0), axis=0, keepdims=True)
        tie = key == thr
        rank = jnp.dot(tri_ref[...], jnp.where(tie, 1.0, 0.0).astype(bf16), preferred_element_type=f32)
        sel = above | (tie & (rank <= NSA_TOPN - n_above))
        selb_ref[g] = jnp.where(sel, 0.0, NEG)

    m_ref[...] = jnp.full(m_ref.shape, NEG, f32)
    l_ref[...] = jnp.zeros(l_ref.shape, f32)
    acc_ref[...] = jnp.zeros(acc_ref.shape, f32)
    per_tile = NSA_KT // SEL_BLOCK

    def slc_tile(kt, causal):
        k0 = pl.multiple_of(kt * NSA_KT, NSA_KT)
        s = jnp.dot(ks_ref[pl.ds(k0, NSA_KT), :], qbd, preferred_element_type=f32)
        rows = []
        for b in range(per_tile):
            brow = jnp.concatenate([selb_ref[g, pl.ds(kt * per_tile + b, 1), :] for g in range(NSA_KV_HEADS)
                                    for _ in range(NSA_GROUP)], axis=1)
            rows.append(s[b * SEL_BLOCK:(b + 1) * SEL_BLOCK] + brow)
        s = jnp.concatenate(rows, axis=0)
        if causal:
            kpos = k0 + lax.broadcasted_iota(jnp.int32, (NSA_KT, qb), 0)
            qq = s0 + lax.broadcasted_iota(jnp.int32, (NSA_KT, qb), 1)
            s = s + _lane_tile(jnp.where(kpos <= qq, 0.0, NEG), NSA_HEADS)
        m_old = m_ref[...]
        m_new = jnp.maximum(m_old, jnp.max(s, axis=0, keepdims=True))
        alpha = jnp.exp(m_old - m_new)
        p = jnp.exp(s - m_new)
        m_ref[...] = m_new
        l_ref[...] = l_ref[...] * alpha + jnp.sum(p, axis=0, keepdims=True)
        upd = pv(vst_ref[:, pl.ds(k0, NSA_KT)], p)
        for g in range(NSA_KV_HEADS):
            acc_ref[g] = acc_ref[g] * alpha[:, g * half:(g + 1) * half] + upd[g]

    kd = s0 // NSA_KT

    def body(kt, carry):
        slc_tile(kt, False)
        return carry

    lax.fori_loop(0, kd, body, 0)
    slc_tile(kd, True)
    inv_slc = 1.0 / l_ref[...]

    w0 = pl.multiple_of(s0, qb)
    sw = jnp.dot(kw_ref[pl.ds(w0, NSA_WSPAN), :], qbd, preferred_element_type=f32)
    rr = lax.broadcasted_iota(jnp.int32, (NSA_WSPAN, qb), 0)
    qi = lax.broadcasted_iota(jnp.int32, (NSA_WSPAN, qb), 1)
    ok = (rr >= qi) & (rr <= qi + NSA_WINDOW) & (rr + s0 >= NSA_WINDOW)
    sw = sw + _lane_tile(jnp.where(ok, 0.0, NEG), NSA_HEADS)
    _, pw, lw = softmax_cols(sw)
    o_win = pv(vwt_ref[:, pl.ds(w0, NSA_WSPAN)], pw)
    inv_win = 1.0 / lw

    gt = jax.nn.sigmoid(sm_ref[...].T)
    outs = []
    for h in range(NSA_HEADS):
        g, hg = divmod(h, NSA_GROUP)
        c0, c1 = hg * qb, (hg + 1) * qb
        g_cmp = gt[h:h + 1]
        g_slc = gt[NSA_HEADS + h:NSA_HEADS + h + 1] * inv_slc[:, g * half + c0:g * half + c1]
        g_win = gt[2 * NSA_HEADS + h:2 * NSA_HEADS + h + 1] * inv_win[:, g * half + c0:g * half + c1]
        outs.append(o_cmp[g][:, c0:c1] * g_cmp + acc_ref[g, :, c0:c1] * g_slc + o_win[g][:, c0:c1] * g_win)
    o_ref[...] = jnp.concatenate(outs, axis=0).T


def _nsa_prompt_attention(qr, small, ck, cv, ksr, vs, kwr, vw):
    l = qr.shape[0]
    ns = l // SEL_BLOCK
    nc = 2 * ns
    bf16 = jnp.bfloat16
    perm = jnp.concatenate([jnp.arange(0, nc, 2), jnp.arange(1, nc, 2)])
    ckp = ck[perm].astype(bf16)
    cvt = cv[perm].T.astype(bf16)
    pad = jnp.zeros((NSA_WINDOW, NSA_KV_W), bf16)
    kwp = jnp.concatenate([pad, kwr.astype(bf16)], axis=0)
    vwt = jnp.concatenate([pad, vw.astype(bf16)], axis=0).T
    tri = (jnp.arange(ns)[:, None] >= jnp.arange(ns)[None, :]).astype(bf16)
    full = lambda a: pl.BlockSpec(a.shape, lambda i: (0,) * a.ndim)
    args = (qr, small, ckp, cvt, ksr.astype(bf16), vs.T.astype(bf16), kwp, vwt, tri)
    return pl.pallas_call(
        functools.partial(_nsa_prompt_kernel, ns=ns),
        grid=(l // NSA_QBLOCK,),
        in_specs=[pl.BlockSpec((NSA_QBLOCK, NSA_Q_W), lambda i: (i, 0)),
                  pl.BlockSpec((NSA_QBLOCK, LANES), lambda i: (i, 0))] + [full(a) for a in args[2:]],
        out_specs=pl.BlockSpec((NSA_QBLOCK, NSA_Q_W), lambda i: (i, 0)),
        out_shape=jax.ShapeDtypeStruct((l, NSA_Q_W), jnp.float32),
        scratch_shapes=[pltpu.VMEM((NSA_KV_HEADS, ns, NSA_QBLOCK), jnp.float32),
                        pltpu.VMEM((1, NSA_COLS), jnp.float32),
                        pltpu.VMEM((1, NSA_COLS), jnp.float32),
                        pltpu.VMEM((NSA_KV_HEADS, HEAD_DIM, NSA_COLS // 2), jnp.float32)],
        compiler_params=pltpu.CompilerParams(dimension_semantics=("arbitrary",),
                                             vmem_limit_bytes=VMEM_LIMIT_BYTES),
        name="nsa_prompt",
    )(*args)


def _split_cols(h, widths):
    parts, start = [], 0
    for w in widths:
        parts.append(h[..., start:start + w])
        start += w
    return parts


def _even_widths():
    return (NSA_Q_W,) + (NSA_KV_W,) * 6 + (3 * NSA_HEADS, 3 * GDN_W, GDN_HEADS, GDN_HEADS, GDN_W)


def _rms_norm(x, w):
    return x * lax.rsqrt(jnp.mean(jnp.square(x), axis=-1, keepdims=True) + NORM_EPS) * w


def _l2_norm(x):
    return x * lax.rsqrt(jnp.sum(jnp.square(x), axis=-1, keepdims=True) + NORM_EPS)


def _rope(x, pos):
    half = HEAD_DIM // 2
    inv_freq = ROPE_THETA ** (-2.0 * jnp.arange(half, dtype=jnp.float32) / HEAD_DIM)
    ang = pos.astype(jnp.float32)[:, None] * inv_freq[None, :]
    cos, sin = jnp.cos(ang)[:, None, :], jnp.sin(ang)[:, None, :]
    xf = x.astype(jnp.float32)
    x1, x2 = xf[..., :half], xf[..., half:]
    return jnp.concatenate([x1 * cos - x2 * sin, x2 * cos + x1 * sin], axis=-1)


def _causal_dwconv(hist, u, w):
    width, s = w.shape[0], u.shape[1]
    ext = jnp.concatenate([hist.astype(u.dtype), u], axis=1)
    out = w[0] * ext[:, :s]
    for j in range(1, width):
        out = out + w[j] * ext[:, j:j + s]
    return out, ext[:, s:]


def _masked_softmax(s, mask):
    s = jnp.where(mask, s, -jnp.inf)
    m = jnp.max(s, axis=-1, keepdims=True)
    m = jnp.where(jnp.isfinite(m), m, 0.0)
    e = jnp.where(mask, jnp.exp(s - m), 0.0)
    den = jnp.sum(e, axis=-1, keepdims=True)
    return e / jnp.where(den > 0.0, den, 1.0)


def _gather_pages(pool, page_table, layer_idx):
    rows = pool[page_table, layer_idx]
    return rows.reshape(rows.shape[0], -1, *rows.shape[3:])


def _nsa_compress(rows, w1, b1, w2, pe):
    b, l, g, dh = rows.shape
    nc = l // CMP_BLOCK
    blk = rows[:, :nc * CMP_BLOCK].astype(jnp.float32).reshape(b, nc, CMP_BLOCK, g, dh) + pe[:, None, :]
    flat = blk.transpose(0, 1, 3, 2, 4).reshape(b, nc, g, CMP_BLOCK * dh)
    return jax.nn.gelu(flat @ w1 + b1) @ w2


def _nsa_compressed_kv(k_rows, v_rows, cw1, cb1, cw2, cpe):
    ck = _nsa_compress(k_rows, cw1[0], cb1[0], cw2[0], cpe[0])
    cv = _nsa_compress(v_rows, cw1[1], cb1[1], cw2[1], cpe[1])
    nc = ck.shape[1]
    ck = _rope(ck, (jnp.arange(nc) + 1) * CMP_BLOCK - 1)
    return ck, cv


def _nsa_attend(q, qpos, ck, cv, sk, sv, wk, wv, wpos):
    b, nq = q.shape[:2]
    scale = HEAD_DIM ** -0.5
    nc, ns = ck.shape[1], sk.shape[2]
    cend = (jnp.arange(nc) + 1) * CMP_BLOCK - 1
    s = jnp.einsum('bqghd,bcgd->bghqc', q, ck) * scale
    p_cmp = _masked_softmax(s, cend[None, :] <= qpos[:, None])
    o_cmp = jnp.einsum('bghqc,bcgd->bqghd', p_cmp, cv)
    imp = jnp.sum(p_cmp, axis=2)
    imp = jnp.pad(imp, ((0, 0), (0, 0), (0, 0), (0, ns * CMP_PER_SEL - nc)))
    imp = imp.reshape(b, NSA_KV_HEADS, nq, ns, CMP_PER_SEL).sum(-1)
    blk = jnp.arange(ns)[None, :]
    cur = (qpos // SEL_BLOCK)[:, None]
    forced = (blk == 0) | (blk == cur) | (blk == cur - 1)
    imp = jnp.where(blk <= cur, jnp.where(forced, NSA_FORCE, imp), -1.0)
    _, idx = lax.top_k(imp, min(NSA_TOPN, ns))
    n = idx.shape[-1]
    pick = jax.vmap(jax.vmap(lambda kb, ix: kb[ix]))
    ksel = pick(sk, idx).reshape(b, NSA_KV_HEADS, nq, n * SEL_BLOCK, HEAD_DIM)
    vsel = pick(sv, idx).reshape(b, NSA_KV_HEADS, nq, n * SEL_BLOCK, HEAD_DIM)
    kpos = (idx[..., None] * SEL_BLOCK + jnp.arange(SEL_BLOCK)).reshape(b, NSA_KV_HEADS, nq, n * SEL_BLOCK)
    s = jnp.einsum('bqghd,bgqkd->bghqk', q, ksel) * scale
    p = _masked_softmax(s, (kpos <= qpos[:, None])[:, :, None])
    o_slc = jnp.einsum('bghqk,bgqkd->bqghd', p, vsel)
    dist = qpos[:, None] - wpos[None, :]
    wmask = (dist >= 0) & (dist <= NSA_WINDOW) & (wpos[None, :] >= 0)
    s = jnp.einsum('bqghd,bkgd->bghqk', q, wk) * scale
    p = _masked_softmax(s, wmask)
    o_win = jnp.einsum('bghqk,bkgd->bqghd', p, wv)
    return o_cmp, o_slc, o_win


def _nsa_prompt(q, kc, vc, ks, vs, kw, vw, gate, cw1, cb1, cw2, cpe):
    b, l = q.shape[:2]
    pos = jnp.arange(l)
    qr = _rope(q, pos)
    ck, cv = _nsa_compressed_kv(kc, vc, cw1, cb1, cw2, cpe)
    ksr = _rope(ks, pos)
    vsf = vs.astype(jnp.float32)
    kwr = _rope(kw, pos)
    vwf = vw.astype(jnp.float32)
    small = jnp.pad(gate.reshape(l, -1), ((0, 0), (0, LANES - gate.shape[-1])))
    flat = lambda t: t.reshape(t.shape[1], -1)
    o_nsa = _nsa_prompt_attention(flat(qr), small, flat(ck), flat(cv), flat(ksr), flat(vsf), flat(kwr), flat(vwf))
    keep = min(NSA_WINDOW, l)
    rows_cmp = jnp.stack([kc, vc], axis=2)
    rows_slc = jnp.stack([ksr, vsf], axis=2)
    rows_win = jnp.stack([kwr[:, l - keep:], vwf[:, l - keep:]], axis=2)
    return o_nsa[None], rows_cmp, rows_slc, rows_win


def _nsa_sample(q, kc, vc, ks, vs, kw, vw, cmp_pool, slc_pool, layer_idx, win_buf, page_table, cw1, cb1, cw2, cpe):
    db, s = q.shape[:2]
    past = page_table.shape[1] * PAGE_SIZE
    l = past + s
    qpos = past + jnp.arange(s)
    qr = _rope(q, qpos).reshape(db, s, NSA_KV_HEADS, NSA_GROUP, HEAD_DIM)
    past_c = _gather_pages(cmp_pool, page_table, layer_idx)
    ck, cv = _nsa_compressed_kv(jnp.concatenate([past_c[:, :, 0], kc], axis=1),
                                jnp.concatenate([past_c[:, :, 1], vc], axis=1), cw1, cb1, cw2, cpe)
    ksr = _rope(ks, qpos)
    vsf = vs.astype(jnp.float32)
    past_s = _gather_pages(slc_pool, page_table, layer_idx)
    ns = -(-l // SEL_BLOCK)
    pad = ((0, 0), (0, ns * SEL_BLOCK - l), (0, 0), (0, 0))
    to_blocks = lambda t: jnp.pad(t.astype(jnp.float32), pad).reshape(
        db, ns, SEL_BLOCK, NSA_KV_HEADS, HEAD_DIM).transpose(0, 3, 1, 2, 4)
    sk = to_blocks(jnp.concatenate([past_s[:, :, 0], ksr], axis=1))
    sv = to_blocks(jnp.concatenate([past_s[:, :, 1], vsf], axis=1))
    kwr = _rope(kw, qpos)
    wb = win_buf.shape[1]
    wk = jnp.concatenate([win_buf[:, :, 0].astype(jnp.float32), kwr], axis=1)
    wv = jnp.concatenate([win_buf[:, :, 1].astype(jnp.float32), vw.astype(jnp.float32)], axis=1)
    wpos = past - wb + jnp.arange(wb + s)
    o_cmp, o_slc, o_win = _nsa_attend(qr, qpos, ck, cv, sk, sv, wk, wv, wpos)
    flat = lambda o: o.reshape(db, s, NSA_HEADS, HEAD_DIM)
    keep = min(NSA_WINDOW, l)
    rows_cmp = jnp.stack([kc, vc], axis=2)
    rows_slc = jnp.stack([ksr, vsf], axis=2)
    rows_win = jnp.stack([wk[:, wb + s - keep:], wv[:, wb + s - keep:]], axis=2)
    return flat(o_cmp), flat(o_slc), flat(o_win), rows_cmp, rows_slc, rows_win


def _gdn_chunked(q, k, v, g, beta, s0):
    b, l, h, dk = q.shape
    dv = v.shape[-1]
    c = GDN_CHUNK
    nch = l // c
    r = lambda a: jnp.moveaxis(a.reshape(b, nch, c, h, *a.shape[3:]), 3, 2)
    q, k, v, g, beta = r(q), r(k), r(v), r(g), r(beta)
    gc = jnp.cumsum(g, axis=-1)
    ii = jnp.arange(c)
    tri = ii[:, None] >= ii[None, :]
    strict = ii[:, None] > ii[None, :]
    diff = gc[..., :, None] - gc[..., None, :]
    gamma = jnp.where(tri, jnp.exp(jnp.where(tri, diff, 0.0)), 0.0)
    kb = k * beta[..., None]
    a_mat = jnp.where(strict, jnp.einsum('bnhik,bnhjk->bnhij', kb, k) * gamma, 0.0)
    eye = jnp.eye(c, dtype=jnp.float32)
    t_inv = lax.linalg.triangular_solve(eye + a_mat, jnp.broadcast_to(eye, a_mat.shape),
                                        left_side=True, lower=True, unit_diagonal=True)
    u = t_inv @ (v * beta[..., None])
    w = t_inv @ (kb * jnp.exp(gc)[..., None])
    qk = jnp.where(tri, jnp.einsum('bnhik,bnhjk->bnhij', q, k) * gamma, 0.0)
    qg = q * jnp.exp(gc)[..., None]
    kd = k * jnp.exp(gc[..., -1:] - gc)[..., None]
    glast = jnp.exp(gc[..., -1])

    def step(state, xs):
        qg_c, kd_c, u_c, w_c, qk_c, gl_c = xs
        v_new = u_c - jnp.einsum('bhck,bhkv->bhcv', w_c, state)
        o = jnp.einsum('bhck,bhkv->bhcv', qg_c, state) + jnp.einsum('bhij,bhjv->bhiv', qk_c, v_new)
        state = state * gl_c[..., None, None] + jnp.einsum('bhck,bhcv->bhkv', kd_c, v_new)
        return state, o

    xs = tuple(jnp.moveaxis(a, 1, 0) for a in (qg, kd, u, w, qk, glast))
    s_fin, o = lax.scan(step, s0, xs)
    o = jnp.moveaxis(jnp.moveaxis(o, 0, 1), 2, 3).reshape(b, l, h, dv)
    return o, s_fin


def _gdn_recurrent(q, k, v, g, beta, s0):
    def step(state, xs):
        q_t, k_t, v_t, g_t, b_t = xs
        state = state * jnp.exp(g_t)[..., None, None]
        v_t = (v_t - jnp.einsum('bhk,bhkv->bhv', k_t, state)) * b_t[..., None]
        state = state + jnp.einsum('bhk,bhv->bhkv', k_t, v_t)
        return state, jnp.einsum('bhk,bhkv->bhv', q_t, state)

    xs = tuple(jnp.moveaxis(a, 1, 0) for a in (q, k, v, g, beta))
    s_fin, o = lax.scan(step, s0, xs)
    return jnp.moveaxis(o, 0, 1), s_fin


def _gdn_mix(qkv, a, bt, z, conv_hist, s0, conv_w, a_log, dt_bias, norm_w, chunked):
    b, s = qkv.shape[:2]
    c, new_hist = _causal_dwconv(conv_hist, qkv, conv_w)
    c = jax.nn.silu(c.astype(jnp.float32))
    q, k, v = [t.reshape(b, s, GDN_HEADS, HEAD_DIM) for t in jnp.split(c, 3, axis=-1)]
    q = _l2_norm(q) * HEAD_DIM ** -0.5
    k = _l2_norm(k)
    beta = jax.nn.sigmoid(bt.astype(jnp.float32))
    g = -jnp.exp(a_log) * jax.nn.softplus(a.astype(jnp.float32) + dt_bias)
    s0 = s0.astype(jnp.float32)
    if chunked:
        o, s_fin = _gdn_chunked(q, k, v, g, beta, s0)
    else:
        o, s_fin = _gdn_recurrent(q, k, v, g, beta, s0)
    o = _rms_norm(o, norm_w) * jax.nn.silu(z.astype(jnp.float32).reshape(b, s, GDN_HEADS, HEAD_DIM))
    return o.reshape(b, s, GDN_W), new_hist, s_fin


def _even_merge(o_cmp, o_slc, o_win, gate, o_gdn):
    b, s = gate.shape[:2]
    gt = jax.nn.sigmoid(gate.astype(jnp.float32)).reshape(b, s, 3, NSA_HEADS, 1)
    o_nsa = gt[:, :, 0] * o_cmp + gt[:, :, 1] * o_slc + gt[:, :, 2] * o_win
    return jnp.concatenate([o_nsa.reshape(b, s, NSA_Q_W), o_gdn], axis=-1)


def _proj(x, w):
    b, s, d = x.shape
    n = w.shape[1]
    npad = -(-n // LANES) * LANES
    wp = jnp.pad(w, ((0, 0), (0, npad - n)))
    return _matmul(x.reshape(b * s, d), wp)[:, :n].reshape(b, s, n)


def _even_prompt(x, w_in, cw1, cb1, cw2, cpe, conv_w, a_log, dt_bias, norm_w):
    b, l, _ = x.shape
    q, kc, vc, ks, vs, kw, vw, gate, qkv, a, bt, z = _split_cols(_proj(x, w_in), _even_widths())
    heads = lambda t: t.reshape(b, l, -1, HEAD_DIM)
    o_nsa, r_cmp, r_slc, r_win = _nsa_prompt(
        heads(q), heads(kc), heads(vc), heads(ks), heads(vs), heads(kw), heads(vw), gate, cw1, cb1, cw2, cpe)
    hist0 = jnp.zeros((b, GDN_CONV - 1, 3 * GDN_W), qkv.dtype)
    s0 = jnp.zeros((b, GDN_HEADS, HEAD_DIM, HEAD_DIM), jnp.float32)
    o_gdn, conv_hist, s_fin = _gdn_mix(qkv, a, bt, z, hist0, s0, conv_w, a_log, dt_bias, norm_w, True)
    return jnp.concatenate([o_nsa, o_gdn], axis=-1), r_cmp, r_slc, r_win, conv_hist, s_fin


def _even_sample(x, cmp_pool, slc_pool, layer_idx, win_buf, conv_hist, s0, page_table,
                 w_in, cw1, cb1, cw2, cpe, conv_w, a_log, dt_bias, norm_w):
    b, s, _ = x.shape
    q, kc, vc, ks, vs, kw, vw, gate, qkv, a, bt, z = _split_cols(_proj(x, w_in), _even_widths())
    heads = lambda t: t.reshape(b, s, -1, HEAD_DIM)
    o_cmp, o_slc, o_win, r_cmp, r_slc, r_win = _nsa_sample(
        heads(q), heads(kc), heads(vc), heads(ks), heads(vs), heads(kw), heads(vw),
        cmp_pool, slc_pool, layer_idx, win_buf, page_table, cw1, cb1, cw2, cpe)
    o_gdn, new_hist, s_fin = _gdn_mix(qkv, a, bt, z, conv_hist, s0, conv_w, a_log, dt_bias, norm_w, False)
    return _even_merge(o_cmp, o_slc, o_win, gate, o_gdn), r_cmp, r_slc, r_win, new_hist, s_fin


def _dilated_band_stats(q, k, v, d):
    b, l, h, dh = q.shape
    unit = d * DIL_BLOCK
    lp = -(-l // unit) * unit
    nb = lp // unit
    to_sub = lambda a: jnp.pad(a, ((0, 0), (0, lp - l), (0, 0), (0, 0))).reshape(b, nb, DIL_BLOCK, d, h, dh)
    qs, ks, vs = to_sub(q), to_sub(k), to_sub(v)
    prev = lambda a: jnp.concatenate([jnp.zeros_like(a[:, :1]), a[:, :-1]], axis=1)
    kk = jnp.concatenate([prev(ks), ks], axis=2)
    vv = jnp.concatenate([prev(vs), vs], axis=2)
    s = jnp.einsum('bnirhd,bnjrhd->bnrhij', qs, kk) * HEAD_DIM ** -0.5
    i = jnp.arange(DIL_BLOCK)
    j = jnp.arange(2 * DIL_BLOCK) - DIL_BLOCK
    dist = i[:, None] - j[None, :]
    sub_k = jnp.arange(nb)[:, None, None] * DIL_BLOCK + j[None, None, :]
    mask = (dist >= 0)[None] & (dist <= DIL_SPAN)[None] & (sub_k >= 0)
    s = jnp.where(mask[None, :, None, None], s, -jnp.inf)
    m = jnp.max(s, axis=-1)
    e = jnp.exp(s - m[..., None])
    den = jnp.sum(e, axis=-1)
    num = jnp.einsum('bnrhij,bnjrhd->bnrhid', e, vv)
    m = jnp.transpose(m, (0, 1, 4, 2, 3)).reshape(b, lp, h)[:, :l]
    den = jnp.transpose(den, (0, 1, 4, 2, 3)).reshape(b, lp, h)[:, :l]
    num = jnp.transpose(num, (0, 1, 4, 2, 3, 5)).reshape(b, lp, h, dh)[:, :l]
    return m, den, num


def _dilated_gather_stats(q, k_all, v_all, qpos, kpos0, d):
    kp = qpos[:, None] - jnp.arange(DIL_SPAN + 1)[None, :] * d
    idx = kp - kpos0
    valid = idx >= 0
    idxc = jnp.clip(idx, 0)
    kg, vg = k_all[:, idxc], v_all[:, idxc]
    s = jnp.einsum('bshd,bsmhd->bshm', q, kg) * HEAD_DIM ** -0.5
    s = jnp.where(valid[None, :, None, :], s, -jnp.inf)
    m = jnp.max(s, axis=-1)
    e = jnp.exp(s - m[..., None])
    return m, jnp.sum(e, axis=-1), jnp.einsum('bshm,bsmhd->bshd', e, vg)


def _combine_by_denominators(stats):
    m_all = stats[0][0]
    for m, _, _ in stats[1:]:
        m_all = jnp.maximum(m_all, m)
    num, den = None, None
    for m, dn, nm in stats:
        w = jnp.exp(m - m_all)
        num = w[..., None] * nm if num is None else num + w[..., None] * nm
        den = w * dn if den is None else den + w * dn
    return num / den[..., None]


def _dil_prompt(x, w_in):
    b, l, _ = x.shape
    q, k, v = [t.reshape(b, l, DIL_HEADS, HEAD_DIM) for t in jnp.split(_proj(x, w_in), 3, axis=-1)]
    pos = jnp.arange(l)
    qr, kr, vf = _rope(q, pos), _rope(k, pos), v.astype(jnp.float32)
    o = _combine_by_denominators([_dilated_band_stats(qr, kr, vf, d) for _, d in DIL_GROUPS])
    keep = min(DIL_MAX_WINDOW, l)
    buf = jnp.stack([kr[:, l - keep:], vf[:, l - keep:]], axis=2)
    return o.reshape(b, l, DIL_W), buf


def _dil_sample(x, buf, past, w_in):
    db, s, _ = x.shape
    q, k, v = [t.reshape(db, s, DIL_HEADS, HEAD_DIM) for t in jnp.split(_proj(x, w_in), 3, axis=-1)]
    qpos = past + jnp.arange(s)
    qr, kr = _rope(q, qpos), _rope(k, qpos)
    wb = buf.shape[1]
    k_all = jnp.concatenate([buf[:, :, 0].astype(jnp.float32), kr], axis=1)
    v_all = jnp.concatenate([buf[:, :, 1].astype(jnp.float32), v.astype(jnp.float32)], axis=1)
    o = _combine_by_denominators(
        [_dilated_gather_stats(qr, k_all, v_all, qpos, past - wb, d) for _, d in DIL_GROUPS])
    keep = min(DIL_MAX_WINDOW, past + s)
    new_buf = jnp.stack([k_all[:, wb + s - keep:], v_all[:, wb + s - keep:]], axis=2)
    return o.reshape(db, s, DIL_W), new_buf


def kernel(x_prompt, x_sample, cache_nsa_cmp_kv, cache_nsa_slc_kv, state_nsa_win_kv, state_gdn_conv,
           state_gdn_S, state_dil_kv, state_ffn_conv, page_table, w_in_a, nsa_cmp_w1, nsa_cmp_b1, nsa_cmp_w2,
           nsa_cmp_pe, gdn_conv_w, gdn_A_log, gdn_dt_bias, gdn_norm_w, w_out_a, w_in_c, w_out_c,
           ln_mix_g, ln_mix_b, ffn_w_in, ffn_conv_w, ffn_conv_b, ffn_w_out, ln_ffn_g, ln_ffn_b):
    past = page_table.shape[1] * PAGE_SIZE
    bp, lp, d = x_prompt.shape
    bs, ls, _ = x_sample.shape
    assert bp == 1 and ls == 1
    xp, xs = x_prompt, x_sample
    cmp_p, cmp_s, slc_p, slc_s, win_p, win_s = [], [], [], [], [], []
    gconv_p, gconv_s, gstate_p, gstate_s = [], [], [], []
    dil_p, dil_s, ffn_p, ffn_s = [], [], [], []
    for layer in range(DEPTH):
        if layer % 2 == 0:
            la = layer // 2
            wa = (w_in_a[la], nsa_cmp_w1[la], nsa_cmp_b1[la], nsa_cmp_w2[la], nsa_cmp_pe[la],
                  gdn_conv_w[la], gdn_A_log[la], gdn_dt_bias[la], gdn_norm_w[la])
            mp, rc, rs, rw, hc, hs_ = _even_prompt(xp, *wa)
            cmp_p.append(rc); slc_p.append(rs); win_p.append(rw); gconv_p.append(hc); gstate_p.append(hs_)
            ms, rc, rs, rw, hc, hs_ = _even_sample(xs, cache_nsa_cmp_kv, cache_nsa_slc_kv, la,
                                                   state_nsa_win_kv[:, la], state_gdn_conv[:, la],
                                                   state_gdn_S[:, la], page_table, *wa)
            cmp_s.append(rc); slc_s.append(rs); win_s.append(rw); gconv_s.append(hc); gstate_s.append(hs_)
            w_out = w_out_a[la]
        else:
            lc = layer // 2
            mp, bpf = _dil_prompt(xp, w_in_c[lc])
            ms, bsf = _dil_sample(xs, state_dil_kv[:, lc], past, w_in_c[lc])
            dil_p.append(bpf); dil_s.append(bsf)
            w_out = w_out_c[lc]
        xp2 = _matmul_ln(mp.reshape(lp, -1), w_out, xp.reshape(lp, d), ln_mix_g[layer], ln_mix_b[layer])
        xs2 = _matmul_ln(ms.reshape(bs, -1), w_out, xs.reshape(bs, d), ln_mix_g[layer], ln_mix_b[layer])
        fargs = (ffn_w_in[layer], ffn_conv_w[layer], ffn_conv_b[layer], ffn_w_out[layer],
                 ln_ffn_g[layer], ln_ffn_b[layer])
        xp3, hp = _ffn_seq(xp2, *fargs)
        xs3, hs = _ffn_step(xs2, state_ffn_conv[:, layer], *fargs)
        xp, xs = xp3.reshape(1, lp, d), xs3.reshape(bs, 1, d)
        ffn_p.append(hp[None]); ffn_s.append(hs)

    def stk(lst):
        return jnp.stack(lst, axis=1)

    return (xp, xs, stk(cmp_p), stk(cmp_s), stk(slc_p), stk(slc_s), stk(win_p), stk(win_s),
            stk(gconv_p), stk(gconv_s), stk(gstate_p), stk(gstate_s), stk(dil_p), stk(dil_s),
            stk(ffn_p), stk(ffn_s))
```

```python
import functools
import math

import jax
import jax.numpy as jnp
from jax import lax
from jax.experimental import pallas as pl
from jax.experimental.pallas import tpu as pltpu

D_MODEL = 1024
DEPTH = 2
PAGE_SIZE = 128
HEAD_DIM = 64
ROPE_THETA = 10000.0
NSA_HEADS = 8
NSA_KV_HEADS = 2
NSA_GROUP = NSA_HEADS // NSA_KV_HEADS
CMP_BLOCK = 32
SEL_BLOCK = 64
CMP_PER_SEL = SEL_BLOCK // CMP_BLOCK
NSA_TOPN = 16
NSA_WINDOW = 512
NSA_QBLOCK = 128
NSA_FORCE = 1.0e4
GDN_HEADS = 8
GDN_CONV = 4
GDN_CHUNK = 64
DIL_HEADS = 16
DIL_GROUPS = ((128, 1), (512, 4), (2048, 16))
DIL_SPAN = 128
DIL_BLOCK = 128
DIL_MAX_WINDOW = 2048
D_FF = 2816
FFN_CONV = 3
DEEPNORM_ALPHA = (2.0 * DEPTH) ** 0.25
LN_EPS = 1e-5
NORM_EPS = 1e-6
NSA_Q_W = NSA_HEADS * HEAD_DIM
NSA_KV_W = NSA_KV_HEADS * HEAD_DIM
GDN_W = GDN_HEADS * HEAD_DIM
DIL_W = DIL_HEADS * HEAD_DIM

LANES = 128
VMEM_LIMIT_BYTES = 56 * 1024 * 1024


def _layer_norm_rows(r, g, b):
    mu = jnp.mean(r, axis=-1, keepdims=True)
    d = r - mu
    var = jnp.mean(d * d, axis=-1, keepdims=True)
    return d * lax.rsqrt(var + LN_EPS) * g + b


def _mm_kernel(x_ref, w_ref, o_ref):
    o_ref[...] = jnp.dot(x_ref[...].astype(jnp.bfloat16), w_ref[...], preferred_element_type=jnp.float32)


def _mm_ln_kernel(x_ref, w_ref, res_ref, g_ref, b_ref, o_ref):
    acc = jnp.dot(x_ref[...].astype(jnp.bfloat16), w_ref[...], preferred_element_type=jnp.float32)
    o_ref[...] = _layer_norm_rows(DEEPNORM_ALPHA * res_ref[...] + acc, g_ref[...], b_ref[...])


def _row_tile(m):
    return 512 if m % 512 == 0 else m


def _matmul(x, w):
    m, k = x.shape
    n = w.shape[1]
    tm = _row_tile(m)
    tn = n
    for cand in (1152, 1024, 768, 512):
        if n % cand == 0:
            tn = cand
            break
    return pl.pallas_call(
        _mm_kernel,
        grid=(m // tm, n // tn),
        in_specs=[pl.BlockSpec((tm, k), lambda i, j: (i, 0)),
                  pl.BlockSpec((k, tn), lambda i, j: (0, j))],
        out_specs=pl.BlockSpec((tm, tn), lambda i, j: (i, j)),
        out_shape=jax.ShapeDtypeStruct((m, n), jnp.float32),
        compiler_params=pltpu.CompilerParams(dimension_semantics=("parallel", "arbitrary"),
                                             vmem_limit_bytes=VMEM_LIMIT_BYTES),
        name="matmul",
    )(x, w.astype(jnp.bfloat16))


def _matmul_ln(x, w, res, g, b):
    m, k = x.shape
    n = w.shape[1]
    tm = _row_tile(m)
    return pl.pallas_call(
        _mm_ln_kernel,
        grid=(m // tm,),
        in_specs=[pl.BlockSpec((tm, k), lambda i: (i, 0)),
                  pl.BlockSpec((k, n), lambda i: (0, 0)),
                  pl.BlockSpec((tm, n), lambda i: (i, 0)),
                  pl.BlockSpec((1, n), lambda i: (0, 0)),
                  pl.BlockSpec((1, n), lambda i: (0, 0))],
        out_specs=pl.BlockSpec((tm, n), lambda i: (i, 0)),
        out_shape=jax.ShapeDtypeStruct((m, n), jnp.float32),
        compiler_params=pltpu.CompilerParams(dimension_semantics=("arbitrary",),
                                             vmem_limit_bytes=VMEM_LIMIT_BYTES),
        name="matmul_ln",
    )(x, w.astype(jnp.bfloat16), res, g.reshape(1, n), b.reshape(1, n))


FFN_CHUNK = D_FF // 2
FFN_NCHUNK = D_FF // FFN_CHUNK


def _ffn_seq_kernel(x_ref, wa_ref, wg_ref, cwa_ref, cwg_ref, cba_ref, cbg_ref, wo_ref, lg_ref, lb_ref,
                    y_ref, ha_ref, hg_ref, acc_ref, carry_ref):
    i, j = pl.program_id(0), pl.program_id(1)
    tm = x_ref.shape[0]
    x = x_ref[...]
    xb = x.astype(jnp.bfloat16)

    @pl.when(i == 0)
    def _():
        carry_ref[j] = jnp.zeros(carry_ref.shape[1:], jnp.float32)

    def conv(u, cw_ref, cb_ref, slot):
        prev = carry_ref[j, slot]
        p2, p1 = prev[6:7], prev[7:8]
        row = lax.broadcasted_iota(jnp.int32, u.shape, 0)
        u1 = jnp.where(row == 0, p1, pltpu.roll(u, 1, 0))
        u2 = jnp.where(row == 0, p2, jnp.where(row == 1, p1, pltpu.roll(u, 2, 0)))
        carry_ref[j, slot] = u[tm - 8:]
        cw = cw_ref[...]
        return cw[0:1] * u2 + cw[1:2] * u1 + cw[2:3] * u + cb_ref[...]

    ua = jnp.dot(xb, wa_ref[...], preferred_element_type=jnp.float32)
    ug = jnp.dot(xb, wg_ref[...], preferred_element_type=jnp.float32)
    ha_ref[...] = ua[tm - 8:]
    hg_ref[...] = ug[tm - 8:]
    a = conv(ua, cwa_ref, cba_ref, 0)
    g = conv(ug, cwg_ref, cbg_ref, 1)
    h = (a * jax.nn.sigmoid(a) * g).astype(jnp.bfloat16)
    part = jnp.dot(h, wo_ref[...], preferred_element_type=jnp.float32)

    @pl.when(j == 0)
    def _():
        acc_ref[...] = part

    @pl.when(j > 0)
    def _():
        acc_ref[...] += part

    @pl.when(j == pl.num_programs(1) - 1)
    def _():
        y_ref[...] = _layer_norm_rows(DEEPNORM_ALPHA * x + acc_ref[...], lg_ref[...], lb_ref[...])


def _ffn_seq(x, w_in, conv_w, conv_b, w_out, ln_g, ln_b):
    l, d = x.shape
    tm = 512
    c, nc = FFN_CHUNK, FFN_NCHUNK
    w_in = w_in.astype(jnp.bfloat16)
    cw8 = jnp.zeros((8, 2 * D_FF), jnp.float32).at[:FFN_CONV].set(conv_w)
    cb = conv_b.reshape(1, 2 * D_FF)
    y, ha, hg = pl.pallas_call(
        _ffn_seq_kernel,
        grid=(l // tm, nc),
        in_specs=[pl.BlockSpec((tm, d), lambda i, j: (i, 0)),
                  pl.BlockSpec((d, c), lambda i, j: (0, j)),
                  pl.BlockSpec((d, c), lambda i, j: (0, j + nc)),
                  pl.BlockSpec((8, c), lambda i, j: (0, j)),
                  pl.BlockSpec((8, c), lambda i, j: (0, j + nc)),
                  pl.BlockSpec((1, c), lambda i, j: (0, j)),
                  pl.BlockSpec((1, c), lambda i, j: (0, j + nc)),
                  pl.BlockSpec((c, d), lambda i, j: (j, 0)),
                  pl.BlockSpec((1, d), lambda i, j: (0, 0)),
                  pl.BlockSpec((1, d), lambda i, j: (0, 0))],
        out_specs=[pl.BlockSpec((tm, d), lambda i, j: (i, 0)),
                   pl.BlockSpec((8, c), lambda i, j: (i, j)),
                   pl.BlockSpec((8, c), lambda i, j: (i, j))],
        out_shape=[jax.ShapeDtypeStruct((l, d), jnp.float32),
                   jax.ShapeDtypeStruct((l // tm * 8, D_FF), jnp.float32),
                   jax.ShapeDtypeStruct((l // tm * 8, D_FF), jnp.float32)],
        scratch_shapes=[pltpu.VMEM((tm, d), jnp.float32),
                        pltpu.VMEM((nc, 2, 8, c), jnp.float32)],
        compiler_params=pltpu.CompilerParams(dimension_semantics=("arbitrary", "arbitrary"),
                                             vmem_limit_bytes=VMEM_LIMIT_BYTES),
        name="ffn_seq",
    )(x, w_in, w_in, cw8, cw8, cb, cb, w_out.astype(jnp.bfloat16), ln_g.reshape(1, d), ln_b.reshape(1, d))
    hist = jnp.concatenate([ha[-(FFN_CONV - 1):], hg[-(FFN_CONV - 1):]], axis=-1)
    return y, hist


def _ffn_step_kernel(x_ref, h_ref, wa_ref, wg_ref, cwa_ref, cwg_ref, cba_ref, cbg_ref, wo_ref, lg_ref, lb_ref,
                     y_ref, ua_ref, ug_ref, acc_ref):
    j = pl.program_id(0)
    x = x_ref[...]
    xb = x.astype(jnp.bfloat16)
    ua = jnp.dot(xb, wa_ref[...], preferred_element_type=jnp.float32)
    ug = jnp.dot(xb, wg_ref[...], preferred_element_type=jnp.float32)
    ua_ref[...] = ua
    ug_ref[...] = ug
    cwa, cwg = cwa_ref[...], cwg_ref[...]
    a = cwa[0:1] * h_ref[0, 0] + cwa[1:2] * h_ref[1, 0] + cwa[2:3] * ua + cba_ref[...]
    g = cwg[0:1] * h_ref[0, 1] + cwg[1:2] * h_ref[1, 1] + cwg[2:3] * ug + cbg_ref[...]
    h = (a * jax.nn.sigmoid(a) * g).astype(jnp.bfloat16)
    part = jnp.dot(h, wo_ref[...], preferred_element_type=jnp.float32)

    @pl.when(j == 0)
    def _():
        acc_ref[...] = part

    @pl.when(j > 0)
    def _():
        acc_ref[...] += part

    @pl.when(j == pl.num_programs(0) - 1)
    def _():
        y_ref[...] = _layer_norm_rows(DEEPNORM_ALPHA * x + acc_ref[...], lg_ref[...], lb_ref[...])


def _ffn_step(x, hist, w_in, conv_w, conv_b, w_out, ln_g, ln_b):
    b, d = x.shape
    c, nc = FFN_CHUNK, FFN_NCHUNK
    w_in = w_in.astype(jnp.bfloat16)
    cw8 = jnp.zeros((8, 2 * D_FF), jnp.float32).at[:FFN_CONV].set(conv_w)
    cb = conv_b.reshape(1, 2 * D_FF)
    h4 = jnp.transpose(hist, (1, 0, 2)).reshape(2, b, 2, D_FF).transpose(0, 2, 1, 3)
    y, ua, ug = pl.pallas_call(
        _ffn_step_kernel,
        grid=(nc,),
        in_specs=[pl.BlockSpec((b, d), lambda j: (0, 0)),
                  pl.BlockSpec((2, 2, b, c), lambda j: (0, 0, 0, j)),
                  pl.BlockSpec((d, c), lambda j: (0, j)),
                  pl.BlockSpec((d, c), lambda j: (0, j + nc)),
                  pl.BlockSpec((8, c), lambda j: (0, j)),
                  pl.BlockSpec((8, c), lambda j: (0, j + nc)),
                  pl.BlockSpec((1, c), lambda j: (0, j)),
                  pl.BlockSpec((1, c), lambda j: (0, j + nc)),
                  pl.BlockSpec((c, d), lambda j: (j, 0)),
                  pl.BlockSpec((1, d), lambda j: (0, 0)),
                  pl.BlockSpec((1, d), lambda j: (0, 0))],
        out_specs=[pl.BlockSpec((b, d), lambda j: (0, 0)),
                   pl.BlockSpec((b, c), lambda j: (0, j)),
                   pl.BlockSpec((b, c), lambda j: (0, j))],
        out_shape=[jax.ShapeDtypeStruct((b, d), jnp.float32),
                   jax.ShapeDtypeStruct((b, D_FF), jnp.float32),
                   jax.ShapeDtypeStruct((b, D_FF), jnp.float32)],
        scratch_shapes=[pltpu.VMEM((b, d), jnp.float32)],
        compiler_params=pltpu.CompilerParams(dimension_semantics=("arbitrary",),
                                             vmem_limit_bytes=VMEM_LIMIT_BYTES),
        name="ffn_step",
    )(x, h4, w_in, w_in, cw8, cw8, cb, cb, w_out.astype(jnp.bfloat16), ln_g.reshape(1, d), ln_b.reshape(1, d))
    u = jnp.concatenate([ua, ug], axis=-1)
    return y, jnp.concatenate([hist[:, 1:], u[:, None]], axis=1)


NEG = -1e30
NSA_KT = 512
NSA_COLS = NSA_HEADS * NSA_QBLOCK
NSA_WSPAN = NSA_WINDOW + NSA_QBLOCK


def _lane_tile(x, n):
    return jnp.concatenate([x] * n, axis=1)


def _nsa_prompt_kernel(q_ref, sm_ref, ck_ref, cvt_ref, ks_ref, vst_ref, kw_ref, vwt_ref, tri_ref, o_ref,
                       selb_ref, m_ref, l_ref, acc_ref, *, ns):
    f32, bf16 = jnp.float32, jnp.bfloat16
    qb = NSA_QBLOCK
    i = pl.program_id(0)
    s0 = i * qb
    half = NSA_COLS // 2

    qt = (q_ref[...] * (HEAD_DIM ** -0.5)).T
    zero = jnp.zeros((HEAD_DIM, qb), f32)
    top = jnp.concatenate([qt[h * HEAD_DIM:(h + 1) * HEAD_DIM] for h in range(NSA_GROUP)] + [zero] * NSA_GROUP, axis=1)
    bot = jnp.concatenate([zero] * NSA_GROUP + [qt[h * HEAD_DIM:(h + 1) * HEAD_DIM]
                                                for h in range(NSA_GROUP, NSA_HEADS)], axis=1)
    qbd = jnp.concatenate([top, bot], axis=0).astype(bf16)

    def softmax_cols(s):
        m = jnp.max(s, axis=0, keepdims=True)
        p = jnp.exp(s - m)
        return m, p, jnp.sum(p, axis=0, keepdims=True)

    def pv(vt, p):
        pb = p.astype(bf16)
        return [jnp.dot(vt[g * HEAD_DIM:(g + 1) * HEAD_DIM], pb[:, g * half:(g + 1) * half],
                        preferred_element_type=f32) for g in range(NSA_KV_HEADS)]

    nc = 2 * ns
    r = lax.broadcasted_iota(jnp.int32, (nc, qb), 0)
    lane = lax.broadcasted_iota(jnp.int32, (nc, qb), 1)
    cidx = jnp.where(r < ns, 2 * r, 2 * (r - ns) + 1)
    cbias = jnp.where((cidx + 1) * CMP_BLOCK - 1 <= s0 + lane, 0.0, NEG)
    sc = jnp.dot(ck_ref[...], qbd, preferred_element_type=f32) + _lane_tile(cbias, NSA_HEADS)
    m, p, l = softmax_cols(sc)
    pn = p * jnp.where(m > 0.5 * NEG, 1.0 / l, 0.0)
    o_cmp = pv(cvt_ref[...], pn)

    blk = lax.broadcasted_iota(jnp.int32, (ns, qb), 0)
    qpos = s0 + lax.broadcasted_iota(jnp.int32, (ns, qb), 1)
    cur = qpos // SEL_BLOCK
    forced = (blk == 0) | (blk == cur) | (blk == cur - 1)
    force_key = lax.bitcast_convert_type(jnp.full((ns, qb), NSA_FORCE, f32), jnp.int32)
    for g in range(NSA_KV_HEADS):
        imp = pn[:, g * half:g * half + qb]
        for h in range(1, NSA_GROUP):
            imp = imp + pn[:, g * half + h * qb:g * half + (h + 1) * qb]
        imp = imp[:ns] + imp[ns:]
        key = jnp.where(blk > cur, -1, jnp.where(forced, force_key, lax.bitcast_convert_type(imp, jnp.int32)))
        thr = jnp.zeros((1, qb), jnp.int32)
        for bit in range(30, -1, -1):
            cand = thr | (1 << bit)
            cnt = jnp.sum(jnp.where(key >= cand, 1.0, 0.0), axis=0, keepdims=True)
            thr = jnp.where(cnt >= NSA_TOPN, cand, thr)
        above = key > thr
        n_above = jnp.sum(jnp.where(above, 1.0, 0.0), axis=0, keepdims=True)
        tie = key == thr
        rank = jnp.dot(tri_ref[...], jnp.where(tie, 1.0, 0.0).astype(bf16), preferred_element_type=f32)
        sel = above | (tie & (rank <= NSA_TOPN - n_above))
        selb_ref[g] = jnp.where(sel, 0.0, NEG)

    m_ref[...] = jnp.full(m_ref.shape, NEG, f32)
    l_ref[...] = jnp.zeros(l_ref.shape, f32)
    acc_ref[...] = jnp.zeros(acc_ref.shape, f32)
    per_tile = NSA_KT // SEL_BLOCK

    def slc_tile(kt, causal):
        k0 = pl.multiple_of(kt * NSA_KT, NSA_KT)
        s = jnp.dot(ks_ref[pl.ds(k0, NSA_KT), :], qbd, preferred_element_type=f32)
        rows = []
        for b in range(per_tile):
            brow = jnp.concatenate([selb_ref[g, pl.ds(kt * per_tile + b, 1), :] for g in range(NSA_KV_HEADS)
                                    for _ in range(NSA_GROUP)], axis=1)
            rows.append(s[b * SEL_BLOCK:(b + 1) * SEL_BLOCK] + brow)
        s = jnp.concatenate(rows, axis=0)
        if causal:
            kpos = k0 + lax.broadcasted_iota(jnp.int32, (NSA_KT, qb), 0)
            qq = s0 + lax.broadcasted_iota(jnp.int32, (NSA_KT, qb), 1)
            s = s + _lane_tile(jnp.where(kpos <= qq, 0.0, NEG), NSA_HEADS)
        m_old = m_ref[...]
        m_new = jnp.maximum(m_old, jnp.max(s, axis=0, keepdims=True))
        alpha = jnp.exp(m_old - m_new)
        p = jnp.exp(s - m_new)
        m_ref[...] = m_new
        l_ref[...] = l_ref[...] * alpha + jnp.sum(p, axis=0, keepdims=True)
        upd = pv(vst_ref[:, pl.ds(k0, NSA_KT)], p)
        for g in range(NSA_KV_HEADS):
            acc_ref[g] = acc_ref[g] * alpha[:, g * half:(g + 1) * half] + upd[g]

    kd = s0 // NSA_KT

    def body(kt, carry):
        slc_tile(kt, False)
        return carry

    lax.fori_loop(0, kd, body, 0)
    slc_tile(kd, True)
    inv_slc = 1.0 / l_ref[...]

    w0 = pl.multiple_of(s0, qb)
    sw = jnp.dot(kw_ref[pl.ds(w0, NSA_WSPAN), :], qbd, preferred_element_type=f32)
    rr = lax.broadcasted_iota(jnp.int32, (NSA_WSPAN, qb), 0)
    qi = lax.broadcasted_iota(jnp.int32, (NSA_WSPAN, qb), 1)
    ok = (rr >= qi) & (rr <= qi + NSA_WINDOW) & (rr + s0 >= NSA_WINDOW)
    sw = sw + _lane_tile(jnp.where(ok, 0.0, NEG), NSA_HEADS)
    _, pw, lw = softmax_cols(sw)
    o_win = pv(vwt_ref[:, pl.ds(w0, NSA_WSPAN)], pw)
    inv_win = 1.0 / lw

    gt = jax.nn.sigmoid(sm_ref[...].T)
    outs = []
    for h in range(NSA_HEADS):
        g, hg = divmod(h, NSA_GROUP)
        c0, c1 = hg * qb, (hg + 1) * qb
        g_cmp = gt[h:h + 1]
        g_slc = gt[NSA_HEADS + h:NSA_HEADS + h + 1] * inv_slc[:, g * half + c0:g * half + c1]
        g_win = gt[2 * NSA_HEADS + h:2 * NSA_HEADS + h + 1] * inv_win[:, g * half + c0:g * half + c1]
        outs.append(o_cmp[g][:, c0:c1] * g_cmp + acc_ref[g, :, c0:c1] * g_slc + o_win[g][:, c0:c1] * g_win)
    o_ref[...] = jnp.concatenate(outs, axis=0).T


def _nsa_prompt_attention(qr, small, ck, cv, ksr, vs, kwr, vw):
    l = qr.shape[0]
    ns = l // SEL_BLOCK
    nc = 2 * ns
    bf16 = jnp.bfloat16
    perm = jnp.concatenate([jnp.arange(0, nc, 2), jnp.arange(1, nc, 2)])
    ckp = ck[perm].astype(bf16)
    cvt = cv[perm].T.astype(bf16)
    pad = jnp.zeros((NSA_WINDOW, NSA_KV_W), bf16)
    kwp = jnp.concatenate([pad, kwr.astype(bf16)], axis=0)
    vwt = jnp.concatenate([pad, vw.astype(bf16)], axis=0).T
    tri = (jnp.arange(ns)[:, None] >= jnp.arange(ns)[None, :]).astype(bf16)
    full = lambda a: pl.BlockSpec(a.shape, lambda i: (0,) * a.ndim)
    args = (qr, small, ckp, cvt, ksr.astype(bf16), vs.T.astype(bf16), kwp, vwt, tri)
    return pl.pallas_call(
        functools.partial(_nsa_prompt_kernel, ns=ns),
        grid=(l // NSA_QBLOCK,),
        in_specs=[pl.BlockSpec((NSA_QBLOCK, NSA_Q_W), lambda i: (i, 0)),
                  pl.BlockSpec((NSA_QBLOCK, LANES), lambda i: (i, 0))] + [full(a) for a in args[2:]],
        out_specs=pl.BlockSpec((NSA_QBLOCK, NSA_Q_W), lambda i: (i, 0)),
        out_shape=jax.ShapeDtypeStruct((l, NSA_Q_W), jnp.float32),
        scratch_shapes=[pltpu.VMEM((NSA_KV_HEADS, ns, NSA_QBLOCK), jnp.float32),
                        pltpu.VMEM((1, NSA_COLS), jnp.float32),
                        pltpu.VMEM((1, NSA_COLS), jnp.float32),
                        pltpu.VMEM((NSA_KV_HEADS, HEAD_DIM, NSA_COLS // 2), jnp.float32)],
        compiler_params=pltpu.CompilerParams(dimension_semantics=("arbitrary",),
                                             vmem_limit_bytes=VMEM_LIMIT_BYTES),
        name="nsa_prompt",
    )(*args)


PAGE_W = 2 * NSA_KV_W
CMP_HIDDEN = 2 * HEAD_DIM
CMP_PER_PAGE = PAGE_SIZE // CMP_BLOCK
CMP_FLAT = CMP_BLOCK * PAGE_W
CMP_PAGES_PER_STEP = 32
STEP_PAGES = 64
STEP_ROWS = 16


def _rope_tables(pos, width):
    half = HEAD_DIM // 2
    inv_freq = ROPE_THETA ** (-2.0 * jnp.arange(half, dtype=jnp.float32) / HEAD_DIM)
    ang = pos.astype(jnp.float32)[:, None] * inv_freq[None, :]
    cos, sin = jnp.cos(ang), jnp.sin(ang)
    reps = width // HEAD_DIM
    return (jnp.tile(jnp.concatenate([cos, cos], axis=1), (1, reps)),
            jnp.tile(jnp.concatenate([-sin, sin], axis=1), (1, reps)))


def _rope_lanes(x, cos, sin_signed):
    n = x.shape[-1]
    lane = lax.broadcasted_iota(jnp.int32, x.shape, x.ndim - 1)
    first = (lane % HEAD_DIM) < HEAD_DIM // 2
    partner = jnp.where(first, pltpu.roll(x, n - HEAD_DIM // 2, x.ndim - 1), pltpu.roll(x, HEAD_DIM // 2, x.ndim - 1))
    return x * cos + partner * sin_signed


def _cmp_step_kernel(pt_ref, *refs):
    npg = CMP_PAGES_PER_STEP
    pages = refs[:npg]
    pe_ref, w1_ref, b1_ref, w2_ref, cos_ref, sin_ref, ck_ref, cv_ref = refs[npg:]
    x = jnp.concatenate([r[0] for r in pages], axis=0) + pe_ref[...]
    h = jnp.dot(x.astype(jnp.bfloat16), w1_ref[...], preferred_element_type=jnp.float32) + b1_ref[...]
    c = jnp.dot(jax.nn.gelu(h).astype(jnp.bfloat16), w2_ref[...], preferred_element_type=jnp.float32)
    ck_ref[0] = _rope_lanes(c[:, :NSA_KV_W], cos_ref[...], sin_ref[...])
    cv_ref[0] = c[:, NSA_KV_W:]


def _compress_weights(cw1, cb1, cw2, cpe):
    eye = jnp.eye(2, dtype=jnp.float32)
    w1r = cw1.reshape(2, CMP_BLOCK, HEAD_DIM, CMP_HIDDEN)
    w1 = jnp.einsum('ktdj,ka,gb->tkgdabj', w1r, eye, eye).reshape(CMP_FLAT, 4 * CMP_HIDDEN)
    b1 = jnp.broadcast_to(cb1[:, None, :], (2, 2, CMP_HIDDEN)).reshape(1, 4 * CMP_HIDDEN)
    w2 = jnp.einsum('kjd,ka,gb->kgjabd', cw2, eye, eye).reshape(4 * CMP_HIDDEN, PAGE_W)
    pe = jnp.broadcast_to(cpe.transpose(1, 0, 2)[:, :, None, :], (CMP_BLOCK, 2, 2, HEAD_DIM)).reshape(1, CMP_FLAT)
    return w1.astype(jnp.bfloat16), b1, w2.astype(jnp.bfloat16), pe


def _nsa_sample_compress(pool, layer_idx, page_table, cw1, cb1, cw2, cpe):
    n_pool, nl = pool.shape[:2]
    db, n_pages = page_table.shape
    npg = CMP_PAGES_PER_STEP
    nchunk = n_pages // npg
    nc = n_pages * CMP_PER_PAGE
    view = pool.reshape(n_pool * nl, CMP_PER_PAGE, CMP_FLAT)
    pt = (page_table * nl + layer_idx).reshape(-1).astype(jnp.int32)
    w1, b1, w2, pe = _compress_weights(cw1, cb1, cw2, cpe)
    cos, sin = _rope_tables((jnp.arange(nc) + 1) * CMP_BLOCK - 1, NSA_KV_W)
    rows = npg * CMP_PER_PAGE

    def page_map(k):
        return lambda b, c, pt_ref: (pt_ref[b * n_pages + c * npg + k], 0, 0)

    const = lambda a: pl.BlockSpec(a.shape, lambda b, c, pt_ref: (0,) * a.ndim, pipeline_mode=pl.Buffered(1))
    grid_spec = pltpu.PrefetchScalarGridSpec(
        num_scalar_prefetch=1, grid=(db, nchunk),
        in_specs=[pl.BlockSpec((1, CMP_PER_PAGE, CMP_FLAT), page_map(k)) for k in range(npg)]
        + [const(pe), const(w1), const(b1), const(w2),
           pl.BlockSpec((rows, NSA_KV_W), lambda b, c, pt_ref: (c, 0)),
           pl.BlockSpec((rows, NSA_KV_W), lambda b, c, pt_ref: (c, 0))],
        out_specs=[pl.BlockSpec((1, rows, NSA_KV_W), lambda b, c, pt_ref: (b, c, 0))] * 2)
    return pl.pallas_call(
        _cmp_step_kernel, grid_spec=grid_spec,
        out_shape=[jax.ShapeDtypeStruct((db, nc, NSA_KV_W), jnp.float32)] * 2,
        compiler_params=pltpu.CompilerParams(dimension_semantics=("arbitrary", "arbitrary"),
                                             vmem_limit_bytes=VMEM_LIMIT_BYTES),
        name="nsa_sample_compress",
    )(pt, *([view] * npg), pe, w1, b1, w2, cos, sin)


def _nt_dot(a, b):
    return lax.dot_general(a, b, (((1,), (1,)), ((), ())), preferred_element_type=jnp.float32)


def _nsa_step_kernel(pt_ref, *refs, cur, nsl):
    f32, bf16 = jnp.float32, jnp.bfloat16
    npg = STEP_PAGES
    q_ref, ck_ref, cv_ref = refs[:3]
    pages = refs[3:3 + npg]
    (win_ref, new_ref, gate_ref, exp_ref, triu_ref, o_ref, wout_ref,
     selt_ref, m_ref, l_ref, acc_ref, side_ref) = refs[3 + npg:]
    cc = pl.program_id(1)
    q16 = q_ref[0]
    qb = q16.astype(bf16)
    row16 = lax.broadcasted_iota(jnp.int32, (STEP_ROWS, 1), 0)

    def new_key_scores(krow):
        return jnp.sum(q16 * krow, axis=1, keepdims=True)

    @pl.when(cc == 0)
    def _():
        nch = ck_ref.shape[1] // 2
        halves = lambda r: jnp.concatenate([r[0, pl.ds(0, nch, stride=2), :], r[0, pl.ds(1, nch, stride=2), :]], axis=0)
        ck, cv = halves(ck_ref), halves(cv_ref)
        s = _nt_dot(qb, ck.astype(bf16))
        p = jnp.exp(s - jnp.max(s, axis=1, keepdims=True))
        pn = p / jnp.sum(p, axis=1, keepdims=True)
        o_cmp = jnp.dot(pn.astype(bf16), cv.astype(bf16), preferred_element_type=f32)

        rowp = lax.broadcasted_iota(jnp.int32, pn.shape, 0)
        row8 = lax.broadcasted_iota(jnp.int32, (8, nsl), 0)
        blk = lax.broadcasted_iota(jnp.int32, (8, nsl), 1)
        forced = (blk == 0) | (blk == cur) | (blk == cur - 1)
        key = jnp.full((8, nsl), -1, jnp.int32)
        for g in range(NSA_KV_HEADS):
            ig = jnp.sum(jnp.where((rowp >= g * NSA_GROUP) & (rowp < (g + 1) * NSA_GROUP), pn, 0.0),
                         axis=0, keepdims=True)
            ig = ig[:, :nch] + ig[:, nch:]
            ig = jnp.concatenate([ig, jnp.zeros((1, nsl - nch), f32)], axis=1)
            kg = jnp.where(blk[:1] > cur, -1,
                           jnp.where(forced[:1], lax.bitcast_convert_type(jnp.full((1, nsl), NSA_FORCE, f32), jnp.int32),
                                     lax.bitcast_convert_type(ig, jnp.int32)))
            key = jnp.where(row8 == g, kg, key)
        thr = jnp.zeros((8, 1), jnp.int32)
        for bit in range(30, -1, -1):
            cand = thr | (1 << bit)
            cnt = jnp.sum(jnp.where(key >= cand, 1.0, 0.0), axis=1, keepdims=True)
            thr = jnp.where(cnt >= NSA_TOPN, cand, thr)
        above = key > thr
        n_above = jnp.sum(jnp.where(above, 1.0, 0.0), axis=1, keepdims=True)
        tie = key == thr
        rank = jnp.dot(jnp.where(tie, 1.0, 0.0).astype(bf16), triu_ref[...], preferred_element_type=f32)
        sel = jnp.where(above | (tie & (rank <= NSA_TOPN - n_above)), 1.0, 0.0)
        selh = jnp.where(row16 < NSA_GROUP, sel[0:1], jnp.where(row16 < NSA_HEADS, sel[1:2], 0.0))
        for j in range(selt_ref.shape[0]):
            selt_ref[j] = selh[:, j * LANES:(j + 1) * LANES]

        m_ref[...] = new_key_scores(new_ref[0, 0:1, :])
        l_ref[...] = jnp.ones(l_ref.shape, f32)
        acc_ref[...] = jnp.broadcast_to(new_ref[0, 1:2, :], acc_ref.shape)

        win = win_ref[0]
        sw = _nt_dot(qb, win[:, :NSA_KV_W].astype(bf16))
        sn = new_key_scores(new_ref[0, 2:3, :])
        mw = jnp.maximum(jnp.max(sw, axis=1, keepdims=True), sn)
        pw, pnw = jnp.exp(sw - mw), jnp.exp(sn - mw)
        lw = jnp.sum(pw, axis=1, keepdims=True) + pnw
        o_win = (jnp.dot(pw.astype(bf16), win[:, NSA_KV_W:].astype(bf16), preferred_element_type=f32)
                 + pnw * new_ref[0, 3:4, :]) / lw
        gt = jax.nn.sigmoid(gate_ref[0])
        side_ref[...] = gt[:, 0:1] * o_cmp + gt[:, 2:3] * o_win
        rw = lax.broadcasted_iota(jnp.int32, win.shape, 0)
        newrow = jnp.concatenate([new_ref[0, 2:3, :], new_ref[0, 3:4, :]], axis=1)
        wout_ref[0] = jnp.where(rw == win.shape[0] - 1, newrow, pltpu.roll(win, win.shape[0] - 1, 0))

    kv = jnp.concatenate([r[0] for r in pages], axis=0)
    s = _nt_dot(qb, kv[:, :NSA_KV_W].astype(bf16))
    picked = jnp.dot(selt_ref[cc].astype(bf16), exp_ref[...], preferred_element_type=f32)
    s = s + (picked - 1.0) * (-NEG)
    m_old = m_ref[...]
    m_new = jnp.maximum(m_old, jnp.max(s, axis=1, keepdims=True))
    alpha = jnp.exp(m_old - m_new)
    p = jnp.exp(s - m_new)
    m_ref[...] = m_new
    l_ref[...] = l_ref[...] * alpha + jnp.sum(p, axis=1, keepdims=True)
    acc_ref[...] = acc_ref[...] * alpha + jnp.dot(p.astype(bf16), kv[:, NSA_KV_W:].astype(bf16),
                                                  preferred_element_type=f32)

    @pl.when(cc == pl.num_programs(1) - 1)
    def _():
        gt = jax.nn.sigmoid(gate_ref[0])
        o_ref[0] = side_ref[...] + gt[:, 1:2] * acc_ref[...] / l_ref[...]


def _nsa_sample_attention(qr, ck, cv, slc_pool, layer_idx, page_table, win_buf, newrows, gate):
    db, n_pages = page_table.shape
    n_pool, nl = slc_pool.shape[:2]
    past = n_pages * PAGE_SIZE
    cur = past // SEL_BLOCK
    nsl = -(-(cur + 1) // LANES) * LANES
    npg = STEP_PAGES
    nchunk = n_pages // npg
    keys = npg * PAGE_SIZE
    view = slc_pool.reshape(n_pool * nl, PAGE_SIZE, PAGE_W)
    pt = (page_table * nl + layer_idx).reshape(-1).astype(jnp.int32)
    f32 = jnp.float32
    hmask = (jnp.arange(NSA_HEADS)[:, None] // NSA_GROUP == jnp.arange(NSA_KV_HEADS)[None, :]).astype(f32)
    q16 = (qr * HEAD_DIM ** -0.5)[:, :, None, :] * hmask[None, :, :, None]
    q16 = jnp.pad(q16.reshape(db, NSA_HEADS, NSA_KV_W), ((0, 0), (0, STEP_ROWS - NSA_HEADS), (0, 0)))
    new8 = jnp.pad(newrows, ((0, 0), (0, 8 - newrows.shape[1]), (0, 0)))
    g16 = jnp.pad(gate.reshape(db, 3, NSA_HEADS).transpose(0, 2, 1),
                  ((0, 0), (0, STEP_ROWS - NSA_HEADS), (0, LANES - 3)))
    expand = (jnp.arange(LANES)[:, None] == jnp.arange(keys)[None, :] // SEL_BLOCK).astype(jnp.bfloat16)
    triu = (jnp.arange(nsl)[:, None] <= jnp.arange(nsl)[None, :]).astype(jnp.bfloat16)
    wlen = win_buf.shape[1]

    def page_map(k):
        return lambda b, c, pt_ref: (pt_ref[b * n_pages + c * npg + k], 0, 0)

    per_b = lambda shp: pl.BlockSpec((1,) + shp, lambda b, c, pt_ref: (b, 0, 0))
    const = lambda a: pl.BlockSpec(a.shape, lambda b, c, pt_ref: (0,) * a.ndim)
    grid_spec = pltpu.PrefetchScalarGridSpec(
        num_scalar_prefetch=1, grid=(db, nchunk),
        in_specs=[per_b((STEP_ROWS, NSA_KV_W)), per_b(ck.shape[1:]), per_b(cv.shape[1:])]
        + [pl.BlockSpec((1, PAGE_SIZE, PAGE_W), page_map(k)) for k in range(npg)]
        + [per_b((wlen, PAGE_W)), per_b((8, NSA_KV_W)), per_b((STEP_ROWS, LANES)), const(expand), const(triu)],
        out_specs=[per_b((STEP_ROWS, NSA_KV_W)), per_b((wlen, PAGE_W))],
        scratch_shapes=[pltpu.VMEM((nsl // LANES, STEP_ROWS, LANES), f32),
                        pltpu.VMEM((STEP_ROWS, 1), f32), pltpu.VMEM((STEP_ROWS, 1), f32),
                        pltpu.VMEM((STEP_ROWS, NSA_KV_W), f32), pltpu.VMEM((STEP_ROWS, NSA_KV_W), f32)])
    o16, wout = pl.pallas_call(
        functools.partial(_nsa_step_kernel, cur=cur, nsl=nsl), grid_spec=grid_spec,
        out_shape=[jax.ShapeDtypeStruct((db, STEP_ROWS, NSA_KV_W), f32),
                   jax.ShapeDtypeStruct((db, wlen, PAGE_W), f32)],
        compiler_params=pltpu.CompilerParams(dimension_semantics=("arbitrary", "arbitrary"),
                                             vmem_limit_bytes=VMEM_LIMIT_BYTES),
        name="nsa_sample_attention",
    )(pt, q16, ck, cv, *([view] * npg), win_buf, new8, g16, expand, triu)
    o = o16[:, :NSA_HEADS].reshape(db, NSA_HEADS, NSA_KV_HEADS, HEAD_DIM)
    o = jnp.take_along_axis(o, (jnp.arange(NSA_HEADS) // NSA_GROUP)[None, :, None, None], axis=2)
    return o.reshape(db, NSA_HEADS * HEAD_DIM), wout


def _split_cols(h, widths):
    parts, start = [], 0
    for w in widths:
        parts.append(h[..., start:start + w])
        start += w
    return parts


def _even_widths():
    return (NSA_Q_W,) + (NSA_KV_W,) * 6 + (3 * NSA_HEADS, 3 * GDN_W, GDN_HEADS, GDN_HEADS, GDN_W)


def _rms_norm(x, w):
    return x * lax.rsqrt(jnp.mean(jnp.square(x), axis=-1, keepdims=True) + NORM_EPS) * w


def _l2_norm(x):
    return x * lax.rsqrt(jnp.sum(jnp.square(x), axis=-1, keepdims=True) + NORM_EPS)


def _rope(x, pos):
    half = HEAD_DIM // 2
    inv_freq = ROPE_THETA ** (-2.0 * jnp.arange(half, dtype=jnp.float32) / HEAD_DIM)
    ang = pos.astype(jnp.float32)[:, None] * inv_freq[None, :]
    cos, sin = jnp.cos(ang)[:, None, :], jnp.sin(ang)[:, None, :]
    xf = x.astype(jnp.float32)
    x1, x2 = xf[..., :half], xf[..., half:]
    return jnp.concatenate([x1 * cos - x2 * sin, x2 * cos + x1 * sin], axis=-1)


def _causal_dwconv(hist, u, w):
    width, s = w.shape[0], u.shape[1]
    ext = jnp.concatenate([hist.astype(u.dtype), u], axis=1)
    out = w[0] * ext[:, :s]
    for j in range(1, width):
        out = out + w[j] * ext[:, j:j + s]
    return out, ext[:, s:]


def _masked_softmax(s, mask):
    s = jnp.where(mask, s, -jnp.inf)
    m = jnp.max(s, axis=-1, keepdims=True)
    m = jnp.where(jnp.isfinite(m), m, 0.0)
    e = jnp.where(mask, jnp.exp(s - m), 0.0)
    den = jnp.sum(e, axis=-1, keepdims=True)
    return e / jnp.where(den > 0.0, den, 1.0)


def _gather_pages(pool, page_table, layer_idx):
    rows = pool[page_table, layer_idx]
    return rows.reshape(rows.shape[0], -1, *rows.shape[3:])


def _nsa_compress(rows, w1, b1, w2, pe):
    b, l, g, dh = rows.shape
    nc = l // CMP_BLOCK
    blk = rows[:, :nc * CMP_BLOCK].astype(jnp.float32).reshape(b, nc, CMP_BLOCK, g, dh) + pe[:, None, :]
    flat = blk.transpose(0, 1, 3, 2, 4).reshape(b, nc, g, CMP_BLOCK * dh)
    return jax.nn.gelu(flat @ w1 + b1) @ w2


def _nsa_compressed_kv(k_rows, v_rows, cw1, cb1, cw2, cpe):
    ck = _nsa_compress(k_rows, cw1[0], cb1[0], cw2[0], cpe[0])
    cv = _nsa_compress(v_rows, cw1[1], cb1[1], cw2[1], cpe[1])
    nc = ck.shape[1]
    ck = _rope(ck, (jnp.arange(nc) + 1) * CMP_BLOCK - 1)
    return ck, cv


def _nsa_attend(q, qpos, ck, cv, sk, sv, wk, wv, wpos):
    b, nq = q.shape[:2]
    scale = HEAD_DIM ** -0.5
    nc, ns = ck.shape[1], sk.shape[2]
    cend = (jnp.arange(nc) + 1) * CMP_BLOCK - 1
    s = jnp.einsum('bqghd,bcgd->bghqc', q, ck) * scale
    p_cmp = _masked_softmax(s, cend[None, :] <= qpos[:, None])
    o_cmp = jnp.einsum('bghqc,bcgd->bqghd', p_cmp, cv)
    imp = jnp.sum(p_cmp, axis=2)
    imp = jnp.pad(imp, ((0, 0), (0, 0), (0, 0), (0, ns * CMP_PER_SEL - nc)))
    imp = imp.reshape(b, NSA_KV_HEADS, nq, ns, CMP_PER_SEL).sum(-1)
    blk = jnp.arange(ns)[None, :]
    cur = (qpos // SEL_BLOCK)[:, None]
    forced = (blk == 0) | (blk == cur) | (blk == cur - 1)
    imp = jnp.where(blk <= cur, jnp.where(forced, NSA_FORCE, imp), -1.0)
    _, idx = lax.top_k(imp, min(NSA_TOPN, ns))
    n = idx.shape[-1]
    pick = jax.vmap(jax.vmap(lambda kb, ix: kb[ix]))
    ksel = pick(sk, idx).reshape(b, NSA_KV_HEADS, nq, n * SEL_BLOCK, HEAD_DIM)
    vsel = pick(sv, idx).reshape(b, NSA_KV_HEADS, nq, n * SEL_BLOCK, HEAD_DIM)
    kpos = (idx[..., None] * SEL_BLOCK + jnp.arange(SEL_BLOCK)).reshape(b, NSA_KV_HEADS, nq, n * SEL_BLOCK)
    s = jnp.einsum('bqghd,bgqkd->bghqk', q, ksel) * scale
    p = _masked_softmax(s, (kpos <= qpos[:, None])[:, :, None])
    o_slc = jnp.einsum('bghqk,bgqkd->bqghd', p, vsel)
    dist = qpos[:, None] - wpos[None, :]
    wmask = (dist >= 0) & (dist <= NSA_WINDOW) & (wpos[None, :] >= 0)
    s = jnp.einsum('bqghd,bkgd->bghqk', q, wk) * scale
    p = _masked_softmax(s, wmask)
    o_win = jnp.einsum('bghqk,bkgd->bqghd', p, wv)
    return o_cmp, o_slc, o_win


def _nsa_prompt(q, kc, vc, ks, vs, kw, vw, gate, cw1, cb1, cw2, cpe):
    b, l = q.shape[:2]
    pos = jnp.arange(l)
    qr = _rope(q, pos)
    ck, cv = _nsa_compressed_kv(kc, vc, cw1, cb1, cw2, cpe)
    ksr = _rope(ks, pos)
    vsf = vs.astype(jnp.float32)
    kwr = _rope(kw, pos)
    vwf = vw.astype(jnp.float32)
    small = jnp.pad(gate.reshape(l, -1), ((0, 0), (0, LANES - gate.shape[-1])))
    flat = lambda t: t.reshape(t.shape[1], -1)
    o_nsa = _nsa_prompt_attention(flat(qr), small, flat(ck), flat(cv), flat(ksr), flat(vsf), flat(kwr), flat(vwf))
    keep = min(NSA_WINDOW, l)
    rows_cmp = jnp.stack([kc, vc], axis=2)
    rows_slc = jnp.stack([ksr, vsf], axis=2)
    rows_win = jnp.stack([kwr[:, l - keep:], vwf[:, l - keep:]], axis=2)
    return o_nsa[None], rows_cmp, rows_slc, rows_win


def _nsa_sample(q, kc, vc, ks, vs, kw, vw, gate, cmp_pool, slc_pool, layer_idx, win_buf, page_table,
                cw1, cb1, cw2, cpe):
    db, s = q.shape[:2]
    past = page_table.shape[1] * PAGE_SIZE
    wb = win_buf.shape[1]
    assert s == 1 and wb == NSA_WINDOW and past >= wb and past % (STEP_PAGES * PAGE_SIZE) == 0
    qpos = past + jnp.arange(s)
    qr = _rope(q, qpos)
    ck, cv = _nsa_sample_compress(cmp_pool, layer_idx, page_table, cw1, cb1, cw2, cpe)
    ksr = _rope(ks, qpos)
    vsf = vs.astype(jnp.float32)
    kwr = _rope(kw, qpos)
    vwf = vw.astype(jnp.float32)
    newrows = jnp.stack([t.reshape(db, NSA_KV_W) for t in (ksr, vsf, kwr, vwf)], axis=1)
    o_nsa, wout = _nsa_sample_attention(qr[:, 0], ck, cv, slc_pool, layer_idx, page_table,
                                        win_buf.reshape(db, wb, PAGE_W), newrows, gate.reshape(db, -1))
    rows_cmp = jnp.stack([kc, vc], axis=2)
    rows_slc = jnp.stack([ksr, vsf], axis=2)
    rows_win = wout.reshape(db, wb, 2, NSA_KV_HEADS, HEAD_DIM)
    return o_nsa[:, None], rows_cmp, rows_slc, rows_win


def _gdn_chunked(q, k, v, g, beta, s0):
    b, l, h, dk = q.shape
    dv = v.shape[-1]
    c = GDN_CHUNK
    nch = l // c
    r = lambda a: jnp.moveaxis(a.reshape(b, nch, c, h, *a.shape[3:]), 3, 2)
    q, k, v, g, beta = r(q), r(k), r(v), r(g), r(beta)
    gc = jnp.cumsum(g, axis=-1)
    ii = jnp.arange(c)
    tri = ii[:, None] >= ii[None, :]
    strict = ii[:, None] > ii[None, :]
    diff = gc[..., :, None] - gc[..., None, :]
    gamma = jnp.where(tri, jnp.exp(jnp.where(tri, diff, 0.0)), 0.0)
    kb = k * beta[..., None]
    a_mat = jnp.where(strict, jnp.einsum('bnhik,bnhjk->bnhij', kb, k) * gamma, 0.0)
    eye = jnp.eye(c, dtype=jnp.float32)
    t_inv = lax.linalg.triangular_solve(eye + a_mat, jnp.broadcast_to(eye, a_mat.shape),
                                        left_side=True, lower=True, unit_diagonal=True)
    u = t_inv @ (v * beta[..., None])
    w = t_inv @ (kb * jnp.exp(gc)[..., None])
    qk = jnp.where(tri, jnp.einsum('bnhik,bnhjk->bnhij', q, k) * gamma, 0.0)
    qg = q * jnp.exp(gc)[..., None]
    kd = k * jnp.exp(gc[..., -1:] - gc)[..., None]
    glast = jnp.exp(gc[..., -1])

    def step(state, xs):
        qg_c, kd_c, u_c, w_c, qk_c, gl_c = xs
        v_new = u_c - jnp.einsum('bhck,bhkv->bhcv', w_c, state)
        o = jnp.einsum('bhck,bhkv->bhcv', qg_c, state) + jnp.einsum('bhij,bhjv->bhiv', qk_c, v_new)
        state = state * gl_c[..., None, None] + jnp.einsum('bhck,bhcv->bhkv', kd_c, v_new)
        return state, o

    xs = tuple(jnp.moveaxis(a, 1, 0) for a in (qg, kd, u, w, qk, glast))
    s_fin, o = lax.scan(step, s0, xs)
    o = jnp.moveaxis(jnp.moveaxis(o, 0, 1), 2, 3).reshape(b, l, h, dv)
    return o, s_fin


def _gdn_recurrent(q, k, v, g, beta, s0):
    def step(state, xs):
        q_t, k_t, v_t, g_t, b_t = xs
        state = state * jnp.exp(g_t)[..., None, None]
        v_t = (v_t - jnp.einsum('bhk,bhkv->bhv', k_t, state)) * b_t[..., None]
        state = state + jnp.einsum('bhk,bhv->bhkv', k_t, v_t)
        return state, jnp.einsum('bhk,bhkv->bhv', q_t, state)

    xs = tuple(jnp.moveaxis(a, 1, 0) for a in (q, k, v, g, beta))
    s_fin, o = lax.scan(step, s0, xs)
    return jnp.moveaxis(o, 0, 1), s_fin


def _gdn_mix(qkv, a, bt, z, conv_hist, s0, conv_w, a_log, dt_bias, norm_w, chunked):
    b, s = qkv.shape[:2]
    c, new_hist = _causal_dwconv(conv_hist, qkv, conv_w)
    c = jax.nn.silu(c.astype(jnp.float32))
    q, k, v = [t.reshape(b, s, GDN_HEADS, HEAD_DIM) for t in jnp.split(c, 3, axis=-1)]
    q = _l2_norm(q) * HEAD_DIM ** -0.5
    k = _l2_norm(k)
    beta = jax.nn.sigmoid(bt.astype(jnp.float32))
    g = -jnp.exp(a_log) * jax.nn.softplus(a.astype(jnp.float32) + dt_bias)
    s0 = s0.astype(jnp.float32)
    if chunked:
        o, s_fin = _gdn_chunked(q, k, v, g, beta, s0)
    else:
        o, s_fin = _gdn_recurrent(q, k, v, g, beta, s0)
    o = _rms_norm(o, norm_w) * jax.nn.silu(z.astype(jnp.float32).reshape(b, s, GDN_HEADS, HEAD_DIM))
    return o.reshape(b, s, GDN_W), new_hist, s_fin


def _even_merge(o_cmp, o_slc, o_win, gate, o_gdn):
    b, s = gate.shape[:2]
    gt = jax.nn.sigmoid(gate.astype(jnp.float32)).reshape(b, s, 3, NSA_HEADS, 1)
    o_nsa = gt[:, :, 0] * o_cmp + gt[:, :, 1] * o_slc + gt[:, :, 2] * o_win
    return jnp.concatenate([o_nsa.reshape(b, s, NSA_Q_W), o_gdn], axis=-1)


def _proj(x, w):
    b, s, d = x.shape
    n = w.shape[1]
    npad = -(-n // LANES) * LANES
    wp = jnp.pad(w, ((0, 0), (0, npad - n)))
    return _matmul(x.reshape(b * s, d), wp)[:, :n].reshape(b, s, n)


def _even_prompt(x, w_in, cw1, cb1, cw2, cpe, conv_w, a_log, dt_bias, norm_w):
    b, l, _ = x.shape
    q, kc, vc, ks, vs, kw, vw, gate, qkv, a, bt, z = _split_cols(_proj(x, w_in), _even_widths())
    heads = lambda t: t.reshape(b, l, -1, HEAD_DIM)
    o_nsa, r_cmp, r_slc, r_win = _nsa_prompt(
        heads(q), heads(kc), heads(vc), heads(ks), heads(vs), heads(kw), heads(vw), gate, cw1, cb1, cw2, cpe)
    hist0 = jnp.zeros((b, GDN_CONV - 1, 3 * GDN_W), qkv.dtype)
    s0 = jnp.zeros((b, GDN_HEADS, HEAD_DIM, HEAD_DIM), jnp.float32)
    o_gdn, conv_hist, s_fin = _gdn_mix(qkv, a, bt, z, hist0, s0, conv_w, a_log, dt_bias, norm_w, True)
    return jnp.concatenate([o_nsa, o_gdn], axis=-1), r_cmp, r_slc, r_win, conv_hist, s_fin


def _even_sample(x, cmp_pool, slc_pool, layer_idx, win_buf, conv_hist, s0, page_table,
                 w_in, cw1, cb1, cw2, cpe, conv_w, a_log, dt_bias, norm_w):
    b, s, _ = x.shape
    q, kc, vc, ks, vs, kw, vw, gate, qkv, a, bt, z = _split_cols(_proj(x, w_in), _even_widths())
    heads = lambda t: t.reshape(b, s, -1, HEAD_DIM)
    o_nsa, r_cmp, r_slc, r_win = _nsa_sample(
        heads(q), heads(kc), heads(vc), heads(ks), heads(vs), heads(kw), heads(vw), gate,
        cmp_pool, slc_pool, layer_idx, win_buf, page_table, cw1, cb1, cw2, cpe)
    o_gdn, new_hist, s_fin = _gdn_mix(qkv, a, bt, z, conv_hist, s0, conv_w, a_log, dt_bias, norm_w, False)
    return jnp.concatenate([o_nsa, o_gdn], axis=-1), r_cmp, r_slc, r_win, new_hist, s_fin


def _dilated_band_stats(q, k, v, d):
    b, l, h, dh = q.shape
    unit = d * DIL_BLOCK
    lp = -(-l // unit) * unit
    nb = lp // unit
    to_sub = lambda a: jnp.pad(a, ((0, 0), (0, lp - l), (0, 0), (0, 0))).reshape(b, nb, DIL_BLOCK, d, h, dh)
    qs, ks, vs = to_sub(q), to_sub(k), to_sub(v)
    prev = lambda a: jnp.concatenate([jnp.zeros_like(a[:, :1]), a[:, :-1]], axis=1)
    kk = jnp.concatenate([prev(ks), ks], axis=2)
    vv = jnp.concatenate([prev(vs), vs], axis=2)
    s = jnp.einsum('bnirhd,bnjrhd->bnrhij', qs, kk) * HEAD_DIM ** -0.5
    i = jnp.arange(DIL_BLOCK)
    j = jnp.arange(2 * DIL_BLOCK) - DIL_BLOCK
    dist = i[:, None] - j[None, :]
    sub_k = jnp.arange(nb)[:, None, None] * DIL_BLOCK + j[None, None, :]
    mask = (dist >= 0)[None] & (dist <= DIL_SPAN)[None] & (sub_k >= 0)
    s = jnp.where(mask[None, :, None, None], s, -jnp.inf)
    m = jnp.max(s, axis=-1)
    e = jnp.exp(s - m[..., None])
    den = jnp.sum(e, axis=-1)
    num = jnp.einsum('bnrhij,bnjrhd->bnrhid', e, vv)
    m = jnp.transpose(m, (0, 1, 4, 2, 3)).reshape(b, lp, h)[:, :l]
    den = jnp.transpose(den, (0, 1, 4, 2, 3)).reshape(b, lp, h)[:, :l]
    num = jnp.transpose(num, (0, 1, 4, 2, 3, 5)).reshape(b, lp, h, dh)[:, :l]
    return m, den, num


def _dilated_gather_stats(q, k_all, v_all, qpos, kpos0, d):
    kp = qpos[:, None] - jnp.arange(DIL_SPAN + 1)[None, :] * d
    idx = kp - kpos0
    valid = idx >= 0
    idxc = jnp.clip(idx, 0)
    kg, vg = k_all[:, idxc], v_all[:, idxc]
    s = jnp.einsum('bshd,bsmhd->bshm', q, kg) * HEAD_DIM ** -0.5
    s = jnp.where(valid[None, :, None, :], s, -jnp.inf)
    m = jnp.max(s, axis=-1)
    e = jnp.exp(s - m[..., None])
    return m, jnp.sum(e, axis=-1), jnp.einsum('bshm,bsmhd->bshd', e, vg)


def _combine_by_denominators(stats):
    m_all = stats[0][0]
    for m, _, _ in stats[1:]:
        m_all = jnp.maximum(m_all, m)
    num, den = None, None
    for m, dn, nm in stats:
        w = jnp.exp(m - m_all)
        num = w[..., None] * nm if num is None else num + w[..., None] * nm
        den = w * dn if den is None else den + w * dn
    return num / den[..., None]


def _dil_prompt(x, w_in):
    b, l, _ = x.shape
    q, k, v = [t.reshape(b, l, DIL_HEADS, HEAD_DIM) for t in jnp.split(_proj(x, w_in), 3, axis=-1)]
    pos = jnp.arange(l)
    qr, kr, vf = _rope(q, pos), _rope(k, pos), v.astype(jnp.float32)
    o = _combine_by_denominators([_dilated_band_stats(qr, kr, vf, d) for _, d in DIL_GROUPS])
    keep = min(DIL_MAX_WINDOW, l)
    buf = jnp.stack([kr[:, l - keep:], vf[:, l - keep:]], axis=2)
    return o.reshape(b, l, DIL_W), buf


def _dil_sample(x, buf, past, w_in):
    db, s, _ = x.shape
    q, k, v = [t.reshape(db, s, DIL_HEADS, HEAD_DIM) for t in jnp.split(_proj(x, w_in), 3, axis=-1)]
    qpos = past + jnp.arange(s)
    qr, kr = _rope(q, qpos), _rope(k, qpos)
    wb = buf.shape[1]
    k_all = jnp.concatenate([buf[:, :, 0].astype(jnp.float32), kr], axis=1)
    v_all = jnp.concatenate([buf[:, :, 1].astype(jnp.float32), v.astype(jnp.float32)], axis=1)
    o = _combine_by_denominators(
        [_dilated_gather_stats(qr, k_all, v_all, qpos, past - wb, d) for _, d in DIL_GROUPS])
    keep = min(DIL_MAX_WINDOW, past + s)
    new_buf = jnp.stack([k_all[:, wb + s - keep:], v_all[:, wb + s - keep:]], axis=2)
    return o.reshape(db, s, DIL_W), new_buf


def kernel(x_prompt, x_sample, cache_nsa_cmp_kv, cache_nsa_slc_kv, state_nsa_win_kv, state_gdn_conv,
           state_gdn_S, state_dil_kv, state_ffn_conv, page_table, w_in_a, nsa_cmp_w1, nsa_cmp_b1, nsa_cmp_w2,
           nsa_cmp_pe, gdn_conv_w, gdn_A_log, gdn_dt_bias, gdn_norm_w, w_out_a, w_in_c, w_out_c,
           ln_mix_g, ln_mix_b, ffn_w_in, ffn_conv_w, ffn_conv_b, ffn_w_out, ln_ffn_g, ln_ffn_b):
    past = page_table.shape[1] * PAGE_SIZE
    bp, lp, d = x_prompt.shape
    bs, ls, _ = x_sample.shape
    assert bp == 1 and ls == 1
    xp, xs = x_prompt, x_sample
    cmp_p, cmp_s, slc_p, slc_s, win_p, win_s = [], [], [], [], [], []
    gconv_p, gconv_s, gstate_p, gstate_s = [], [], [], []
    dil_p, dil_s, ffn_p, ffn_s = [], [], [], []
    for layer in range(DEPTH):
        if layer % 2 == 0:
            la = layer // 2
            wa = (w_in_a[la], nsa_cmp_w1[la], nsa_cmp_b1[la], nsa_cmp_w2[la], nsa_cmp_pe[la],
                  gdn_conv_w[la], gdn_A_log[la], gdn_dt_bias[la], gdn_norm_w[la])
            mp, rc, rs, rw, hc, hs_ = _even_prompt(xp, *wa)
            cmp_p.append(rc); slc_p.append(rs); win_p.append(rw); gconv_p.append(hc); gstate_p.append(hs_)
            ms, rc, rs, rw, hc, hs_ = _even_sample(xs, cache_nsa_cmp_kv, cache_nsa_slc_kv, la,
                                                   state_nsa_win_kv[:, la], state_gdn_conv[:, la],
                                                   state_gdn_S[:, la], page_table, *wa)
            cmp_s.append(rc); slc_s.append(rs); win_s.append(rw); gconv_s.append(hc); gstate_s.append(hs_)
            w_out = w_out_a[la]
        else:
            lc = layer // 2
            mp, bpf = _dil_prompt(xp, w_in_c[lc])
            ms, bsf = _dil_sample(xs, state_dil_kv[:, lc], past, w_in_c[lc])
            dil_p.append(bpf); dil_s.append(bsf)
            w_out = w_out_c[lc]
        xp2 = _matmul_ln(mp.reshape(lp, -1), w_out, xp.reshape(lp, d), ln_mix_g[layer], ln_mix_b[layer])
        xs2 = _matmul_ln(ms.reshape(bs, -1), w_out, xs.reshape(bs, d), ln_mix_g[layer], ln_mix_b[layer])
        fargs = (ffn_w_in[layer], ffn_conv_w[layer], ffn_conv_b[layer], ffn_w_out[layer],
                 ln_ffn_g[layer], ln_ffn_b[layer])
        xp3, hp = _ffn_seq(xp2, *fargs)
        xs3, hs = _ffn_step(xs2, state_ffn_conv[:, layer], *fargs)
        xp, xs = xp3.reshape(1, lp, d), xs3.reshape(bs, 1, d)
        ffn_p.append(hp[None]); ffn_s.append(hs)

    def stk(lst):
        return jnp.stack(lst, axis=1)

    return (xp, xs, stk(cmp_p), stk(cmp_s), stk(slc_p), stk(slc_s), stk(win_p), stk(win_s),
            stk(gconv_p), stk(gconv_s), stk(gstate_p), stk(gstate_s), stk(dil_p), stk(dil_s),
            stk(ffn_p), stk(ffn_s))
```

```python
import functools
import math

import jax
import jax.numpy as jnp
from jax import lax
from jax.experimental import pallas as pl
from jax.experimental.pallas import tpu as pltpu
import numpy as np

D_MODEL = 1024
DEPTH = 2
PAGE_SIZE = 128
HEAD_DIM = 64
ROPE_THETA = 10000.0
NSA_HEADS = 8
NSA_KV_HEADS = 2
NSA_GROUP = NSA_HEADS // NSA_KV_HEADS
CMP_BLOCK = 32
SEL_BLOCK = 64
CMP_PER_SEL = SEL_BLOCK // CMP_BLOCK
NSA_TOPN = 16
NSA_WINDOW = 512
NSA_QBLOCK = 128
NSA_FORCE = 1.0e4
GDN_HEADS = 8
GDN_CONV = 4
GDN_CHUNK = 64
DIL_HEADS = 16
DIL_GROUPS = ((128, 1), (512, 4), (2048, 16))
DIL_SPAN = 128
DIL_BLOCK = 128
DIL_MAX_WINDOW = 2048
D_FF = 2816
FFN_CONV = 3
DEEPNORM_ALPHA = (2.0 * DEPTH) ** 0.25
LN_EPS = 1e-5
NORM_EPS = 1e-6
NSA_Q_W = NSA_HEADS * HEAD_DIM
NSA_KV_W = NSA_KV_HEADS * HEAD_DIM
GDN_W = GDN_HEADS * HEAD_DIM
DIL_W = DIL_HEADS * HEAD_DIM

LANES = 128
VMEM_LIMIT_BYTES = 56 * 1024 * 1024


def _layer_norm_rows(r, g, b):
    mu = jnp.mean(r, axis=-1, keepdims=True)
    d = r - mu
    var = jnp.mean(d * d, axis=-1, keepdims=True)
    return d * lax.rsqrt(var + LN_EPS) * g + b


def _mm_kernel(x_ref, w_ref, o_ref):
    o_ref[...] = jnp.dot(x_ref[...].astype(jnp.bfloat16), w_ref[...], preferred_element_type=jnp.float32)


def _mm_ln_kernel(x_ref, w_ref, res_ref, g_ref, b_ref, o_ref):
    acc = jnp.dot(x_ref[...].astype(jnp.bfloat16), w_ref[...], preferred_element_type=jnp.float32)
    o_ref[...] = _layer_norm_rows(DEEPNORM_ALPHA * res_ref[...] + acc, g_ref[...], b_ref[...])


def _row_tile(m):
    return 512 if m % 512 == 0 else m


def _matmul(x, w):
    m, k = x.shape
    n = w.shape[1]
    tm = _row_tile(m)
    tn = n
    for cand in (1152, 1024, 768, 512):
        if n % cand == 0:
            tn = cand
            break
    return pl.pallas_call(
        _mm_kernel,
        grid=(m // tm, n // tn),
        in_specs=[pl.BlockSpec((tm, k), lambda i, j: (i, 0)),
                  pl.BlockSpec((k, tn), lambda i, j: (0, j))],
        out_specs=pl.BlockSpec((tm, tn), lambda i, j: (i, j)),
        out_shape=jax.ShapeDtypeStruct((m, n), jnp.float32),
        compiler_params=pltpu.CompilerParams(dimension_semantics=("parallel", "arbitrary"),
                                             vmem_limit_bytes=VMEM_LIMIT_BYTES),
        name="matmul",
    )(x, w.astype(jnp.bfloat16))


def _matmul_ln(x, w, res, g, b):
    m, k = x.shape
    n = w.shape[1]
    tm = _row_tile(m)
    return pl.pallas_call(
        _mm_ln_kernel,
        grid=(m // tm,),
        in_specs=[pl.BlockSpec((tm, k), lambda i: (i, 0)),
                  pl.BlockSpec((k, n), lambda i: (0, 0)),
                  pl.BlockSpec((tm, n), lambda i: (i, 0)),
                  pl.BlockSpec((1, n), lambda i: (0, 0)),
                  pl.BlockSpec((1, n), lambda i: (0, 0))],
        out_specs=pl.BlockSpec((tm, n), lambda i: (i, 0)),
        out_shape=jax.ShapeDtypeStruct((m, n), jnp.float32),
        compiler_params=pltpu.CompilerParams(dimension_semantics=("arbitrary",),
                                             vmem_limit_bytes=VMEM_LIMIT_BYTES),
        name="matmul_ln",
    )(x, w.astype(jnp.bfloat16), res, g.reshape(1, n), b.reshape(1, n))


FFN_CHUNK = D_FF // 2
FFN_NCHUNK = D_FF // FFN_CHUNK


def _ffn_seq_kernel(x_ref, wa_ref, wg_ref, cwa_ref, cwg_ref, cba_ref, cbg_ref, wo_ref, lg_ref, lb_ref,
                    y_ref, ha_ref, hg_ref, acc_ref, carry_ref):
    i, j = pl.program_id(0), pl.program_id(1)
    tm = x_ref.shape[0]
    x = x_ref[...]
    xb = x.astype(jnp.bfloat16)

    @pl.when(i == 0)
    def _():
        carry_ref[j] = jnp.zeros(carry_ref.shape[1:], jnp.float32)

    def conv(u, cw_ref, cb_ref, slot):
        prev = carry_ref[j, slot]
        p2, p1 = prev[6:7], prev[7:8]
        row = lax.broadcasted_iota(jnp.int32, u.shape, 0)
        u1 = jnp.where(row == 0, p1, pltpu.roll(u, 1, 0))
        u2 = jnp.where(row == 0, p2, jnp.where(row == 1, p1, pltpu.roll(u, 2, 0)))
        carry_ref[j, slot] = u[tm - 8:]
        cw = cw_ref[...]
        return cw[0:1] * u2 + cw[1:2] * u1 + cw[2:3] * u + cb_ref[...]

    ua = jnp.dot(xb, wa_ref[...], preferred_element_type=jnp.float32)
    ug = jnp.dot(xb, wg_ref[...], preferred_element_type=jnp.float32)
    ha_ref[...] = ua[tm - 8:]
    hg_ref[...] = ug[tm - 8:]
    a = conv(ua, cwa_ref, cba_ref, 0)
    g = conv(ug, cwg_ref, cbg_ref, 1)
    h = (a * jax.nn.sigmoid(a) * g).astype(jnp.bfloat16)
    part = jnp.dot(h, wo_ref[...], preferred_element_type=jnp.float32)

    @pl.when(j == 0)
    def _():
        acc_ref[...] = part

    @pl.when(j > 0)
    def _():
        acc_ref[...] += part

    @pl.when(j == pl.num_programs(1) - 1)
    def _():
        y_ref[...] = _layer_norm_rows(DEEPNORM_ALPHA * x + acc_ref[...], lg_ref[...], lb_ref[...])


def _ffn_seq(x, w_in, conv_w, conv_b, w_out, ln_g, ln_b):
    l, d = x.shape
    tm = 512
    c, nc = FFN_CHUNK, FFN_NCHUNK
    w_in = w_in.astype(jnp.bfloat16)
    cw8 = jnp.zeros((8, 2 * D_FF), jnp.float32).at[:FFN_CONV].set(conv_w)
    cb = conv_b.reshape(1, 2 * D_FF)
    y, ha, hg = pl.pallas_call(
        _ffn_seq_kernel,
        grid=(l // tm, nc),
        in_specs=[pl.BlockSpec((tm, d), lambda i, j: (i, 0)),
                  pl.BlockSpec((d, c), lambda i, j: (0, j)),
                  pl.BlockSpec((d, c), lambda i, j: (0, j + nc)),
                  pl.BlockSpec((8, c), lambda i, j: (0, j)),
                  pl.BlockSpec((8, c), lambda i, j: (0, j + nc)),
                  pl.BlockSpec((1, c), lambda i, j: (0, j)),
                  pl.BlockSpec((1, c), lambda i, j: (0, j + nc)),
                  pl.BlockSpec((c, d), lambda i, j: (j, 0)),
                  pl.BlockSpec((1, d), lambda i, j: (0, 0)),
                  pl.BlockSpec((1, d), lambda i, j: (0, 0))],
        out_specs=[pl.BlockSpec((tm, d), lambda i, j: (i, 0)),
                   pl.BlockSpec((8, c), lambda i, j: (i, j)),
                   pl.BlockSpec((8, c), lambda i, j: (i, j))],
        out_shape=[jax.ShapeDtypeStruct((l, d), jnp.float32),
                   jax.ShapeDtypeStruct((l // tm * 8, D_FF), jnp.float32),
                   jax.ShapeDtypeStruct((l // tm * 8, D_FF), jnp.float32)],
        scratch_shapes=[pltpu.VMEM((tm, d), jnp.float32),
                        pltpu.VMEM((nc, 2, 8, c), jnp.float32)],
        compiler_params=pltpu.CompilerParams(dimension_semantics=("arbitrary", "arbitrary"),
                                             vmem_limit_bytes=VMEM_LIMIT_BYTES),
        name="ffn_seq",
    )(x, w_in, w_in, cw8, cw8, cb, cb, w_out.astype(jnp.bfloat16), ln_g.reshape(1, d), ln_b.reshape(1, d))
    hist = jnp.concatenate([ha[-(FFN_CONV - 1):], hg[-(FFN_CONV - 1):]], axis=-1)
    return y, hist


def _ffn_step_kernel(x_ref, h_ref, wa_ref, wg_ref, cwa_ref, cwg_ref, cba_ref, cbg_ref, wo_ref, lg_ref, lb_ref,
                     y_ref, ua_ref, ug_ref, acc_ref):
    j = pl.program_id(0)
    x = x_ref[...]
    xb = x.astype(jnp.bfloat16)
    ua = jnp.dot(xb, wa_ref[...], preferred_element_type=jnp.float32)
    ug = jnp.dot(xb, wg_ref[...], preferred_element_type=jnp.float32)
    ua_ref[...] = ua
    ug_ref[...] = ug
    cwa, cwg = cwa_ref[...], cwg_ref[...]
    a = cwa[0:1] * h_ref[0, 0] + cwa[1:2] * h_ref[1, 0] + cwa[2:3] * ua + cba_ref[...]
    g = cwg[0:1] * h_ref[0, 1] + cwg[1:2] * h_ref[1, 1] + cwg[2:3] * ug + cbg_ref[...]
    h = (a * jax.nn.sigmoid(a) * g).astype(jnp.bfloat16)
    part = jnp.dot(h, wo_ref[...], preferred_element_type=jnp.float32)

    @pl.when(j == 0)
    def _():
        acc_ref[...] = part

    @pl.when(j > 0)
    def _():
        acc_ref[...] += part

    @pl.when(j == pl.num_programs(0) - 1)
    def _():
        y_ref[...] = _layer_norm_rows(DEEPNORM_ALPHA * x + acc_ref[...], lg_ref[...], lb_ref[...])


def _ffn_step(x, hist, w_in, conv_w, conv_b, w_out, ln_g, ln_b):
    b, d = x.shape
    c, nc = FFN_CHUNK, FFN_NCHUNK
    w_in = w_in.astype(jnp.bfloat16)
    cw8 = jnp.zeros((8, 2 * D_FF), jnp.float32).at[:FFN_CONV].set(conv_w)
    cb = conv_b.reshape(1, 2 * D_FF)
    h4 = jnp.transpose(hist, (1, 0, 2)).reshape(2, b, 2, D_FF).transpose(0, 2, 1, 3)
    y, ua, ug = pl.pallas_call(
        _ffn_step_kernel,
        grid=(nc,),
        in_specs=[pl.BlockSpec((b, d), lambda j: (0, 0)),
                  pl.BlockSpec((2, 2, b, c), lambda j: (0, 0, 0, j)),
                  pl.BlockSpec((d, c), lambda j: (0, j)),
                  pl.BlockSpec((d, c), lambda j: (0, j + nc)),
                  pl.BlockSpec((8, c), lambda j: (0, j)),
                  pl.BlockSpec((8, c), lambda j: (0, j + nc)),
                  pl.BlockSpec((1, c), lambda j: (0, j)),
                  pl.BlockSpec((1, c), lambda j: (0, j + nc)),
                  pl.BlockSpec((c, d), lambda j: (j, 0)),
                  pl.BlockSpec((1, d), lambda j: (0, 0)),
                  pl.BlockSpec((1, d), lambda j: (0, 0))],
        out_specs=[pl.BlockSpec((b, d), lambda j: (0, 0)),
                   pl.BlockSpec((b, c), lambda j: (0, j)),
                   pl.BlockSpec((b, c), lambda j: (0, j))],
        out_shape=[jax.ShapeDtypeStruct((b, d), jnp.float32),
                   jax.ShapeDtypeStruct((b, D_FF), jnp.float32),
                   jax.ShapeDtypeStruct((b, D_FF), jnp.float32)],
        scratch_shapes=[pltpu.VMEM((b, d), jnp.float32)],
        compiler_params=pltpu.CompilerParams(dimension_semantics=("arbitrary",),
                                             vmem_limit_bytes=VMEM_LIMIT_BYTES),
        name="ffn_step",
    )(x, h4, w_in, w_in, cw8, cw8, cb, cb, w_out.astype(jnp.bfloat16), ln_g.reshape(1, d), ln_b.reshape(1, d))
    u = jnp.concatenate([ua, ug], axis=-1)
    return y, jnp.concatenate([hist[:, 1:], u[:, None]], axis=1)


NEG = -1e30
NSA_KT = 512
NSA_COLS = NSA_HEADS * NSA_QBLOCK
NSA_WSPAN = NSA_WINDOW + NSA_QBLOCK


def _lane_tile(x, n):
    return jnp.concatenate([x] * n, axis=1)


def _nsa_prompt_kernel(q_ref, sm_ref, ck_ref, cvt_ref, ks_ref, vst_ref, kw_ref, vwt_ref, tri_ref, o_ref,
                       selb_ref, m_ref, l_ref, acc_ref, *, ns):
    f32, bf16 = jnp.float32, jnp.bfloat16
    qb = NSA_QBLOCK
    i = pl.program_id(0)
    s0 = i * qb
    half = NSA_COLS // 2

    qt = (q_ref[...] * (HEAD_DIM ** -0.5)).T
    zero = jnp.zeros((HEAD_DIM, qb), f32)
    top = jnp.concatenate([qt[h * HEAD_DIM:(h + 1) * HEAD_DIM] for h in range(NSA_GROUP)] + [zero] * NSA_GROUP, axis=1)
    bot = jnp.concatenate([zero] * NSA_GROUP + [qt[h * HEAD_DIM:(h + 1) * HEAD_DIM]
                                                for h in range(NSA_GROUP, NSA_HEADS)], axis=1)
    qbd = jnp.concatenate([top, bot], axis=0).astype(bf16)

    def softmax_cols(s):
        m = jnp.max(s, axis=0, keepdims=True)
        p = jnp.exp(s - m)
        return m, p, jnp.sum(p, axis=0, keepdims=True)

    def pv(vt, p):
        pb = p.astype(bf16)
        return [jnp.dot(vt[g * HEAD_DIM:(g + 1) * HEAD_DIM], pb[:, g * half:(g + 1) * half],
                        preferred_element_type=f32) for g in range(NSA_KV_HEADS)]

    nc = 2 * ns
    r = lax.broadcasted_iota(jnp.int32, (nc, qb), 0)
    lane = lax.broadcasted_iota(jnp.int32, (nc, qb), 1)
    cidx = jnp.where(r < ns, 2 * r, 2 * (r - ns) + 1)
    cbias = jnp.where((cidx + 1) * CMP_BLOCK - 1 <= s0 + lane, 0.0, NEG)
    sc = jnp.dot(ck_ref[...], qbd, preferred_element_type=f32) + _lane_tile(cbias, NSA_HEADS)
    m, p, l = softmax_cols(sc)
    pn = p * jnp.where(m > 0.5 * NEG, 1.0 / l, 0.0)
    o_cmp = pv(cvt_ref[...], pn)

    blk = lax.broadcasted_iota(jnp.int32, (ns, qb), 0)
    qpos = s0 + lax.broadcasted_iota(jnp.int32, (ns, qb), 1)
    cur = qpos // SEL_BLOCK
    forced = (blk == 0) | (blk == cur) | (blk == cur - 1)
    force_key = lax.bitcast_convert_type(jnp.full((ns, qb), NSA_FORCE, f32), jnp.int32)
    for g in range(NSA_KV_HEADS):
        imp = pn[:, g * half:g * half + qb]
        for h in range(1, NSA_GROUP):
            imp = imp + pn[:, g * half + h * qb:g * half + (h + 1) * qb]
        imp = imp[:ns] + imp[ns:]
        key = jnp.where(blk > cur, -1, jnp.where(forced, force_key, lax.bitcast_convert_type(imp, jnp.int32)))
        thr = jnp.zeros((1, qb), jnp.int32)
        for bit in range(30, -1, -1):
            cand = thr | (1 << bit)
            cnt = jnp.sum(jnp.where(key >= cand, 1.0, 0.0), axis=0, keepdims=True)
            thr = jnp.where(cnt >= NSA_TOPN, cand, thr)
        above = key > thr
        n_above = jnp.sum(jnp.where(above, 1.0, 0.0), axis=0, keepdims=True)
        tie = key == thr
        rank = jnp.dot(tri_ref[...], jnp.where(tie, 1.0, 0.0).astype(bf16), preferred_element_type=f32)
        sel = above | (tie & (rank <= NSA_TOPN - n_above))
        selb_ref[g] = jnp.where(sel, 0.0, NEG)

    m_ref[...] = jnp.full(m_ref.shape, NEG, f32)
    l_ref[...] = jnp.zeros(l_ref.shape, f32)
    acc_ref[...] = jnp.zeros(acc_ref.shape, f32)
    per_tile = NSA_KT // SEL_BLOCK

    def slc_tile(kt, causal):
        k0 = pl.multiple_of(kt * NSA_KT, NSA_KT)
        s = jnp.dot(ks_ref[pl.ds(k0, NSA_KT), :], qbd, preferred_element_type=f32)
        rows = []
        for b in range(per_tile):
            brow = jnp.concatenate([selb_ref[g, pl.ds(kt * per_tile + b, 1), :] for g in range(NSA_KV_HEADS)
                                    for _ in range(NSA_GROUP)], axis=1)
            rows.append(s[b * SEL_BLOCK:(b + 1) * SEL_BLOCK] + brow)
        s = jnp.concatenate(rows, axis=0)
        if causal:
            kpos = k0 + lax.broadcasted_iota(jnp.int32, (NSA_KT, qb), 0)
            qq = s0 + lax.broadcasted_iota(jnp.int32, (NSA_KT, qb), 1)
            s = s + _lane_tile(jnp.where(kpos <= qq, 0.0, NEG), NSA_HEADS)
        m_old = m_ref[...]
        m_new = jnp.maximum(m_old, jnp.max(s, axis=0, keepdims=True))
        alpha = jnp.exp(m_old - m_new)
        p = jnp.exp(s - m_new)
        m_ref[...] = m_new
        l_ref[...] = l_ref[...] * alpha + jnp.sum(p, axis=0, keepdims=True)
        upd = pv(vst_ref[:, pl.ds(k0, NSA_KT)], p)
        for g in range(NSA_KV_HEADS):
            acc_ref[g] = acc_ref[g] * alpha[:, g * half:(g + 1) * half] + upd[g]

    kd = s0 // NSA_KT

    def body(kt, carry):
        slc_tile(kt, False)
        return carry

    lax.fori_loop(0, kd, body, 0)
    slc_tile(kd, True)
    inv_slc = 1.0 / l_ref[...]

    w0 = pl.multiple_of(s0, qb)
    sw = jnp.dot(kw_ref[pl.ds(w0, NSA_WSPAN), :], qbd, preferred_element_type=f32)
    rr = lax.broadcasted_iota(jnp.int32, (NSA_WSPAN, qb), 0)
    qi = lax.broadcasted_iota(jnp.int32, (NSA_WSPAN, qb), 1)
    ok = (rr >= qi) & (rr <= qi + NSA_WINDOW) & (rr + s0 >= NSA_WINDOW)
    sw = sw + _lane_tile(jnp.where(ok, 0.0, NEG), NSA_HEADS)
    _, pw, lw = softmax_cols(sw)
    o_win = pv(vwt_ref[:, pl.ds(w0, NSA_WSPAN)], pw)
    inv_win = 1.0 / lw

    gt = jax.nn.sigmoid(sm_ref[...].T)
    outs = []
    for h in range(NSA_HEADS):
        g, hg = divmod(h, NSA_GROUP)
        c0, c1 = hg * qb, (hg + 1) * qb
        g_cmp = gt[h:h + 1]
        g_slc = gt[NSA_HEADS + h:NSA_HEADS + h + 1] * inv_slc[:, g * half + c0:g * half + c1]
        g_win = gt[2 * NSA_HEADS + h:2 * NSA_HEADS + h + 1] * inv_win[:, g * half + c0:g * half + c1]
        outs.append(o_cmp[g][:, c0:c1] * g_cmp + acc_ref[g, :, c0:c1] * g_slc + o_win[g][:, c0:c1] * g_win)
    o_ref[...] = jnp.concatenate(outs, axis=0).T


def _nsa_prompt_attention(qr, small, ck, cv, ksr, vs, kwr, vw):
    l = qr.shape[0]
    ns = l // SEL_BLOCK
    nc = 2 * ns
    bf16 = jnp.bfloat16
    perm = jnp.concatenate([jnp.arange(0, nc, 2), jnp.arange(1, nc, 2)])
    ckp = ck[perm].astype(bf16)
    cvt = cv[perm].T.astype(bf16)
    pad = jnp.zeros((NSA_WINDOW, NSA_KV_W), bf16)
    kwp = jnp.concatenate([pad, kwr.astype(bf16)], axis=0)
    vwt = jnp.concatenate([pad, vw.astype(bf16)], axis=0).T
    tri = (jnp.arange(ns)[:, None] >= jnp.arange(ns)[None, :]).astype(bf16)
    full = lambda a: pl.BlockSpec(a.shape, lambda i: (0,) * a.ndim)
    args = (qr, small, ckp, cvt, ksr.astype(bf16), vs.T.astype(bf16), kwp, vwt, tri)
    return pl.pallas_call(
        functools.partial(_nsa_prompt_kernel, ns=ns),
        grid=(l // NSA_QBLOCK,),
        in_specs=[pl.BlockSpec((NSA_QBLOCK, NSA_Q_W), lambda i: (i, 0)),
                  pl.BlockSpec((NSA_QBLOCK, LANES), lambda i: (i, 0))] + [full(a) for a in args[2:]],
        out_specs=pl.BlockSpec((NSA_QBLOCK, NSA_Q_W), lambda i: (i, 0)),
        out_shape=jax.ShapeDtypeStruct((l, NSA_Q_W), jnp.float32),
        scratch_shapes=[pltpu.VMEM((NSA_KV_HEADS, ns, NSA_QBLOCK), jnp.float32),
                        pltpu.VMEM((1, NSA_COLS), jnp.float32),
                        pltpu.VMEM((1, NSA_COLS), jnp.float32),
                        pltpu.VMEM((NSA_KV_HEADS, HEAD_DIM, NSA_COLS // 2), jnp.float32)],
        compiler_params=pltpu.CompilerParams(dimension_semantics=("arbitrary",),
                                             vmem_limit_bytes=VMEM_LIMIT_BYTES),
        name="nsa_prompt",
    )(*args)


PAGE_W = 2 * NSA_KV_W
CMP_HIDDEN = 2 * HEAD_DIM
CMP_PER_PAGE = PAGE_SIZE // CMP_BLOCK
CMP_FLAT = CMP_BLOCK * PAGE_W
CMP_PAGES_PER_STEP = 32
STEP_PAGES = 64
STEP_ROWS = 16


def _rope_tables(pos, width):
    half = HEAD_DIM // 2
    inv_freq = ROPE_THETA ** (-2.0 * jnp.arange(half, dtype=jnp.float32) / HEAD_DIM)
    ang = pos.astype(jnp.float32)[:, None] * inv_freq[None, :]
    cos, sin = jnp.cos(ang), jnp.sin(ang)
    reps = width // HEAD_DIM
    return (jnp.tile(jnp.concatenate([cos, cos], axis=1), (1, reps)),
            jnp.tile(jnp.concatenate([-sin, sin], axis=1), (1, reps)))


def _rope_lanes(x, cos, sin_signed):
    n = x.shape[-1]
    lane = lax.broadcasted_iota(jnp.int32, x.shape, x.ndim - 1)
    first = (lane % HEAD_DIM) < HEAD_DIM // 2
    partner = jnp.where(first, pltpu.roll(x, n - HEAD_DIM // 2, x.ndim - 1), pltpu.roll(x, HEAD_DIM // 2, x.ndim - 1))
    return x * cos + partner * sin_signed


def _cmp_step_kernel(pt_ref, *refs):
    npg = CMP_PAGES_PER_STEP
    pages = refs[:npg]
    pe_ref, w1_ref, b1_ref, w2_ref, cos_ref, sin_ref, ck_ref, cv_ref = refs[npg:]
    x = jnp.concatenate([r[0] for r in pages], axis=0) + pe_ref[...]
    h = jnp.dot(x.astype(jnp.bfloat16), w1_ref[...], preferred_element_type=jnp.float32) + b1_ref[...]
    c = jnp.dot(jax.nn.gelu(h).astype(jnp.bfloat16), w2_ref[...], preferred_element_type=jnp.float32)
    ck_ref[0] = _rope_lanes(c[:, :NSA_KV_W], cos_ref[...], sin_ref[...])
    cv_ref[0] = c[:, NSA_KV_W:]


def _compress_weights(cw1, cb1, cw2, cpe):
    eye = jnp.eye(2, dtype=jnp.float32)
    w1r = cw1.reshape(2, CMP_BLOCK, HEAD_DIM, CMP_HIDDEN)
    w1 = jnp.einsum('ktdj,ka,gb->tkgdabj', w1r, eye, eye).reshape(CMP_FLAT, 4 * CMP_HIDDEN)
    b1 = jnp.broadcast_to(cb1[:, None, :], (2, 2, CMP_HIDDEN)).reshape(1, 4 * CMP_HIDDEN)
    w2 = jnp.einsum('kjd,ka,gb->kgjabd', cw2, eye, eye).reshape(4 * CMP_HIDDEN, PAGE_W)
    pe = jnp.broadcast_to(cpe.transpose(1, 0, 2)[:, :, None, :], (CMP_BLOCK, 2, 2, HEAD_DIM)).reshape(1, CMP_FLAT)
    return w1.astype(jnp.bfloat16), b1, w2.astype(jnp.bfloat16), pe


def _nsa_sample_compress(pool, layer_idx, page_table, cw1, cb1, cw2, cpe):
    n_pool, nl = pool.shape[:2]
    db, n_pages = page_table.shape
    npg = CMP_PAGES_PER_STEP
    nchunk = n_pages // npg
    nc = n_pages * CMP_PER_PAGE
    view = pool.reshape(n_pool * nl, CMP_PER_PAGE, CMP_FLAT)
    pt = (page_table * nl + layer_idx).reshape(-1).astype(jnp.int32)
    w1, b1, w2, pe = _compress_weights(cw1, cb1, cw2, cpe)
    cos, sin = _rope_tables((jnp.arange(nc) + 1) * CMP_BLOCK - 1, NSA_KV_W)
    rows = npg * CMP_PER_PAGE

    def page_map(k):
        return lambda b, c, pt_ref: (pt_ref[b * n_pages + c * npg + k], 0, 0)

    const = lambda a: pl.BlockSpec(a.shape, lambda b, c, pt_ref: (0,) * a.ndim, pipeline_mode=pl.Buffered(1))
    grid_spec = pltpu.PrefetchScalarGridSpec(
        num_scalar_prefetch=1, grid=(db, nchunk),
        in_specs=[pl.BlockSpec((1, CMP_PER_PAGE, CMP_FLAT), page_map(k)) for k in range(npg)]
        + [const(pe), const(w1), const(b1), const(w2),
           pl.BlockSpec((rows, NSA_KV_W), lambda b, c, pt_ref: (c, 0)),
           pl.BlockSpec((rows, NSA_KV_W), lambda b, c, pt_ref: (c, 0))],
        out_specs=[pl.BlockSpec((1, rows, NSA_KV_W), lambda b, c, pt_ref: (b, c, 0))] * 2)
    return pl.pallas_call(
        _cmp_step_kernel, grid_spec=grid_spec,
        out_shape=[jax.ShapeDtypeStruct((db, nc, NSA_KV_W), jnp.float32)] * 2,
        compiler_params=pltpu.CompilerParams(dimension_semantics=("arbitrary", "arbitrary"),
                                             vmem_limit_bytes=VMEM_LIMIT_BYTES),
        name="nsa_sample_compress",
    )(pt, *([view] * npg), pe, w1, b1, w2, cos, sin)


def _nt_dot(a, b):
    return lax.dot_general(a, b, (((1,), (1,)), ((), ())), preferred_element_type=jnp.float32)


def _nsa_step_kernel(pt_ref, *refs, cur, nsl):
    f32, bf16 = jnp.float32, jnp.bfloat16
    npg = STEP_PAGES
    q_ref, ck_ref, cv_ref = refs[:3]
    pages = refs[3:3 + npg]
    (win_ref, new_ref, gate_ref, exp_ref, triu_ref, o_ref, wout_ref,
     selt_ref, m_ref, l_ref, acc_ref, side_ref) = refs[3 + npg:]
    cc = pl.program_id(1)
    q16 = q_ref[0]
    qb = q16.astype(bf16)
    row16 = lax.broadcasted_iota(jnp.int32, (STEP_ROWS, 1), 0)

    def new_key_scores(krow):
        return jnp.sum(q16 * krow, axis=1, keepdims=True)

    @pl.when(cc == 0)
    def _():
        nch = ck_ref.shape[1] // 2
        halves = lambda r: jnp.concatenate([r[0, pl.ds(0, nch, stride=2), :], r[0, pl.ds(1, nch, stride=2), :]], axis=0)
        ck, cv = halves(ck_ref), halves(cv_ref)
        s = _nt_dot(qb, ck.astype(bf16))
        p = jnp.exp(s - jnp.max(s, axis=1, keepdims=True))
        pn = p / jnp.sum(p, axis=1, keepdims=True)
        o_cmp = jnp.dot(pn.astype(bf16), cv.astype(bf16), preferred_element_type=f32)

        rowp = lax.broadcasted_iota(jnp.int32, pn.shape, 0)
        row8 = lax.broadcasted_iota(jnp.int32, (8, nsl), 0)
        blk = lax.broadcasted_iota(jnp.int32, (8, nsl), 1)
        forced = (blk == 0) | (blk == cur) | (blk == cur - 1)
        key = jnp.full((8, nsl), -1, jnp.int32)
        for g in range(NSA_KV_HEADS):
            ig = jnp.sum(jnp.where((rowp >= g * NSA_GROUP) & (rowp < (g + 1) * NSA_GROUP), pn, 0.0),
                         axis=0, keepdims=True)
            ig = ig[:, :nch] + ig[:, nch:]
            ig = jnp.concatenate([ig, jnp.zeros((1, nsl - nch), f32)], axis=1)
            kg = jnp.where(blk[:1] > cur, -1,
                           jnp.where(forced[:1], lax.bitcast_convert_type(jnp.full((1, nsl), NSA_FORCE, f32), jnp.int32),
                                     lax.bitcast_convert_type(ig, jnp.int32)))
            key = jnp.where(row8 == g, kg, key)
        thr = jnp.zeros((8, 1), jnp.int32)
        for bit in range(30, -1, -1):
            cand = thr | (1 << bit)
            cnt = jnp.sum(jnp.where(key >= cand, 1.0, 0.0), axis=1, keepdims=True)
            thr = jnp.where(cnt >= NSA_TOPN, cand, thr)
        above = key > thr
        n_above = jnp.sum(jnp.where(above, 1.0, 0.0), axis=1, keepdims=True)
        tie = key == thr
        rank = jnp.dot(jnp.where(tie, 1.0, 0.0).astype(bf16), triu_ref[...], preferred_element_type=f32)
        sel = jnp.where(above | (tie & (rank <= NSA_TOPN - n_above)), 1.0, 0.0)
        selh = jnp.where(row16 < NSA_GROUP, sel[0:1], jnp.where(row16 < NSA_HEADS, sel[1:2], 0.0))
        for j in range(selt_ref.shape[0]):
            selt_ref[j] = selh[:, j * LANES:(j + 1) * LANES]

        m_ref[...] = new_key_scores(new_ref[0, 0:1, :])
        l_ref[...] = jnp.ones(l_ref.shape, f32)
        acc_ref[...] = jnp.broadcast_to(new_ref[0, 1:2, :], acc_ref.shape)

        win = win_ref[0]
        sw = _nt_dot(qb, win[:, :NSA_KV_W].astype(bf16))
        sn = new_key_scores(new_ref[0, 2:3, :])
        mw = jnp.maximum(jnp.max(sw, axis=1, keepdims=True), sn)
        pw, pnw = jnp.exp(sw - mw), jnp.exp(sn - mw)
        lw = jnp.sum(pw, axis=1, keepdims=True) + pnw
        o_win = (jnp.dot(pw.astype(bf16), win[:, NSA_KV_W:].astype(bf16), preferred_element_type=f32)
                 + pnw * new_ref[0, 3:4, :]) / lw
        gt = jax.nn.sigmoid(gate_ref[0])
        side_ref[...] = gt[:, 0:1] * o_cmp + gt[:, 2:3] * o_win
        rw = lax.broadcasted_iota(jnp.int32, win.shape, 0)
        newrow = jnp.concatenate([new_ref[0, 2:3, :], new_ref[0, 3:4, :]], axis=1)
        wout_ref[0] = jnp.where(rw == win.shape[0] - 1, newrow, pltpu.roll(win, win.shape[0] - 1, 0))

    kv = jnp.concatenate([r[0] for r in pages], axis=0)
    s = _nt_dot(qb, kv[:, :NSA_KV_W].astype(bf16))
    picked = jnp.dot(selt_ref[cc].astype(bf16), exp_ref[...], preferred_element_type=f32)
    s = s + (picked - 1.0) * (-NEG)
    m_old = m_ref[...]
    m_new = jnp.maximum(m_old, jnp.max(s, axis=1, keepdims=True))
    alpha = jnp.exp(m_old - m_new)
    p = jnp.exp(s - m_new)
    m_ref[...] = m_new
    l_ref[...] = l_ref[...] * alpha + jnp.sum(p, axis=1, keepdims=True)
    acc_ref[...] = acc_ref[...] * alpha + jnp.dot(p.astype(bf16), kv[:, NSA_KV_W:].astype(bf16),
                                                  preferred_element_type=f32)

    @pl.when(cc == pl.num_programs(1) - 1)
    def _():
        gt = jax.nn.sigmoid(gate_ref[0])
        o_ref[0] = side_ref[...] + gt[:, 1:2] * acc_ref[...] / l_ref[...]


def _nsa_sample_attention(qr, ck, cv, slc_pool, layer_idx, page_table, win_buf, newrows, gate):
    db, n_pages = page_table.shape
    n_pool, nl = slc_pool.shape[:2]
    past = n_pages * PAGE_SIZE
    cur = past // SEL_BLOCK
    nsl = -(-(cur + 1) // LANES) * LANES
    npg = STEP_PAGES
    nchunk = n_pages // npg
    keys = npg * PAGE_SIZE
    view = slc_pool.reshape(n_pool * nl, PAGE_SIZE, PAGE_W)
    pt = (page_table * nl + layer_idx).reshape(-1).astype(jnp.int32)
    f32 = jnp.float32
    hmask = (jnp.arange(NSA_HEADS)[:, None] // NSA_GROUP == jnp.arange(NSA_KV_HEADS)[None, :]).astype(f32)
    q16 = (qr * HEAD_DIM ** -0.5)[:, :, None, :] * hmask[None, :, :, None]
    q16 = jnp.pad(q16.reshape(db, NSA_HEADS, NSA_KV_W), ((0, 0), (0, STEP_ROWS - NSA_HEADS), (0, 0)))
    new8 = jnp.pad(newrows, ((0, 0), (0, 8 - newrows.shape[1]), (0, 0)))
    g16 = jnp.pad(gate.reshape(db, 3, NSA_HEADS).transpose(0, 2, 1),
                  ((0, 0), (0, STEP_ROWS - NSA_HEADS), (0, LANES - 3)))
    expand = (jnp.arange(LANES)[:, None] == jnp.arange(keys)[None, :] // SEL_BLOCK).astype(jnp.bfloat16)
    triu = (jnp.arange(nsl)[:, None] <= jnp.arange(nsl)[None, :]).astype(jnp.bfloat16)
    wlen = win_buf.shape[1]

    def page_map(k):
        return lambda b, c, pt_ref: (pt_ref[b * n_pages + c * npg + k], 0, 0)

    per_b = lambda shp: pl.BlockSpec((1,) + shp, lambda b, c, pt_ref: (b, 0, 0))
    const = lambda a: pl.BlockSpec(a.shape, lambda b, c, pt_ref: (0,) * a.ndim)
    grid_spec = pltpu.PrefetchScalarGridSpec(
        num_scalar_prefetch=1, grid=(db, nchunk),
        in_specs=[per_b((STEP_ROWS, NSA_KV_W)), per_b(ck.shape[1:]), per_b(cv.shape[1:])]
        + [pl.BlockSpec((1, PAGE_SIZE, PAGE_W), page_map(k)) for k in range(npg)]
        + [per_b((wlen, PAGE_W)), per_b((8, NSA_KV_W)), per_b((STEP_ROWS, LANES)), const(expand), const(triu)],
        out_specs=[per_b((STEP_ROWS, NSA_KV_W)), per_b((wlen, PAGE_W))],
        scratch_shapes=[pltpu.VMEM((nsl // LANES, STEP_ROWS, LANES), f32),
                        pltpu.VMEM((STEP_ROWS, 1), f32), pltpu.VMEM((STEP_ROWS, 1), f32),
                        pltpu.VMEM((STEP_ROWS, NSA_KV_W), f32), pltpu.VMEM((STEP_ROWS, NSA_KV_W), f32)])
    o16, wout = pl.pallas_call(
        functools.partial(_nsa_step_kernel, cur=cur, nsl=nsl), grid_spec=grid_spec,
        out_shape=[jax.ShapeDtypeStruct((db, STEP_ROWS, NSA_KV_W), f32),
                   jax.ShapeDtypeStruct((db, wlen, PAGE_W), f32)],
        compiler_params=pltpu.CompilerParams(dimension_semantics=("arbitrary", "arbitrary"),
                                             vmem_limit_bytes=VMEM_LIMIT_BYTES),
        name="nsa_sample_attention",
    )(pt, q16, ck, cv, *([view] * npg), win_buf, new8, g16, expand, triu)
    o = o16[:, :NSA_HEADS].reshape(db, NSA_HEADS, NSA_KV_HEADS, HEAD_DIM)
    o = jnp.take_along_axis(o, (jnp.arange(NSA_HEADS) // NSA_GROUP)[None, :, None, None], axis=2)
    return o.reshape(db, NSA_HEADS * HEAD_DIM), wout


CMP_STEP_LANES = CMP_PAGES_PER_STEP * CMP_PER_PAGE


def _cmp_lane_blocks(n_pages):
    lane = np.arange(n_pages * CMP_PER_PAGE)
    step, rem = lane // CMP_STEP_LANES, lane % CMP_STEP_LANES
    j, pl_ = rem // CMP_PAGES_PER_STEP, rem % CMP_PAGES_PER_STEP
    return (step * CMP_PAGES_PER_STEP + pl_) * CMP_PER_PAGE + j


def _cmp_step_t_kernel(pt_ref, *refs):
    f32 = jnp.float32
    npg = CMP_PAGES_PER_STEP
    pages = refs[:npg]
    pe_ref, w1_ref, b1_ref, w2_ref, cos_ref, sin_ref, ck_ref, cv_ref, slab_ref = refs[npg:]
    outs = (ck_ref, cv_ref)
    for k, r in enumerate(pages):
        for s in range(2 * NSA_KV_HEADS):
            slab_ref[s, k * HEAD_DIM:(k + 1) * HEAD_DIM, :] = r[0, s]
    for kv in range(2):
        def add_feature(d, h):
            x = jnp.concatenate([slab_ref[2 * kv + g, pl.ds(d, npg, stride=HEAD_DIM), :]
                                 for g in range(NSA_KV_HEADS)], axis=0) + pe_ref[kv, d]
            return h + jnp.dot(x.astype(jnp.bfloat16), w1_ref[kv, d], preferred_element_type=f32)

        h = lax.fori_loop(0, HEAD_DIM, add_feature, jnp.zeros((NSA_KV_HEADS * npg, CMP_PER_PAGE * CMP_HIDDEN), f32))
        act = jax.nn.gelu(h + b1_ref[kv]).astype(jnp.bfloat16)
        ct = _nt_dot(w2_ref[kv], act)
        tile = jnp.concatenate(
            [jnp.concatenate([ct[j * HEAD_DIM:(j + 1) * HEAD_DIM, g * npg:(g + 1) * npg] for j in range(CMP_PER_PAGE)],
                             axis=1) for g in range(NSA_KV_HEADS)], axis=0)
        if kv == 0:
            row = lax.broadcasted_iota(jnp.int32, tile.shape, 0)
            n = tile.shape[0]
            partner = jnp.where((row % HEAD_DIM) < HEAD_DIM // 2, pltpu.roll(tile, n - HEAD_DIM // 2, 0),
                                pltpu.roll(tile, HEAD_DIM // 2, 0))
            tile = tile * cos_ref[...] + partner * sin_ref[...]
        outs[kv][0] = tile


def _compress_weights_t(cw1, cb1, cw2, cpe):
    eye = jnp.eye(CMP_PER_PAGE, dtype=jnp.float32)
    w1r = cw1.reshape(2, CMP_BLOCK, HEAD_DIM, CMP_HIDDEN)
    w1 = jnp.einsum('ktdn,ja->kdjtan', w1r, eye).reshape(2, HEAD_DIM, PAGE_SIZE, CMP_PER_PAGE * CMP_HIDDEN)
    b1 = jnp.tile(cb1, (1, CMP_PER_PAGE))[:, None, :]
    w2 = jnp.einsum('knd,ja->kjdan', cw2, eye).reshape(2, CMP_PER_PAGE * HEAD_DIM, CMP_PER_PAGE * CMP_HIDDEN)
    pe = jnp.tile(cpe.transpose(0, 2, 1), (1, 1, CMP_PER_PAGE))[:, :, None, :]
    return w1.astype(jnp.bfloat16), b1, w2.astype(jnp.bfloat16), pe


def _nsa_sample_compress_t(pool, layer_idx, page_table, cw1, cb1, cw2, cpe):
    n_pool, nl = pool.shape[:2]
    db, n_pages = page_table.shape
    npg = CMP_PAGES_PER_STEP
    nchunk = n_pages // npg
    nc = n_pages * CMP_PER_PAGE
    view = jnp.transpose(pool, (0, 1, 3, 4, 5, 2)).reshape(n_pool * nl, 2 * NSA_KV_HEADS, HEAD_DIM, PAGE_SIZE)
    pt = (page_table * nl + layer_idx).reshape(-1).astype(jnp.int32)
    w1, b1, w2, pe = _compress_weights_t(cw1, cb1, cw2, cpe)
    pos = (jnp.asarray(_cmp_lane_blocks(n_pages)) + 1) * CMP_BLOCK - 1
    half = HEAD_DIM // 2
    inv_freq = ROPE_THETA ** (-2.0 * jnp.arange(half, dtype=jnp.float32) / HEAD_DIM)
    ang = inv_freq[:, None] * pos.astype(jnp.float32)[None, :]
    cos = jnp.tile(jnp.cos(ang), (2 * NSA_KV_HEADS, 1))
    sin = jnp.tile(jnp.concatenate([-jnp.sin(ang), jnp.sin(ang)], axis=0), (NSA_KV_HEADS, 1))

    def page_map(k):
        return lambda b, c, pt_ref: (pt_ref[b * n_pages + c * npg + k], 0, 0, 0)

    const = lambda a: pl.BlockSpec(a.shape, lambda b, c, pt_ref: (0,) * a.ndim, pipeline_mode=pl.Buffered(1))
    lanes_c = lambda: pl.BlockSpec((NSA_KV_W, CMP_STEP_LANES), lambda b, c, pt_ref: (0, c))
    grid_spec = pltpu.PrefetchScalarGridSpec(
        num_scalar_prefetch=1, grid=(db, nchunk),
        in_specs=[pl.BlockSpec((1, 2 * NSA_KV_HEADS, HEAD_DIM, PAGE_SIZE), page_map(k)) for k in range(npg)]
        + [const(pe), const(w1), const(b1), const(w2), lanes_c(), lanes_c()],
        out_specs=[pl.BlockSpec((1, NSA_KV_W, CMP_STEP_LANES), lambda b, c, pt_ref: (b, 0, c))] * 2,
        scratch_shapes=[pltpu.VMEM((2 * NSA_KV_HEADS, npg * HEAD_DIM, PAGE_SIZE), jnp.float32)])
    return pl.pallas_call(
        _cmp_step_t_kernel, grid_spec=grid_spec,
        out_shape=[jax.ShapeDtypeStruct((db, NSA_KV_W, nc), jnp.float32)] * 2,
        compiler_params=pltpu.CompilerParams(dimension_semantics=("arbitrary", "arbitrary"),
                                             vmem_limit_bytes=VMEM_LIMIT_BYTES),
        name="nsa_sample_compress",
    )(pt, *([view] * npg), pe, w1, b1, w2, cos, sin)


def _nsa_step_t_kernel(pt_ref, *refs, topn):
    f32, bf16 = jnp.float32, jnp.bfloat16
    npg = STEP_PAGES
    q_ref, ck_ref, cv_ref = refs[:3]
    pages = refs[3:3 + npg]
    (win_ref, newr_ref, newt_ref, gate_ref, blk_ref, exp_ref, rank_ref, o_ref, wout_ref,
     selt_ref, m_ref, l_ref, acc_ref, side_ref) = refs[3 + npg:]
    cc = pl.program_id(1)
    q16 = q_ref[0]
    qb = q16.astype(bf16)
    row16 = lax.broadcasted_iota(jnp.int32, (STEP_ROWS, 1), 0)

    def new_key_scores(krow):
        return jnp.sum(q16 * krow, axis=1, keepdims=True)

    @pl.when(cc == 0)
    def _():
        nc = ck_ref.shape[2]
        s = jnp.dot(qb, ck_ref[0].astype(bf16), preferred_element_type=f32)
        p = jnp.exp(s - jnp.max(s, axis=1, keepdims=True))
        pn = p / jnp.sum(p, axis=1, keepdims=True)
        o_cmp = _nt_dot(pn.astype(bf16), cv_ref[0].astype(bf16))

        rowp = lax.broadcasted_iota(jnp.int32, pn.shape, 0)
        row8 = lax.broadcasted_iota(jnp.int32, (8, nc), 0)
        blk = blk_ref[...]
        key = jnp.full((8, nc), -1, jnp.int32)
        force_key = lax.bitcast_convert_type(jnp.full((1, nc), NSA_FORCE, f32), jnp.int32)
        for g in range(NSA_KV_HEADS):
            ig = jnp.sum(jnp.where((rowp >= g * NSA_GROUP) & (rowp < (g + 1) * NSA_GROUP), pn, 0.0),
                         axis=0, keepdims=True)
            ig = ig + pltpu.roll(ig, nc - CMP_PAGES_PER_STEP, 1)
            kg = jnp.where(blk[:1] < 0, -1, jnp.where(blk[1:2] > 0, force_key, lax.bitcast_convert_type(ig, jnp.int32)))
            key = jnp.where(row8 == g, kg, key)
        thr = jnp.zeros((8, 1), jnp.int32)
        for bit in range(30, -1, -1):
            cand = thr | (1 << bit)
            cnt = jnp.sum(jnp.where(key >= cand, 1.0, 0.0), axis=1, keepdims=True)
            thr = jnp.where(cnt >= topn, cand, thr)
        above = key > thr
        n_above = jnp.sum(jnp.where(above, 1.0, 0.0), axis=1, keepdims=True)
        tie = key == thr
        rank = jnp.dot(jnp.where(tie, 1.0, 0.0).astype(bf16), rank_ref[...], preferred_element_type=f32)
        sel = jnp.where(above | (tie & (rank <= topn - n_above)), 1.0, 0.0)
        selh = jnp.where(row16 < NSA_GROUP, sel[0:1], jnp.where(row16 < NSA_HEADS, sel[1:2], 0.0))
        wsel = selt_ref.shape[2]
        for j in range(selt_ref.shape[0]):
            selt_ref[j] = selh[:, j * wsel:(j + 1) * wsel]

        m_ref[...] = new_key_scores(newr_ref[0, 0:1, :])
        l_ref[...] = jnp.ones(l_ref.shape, f32)
        acc_ref[...] = jnp.broadcast_to(newr_ref[0, 1:2, :], acc_ref.shape)

        sw = jnp.dot(qb, win_ref[0, 0].astype(bf16), preferred_element_type=f32)
        sn = new_key_scores(newr_ref[0, 2:3, :])
        mw = jnp.maximum(jnp.max(sw, axis=1, keepdims=True), sn)
        pw, pnw = jnp.exp(sw - mw), jnp.exp(sn - mw)
        lw = jnp.sum(pw, axis=1, keepdims=True) + pnw
        o_win = (_nt_dot(pw.astype(bf16), win_ref[0, 1].astype(bf16)) + pnw * newr_ref[0, 3:4, :]) / lw
        gt = jax.nn.sigmoid(gate_ref[0])
        side_ref[...] = gt[:, 0:1] * o_cmp + gt[:, 2:3] * o_win
        wl = win_ref.shape[3]
        lane = lax.broadcasted_iota(jnp.int32, (NSA_KV_W, wl), 1)
        for kv in range(2):
            wout_ref[0, kv] = jnp.where(lane == wl - 1, newt_ref[0, :, 2 + kv:3 + kv],
                                        pltpu.roll(win_ref[0, kv], wl - 1, 1))

    kt = jnp.concatenate([r[0, 0] for r in pages], axis=1)
    vt = jnp.concatenate([r[0, 1] for r in pages], axis=1)
    s = jnp.dot(qb, kt.astype(bf16), preferred_element_type=f32)
    picked = jnp.dot(selt_ref[cc].astype(bf16), exp_ref[...], preferred_element_type=f32)
    s = s + (picked - 1.0) * (-NEG)
    m_old = m_ref[...]
    m_new = jnp.maximum(m_old, jnp.max(s, axis=1, keepdims=True))
    alpha = jnp.exp(m_old - m_new)
    p = jnp.exp(s - m_new)
    m_ref[...] = m_new
    l_ref[...] = l_ref[...] * alpha + jnp.sum(p, axis=1, keepdims=True)
    acc_ref[...] = acc_ref[...] * alpha + _nt_dot(p.astype(bf16), vt.astype(bf16))

    @pl.when(cc == pl.num_programs(1) - 1)
    def _():
        gt = jax.nn.sigmoid(gate_ref[0])
        o_ref[0] = side_ref[...] + gt[:, 1:2] * acc_ref[...] / l_ref[...]


def _nsa_sample_attention_t(qr, ckt, cvt, slc_pool, layer_idx, page_table, win_buf, newrows, gate):
    db, n_pages = page_table.shape
    n_pool, nl = slc_pool.shape[:2]
    wlen = win_buf.shape[1]
    nc = ckt.shape[2]
    past = n_pages * PAGE_SIZE
    cur = past // SEL_BLOCK
    npg = STEP_PAGES
    nchunk = n_pages // npg
    keys = npg * PAGE_SIZE
    wsel = npg * CMP_PER_PAGE
    f32, bf16 = jnp.float32, jnp.bfloat16
    view = jnp.transpose(slc_pool, (0, 1, 3, 4, 5, 2)).reshape(n_pool * nl, 2, NSA_KV_W, PAGE_SIZE)
    wint = jnp.transpose(win_buf, (0, 2, 3, 4, 1)).reshape(db, 2, NSA_KV_W, wlen)
    pt = (page_table * nl + layer_idx).reshape(-1).astype(jnp.int32)
    hmask = (jnp.arange(NSA_HEADS)[:, None] // NSA_GROUP == jnp.arange(NSA_KV_HEADS)[None, :]).astype(f32)
    q16 = (qr * HEAD_DIM ** -0.5)[:, :, None, :] * hmask[None, :, :, None]
    q16 = jnp.pad(q16.reshape(db, NSA_HEADS, NSA_KV_W), ((0, 0), (0, STEP_ROWS - NSA_HEADS), (0, 0)))
    newr = jnp.pad(newrows, ((0, 0), (0, 8 - newrows.shape[1]), (0, 0)))
    newt = jnp.pad(newrows.transpose(0, 2, 1), ((0, 0), (0, 0), (0, LANES - newrows.shape[1])))
    g16 = jnp.pad(gate.reshape(db, 3, NSA_HEADS).transpose(0, 2, 1),
                  ((0, 0), (0, STEP_ROWS - NSA_HEADS), (0, LANES - 3)))
    cblk = _cmp_lane_blocks(n_pages)
    jj = cblk % CMP_PER_PAGE
    sblk = np.where(jj % 2 == 0, cblk // 2, -1)
    forced = ((sblk == 0) | (sblk == cur - 1)).astype(np.int32)
    blk8 = np.zeros((8, nc), np.int32)
    blk8[0], blk8[1] = sblk, forced
    rankm = ((sblk[:, None] >= 0) & (sblk[:, None] <= sblk[None, :])).astype(np.float32)
    loc = np.arange(wsel)
    lstep, lrem = loc // CMP_STEP_LANES, loc % CMP_STEP_LANES
    lj, lpage = lrem // CMP_PAGES_PER_STEP, lstep * CMP_PAGES_PER_STEP + lrem % CMP_PAGES_PER_STEP
    kidx = np.arange(keys)
    expand = ((lj[:, None] % 2 == 0) & (kidx[None, :] // PAGE_SIZE == lpage[:, None])
              & ((kidx[None, :] % PAGE_SIZE) // SEL_BLOCK == lj[:, None] // 2)).astype(np.float32)
    topn = min(NSA_TOPN, cur + 1) - 1

    def page_map(k):
        return lambda b, c, pt_ref: (pt_ref[b * n_pages + c * npg + k], 0, 0, 0)

    per_b = lambda shp: pl.BlockSpec((1,) + shp, lambda b, c, pt_ref: (b,) + (0,) * len(shp))
    const = lambda a: pl.BlockSpec(a.shape, lambda b, c, pt_ref: (0,) * a.ndim)
    consts = (jnp.asarray(blk8), jnp.asarray(expand, bf16), jnp.asarray(rankm, bf16))
    grid_spec = pltpu.PrefetchScalarGridSpec(
        num_scalar_prefetch=1, grid=(db, nchunk),
        in_specs=[per_b((STEP_ROWS, NSA_KV_W)), per_b((NSA_KV_W, nc)), per_b((NSA_KV_W, nc))]
        + [pl.BlockSpec((1, 2, NSA_KV_W, PAGE_SIZE), page_map(k)) for k in range(npg)]
        + [per_b((2, NSA_KV_W, wlen)), per_b((8, NSA_KV_W)), per_b((NSA_KV_W, LANES)), per_b((STEP_ROWS, LANES))]
        + [const(a) for a in consts],
        out_specs=[per_b((STEP_ROWS, NSA_KV_W)), per_b((2, NSA_KV_W, wlen))],
        scratch_shapes=[pltpu.VMEM((nc // wsel, STEP_ROWS, wsel), f32),
                        pltpu.VMEM((STEP_ROWS, 1), f32), pltpu.VMEM((STEP_ROWS, 1), f32),
                        pltpu.VMEM((STEP_ROWS, NSA_KV_W), f32), pltpu.VMEM((STEP_ROWS, NSA_KV_W), f32)])
    o16, wout = pl.pallas_call(
        functools.partial(_nsa_step_t_kernel, topn=topn), grid_spec=grid_spec,
        out_shape=[jax.ShapeDtypeStruct((db, STEP_ROWS, NSA_KV_W), f32),
                   jax.ShapeDtypeStruct((db, 2, NSA_KV_W, wlen), f32)],
        compiler_params=pltpu.CompilerParams(dimension_semantics=("arbitrary", "arbitrary"),
                                             vmem_limit_bytes=VMEM_LIMIT_BYTES),
        name="nsa_sample_attention",
    )(pt, q16, ckt, cvt, *([view] * npg), wint, newr, newt, g16, *consts)
    o = o16[:, :NSA_HEADS].reshape(db, NSA_HEADS, NSA_KV_HEADS, HEAD_DIM)
    o = jnp.take_along_axis(o, (jnp.arange(NSA_HEADS) // NSA_GROUP)[None, :, None, None], axis=2)
    wout = jnp.transpose(wout.reshape(db, 2, NSA_KV_HEADS, HEAD_DIM, wlen), (0, 4, 1, 2, 3))
    return o.reshape(db, NSA_HEADS * HEAD_DIM), wout


DIL_ROW_CHUNK = 64


def _dil_step_kernel(q_ref, buf_ref, newt_ref, newr_ref, bias_ref, o_ref, out_ref, p_ref, pn_ref, den_ref):
    f32, bf16 = jnp.float32, jnp.bfloat16
    kv = pl.program_id(1)
    wlen = buf_ref.shape[3]
    nrow = buf_ref.shape[2]
    q16 = q_ref[0]

    @pl.when(kv == 0)
    def _():
        s = jnp.dot(q16.astype(bf16), buf_ref[0, 0].astype(bf16), preferred_element_type=f32)
        s_new = jnp.sum(q16 * newr_ref[0, 0:1, :], axis=1, keepdims=True)
        ms, es, ens, dens = [], [], [], []
        for g in range(len(DIL_GROUPS)):
            sg = s + bias_ref[g:g + 1, :]
            m = jnp.maximum(jnp.max(sg, axis=1, keepdims=True), s_new)
            e, en = jnp.exp(sg - m), jnp.exp(s_new - m)
            ms.append(m); es.append(e); ens.append(en)
            dens.append(jnp.sum(e, axis=1, keepdims=True) + en)
        m_all = functools.reduce(jnp.maximum, ms)
        ws = [jnp.exp(m - m_all) for m in ms]
        p_ref[...] = sum(w * e for w, e in zip(ws, es))
        pn_ref[...] = sum(w * en for w, en in zip(ws, ens))
        den_ref[...] = sum(w * d for w, d in zip(ws, dens))

    @pl.when(kv == 1)
    def _():
        r = _nt_dot(p_ref[...].astype(bf16), buf_ref[0, 0].astype(bf16))
        r = (r + pn_ref[...] * newr_ref[0, 1:2, :]) / den_ref[...]
        head = lax.broadcasted_iota(jnp.int32, r.shape, 1) // HEAD_DIM
        row = lax.broadcasted_iota(jnp.int32, r.shape, 0)
        o_ref[0] = jnp.broadcast_to(jnp.sum(jnp.where(head == row, r, 0.0), axis=0, keepdims=True), o_ref.shape[1:])

    lane = lax.broadcasted_iota(jnp.int32, (DIL_ROW_CHUNK, wlen), 1)
    for c in range(nrow // DIL_ROW_CHUNK):
        rs = slice(c * DIL_ROW_CHUNK, (c + 1) * DIL_ROW_CHUNK)
        col = jnp.where(kv == 0, newt_ref[0, rs, 0:1], newt_ref[0, rs, 1:2])
        out_ref[0, 0, rs, :] = jnp.where(lane == wlen - 1, col, pltpu.roll(buf_ref[0, 0, rs, :], wlen - 1, 1))


def _dil_sample_attention(qr, kr_new, v_new, buf):
    db, wlen = buf.shape[:2]
    f32 = jnp.float32
    buft = jnp.transpose(buf, (0, 2, 3, 4, 1)).reshape(db, 2, DIL_W, wlen)
    eye = jnp.eye(DIL_HEADS, dtype=f32)
    q16 = ((qr * HEAD_DIM ** -0.5)[:, :, None, :] * eye[None, :, :, None]).reshape(db, DIL_HEADS, DIL_W)
    newr = jnp.pad(jnp.stack([kr_new, v_new], axis=1), ((0, 0), (0, 6), (0, 0)))
    newt = jnp.pad(jnp.stack([kr_new, v_new], axis=2), ((0, 0), (0, 0), (0, LANES - 2)))
    back = wlen - jnp.arange(wlen)
    bias = jnp.stack([jnp.where((back % d == 0) & (back // d <= DIL_SPAN), 0.0, NEG) for _, d in DIL_GROUPS])
    bias = jnp.pad(bias, ((0, 8 - len(DIL_GROUPS)), (0, 0))).astype(f32)
    o, new_buf = pl.pallas_call(
        _dil_step_kernel,
        grid=(db, 2),
        in_specs=[pl.BlockSpec((1, DIL_HEADS, DIL_W), lambda b, k: (b, 0, 0)),
                  pl.BlockSpec((1, 1, DIL_W, wlen), lambda b, k: (b, k, 0, 0)),
                  pl.BlockSpec((1, DIL_W, LANES), lambda b, k: (b, 0, 0)),
                  pl.BlockSpec((1, 8, DIL_W), lambda b, k: (b, 0, 0)),
                  pl.BlockSpec((8, wlen), lambda b, k: (0, 0))],
        out_specs=[pl.BlockSpec((1, 8, DIL_W), lambda b, k: (b, 0, 0)),
                   pl.BlockSpec((1, 1, DIL_W, wlen), lambda b, k: (b, k, 0, 0))],
        out_shape=[jax.ShapeDtypeStruct((db, 8, DIL_W), f32),
                   jax.ShapeDtypeStruct((db, 2, DIL_W, wlen), f32)],
        scratch_shapes=[pltpu.VMEM((DIL_HEADS, wlen), f32), pltpu.VMEM((DIL_HEADS, 1), f32),
                        pltpu.VMEM((DIL_HEADS, 1), f32)],
        compiler_params=pltpu.CompilerParams(dimension_semantics=("arbitrary", "arbitrary"),
                                             vmem_limit_bytes=VMEM_LIMIT_BYTES),
        name="dil_sample",
    )(q16, buft, newt, newr, bias)
    new_buf = jnp.transpose(new_buf.reshape(db, 2, DIL_HEADS, HEAD_DIM, wlen), (0, 4, 1, 2, 3))
    return o[:, 0], new_buf


def _split_cols(h, widths):
    parts, start = [], 0
    for w in widths:
        parts.append(h[..., start:start + w])
        start += w
    return parts


def _even_widths():
    return (NSA_Q_W,) + (NSA_KV_W,) * 6 + (3 * NSA_HEADS, 3 * GDN_W, GDN_HEADS, GDN_HEADS, GDN_W)


def _rms_norm(x, w):
    return x * lax.rsqrt(jnp.mean(jnp.square(x), axis=-1, keepdims=True) + NORM_EPS) * w


def _l2_norm(x):
    return x * lax.rsqrt(jnp.sum(jnp.square(x), axis=-1, keepdims=True) + NORM_EPS)


def _rope(x, pos):
    half = HEAD_DIM // 2
    inv_freq = ROPE_THETA ** (-2.0 * jnp.arange(half, dtype=jnp.float32) / HEAD_DIM)
    ang = pos.astype(jnp.float32)[:, None] * inv_freq[None, :]
    cos, sin = jnp.cos(ang)[:, None, :], jnp.sin(ang)[:, None, :]
    xf = x.astype(jnp.float32)
    x1, x2 = xf[..., :half], xf[..., half:]
    return jnp.concatenate([x1 * cos - x2 * sin, x2 * cos + x1 * sin], axis=-1)


def _causal_dwconv(hist, u, w):
    width, s = w.shape[0], u.shape[1]
    ext = jnp.concatenate([hist.astype(u.dtype), u], axis=1)
    out = w[0] * ext[:, :s]
    for j in range(1, width):
        out = out + w[j] * ext[:, j:j + s]
    return out, ext[:, s:]


def _masked_softmax(s, mask):
    s = jnp.where(mask, s, -jnp.inf)
    m = jnp.max(s, axis=-1, keepdims=True)
    m = jnp.where(jnp.isfinite(m), m, 0.0)
    e = jnp.where(mask, jnp.exp(s - m), 0.0)
    den = jnp.sum(e, axis=-1, keepdims=True)
    return e / jnp.where(den > 0.0, den, 1.0)


def _gather_pages(pool, page_table, layer_idx):
    rows = pool[page_table, layer_idx]
    return rows.reshape(rows.shape[0], -1, *rows.shape[3:])


def _nsa_compress(rows, w1, b1, w2, pe):
    b, l, g, dh = rows.shape
    nc = l // CMP_BLOCK
    blk = rows[:, :nc * CMP_BLOCK].astype(jnp.float32).reshape(b, nc, CMP_BLOCK, g, dh) + pe[:, None, :]
    flat = blk.transpose(0, 1, 3, 2, 4).reshape(b, nc, g, CMP_BLOCK * dh)
    return jax.nn.gelu(flat @ w1 + b1) @ w2


def _nsa_compressed_kv(k_rows, v_rows, cw1, cb1, cw2, cpe):
    ck = _nsa_compress(k_rows, cw1[0], cb1[0], cw2[0], cpe[0])
    cv = _nsa_compress(v_rows, cw1[1], cb1[1], cw2[1], cpe[1])
    nc = ck.shape[1]
    ck = _rope(ck, (jnp.arange(nc) + 1) * CMP_BLOCK - 1)
    return ck, cv


def _nsa_attend(q, qpos, ck, cv, sk, sv, wk, wv, wpos):
    b, nq = q.shape[:2]
    scale = HEAD_DIM ** -0.5
    nc, ns = ck.shape[1], sk.shape[2]
    cend = (jnp.arange(nc) + 1) * CMP_BLOCK - 1
    s = jnp.einsum('bqghd,bcgd->bghqc', q, ck) * scale
    p_cmp = _masked_softmax(s, cend[None, :] <= qpos[:, None])
    o_cmp = jnp.einsum('bghqc,bcgd->bqghd', p_cmp, cv)
    imp = jnp.sum(p_cmp, axis=2)
    imp = jnp.pad(imp, ((0, 0), (0, 0), (0, 0), (0, ns * CMP_PER_SEL - nc)))
    imp = imp.reshape(b, NSA_KV_HEADS, nq, ns, CMP_PER_SEL).sum(-1)
    blk = jnp.arange(ns)[None, :]
    cur = (qpos // SEL_BLOCK)[:, None]
    forced = (blk == 0) | (blk == cur) | (blk == cur - 1)
    imp = jnp.where(blk <= cur, jnp.where(forced, NSA_FORCE, imp), -1.0)
    _, idx = lax.top_k(imp, min(NSA_TOPN, ns))
    n = idx.shape[-1]
    pick = jax.vmap(jax.vmap(lambda kb, ix: kb[ix]))
    ksel = pick(sk, idx).reshape(b, NSA_KV_HEADS, nq, n * SEL_BLOCK, HEAD_DIM)
    vsel = pick(sv, idx).reshape(b, NSA_KV_HEADS, nq, n * SEL_BLOCK, HEAD_DIM)
    kpos = (idx[..., None] * SEL_BLOCK + jnp.arange(SEL_BLOCK)).reshape(b, NSA_KV_HEADS, nq, n * SEL_BLOCK)
    s = jnp.einsum('bqghd,bgqkd->bghqk', q, ksel) * scale
    p = _masked_softmax(s, (kpos <= qpos[:, None])[:, :, None])
    o_slc = jnp.einsum('bghqk,bgqkd->bqghd', p, vsel)
    dist = qpos[:, None] - wpos[None, :]
    wmask = (dist >= 0) & (dist <= NSA_WINDOW) & (wpos[None, :] >= 0)
    s = jnp.einsum('bqghd,bkgd->bghqk', q, wk) * scale
    p = _masked_softmax(s, wmask)
    o_win = jnp.einsum('bghqk,bkgd->bqghd', p, wv)
    return o_cmp, o_slc, o_win


def _nsa_prompt(q, kc, vc, ks, vs, kw, vw, gate, cw1, cb1, cw2, cpe):
    b, l = q.shape[:2]
    pos = jnp.arange(l)
    qr = _rope(q, pos)
    ck, cv = _nsa_compressed_kv(kc, vc, cw1, cb1, cw2, cpe)
    ksr = _rope(ks, pos)
    vsf = vs.astype(jnp.float32)
    kwr = _rope(kw, pos)
    vwf = vw.astype(jnp.float32)
    small = jnp.pad(gate.reshape(l, -1), ((0, 0), (0, LANES - gate.shape[-1])))
    flat = lambda t: t.reshape(t.shape[1], -1)
    o_nsa = _nsa_prompt_attention(flat(qr), small, flat(ck), flat(cv), flat(ksr), flat(vsf), flat(kwr), flat(vwf))
    keep = min(NSA_WINDOW, l)
    rows_cmp = jnp.stack([kc, vc], axis=2)
    rows_slc = jnp.stack([ksr, vsf], axis=2)
    rows_win = jnp.stack([kwr[:, l - keep:], vwf[:, l - keep:]], axis=2)
    return o_nsa[None], rows_cmp, rows_slc, rows_win


def _nsa_sample(q, kc, vc, ks, vs, kw, vw, gate, cmp_pool, slc_pool, layer_idx, win_buf, page_table,
                cw1, cb1, cw2, cpe):
    db, s = q.shape[:2]
    past = page_table.shape[1] * PAGE_SIZE
    wb = win_buf.shape[1]
    assert s == 1 and wb == NSA_WINDOW and past >= wb and past % (STEP_PAGES * PAGE_SIZE) == 0
    qpos = past + jnp.arange(s)
    qr = _rope(q, qpos)
    ckt, cvt = _nsa_sample_compress_t(cmp_pool, layer_idx, page_table, cw1, cb1, cw2, cpe)
    ksr = _rope(ks, qpos)
    vsf = vs.astype(jnp.float32)
    kwr = _rope(kw, qpos)
    vwf = vw.astype(jnp.float32)
    newrows = jnp.stack([t.reshape(db, NSA_KV_W) for t in (ksr, vsf, kwr, vwf)], axis=1)
    o_nsa, rows_win = _nsa_sample_attention_t(qr[:, 0], ckt, cvt, slc_pool, layer_idx, page_table, win_buf,
                                              newrows, gate.reshape(db, -1))
    rows_cmp = jnp.stack([kc, vc], axis=2)
    rows_slc = jnp.stack([ksr, vsf], axis=2)
    return o_nsa[:, None], rows_cmp, rows_slc, rows_win


def _gdn_chunked(q, k, v, g, beta, s0):
    b, l, h, dk = q.shape
    dv = v.shape[-1]
    c = GDN_CHUNK
    nch = l // c
    r = lambda a: jnp.moveaxis(a.reshape(b, nch, c, h, *a.shape[3:]), 3, 2)
    q, k, v, g, beta = r(q), r(k), r(v), r(g), r(beta)
    gc = jnp.cumsum(g, axis=-1)
    ii = jnp.arange(c)
    tri = ii[:, None] >= ii[None, :]
    strict = ii[:, None] > ii[None, :]
    diff = gc[..., :, None] - gc[..., None, :]
    gamma = jnp.where(tri, jnp.exp(jnp.where(tri, diff, 0.0)), 0.0)
    kb = k * beta[..., None]
    a_mat = jnp.where(strict, jnp.einsum('bnhik,bnhjk->bnhij', kb, k) * gamma, 0.0)
    eye = jnp.eye(c, dtype=jnp.float32)
    t_inv = lax.linalg.triangular_solve(eye + a_mat, jnp.broadcast_to(eye, a_mat.shape),
                                        left_side=True, lower=True, unit_diagonal=True)
    u = t_inv @ (v * beta[..., None])
    w = t_inv @ (kb * jnp.exp(gc)[..., None])
    qk = jnp.where(tri, jnp.einsum('bnhik,bnhjk->bnhij', q, k) * gamma, 0.0)
    qg = q * jnp.exp(gc)[..., None]
    kd = k * jnp.exp(gc[..., -1:] - gc)[..., None]
    glast = jnp.exp(gc[..., -1])

    def step(state, xs):
        qg_c, kd_c, u_c, w_c, qk_c, gl_c = xs
        v_new = u_c - jnp.einsum('bhck,bhkv->bhcv', w_c, state)
        o = jnp.einsum('bhck,bhkv->bhcv', qg_c, state) + jnp.einsum('bhij,bhjv->bhiv', qk_c, v_new)
        state = state * gl_c[..., None, None] + jnp.einsum('bhck,bhcv->bhkv', kd_c, v_new)
        return state, o

    xs = tuple(jnp.moveaxis(a, 1, 0) for a in (qg, kd, u, w, qk, glast))
    s_fin, o = lax.scan(step, s0, xs)
    o = jnp.moveaxis(jnp.moveaxis(o, 0, 1), 2, 3).reshape(b, l, h, dv)
    return o, s_fin


def _gdn_recurrent(q, k, v, g, beta, s0):
    def step(state, xs):
        q_t, k_t, v_t, g_t, b_t = xs
        state = state * jnp.exp(g_t)[..., None, None]
        v_t = (v_t - jnp.einsum('bhk,bhkv->bhv', k_t, state)) * b_t[..., None]
        state = state + jnp.einsum('bhk,bhv->bhkv', k_t, v_t)
        return state, jnp.einsum('bhk,bhkv->bhv', q_t, state)

    xs = tuple(jnp.moveaxis(a, 1, 0) for a in (q, k, v, g, beta))
    s_fin, o = lax.scan(step, s0, xs)
    return jnp.moveaxis(o, 0, 1), s_fin


def _gdn_mix(qkv, a, bt, z, conv_hist, s0, conv_w, a_log, dt_bias, norm_w, chunked):
    b, s = qkv.shape[:2]
    c, new_hist = _causal_dwconv(conv_hist, qkv, conv_w)
    c = jax.nn.silu(c.astype(jnp.float32))
    q, k, v = [t.reshape(b, s, GDN_HEADS, HEAD_DIM) for t in jnp.split(c, 3, axis=-1)]
    q = _l2_norm(q) * HEAD_DIM ** -0.5
    k = _l2_norm(k)
    beta = jax.nn.sigmoid(bt.astype(jnp.float32))
    g = -jnp.exp(a_log) * jax.nn.softplus(a.astype(jnp.float32) + dt_bias)
    s0 = s0.astype(jnp.float32)
    if chunked:
        o, s_fin = _gdn_chunked(q, k, v, g, beta, s0)
    else:
        o, s_fin = _gdn_recurrent(q, k, v, g, beta, s0)
    o = _rms_norm(o, norm_w) * jax.nn.silu(z.astype(jnp.float32).reshape(b, s, GDN_HEADS, HEAD_DIM))
    return o.reshape(b, s, GDN_W), new_hist, s_fin


def _even_merge(o_cmp, o_slc, o_win, gate, o_gdn):
    b, s = gate.shape[:2]
    gt = jax.nn.sigmoid(gate.astype(jnp.float32)).reshape(b, s, 3, NSA_HEADS, 1)
    o_nsa = gt[:, :, 0] * o_cmp + gt[:, :, 1] * o_slc + gt[:, :, 2] * o_win
    return jnp.concatenate([o_nsa.reshape(b, s, NSA_Q_W), o_gdn], axis=-1)


def _proj(x, w):
    b, s, d = x.shape
    n = w.shape[1]
    npad = -(-n // LANES) * LANES
    wp = jnp.pad(w, ((0, 0), (0, npad - n)))
    return _matmul(x.reshape(b * s, d), wp)[:, :n].reshape(b, s, n)


def _even_prompt(x, w_in, cw1, cb1, cw2, cpe, conv_w, a_log, dt_bias, norm_w):
    b, l, _ = x.shape
    q, kc, vc, ks, vs, kw, vw, gate, qkv, a, bt, z = _split_cols(_proj(x, w_in), _even_widths())
    heads = lambda t: t.reshape(b, l, -1, HEAD_DIM)
    o_nsa, r_cmp, r_slc, r_win = _nsa_prompt(
        heads(q), heads(kc), heads(vc), heads(ks), heads(vs), heads(kw), heads(vw), gate, cw1, cb1, cw2, cpe)
    hist0 = jnp.zeros((b, GDN_CONV - 1, 3 * GDN_W), qkv.dtype)
    s0 = jnp.zeros((b, GDN_HEADS, HEAD_DIM, HEAD_DIM), jnp.float32)
    o_gdn, conv_hist, s_fin = _gdn_mix(qkv, a, bt, z, hist0, s0, conv_w, a_log, dt_bias, norm_w, True)
    return jnp.concatenate([o_nsa, o_gdn], axis=-1), r_cmp, r_slc, r_win, conv_hist, s_fin


def _even_sample(x, cmp_pool, slc_pool, layer_idx, win_buf, conv_hist, s0, page_table,
                 w_in, cw1, cb1, cw2, cpe, conv_w, a_log, dt_bias, norm_w):
    b, s, _ = x.shape
    q, kc, vc, ks, vs, kw, vw, gate, qkv, a, bt, z = _split_cols(_proj(x, w_in), _even_widths())
    heads = lambda t: t.reshape(b, s, -1, HEAD_DIM)
    o_nsa, r_cmp, r_slc, r_win = _nsa_sample(
        heads(q), heads(kc), heads(vc), heads(ks), heads(vs), heads(kw), heads(vw), gate,
        cmp_pool, slc_pool, layer_idx, win_buf, page_table, cw1, cb1, cw2, cpe)
    o_gdn, new_hist, s_fin = _gdn_mix(qkv, a, bt, z, conv_hist, s0, conv_w, a_log, dt_bias, norm_w, False)
    return jnp.concatenate([o_nsa, o_gdn], axis=-1), r_cmp, r_slc, r_win, new_hist, s_fin


def _dilated_band_stats(q, k, v, d):
    b, l, h, dh = q.shape
    unit = d * DIL_BLOCK
    lp = -(-l // unit) * unit
    nb = lp // unit
    to_sub = lambda a: jnp.pad(a, ((0, 0), (0, lp - l), (0, 0), (0, 0))).reshape(b, nb, DIL_BLOCK, d, h, dh)
    qs, ks, vs = to_sub(q), to_sub(k), to_sub(v)
    prev = lambda a: jnp.concatenate([jnp.zeros_like(a[:, :1]), a[:, :-1]], axis=1)
    kk = jnp.concatenate([prev(ks), ks], axis=2)
    vv = jnp.concatenate([prev(vs), vs], axis=2)
    s = jnp.einsum('bnirhd,bnjrhd->bnrhij', qs, kk) * HEAD_DIM ** -0.5
    i = jnp.arange(DIL_BLOCK)
    j = jnp.arange(2 * DIL_BLOCK) - DIL_BLOCK
    dist = i[:, None] - j[None, :]
    sub_k = jnp.arange(nb)[:, None, None] * DIL_BLOCK + j[None, None, :]
    mask = (dist >= 0)[None] & (dist <= DIL_SPAN)[None] & (sub_k >= 0)
    s = jnp.where(mask[None, :, None, None], s, -jnp.inf)
    m = jnp.max(s, axis=-1)
    e = jnp.exp(s - m[..., None])
    den = jnp.sum(e, axis=-1)
    num = jnp.einsum('bnrhij,bnjrhd->bnrhid', e, vv)
    m = jnp.transpose(m, (0, 1, 4, 2, 3)).reshape(b, lp, h)[:, :l]
    den = jnp.transpose(den, (0, 1, 4, 2, 3)).reshape(b, lp, h)[:, :l]
    num = jnp.transpose(num, (0, 1, 4, 2, 3, 5)).reshape(b, lp, h, dh)[:, :l]
    return m, den, num


def _dilated_gather_stats(q, k_all, v_all, qpos, kpos0, d):
    kp = qpos[:, None] - jnp.arange(DIL_SPAN + 1)[None, :] * d
    idx = kp - kpos0
    valid = idx >= 0
    idxc = jnp.clip(idx, 0)
    kg, vg = k_all[:, idxc], v_all[:, idxc]
    s = jnp.einsum('bshd,bsmhd->bshm', q, kg) * HEAD_DIM ** -0.5
    s = jnp.where(valid[None, :, None, :], s, -jnp.inf)
    m = jnp.max(s, axis=-1)
    e = jnp.exp(s - m[..., None])
    return m, jnp.sum(e, axis=-1), jnp.einsum('bshm,bsmhd->bshd', e, vg)


def _combine_by_denominators(stats):
    m_all = stats[0][0]
    for m, _, _ in stats[1:]:
        m_all = jnp.maximum(m_all, m)
    num, den = None, None
    for m, dn, nm in stats:
        w = jnp.exp(m - m_all)
        num = w[..., None] * nm if num is None else num + w[..., None] * nm
        den = w * dn if den is None else den + w * dn
    return num / den[..., None]


def _dil_prompt(x, w_in):
    b, l, _ = x.shape
    q, k, v = [t.reshape(b, l, DIL_HEADS, HEAD_DIM) for t in jnp.split(_proj(x, w_in), 3, axis=-1)]
    pos = jnp.arange(l)
    qr, kr, vf = _rope(q, pos), _rope(k, pos), v.astype(jnp.float32)
    o = _combine_by_denominators([_dilated_band_stats(qr, kr, vf, d) for _, d in DIL_GROUPS])
    keep = min(DIL_MAX_WINDOW, l)
    buf = jnp.stack([kr[:, l - keep:], vf[:, l - keep:]], axis=2)
    return o.reshape(b, l, DIL_W), buf


def _dil_sample(x, buf, past, w_in):
    db, s, _ = x.shape
    q, k, v = [t.reshape(db, s, DIL_HEADS, HEAD_DIM) for t in jnp.split(_proj(x, w_in), 3, axis=-1)]
    assert s == 1 and buf.shape[1] == DIL_MAX_WINDOW <= past
    qpos = past + jnp.arange(s)
    qr, kr = _rope(q, qpos), _rope(k, qpos)
    o, new_buf = _dil_sample_attention(qr[:, 0], kr.reshape(db, DIL_W), v.reshape(db, DIL_W).astype(jnp.float32), buf)
    return o[:, None], new_buf


def kernel(x_prompt, x_sample, cache_nsa_cmp_kv, cache_nsa_slc_kv, state_nsa_win_kv, state_gdn_conv,
           state_gdn_S, state_dil_kv, state_ffn_conv, page_table, w_in_a, nsa_cmp_w1, nsa_cmp_b1, nsa_cmp_w2,
           nsa_cmp_pe, gdn_conv_w, gdn_A_log, gdn_dt_bias, gdn_norm_w, w_out_a, w_in_c, w_out_c,
           ln_mix_g, ln_mix_b, ffn_w_in, ffn_conv_w, ffn_conv_b, ffn_w_out, ln_ffn_g, ln_ffn_b):
    past = page_table.shape[1] * PAGE_SIZE
    bp, lp, d = x_prompt.shape
    bs, ls, _ = x_sample.shape
    assert bp == 1 and ls == 1
    xp, xs = x_prompt, x_sample
    cmp_p, cmp_s, slc_p, slc_s, win_p, win_s = [], [], [], [], [], []
    gconv_p, gconv_s, gstate_p, gstate_s = [], [], [], []
    dil_p, dil_s, ffn_p, ffn_s = [], [], [], []
    for layer in range(DEPTH):
        if layer % 2 == 0:
            la = layer // 2
            wa = (w_in_a[la], nsa_cmp_w1[la], nsa_cmp_b1[la], nsa_cmp_w2[la], nsa_cmp_pe[la],
                  gdn_conv_w[la], gdn_A_log[la], gdn_dt_bias[la], gdn_norm_w[la])
            mp, rc, rs, rw, hc, hs_ = _even_prompt(xp, *wa)
            cmp_p.append(rc); slc_p.append(rs); win_p.append(rw); gconv_p.append(hc); gstate_p.append(hs_)
            ms, rc, rs, rw, hc, hs_ = _even_sample(xs, cache_nsa_cmp_kv, cache_nsa_slc_kv, la,
                                                   state_nsa_win_kv[:, la], state_gdn_conv[:, la],
                                                   state_gdn_S[:, la], page_table, *wa)
            cmp_s.append(rc); slc_s.append(rs); win_s.append(rw); gconv_s.append(hc); gstate_s.append(hs_)
            w_out = w_out_a[la]
        else:
            lc = layer // 2
            mp, bpf = _dil_prompt(xp, w_in_c[lc])
            ms, bsf = _dil_sample(xs, state_dil_kv[:, lc], past, w_in_c[lc])
            dil_p.append(bpf); dil_s.append(bsf)
            w_out = w_out_c[lc]
        xp2 = _matmul_ln(mp.reshape(lp, -1), w_out, xp.reshape(lp, d), ln_mix_g[layer], ln_mix_b[layer])
        xs2 = _matmul_ln(ms.reshape(bs, -1), w_out, xs.reshape(bs, d), ln_mix_g[layer], ln_mix_b[layer])
        fargs = (ffn_w_in[layer], ffn_conv_w[layer], ffn_conv_b[layer], ffn_w_out[layer],
                 ln_ffn_g[layer], ln_ffn_b[layer])
        xp3, hp = _ffn_seq(xp2, *fargs)
        xs3, hs = _ffn_step(xs2, state_ffn_conv[:, layer], *fargs)
        xp, xs = xp3.reshape(1, lp, d), xs3.reshape(bs, 1, d)
        ffn_p.append(hp[None]); ffn_s.append(hs)

    def stk(lst):
        return jnp.stack(lst, axis=1)

    return (xp, xs, stk(cmp_p), stk(cmp_s), stk(slc_p), stk(slc_s), stk(win_p), stk(win_s),
            stk(gconv_p), stk(gconv_s), stk(gstate_p), stk(gstate_s), stk(dil_p), stk(dil_s),
            stk(ffn_p), stk(ffn_s))
```

```python
import functools
import math

import jax
import jax.numpy as jnp
from jax import lax
from jax.experimental import pallas as pl
from jax.experimental.pallas import tpu as pltpu
import numpy as np

D_MODEL = 1024
DEPTH = 2
PAGE_SIZE = 128
HEAD_DIM = 64
ROPE_THETA = 10000.0
NSA_HEADS = 8
NSA_KV_HEADS = 2
NSA_GROUP = NSA_HEADS // NSA_KV_HEADS
CMP_BLOCK = 32
SEL_BLOCK = 64
CMP_PER_SEL = SEL_BLOCK // CMP_BLOCK
NSA_TOPN = 16
NSA_WINDOW = 512
NSA_QBLOCK = 128
NSA_FORCE = 1.0e4
GDN_HEADS = 8
GDN_CONV = 4
GDN_CHUNK = 64
DIL_HEADS = 16
DIL_GROUPS = ((128, 1), (512, 4), (2048, 16))
DIL_SPAN = 128
DIL_BLOCK = 128
DIL_MAX_WINDOW = 2048
D_FF = 2816
FFN_CONV = 3
DEEPNORM_ALPHA = (2.0 * DEPTH) ** 0.25
LN_EPS = 1e-5
NORM_EPS = 1e-6
NSA_Q_W = NSA_HEADS * HEAD_DIM
NSA_KV_W = NSA_KV_HEADS * HEAD_DIM
GDN_W = GDN_HEADS * HEAD_DIM
DIL_W = DIL_HEADS * HEAD_DIM

LANES = 128
VMEM_LIMIT_BYTES = 56 * 1024 * 1024


def _layer_norm_rows(r, g, b):
    mu = jnp.mean(r, axis=-1, keepdims=True)
    d = r - mu
    var = jnp.mean(d * d, axis=-1, keepdims=True)
    return d * lax.rsqrt(var + LN_EPS) * g + b


def _mm_kernel(x_ref, w_ref, o_ref):
    o_ref[...] = jnp.dot(x_ref[...].astype(jnp.bfloat16), w_ref[...], preferred_element_type=jnp.float32)


def _mm_ln_kernel(x_ref, w_ref, res_ref, g_ref, b_ref, o_ref):
    acc = jnp.dot(x_ref[...].astype(jnp.bfloat16), w_ref[...], preferred_element_type=jnp.float32)
    o_ref[...] = _layer_norm_rows(DEEPNORM_ALPHA * res_ref[...] + acc, g_ref[...], b_ref[...])


def _row_tile(m):
    return 512 if m % 512 == 0 else m


def _matmul(x, w):
    m, k = x.shape
    n = w.shape[1]
    tm = _row_tile(m)
    tn = n
    for cand in (1152, 1024, 768, 512):
        if n % cand == 0:
            tn = cand
            break
    return pl.pallas_call(
        _mm_kernel,
        grid=(m // tm, n // tn),
        in_specs=[pl.BlockSpec((tm, k), lambda i, j: (i, 0)),
                  pl.BlockSpec((k, tn), lambda i, j: (0, j))],
        out_specs=pl.BlockSpec((tm, tn), lambda i, j: (i, j)),
        out_shape=jax.ShapeDtypeStruct((m, n), jnp.float32),
        compiler_params=pltpu.CompilerParams(dimension_semantics=("parallel", "arbitrary"),
                                             vmem_limit_bytes=VMEM_LIMIT_BYTES),
        name="matmul",
    )(x, w.astype(jnp.bfloat16))


def _matmul_ln(x, w, res, g, b):
    m, k = x.shape
    n = w.shape[1]
    tm = _row_tile(m)
    return pl.pallas_call(
        _mm_ln_kernel,
        grid=(m // tm,),
        in_specs=[pl.BlockSpec((tm, k), lambda i: (i, 0)),
                  pl.BlockSpec((k, n), lambda i: (0, 0)),
                  pl.BlockSpec((tm, n), lambda i: (i, 0)),
                  pl.BlockSpec((1, n), lambda i: (0, 0)),
                  pl.BlockSpec((1, n), lambda i: (0, 0))],
        out_specs=pl.BlockSpec((tm, n), lambda i: (i, 0)),
        out_shape=jax.ShapeDtypeStruct((m, n), jnp.float32),
        compiler_params=pltpu.CompilerParams(dimension_semantics=("arbitrary",),
                                             vmem_limit_bytes=VMEM_LIMIT_BYTES),
        name="matmul_ln",
    )(x, w.astype(jnp.bfloat16), res, g.reshape(1, n), b.reshape(1, n))


FFN_CHUNK = D_FF // 2
FFN_NCHUNK = D_FF // FFN_CHUNK


def _ffn_seq_kernel(x_ref, wa_ref, wg_ref, cwa_ref, cwg_ref, cba_ref, cbg_ref, wo_ref, lg_ref, lb_ref,
                    y_ref, ha_ref, hg_ref, acc_ref, carry_ref):
    i, j = pl.program_id(0), pl.program_id(1)
    tm = x_ref.shape[0]
    x = x_ref[...]
    xb = x.astype(jnp.bfloat16)

    @pl.when(i == 0)
    def _():
        carry_ref[j] = jnp.zeros(carry_ref.shape[1:], jnp.float32)

    def conv(u, cw_ref, cb_ref, slot):
        prev = carry_ref[j, slot]
        p2, p1 = prev[6:7], prev[7:8]
        row = lax.broadcasted_iota(jnp.int32, u.shape, 0)
        u1 = jnp.where(row == 0, p1, pltpu.roll(u, 1, 0))
        u2 = jnp.where(row == 0, p2, jnp.where(row == 1, p1, pltpu.roll(u, 2, 0)))
        carry_ref[j, slot] = u[tm - 8:]
        cw = cw_ref[...]
        return cw[0:1] * u2 + cw[1:2] * u1 + cw[2:3] * u + cb_ref[...]

    ua = jnp.dot(xb, wa_ref[...], preferred_element_type=jnp.float32)
    ug = jnp.dot(xb, wg_ref[...], preferred_element_type=jnp.float32)
    ha_ref[...] = ua[tm - 8:]
    hg_ref[...] = ug[tm - 8:]
    a = conv(ua, cwa_ref, cba_ref, 0)
    g = conv(ug, cwg_ref, cbg_ref, 1)
    h = (a * jax.nn.sigmoid(a) * g).astype(jnp.bfloat16)
    part = jnp.dot(h, wo_ref[...], preferred_element_type=jnp.float32)

    @pl.when(j == 0)
    def _():
        acc_ref[...] = part

    @pl.when(j > 0)
    def _():
        acc_ref[...] += part

    @pl.when(j == pl.num_programs(1) - 1)
    def _():
        y_ref[...] = _layer_norm_rows(DEEPNORM_ALPHA * x + acc_ref[...], lg_ref[...], lb_ref[...])


def _ffn_seq(x, w_in, conv_w, conv_b, w_out, ln_g, ln_b):
    l, d = x.shape
    tm = 512
    c, nc = FFN_CHUNK, FFN_NCHUNK
    w_in = w_in.astype(jnp.bfloat16)
    cw8 = jnp.zeros((8, 2 * D_FF), jnp.float32).at[:FFN_CONV].set(conv_w)
    cb = conv_b.reshape(1, 2 * D_FF)
    y, ha, hg = pl.pallas_call(
        _ffn_seq_kernel,
        grid=(l // tm, nc),
        in_specs=[pl.BlockSpec((tm, d), lambda i, j: (i, 0)),
                  pl.BlockSpec((d, c), lambda i, j: (0, j)),
                  pl.BlockSpec((d, c), lambda i, j: (0, j + nc)),
                  pl.BlockSpec((8, c), lambda i, j: (0, j)),
                  pl.BlockSpec((8, c), lambda i, j: (0, j + nc)),
                  pl.BlockSpec((1, c), lambda i, j: (0, j)),
                  pl.BlockSpec((1, c), lambda i, j: (0, j + nc)),
                  pl.BlockSpec((c, d), lambda i, j: (j, 0)),
                  pl.BlockSpec((1, d), lambda i, j: (0, 0)),
                  pl.BlockSpec((1, d), lambda i, j: (0, 0))],
        out_specs=[pl.BlockSpec((tm, d), lambda i, j: (i, 0)),
                   pl.BlockSpec((8, c), lambda i, j: (i, j)),
                   pl.BlockSpec((8, c), lambda i, j: (i, j))],
        out_shape=[jax.ShapeDtypeStruct((l, d), jnp.float32),
                   jax.ShapeDtypeStruct((l // tm * 8, D_FF), jnp.float32),
                   jax.ShapeDtypeStruct((l // tm * 8, D_FF), jnp.float32)],
        scratch_shapes=[pltpu.VMEM((tm, d), jnp.float32),
                        pltpu.VMEM((nc, 2, 8, c), jnp.float32)],
        compiler_params=pltpu.CompilerParams(dimension_semantics=("arbitrary", "arbitrary"),
                                             vmem_limit_bytes=VMEM_LIMIT_BYTES),
        name="ffn_seq",
    )(x, w_in, w_in, cw8, cw8, cb, cb, w_out.astype(jnp.bfloat16), ln_g.reshape(1, d), ln_b.reshape(1, d))
    hist = jnp.concatenate([ha[-(FFN_CONV - 1):], hg[-(FFN_CONV - 1):]], axis=-1)
    return y, hist


def _ffn_step_kernel(x_ref, h_ref, wa_ref, wg_ref, cwa_ref, cwg_ref, cba_ref, cbg_ref, wo_ref, lg_ref, lb_ref,
                     y_ref, ua_ref, ug_ref, acc_ref):
    j = pl.program_id(0)
    x = x_ref[...]
    xb = x.astype(jnp.bfloat16)
    ua = jnp.dot(xb, wa_ref[...], preferred_element_type=jnp.float32)
    ug = jnp.dot(xb, wg_ref[...], preferred_element_type=jnp.float32)
    ua_ref[...] = ua
    ug_ref[...] = ug
    cwa, cwg = cwa_ref[...], cwg_ref[...]
    a = cwa[0:1] * h_ref[0, 0] + cwa[1:2] * h_ref[1, 0] + cwa[2:3] * ua + cba_ref[...]
    g = cwg[0:1] * h_ref[0, 1] + cwg[1:2] * h_ref[1, 1] + cwg[2:3] * ug + cbg_ref[...]
    h = (a * jax.nn.sigmoid(a) * g).astype(jnp.bfloat16)
    part = jnp.dot(h, wo_ref[...], preferred_element_type=jnp.float32)

    @pl.when(j == 0)
    def _():
        acc_ref[...] = part

    @pl.when(j > 0)
    def _():
        acc_ref[...] += part

    @pl.when(j == pl.num_programs(0) - 1)
    def _():
        y_ref[...] = _layer_norm_rows(DEEPNORM_ALPHA * x + acc_ref[...], lg_ref[...], lb_ref[...])


def _ffn_step(x, hist, w_in, conv_w, conv_b, w_out, ln_g, ln_b):
    b, d = x.shape
    c, nc = FFN_CHUNK, FFN_NCHUNK
    w_in = w_in.astype(jnp.bfloat16)
    cw8 = jnp.zeros((8, 2 * D_FF), jnp.float32).at[:FFN_CONV].set(conv_w)
    cb = conv_b.reshape(1, 2 * D_FF)
    h4 = jnp.transpose(hist, (1, 0, 2)).reshape(2, b, 2, D_FF).transpose(0, 2, 1, 3)
    y, ua, ug = pl.pallas_call(
        _ffn_step_kernel,
        grid=(nc,),
        in_specs=[pl.BlockSpec((b, d), lambda j: (0, 0)),
                  pl.BlockSpec((2, 2, b, c), lambda j: (0, 0, 0, j)),
                  pl.BlockSpec((d, c), lambda j: (0, j)),
                  pl.BlockSpec((d, c), lambda j: (0, j + nc)),
                  pl.BlockSpec((8, c), lambda j: (0, j)),
                  pl.BlockSpec((8, c), lambda j: (0, j + nc)),
                  pl.BlockSpec((1, c), lambda j: (0, j)),
                  pl.BlockSpec((1, c), lambda j: (0, j + nc)),
                  pl.BlockSpec((c, d), lambda j: (j, 0)),
                  pl.BlockSpec((1, d), lambda j: (0, 0)),
                  pl.BlockSpec((1, d), lambda j: (0, 0))],
        out_specs=[pl.BlockSpec((b, d), lambda j: (0, 0)),
                   pl.BlockSpec((b, c), lambda j: (0, j)),
                   pl.BlockSpec((b, c), lambda j: (0, j))],
        out_shape=[jax.ShapeDtypeStruct((b, d), jnp.float32),
                   jax.ShapeDtypeStruct((b, D_FF), jnp.float32),
                   jax.ShapeDtypeStruct((b, D_FF), jnp.float32)],
        scratch_shapes=[pltpu.VMEM((b, d), jnp.float32)],
        compiler_params=pltpu.CompilerParams(dimension_semantics=("arbitrary",),
                                             vmem_limit_bytes=VMEM_LIMIT_BYTES),
        name="ffn_step",
    )(x, h4, w_in, w_in, cw8, cw8, cb, cb, w_out.astype(jnp.bfloat16), ln_g.reshape(1, d), ln_b.reshape(1, d))
    u = jnp.concatenate([ua, ug], axis=-1)
    return y, jnp.concatenate([hist[:, 1:], u[:, None]], axis=1)


NEG = -1e30
NSA_KT = 512
NSA_COLS = NSA_HEADS * NSA_QBLOCK
NSA_WSPAN = NSA_WINDOW + NSA_QBLOCK


def _lane_tile(x, n):
    return jnp.concatenate([x] * n, axis=1)


def _nsa_prompt_kernel(q_ref, sm_ref, ck_ref, cvt_ref, ks_ref, vst_ref, kw_ref, vwt_ref, tri_ref, o_ref,
                       selb_ref, m_ref, l_ref, acc_ref, *, ns):
    f32, bf16 = jnp.float32, jnp.bfloat16
    qb = NSA_QBLOCK
    i = pl.program_id(0)
    s0 = i * qb
    half = NSA_COLS // 2

    qt = (q_ref[...] * (HEAD_DIM ** -0.5)).T
    zero = jnp.zeros((HEAD_DIM, qb), f32)
    top = jnp.concatenate([qt[h * HEAD_DIM:(h + 1) * HEAD_DIM] for h in range(NSA_GROUP)] + [zero] * NSA_GROUP, axis=1)
    bot = jnp.concatenate([zero] * NSA_GROUP + [qt[h * HEAD_DIM:(h + 1) * HEAD_DIM]
                                                for h in range(NSA_GROUP, NSA_HEADS)], axis=1)
    qbd = jnp.concatenate([top, bot], axis=0).astype(bf16)

    def softmax_cols(s):
        m = jnp.max(s, axis=0, keepdims=True)
        p = jnp.exp(s - m)
        return m, p, jnp.sum(p, axis=0, keepdims=True)

    def pv(vt, p):
        pb = p.astype(bf16)
        return [jnp.dot(vt[g * HEAD_DIM:(g + 1) * HEAD_DIM], pb[:, g * half:(g + 1) * half],
                        preferred_element_type=f32) for g in range(NSA_KV_HEADS)]

    nc = 2 * ns
    r = lax.broadcasted_iota(jnp.int32, (nc, qb), 0)
    lane = lax.broadcasted_iota(jnp.int32, (nc, qb), 1)
    cidx = jnp.where(r < ns, 2 * r, 2 * (r - ns) + 1)
    cbias = jnp.where((cidx + 1) * CMP_BLOCK - 1 <= s0 + lane, 0.0, NEG)
    sc = jnp.dot(ck_ref[...], qbd, preferred_element_type=f32) + _lane_tile(cbias, NSA_HEADS)
    m, p, l = softmax_cols(sc)
    pn = p * jnp.where(m > 0.5 * NEG, 1.0 / l, 0.0)
    o_cmp = pv(cvt_ref[...], pn)

    blk = lax.broadcasted_iota(jnp.int32, (ns, qb), 0)
    qpos = s0 + lax.broadcasted_iota(jnp.int32, (ns, qb), 1)
    cur = qpos // SEL_BLOCK
    forced = (blk == 0) | (blk == cur) | (blk == cur - 1)
    force_key = lax.bitcast_convert_type(jnp.full((ns, qb), NSA_FORCE, f32), jnp.int32)
    for g in range(NSA_KV_HEADS):
        imp = pn[:, g * half:g * half + qb]
        for h in range(1, NSA_GROUP):
            imp = imp + pn[:, g * half + h * qb:g * half + (h + 1) * qb]
        imp = imp[:ns] + imp[ns:]
        key = jnp.where(blk > cur, -1, jnp.where(forced, force_key, lax.bitcast_convert_type(imp, jnp.int32)))
        thr = jnp.zeros((1, qb), jnp.int32)
        for bit in range(30, -1, -1):
            cand = thr | (1 << bit)
            cnt = jnp.sum(jnp.where(key >= cand, 1.0, 0.0), axis=0, keepdims=True)
            thr = jnp.where(cnt >= NSA_TOPN, cand, thr)
        above = key > thr
        n_above = jnp.sum(jnp.where(above, 1.0, 0.0), axis=0, keepdims=True)
        tie = key == thr
        rank = jnp.dot(tri_ref[...], jnp.where(tie, 1.0, 0.0).astype(bf16), preferred_element_type=f32)
        sel = above | (tie & (rank <= NSA_TOPN - n_above))
        selb_ref[g] = jnp.where(sel, 0.0, NEG)

    m_ref[...] = jnp.full(m_ref.shape, NEG, f32)
    l_ref[...] = jnp.zeros(l_ref.shape, f32)
    acc_ref[...] = jnp.zeros(acc_ref.shape, f32)
    per_tile = NSA_KT // SEL_BLOCK

    def slc_tile(kt, causal):
        k0 = pl.multiple_of(kt * NSA_KT, NSA_KT)
        s = jnp.dot(ks_ref[pl.ds(k0, NSA_KT), :], qbd, preferred_element_type=f32)
        rows = []
        for b in range(per_tile):
            brow = jnp.concatenate([selb_ref[g, pl.ds(kt * per_tile + b, 1), :] for g in range(NSA_KV_HEADS)
                                    for _ in range(NSA_GROUP)], axis=1)
            rows.append(s[b * SEL_BLOCK:(b + 1) * SEL_BLOCK] + brow)
        s = jnp.concatenate(rows, axis=0)
        if causal:
            kpos = k0 + lax.broadcasted_iota(jnp.int32, (NSA_KT, qb), 0)
            qq = s0 + lax.broadcasted_iota(jnp.int32, (NSA_KT, qb), 1)
            s = s + _lane_tile(jnp.where(kpos <= qq, 0.0, NEG), NSA_HEADS)
        m_old = m_ref[...]
        m_new = jnp.maximum(m_old, jnp.max(s, axis=0, keepdims=True))
        alpha = jnp.exp(m_old - m_new)
        p = jnp.exp(s - m_new)
        m_ref[...] = m_new
        l_ref[...] = l_ref[...] * alpha + jnp.sum(p, axis=0, keepdims=True)
        upd = pv(vst_ref[:, pl.ds(k0, NSA_KT)], p)
        for g in range(NSA_KV_HEADS):
            acc_ref[g] = acc_ref[g] * alpha[:, g * half:(g + 1) * half] + upd[g]

    kd = s0 // NSA_KT

    def body(kt, carry):
        slc_tile(kt, False)
        return carry

    lax.fori_loop(0, kd, body, 0)
    slc_tile(kd, True)
    inv_slc = 1.0 / l_ref[...]

    w0 = pl.multiple_of(s0, qb)
    sw = jnp.dot(kw_ref[pl.ds(w0, NSA_WSPAN), :], qbd, preferred_element_type=f32)
    rr = lax.broadcasted_iota(jnp.int32, (NSA_WSPAN, qb), 0)
    qi = lax.broadcasted_iota(jnp.int32, (NSA_WSPAN, qb), 1)
    ok = (rr >= qi) & (rr <= qi + NSA_WINDOW) & (rr + s0 >= NSA_WINDOW)
    sw = sw + _lane_tile(jnp.where(ok, 0.0, NEG), NSA_HEADS)
    _, pw, lw = softmax_cols(sw)
    o_win = pv(vwt_ref[:, pl.ds(w0, NSA_WSPAN)], pw)
    inv_win = 1.0 / lw

    gt = jax.nn.sigmoid(sm_ref[...].T)
    outs = []
    for h in range(NSA_HEADS):
        g, hg = divmod(h, NSA_GROUP)
        c0, c1 = hg * qb, (hg + 1) * qb
        g_cmp = gt[h:h + 1]
        g_slc = gt[NSA_HEADS + h:NSA_HEADS + h + 1] * inv_slc[:, g * half + c0:g * half + c1]
        g_win = gt[2 * NSA_HEADS + h:2 * NSA_HEADS + h + 1] * inv_win[:, g * half + c0:g * half + c1]
        outs.append(o_cmp[g][:, c0:c1] * g_cmp + acc_ref[g, :, c0:c1] * g_slc + o_win[g][:, c0:c1] * g_win)
    o_ref[...] = jnp.concatenate(outs, axis=0).T


def _nsa_prompt_attention(qr, small, ck, cv, ksr, vs, kwr, vw):
    l = qr.shape[0]
    ns = l // SEL_BLOCK
    nc = 2 * ns
    bf16 = jnp.bfloat16
    perm = jnp.concatenate([jnp.arange(0, nc, 2), jnp.arange(1, nc, 2)])
    ckp = ck[perm].astype(bf16)
    cvt = cv[perm].T.astype(bf16)
    pad = jnp.zeros((NSA_WINDOW, NSA_KV_W), bf16)
    kwp = jnp.concatenate([pad, kwr.astype(bf16)], axis=0)
    vwt = jnp.concatenate([pad, vw.astype(bf16)], axis=0).T
    tri = (jnp.arange(ns)[:, None] >= jnp.arange(ns)[None, :]).astype(bf16)
    full = lambda a: pl.BlockSpec(a.shape, lambda i: (0,) * a.ndim)
    args = (qr, small, ckp, cvt, ksr.astype(bf16), vs.T.astype(bf16), kwp, vwt, tri)
    return pl.pallas_call(
        functools.partial(_nsa_prompt_kernel, ns=ns),
        grid=(l // NSA_QBLOCK,),
        in_specs=[pl.BlockSpec((NSA_QBLOCK, NSA_Q_W), lambda i: (i, 0)),
                  pl.BlockSpec((NSA_QBLOCK, LANES), lambda i: (i, 0))] + [full(a) for a in args[2:]],
        out_specs=pl.BlockSpec((NSA_QBLOCK, NSA_Q_W), lambda i: (i, 0)),
        out_shape=jax.ShapeDtypeStruct((l, NSA_Q_W), jnp.float32),
        scratch_shapes=[pltpu.VMEM((NSA_KV_HEADS, ns, NSA_QBLOCK), jnp.float32),
                        pltpu.VMEM((1, NSA_COLS), jnp.float32),
                        pltpu.VMEM((1, NSA_COLS), jnp.float32),
                        pltpu.VMEM((NSA_KV_HEADS, HEAD_DIM, NSA_COLS // 2), jnp.float32)],
        compiler_params=pltpu.CompilerParams(dimension_semantics=("arbitrary",),
                                             vmem_limit_bytes=VMEM_LIMIT_BYTES),
        name="nsa_prompt",
    )(*args)


PAGE_W = 2 * NSA_KV_W
CMP_HIDDEN = 2 * HEAD_DIM
CMP_PER_PAGE = PAGE_SIZE // CMP_BLOCK
CMP_FLAT = CMP_BLOCK * PAGE_W
CMP_PAGES_PER_STEP = 32
STEP_PAGES = 64
STEP_ROWS = 16


def _rope_tables(pos, width):
    half = HEAD_DIM // 2
    inv_freq = ROPE_THETA ** (-2.0 * jnp.arange(half, dtype=jnp.float32) / HEAD_DIM)
    ang = pos.astype(jnp.float32)[:, None] * inv_freq[None, :]
    cos, sin = jnp.cos(ang), jnp.sin(ang)
    reps = width // HEAD_DIM
    return (jnp.tile(jnp.concatenate([cos, cos], axis=1), (1, reps)),
            jnp.tile(jnp.concatenate([-sin, sin], axis=1), (1, reps)))


def _rope_lanes(x, cos, sin_signed):
    n = x.shape[-1]
    lane = lax.broadcasted_iota(jnp.int32, x.shape, x.ndim - 1)
    first = (lane % HEAD_DIM) < HEAD_DIM // 2
    partner = jnp.where(first, pltpu.roll(x, n - HEAD_DIM // 2, x.ndim - 1), pltpu.roll(x, HEAD_DIM // 2, x.ndim - 1))
    return x * cos + partner * sin_signed


def _cmp_step_kernel(pt_ref, *refs):
    npg = CMP_PAGES_PER_STEP
    pages = refs[:npg]
    pe_ref, w1_ref, b1_ref, w2_ref, cos_ref, sin_ref, ck_ref, cv_ref = refs[npg:]
    x = jnp.concatenate([r[0] for r in pages], axis=0) + pe_ref[...]
    h = jnp.dot(x.astype(jnp.bfloat16), w1_ref[...], preferred_element_type=jnp.float32) + b1_ref[...]
    c = jnp.dot(jax.nn.gelu(h).astype(jnp.bfloat16), w2_ref[...], preferred_element_type=jnp.float32)
    ck_ref[0] = _rope_lanes(c[:, :NSA_KV_W], cos_ref[...], sin_ref[...])
    cv_ref[0] = c[:, NSA_KV_W:]


def _compress_weights(cw1, cb1, cw2, cpe):
    eye = jnp.eye(2, dtype=jnp.float32)
    w1r = cw1.reshape(2, CMP_BLOCK, HEAD_DIM, CMP_HIDDEN)
    w1 = jnp.einsum('ktdj,ka,gb->tkgdabj', w1r, eye, eye).reshape(CMP_FLAT, 4 * CMP_HIDDEN)
    b1 = jnp.broadcast_to(cb1[:, None, :], (2, 2, CMP_HIDDEN)).reshape(1, 4 * CMP_HIDDEN)
    w2 = jnp.einsum('kjd,ka,gb->kgjabd', cw2, eye, eye).reshape(4 * CMP_HIDDEN, PAGE_W)
    pe = jnp.broadcast_to(cpe.transpose(1, 0, 2)[:, :, None, :], (CMP_BLOCK, 2, 2, HEAD_DIM)).reshape(1, CMP_FLAT)
    return w1.astype(jnp.bfloat16), b1, w2.astype(jnp.bfloat16), pe


def _nsa_sample_compress(pool, layer_idx, page_table, cw1, cb1, cw2, cpe):
    n_pool, nl = pool.shape[:2]
    db, n_pages = page_table.shape
    npg = CMP_PAGES_PER_STEP
    nchunk = n_pages // npg
    nc = n_pages * CMP_PER_PAGE
    view = pool.reshape(n_pool * nl, CMP_PER_PAGE, CMP_FLAT)
    pt = (page_table * nl + layer_idx).reshape(-1).astype(jnp.int32)
    w1, b1, w2, pe = _compress_weights(cw1, cb1, cw2, cpe)
    cos, sin = _rope_tables((jnp.arange(nc) + 1) * CMP_BLOCK - 1, NSA_KV_W)
    rows = npg * CMP_PER_PAGE

    def page_map(k):
        return lambda b, c, pt_ref: (pt_ref[b * n_pages + c * npg + k], 0, 0)

    const = lambda a: pl.BlockSpec(a.shape, lambda b, c, pt_ref: (0,) * a.ndim, pipeline_mode=pl.Buffered(1))
    grid_spec = pltpu.PrefetchScalarGridSpec(
        num_scalar_prefetch=1, grid=(db, nchunk),
        in_specs=[pl.BlockSpec((1, CMP_PER_PAGE, CMP_FLAT), page_map(k)) for k in range(npg)]
        + [const(pe), const(w1), const(b1), const(w2),
           pl.BlockSpec((rows, NSA_KV_W), lambda b, c, pt_ref: (c, 0)),
           pl.BlockSpec((rows, NSA_KV_W), lambda b, c, pt_ref: (c, 0))],
        out_specs=[pl.BlockSpec((1, rows, NSA_KV_W), lambda b, c, pt_ref: (b, c, 0))] * 2)
    return pl.pallas_call(
        _cmp_step_kernel, grid_spec=grid_spec,
        out_shape=[jax.ShapeDtypeStruct((db, nc, NSA_KV_W), jnp.float32)] * 2,
        compiler_params=pltpu.CompilerParams(dimension_semantics=("arbitrary", "arbitrary"),
                                             vmem_limit_bytes=VMEM_LIMIT_BYTES),
        name="nsa_sample_compress",
    )(pt, *([view] * npg), pe, w1, b1, w2, cos, sin)


def _nt_dot(a, b):
    return lax.dot_general(a, b, (((1,), (1,)), ((), ())), preferred_element_type=jnp.float32)


def _nsa_step_kernel(pt_ref, *refs, cur, nsl):
    f32, bf16 = jnp.float32, jnp.bfloat16
    npg = STEP_PAGES
    q_ref, ck_ref, cv_ref = refs[:3]
    pages = refs[3:3 + npg]
    (win_ref, new_ref, gate_ref, exp_ref, triu_ref, o_ref, wout_ref,
     selt_ref, m_ref, l_ref, acc_ref, side_ref) = refs[3 + npg:]
    cc = pl.program_id(1)
    q16 = q_ref[0]
    qb = q16.astype(bf16)
    row16 = lax.broadcasted_iota(jnp.int32, (STEP_ROWS, 1), 0)

    def new_key_scores(krow):
        return jnp.sum(q16 * krow, axis=1, keepdims=True)

    @pl.when(cc == 0)
    def _():
        nch = ck_ref.shape[1] // 2
        halves = lambda r: jnp.concatenate([r[0, pl.ds(0, nch, stride=2), :], r[0, pl.ds(1, nch, stride=2), :]], axis=0)
        ck, cv = halves(ck_ref), halves(cv_ref)
        s = _nt_dot(qb, ck.astype(bf16))
        p = jnp.exp(s - jnp.max(s, axis=1, keepdims=True))
        pn = p / jnp.sum(p, axis=1, keepdims=True)
        o_cmp = jnp.dot(pn.astype(bf16), cv.astype(bf16), preferred_element_type=f32)

        rowp = lax.broadcasted_iota(jnp.int32, pn.shape, 0)
        row8 = lax.broadcasted_iota(jnp.int32, (8, nsl), 0)
        blk = lax.broadcasted_iota(jnp.int32, (8, nsl), 1)
        forced = (blk == 0) | (blk == cur) | (blk == cur - 1)
        key = jnp.full((8, nsl), -1, jnp.int32)
        for g in range(NSA_KV_HEADS):
            ig = jnp.sum(jnp.where((rowp >= g * NSA_GROUP) & (rowp < (g + 1) * NSA_GROUP), pn, 0.0),
                         axis=0, keepdims=True)
            ig = ig[:, :nch] + ig[:, nch:]
            ig = jnp.concatenate([ig, jnp.zeros((1, nsl - nch), f32)], axis=1)
            kg = jnp.where(blk[:1] > cur, -1,
                           jnp.where(forced[:1], lax.bitcast_convert_type(jnp.full((1, nsl), NSA_FORCE, f32), jnp.int32),
                                     lax.bitcast_convert_type(ig, jnp.int32)))
            key = jnp.where(row8 == g, kg, key)
        thr = jnp.zeros((8, 1), jnp.int32)
        for bit in range(30, -1, -1):
            cand = thr | (1 << bit)
            cnt = jnp.sum(jnp.where(key >= cand, 1.0, 0.0), axis=1, keepdims=True)
            thr = jnp.where(cnt >= NSA_TOPN, cand, thr)
        above = key > thr
        n_above = jnp.sum(jnp.where(above, 1.0, 0.0), axis=1, keepdims=True)
        tie = key == thr
        rank = jnp.dot(jnp.where(tie, 1.0, 0.0).astype(bf16), triu_ref[...], preferred_element_type=f32)
        sel = jnp.where(above | (tie & (rank <= NSA_TOPN - n_above)), 1.0, 0.0)
        selh = jnp.where(row16 < NSA_GROUP, sel[0:1], jnp.where(row16 < NSA_HEADS, sel[1:2], 0.0))
        for j in range(selt_ref.shape[0]):
            selt_ref[j] = selh[:, j * LANES:(j + 1) * LANES]

        m_ref[...] = new_key_scores(new_ref[0, 0:1, :])
        l_ref[...] = jnp.ones(l_ref.shape, f32)
        acc_ref[...] = jnp.broadcast_to(new_ref[0, 1:2, :], acc_ref.shape)

        win = win_ref[0]
        sw = _nt_dot(qb, win[:, :NSA_KV_W].astype(bf16))
        sn = new_key_scores(new_ref[0, 2:3, :])
        mw = jnp.maximum(jnp.max(sw, axis=1, keepdims=True), sn)
        pw, pnw = jnp.exp(sw - mw), jnp.exp(sn - mw)
        lw = jnp.sum(pw, axis=1, keepdims=True) + pnw
        o_win = (jnp.dot(pw.astype(bf16), win[:, NSA_KV_W:].astype(bf16), preferred_element_type=f32)
                 + pnw * new_ref[0, 3:4, :]) / lw
        gt = jax.nn.sigmoid(gate_ref[0])
        side_ref[...] = gt[:, 0:1] * o_cmp + gt[:, 2:3] * o_win
        rw = lax.broadcasted_iota(jnp.int32, win.shape, 0)
        newrow = jnp.concatenate([new_ref[0, 2:3, :], new_ref[0, 3:4, :]], axis=1)
        wout_ref[0] = jnp.where(rw == win.shape[0] - 1, newrow, pltpu.roll(win, win.shape[0] - 1, 0))

    kv = jnp.concatenate([r[0] for r in pages], axis=0)
    s = _nt_dot(qb, kv[:, :NSA_KV_W].astype(bf16))
    picked = jnp.dot(selt_ref[cc].astype(bf16), exp_ref[...], preferred_element_type=f32)
    s = s + (picked - 1.0) * (-NEG)
    m_old = m_ref[...]
    m_new = jnp.maximum(m_old, jnp.max(s, axis=1, keepdims=True))
    alpha = jnp.exp(m_old - m_new)
    p = jnp.exp(s - m_new)
    m_ref[...] = m_new
    l_ref[...] = l_ref[...] * alpha + jnp.sum(p, axis=1, keepdims=True)
    acc_ref[...] = acc_ref[...] * alpha + jnp.dot(p.astype(bf16), kv[:, NSA_KV_W:].astype(bf16),
                                                  preferred_element_type=f32)

    @pl.when(cc == pl.num_programs(1) - 1)
    def _():
        gt = jax.nn.sigmoid(gate_ref[0])
        o_ref[0] = side_ref[...] + gt[:, 1:2] * acc_ref[...] / l_ref[...]


def _nsa_sample_attention(qr, ck, cv, slc_pool, layer_idx, page_table, win_buf, newrows, gate):
    db, n_pages = page_table.shape
    n_pool, nl = slc_pool.shape[:2]
    past = n_pages * PAGE_SIZE
    cur = past // SEL_BLOCK
    nsl = -(-(cur + 1) // LANES) * LANES
    npg = STEP_PAGES
    nchunk = n_pages // npg
    keys = npg * PAGE_SIZE
    view = slc_pool.reshape(n_pool * nl, PAGE_SIZE, PAGE_W)
    pt = (page_table * nl + layer_idx).reshape(-1).astype(jnp.int32)
    f32 = jnp.float32
    hmask = (jnp.arange(NSA_HEADS)[:, None] // NSA_GROUP == jnp.arange(NSA_KV_HEADS)[None, :]).astype(f32)
    q16 = (qr * HEAD_DIM ** -0.5)[:, :, None, :] * hmask[None, :, :, None]
    q16 = jnp.pad(q16.reshape(db, NSA_HEADS, NSA_KV_W), ((0, 0), (0, STEP_ROWS - NSA_HEADS), (0, 0)))
    new8 = jnp.pad(newrows, ((0, 0), (0, 8 - newrows.shape[1]), (0, 0)))
    g16 = jnp.pad(gate.reshape(db, 3, NSA_HEADS).transpose(0, 2, 1),
                  ((0, 0), (0, STEP_ROWS - NSA_HEADS), (0, LANES - 3)))
    expand = (jnp.arange(LANES)[:, None] == jnp.arange(keys)[None, :] // SEL_BLOCK).astype(jnp.bfloat16)
    triu = (jnp.arange(nsl)[:, None] <= jnp.arange(nsl)[None, :]).astype(jnp.bfloat16)
    wlen = win_buf.shape[1]

    def page_map(k):
        return lambda b, c, pt_ref: (pt_ref[b * n_pages + c * npg + k], 0, 0)

    per_b = lambda shp: pl.BlockSpec((1,) + shp, lambda b, c, pt_ref: (b, 0, 0))
    const = lambda a: pl.BlockSpec(a.shape, lambda b, c, pt_ref: (0,) * a.ndim)
    grid_spec = pltpu.PrefetchScalarGridSpec(
        num_scalar_prefetch=1, grid=(db, nchunk),
        in_specs=[per_b((STEP_ROWS, NSA_KV_W)), per_b(ck.shape[1:]), per_b(cv.shape[1:])]
        + [pl.BlockSpec((1, PAGE_SIZE, PAGE_W), page_map(k)) for k in range(npg)]
        + [per_b((wlen, PAGE_W)), per_b((8, NSA_KV_W)), per_b((STEP_ROWS, LANES)), const(expand), const(triu)],
        out_specs=[per_b((STEP_ROWS, NSA_KV_W)), per_b((wlen, PAGE_W))],
        scratch_shapes=[pltpu.VMEM((nsl // LANES, STEP_ROWS, LANES), f32),
                        pltpu.VMEM((STEP_ROWS, 1), f32), pltpu.VMEM((STEP_ROWS, 1), f32),
                        pltpu.VMEM((STEP_ROWS, NSA_KV_W), f32), pltpu.VMEM((STEP_ROWS, NSA_KV_W), f32)])
    o16, wout = pl.pallas_call(
        functools.partial(_nsa_step_kernel, cur=cur, nsl=nsl), grid_spec=grid_spec,
        out_shape=[jax.ShapeDtypeStruct((db, STEP_ROWS, NSA_KV_W), f32),
                   jax.ShapeDtypeStruct((db, wlen, PAGE_W), f32)],
        compiler_params=pltpu.CompilerParams(dimension_semantics=("arbitrary", "arbitrary"),
                                             vmem_limit_bytes=VMEM_LIMIT_BYTES),
        name="nsa_sample_attention",
    )(pt, q16, ck, cv, *([view] * npg), win_buf, new8, g16, expand, triu)
    o = o16[:, :NSA_HEADS].reshape(db, NSA_HEADS, NSA_KV_HEADS, HEAD_DIM)
    o = jnp.take_along_axis(o, (jnp.arange(NSA_HEADS) // NSA_GROUP)[None, :, None, None], axis=2)
    return o.reshape(db, NSA_HEADS * HEAD_DIM), wout


CMP_STEP_LANES = CMP_PAGES_PER_STEP * CMP_PER_PAGE
CMP_FEATURE_GROUP = 16


def _cmp_lane_blocks(n_pages):
    lane = np.arange(n_pages * CMP_PER_PAGE)
    step, rem = lane // CMP_STEP_LANES, lane % CMP_STEP_LANES
    j, pl_ = rem // CMP_PAGES_PER_STEP, rem % CMP_PAGES_PER_STEP
    return (step * CMP_PAGES_PER_STEP + pl_) * CMP_PER_PAGE + j


def _cmp_step_t_kernel(pt_ref, *refs):
    f32 = jnp.float32
    npg = CMP_PAGES_PER_STEP
    pages = refs[:npg]
    pe_ref, w1_ref, b1_ref, w2_ref, cos_ref, sin_ref, ck_ref, cv_ref, slab_ref = refs[npg:]
    outs = (ck_ref, cv_ref)
    for k, r in enumerate(pages):
        for s in range(2 * NSA_KV_HEADS):
            slab_ref[s, k * HEAD_DIM:(k + 1) * HEAD_DIM, :] = r[0, s]
    for kv in range(2):
        h = jnp.zeros((NSA_KV_HEADS * npg, CMP_PER_PAGE * CMP_HIDDEN), f32)
        for dg in range(HEAD_DIM // CMP_FEATURE_GROUP):
            x = jnp.concatenate(
                [jnp.concatenate([slab_ref[2 * kv + g, pl.ds(d, npg, stride=HEAD_DIM), :]
                                  for g in range(NSA_KV_HEADS)], axis=0) + pe_ref[kv, d]
                 for d in range(dg * CMP_FEATURE_GROUP, (dg + 1) * CMP_FEATURE_GROUP)], axis=1)
            h = h + jnp.dot(x.astype(jnp.bfloat16), w1_ref[kv, dg], preferred_element_type=f32)
        act = jax.nn.gelu(h + b1_ref[kv]).astype(jnp.bfloat16)
        ct = _nt_dot(w2_ref[kv], act)
        tile = jnp.concatenate(
            [jnp.concatenate([ct[j * HEAD_DIM:(j + 1) * HEAD_DIM, g * npg:(g + 1) * npg] for j in range(CMP_PER_PAGE)],
                             axis=1) for g in range(NSA_KV_HEADS)], axis=0)
        if kv == 0:
            row = lax.broadcasted_iota(jnp.int32, tile.shape, 0)
            n = tile.shape[0]
            partner = jnp.where((row % HEAD_DIM) < HEAD_DIM // 2, pltpu.roll(tile, n - HEAD_DIM // 2, 0),
                                pltpu.roll(tile, HEAD_DIM // 2, 0))
            tile = tile * cos_ref[...] + partner * sin_ref[...]
        outs[kv][0] = tile


def _compress_weights_t(cw1, cb1, cw2, cpe):
    eye = jnp.eye(CMP_PER_PAGE, dtype=jnp.float32)
    w1r = cw1.reshape(2, CMP_BLOCK, HEAD_DIM, CMP_HIDDEN)
    w1 = jnp.einsum('ktdn,ja->kdjtan', w1r, eye).reshape(
        2, HEAD_DIM // CMP_FEATURE_GROUP, CMP_FEATURE_GROUP * PAGE_SIZE, CMP_PER_PAGE * CMP_HIDDEN)
    b1 = jnp.tile(cb1, (1, CMP_PER_PAGE))[:, None, :]
    w2 = jnp.einsum('knd,ja->kjdan', cw2, eye).reshape(2, CMP_PER_PAGE * HEAD_DIM, CMP_PER_PAGE * CMP_HIDDEN)
    pe = jnp.tile(cpe.transpose(0, 2, 1), (1, 1, CMP_PER_PAGE))[:, :, None, :]
    return w1.astype(jnp.bfloat16), b1, w2.astype(jnp.bfloat16), pe


def _nsa_sample_compress_t(pool, layer_idx, page_table, cw1, cb1, cw2, cpe):
    n_pool, nl = pool.shape[:2]
    db, n_pages = page_table.shape
    npg = CMP_PAGES_PER_STEP
    nchunk = n_pages // npg
    nc = n_pages * CMP_PER_PAGE
    view = jnp.transpose(pool, (0, 1, 3, 4, 5, 2)).reshape(n_pool * nl, 2 * NSA_KV_HEADS, HEAD_DIM, PAGE_SIZE)
    pt = (page_table * nl + layer_idx).reshape(-1).astype(jnp.int32)
    w1, b1, w2, pe = _compress_weights_t(cw1, cb1, cw2, cpe)
    pos = (jnp.asarray(_cmp_lane_blocks(n_pages)) + 1) * CMP_BLOCK - 1
    half = HEAD_DIM // 2
    inv_freq = ROPE_THETA ** (-2.0 * jnp.arange(half, dtype=jnp.float32) / HEAD_DIM)
    ang = inv_freq[:, None] * pos.astype(jnp.float32)[None, :]
    cos = jnp.tile(jnp.cos(ang), (2 * NSA_KV_HEADS, 1))
    sin = jnp.tile(jnp.concatenate([-jnp.sin(ang), jnp.sin(ang)], axis=0), (NSA_KV_HEADS, 1))

    def page_map(k):
        return lambda b, c, pt_ref: (pt_ref[b * n_pages + c * npg + k], 0, 0, 0)

    const = lambda a: pl.BlockSpec(a.shape, lambda b, c, pt_ref: (0,) * a.ndim, pipeline_mode=pl.Buffered(1))
    lanes_c = lambda: pl.BlockSpec((NSA_KV_W, CMP_STEP_LANES), lambda b, c, pt_ref: (0, c))
    grid_spec = pltpu.PrefetchScalarGridSpec(
        num_scalar_prefetch=1, grid=(db, nchunk),
        in_specs=[pl.BlockSpec((1, 2 * NSA_KV_HEADS, HEAD_DIM, PAGE_SIZE), page_map(k)) for k in range(npg)]
        + [const(pe), const(w1), const(b1), const(w2), lanes_c(), lanes_c()],
        out_specs=[pl.BlockSpec((1, NSA_KV_W, CMP_STEP_LANES), lambda b, c, pt_ref: (b, 0, c))] * 2,
        scratch_shapes=[pltpu.VMEM((2 * NSA_KV_HEADS, npg * HEAD_DIM, PAGE_SIZE), jnp.float32)])
    return pl.pallas_call(
        _cmp_step_t_kernel, grid_spec=grid_spec,
        out_shape=[jax.ShapeDtypeStruct((db, NSA_KV_W, nc), jnp.float32)] * 2,
        compiler_params=pltpu.CompilerParams(dimension_semantics=("arbitrary", "arbitrary"),
                                             vmem_limit_bytes=VMEM_LIMIT_BYTES),
        name="nsa_sample_compress",
    )(pt, *([view] * npg), pe, w1, b1, w2, cos, sin)


def _nsa_step_t_kernel(pt_ref, *refs, topn):
    f32, bf16 = jnp.float32, jnp.bfloat16
    npg = STEP_PAGES
    q_ref, ck_ref, cv_ref = refs[:3]
    pages = refs[3:3 + npg]
    (win_ref, newr_ref, newt_ref, gate_ref, blk_ref, exp_ref, rank_ref, o_ref, wout_ref,
     selt_ref, m_ref, l_ref, acc_ref, side_ref) = refs[3 + npg:]
    cc = pl.program_id(1)
    q16 = q_ref[0]
    qb = q16.astype(bf16)
    row16 = lax.broadcasted_iota(jnp.int32, (STEP_ROWS, 1), 0)

    def new_key_scores(krow):
        return jnp.sum(q16 * krow, axis=1, keepdims=True)

    @pl.when(cc == 0)
    def _():
        nc = ck_ref.shape[2]
        s = jnp.dot(qb, ck_ref[0].astype(bf16), preferred_element_type=f32)
        p = jnp.exp(s - jnp.max(s, axis=1, keepdims=True))
        pn = p / jnp.sum(p, axis=1, keepdims=True)
        o_cmp = _nt_dot(pn.astype(bf16), cv_ref[0].astype(bf16))

        rowp = lax.broadcasted_iota(jnp.int32, pn.shape, 0)
        row8 = lax.broadcasted_iota(jnp.int32, (8, nc), 0)
        blk = blk_ref[...]
        key = jnp.full((8, nc), -1, jnp.int32)
        force_key = lax.bitcast_convert_type(jnp.full((1, nc), NSA_FORCE, f32), jnp.int32)
        for g in range(NSA_KV_HEADS):
            ig = jnp.sum(jnp.where((rowp >= g * NSA_GROUP) & (rowp < (g + 1) * NSA_GROUP), pn, 0.0),
                         axis=0, keepdims=True)
            ig = ig + pltpu.roll(ig, nc - CMP_PAGES_PER_STEP, 1)
            kg = jnp.where(blk[:1] < 0, -1, jnp.where(blk[1:2] > 0, force_key, lax.bitcast_convert_type(ig, jnp.int32)))
            key = jnp.where(row8 == g, kg, key)
        thr = jnp.zeros((8, 1), jnp.int32)
        for bit in range(30, -1, -1):
            cand = thr | (1 << bit)
            cnt = jnp.sum(jnp.where(key >= cand, 1.0, 0.0), axis=1, keepdims=True)
            thr = jnp.where(cnt >= topn, cand, thr)
        above = key > thr
        n_above = jnp.sum(jnp.where(above, 1.0, 0.0), axis=1, keepdims=True)
        tie = key == thr
        rank = jnp.dot(jnp.where(tie, 1.0, 0.0).astype(bf16), rank_ref[...], preferred_element_type=f32)
        sel = jnp.where(above | (tie & (rank <= topn - n_above)), 1.0, 0.0)
        selh = jnp.where(row16 < NSA_GROUP, sel[0:1], jnp.where(row16 < NSA_HEADS, sel[1:2], 0.0))
        wsel = selt_ref.shape[2]
        for j in range(selt_ref.shape[0]):
            selt_ref[j] = selh[:, j * wsel:(j + 1) * wsel]

        m_ref[...] = new_key_scores(newr_ref[0, 0:1, :])
        l_ref[...] = jnp.ones(l_ref.shape, f32)
        acc_ref[...] = jnp.broadcast_to(newr_ref[0, 1:2, :], acc_ref.shape)

        sw = jnp.dot(qb, win_ref[0, 0].astype(bf16), preferred_element_type=f32)
        sn = new_key_scores(newr_ref[0, 2:3, :])
        mw = jnp.maximum(jnp.max(sw, axis=1, keepdims=True), sn)
        pw, pnw = jnp.exp(sw - mw), jnp.exp(sn - mw)
        lw = jnp.sum(pw, axis=1, keepdims=True) + pnw
        o_win = (_nt_dot(pw.astype(bf16), win_ref[0, 1].astype(bf16)) + pnw * newr_ref[0, 3:4, :]) / lw
        gt = jax.nn.sigmoid(gate_ref[0])
        side_ref[...] = gt[:, 0:1] * o_cmp + gt[:, 2:3] * o_win
        wl = win_ref.shape[3]
        lane = lax.broadcasted_iota(jnp.int32, (NSA_KV_W, wl), 1)
        for kv in range(2):
            wout_ref[0, kv] = jnp.where(lane == wl - 1, newt_ref[0, :, 2 + kv:3 + kv],
                                        pltpu.roll(win_ref[0, kv], wl - 1, 1))

    kt = jnp.concatenate([r[0, 0] for r in pages], axis=1)
    vt = jnp.concatenate([r[0, 1] for r in pages], axis=1)
    s = jnp.dot(qb, kt.astype(bf16), preferred_element_type=f32)
    picked = jnp.dot(selt_ref[cc].astype(bf16), exp_ref[...], preferred_element_type=f32)
    s = s + (picked - 1.0) * (-NEG)
    m_old = m_ref[...]
    m_new = jnp.maximum(m_old, jnp.max(s, axis=1, keepdims=True))
    alpha = jnp.exp(m_old - m_new)
    p = jnp.exp(s - m_new)
    m_ref[...] = m_new
    l_ref[...] = l_ref[...] * alpha + jnp.sum(p, axis=1, keepdims=True)
    acc_ref[...] = acc_ref[...] * alpha + _nt_dot(p.astype(bf16), vt.astype(bf16))

    @pl.when(cc == pl.num_programs(1) - 1)
    def _():
        gt = jax.nn.sigmoid(gate_ref[0])
        o_ref[0] = side_ref[...] + gt[:, 1:2] * acc_ref[...] / l_ref[...]


def _nsa_sample_attention_t(qr, ckt, cvt, slc_pool, layer_idx, page_table, win_buf, newrows, gate):
    db, n_pages = page_table.shape
    n_pool, nl = slc_pool.shape[:2]
    wlen = win_buf.shape[1]
    nc = ckt.shape[2]
    past = n_pages * PAGE_SIZE
    cur = past // SEL_BLOCK
    npg = STEP_PAGES
    nchunk = n_pages // npg
    keys = npg * PAGE_SIZE
    wsel = npg * CMP_PER_PAGE
    f32, bf16 = jnp.float32, jnp.bfloat16
    view = jnp.transpose(slc_pool, (0, 1, 3, 4, 5, 2)).reshape(n_pool * nl, 2, NSA_KV_W, PAGE_SIZE)
    wint = jnp.transpose(win_buf, (0, 2, 3, 4, 1)).reshape(db, 2, NSA_KV_W, wlen)
    pt = (page_table * nl + layer_idx).reshape(-1).astype(jnp.int32)
    hmask = (jnp.arange(NSA_HEADS)[:, None] // NSA_GROUP == jnp.arange(NSA_KV_HEADS)[None, :]).astype(f32)
    q16 = (qr * HEAD_DIM ** -0.5)[:, :, None, :] * hmask[None, :, :, None]
    q16 = jnp.pad(q16.reshape(db, NSA_HEADS, NSA_KV_W), ((0, 0), (0, STEP_ROWS - NSA_HEADS), (0, 0)))
    newr = jnp.pad(newrows, ((0, 0), (0, 8 - newrows.shape[1]), (0, 0)))
    newt = jnp.pad(newrows.transpose(0, 2, 1), ((0, 0), (0, 0), (0, LANES - newrows.shape[1])))
    g16 = jnp.pad(gate.reshape(db, 3, NSA_HEADS).transpose(0, 2, 1),
                  ((0, 0), (0, STEP_ROWS - NSA_HEADS), (0, LANES - 3)))
    cblk = _cmp_lane_blocks(n_pages)
    jj = cblk % CMP_PER_PAGE
    sblk = np.where(jj % 2 == 0, cblk // 2, -1)
    forced = ((sblk == 0) | (sblk == cur - 1)).astype(np.int32)
    blk8 = np.zeros((8, nc), np.int32)
    blk8[0], blk8[1] = sblk, forced
    rankm = ((sblk[:, None] >= 0) & (sblk[:, None] <= sblk[None, :])).astype(np.float32)
    loc = np.arange(wsel)
    lstep, lrem = loc // CMP_STEP_LANES, loc % CMP_STEP_LANES
    lj, lpage = lrem // CMP_PAGES_PER_STEP, lstep * CMP_PAGES_PER_STEP + lrem % CMP_PAGES_PER_STEP
    kidx = np.arange(keys)
    expand = ((lj[:, None] % 2 == 0) & (kidx[None, :] // PAGE_SIZE == lpage[:, None])
              & ((kidx[None, :] % PAGE_SIZE) // SEL_BLOCK == lj[:, None] // 2)).astype(np.float32)
    topn = min(NSA_TOPN, cur + 1) - 1

    def page_map(k):
        return lambda b, c, pt_ref: (pt_ref[b * n_pages + c * npg + k], 0, 0, 0)

    per_b = lambda shp: pl.BlockSpec((1,) + shp, lambda b, c, pt_ref: (b,) + (0,) * len(shp))
    const = lambda a: pl.BlockSpec(a.shape, lambda b, c, pt_ref: (0,) * a.ndim)
    consts = (jnp.asarray(blk8), jnp.asarray(expand, bf16), jnp.asarray(rankm, bf16))
    grid_spec = pltpu.PrefetchScalarGridSpec(
        num_scalar_prefetch=1, grid=(db, nchunk),
        in_specs=[per_b((STEP_ROWS, NSA_KV_W)), per_b((NSA_KV_W, nc)), per_b((NSA_KV_W, nc))]
        + [pl.BlockSpec((1, 2, NSA_KV_W, PAGE_SIZE), page_map(k)) for k in range(npg)]
        + [per_b((2, NSA_KV_W, wlen)), per_b((8, NSA_KV_W)), per_b((NSA_KV_W, LANES)), per_b((STEP_ROWS, LANES))]
        + [const(a) for a in consts],
        out_specs=[per_b((STEP_ROWS, NSA_KV_W)), per_b((2, NSA_KV_W, wlen))],
        scratch_shapes=[pltpu.VMEM((nc // wsel, STEP_ROWS, wsel), f32),
                        pltpu.VMEM((STEP_ROWS, 1), f32), pltpu.VMEM((STEP_ROWS, 1), f32),
                        pltpu.VMEM((STEP_ROWS, NSA_KV_W), f32), pltpu.VMEM((STEP_ROWS, NSA_KV_W), f32)])
    o16, wout = pl.pallas_call(
        functools.partial(_nsa_step_t_kernel, topn=topn), grid_spec=grid_spec,
        out_shape=[jax.ShapeDtypeStruct((db, STEP_ROWS, NSA_KV_W), f32),
                   jax.ShapeDtypeStruct((db, 2, NSA_KV_W, wlen), f32)],
        compiler_params=pltpu.CompilerParams(dimension_semantics=("arbitrary", "arbitrary"),
                                             vmem_limit_bytes=VMEM_LIMIT_BYTES),
        name="nsa_sample_attention",
    )(pt, q16, ckt, cvt, *([view] * npg), wint, newr, newt, g16, *consts)
    o = o16[:, :NSA_HEADS].reshape(db, NSA_HEADS, NSA_KV_HEADS, HEAD_DIM)
    o = jnp.take_along_axis(o, (jnp.arange(NSA_HEADS) // NSA_GROUP)[None, :, None, None], axis=2)
    wout = jnp.transpose(wout.reshape(db, 2, NSA_KV_HEADS, HEAD_DIM, wlen), (0, 4, 1, 2, 3))
    return o.reshape(db, NSA_HEADS * HEAD_DIM), wout


DIL_ROW_CHUNK = 64


def _dil_step_kernel(q_ref, buf_ref, newt_ref, newr_ref, bias_ref, o_ref, out_ref, p_ref, pn_ref, den_ref):
    f32, bf16 = jnp.float32, jnp.bfloat16
    kv = pl.program_id(1)
    wlen = buf_ref.shape[3]
    nrow = buf_ref.shape[2]
    q16 = q_ref[0]

    @pl.when(kv == 0)
    def _():
        s = jnp.dot(q16.astype(bf16), buf_ref[0, 0].astype(bf16), preferred_element_type=f32)
        s_new = jnp.sum(q16 * newr_ref[0, 0:1, :], axis=1, keepdims=True)
        ms, es, ens, dens = [], [], [], []
        for g in range(len(DIL_GROUPS)):
            sg = s + bias_ref[g:g + 1, :]
            m = jnp.maximum(jnp.max(sg, axis=1, keepdims=True), s_new)
            e, en = jnp.exp(sg - m), jnp.exp(s_new - m)
            ms.append(m); es.append(e); ens.append(en)
            dens.append(jnp.sum(e, axis=1, keepdims=True) + en)
        m_all = functools.reduce(jnp.maximum, ms)
        ws = [jnp.exp(m - m_all) for m in ms]
        p_ref[...] = sum(w * e for w, e in zip(ws, es))
        pn_ref[...] = sum(w * en for w, en in zip(ws, ens))
        den_ref[...] = sum(w * d for w, d in zip(ws, dens))

    @pl.when(kv == 1)
    def _():
        r = _nt_dot(p_ref[...].astype(bf16), buf_ref[0, 0].astype(bf16))
        r = (r + pn_ref[...] * newr_ref[0, 1:2, :]) / den_ref[...]
        head = lax.broadcasted_iota(jnp.int32, r.shape, 1) // HEAD_DIM
        row = lax.broadcasted_iota(jnp.int32, r.shape, 0)
        o_ref[0] = jnp.broadcast_to(jnp.sum(jnp.where(head == row, r, 0.0), axis=0, keepdims=True), o_ref.shape[1:])

    lane = lax.broadcasted_iota(jnp.int32, (DIL_ROW_CHUNK, wlen), 1)
    for c in range(nrow // DIL_ROW_CHUNK):
        rs = slice(c * DIL_ROW_CHUNK, (c + 1) * DIL_ROW_CHUNK)
        col = jnp.where(kv == 0, newt_ref[0, rs, 0:1], newt_ref[0, rs, 1:2])
        out_ref[0, 0, rs, :] = jnp.where(lane == wlen - 1, col, pltpu.roll(buf_ref[0, 0, rs, :], wlen - 1, 1))


def _dil_sample_attention(qr, kr_new, v_new, buf):
    db, wlen = buf.shape[:2]
    f32 = jnp.float32
    buft = jnp.transpose(buf, (0, 2, 3, 4, 1)).reshape(db, 2, DIL_W, wlen)
    eye = jnp.eye(DIL_HEADS, dtype=f32)
    q16 = ((qr * HEAD_DIM ** -0.5)[:, :, None, :] * eye[None, :, :, None]).reshape(db, DIL_HEADS, DIL_W)
    newr = jnp.pad(jnp.stack([kr_new, v_new], axis=1), ((0, 0), (0, 6), (0, 0)))
    newt = jnp.pad(jnp.stack([kr_new, v_new], axis=2), ((0, 0), (0, 0), (0, LANES - 2)))
    back = wlen - jnp.arange(wlen)
    bias = jnp.stack([jnp.where((back % d == 0) & (back // d <= DIL_SPAN), 0.0, NEG) for _, d in DIL_GROUPS])
    bias = jnp.pad(bias, ((0, 8 - len(DIL_GROUPS)), (0, 0))).astype(f32)
    o, new_buf = pl.pallas_call(
        _dil_step_kernel,
        grid=(db, 2),
        in_specs=[pl.BlockSpec((1, DIL_HEADS, DIL_W), lambda b, k: (b, 0, 0)),
                  pl.BlockSpec((1, 1, DIL_W, wlen), lambda b, k: (b, k, 0, 0)),
                  pl.BlockSpec((1, DIL_W, LANES), lambda b, k: (b, 0, 0)),
                  pl.BlockSpec((1, 8, DIL_W), lambda b, k: (b, 0, 0)),
                  pl.BlockSpec((8, wlen), lambda b, k: (0, 0))],
        out_specs=[pl.BlockSpec((1, 8, DIL_W), lambda b, k: (b, 0, 0)),
                   pl.BlockSpec((1, 1, DIL_W, wlen), lambda b, k: (b, k, 0, 0))],
        out_shape=[jax.ShapeDtypeStruct((db, 8, DIL_W), f32),
                   jax.ShapeDtypeStruct((db, 2, DIL_W, wlen), f32)],
        scratch_shapes=[pltpu.VMEM((DIL_HEADS, wlen), f32), pltpu.VMEM((DIL_HEADS, 1), f32),
                        pltpu.VMEM((DIL_HEADS, 1), f32)],
        compiler_params=pltpu.CompilerParams(dimension_semantics=("arbitrary", "arbitrary"),
                                             vmem_limit_bytes=VMEM_LIMIT_BYTES),
        name="dil_sample",
    )(q16, buft, newt, newr, bias)
    new_buf = jnp.transpose(new_buf.reshape(db, 2, DIL_HEADS, HEAD_DIM, wlen), (0, 4, 1, 2, 3))
    return o[:, 0], new_buf


def _dil_band_kernel(q_ref, kp_ref, kc_ref, vp_ref, vc_ref, num_ref, st_ref):
    f32, bf16 = jnp.float32, jnp.bfloat16
    blk = DIL_BLOCK
    n = pl.program_id(0)
    i = lax.broadcasted_iota(jnp.int32, (blk, 2 * blk), 0)
    j = lax.broadcasted_iota(jnp.int32, (blk, 2 * blk), 1) - blk
    ok = (i - j >= 0) & (i - j <= DIL_SPAN) & (n * blk + j >= 0)
    bias = jnp.where(ok, 0.0, NEG)
    bias = jnp.concatenate([bias, bias], axis=0)
    lane = lax.broadcasted_iota(jnp.int32, (blk, LANES), 1)
    first = lane < HEAD_DIM
    stats = jnp.zeros((blk, LANES), f32)
    for p in range(DIL_HEADS // 2):
        cols = slice(p * LANES, (p + 1) * LANES)
        qp = q_ref[:, cols] * (HEAD_DIM ** -0.5)
        qst = jnp.concatenate([jnp.where(first, qp, 0.0), jnp.where(first, 0.0, qp)], axis=0).astype(bf16)
        kk = jnp.concatenate([kp_ref[:, cols], kc_ref[:, cols]], axis=0).astype(bf16)
        vv = jnp.concatenate([vp_ref[:, cols], vc_ref[:, cols]], axis=0).astype(bf16)
        s = _nt_dot(qst, kk) + bias
        m = jnp.max(s, axis=1, keepdims=True)
        e = jnp.exp(s - m)
        den = jnp.sum(e, axis=1, keepdims=True)
        nm = jnp.dot(e.astype(bf16), vv, preferred_element_type=f32)
        num_ref[:, cols] = jnp.where(first, nm[:blk], nm[blk:])
        for a in range(2):
            h = 2 * p + a
            stats = jnp.where(lane == h, m[a * blk:(a + 1) * blk], stats)
            stats = jnp.where(lane == DIL_HEADS + h, den[a * blk:(a + 1) * blk], stats)
    st_ref[...] = stats


def _dil_band_stats(qr, kr, v, d):
    l = qr.shape[0]
    assert l % (d * DIL_BLOCK) == 0
    nb = l // (d * DIL_BLOCK)
    view = lambda a: a.reshape(l // d, d * a.shape[1])
    blk = lambda w, prev: pl.BlockSpec((DIL_BLOCK, w),
                                       (lambda n, r: (jnp.maximum(n - 1, 0), r)) if prev else (lambda n, r: (n, r)))
    num, st = pl.pallas_call(
        _dil_band_kernel,
        grid=(nb, d),
        in_specs=[blk(DIL_W, False), blk(DIL_W, True), blk(DIL_W, False), blk(DIL_W, True), blk(DIL_W, False)],
        out_specs=[blk(DIL_W, False), blk(LANES, False)],
        out_shape=[jax.ShapeDtypeStruct((l // d, d * DIL_W), jnp.float32),
                   jax.ShapeDtypeStruct((l // d, d * LANES), jnp.float32)],
        compiler_params=pltpu.CompilerParams(dimension_semantics=("arbitrary", "arbitrary"),
                                             vmem_limit_bytes=VMEM_LIMIT_BYTES),
        name="dil_band_stats",
    )(view(qr), view(kr), view(kr), view(v), view(v))
    st = st.reshape(l, LANES)
    return st[:, :DIL_HEADS], st[:, DIL_HEADS:2 * DIL_HEADS], num.reshape(l, DIL_HEADS, HEAD_DIM)


def _split_cols(h, widths):
    parts, start = [], 0
    for w in widths:
        parts.append(h[..., start:start + w])
        start += w
    return parts


def _even_widths():
    return (NSA_Q_W,) + (NSA_KV_W,) * 6 + (3 * NSA_HEADS, 3 * GDN_W, GDN_HEADS, GDN_HEADS, GDN_W)


def _rms_norm(x, w):
    return x * lax.rsqrt(jnp.mean(jnp.square(x), axis=-1, keepdims=True) + NORM_EPS) * w


def _l2_norm(x):
    return x * lax.rsqrt(jnp.sum(jnp.square(x), axis=-1, keepdims=True) + NORM_EPS)


def _rope(x, pos):
    half = HEAD_DIM // 2
    inv_freq = ROPE_THETA ** (-2.0 * jnp.arange(half, dtype=jnp.float32) / HEAD_DIM)
    ang = pos.astype(jnp.float32)[:, None] * inv_freq[None, :]
    cos, sin = jnp.cos(ang)[:, None, :], jnp.sin(ang)[:, None, :]
    xf = x.astype(jnp.float32)
    x1, x2 = xf[..., :half], xf[..., half:]
    return jnp.concatenate([x1 * cos - x2 * sin, x2 * cos + x1 * sin], axis=-1)


def _causal_dwconv(hist, u, w):
    width, s = w.shape[0], u.shape[1]
    ext = jnp.concatenate([hist.astype(u.dtype), u], axis=1)
    out = w[0] * ext[:, :s]
    for j in range(1, width):
        out = out + w[j] * ext[:, j:j + s]
    return out, ext[:, s:]


def _masked_softmax(s, mask):
    s = jnp.where(mask, s, -jnp.inf)
    m = jnp.max(s, axis=-1, keepdims=True)
    m = jnp.where(jnp.isfinite(m), m, 0.0)
    e = jnp.where(mask, jnp.exp(s - m), 0.0)
    den = jnp.sum(e, axis=-1, keepdims=True)
    return e / jnp.where(den > 0.0, den, 1.0)


def _gather_pages(pool, page_table, layer_idx):
    rows = pool[page_table, layer_idx]
    return rows.reshape(rows.shape[0], -1, *rows.shape[3:])


def _nsa_compress(rows, w1, b1, w2, pe):
    b, l, g, dh = rows.shape
    nc = l // CMP_BLOCK
    blk = rows[:, :nc * CMP_BLOCK].astype(jnp.float32).reshape(b, nc, CMP_BLOCK, g, dh) + pe[:, None, :]
    flat = blk.transpose(0, 1, 3, 2, 4).reshape(b, nc, g, CMP_BLOCK * dh)
    return jax.nn.gelu(flat @ w1 + b1) @ w2


def _nsa_compressed_kv(k_rows, v_rows, cw1, cb1, cw2, cpe):
    ck = _nsa_compress(k_rows, cw1[0], cb1[0], cw2[0], cpe[0])
    cv = _nsa_compress(v_rows, cw1[1], cb1[1], cw2[1], cpe[1])
    nc = ck.shape[1]
    ck = _rope(ck, (jnp.arange(nc) + 1) * CMP_BLOCK - 1)
    return ck, cv


def _nsa_attend(q, qpos, ck, cv, sk, sv, wk, wv, wpos):
    b, nq = q.shape[:2]
    scale = HEAD_DIM ** -0.5
    nc, ns = ck.shape[1], sk.shape[2]
    cend = (jnp.arange(nc) + 1) * CMP_BLOCK - 1
    s = jnp.einsum('bqghd,bcgd->bghqc', q, ck) * scale
    p_cmp = _masked_softmax(s, cend[None, :] <= qpos[:, None])
    o_cmp = jnp.einsum('bghqc,bcgd->bqghd', p_cmp, cv)
    imp = jnp.sum(p_cmp, axis=2)
    imp = jnp.pad(imp, ((0, 0), (0, 0), (0, 0), (0, ns * CMP_PER_SEL - nc)))
    imp = imp.reshape(b, NSA_KV_HEADS, nq, ns, CMP_PER_SEL).sum(-1)
    blk = jnp.arange(ns)[None, :]
    cur = (qpos // SEL_BLOCK)[:, None]
    forced = (blk == 0) | (blk == cur) | (blk == cur - 1)
    imp = jnp.where(blk <= cur, jnp.where(forced, NSA_FORCE, imp), -1.0)
    _, idx = lax.top_k(imp, min(NSA_TOPN, ns))
    n = idx.shape[-1]
    pick = jax.vmap(jax.vmap(lambda kb, ix: kb[ix]))
    ksel = pick(sk, idx).reshape(b, NSA_KV_HEADS, nq, n * SEL_BLOCK, HEAD_DIM)
    vsel = pick(sv, idx).reshape(b, NSA_KV_HEADS, nq, n * SEL_BLOCK, HEAD_DIM)
    kpos = (idx[..., None] * SEL_BLOCK + jnp.arange(SEL_BLOCK)).reshape(b, NSA_KV_HEADS, nq, n * SEL_BLOCK)
    s = jnp.einsum('bqghd,bgqkd->bghqk', q, ksel) * scale
    p = _masked_softmax(s, (kpos <= qpos[:, None])[:, :, None])
    o_slc = jnp.einsum('bghqk,bgqkd->bqghd', p, vsel)
    dist = qpos[:, None] - wpos[None, :]
    wmask = (dist >= 0) & (dist <= NSA_WINDOW) & (wpos[None, :] >= 0)
    s = jnp.einsum('bqghd,bkgd->bghqk', q, wk) * scale
    p = _masked_softmax(s, wmask)
    o_win = jnp.einsum('bghqk,bkgd->bqghd', p, wv)
    return o_cmp, o_slc, o_win


def _nsa_prompt(q, kc, vc, ks, vs, kw, vw, gate, cw1, cb1, cw2, cpe):
    b, l = q.shape[:2]
    pos = jnp.arange(l)
    qr = _rope(q, pos)
    ck, cv = _nsa_compressed_kv(kc, vc, cw1, cb1, cw2, cpe)
    ksr = _rope(ks, pos)
    vsf = vs.astype(jnp.float32)
    kwr = _rope(kw, pos)
    vwf = vw.astype(jnp.float32)
    small = jnp.pad(gate.reshape(l, -1), ((0, 0), (0, LANES - gate.shape[-1])))
    flat = lambda t: t.reshape(t.shape[1], -1)
    o_nsa = _nsa_prompt_attention(flat(qr), small, flat(ck), flat(cv), flat(ksr), flat(vsf), flat(kwr), flat(vwf))
    keep = min(NSA_WINDOW, l)
    rows_cmp = jnp.stack([kc, vc], axis=2)
    rows_slc = jnp.stack([ksr, vsf], axis=2)
    rows_win = jnp.stack([kwr[:, l - keep:], vwf[:, l - keep:]], axis=2)
    return o_nsa[None], rows_cmp, rows_slc, rows_win


def _nsa_sample(q, kc, vc, ks, vs, kw, vw, gate, cmp_pool, slc_pool, layer_idx, win_buf, page_table,
                cw1, cb1, cw2, cpe):
    db, s = q.shape[:2]
    past = page_table.shape[1] * PAGE_SIZE
    wb = win_buf.shape[1]
    assert s == 1 and wb == NSA_WINDOW and past >= wb and past % (STEP_PAGES * PAGE_SIZE) == 0
    qpos = past + jnp.arange(s)
    qr = _rope(q, qpos)
    ckt, cvt = _nsa_sample_compress_t(cmp_pool, layer_idx, page_table, cw1, cb1, cw2, cpe)
    ksr = _rope(ks, qpos)
    vsf = vs.astype(jnp.float32)
    kwr = _rope(kw, qpos)
    vwf = vw.astype(jnp.float32)
    newrows = jnp.stack([t.reshape(db, NSA_KV_W) for t in (ksr, vsf, kwr, vwf)], axis=1)
    o_nsa, rows_win = _nsa_sample_attention_t(qr[:, 0], ckt, cvt, slc_pool, layer_idx, page_table, win_buf,
                                              newrows, gate.reshape(db, -1))
    rows_cmp = jnp.stack([kc, vc], axis=2)
    rows_slc = jnp.stack([ksr, vsf], axis=2)
    return o_nsa[:, None], rows_cmp, rows_slc, rows_win


def _gdn_chunked(q, k, v, g, beta, s0):
    b, l, h, dk = q.shape
    dv = v.shape[-1]
    c = GDN_CHUNK
    nch = l // c
    r = lambda a: jnp.moveaxis(a.reshape(b, nch, c, h, *a.shape[3:]), 3, 2)
    q, k, v, g, beta = r(q), r(k), r(v), r(g), r(beta)
    gc = jnp.cumsum(g, axis=-1)
    ii = jnp.arange(c)
    tri = ii[:, None] >= ii[None, :]
    strict = ii[:, None] > ii[None, :]
    diff = gc[..., :, None] - gc[..., None, :]
    gamma = jnp.where(tri, jnp.exp(jnp.where(tri, diff, 0.0)), 0.0)
    kb = k * beta[..., None]
    a_mat = jnp.where(strict, jnp.einsum('bnhik,bnhjk->bnhij', kb, k) * gamma, 0.0)
    eye = jnp.eye(c, dtype=jnp.float32)
    t_inv = lax.linalg.triangular_solve(eye + a_mat, jnp.broadcast_to(eye, a_mat.shape),
                                        left_side=True, lower=True, unit_diagonal=True)
    u = t_inv @ (v * beta[..., None])
    w = t_inv @ (kb * jnp.exp(gc)[..., None])
    qk = jnp.where(tri, jnp.einsum('bnhik,bnhjk->bnhij', q, k) * gamma, 0.0)
    qg = q * jnp.exp(gc)[..., None]
    kd = k * jnp.exp(gc[..., -1:] - gc)[..., None]
    glast = jnp.exp(gc[..., -1])

    def step(state, xs):
        qg_c, kd_c, u_c, w_c, qk_c, gl_c = xs
        v_new = u_c - jnp.einsum('bhck,bhkv->bhcv', w_c, state)
        o = jnp.einsum('bhck,bhkv->bhcv', qg_c, state) + jnp.einsum('bhij,bhjv->bhiv', qk_c, v_new)
        state = state * gl_c[..., None, None] + jnp.einsum('bhck,bhcv->bhkv', kd_c, v_new)
        return state, o

    xs = tuple(jnp.moveaxis(a, 1, 0) for a in (qg, kd, u, w, qk, glast))
    s_fin, o = lax.scan(step, s0, xs)
    o = jnp.moveaxis(jnp.moveaxis(o, 0, 1), 2, 3).reshape(b, l, h, dv)
    return o, s_fin


def _gdn_recurrent(q, k, v, g, beta, s0):
    def step(state, xs):
        q_t, k_t, v_t, g_t, b_t = xs
        state = state * jnp.exp(g_t)[..., None, None]
        v_t = (v_t - jnp.einsum('bhk,bhkv->bhv', k_t, state)) * b_t[..., None]
        state = state + jnp.einsum('bhk,bhv->bhkv', k_t, v_t)
        return state, jnp.einsum('bhk,bhkv->bhv', q_t, state)

    xs = tuple(jnp.moveaxis(a, 1, 0) for a in (q, k, v, g, beta))
    s_fin, o = lax.scan(step, s0, xs)
    return jnp.moveaxis(o, 0, 1), s_fin


def _gdn_mix(qkv, a, bt, z, conv_hist, s0, conv_w, a_log, dt_bias, norm_w, chunked):
    b, s = qkv.shape[:2]
    c, new_hist = _causal_dwconv(conv_hist, qkv, conv_w)
    c = jax.nn.silu(c.astype(jnp.float32))
    q, k, v = [t.reshape(b, s, GDN_HEADS, HEAD_DIM) for t in jnp.split(c, 3, axis=-1)]
    q = _l2_norm(q) * HEAD_DIM ** -0.5
    k = _l2_norm(k)
    beta = jax.nn.sigmoid(bt.astype(jnp.float32))
    g = -jnp.exp(a_log) * jax.nn.softplus(a.astype(jnp.float32) + dt_bias)
    s0 = s0.astype(jnp.float32)
    if chunked:
        o, s_fin = _gdn_chunked(q, k, v, g, beta, s0)
    else:
        o, s_fin = _gdn_recurrent(q, k, v, g, beta, s0)
    o = _rms_norm(o, norm_w) * jax.nn.silu(z.astype(jnp.float32).reshape(b, s, GDN_HEADS, HEAD_DIM))
    return o.reshape(b, s, GDN_W), new_hist, s_fin


def _even_merge(o_cmp, o_slc, o_win, gate, o_gdn):
    b, s = gate.shape[:2]
    gt = jax.nn.sigmoid(gate.astype(jnp.float32)).reshape(b, s, 3, NSA_HEADS, 1)
    o_nsa = gt[:, :, 0] * o_cmp + gt[:, :, 1] * o_slc + gt[:, :, 2] * o_win
    return jnp.concatenate([o_nsa.reshape(b, s, NSA_Q_W), o_gdn], axis=-1)


def _proj(x, w):
    b, s, d = x.shape
    n = w.shape[1]
    npad = -(-n // LANES) * LANES
    wp = jnp.pad(w, ((0, 0), (0, npad - n)))
    return _matmul(x.reshape(b * s, d), wp)[:, :n].reshape(b, s, n)


def _even_prompt(x, w_in, cw1, cb1, cw2, cpe, conv_w, a_log, dt_bias, norm_w):
    b, l, _ = x.shape
    q, kc, vc, ks, vs, kw, vw, gate, qkv, a, bt, z = _split_cols(_proj(x, w_in), _even_widths())
    heads = lambda t: t.reshape(b, l, -1, HEAD_DIM)
    o_nsa, r_cmp, r_slc, r_win = _nsa_prompt(
        heads(q), heads(kc), heads(vc), heads(ks), heads(vs), heads(kw), heads(vw), gate, cw1, cb1, cw2, cpe)
    hist0 = jnp.zeros((b, GDN_CONV - 1, 3 * GDN_W), qkv.dtype)
    s0 = jnp.zeros((b, GDN_HEADS, HEAD_DIM, HEAD_DIM), jnp.float32)
    o_gdn, conv_hist, s_fin = _gdn_mix(qkv, a, bt, z, hist0, s0, conv_w, a_log, dt_bias, norm_w, True)
    return jnp.concatenate([o_nsa, o_gdn], axis=-1), r_cmp, r_slc, r_win, conv_hist, s_fin


def _even_sample(x, cmp_pool, slc_pool, layer_idx, win_buf, conv_hist, s0, page_table,
                 w_in, cw1, cb1, cw2, cpe, conv_w, a_log, dt_bias, norm_w):
    b, s, _ = x.shape
    q, kc, vc, ks, vs, kw, vw, gate, qkv, a, bt, z = _split_cols(_proj(x, w_in), _even_widths())
    heads = lambda t: t.reshape(b, s, -1, HEAD_DIM)
    o_nsa, r_cmp, r_slc, r_win = _nsa_sample(
        heads(q), heads(kc), heads(vc), heads(ks), heads(vs), heads(kw), heads(vw), gate,
        cmp_pool, slc_pool, layer_idx, win_buf, page_table, cw1, cb1, cw2, cpe)
    o_gdn, new_hist, s_fin = _gdn_mix(qkv, a, bt, z, conv_hist, s0, conv_w, a_log, dt_bias, norm_w, False)
    return jnp.concatenate([o_nsa, o_gdn], axis=-1), r_cmp, r_slc, r_win, new_hist, s_fin


def _dilated_band_stats(q, k, v, d):
    b, l, h, dh = q.shape
    unit = d * DIL_BLOCK
    lp = -(-l // unit) * unit
    nb = lp // unit
    to_sub = lambda a: jnp.pad(a, ((0, 0), (0, lp - l), (0, 0), (0, 0))).reshape(b, nb, DIL_BLOCK, d, h, dh)
    qs, ks, vs = to_sub(q), to_sub(k), to_sub(v)
    prev = lambda a: jnp.concatenate([jnp.zeros_like(a[:, :1]), a[:, :-1]], axis=1)
    kk = jnp.concatenate([prev(ks), ks], axis=2)
    vv = jnp.concatenate([prev(vs), vs], axis=2)
    s = jnp.einsum('bnirhd,bnjrhd->bnrhij', qs, kk) * HEAD_DIM ** -0.5
    i = jnp.arange(DIL_BLOCK)
    j = jnp.arange(2 * DIL_BLOCK) - DIL_BLOCK
    dist = i[:, None] - j[None, :]
    sub_k = jnp.arange(nb)[:, None, None] * DIL_BLOCK + j[None, None, :]
    mask = (dist >= 0)[None] & (dist <= DIL_SPAN)[None] & (sub_k >= 0)
    s = jnp.where(mask[None, :, None, None], s, -jnp.inf)
    m = jnp.max(s, axis=-1)
    e = jnp.exp(s - m[..., None])
    den = jnp.sum(e, axis=-1)
    num = jnp.einsum('bnrhij,bnjrhd->bnrhid', e, vv)
    m = jnp.transpose(m, (0, 1, 4, 2, 3)).reshape(b, lp, h)[:, :l]
    den = jnp.transpose(den, (0, 1, 4, 2, 3)).reshape(b, lp, h)[:, :l]
    num = jnp.transpose(num, (0, 1, 4, 2, 3, 5)).reshape(b, lp, h, dh)[:, :l]
    return m, den, num


def _dilated_gather_stats(q, k_all, v_all, qpos, kpos0, d):
    kp = qpos[:, None] - jnp.arange(DIL_SPAN + 1)[None, :] * d
    idx = kp - kpos0
    valid = idx >= 0
    idxc = jnp.clip(idx, 0)
    kg, vg = k_all[:, idxc], v_all[:, idxc]
    s = jnp.einsum('bshd,bsmhd->bshm', q, kg) * HEAD_DIM ** -0.5
    s = jnp.where(valid[None, :, None, :], s, -jnp.inf)
    m = jnp.max(s, axis=-1)
    e = jnp.exp(s - m[..., None])
    return m, jnp.sum(e, axis=-1), jnp.einsum('bshm,bsmhd->bshd', e, vg)


def _combine_by_denominators(stats):
    m_all = stats[0][0]
    for m, _, _ in stats[1:]:
        m_all = jnp.maximum(m_all, m)
    num, den = None, None
    for m, dn, nm in stats:
        w = jnp.exp(m - m_all)
        num = w[..., None] * nm if num is None else num + w[..., None] * nm
        den = w * dn if den is None else den + w * dn
    return num / den[..., None]


def _dil_prompt(x, w_in):
    b, l, _ = x.shape
    q, k, v = [t.reshape(b, l, DIL_HEADS, HEAD_DIM) for t in jnp.split(_proj(x, w_in), 3, axis=-1)]
    pos = jnp.arange(l)
    qr, kr, vf = _rope(q, pos), _rope(k, pos), v.astype(jnp.float32)
    flat = lambda t: t.reshape(l, DIL_W)
    o = _combine_by_denominators([_dil_band_stats(flat(qr), flat(kr), flat(vf), d) for _, d in DIL_GROUPS])
    keep = min(DIL_MAX_WINDOW, l)
    buf = jnp.stack([kr[:, l - keep:], vf[:, l - keep:]], axis=2)
    return o.reshape(b, l, DIL_W), buf


def _dil_sample(x, buf, past, w_in):
    db, s, _ = x.shape
    q, k, v = [t.reshape(db, s, DIL_HEADS, HEAD_DIM) for t in jnp.split(_proj(x, w_in), 3, axis=-1)]
    assert s == 1 and buf.shape[1] == DIL_MAX_WINDOW <= past
    qpos = past + jnp.arange(s)
    qr, kr = _rope(q, qpos), _rope(k, qpos)
    o, new_buf = _dil_sample_attention(qr[:, 0], kr.reshape(db, DIL_W), v.reshape(db, DIL_W).astype(jnp.float32), buf)
    return o[:, None], new_buf


def kernel(x_prompt, x_sample, cache_nsa_cmp_kv, cache_nsa_slc_kv, state_nsa_win_kv, state_gdn_conv,
           state_gdn_S, state_dil_kv, state_ffn_conv, page_table, w_in_a, nsa_cmp_w1, nsa_cmp_b1, nsa_cmp_w2,
           nsa_cmp_pe, gdn_conv_w, gdn_A_log, gdn_dt_bias, gdn_norm_w, w_out_a, w_in_c, w_out_c,
           ln_mix_g, ln_mix_b, ffn_w_in, ffn_conv_w, ffn_conv_b, ffn_w_out, ln_ffn_g, ln_ffn_b):
    past = page_table.shape[1] * PAGE_SIZE
    bp, lp, d = x_prompt.shape
    bs, ls, _ = x_sample.shape
    assert bp == 1 and ls == 1
    xp, xs = x_prompt, x_sample
    cmp_p, cmp_s, slc_p, slc_s, win_p, win_s = [], [], [], [], [], []
    gconv_p, gconv_s, gstate_p, gstate_s = [], [], [], []
    dil_p, dil_s, ffn_p, ffn_s = [], [], [], []
    for layer in range(DEPTH):
        if layer % 2 == 0:
            la = layer // 2
            wa = (w_in_a[la], nsa_cmp_w1[la], nsa_cmp_b1[la], nsa_cmp_w2[la], nsa_cmp_pe[la],
                  gdn_conv_w[la], gdn_A_log[la], gdn_dt_bias[la], gdn_norm_w[la])
            mp, rc, rs, rw, hc, hs_ = _even_prompt(xp, *wa)
            cmp_p.append(rc); slc_p.append(rs); win_p.append(rw); gconv_p.append(hc); gstate_p.append(hs_)
            ms, rc, rs, rw, hc, hs_ = _even_sample(xs, cache_nsa_cmp_kv, cache_nsa_slc_kv, la,
                                                   state_nsa_win_kv[:, la], state_gdn_conv[:, la],
                                                   state_gdn_S[:, la], page_table, *wa)
            cmp_s.append(rc); slc_s.append(rs); win_s.append(rw); gconv_s.append(hc); gstate_s.append(hs_)
            w_out = w_out_a[la]
        else:
            lc = layer // 2
            mp, bpf = _dil_prompt(xp, w_in_c[lc])
            ms, bsf = _dil_sample(xs, state_dil_kv[:, lc], past, w_in_c[lc])
            dil_p.append(bpf); dil_s.append(bsf)
            w_out = w_out_c[lc]
        xp2 = _matmul_ln(mp.reshape(lp, -1), w_out, xp.reshape(lp, d), ln_mix_g[layer], ln_mix_b[layer])
        xs2 = _matmul_ln(ms.reshape(bs, -1), w_out, xs.reshape(bs, d), ln_mix_g[layer], ln_mix_b[layer])
        fargs = (ffn_w_in[layer], ffn_conv_w[layer], ffn_conv_b[layer], ffn_w_out[layer],
                 ln_ffn_g[layer], ln_ffn_b[layer])
        xp3, hp = _ffn_seq(xp2, *fargs)
        xs3, hs = _ffn_step(xs2, state_ffn_conv[:, layer], *fargs)
        xp, xs = xp3.reshape(1, lp, d), xs3.reshape(bs, 1, d)
        ffn_p.append(hp[None]); ffn_s.append(hs)

    def stk(lst):
        return jnp.stack(lst, axis=1)

    return (xp, xs, stk(cmp_p), stk(cmp_s), stk(slc_p), stk(slc_s), stk(win_p), stk(win_s),
            stk(gconv_p), stk(gconv_s), stk(gstate_p), stk(gstate_s), stk(dil_p), stk(dil_s),
            stk(ffn_p), stk(ffn_s))
```

```python
import functools
import math

import jax
import jax.numpy as jnp
from jax import lax
from jax.experimental import pallas as pl
from jax.experimental.pallas import tpu as pltpu
import numpy as np

D_MODEL = 1024
DEPTH = 2
PAGE_SIZE = 128
HEAD_DIM = 64
ROPE_THETA = 10000.0
NSA_HEADS = 8
NSA_KV_HEADS = 2
NSA_GROUP = NSA_HEADS // NSA_KV_HEADS
CMP_BLOCK = 32
SEL_BLOCK = 64
CMP_PER_SEL = SEL_BLOCK // CMP_BLOCK
NSA_TOPN = 16
NSA_WINDOW = 512
NSA_QBLOCK = 128
NSA_FORCE = 1.0e4
GDN_HEADS = 8
GDN_CONV = 4
GDN_CHUNK = 64
DIL_HEADS = 16
DIL_GROUPS = ((128, 1), (512, 4), (2048, 16))
DIL_SPAN = 128
DIL_BLOCK = 128
DIL_MAX_WINDOW = 2048
D_FF = 2816
FFN_CONV = 3
DEEPNORM_ALPHA = (2.0 * DEPTH) ** 0.25
LN_EPS = 1e-5
NORM_EPS = 1e-6
NSA_Q_W = NSA_HEADS * HEAD_DIM
NSA_KV_W = NSA_KV_HEADS * HEAD_DIM
GDN_W = GDN_HEADS * HEAD_DIM
DIL_W = DIL_HEADS * HEAD_DIM

LANES = 128
VMEM_LIMIT_BYTES = 56 * 1024 * 1024


def _layer_norm_rows(r, g, b):
    mu = jnp.mean(r, axis=-1, keepdims=True)
    d = r - mu
    var = jnp.mean(d * d, axis=-1, keepdims=True)
    return d * lax.rsqrt(var + LN_EPS) * g + b


def _mm_kernel(x_ref, w_ref, o_ref):
    o_ref[...] = jnp.dot(x_ref[...].astype(jnp.bfloat16), w_ref[...], preferred_element_type=jnp.float32)


def _mm_ln_kernel(x_ref, w_ref, res_ref, g_ref, b_ref, o_ref):
    acc = jnp.dot(x_ref[...].astype(jnp.bfloat16), w_ref[...], preferred_element_type=jnp.float32)
    o_ref[...] = _layer_norm_rows(DEEPNORM_ALPHA * res_ref[...] + acc, g_ref[...], b_ref[...])


def _row_tile(m):
    return 512 if m % 512 == 0 else m


def _matmul(x, w):
    m, k = x.shape
    n = w.shape[1]
    tm = _row_tile(m)
    tn = n
    for cand in (1152, 1024, 768, 512):
        if n % cand == 0:
            tn = cand
            break
    return pl.pallas_call(
        _mm_kernel,
        grid=(m // tm, n // tn),
        in_specs=[pl.BlockSpec((tm, k), lambda i, j: (i, 0)),
                  pl.BlockSpec((k, tn), lambda i, j: (0, j))],
        out_specs=pl.BlockSpec((tm, tn), lambda i, j: (i, j)),
        out_shape=jax.ShapeDtypeStruct((m, n), jnp.float32),
        compiler_params=pltpu.CompilerParams(dimension_semantics=("parallel", "arbitrary"),
                                             vmem_limit_bytes=VMEM_LIMIT_BYTES),
        name="matmul",
    )(x, w.astype(jnp.bfloat16))


def _matmul_ln(x, w, res, g, b):
    m, k = x.shape
    n = w.shape[1]
    tm = _row_tile(m)
    return pl.pallas_call(
        _mm_ln_kernel,
        grid=(m // tm,),
        in_specs=[pl.BlockSpec((tm, k), lambda i: (i, 0)),
                  pl.BlockSpec((k, n), lambda i: (0, 0)),
                  pl.BlockSpec((tm, n), lambda i: (i, 0)),
                  pl.BlockSpec((1, n), lambda i: (0, 0)),
                  pl.BlockSpec((1, n), lambda i: (0, 0))],
        out_specs=pl.BlockSpec((tm, n), lambda i: (i, 0)),
        out_shape=jax.ShapeDtypeStruct((m, n), jnp.float32),
        compiler_params=pltpu.CompilerParams(dimension_semantics=("arbitrary",),
                                             vmem_limit_bytes=VMEM_LIMIT_BYTES),
        name="matmul_ln",
    )(x, w.astype(jnp.bfloat16), res, g.reshape(1, n), b.reshape(1, n))


FFN_CHUNK = D_FF // 2
FFN_NCHUNK = D_FF // FFN_CHUNK


def _ffn_seq_kernel(x_ref, wa_ref, wg_ref, cwa_ref, cwg_ref, cba_ref, cbg_ref, wo_ref, lg_ref, lb_ref,
                    y_ref, ha_ref, hg_ref, acc_ref, carry_ref):
    i, j = pl.program_id(0), pl.program_id(1)
    tm = x_ref.shape[0]
    x = x_ref[...]
    xb = x.astype(jnp.bfloat16)

    @pl.when(i == 0)
    def _():
        carry_ref[j] = jnp.zeros(carry_ref.shape[1:], jnp.float32)

    def conv(u, cw_ref, cb_ref, slot):
        prev = carry_ref[j, slot]
        p2, p1 = prev[6:7], prev[7:8]
        row = lax.broadcasted_iota(jnp.int32, u.shape, 0)
        u1 = jnp.where(row == 0, p1, pltpu.roll(u, 1, 0))
        u2 = jnp.where(row == 0, p2, jnp.where(row == 1, p1, pltpu.roll(u, 2, 0)))
        carry_ref[j, slot] = u[tm - 8:]
        cw = cw_ref[...]
        return cw[0:1] * u2 + cw[1:2] * u1 + cw[2:3] * u + cb_ref[...]

    ua = jnp.dot(xb, wa_ref[...], preferred_element_type=jnp.float32)
    ug = jnp.dot(xb, wg_ref[...], preferred_element_type=jnp.float32)
    ha_ref[...] = ua[tm - 8:]
    hg_ref[...] = ug[tm - 8:]
    a = conv(ua, cwa_ref, cba_ref, 0)
    g = conv(ug, cwg_ref, cbg_ref, 1)
    h = (a * jax.nn.sigmoid(a) * g).astype(jnp.bfloat16)
    part = jnp.dot(h, wo_ref[...], preferred_element_type=jnp.float32)

    @pl.when(j == 0)
    def _():
        acc_ref[...] = part

    @pl.when(j > 0)
    def _():
        acc_ref[...] += part

    @pl.when(j == pl.num_programs(1) - 1)
    def _():
        y_ref[...] = _layer_norm_rows(DEEPNORM_ALPHA * x + acc_ref[...], lg_ref[...], lb_ref[...])


def _ffn_seq(x, w_in, conv_w, conv_b, w_out, ln_g, ln_b):
    l, d = x.shape
    tm = 512
    c, nc = FFN_CHUNK, FFN_NCHUNK
    w_in = w_in.astype(jnp.bfloat16)
    cw8 = jnp.zeros((8, 2 * D_FF), jnp.float32).at[:FFN_CONV].set(conv_w)
    cb = conv_b.reshape(1, 2 * D_FF)
    y, ha, hg = pl.pallas_call(
        _ffn_seq_kernel,
        grid=(l // tm, nc),
        in_specs=[pl.BlockSpec((tm, d), lambda i, j: (i, 0)),
                  pl.BlockSpec((d, c), lambda i, j: (0, j)),
                  pl.BlockSpec((d, c), lambda i, j: (0, j + nc)),
                  pl.BlockSpec((8, c), lambda i, j: (0, j)),
                  pl.BlockSpec((8, c), lambda i, j: (0, j + nc)),
                  pl.BlockSpec((1, c), lambda i, j: (0, j)),
                  pl.BlockSpec((1, c), lambda i, j: (0, j + nc)),
                  pl.BlockSpec((c, d), lambda i, j: (j, 0)),
                  pl.BlockSpec((1, d), lambda i, j: (0, 0)),
                  pl.BlockSpec((1, d), lambda i, j: (0, 0))],
        out_specs=[pl.BlockSpec((tm, d), lambda i, j: (i, 0)),
                   pl.BlockSpec((8, c), lambda i, j: (i, j)),
                   pl.BlockSpec((8, c), lambda i, j: (i, j))],
        out_shape=[jax.ShapeDtypeStruct((l, d), jnp.float32),
                   jax.ShapeDtypeStruct((l // tm * 8, D_FF), jnp.float32),
                   jax.ShapeDtypeStruct((l // tm * 8, D_FF), jnp.float32)],
        scratch_shapes=[pltpu.VMEM((tm, d), jnp.float32),
                        pltpu.VMEM((nc, 2, 8, c), jnp.float32)],
        compiler_params=pltpu.CompilerParams(dimension_semantics=("arbitrary", "arbitrary"),
                                             vmem_limit_bytes=VMEM_LIMIT_BYTES),
        name="ffn_seq",
    )(x, w_in, w_in, cw8, cw8, cb, cb, w_out.astype(jnp.bfloat16), ln_g.reshape(1, d), ln_b.reshape(1, d))
    hist = jnp.concatenate([ha[-(FFN_CONV - 1):], hg[-(FFN_CONV - 1):]], axis=-1)
    return y, hist


def _ffn_step_kernel(x_ref, h_ref, wa_ref, wg_ref, cwa_ref, cwg_ref, cba_ref, cbg_ref, wo_ref, lg_ref, lb_ref,
                     y_ref, ua_ref, ug_ref, acc_ref):
    j = pl.program_id(0)
    x = x_ref[...]
    xb = x.astype(jnp.bfloat16)
    ua = jnp.dot(xb, wa_ref[...], preferred_element_type=jnp.float32)
    ug = jnp.dot(xb, wg_ref[...], preferred_element_type=jnp.float32)
    ua_ref[...] = ua
    ug_ref[...] = ug
    cwa, cwg = cwa_ref[...], cwg_ref[...]
    a = cwa[0:1] * h_ref[0, 0] + cwa[1:2] * h_ref[1, 0] + cwa[2:3] * ua + cba_ref[...]
    g = cwg[0:1] * h_ref[0, 1] + cwg[1:2] * h_ref[1, 1] + cwg[2:3] * ug + cbg_ref[...]
    h = (a * jax.nn.sigmoid(a) * g).astype(jnp.bfloat16)
    part = jnp.dot(h, wo_ref[...], preferred_element_type=jnp.float32)

    @pl.when(j == 0)
    def _():
        acc_ref[...] = part

    @pl.when(j > 0)
    def _():
        acc_ref[...] += part

    @pl.when(j == pl.num_programs(0) - 1)
    def _():
        y_ref[...] = _layer_norm_rows(DEEPNORM_ALPHA * x + acc_ref[...], lg_ref[...], lb_ref[...])


def _ffn_step(x, hist, w_in, conv_w, conv_b, w_out, ln_g, ln_b):
    b, d = x.shape
    c, nc = FFN_CHUNK, FFN_NCHUNK
    w_in = w_in.astype(jnp.bfloat16)
    cw8 = jnp.zeros((8, 2 * D_FF), jnp.float32).at[:FFN_CONV].set(conv_w)
    cb = conv_b.reshape(1, 2 * D_FF)
    h4 = jnp.transpose(hist, (1, 0, 2)).reshape(2, b, 2, D_FF).transpose(0, 2, 1, 3)
    y, ua, ug = pl.pallas_call(
        _ffn_step_kernel,
        grid=(nc,),
        in_specs=[pl.BlockSpec((b, d), lambda j: (0, 0)),
                  pl.BlockSpec((2, 2, b, c), lambda j: (0, 0, 0, j)),
                  pl.BlockSpec((d, c), lambda j: (0, j)),
                  pl.BlockSpec((d, c), lambda j: (0, j + nc)),
                  pl.BlockSpec((8, c), lambda j: (0, j)),
                  pl.BlockSpec((8, c), lambda j: (0, j + nc)),
                  pl.BlockSpec((1, c), lambda j: (0, j)),
                  pl.BlockSpec((1, c), lambda j: (0, j + nc)),
                  pl.BlockSpec((c, d), lambda j: (j, 0)),
                  pl.BlockSpec((1, d), lambda j: (0, 0)),
                  pl.BlockSpec((1, d), lambda j: (0, 0))],
        out_specs=[pl.BlockSpec((b, d), lambda j: (0, 0)),
                   pl.BlockSpec((b, c), lambda j: (0, j)),
                   pl.BlockSpec((b, c), lambda j: (0, j))],
        out_shape=[jax.ShapeDtypeStruct((b, d), jnp.float32),
                   jax.ShapeDtypeStruct((b, D_FF), jnp.float32),
                   jax.ShapeDtypeStruct((b, D_FF), jnp.float32)],
        scratch_shapes=[pltpu.VMEM((b, d), jnp.float32)],
        compiler_params=pltpu.CompilerParams(dimension_semantics=("arbitrary",),
                                             vmem_limit_bytes=VMEM_LIMIT_BYTES),
        name="ffn_step",
    )(x, h4, w_in, w_in, cw8, cw8, cb, cb, w_out.astype(jnp.bfloat16), ln_g.reshape(1, d), ln_b.reshape(1, d))
    u = jnp.concatenate([ua, ug], axis=-1)
    return y, jnp.concatenate([hist[:, 1:], u[:, None]], axis=1)


NEG = -1e30
NSA_KT = 512
NSA_COLS = NSA_HEADS * NSA_QBLOCK
NSA_WSPAN = NSA_WINDOW + NSA_QBLOCK
LOG2E = 1.4426950408889634
NSA_VROWS = HEAD_DIM + 8


def _lane_tile(x, n):
    return jnp.concatenate([x] * n, axis=1)


def _nsa_prompt_kernel(q_ref, sm_ref, ck_ref, cvt_ref, ks_ref, vst_ref, kw_ref, vwt_ref, hot_ref, o_ref,
                       selb_ref, m_ref, acc_ref, *, ns):
    f32, bf16 = jnp.float32, jnp.bfloat16
    qb = NSA_QBLOCK
    i = pl.program_id(0)
    s0 = i * qb
    half = NSA_COLS // 2

    qt = (q_ref[...] * (HEAD_DIM ** -0.5 * LOG2E)).T
    zero = jnp.zeros((HEAD_DIM, qb), f32)
    top = jnp.concatenate([qt[h * HEAD_DIM:(h + 1) * HEAD_DIM] for h in range(NSA_GROUP)] + [zero] * NSA_GROUP, axis=1)
    bot = jnp.concatenate([zero] * NSA_GROUP + [qt[h * HEAD_DIM:(h + 1) * HEAD_DIM]
                                                for h in range(NSA_GROUP, NSA_HEADS)], axis=1)
    qbd = jnp.concatenate([top, bot], axis=0).astype(bf16)

    def pv(vt, p):
        pb = p.astype(bf16)
        rows = vt.shape[0] // NSA_KV_HEADS
        return [jnp.dot(vt[g * rows:(g + 1) * rows], pb[:, g * half:(g + 1) * half],
                        preferred_element_type=f32) for g in range(NSA_KV_HEADS)]

    nc = 2 * ns
    r = lax.broadcasted_iota(jnp.int32, (nc, qb), 0)
    lane = lax.broadcasted_iota(jnp.int32, (nc, qb), 1)
    cidx = jnp.where(r < ns, 2 * r, 2 * (r - ns) + 1)
    cbias = jnp.where((cidx + 1) * CMP_BLOCK - 1 <= s0 + lane, 0.0, NEG)
    sc = jnp.dot(ck_ref[...], qbd, preferred_element_type=f32) + _lane_tile(cbias, NSA_HEADS)
    m = jnp.max(sc, axis=0, keepdims=True)
    p = jnp.exp2(sc - m)
    pn = p * jnp.where(m > 0.5 * NEG, 1.0 / jnp.sum(p, axis=0, keepdims=True), 0.0)
    o_cmp = pv(cvt_ref[...], pn)

    blk = lax.broadcasted_iota(jnp.int32, (ns, qb), 0)
    qpos = s0 + lax.broadcasted_iota(jnp.int32, (ns, qb), 1)
    cur = qpos // SEL_BLOCK
    forced = (blk == 0) | (blk == cur) | (blk == cur - 1)
    for g in range(NSA_KV_HEADS):
        imp = pn[:, g * half:g * half + qb]
        for h in range(1, NSA_GROUP):
            imp = imp + pn[:, g * half + h * qb:g * half + (h + 1) * qb]
        imp = imp[:ns] + imp[ns:]
        val = jnp.where(blk > cur, -1.0, jnp.where(forced, NSA_FORCE, imp))
        bias = jnp.full((ns, qb), NEG, f32)
        for _ in range(min(NSA_TOPN, ns)):
            top = jnp.max(val, axis=0, keepdims=True)
            pick = jnp.min(jnp.where(val == top, blk, ns), axis=0, keepdims=True)
            hit = blk == pick
            bias = jnp.where(hit, 0.0, bias)
            val = jnp.where(hit, -jnp.inf, val)
        selb_ref[g] = bias

    m_ref[...] = jnp.full(m_ref.shape, NEG, f32)
    acc_ref[...] = jnp.zeros(acc_ref.shape, f32)
    per_tile = NSA_KT // SEL_BLOCK
    zpad = jnp.zeros((LANES - 16, NSA_COLS), bf16)

    def slc_tile(kt, causal):
        k0 = pl.multiple_of(kt * NSA_KT, NSA_KT)
        b0 = pl.multiple_of(kt * per_tile, per_tile)
        brow = jnp.concatenate([selb_ref[g, pl.ds(b0, per_tile), :] for g in range(NSA_KV_HEADS)
                                for _ in range(NSA_GROUP)], axis=1)
        brow = jnp.concatenate([brow, jnp.zeros((16 - per_tile, NSA_COLS), f32)], axis=0).astype(bf16)
        q_aug = jnp.concatenate([qbd, brow, zpad], axis=0)
        k_aug = jnp.concatenate([ks_ref[pl.ds(k0, NSA_KT), :], hot_ref[...]], axis=1)
        s = jnp.dot(k_aug, q_aug, preferred_element_type=f32)
        if causal:
            kpos = k0 + lax.broadcasted_iota(jnp.int32, (NSA_KT, qb), 0)
            qq = s0 + lax.broadcasted_iota(jnp.int32, (NSA_KT, qb), 1)
            s = s + _lane_tile(jnp.where(kpos <= qq, 0.0, NEG), NSA_HEADS)
        m_old = m_ref[...]
        m_new = jnp.maximum(m_old, jnp.max(s, axis=0, keepdims=True))
        alpha = jnp.exp2(m_old - m_new)
        p = jnp.exp2(s - m_new)
        m_ref[...] = m_new
        upd = pv(vst_ref[:, pl.ds(k0, NSA_KT)], p)
        for g in range(NSA_KV_HEADS):
            acc_ref[g] = acc_ref[g] * alpha[:, g * half:(g + 1) * half] + upd[g]

    kd = s0 // NSA_KT

    def body(kt, carry):
        slc_tile(kt, False)
        return carry

    lax.fori_loop(0, kd, body, 0)
    slc_tile(kd, True)
    inv_slc = [1.0 / acc_ref[g, HEAD_DIM:HEAD_DIM + 1, :] for g in range(NSA_KV_HEADS)]

    w0 = pl.multiple_of(s0, qb)
    sw = jnp.dot(kw_ref[pl.ds(w0, NSA_WSPAN), :], qbd, preferred_element_type=f32)
    rr = lax.broadcasted_iota(jnp.int32, (NSA_WSPAN, qb), 0)
    qi = lax.broadcasted_iota(jnp.int32, (NSA_WSPAN, qb), 1)
    ok = (rr >= qi) & (rr <= qi + NSA_WINDOW) & (rr + s0 >= NSA_WINDOW)
    sw = sw + _lane_tile(jnp.where(ok, 0.0, NEG), NSA_HEADS)
    pw = jnp.exp2(sw - jnp.max(sw, axis=0, keepdims=True))
    o_win = pv(vwt_ref[:, pl.ds(w0, NSA_WSPAN)], pw)
    inv_win = [1.0 / o[HEAD_DIM:HEAD_DIM + 1] for o in o_win]

    gt = jax.nn.sigmoid(sm_ref[...].T)
    outs = []
    for h in range(NSA_HEADS):
        g, hg = divmod(h, NSA_GROUP)
        c0, c1 = hg * qb, (hg + 1) * qb
        g_cmp = gt[h:h + 1]
        g_slc = gt[NSA_HEADS + h:NSA_HEADS + h + 1] * inv_slc[g][:, c0:c1]
        g_win = gt[2 * NSA_HEADS + h:2 * NSA_HEADS + h + 1] * inv_win[g][:, c0:c1]
        outs.append(o_cmp[g][:, c0:c1] * g_cmp + acc_ref[g, :HEAD_DIM, c0:c1] * g_slc
                    + o_win[g][:HEAD_DIM, c0:c1] * g_win)
    o_ref[...] = jnp.concatenate(outs, axis=0).T


def _nsa_prompt_attention(qr, small, ck, cv, ksr, vs, kwr, vw):
    l = qr.shape[0]
    ns = l // SEL_BLOCK
    nc = 2 * ns
    bf16 = jnp.bfloat16
    perm = jnp.concatenate([jnp.arange(0, nc, 2), jnp.arange(1, nc, 2)])
    ckp = ck[perm].astype(bf16)
    cvt = cv[perm].T.astype(bf16)
    pad = jnp.zeros((NSA_WINDOW, NSA_KV_W), bf16)
    kwp = jnp.concatenate([pad, kwr.astype(bf16)], axis=0)
    def with_ones(vt):
        n = vt.shape[1]
        extra = jnp.concatenate([jnp.ones((1, n), bf16), jnp.zeros((NSA_VROWS - HEAD_DIM - 1, n), bf16)], axis=0)
        return jnp.concatenate([x for g in range(NSA_KV_HEADS) for x in (vt[g * HEAD_DIM:(g + 1) * HEAD_DIM], extra)], axis=0)

    vwt = with_ones(jnp.concatenate([pad, vw.astype(bf16)], axis=0).T)
    hot = (jnp.arange(NSA_KT)[:, None] // SEL_BLOCK == jnp.arange(LANES)[None, :]).astype(bf16)
    full = lambda a: pl.BlockSpec(a.shape, lambda i: (0,) * a.ndim)
    args = (qr, small, ckp, cvt, ksr.astype(bf16), with_ones(vs.T.astype(bf16)), kwp, vwt, hot)
    return pl.pallas_call(
        functools.partial(_nsa_prompt_kernel, ns=ns),
        grid=(l // NSA_QBLOCK,),
        in_specs=[pl.BlockSpec((NSA_QBLOCK, NSA_Q_W), lambda i: (i, 0)),
                  pl.BlockSpec((NSA_QBLOCK, LANES), lambda i: (i, 0))] + [full(a) for a in args[2:]],
        out_specs=pl.BlockSpec((NSA_QBLOCK, NSA_Q_W), lambda i: (i, 0)),
        out_shape=jax.ShapeDtypeStruct((l, NSA_Q_W), jnp.float32),
        scratch_shapes=[pltpu.VMEM((NSA_KV_HEADS, ns, NSA_QBLOCK), jnp.float32),
                        pltpu.VMEM((1, NSA_COLS), jnp.float32),
                        pltpu.VMEM((NSA_KV_HEADS, NSA_VROWS, NSA_COLS // 2), jnp.float32)],
        compiler_params=pltpu.CompilerParams(dimension_semantics=("arbitrary",),
                                             vmem_limit_bytes=VMEM_LIMIT_BYTES),
        name="nsa_prompt",
    )(*args)


PAGE_W = 2 * NSA_KV_W
CMP_HIDDEN = 2 * HEAD_DIM
CMP_PER_PAGE = PAGE_SIZE // CMP_BLOCK
CMP_FLAT = CMP_BLOCK * PAGE_W
CMP_PAGES_PER_STEP = 32
STEP_PAGES = 64
STEP_ROWS = 16


def _rope_tables(pos, width):
    half = HEAD_DIM // 2
    inv_freq = ROPE_THETA ** (-2.0 * jnp.arange(half, dtype=jnp.float32) / HEAD_DIM)
    ang = pos.astype(jnp.float32)[:, None] * inv_freq[None, :]
    cos, sin = jnp.cos(ang), jnp.sin(ang)
    reps = width // HEAD_DIM
    return (jnp.tile(jnp.concatenate([cos, cos], axis=1), (1, reps)),
            jnp.tile(jnp.concatenate([-sin, sin], axis=1), (1, reps)))


def _rope_lanes(x, cos, sin_signed):
    n = x.shape[-1]
    lane = lax.broadcasted_iota(jnp.int32, x.shape, x.ndim - 1)
    first = (lane % HEAD_DIM) < HEAD_DIM // 2
    partner = jnp.where(first, pltpu.roll(x, n - HEAD_DIM // 2, x.ndim - 1), pltpu.roll(x, HEAD_DIM // 2, x.ndim - 1))
    return x * cos + partner * sin_signed


def _cmp_step_kernel(pt_ref, *refs):
    npg = CMP_PAGES_PER_STEP
    pages = refs[:npg]
    pe_ref, w1_ref, b1_ref, w2_ref, cos_ref, sin_ref, ck_ref, cv_ref = refs[npg:]
    x = jnp.concatenate([r[0] for r in pages], axis=0) + pe_ref[...]
    h = jnp.dot(x.astype(jnp.bfloat16), w1_ref[...], preferred_element_type=jnp.float32) + b1_ref[...]
    c = jnp.dot(jax.nn.gelu(h).astype(jnp.bfloat16), w2_ref[...], preferred_element_type=jnp.float32)
    ck_ref[0] = _rope_lanes(c[:, :NSA_KV_W], cos_ref[...], sin_ref[...])
    cv_ref[0] = c[:, NSA_KV_W:]


def _compress_weights(cw1, cb1, cw2, cpe):
    eye = jnp.eye(2, dtype=jnp.float32)
    w1r = cw1.reshape(2, CMP_BLOCK, HEAD_DIM, CMP_HIDDEN)
    w1 = jnp.einsum('ktdj,ka,gb->tkgdabj', w1r, eye, eye).reshape(CMP_FLAT, 4 * CMP_HIDDEN)
    b1 = jnp.broadcast_to(cb1[:, None, :], (2, 2, CMP_HIDDEN)).reshape(1, 4 * CMP_HIDDEN)
    w2 = jnp.einsum('kjd,ka,gb->kgjabd', cw2, eye, eye).reshape(4 * CMP_HIDDEN, PAGE_W)
    pe = jnp.broadcast_to(cpe.transpose(1, 0, 2)[:, :, None, :], (CMP_BLOCK, 2, 2, HEAD_DIM)).reshape(1, CMP_FLAT)
    return w1.astype(jnp.bfloat16), b1, w2.astype(jnp.bfloat16), pe


def _nsa_sample_compress(pool, layer_idx, page_table, cw1, cb1, cw2, cpe):
    n_pool, nl = pool.shape[:2]
    db, n_pages = page_table.shape
    npg = CMP_PAGES_PER_STEP
    nchunk = n_pages // npg
    nc = n_pages * CMP_PER_PAGE
    view = pool.reshape(n_pool * nl, CMP_PER_PAGE, CMP_FLAT)
    pt = (page_table * nl + layer_idx).reshape(-1).astype(jnp.int32)
    w1, b1, w2, pe = _compress_weights(cw1, cb1, cw2, cpe)
    cos, sin = _rope_tables((jnp.arange(nc) + 1) * CMP_BLOCK - 1, NSA_KV_W)
    rows = npg * CMP_PER_PAGE

    def page_map(k):
        return lambda b, c, pt_ref: (pt_ref[b * n_pages + c * npg + k], 0, 0)

    const = lambda a: pl.BlockSpec(a.shape, lambda b, c, pt_ref: (0,) * a.ndim, pipeline_mode=pl.Buffered(1))
    grid_spec = pltpu.PrefetchScalarGridSpec(
        num_scalar_prefetch=1, grid=(db, nchunk),
        in_specs=[pl.BlockSpec((1, CMP_PER_PAGE, CMP_FLAT), page_map(k)) for k in range(npg)]
        + [const(pe), const(w1), const(b1), const(w2),
           pl.BlockSpec((rows, NSA_KV_W), lambda b, c, pt_ref: (c, 0)),
           pl.BlockSpec((rows, NSA_KV_W), lambda b, c, pt_ref: (c, 0))],
        out_specs=[pl.BlockSpec((1, rows, NSA_KV_W), lambda b, c, pt_ref: (b, c, 0))] * 2)
    return pl.pallas_call(
        _cmp_step_kernel, grid_spec=grid_spec,
        out_shape=[jax.ShapeDtypeStruct((db, nc, NSA_KV_W), jnp.float32)] * 2,
        compiler_params=pltpu.CompilerParams(dimension_semantics=("arbitrary", "arbitrary"),
                                             vmem_limit_bytes=VMEM_LIMIT_BYTES),
        name="nsa_sample_compress",
    )(pt, *([view] * npg), pe, w1, b1, w2, cos, sin)


def _nt_dot(a, b):
    return lax.dot_general(a, b, (((1,), (1,)), ((), ())), preferred_element_type=jnp.float32)


def _nsa_step_kernel(pt_ref, *refs, cur, nsl):
    f32, bf16 = jnp.float32, jnp.bfloat16
    npg = STEP_PAGES
    q_ref, ck_ref, cv_ref = refs[:3]
    pages = refs[3:3 + npg]
    (win_ref, new_ref, gate_ref, exp_ref, triu_ref, o_ref, wout_ref,
     selt_ref, m_ref, l_ref, acc_ref, side_ref) = refs[3 + npg:]
    cc = pl.program_id(1)
    q16 = q_ref[0]
    qb = q16.astype(bf16)
    row16 = lax.broadcasted_iota(jnp.int32, (STEP_ROWS, 1), 0)

    def new_key_scores(krow):
        return jnp.sum(q16 * krow, axis=1, keepdims=True)

    @pl.when(cc == 0)
    def _():
        nch = ck_ref.shape[1] // 2
        halves = lambda r: jnp.concatenate([r[0, pl.ds(0, nch, stride=2), :], r[0, pl.ds(1, nch, stride=2), :]], axis=0)
        ck, cv = halves(ck_ref), halves(cv_ref)
        s = _nt_dot(qb, ck.astype(bf16))
        p = jnp.exp(s - jnp.max(s, axis=1, keepdims=True))
        pn = p / jnp.sum(p, axis=1, keepdims=True)
        o_cmp = jnp.dot(pn.astype(bf16), cv.astype(bf16), preferred_element_type=f32)

        rowp = lax.broadcasted_iota(jnp.int32, pn.shape, 0)
        row8 = lax.broadcasted_iota(jnp.int32, (8, nsl), 0)
        blk = lax.broadcasted_iota(jnp.int32, (8, nsl), 1)
        forced = (blk == 0) | (blk == cur) | (blk == cur - 1)
        key = jnp.full((8, nsl), -1, jnp.int32)
        for g in range(NSA_KV_HEADS):
            ig = jnp.sum(jnp.where((rowp >= g * NSA_GROUP) & (rowp < (g + 1) * NSA_GROUP), pn, 0.0),
                         axis=0, keepdims=True)
            ig = ig[:, :nch] + ig[:, nch:]
            ig = jnp.concatenate([ig, jnp.zeros((1, nsl - nch), f32)], axis=1)
            kg = jnp.where(blk[:1] > cur, -1,
                           jnp.where(forced[:1], lax.bitcast_convert_type(jnp.full((1, nsl), NSA_FORCE, f32), jnp.int32),
                                     lax.bitcast_convert_type(ig, jnp.int32)))
            key = jnp.where(row8 == g, kg, key)
        thr = jnp.zeros((8, 1), jnp.int32)
        for bit in range(30, -1, -1):
            cand = thr | (1 << bit)
            cnt = jnp.sum(jnp.where(key >= cand, 1.0, 0.0), axis=1, keepdims=True)
            thr = jnp.where(cnt >= NSA_TOPN, cand, thr)
        above = key > thr
        n_above = jnp.sum(jnp.where(above, 1.0, 0.0), axis=1, keepdims=True)
        tie = key == thr
        rank = jnp.dot(jnp.where(tie, 1.0, 0.0).astype(bf16), triu_ref[...], preferred_element_type=f32)
        sel = jnp.where(above | (tie & (rank <= NSA_TOPN - n_above)), 1.0, 0.0)
        selh = jnp.where(row16 < NSA_GROUP, sel[0:1], jnp.where(row16 < NSA_HEADS, sel[1:2], 0.0))
        for j in range(selt_ref.shape[0]):
            selt_ref[j] = selh[:, j * LANES:(j + 1) * LANES]

        m_ref[...] = new_key_scores(new_ref[0, 0:1, :])
        l_ref[...] = jnp.ones(l_ref.shape, f32)
        acc_ref[...] = jnp.broadcast_to(new_ref[0, 1:2, :], acc_ref.shape)

        win = win_ref[0]
        sw = _nt_dot(qb, win[:, :NSA_KV_W].astype(bf16))
        sn = new_key_scores(new_ref[0, 2:3, :])
        mw = jnp.maximum(jnp.max(sw, axis=1, keepdims=True), sn)
        pw, pnw = jnp.exp(sw - mw), jnp.exp(sn - mw)
        lw = jnp.sum(pw, axis=1, keepdims=True) + pnw
        o_win = (jnp.dot(pw.astype(bf16), win[:, NSA_KV_W:].astype(bf16), preferred_element_type=f32)
                 + pnw * new_ref[0, 3:4, :]) / lw
        gt = jax.nn.sigmoid(gate_ref[0])
        side_ref[...] = gt[:, 0:1] * o_cmp + gt[:, 2:3] * o_win
        rw = lax.broadcasted_iota(jnp.int32, win.shape, 0)
        newrow = jnp.concatenate([new_ref[0, 2:3, :], new_ref[0, 3:4, :]], axis=1)
        wout_ref[0] = jnp.where(rw == win.shape[0] - 1, newrow, pltpu.roll(win, win.shape[0] - 1, 0))

    kv = jnp.concatenate([r[0] for r in pages], axis=0)
    s = _nt_dot(qb, kv[:, :NSA_KV_W].astype(bf16))
    picked = jnp.dot(selt_ref[cc].astype(bf16), exp_ref[...], preferred_element_type=f32)
    s = s + (picked - 1.0) * (-NEG)
    m_old = m_ref[...]
    m_new = jnp.maximum(m_old, jnp.max(s, axis=1, keepdims=True))
    alpha = jnp.exp(m_old - m_new)
    p = jnp.exp(s - m_new)
    m_ref[...] = m_new
    l_ref[...] = l_ref[...] * alpha + jnp.sum(p, axis=1, keepdims=True)
    acc_ref[...] = acc_ref[...] * alpha + jnp.dot(p.astype(bf16), kv[:, NSA_KV_W:].astype(bf16),
                                                  preferred_element_type=f32)

    @pl.when(cc == pl.num_programs(1) - 1)
    def _():
        gt = jax.nn.sigmoid(gate_ref[0])
        o_ref[0] = side_ref[...] + gt[:, 1:2] * acc_ref[...] / l_ref[...]


def _nsa_sample_attention(qr, ck, cv, slc_pool, layer_idx, page_table, win_buf, newrows, gate):
    db, n_pages = page_table.shape
    n_pool, nl = slc_pool.shape[:2]
    past = n_pages * PAGE_SIZE
    cur = past // SEL_BLOCK
    nsl = -(-(cur + 1) // LANES) * LANES
    npg = STEP_PAGES
    nchunk = n_pages // npg
    keys = npg * PAGE_SIZE
    view = slc_pool.reshape(n_pool * nl, PAGE_SIZE, PAGE_W)
    pt = (page_table * nl + layer_idx).reshape(-1).astype(jnp.int32)
    f32 = jnp.float32
    hmask = (jnp.arange(NSA_HEADS)[:, None] // NSA_GROUP == jnp.arange(NSA_KV_HEADS)[None, :]).astype(f32)
    q16 = (qr * HEAD_DIM ** -0.5)[:, :, None, :] * hmask[None, :, :, None]
    q16 = jnp.pad(q16.reshape(db, NSA_HEADS, NSA_KV_W), ((0, 0), (0, STEP_ROWS - NSA_HEADS), (0, 0)))
    new8 = jnp.pad(newrows, ((0, 0), (0, 8 - newrows.shape[1]), (0, 0)))
    g16 = jnp.pad(gate.reshape(db, 3, NSA_HEADS).transpose(0, 2, 1),
                  ((0, 0), (0, STEP_ROWS - NSA_HEADS), (0, LANES - 3)))
    expand = (jnp.arange(LANES)[:, None] == jnp.arange(keys)[None, :] // SEL_BLOCK).astype(jnp.bfloat16)
    triu = (jnp.arange(nsl)[:, None] <= jnp.arange(nsl)[None, :]).astype(jnp.bfloat16)
    wlen = win_buf.shape[1]

    def page_map(k):
        return lambda b, c, pt_ref: (pt_ref[b * n_pages + c * npg + k], 0, 0)

    per_b = lambda shp: pl.BlockSpec((1,) + shp, lambda b, c, pt_ref: (b, 0, 0))
    const = lambda a: pl.BlockSpec(a.shape, lambda b, c, pt_ref: (0,) * a.ndim)
    grid_spec = pltpu.PrefetchScalarGridSpec(
        num_scalar_prefetch=1, grid=(db, nchunk),
        in_specs=[per_b((STEP_ROWS, NSA_KV_W)), per_b(ck.shape[1:]), per_b(cv.shape[1:])]
        + [pl.BlockSpec((1, PAGE_SIZE, PAGE_W), page_map(k)) for k in range(npg)]
        + [per_b((wlen, PAGE_W)), per_b((8, NSA_KV_W)), per_b((STEP_ROWS, LANES)), const(expand), const(triu)],
        out_specs=[per_b((STEP_ROWS, NSA_KV_W)), per_b((wlen, PAGE_W))],
        scratch_shapes=[pltpu.VMEM((nsl // LANES, STEP_ROWS, LANES), f32),
                        pltpu.VMEM((STEP_ROWS, 1), f32), pltpu.VMEM((STEP_ROWS, 1), f32),
                        pltpu.VMEM((STEP_ROWS, NSA_KV_W), f32), pltpu.VMEM((STEP_ROWS, NSA_KV_W), f32)])
    o16, wout = pl.pallas_call(
        functools.partial(_nsa_step_kernel, cur=cur, nsl=nsl), grid_spec=grid_spec,
        out_shape=[jax.ShapeDtypeStruct((db, STEP_ROWS, NSA_KV_W), f32),
                   jax.ShapeDtypeStruct((db, wlen, PAGE_W), f32)],
        compiler_params=pltpu.CompilerParams(dimension_semantics=("arbitrary", "arbitrary"),
                                             vmem_limit_bytes=VMEM_LIMIT_BYTES),
        name="nsa_sample_attention",
    )(pt, q16, ck, cv, *([view] * npg), win_buf, new8, g16, expand, triu)
    o = o16[:, :NSA_HEADS].reshape(db, NSA_HEADS, NSA_KV_HEADS, HEAD_DIM)
    o = jnp.take_along_axis(o, (jnp.arange(NSA_HEADS) // NSA_GROUP)[None, :, None, None], axis=2)
    return o.reshape(db, NSA_HEADS * HEAD_DIM), wout


CMP_STEP_LANES = CMP_PAGES_PER_STEP * CMP_PER_PAGE
CMP_FEATURE_GROUP = 16


def _cmp_lane_blocks(n_pages):
    lane = np.arange(n_pages * CMP_PER_PAGE)
    step, rem = lane // CMP_STEP_LANES, lane % CMP_STEP_LANES
    j, pl_ = rem // CMP_PAGES_PER_STEP, rem % CMP_PAGES_PER_STEP
    return (step * CMP_PAGES_PER_STEP + pl_) * CMP_PER_PAGE + j


def _cmp_step_t_kernel(pt_ref, *refs):
    f32 = jnp.float32
    npg = CMP_PAGES_PER_STEP
    pages = refs[:npg]
    pe_ref, w1_ref, b1_ref, w2_ref, cos_ref, sin_ref, ck_ref, cv_ref, slab_ref = refs[npg:]
    outs = (ck_ref, cv_ref)
    for k, r in enumerate(pages):
        for s in range(2 * NSA_KV_HEADS):
            slab_ref[s, k * HEAD_DIM:(k + 1) * HEAD_DIM, :] = r[0, s]
    for kv in range(2):
        h = jnp.zeros((NSA_KV_HEADS * npg, CMP_PER_PAGE * CMP_HIDDEN), f32)
        for dg in range(HEAD_DIM // CMP_FEATURE_GROUP):
            x = jnp.concatenate(
                [jnp.concatenate([slab_ref[2 * kv + g, pl.ds(d, npg, stride=HEAD_DIM), :]
                                  for g in range(NSA_KV_HEADS)], axis=0) + pe_ref[kv, d]
                 for d in range(dg * CMP_FEATURE_GROUP, (dg + 1) * CMP_FEATURE_GROUP)], axis=1)
            h = h + jnp.dot(x.astype(jnp.bfloat16), w1_ref[kv, dg], preferred_element_type=f32)
        act = jax.nn.gelu(h + b1_ref[kv]).astype(jnp.bfloat16)
        ct = _nt_dot(w2_ref[kv], act)
        tile = jnp.concatenate(
            [jnp.concatenate([ct[j * HEAD_DIM:(j + 1) * HEAD_DIM, g * npg:(g + 1) * npg] for j in range(CMP_PER_PAGE)],
                             axis=1) for g in range(NSA_KV_HEADS)], axis=0)
        if kv == 0:
            row = lax.broadcasted_iota(jnp.int32, tile.shape, 0)
            n = tile.shape[0]
            partner = jnp.where((row % HEAD_DIM) < HEAD_DIM // 2, pltpu.roll(tile, n - HEAD_DIM // 2, 0),
                                pltpu.roll(tile, HEAD_DIM // 2, 0))
            tile = tile * cos_ref[...] + partner * sin_ref[...]
        outs[kv][0] = tile


def _compress_weights_t(cw1, cb1, cw2, cpe):
    eye = jnp.eye(CMP_PER_PAGE, dtype=jnp.float32)
    w1r = cw1.reshape(2, CMP_BLOCK, HEAD_DIM, CMP_HIDDEN)
    w1 = jnp.einsum('ktdn,ja->kdjtan', w1r, eye).reshape(
        2, HEAD_DIM // CMP_FEATURE_GROUP, CMP_FEATURE_GROUP * PAGE_SIZE, CMP_PER_PAGE * CMP_HIDDEN)
    b1 = jnp.tile(cb1, (1, CMP_PER_PAGE))[:, None, :]
    w2 = jnp.einsum('knd,ja->kjdan', cw2, eye).reshape(2, CMP_PER_PAGE * HEAD_DIM, CMP_PER_PAGE * CMP_HIDDEN)
    pe = jnp.tile(cpe.transpose(0, 2, 1), (1, 1, CMP_PER_PAGE))[:, :, None, :]
    return w1.astype(jnp.bfloat16), b1, w2.astype(jnp.bfloat16), pe


def _nsa_sample_compress_t(pool, layer_idx, page_table, cw1, cb1, cw2, cpe):
    n_pool, nl = pool.shape[:2]
    db, n_pages = page_table.shape
    npg = CMP_PAGES_PER_STEP
    nchunk = n_pages // npg
    nc = n_pages * CMP_PER_PAGE
    view = jnp.transpose(pool, (0, 1, 3, 4, 5, 2)).reshape(n_pool * nl, 2 * NSA_KV_HEADS, HEAD_DIM, PAGE_SIZE)
    pt = (page_table * nl + layer_idx).reshape(-1).astype(jnp.int32)
    w1, b1, w2, pe = _compress_weights_t(cw1, cb1, cw2, cpe)
    pos = (jnp.asarray(_cmp_lane_blocks(n_pages)) + 1) * CMP_BLOCK - 1
    half = HEAD_DIM // 2
    inv_freq = ROPE_THETA ** (-2.0 * jnp.arange(half, dtype=jnp.float32) / HEAD_DIM)
    ang = inv_freq[:, None] * pos.astype(jnp.float32)[None, :]
    cos = jnp.tile(jnp.cos(ang), (2 * NSA_KV_HEADS, 1))
    sin = jnp.tile(jnp.concatenate([-jnp.sin(ang), jnp.sin(ang)], axis=0), (NSA_KV_HEADS, 1))

    def page_map(k):
        return lambda b, c, pt_ref: (pt_ref[b * n_pages + c * npg + k], 0, 0, 0)

    const = lambda a: pl.BlockSpec(a.shape, lambda b, c, pt_ref: (0,) * a.ndim, pipeline_mode=pl.Buffered(1))
    lanes_c = lambda: pl.BlockSpec((NSA_KV_W, CMP_STEP_LANES), lambda b, c, pt_ref: (0, c))
    grid_spec = pltpu.PrefetchScalarGridSpec(
        num_scalar_prefetch=1, grid=(db, nchunk),
        in_specs=[pl.BlockSpec((1, 2 * NSA_KV_HEADS, HEAD_DIM, PAGE_SIZE), page_map(k)) for k in range(npg)]
        + [const(pe), const(w1), const(b1), const(w2), lanes_c(), lanes_c()],
        out_specs=[pl.BlockSpec((1, NSA_KV_W, CMP_STEP_LANES), lambda b, c, pt_ref: (b, 0, c))] * 2,
        scratch_shapes=[pltpu.VMEM((2 * NSA_KV_HEADS, npg * HEAD_DIM, PAGE_SIZE), jnp.float32)])
    return pl.pallas_call(
        _cmp_step_t_kernel, grid_spec=grid_spec,
        out_shape=[jax.ShapeDtypeStruct((db, NSA_KV_W, nc), jnp.float32)] * 2,
        compiler_params=pltpu.CompilerParams(dimension_semantics=("arbitrary", "arbitrary"),
                                             vmem_limit_bytes=VMEM_LIMIT_BYTES),
        name="nsa_sample_compress",
    )(pt, *([view] * npg), pe, w1, b1, w2, cos, sin)


def _nsa_step_t_kernel(pt_ref, *refs, topn):
    f32, bf16 = jnp.float32, jnp.bfloat16
    npg = STEP_PAGES
    q_ref, ck_ref, cv_ref = refs[:3]
    pages = refs[3:3 + npg]
    (win_ref, newr_ref, newt_ref, gate_ref, blk_ref, exp_ref, o_ref, wout_ref,
     selt_ref, m_ref, l_ref, acc_ref, side_ref) = refs[3 + npg:]
    cc = pl.program_id(1)
    q16 = q_ref[0]
    qb = q16.astype(bf16)
    row16 = lax.broadcasted_iota(jnp.int32, (STEP_ROWS, 1), 0)

    def new_key_scores(krow):
        return jnp.sum(q16 * krow, axis=1, keepdims=True)

    @pl.when(cc == 0)
    def _():
        nc = ck_ref.shape[2]
        s = jnp.dot(qb, ck_ref[0].astype(bf16), preferred_element_type=f32)
        p = jnp.exp(s - jnp.max(s, axis=1, keepdims=True))
        pn = p / jnp.sum(p, axis=1, keepdims=True)
        o_cmp = _nt_dot(pn.astype(bf16), cv_ref[0].astype(bf16))

        rowp = lax.broadcasted_iota(jnp.int32, pn.shape, 0)
        row8 = lax.broadcasted_iota(jnp.int32, (8, nc), 0)
        blk = blk_ref[...]
        val = jnp.full((8, nc), -jnp.inf, f32)
        for g in range(NSA_KV_HEADS):
            ig = jnp.sum(jnp.where((rowp >= g * NSA_GROUP) & (rowp < (g + 1) * NSA_GROUP), pn, 0.0),
                         axis=0, keepdims=True)
            ig = ig + pltpu.roll(ig, nc - CMP_PAGES_PER_STEP, 1)
            vg = jnp.where(blk[:1] < 0, -jnp.inf, jnp.where(blk[1:2] > 0, NSA_FORCE, ig))
            val = jnp.where(row8 == g, vg, val)
        sblk = jnp.where(blk[:1] < 0, nc, blk[:1])
        sel = jnp.zeros((8, nc), f32)
        for _ in range(topn):
            top = jnp.max(val, axis=1, keepdims=True)
            pick = jnp.min(jnp.where(val == top, sblk, nc), axis=1, keepdims=True)
            hit = sblk == pick
            sel = jnp.where(hit, 1.0, sel)
            val = jnp.where(hit, -jnp.inf, val)
        selh = jnp.where(row16 < NSA_GROUP, sel[0:1], jnp.where(row16 < NSA_HEADS, sel[1:2], 0.0))
        wsel = selt_ref.shape[2]
        for j in range(selt_ref.shape[0]):
            selt_ref[j] = selh[:, j * wsel:(j + 1) * wsel]

        m_ref[...] = new_key_scores(newr_ref[0, 0:1, :])
        l_ref[...] = jnp.ones(l_ref.shape, f32)
        acc_ref[...] = jnp.broadcast_to(newr_ref[0, 1:2, :], acc_ref.shape)

        sw = jnp.dot(qb, win_ref[0, 0].astype(bf16), preferred_element_type=f32)
        sn = new_key_scores(newr_ref[0, 2:3, :])
        mw = jnp.maximum(jnp.max(sw, axis=1, keepdims=True), sn)
        pw, pnw = jnp.exp(sw - mw), jnp.exp(sn - mw)
        lw = jnp.sum(pw, axis=1, keepdims=True) + pnw
        o_win = (_nt_dot(pw.astype(bf16), win_ref[0, 1].astype(bf16)) + pnw * newr_ref[0, 3:4, :]) / lw
        gt = jax.nn.sigmoid(gate_ref[0])
        side_ref[...] = gt[:, 0:1] * o_cmp + gt[:, 2:3] * o_win
        wl = win_ref.shape[3]
        lane = lax.broadcasted_iota(jnp.int32, (NSA_KV_W, wl), 1)
        for kv in range(2):
            wout_ref[0, kv] = jnp.where(lane == wl - 1, newt_ref[0, :, 2 + kv:3 + kv],
                                        pltpu.roll(win_ref[0, kv], wl - 1, 1))

    kt = jnp.concatenate([r[0, 0] for r in pages], axis=1)
    vt = jnp.concatenate([r[0, 1] for r in pages], axis=1)
    s = jnp.dot(qb, kt.astype(bf16), preferred_element_type=f32)
    picked = jnp.dot(selt_ref[cc].astype(bf16), exp_ref[...], preferred_element_type=f32)
    s = s + (picked - 1.0) * (-NEG)
    m_old = m_ref[...]
    m_new = jnp.maximum(m_old, jnp.max(s, axis=1, keepdims=True))
    alpha = jnp.exp(m_old - m_new)
    p = jnp.exp(s - m_new)
    m_ref[...] = m_new
    l_ref[...] = l_ref[...] * alpha + jnp.sum(p, axis=1, keepdims=True)
    acc_ref[...] = acc_ref[...] * alpha + _nt_dot(p.astype(bf16), vt.astype(bf16))

    @pl.when(cc == pl.num_programs(1) - 1)
    def _():
        gt = jax.nn.sigmoid(gate_ref[0])
        o_ref[0] = side_ref[...] + gt[:, 1:2] * acc_ref[...] / l_ref[...]


def _nsa_sample_attention_t(qr, ckt, cvt, slc_pool, layer_idx, page_table, win_buf, newrows, gate):
    db, n_pages = page_table.shape
    n_pool, nl = slc_pool.shape[:2]
    wlen = win_buf.shape[1]
    nc = ckt.shape[2]
    past = n_pages * PAGE_SIZE
    cur = past // SEL_BLOCK
    npg = STEP_PAGES
    nchunk = n_pages // npg
    keys = npg * PAGE_SIZE
    wsel = npg * CMP_PER_PAGE
    f32, bf16 = jnp.float32, jnp.bfloat16
    view = jnp.transpose(slc_pool, (0, 1, 3, 4, 5, 2)).reshape(n_pool * nl, 2, NSA_KV_W, PAGE_SIZE)
    wint = jnp.transpose(win_buf, (0, 2, 3, 4, 1)).reshape(db, 2, NSA_KV_W, wlen)
    pt = (page_table * nl + layer_idx).reshape(-1).astype(jnp.int32)
    hmask = (jnp.arange(NSA_HEADS)[:, None] // NSA_GROUP == jnp.arange(NSA_KV_HEADS)[None, :]).astype(f32)
    q16 = (qr * HEAD_DIM ** -0.5)[:, :, None, :] * hmask[None, :, :, None]
    q16 = jnp.pad(q16.reshape(db, NSA_HEADS, NSA_KV_W), ((0, 0), (0, STEP_ROWS - NSA_HEADS), (0, 0)))
    newr = jnp.pad(newrows, ((0, 0), (0, 8 - newrows.shape[1]), (0, 0)))
    newt = jnp.pad(newrows.transpose(0, 2, 1), ((0, 0), (0, 0), (0, LANES - newrows.shape[1])))
    g16 = jnp.pad(gate.reshape(db, 3, NSA_HEADS).transpose(0, 2, 1),
                  ((0, 0), (0, STEP_ROWS - NSA_HEADS), (0, LANES - 3)))
    cblk = _cmp_lane_blocks(n_pages)
    jj = cblk % CMP_PER_PAGE
    sblk = np.where(jj % 2 == 0, cblk // 2, -1)
    forced = ((sblk == 0) | (sblk == cur - 1)).astype(np.int32)
    blk8 = np.zeros((8, nc), np.int32)
    blk8[0], blk8[1] = sblk, forced
    loc = np.arange(wsel)
    lstep, lrem = loc // CMP_STEP_LANES, loc % CMP_STEP_LANES
    lj, lpage = lrem // CMP_PAGES_PER_STEP, lstep * CMP_PAGES_PER_STEP + lrem % CMP_PAGES_PER_STEP
    kidx = np.arange(keys)
    expand = ((lj[:, None] % 2 == 0) & (kidx[None, :] // PAGE_SIZE == lpage[:, None])
              & ((kidx[None, :] % PAGE_SIZE) // SEL_BLOCK == lj[:, None] // 2)).astype(np.float32)
    topn = min(NSA_TOPN, cur + 1) - 1

    def page_map(k):
        return lambda b, c, pt_ref: (pt_ref[b * n_pages + c * npg + k], 0, 0, 0)

    per_b = lambda shp: pl.BlockSpec((1,) + shp, lambda b, c, pt_ref: (b,) + (0,) * len(shp))
    const = lambda a: pl.BlockSpec(a.shape, lambda b, c, pt_ref: (0,) * a.ndim)
    consts = (jnp.asarray(blk8), jnp.asarray(expand, bf16))
    grid_spec = pltpu.PrefetchScalarGridSpec(
        num_scalar_prefetch=1, grid=(db, nchunk),
        in_specs=[per_b((STEP_ROWS, NSA_KV_W)), per_b((NSA_KV_W, nc)), per_b((NSA_KV_W, nc))]
        + [pl.BlockSpec((1, 2, NSA_KV_W, PAGE_SIZE), page_map(k)) for k in range(npg)]
        + [per_b((2, NSA_KV_W, wlen)), per_b((8, NSA_KV_W)), per_b((NSA_KV_W, LANES)), per_b((STEP_ROWS, LANES))]
        + [const(a) for a in consts],
        out_specs=[per_b((STEP_ROWS, NSA_KV_W)), per_b((2, NSA_KV_W, wlen))],
        scratch_shapes=[pltpu.VMEM((nc // wsel, STEP_ROWS, wsel), f32),
                        pltpu.VMEM((STEP_ROWS, 1), f32), pltpu.VMEM((STEP_ROWS, 1), f32),
                        pltpu.VMEM((STEP_ROWS, NSA_KV_W), f32), pltpu.VMEM((STEP_ROWS, NSA_KV_W), f32)])
    o16, wout = pl.pallas_call(
        functools.partial(_nsa_step_t_kernel, topn=topn), grid_spec=grid_spec,
        out_shape=[jax.ShapeDtypeStruct((db, STEP_ROWS, NSA_KV_W), f32),
                   jax.ShapeDtypeStruct((db, 2, NSA_KV_W, wlen), f32)],
        compiler_params=pltpu.CompilerParams(dimension_semantics=("arbitrary", "arbitrary"),
                                             vmem_limit_bytes=VMEM_LIMIT_BYTES),
        name="nsa_sample_attention",
    )(pt, q16, ckt, cvt, *([view] * npg), wint, newr, newt, g16, *consts)
    o = o16[:, :NSA_HEADS].reshape(db, NSA_HEADS, NSA_KV_HEADS, HEAD_DIM)
    o = jnp.take_along_axis(o, (jnp.arange(NSA_HEADS) // NSA_GROUP)[None, :, None, None], axis=2)
    wout = jnp.transpose(wout.reshape(db, 2, NSA_KV_HEADS, HEAD_DIM, wlen), (0, 4, 1, 2, 3))
    return o.reshape(db, NSA_HEADS * HEAD_DIM), wout


DIL_ROW_CHUNK = 64


def _dil_step_kernel(q_ref, buf_ref, newt_ref, newr_ref, bias_ref, o_ref, out_ref, p_ref, pn_ref, den_ref):
    f32, bf16 = jnp.float32, jnp.bfloat16
    kv = pl.program_id(1)
    wlen = buf_ref.shape[3]
    nrow = buf_ref.shape[2]
    q16 = q_ref[0]

    @pl.when(kv == 0)
    def _():
        s = jnp.dot(q16.astype(bf16), buf_ref[0, 0].astype(bf16), preferred_element_type=f32)
        s_new = jnp.sum(q16 * newr_ref[0, 0:1, :], axis=1, keepdims=True)
        ms, es, ens, dens = [], [], [], []
        for g in range(len(DIL_GROUPS)):
            sg = s + bias_ref[g:g + 1, :]
            m = jnp.maximum(jnp.max(sg, axis=1, keepdims=True), s_new)
            e, en = jnp.exp(sg - m), jnp.exp(s_new - m)
            ms.append(m); es.append(e); ens.append(en)
            dens.append(jnp.sum(e, axis=1, keepdims=True) + en)
        m_all = functools.reduce(jnp.maximum, ms)
        ws = [jnp.exp(m - m_all) for m in ms]
        p_ref[...] = sum(w * e for w, e in zip(ws, es))
        pn_ref[...] = sum(w * en for w, en in zip(ws, ens))
        den_ref[...] = sum(w * d for w, d in zip(ws, dens))

    @pl.when(kv == 1)
    def _():
        r = _nt_dot(p_ref[...].astype(bf16), buf_ref[0, 0].astype(bf16))
        r = (r + pn_ref[...] * newr_ref[0, 1:2, :]) / den_ref[...]
        head = lax.broadcasted_iota(jnp.int32, r.shape, 1) // HEAD_DIM
        row = lax.broadcasted_iota(jnp.int32, r.shape, 0)
        o_ref[0] = jnp.broadcast_to(jnp.sum(jnp.where(head == row, r, 0.0), axis=0, keepdims=True), o_ref.shape[1:])

    lane = lax.broadcasted_iota(jnp.int32, (DIL_ROW_CHUNK, wlen), 1)
    for c in range(nrow // DIL_ROW_CHUNK):
        rs = slice(c * DIL_ROW_CHUNK, (c + 1) * DIL_ROW_CHUNK)
        col = jnp.where(kv == 0, newt_ref[0, rs, 0:1], newt_ref[0, rs, 1:2])
        out_ref[0, 0, rs, :] = jnp.where(lane == wlen - 1, col, pltpu.roll(buf_ref[0, 0, rs, :], wlen - 1, 1))


def _dil_sample_attention(qr, kr_new, v_new, buf):
    db, wlen = buf.shape[:2]
    f32 = jnp.float32
    buft = jnp.transpose(buf, (0, 2, 3, 4, 1)).reshape(db, 2, DIL_W, wlen)
    eye = jnp.eye(DIL_HEADS, dtype=f32)
    q16 = ((qr * HEAD_DIM ** -0.5)[:, :, None, :] * eye[None, :, :, None]).reshape(db, DIL_HEADS, DIL_W)
    newr = jnp.pad(jnp.stack([kr_new, v_new], axis=1), ((0, 0), (0, 6), (0, 0)))
    newt = jnp.pad(jnp.stack([kr_new, v_new], axis=2), ((0, 0), (0, 0), (0, LANES - 2)))
    back = wlen - jnp.arange(wlen)
    bias = jnp.stack([jnp.where((back % d == 0) & (back // d <= DIL_SPAN), 0.0, NEG) for _, d in DIL_GROUPS])
    bias = jnp.pad(bias, ((0, 8 - len(DIL_GROUPS)), (0, 0))).astype(f32)
    o, new_buf = pl.pallas_call(
        _dil_step_kernel,
        grid=(db, 2),
        in_specs=[pl.BlockSpec((1, DIL_HEADS, DIL_W), lambda b, k: (b, 0, 0)),
                  pl.BlockSpec((1, 1, DIL_W, wlen), lambda b, k: (b, k, 0, 0)),
                  pl.BlockSpec((1, DIL_W, LANES), lambda b, k: (b, 0, 0)),
                  pl.BlockSpec((1, 8, DIL_W), lambda b, k: (b, 0, 0)),
                  pl.BlockSpec((8, wlen), lambda b, k: (0, 0))],
        out_specs=[pl.BlockSpec((1, 8, DIL_W), lambda b, k: (b, 0, 0)),
                   pl.BlockSpec((1, 1, DIL_W, wlen), lambda b, k: (b, k, 0, 0))],
        out_shape=[jax.ShapeDtypeStruct((db, 8, DIL_W), f32),
                   jax.ShapeDtypeStruct((db, 2, DIL_W, wlen), f32)],
        scratch_shapes=[pltpu.VMEM((DIL_HEADS, wlen), f32), pltpu.VMEM((DIL_HEADS, 1), f32),
                        pltpu.VMEM((DIL_HEADS, 1), f32)],
        compiler_params=pltpu.CompilerParams(dimension_semantics=("arbitrary", "arbitrary"),
                                             vmem_limit_bytes=VMEM_LIMIT_BYTES),
        name="dil_sample",
    )(q16, buft, newt, newr, bias)
    new_buf = jnp.transpose(new_buf.reshape(db, 2, DIL_HEADS, HEAD_DIM, wlen), (0, 4, 1, 2, 3))
    return o[:, 0], new_buf


def _dil_band_kernel(q_ref, kp_ref, kc_ref, vp_ref, vc_ref, num_ref, st_ref):
    f32, bf16 = jnp.float32, jnp.bfloat16
    blk = DIL_BLOCK
    n = pl.program_id(0)
    i = lax.broadcasted_iota(jnp.int32, (blk, 2 * blk), 0)
    j = lax.broadcasted_iota(jnp.int32, (blk, 2 * blk), 1) - blk
    ok = (i - j >= 0) & (i - j <= DIL_SPAN) & (n * blk + j >= 0)
    bias = jnp.where(ok, 0.0, NEG)
    bias = jnp.concatenate([bias, bias], axis=0)
    lane = lax.broadcasted_iota(jnp.int32, (blk, LANES), 1)
    first = lane < HEAD_DIM
    stats = jnp.zeros((blk, LANES), f32)
    for p in range(DIL_HEADS // 2):
        cols = slice(p * LANES, (p + 1) * LANES)
        qp = q_ref[:, cols] * (HEAD_DIM ** -0.5)
        qst = jnp.concatenate([jnp.where(first, qp, 0.0), jnp.where(first, 0.0, qp)], axis=0).astype(bf16)
        kk = jnp.concatenate([kp_ref[:, cols], kc_ref[:, cols]], axis=0).astype(bf16)
        vv = jnp.concatenate([vp_ref[:, cols], vc_ref[:, cols]], axis=0).astype(bf16)
        s = _nt_dot(qst, kk) + bias
        m = jnp.max(s, axis=1, keepdims=True)
        e = jnp.exp(s - m)
        den = jnp.sum(e, axis=1, keepdims=True)
        nm = jnp.dot(e.astype(bf16), vv, preferred_element_type=f32)
        num_ref[:, cols] = jnp.where(first, nm[:blk], nm[blk:])
        for a in range(2):
            h = 2 * p + a
            stats = jnp.where(lane == h, m[a * blk:(a + 1) * blk], stats)
            stats = jnp.where(lane == DIL_HEADS + h, den[a * blk:(a + 1) * blk], stats)
    st_ref[...] = stats


def _dil_band_stats(qr, kr, v, d):
    l = qr.shape[0]
    assert l % (d * DIL_BLOCK) == 0
    nb = l // (d * DIL_BLOCK)
    view = lambda a: a.reshape(l // d, d * a.shape[1])
    blk = lambda w, prev: pl.BlockSpec((DIL_BLOCK, w),
                                       (lambda n, r: (jnp.maximum(n - 1, 0), r)) if prev else (lambda n, r: (n, r)))
    num, st = pl.pallas_call(
        _dil_band_kernel,
        grid=(nb, d),
        in_specs=[blk(DIL_W, False), blk(DIL_W, True), blk(DIL_W, False), blk(DIL_W, True), blk(DIL_W, False)],
        out_specs=[blk(DIL_W, False), blk(LANES, False)],
        out_shape=[jax.ShapeDtypeStruct((l // d, d * DIL_W), jnp.float32),
                   jax.ShapeDtypeStruct((l // d, d * LANES), jnp.float32)],
        compiler_params=pltpu.CompilerParams(dimension_semantics=("arbitrary", "arbitrary"),
                                             vmem_limit_bytes=VMEM_LIMIT_BYTES),
        name="dil_band_stats",
    )(view(qr), view(kr), view(kr), view(v), view(v))
    st = st.reshape(l, LANES)
    return st[:, :DIL_HEADS], st[:, DIL_HEADS:2 * DIL_HEADS], num.reshape(l, DIL_HEADS, HEAD_DIM)


GDN_PREP_ROWS = 512
GDN_TILE_CHUNKS = 4
GDN_PAIRS = GDN_HEADS // 2
GDN_A_LANE = 3 * NSA_HEADS
GDN_B_LANE = GDN_A_LANE + GDN_HEADS


def _hi_lo(x):
    hi = x.astype(jnp.bfloat16)
    return hi, (x - hi.astype(jnp.float32)).astype(jnp.bfloat16)


def _three_way(x):
    f32 = jnp.float32
    x1 = x.astype(jnp.bfloat16)
    r1 = x - x1.astype(f32)
    x2 = r1.astype(jnp.bfloat16)
    return x1, x2, (r1 - x2.astype(f32)).astype(jnp.bfloat16)


def _dot_select(x, sel):
    return sum(jnp.dot(piece, sel, preferred_element_type=jnp.float32) for piece in _three_way(x))


def _select_dot(sel, x):
    return sum(jnp.dot(sel, piece, preferred_element_type=jnp.float32) for piece in _three_way(x))


def _dot_hl(a, b):
    f32 = jnp.float32
    ah, al = _hi_lo(a)
    bh, bl = _hi_lo(b)
    return (jnp.dot(ah, bh, preferred_element_type=f32) + jnp.dot(ah, bl, preferred_element_type=f32)
            + jnp.dot(al, bh, preferred_element_type=f32))


def _gdn_prep_kernel(u_ref, sm_ref, cw_ref, prm_ref, ea_ref, eb_ref, eh_ref, q_ref, k_ref, v_ref, g_ref, b_ref,
                     carry_ref):
    f32 = jnp.float32
    i = pl.program_id(0)
    tl = u_ref.shape[0]

    @pl.when(i == 0)
    def _():
        carry_ref[...] = jnp.zeros(carry_ref.shape, f32)

    u = u_ref[...]
    prev = carry_ref[...]
    row = lax.broadcasted_iota(jnp.int32, u.shape, 0)

    def shifted(k):
        r = pltpu.roll(u, k, 0)
        for j in range(k):
            r = jnp.where(row == j, prev[8 - k + j:8 - k + j + 1], r)
        return r

    cw = cw_ref[...]
    c = cw[0:1] * shifted(3) + cw[1:2] * shifted(2) + cw[2:3] * shifted(1) + cw[3:4] * u
    carry_ref[...] = u[tl - 8:]
    c = c * jax.nn.sigmoid(c)
    eh = eh_ref[...]

    def l2n(x):
        return x * lax.rsqrt(_dot_select(x * x, eh) + NORM_EPS)

    q_ref[...] = l2n(c[:, :GDN_W]) * (HEAD_DIM ** -0.5)
    k_ref[...] = l2n(c[:, GDN_W:2 * GDN_W])
    v_ref[...] = c[:, 2 * GDN_W:]
    sm = sm_ref[...]
    x = sm + prm_ref[1:2]
    softplus = jnp.maximum(x, 0.0) + jnp.log(1.0 + jnp.exp(-jnp.abs(x)))
    g_ref[...] = _dot_select(-jnp.exp(prm_ref[0:1]) * softplus, ea_ref[...])
    b_ref[...] = _dot_select(jax.nn.sigmoid(sm), eb_ref[...])


def _gdn_chunk_kernel(q_ref, k_ref, v_ref, g_ref, b_ref, z_ref, nw_ref, lt_ref, eh_ref, o_ref, s_out_ref, s_ref):
    f32 = jnp.float32
    ch = GDN_CHUNK
    i = pl.program_id(0)

    @pl.when(i == 0)
    def _():
        s_ref[...] = jnp.zeros(s_ref.shape, f32)

    lane = lax.broadcasted_iota(jnp.int32, (ch, LANES), 1)
    first = lane < HEAD_DIM
    stack = lambda x: jnp.concatenate([jnp.where(first, x, 0.0), jnp.where(first, 0.0, x)], axis=0)
    r2 = lax.broadcasted_iota(jnp.int32, (2 * ch, 2 * ch), 0)
    c2 = lax.broadcasted_iota(jnp.int32, (2 * ch, 2 * ch), 1)
    same = (r2 // ch) == (c2 // ch)
    tri = same & (r2 % ch >= c2 % ch)
    strict = same & (r2 % ch > c2 % ch)
    eye = r2 == c2
    eye_f = jnp.where(eye, 1.0, 0.0)
    diag2 = lax.broadcasted_iota(jnp.int32, (ch, LANES), 0) == lane % HEAD_DIM
    lt = lt_ref[...]
    bf = lambda x: x.astype(jnp.bfloat16)
    dot = lambda a, b: jnp.dot(bf(a), bf(b), preferred_element_type=f32)

    for c in range(GDN_TILE_CHUNKS):
        rs = slice(c * ch, (c + 1) * ch)
        outs = []
        for p in range(GDN_PAIRS):
            cols = slice(p * LANES, (p + 1) * LANES)
            kk, qq, vv, bb = k_ref[rs, cols], q_ref[rs, cols], v_ref[rs, cols], b_ref[rs, cols]
            gc = _select_dot(lt, g_ref[rs, cols])
            eg = jnp.exp(gc)
            g_end = gc[ch - 1:ch]
            kb = kk * bb
            col = jnp.concatenate([jnp.broadcast_to(gc[:, 0:1], (ch, LANES)),
                                   jnp.broadcast_to(gc[:, HEAD_DIM:HEAD_DIM + 1], (ch, LANES))], axis=0)
            rowv = jnp.sum(jnp.where(diag2, gc, 0.0), axis=0, keepdims=True)
            gam = jnp.where(tri, jnp.exp(jnp.where(tri, col - rowv, 0.0)), 0.0)
            kst = stack(kk)
            a = jnp.where(strict, _nt_dot(bf(stack(kb)), bf(kst)) * gam, 0.0)
            qk = jnp.where(tri, _nt_dot(bf(stack(qq)), bf(kst)) * gam, 0.0)
            x = eye_f - a
            pw = _dot_hl(a, a)
            steps = GDN_CHUNK.bit_length() - 2
            for r in range(steps):
                x = x + _dot_hl(x, pw)
                if r < steps - 1:
                    pw = _dot_hl(pw, pw)
            uu = dot(x, stack(vv * bb))
            ww = dot(x, stack(kb * eg))
            s = s_ref[p]
            v_new = uu - dot(ww, s)
            o_st = dot(stack(qq * eg), s) + dot(qk, v_new)
            dec = jnp.sum(jnp.where(eye, jnp.exp(g_end), 0.0), axis=1, keepdims=True)
            kd = stack(kk * jnp.exp(g_end - gc))
            s_ref[p] = s * dec + dot(kd.T, v_new)
            outs.append(o_st[:ch] + o_st[ch:])
        o = jnp.concatenate(outs, axis=1)
        ms = _dot_select(o * o, eh_ref[...]) * (1.0 / HEAD_DIM)
        z = z_ref[rs, :]
        o_ref[rs, :] = o * lax.rsqrt(ms + NORM_EPS) * nw_ref[...] * (z * jax.nn.sigmoid(z))

    @pl.when(i == pl.num_programs(0) - 1)
    def _():
        s_out_ref[...] = s_ref[...]


def _gdn_prompt(qkv, small, z, conv_w, a_log, dt_bias, norm_w):
    l = qkv.shape[0]
    f32, bf16 = jnp.float32, jnp.bfloat16
    w = GDN_W
    hh = jnp.arange(w) // HEAD_DIM
    expander = lambda base: (jnp.arange(LANES)[:, None] == base + hh[None, :]).astype(bf16)
    eh = (hh[:, None] == hh[None, :]).astype(bf16)
    cw8 = jnp.zeros((8, 3 * w), f32).at[:GDN_CONV].set(conv_w)
    prm = jnp.zeros((8, LANES), f32)
    prm = prm.at[0, GDN_A_LANE:GDN_A_LANE + GDN_HEADS].set(a_log).at[1, GDN_A_LANE:GDN_A_LANE + GDN_HEADS].set(dt_bias)
    tl = GDN_PREP_ROWS
    row = lambda wd: pl.BlockSpec((tl, wd), lambda i: (i, 0))
    const = lambda a: pl.BlockSpec(a.shape, lambda i: (0,) * a.ndim)
    ea, eb = expander(GDN_A_LANE), expander(GDN_B_LANE)
    q, k, v, g, b = pl.pallas_call(
        _gdn_prep_kernel,
        grid=(l // tl,),
        in_specs=[row(3 * w), row(LANES), const(cw8), const(prm), const(ea), const(eb), const(eh)],
        out_specs=[row(w)] * 5,
        out_shape=[jax.ShapeDtypeStruct((l, w), f32)] * 5,
        scratch_shapes=[pltpu.VMEM((8, 3 * w), f32)],
        compiler_params=pltpu.CompilerParams(dimension_semantics=("arbitrary",), vmem_limit_bytes=VMEM_LIMIT_BYTES),
        name="gdn_prep",
    )(qkv, small, cw8, prm, ea, eb, eh)
    tc = GDN_TILE_CHUNKS * GDN_CHUNK
    lt = (jnp.arange(GDN_CHUNK)[:, None] >= jnp.arange(GDN_CHUNK)[None, :]).astype(bf16)
    nw = jnp.tile(norm_w, GDN_HEADS).reshape(1, w)
    rowc = pl.BlockSpec((tc, w), lambda i: (i, 0))
    o, s_bd = pl.pallas_call(
        _gdn_chunk_kernel,
        grid=(l // tc,),
        in_specs=[rowc] * 6 + [const(nw), const(lt), const(eh)],
        out_specs=[rowc, pl.BlockSpec((GDN_PAIRS, LANES, LANES), lambda i: (0, 0, 0))],
        out_shape=[jax.ShapeDtypeStruct((l, w), f32), jax.ShapeDtypeStruct((GDN_PAIRS, LANES, LANES), f32)],
        scratch_shapes=[pltpu.VMEM((GDN_PAIRS, LANES, LANES), f32)],
        compiler_params=pltpu.CompilerParams(dimension_semantics=("arbitrary",), vmem_limit_bytes=VMEM_LIMIT_BYTES),
        name="gdn_chunk",
    )(q, k, v, g, b, z, nw, lt, eh)
    s4 = s_bd.reshape(GDN_PAIRS, 2, HEAD_DIM, 2, HEAD_DIM)
    s_fin = jnp.stack([s4[:, 0, :, 0], s4[:, 1, :, 1]], axis=1).reshape(GDN_HEADS, HEAD_DIM, HEAD_DIM)
    return o, s_fin


def _split_cols(h, widths):
    parts, start = [], 0
    for w in widths:
        parts.append(h[..., start:start + w])
        start += w
    return parts


def _even_widths():
    return (NSA_Q_W,) + (NSA_KV_W,) * 6 + (3 * NSA_HEADS, 3 * GDN_W, GDN_HEADS, GDN_HEADS, GDN_W)


def _rms_norm(x, w):
    return x * lax.rsqrt(jnp.mean(jnp.square(x), axis=-1, keepdims=True) + NORM_EPS) * w


def _l2_norm(x):
    return x * lax.rsqrt(jnp.sum(jnp.square(x), axis=-1, keepdims=True) + NORM_EPS)


def _rope(x, pos):
    half = HEAD_DIM // 2
    inv_freq = ROPE_THETA ** (-2.0 * jnp.arange(half, dtype=jnp.float32) / HEAD_DIM)
    ang = pos.astype(jnp.float32)[:, None] * inv_freq[None, :]
    cos, sin = jnp.cos(ang)[:, None, :], jnp.sin(ang)[:, None, :]
    xf = x.astype(jnp.float32)
    x1, x2 = xf[..., :half], xf[..., half:]
    return jnp.concatenate([x1 * cos - x2 * sin, x2 * cos + x1 * sin], axis=-1)


def _causal_dwconv(hist, u, w):
    width, s = w.shape[0], u.shape[1]
    ext = jnp.concatenate([hist.astype(u.dtype), u], axis=1)
    out = w[0] * ext[:, :s]
    for j in range(1, width):
        out = out + w[j] * ext[:, j:j + s]
    return out, ext[:, s:]


def _masked_softmax(s, mask):
    s = jnp.where(mask, s, -jnp.inf)
    m = jnp.max(s, axis=-1, keepdims=True)
    m = jnp.where(jnp.isfinite(m), m, 0.0)
    e = jnp.where(mask, jnp.exp(s - m), 0.0)
    den = jnp.sum(e, axis=-1, keepdims=True)
    return e / jnp.where(den > 0.0, den, 1.0)


def _gather_pages(pool, page_table, layer_idx):
    rows = pool[page_table, layer_idx]
    return rows.reshape(rows.shape[0], -1, *rows.shape[3:])


def _nsa_compress(rows, w1, b1, w2, pe):
    b, l, g, dh = rows.shape
    nc = l // CMP_BLOCK
    blk = rows[:, :nc * CMP_BLOCK].astype(jnp.float32).reshape(b, nc, CMP_BLOCK, g, dh) + pe[:, None, :]
    flat = blk.transpose(0, 1, 3, 2, 4).reshape(b, nc, g, CMP_BLOCK * dh)
    return jax.nn.gelu(flat @ w1 + b1) @ w2


def _nsa_compressed_kv(k_rows, v_rows, cw1, cb1, cw2, cpe):
    ck = _nsa_compress(k_rows, cw1[0], cb1[0], cw2[0], cpe[0])
    cv = _nsa_compress(v_rows, cw1[1], cb1[1], cw2[1], cpe[1])
    nc = ck.shape[1]
    ck = _rope(ck, (jnp.arange(nc) + 1) * CMP_BLOCK - 1)
    return ck, cv


def _nsa_attend(q, qpos, ck, cv, sk, sv, wk, wv, wpos):
    b, nq = q.shape[:2]
    scale = HEAD_DIM ** -0.5
    nc, ns = ck.shape[1], sk.shape[2]
    cend = (jnp.arange(nc) + 1) * CMP_BLOCK - 1
    s = jnp.einsum('bqghd,bcgd->bghqc', q, ck) * scale
    p_cmp = _masked_softmax(s, cend[None, :] <= qpos[:, None])
    o_cmp = jnp.einsum('bghqc,bcgd->bqghd', p_cmp, cv)
    imp = jnp.sum(p_cmp, axis=2)
    imp = jnp.pad(imp, ((0, 0), (0, 0), (0, 0), (0, ns * CMP_PER_SEL - nc)))
    imp = imp.reshape(b, NSA_KV_HEADS, nq, ns, CMP_PER_SEL).sum(-1)
    blk = jnp.arange(ns)[None, :]
    cur = (qpos // SEL_BLOCK)[:, None]
    forced = (blk == 0) | (blk == cur) | (blk == cur - 1)
    imp = jnp.where(blk <= cur, jnp.where(forced, NSA_FORCE, imp), -1.0)
    _, idx = lax.top_k(imp, min(NSA_TOPN, ns))
    n = idx.shape[-1]
    pick = jax.vmap(jax.vmap(lambda kb, ix: kb[ix]))
    ksel = pick(sk, idx).reshape(b, NSA_KV_HEADS, nq, n * SEL_BLOCK, HEAD_DIM)
    vsel = pick(sv, idx).reshape(b, NSA_KV_HEADS, nq, n * SEL_BLOCK, HEAD_DIM)
    kpos = (idx[..., None] * SEL_BLOCK + jnp.arange(SEL_BLOCK)).reshape(b, NSA_KV_HEADS, nq, n * SEL_BLOCK)
    s = jnp.einsum('bqghd,bgqkd->bghqk', q, ksel) * scale
    p = _masked_softmax(s, (kpos <= qpos[:, None])[:, :, None])
    o_slc = jnp.einsum('bghqk,bgqkd->bqghd', p, vsel)
    dist = qpos[:, None] - wpos[None, :]
    wmask = (dist >= 0) & (dist <= NSA_WINDOW) & (wpos[None, :] >= 0)
    s = jnp.einsum('bqghd,bkgd->bghqk', q, wk) * scale
    p = _masked_softmax(s, wmask)
    o_win = jnp.einsum('bghqk,bkgd->bqghd', p, wv)
    return o_cmp, o_slc, o_win


def _nsa_prompt(q, kc, vc, ks, vs, kw, vw, small, cw1, cb1, cw2, cpe):
    b, l = q.shape[:2]
    pos = jnp.arange(l)
    qr = _rope(q, pos)
    ck, cv = _nsa_compressed_kv(kc, vc, cw1, cb1, cw2, cpe)
    ksr = _rope(ks, pos)
    vsf = vs.astype(jnp.float32)
    kwr = _rope(kw, pos)
    vwf = vw.astype(jnp.float32)
    flat = lambda t: t.reshape(t.shape[1], -1)
    o_nsa = _nsa_prompt_attention(flat(qr), small, flat(ck), flat(cv), flat(ksr), flat(vsf), flat(kwr), flat(vwf))
    keep = min(NSA_WINDOW, l)
    rows_cmp = jnp.stack([kc, vc], axis=2)
    rows_slc = jnp.stack([ksr, vsf], axis=2)
    rows_win = jnp.stack([kwr[:, l - keep:], vwf[:, l - keep:]], axis=2)
    return o_nsa[None], rows_cmp, rows_slc, rows_win


def _nsa_sample(q, kc, vc, ks, vs, kw, vw, gate, cmp_pool, slc_pool, layer_idx, win_buf, page_table,
                cw1, cb1, cw2, cpe):
    db, s = q.shape[:2]
    past = page_table.shape[1] * PAGE_SIZE
    wb = win_buf.shape[1]
    assert s == 1 and wb == NSA_WINDOW and past >= wb and past % (STEP_PAGES * PAGE_SIZE) == 0
    qpos = past + jnp.arange(s)
    qr = _rope(q, qpos)
    ckt, cvt = _nsa_sample_compress_t(cmp_pool, layer_idx, page_table, cw1, cb1, cw2, cpe)
    ksr = _rope(ks, qpos)
    vsf = vs.astype(jnp.float32)
    kwr = _rope(kw, qpos)
    vwf = vw.astype(jnp.float32)
    newrows = jnp.stack([t.reshape(db, NSA_KV_W) for t in (ksr, vsf, kwr, vwf)], axis=1)
    o_nsa, rows_win = _nsa_sample_attention_t(qr[:, 0], ckt, cvt, slc_pool, layer_idx, page_table, win_buf,
                                              newrows, gate.reshape(db, -1))
    rows_cmp = jnp.stack([kc, vc], axis=2)
    rows_slc = jnp.stack([ksr, vsf], axis=2)
    return o_nsa[:, None], rows_cmp, rows_slc, rows_win


def _gdn_chunked(q, k, v, g, beta, s0):
    b, l, h, dk = q.shape
    dv = v.shape[-1]
    c = GDN_CHUNK
    nch = l // c
    r = lambda a: jnp.moveaxis(a.reshape(b, nch, c, h, *a.shape[3:]), 3, 2)
    q, k, v, g, beta = r(q), r(k), r(v), r(g), r(beta)
    gc = jnp.cumsum(g, axis=-1)
    ii = jnp.arange(c)
    tri = ii[:, None] >= ii[None, :]
    strict = ii[:, None] > ii[None, :]
    diff = gc[..., :, None] - gc[..., None, :]
    gamma = jnp.where(tri, jnp.exp(jnp.where(tri, diff, 0.0)), 0.0)
    kb = k * beta[..., None]
    a_mat = jnp.where(strict, jnp.einsum('bnhik,bnhjk->bnhij', kb, k) * gamma, 0.0)
    eye = jnp.eye(c, dtype=jnp.float32)
    t_inv = lax.linalg.triangular_solve(eye + a_mat, jnp.broadcast_to(eye, a_mat.shape),
                                        left_side=True, lower=True, unit_diagonal=True)
    u = t_inv @ (v * beta[..., None])
    w = t_inv @ (kb * jnp.exp(gc)[..., None])
    qk = jnp.where(tri, jnp.einsum('bnhik,bnhjk->bnhij', q, k) * gamma, 0.0)
    qg = q * jnp.exp(gc)[..., None]
    kd = k * jnp.exp(gc[..., -1:] - gc)[..., None]
    glast = jnp.exp(gc[..., -1])

    def step(state, xs):
        qg_c, kd_c, u_c, w_c, qk_c, gl_c = xs
        v_new = u_c - jnp.einsum('bhck,bhkv->bhcv', w_c, state)
        o = jnp.einsum('bhck,bhkv->bhcv', qg_c, state) + jnp.einsum('bhij,bhjv->bhiv', qk_c, v_new)
        state = state * gl_c[..., None, None] + jnp.einsum('bhck,bhcv->bhkv', kd_c, v_new)
        return state, o

    xs = tuple(jnp.moveaxis(a, 1, 0) for a in (qg, kd, u, w, qk, glast))
    s_fin, o = lax.scan(step, s0, xs)
    o = jnp.moveaxis(jnp.moveaxis(o, 0, 1), 2, 3).reshape(b, l, h, dv)
    return o, s_fin


def _gdn_recurrent(q, k, v, g, beta, s0):
    def step(state, xs):
        q_t, k_t, v_t, g_t, b_t = xs
        state = state * jnp.exp(g_t)[..., None, None]
        v_t = (v_t - jnp.einsum('bhk,bhkv->bhv', k_t, state)) * b_t[..., None]
        state = state + jnp.einsum('bhk,bhv->bhkv', k_t, v_t)
        return state, jnp.einsum('bhk,bhkv->bhv', q_t, state)

    xs = tuple(jnp.moveaxis(a, 1, 0) for a in (q, k, v, g, beta))
    s_fin, o = lax.scan(step, s0, xs)
    return jnp.moveaxis(o, 0, 1), s_fin


def _gdn_mix(qkv, a, bt, z, conv_hist, s0, conv_w, a_log, dt_bias, norm_w, chunked):
    b, s = qkv.shape[:2]
    c, new_hist = _causal_dwconv(conv_hist, qkv, conv_w)
    c = jax.nn.silu(c.astype(jnp.float32))
    q, k, v = [t.reshape(b, s, GDN_HEADS, HEAD_DIM) for t in jnp.split(c, 3, axis=-1)]
    q = _l2_norm(q) * HEAD_DIM ** -0.5
    k = _l2_norm(k)
    beta = jax.nn.sigmoid(bt.astype(jnp.float32))
    g = -jnp.exp(a_log) * jax.nn.softplus(a.astype(jnp.float32) + dt_bias)
    s0 = s0.astype(jnp.float32)
    if chunked:
        o, s_fin = _gdn_chunked(q, k, v, g, beta, s0)
    else:
        o, s_fin = _gdn_recurrent(q, k, v, g, beta, s0)
    o = _rms_norm(o, norm_w) * jax.nn.silu(z.astype(jnp.float32).reshape(b, s, GDN_HEADS, HEAD_DIM))
    return o.reshape(b, s, GDN_W), new_hist, s_fin


def _even_merge(o_cmp, o_slc, o_win, gate, o_gdn):
    b, s = gate.shape[:2]
    gt = jax.nn.sigmoid(gate.astype(jnp.float32)).reshape(b, s, 3, NSA_HEADS, 1)
    o_nsa = gt[:, :, 0] * o_cmp + gt[:, :, 1] * o_slc + gt[:, :, 2] * o_win
    return jnp.concatenate([o_nsa.reshape(b, s, NSA_Q_W), o_gdn], axis=-1)


def _proj(x, w):
    b, s, d = x.shape
    n = w.shape[1]
    npad = -(-n // LANES) * LANES
    wp = jnp.pad(w, ((0, 0), (0, npad - n)))
    return _matmul(x.reshape(b * s, d), wp)[:, :n].reshape(b, s, n)


def _even_prompt(x, w_in, cw1, cb1, cw2, cpe, conv_w, a_log, dt_bias, norm_w):
    b, l, _ = x.shape
    q, kc, vc, ks, vs, kw, vw, gate, qkv, a, bt, z = _split_cols(_proj(x, w_in), _even_widths())
    heads = lambda t: t.reshape(b, l, -1, HEAD_DIM)
    assert b == 1
    small = jnp.concatenate([gate, a, bt], axis=-1).reshape(l, -1)
    small = jnp.pad(small, ((0, 0), (0, LANES - small.shape[-1])))
    o_nsa, r_cmp, r_slc, r_win = _nsa_prompt(
        heads(q), heads(kc), heads(vc), heads(ks), heads(vs), heads(kw), heads(vw), small, cw1, cb1, cw2, cpe)
    o_gdn, s_fin = _gdn_prompt(qkv[0], small, z[0], conv_w, a_log, dt_bias, norm_w)
    conv_hist = qkv[:, l - (GDN_CONV - 1):]
    return jnp.concatenate([o_nsa, o_gdn[None]], axis=-1), r_cmp, r_slc, r_win, conv_hist, s_fin[None]


def _even_sample(x, cmp_pool, slc_pool, layer_idx, win_buf, conv_hist, s0, page_table,
                 w_in, cw1, cb1, cw2, cpe, conv_w, a_log, dt_bias, norm_w):
    b, s, _ = x.shape
    q, kc, vc, ks, vs, kw, vw, gate, qkv, a, bt, z = _split_cols(_proj(x, w_in), _even_widths())
    heads = lambda t: t.reshape(b, s, -1, HEAD_DIM)
    o_nsa, r_cmp, r_slc, r_win = _nsa_sample(
        heads(q), heads(kc), heads(vc), heads(ks), heads(vs), heads(kw), heads(vw), gate,
        cmp_pool, slc_pool, layer_idx, win_buf, page_table, cw1, cb1, cw2, cpe)
    o_gdn, new_hist, s_fin = _gdn_mix(qkv, a, bt, z, conv_hist, s0, conv_w, a_log, dt_bias, norm_w, False)
    return jnp.concatenate([o_nsa, o_gdn], axis=-1), r_cmp, r_slc, r_win, new_hist, s_fin


def _dilated_band_stats(q, k, v, d):
    b, l, h, dh = q.shape
    unit = d * DIL_BLOCK
    lp = -(-l // unit) * unit
    nb = lp // unit
    to_sub = lambda a: jnp.pad(a, ((0, 0), (0, lp - l), (0, 0), (0, 0))).reshape(b, nb, DIL_BLOCK, d, h, dh)
    qs, ks, vs = to_sub(q), to_sub(k), to_sub(v)
    prev = lambda a: jnp.concatenate([jnp.zeros_like(a[:, :1]), a[:, :-1]], axis=1)
    kk = jnp.concatenate([prev(ks), ks], axis=2)
    vv = jnp.concatenate([prev(vs), vs], axis=2)
    s = jnp.einsum('bnirhd,bnjrhd->bnrhij', qs, kk) * HEAD_DIM ** -0.5
    i = jnp.arange(DIL_BLOCK)
    j = jnp.arange(2 * DIL_BLOCK) - DIL_BLOCK
    dist = i[:, None] - j[None, :]
    sub_k = jnp.arange(nb)[:, None, None] * DIL_BLOCK + j[None, None, :]
    mask = (dist >= 0)[None] & (dist <= DIL_SPAN)[None] & (sub_k >= 0)
    s = jnp.where(mask[None, :, None, None], s, -jnp.inf)
    m = jnp.max(s, axis=-1)
    e = jnp.exp(s - m[..., None])
    den = jnp.sum(e, axis=-1)
    num = jnp.einsum('bnrhij,bnjrhd->bnrhid', e, vv)
    m = jnp.transpose(m, (0, 1, 4, 2, 3)).reshape(b, lp, h)[:, :l]
    den = jnp.transpose(den, (0, 1, 4, 2, 3)).reshape(b, lp, h)[:, :l]
    num = jnp.transpose(num, (0, 1, 4, 2, 3, 5)).reshape(b, lp, h, dh)[:, :l]
    return m, den, num


def _dilated_gather_stats(q, k_all, v_all, qpos, kpos0, d):
    kp = qpos[:, None] - jnp.arange(DIL_SPAN + 1)[None, :] * d
    idx = kp - kpos0
    valid = idx >= 0
    idxc = jnp.clip(idx, 0)
    kg, vg = k_all[:, idxc], v_all[:, idxc]
    s = jnp.einsum('bshd,bsmhd->bshm', q, kg) * HEAD_DIM ** -0.5
    s = jnp.where(valid[None, :, None, :], s, -jnp.inf)
    m = jnp.max(s, axis=-1)
    e = jnp.exp(s - m[..., None])
    return m, jnp.sum(e, axis=-1), jnp.einsum('bshm,bsmhd->bshd', e, vg)


def _combine_by_denominators(stats):
    m_all = stats[0][0]
    for m, _, _ in stats[1:]:
        m_all = jnp.maximum(m_all, m)
    num, den = None, None
    for m, dn, nm in stats:
        w = jnp.exp(m - m_all)
        num = w[..., None] * nm if num is None else num + w[..., None] * nm
        den = w * dn if den is None else den + w * dn
    return num / den[..., None]


def _dil_prompt(x, w_in):
    b, l, _ = x.shape
    q, k, v = [t.reshape(b, l, DIL_HEADS, HEAD_DIM) for t in jnp.split(_proj(x, w_in), 3, axis=-1)]
    pos = jnp.arange(l)
    qr, kr, vf = _rope(q, pos), _rope(k, pos), v.astype(jnp.float32)
    flat = lambda t: t.reshape(l, DIL_W)
    o = _combine_by_denominators([_dil_band_stats(flat(qr), flat(kr), flat(vf), d) for _, d in DIL_GROUPS])
    keep = min(DIL_MAX_WINDOW, l)
    buf = jnp.stack([kr[:, l - keep:], vf[:, l - keep:]], axis=2)
    return o.reshape(b, l, DIL_W), buf


def _dil_sample(x, buf, past, w_in):
    db, s, _ = x.shape
    q, k, v = [t.reshape(db, s, DIL_HEADS, HEAD_DIM) for t in jnp.split(_proj(x, w_in), 3, axis=-1)]
    assert s == 1 and buf.shape[1] == DIL_MAX_WINDOW <= past
    qpos = past + jnp.arange(s)
    qr, kr = _rope(q, qpos), _rope(k, qpos)
    o, new_buf = _dil_sample_attention(qr[:, 0], kr.reshape(db, DIL_W), v.reshape(db, DIL_W).astype(jnp.float32), buf)
    return o[:, None], new_buf


def kernel(x_prompt, x_sample, cache_nsa_cmp_kv, cache_nsa_slc_kv, state_nsa_win_kv, state_gdn_conv,
           state_gdn_S, state_dil_kv, state_ffn_conv, page_table, w_in_a, nsa_cmp_w1, nsa_cmp_b1, nsa_cmp_w2,
           nsa_cmp_pe, gdn_conv_w, gdn_A_log, gdn_dt_bias, gdn_norm_w, w_out_a, w_in_c, w_out_c,
           ln_mix_g, ln_mix_b, ffn_w_in, ffn_conv_w, ffn_conv_b, ffn_w_out, ln_ffn_g, ln_ffn_b):
    past = page_table.shape[1] * PAGE_SIZE
    bp, lp, d = x_prompt.shape
    bs, ls, _ = x_sample.shape
    assert bp == 1 and ls == 1
    xp, xs = x_prompt, x_sample
    cmp_p, cmp_s, slc_p, slc_s, win_p, win_s = [], [], [], [], [], []
    gconv_p, gconv_s, gstate_p, gstate_s = [], [], [], []
    dil_p, dil_s, ffn_p, ffn_s = [], [], [], []
    for layer in range(DEPTH):
        if layer % 2 == 0:
            la = layer // 2
            wa = (w_in_a[la], nsa_cmp_w1[la], nsa_cmp_b1[la], nsa_cmp_w2[la], nsa_cmp_pe[la],
                  gdn_conv_w[la], gdn_A_log[la], gdn_dt_bias[la], gdn_norm_w[la])
            mp, rc, rs, rw, hc, hs_ = _even_prompt(xp, *wa)
            cmp_p.append(rc); slc_p.append(rs); win_p.append(rw); gconv_p.append(hc); gstate_p.append(hs_)
            ms, rc, rs, rw, hc, hs_ = _even_sample(xs, cache_nsa_cmp_kv, cache_nsa_slc_kv, la,
                                                   state_nsa_win_kv[:, la], state_gdn_conv[:, la],
                                                   state_gdn_S[:, la], page_table, *wa)
            cmp_s.append(rc); slc_s.append(rs); win_s.append(rw); gconv_s.append(hc); gstate_s.append(hs_)
            w_out = w_out_a[la]
        else:
            lc = layer // 2
            mp, bpf = _dil_prompt(xp, w_in_c[lc])
            ms, bsf = _dil_sample(xs, state_dil_kv[:, lc], past, w_in_c[lc])
            dil_p.append(bpf); dil_s.append(bsf)
            w_out = w_out_c[lc]
        xp2 = _matmul_ln(mp.reshape(lp, -1), w_out, xp.reshape(lp, d), ln_mix_g[layer], ln_mix_b[layer])
        xs2 = _matmul_ln(ms.reshape(bs, -1), w_out, xs.reshape(bs, d), ln_mix_g[layer], ln_mix_b[layer])
        fargs = (ffn_w_in[layer], ffn_conv_w[layer], ffn_conv_b[layer], ffn_w_out[layer],
                 ln_ffn_g[layer], ln_ffn_b[layer])
        xp3, hp = _ffn_seq(xp2, *fargs)
        xs3, hs = _ffn_step(xs2, state_ffn_conv[:, layer], *fargs)
        xp, xs = xp3.reshape(1, lp, d), xs3.reshape(bs, 1, d)
        ffn_p.append(hp[None]); ffn_s.append(hs)

    def stk(lst):
        return jnp.stack(lst, axis=1)

    return (xp, xs, stk(cmp_p), stk(cmp_s), stk(slc_p), stk(slc_s), stk(win_p), stk(win_s),
            stk(gconv_p), stk(gconv_s), stk(gstate_p), stk(gstate_s), stk(dil_p), stk(dil_s),
            stk(ffn_p), stk(ffn_s))
```

```python
import functools
import math

import jax
import jax.numpy as jnp
from jax import lax
from jax.experimental import pallas as pl
from jax.experimental.pallas import tpu as pltpu
import numpy as np

D_MODEL = 1024
DEPTH = 2
PAGE_SIZE = 128
HEAD_DIM = 64
ROPE_THETA = 10000.0
NSA_HEADS = 8
NSA_KV_HEADS = 2
NSA_GROUP = NSA_HEADS // NSA_KV_HEADS
CMP_BLOCK = 32
SEL_BLOCK = 64
CMP_PER_SEL = SEL_BLOCK // CMP_BLOCK
NSA_TOPN = 16
NSA_WINDOW = 512
NSA_QBLOCK = 128
NSA_FORCE = 1.0e4
GDN_HEADS = 8
GDN_CONV = 4
GDN_CHUNK = 64
DIL_HEADS = 16
DIL_GROUPS = ((128, 1), (512, 4), (2048, 16))
DIL_SPAN = 128
DIL_BLOCK = 128
DIL_MAX_WINDOW = 2048
D_FF = 2816
FFN_CONV = 3
DEEPNORM_ALPHA = (2.0 * DEPTH) ** 0.25
LN_EPS = 1e-5
NORM_EPS = 1e-6
NSA_Q_W = NSA_HEADS * HEAD_DIM
NSA_KV_W = NSA_KV_HEADS * HEAD_DIM
GDN_W = GDN_HEADS * HEAD_DIM
DIL_W = DIL_HEADS * HEAD_DIM

LANES = 128
VMEM_LIMIT_BYTES = 56 * 1024 * 1024


def _layer_norm_rows(r, g, b):
    mu = jnp.mean(r, axis=-1, keepdims=True)
    d = r - mu
    var = jnp.mean(d * d, axis=-1, keepdims=True)
    return d * lax.rsqrt(var + LN_EPS) * g + b


def _mm_kernel(x_ref, w_ref, o_ref):
    o_ref[...] = jnp.dot(x_ref[...].astype(jnp.bfloat16), w_ref[...], preferred_element_type=jnp.float32)


def _mm_ln_kernel(x_ref, w_ref, res_ref, g_ref, b_ref, o_ref):
    acc = jnp.dot(x_ref[...].astype(jnp.bfloat16), w_ref[...], preferred_element_type=jnp.float32)
    o_ref[...] = _layer_norm_rows(DEEPNORM_ALPHA * res_ref[...] + acc, g_ref[...], b_ref[...])


def _row_tile(m):
    return 512 if m % 512 == 0 else m


def _matmul(x, w):
    m, k = x.shape
    n = w.shape[1]
    tm = _row_tile(m)
    tn = n
    for cand in (1152, 1024, 768, 512):
        if n % cand == 0:
            tn = cand
            break
    return pl.pallas_call(
        _mm_kernel,
        grid=(m // tm, n // tn),
        in_specs=[pl.BlockSpec((tm, k), lambda i, j: (i, 0)),
                  pl.BlockSpec((k, tn), lambda i, j: (0, j))],
        out_specs=pl.BlockSpec((tm, tn), lambda i, j: (i, j)),
        out_shape=jax.ShapeDtypeStruct((m, n), jnp.float32),
        compiler_params=pltpu.CompilerParams(dimension_semantics=("parallel", "arbitrary"),
                                             vmem_limit_bytes=VMEM_LIMIT_BYTES),
        name="matmul",
    )(x, w.astype(jnp.bfloat16))


def _matmul_ln(x, w, res, g, b):
    m, k = x.shape
    n = w.shape[1]
    tm = _row_tile(m)
    return pl.pallas_call(
        _mm_ln_kernel,
        grid=(m // tm,),
        in_specs=[pl.BlockSpec((tm, k), lambda i: (i, 0)),
                  pl.BlockSpec((k, n), lambda i: (0, 0)),
                  pl.BlockSpec((tm, n), lambda i: (i, 0)),
                  pl.BlockSpec((1, n), lambda i: (0, 0)),
                  pl.BlockSpec((1, n), lambda i: (0, 0))],
        out_specs=pl.BlockSpec((tm, n), lambda i: (i, 0)),
        out_shape=jax.ShapeDtypeStruct((m, n), jnp.float32),
        compiler_params=pltpu.CompilerParams(dimension_semantics=("arbitrary",),
                                             vmem_limit_bytes=VMEM_LIMIT_BYTES),
        name="matmul_ln",
    )(x, w.astype(jnp.bfloat16), res, g.reshape(1, n), b.reshape(1, n))


FFN_CHUNK = D_FF // 2
FFN_NCHUNK = D_FF // FFN_CHUNK


def _ffn_seq_kernel(x_ref, wa_ref, wg_ref, cwa_ref, cwg_ref, cba_ref, cbg_ref, wo_ref, lg_ref, lb_ref,
                    y_ref, ha_ref, hg_ref, acc_ref, carry_ref):
    i, j = pl.program_id(0), pl.program_id(1)
    tm = x_ref.shape[0]
    x = x_ref[...]
    xb = x.astype(jnp.bfloat16)

    @pl.when(i == 0)
    def _():
        carry_ref[j] = jnp.zeros(carry_ref.shape[1:], jnp.float32)

    def conv(u, cw_ref, cb_ref, slot):
        prev = carry_ref[j, slot]
        p2, p1 = prev[6:7], prev[7:8]
        row = lax.broadcasted_iota(jnp.int32, u.shape, 0)
        u1 = jnp.where(row == 0, p1, pltpu.roll(u, 1, 0))
        u2 = jnp.where(row == 0, p2, jnp.where(row == 1, p1, pltpu.roll(u, 2, 0)))
        carry_ref[j, slot] = u[tm - 8:]
        cw = cw_ref[...]
        return cw[0:1] * u2 + cw[1:2] * u1 + cw[2:3] * u + cb_ref[...]

    ua = jnp.dot(xb, wa_ref[...], preferred_element_type=jnp.float32)
    ug = jnp.dot(xb, wg_ref[...], preferred_element_type=jnp.float32)
    ha_ref[...] = ua[tm - 8:]
    hg_ref[...] = ug[tm - 8:]
    a = conv(ua, cwa_ref, cba_ref, 0)
    g = conv(ug, cwg_ref, cbg_ref, 1)
    h = (a * jax.nn.sigmoid(a) * g).astype(jnp.bfloat16)
    part = jnp.dot(h, wo_ref[...], preferred_element_type=jnp.float32)

    @pl.when(j == 0)
    def _():
        acc_ref[...] = part

    @pl.when(j > 0)
    def _():
        acc_ref[...] += part

    @pl.when(j == pl.num_programs(1) - 1)
    def _():
        y_ref[...] = _layer_norm_rows(DEEPNORM_ALPHA * x + acc_ref[...], lg_ref[...], lb_ref[...])


def _ffn_seq(x, w_in, conv_w, conv_b, w_out, ln_g, ln_b):
    l, d = x.shape
    tm = 512
    c, nc = FFN_CHUNK, FFN_NCHUNK
    w_in = w_in.astype(jnp.bfloat16)
    cw8 = jnp.zeros((8, 2 * D_FF), jnp.float32).at[:FFN_CONV].set(conv_w)
    cb = conv_b.reshape(1, 2 * D_FF)
    y, ha, hg = pl.pallas_call(
        _ffn_seq_kernel,
        grid=(l // tm, nc),
        in_specs=[pl.BlockSpec((tm, d), lambda i, j: (i, 0)),
                  pl.BlockSpec((d, c), lambda i, j: (0, j)),
                  pl.BlockSpec((d, c), lambda i, j: (0, j + nc)),
                  pl.BlockSpec((8, c), lambda i, j: (0, j)),
                  pl.BlockSpec((8, c), lambda i, j: (0, j + nc)),
                  pl.BlockSpec((1, c), lambda i, j: (0, j)),
                  pl.BlockSpec((1, c), lambda i, j: (0, j + nc)),
                  pl.BlockSpec((c, d), lambda i, j: (j, 0)),
                  pl.BlockSpec((1, d), lambda i, j: (0, 0)),
                  pl.BlockSpec((1, d), lambda i, j: (0, 0))],
        out_specs=[pl.BlockSpec((tm, d), lambda i, j: (i, 0)),
                   pl.BlockSpec((8, c), lambda i, j: (i, j)),
                   pl.BlockSpec((8, c), lambda i, j: (i, j))],
        out_shape=[jax.ShapeDtypeStruct((l, d), jnp.float32),
                   jax.ShapeDtypeStruct((l // tm * 8, D_FF), jnp.float32),
                   jax.ShapeDtypeStruct((l // tm * 8, D_FF), jnp.float32)],
        scratch_shapes=[pltpu.VMEM((tm, d), jnp.float32),
                        pltpu.VMEM((nc, 2, 8, c), jnp.float32)],
        compiler_params=pltpu.CompilerParams(dimension_semantics=("arbitrary", "arbitrary"),
                                             vmem_limit_bytes=VMEM_LIMIT_BYTES),
        name="ffn_seq",
    )(x, w_in, w_in, cw8, cw8, cb, cb, w_out.astype(jnp.bfloat16), ln_g.reshape(1, d), ln_b.reshape(1, d))
    hist = jnp.concatenate([ha[-(FFN_CONV - 1):], hg[-(FFN_CONV - 1):]], axis=-1)
    return y, hist


def _ffn_step_kernel(x_ref, h_ref, wa_ref, wg_ref, cwa_ref, cwg_ref, cba_ref, cbg_ref, wo_ref, lg_ref, lb_ref,
                     y_ref, ua_ref, ug_ref, acc_ref):
    j = pl.program_id(0)
    x = x_ref[...]
    xb = x.astype(jnp.bfloat16)
    ua = jnp.dot(xb, wa_ref[...], preferred_element_type=jnp.float32)
    ug = jnp.dot(xb, wg_ref[...], preferred_element_type=jnp.float32)
    ua_ref[...] = ua
    ug_ref[...] = ug
    cwa, cwg = cwa_ref[...], cwg_ref[...]
    a = cwa[0:1] * h_ref[0, 0] + cwa[1:2] * h_ref[1, 0] + cwa[2:3] * ua + cba_ref[...]
    g = cwg[0:1] * h_ref[0, 1] + cwg[1:2] * h_ref[1, 1] + cwg[2:3] * ug + cbg_ref[...]
    h = (a * jax.nn.sigmoid(a) * g).astype(jnp.bfloat16)
    part = jnp.dot(h, wo_ref[...], preferred_element_type=jnp.float32)

    @pl.when(j == 0)
    def _():
        acc_ref[...] = part

    @pl.when(j > 0)
    def _():
        acc_ref[...] += part

    @pl.when(j == pl.num_programs(0) - 1)
    def _():
        y_ref[...] = _layer_norm_rows(DEEPNORM_ALPHA * x + acc_ref[...], lg_ref[...], lb_ref[...])


def _ffn_step(x, hist, w_in, conv_w, conv_b, w_out, ln_g, ln_b):
    b, d = x.shape
    c, nc = FFN_CHUNK, FFN_NCHUNK
    w_in = w_in.astype(jnp.bfloat16)
    cw8 = jnp.zeros((8, 2 * D_FF), jnp.float32).at[:FFN_CONV].set(conv_w)
    cb = conv_b.reshape(1, 2 * D_FF)
    h4 = jnp.transpose(hist, (1, 0, 2)).reshape(2, b, 2, D_FF).transpose(0, 2, 1, 3)
    y, ua, ug = pl.pallas_call(
        _ffn_step_kernel,
        grid=(nc,),
        in_specs=[pl.BlockSpec((b, d), lambda j: (0, 0)),
                  pl.BlockSpec((2, 2, b, c), lambda j: (0, 0, 0, j)),
                  pl.BlockSpec((d, c), lambda j: (0, j)),
                  pl.BlockSpec((d, c), lambda j: (0, j + nc)),
                  pl.BlockSpec((8, c), lambda j: (0, j)),
                  pl.BlockSpec((8, c), lambda j: (0, j + nc)),
                  pl.BlockSpec((1, c), lambda j: (0, j)),
                  pl.BlockSpec((1, c), lambda j: (0, j + nc)),
                  pl.BlockSpec((c, d), lambda j: (j, 0)),
                  pl.BlockSpec((1, d), lambda j: (0, 0)),
                  pl.BlockSpec((1, d), lambda j: (0, 0))],
        out_specs=[pl.BlockSpec((b, d), lambda j: (0, 0)),
                   pl.BlockSpec((b, c), lambda j: (0, j)),
                   pl.BlockSpec((b, c), lambda j: (0, j))],
        out_shape=[jax.ShapeDtypeStruct((b, d), jnp.float32),
                   jax.ShapeDtypeStruct((b, D_FF), jnp.float32),
                   jax.ShapeDtypeStruct((b, D_FF), jnp.float32)],
        scratch_shapes=[pltpu.VMEM((b, d), jnp.float32)],
        compiler_params=pltpu.CompilerParams(dimension_semantics=("arbitrary",),
                                             vmem_limit_bytes=VMEM_LIMIT_BYTES),
        name="ffn_step",
    )(x, h4, w_in, w_in, cw8, cw8, cb, cb, w_out.astype(jnp.bfloat16), ln_g.reshape(1, d), ln_b.reshape(1, d))
    u = jnp.concatenate([ua, ug], axis=-1)
    return y, jnp.concatenate([hist[:, 1:], u[:, None]], axis=1)


NEG = -1e30
NSA_KT = 512
NSA_COLS = NSA_HEADS * NSA_QBLOCK
NSA_WSPAN = NSA_WINDOW + NSA_QBLOCK
LOG2E = 1.4426950408889634
NSA_VROWS = HEAD_DIM + 8


def _lane_tile(x, n):
    return jnp.concatenate([x] * n, axis=1)


def _nsa_prompt_kernel(q_ref, sm_ref, ck_ref, cvt_ref, ks_ref, vst_ref, kw_ref, vwt_ref, hot_ref, o_ref,
                       selb_ref, m_ref, acc_ref, *, ns):
    f32, bf16 = jnp.float32, jnp.bfloat16
    qb = NSA_QBLOCK
    i = pl.program_id(0)
    s0 = i * qb
    half = NSA_COLS // 2

    qt = (q_ref[...] * (HEAD_DIM ** -0.5 * LOG2E)).T
    zero = jnp.zeros((HEAD_DIM, qb), f32)
    top = jnp.concatenate([qt[h * HEAD_DIM:(h + 1) * HEAD_DIM] for h in range(NSA_GROUP)] + [zero] * NSA_GROUP, axis=1)
    bot = jnp.concatenate([zero] * NSA_GROUP + [qt[h * HEAD_DIM:(h + 1) * HEAD_DIM]
                                                for h in range(NSA_GROUP, NSA_HEADS)], axis=1)
    qbd = jnp.concatenate([top, bot], axis=0).astype(bf16)

    def pv(vt, p):
        pb = p.astype(bf16)
        rows = vt.shape[0] // NSA_KV_HEADS
        return [jnp.dot(vt[g * rows:(g + 1) * rows], pb[:, g * half:(g + 1) * half],
                        preferred_element_type=f32) for g in range(NSA_KV_HEADS)]

    nc = 2 * ns
    r = lax.broadcasted_iota(jnp.int32, (nc, qb), 0)
    lane = lax.broadcasted_iota(jnp.int32, (nc, qb), 1)
    cidx = jnp.where(r < ns, 2 * r, 2 * (r - ns) + 1)
    cbias = jnp.where((cidx + 1) * CMP_BLOCK - 1 <= s0 + lane, 0.0, NEG)
    sc = jnp.dot(ck_ref[...], qbd, preferred_element_type=f32) + _lane_tile(cbias, NSA_HEADS)
    m = jnp.max(sc, axis=0, keepdims=True)
    p = jnp.exp2(sc - m)
    pn = p * jnp.where(m > 0.5 * NEG, 1.0 / jnp.sum(p, axis=0, keepdims=True), 0.0)
    o_cmp = pv(cvt_ref[...], pn)

    blk = lax.broadcasted_iota(jnp.int32, (ns, qb), 0)
    qpos = s0 + lax.broadcasted_iota(jnp.int32, (ns, qb), 1)
    cur = qpos // SEL_BLOCK
    forced = (blk == 0) | (blk == cur) | (blk == cur - 1)
    for g in range(NSA_KV_HEADS):
        imp = pn[:, g * half:g * half + qb]
        for h in range(1, NSA_GROUP):
            imp = imp + pn[:, g * half + h * qb:g * half + (h + 1) * qb]
        imp = imp[:ns] + imp[ns:]
        val = jnp.where(blk > cur, -1.0, jnp.where(forced, NSA_FORCE, imp))
        bias = jnp.full((ns, qb), NEG, f32)
        for _ in range(min(NSA_TOPN, ns)):
            top = jnp.max(val, axis=0, keepdims=True)
            pick = jnp.min(jnp.where(val == top, blk, ns), axis=0, keepdims=True)
            hit = blk == pick
            bias = jnp.where(hit, 0.0, bias)
            val = jnp.where(hit, -jnp.inf, val)
        selb_ref[g] = bias

    m_ref[...] = jnp.full(m_ref.shape, NEG, f32)
    acc_ref[...] = jnp.zeros(acc_ref.shape, f32)
    per_tile = NSA_KT // SEL_BLOCK
    zpad = jnp.zeros((LANES - 16, NSA_COLS), bf16)

    def slc_tile(kt, causal):
        k0 = pl.multiple_of(kt * NSA_KT, NSA_KT)
        b0 = pl.multiple_of(kt * per_tile, per_tile)
        brow = jnp.concatenate([selb_ref[g, pl.ds(b0, per_tile), :] for g in range(NSA_KV_HEADS)
                                for _ in range(NSA_GROUP)], axis=1)
        brow = jnp.concatenate([brow, jnp.zeros((16 - per_tile, NSA_COLS), f32)], axis=0).astype(bf16)
        q_aug = jnp.concatenate([qbd, brow, zpad], axis=0)
        k_aug = jnp.concatenate([ks_ref[pl.ds(k0, NSA_KT), :], hot_ref[...]], axis=1)
        s = jnp.dot(k_aug, q_aug, preferred_element_type=f32)
        if causal:
            kpos = k0 + lax.broadcasted_iota(jnp.int32, (NSA_KT, qb), 0)
            qq = s0 + lax.broadcasted_iota(jnp.int32, (NSA_KT, qb), 1)
            s = s + _lane_tile(jnp.where(kpos <= qq, 0.0, NEG), NSA_HEADS)
        m_old = m_ref[...]
        m_new = jnp.maximum(m_old, jnp.max(s, axis=0, keepdims=True))
        alpha = jnp.exp2(m_old - m_new)
        p = jnp.exp2(s - m_new)
        m_ref[...] = m_new
        upd = pv(vst_ref[:, pl.ds(k0, NSA_KT)], p)
        for g in range(NSA_KV_HEADS):
            acc_ref[g] = acc_ref[g] * alpha[:, g * half:(g + 1) * half] + upd[g]

    kd = s0 // NSA_KT

    def body(j, carry):
        slc_tile(2 * j, False)
        slc_tile(2 * j + 1, False)
        return carry

    lax.fori_loop(0, kd // 2, body, 0)

    @pl.when(kd % 2 == 1)
    def _():
        slc_tile(kd - 1, False)

    slc_tile(kd, True)
    inv_slc = [1.0 / acc_ref[g, HEAD_DIM:HEAD_DIM + 1, :] for g in range(NSA_KV_HEADS)]

    w0 = pl.multiple_of(s0, qb)
    sw = jnp.dot(kw_ref[pl.ds(w0, NSA_WSPAN), :], qbd, preferred_element_type=f32)
    rr = lax.broadcasted_iota(jnp.int32, (NSA_WSPAN, qb), 0)
    qi = lax.broadcasted_iota(jnp.int32, (NSA_WSPAN, qb), 1)
    ok = (rr >= qi) & (rr <= qi + NSA_WINDOW) & (rr + s0 >= NSA_WINDOW)
    sw = sw + _lane_tile(jnp.where(ok, 0.0, NEG), NSA_HEADS)
    pw = jnp.exp2(sw - jnp.max(sw, axis=0, keepdims=True))
    o_win = pv(vwt_ref[:, pl.ds(w0, NSA_WSPAN)], pw)
    inv_win = [1.0 / o[HEAD_DIM:HEAD_DIM + 1] for o in o_win]

    gt = jax.nn.sigmoid(sm_ref[...].T)
    outs = []
    for h in range(NSA_HEADS):
        g, hg = divmod(h, NSA_GROUP)
        c0, c1 = hg * qb, (hg + 1) * qb
        g_cmp = gt[h:h + 1]
        g_slc = gt[NSA_HEADS + h:NSA_HEADS + h + 1] * inv_slc[g][:, c0:c1]
        g_win = gt[2 * NSA_HEADS + h:2 * NSA_HEADS + h + 1] * inv_win[g][:, c0:c1]
        outs.append(o_cmp[g][:, c0:c1] * g_cmp + acc_ref[g, :HEAD_DIM, c0:c1] * g_slc
                    + o_win[g][:HEAD_DIM, c0:c1] * g_win)
    o_ref[...] = jnp.concatenate(outs, axis=0).T


def _nsa_prompt_attention(qr, small, ck, cv, ksr, vs, kwr, vw):
    l = qr.shape[0]
    ns = l // SEL_BLOCK
    nc = 2 * ns
    bf16 = jnp.bfloat16
    perm = jnp.concatenate([jnp.arange(0, nc, 2), jnp.arange(1, nc, 2)])
    ckp = ck[perm].astype(bf16)
    cvt = cv[perm].T.astype(bf16)
    pad = jnp.zeros((NSA_WINDOW, NSA_KV_W), bf16)
    kwp = jnp.concatenate([pad, kwr.astype(bf16)], axis=0)
    def with_ones(vt):
        n = vt.shape[1]
        extra = jnp.concatenate([jnp.ones((1, n), bf16), jnp.zeros((NSA_VROWS - HEAD_DIM - 1, n), bf16)], axis=0)
        return jnp.concatenate([x for g in range(NSA_KV_HEADS) for x in (vt[g * HEAD_DIM:(g + 1) * HEAD_DIM], extra)], axis=0)

    vwt = with_ones(jnp.concatenate([pad, vw.astype(bf16)], axis=0).T)
    hot = (jnp.arange(NSA_KT)[:, None] // SEL_BLOCK == jnp.arange(LANES)[None, :]).astype(bf16)
    full = lambda a: pl.BlockSpec(a.shape, lambda i: (0,) * a.ndim)
    args = (qr, small, ckp, cvt, ksr.astype(bf16), with_ones(vs.T.astype(bf16)), kwp, vwt, hot)
    return pl.pallas_call(
        functools.partial(_nsa_prompt_kernel, ns=ns),
        grid=(l // NSA_QBLOCK,),
        in_specs=[pl.BlockSpec((NSA_QBLOCK, NSA_Q_W), lambda i: (i, 0)),
                  pl.BlockSpec((NSA_QBLOCK, LANES), lambda i: (i, 0))] + [full(a) for a in args[2:]],
        out_specs=pl.BlockSpec((NSA_QBLOCK, NSA_Q_W), lambda i: (i, 0)),
        out_shape=jax.ShapeDtypeStruct((l, NSA_Q_W), jnp.float32),
        scratch_shapes=[pltpu.VMEM((NSA_KV_HEADS, ns, NSA_QBLOCK), jnp.float32),
                        pltpu.VMEM((1, NSA_COLS), jnp.float32),
                        pltpu.VMEM((NSA_KV_HEADS, NSA_VROWS, NSA_COLS // 2), jnp.float32)],
        compiler_params=pltpu.CompilerParams(dimension_semantics=("arbitrary",),
                                             vmem_limit_bytes=VMEM_LIMIT_BYTES),
        name="nsa_prompt",
    )(*args)


PAGE_W = 2 * NSA_KV_W
CMP_HIDDEN = 2 * HEAD_DIM
CMP_PER_PAGE = PAGE_SIZE // CMP_BLOCK
CMP_FLAT = CMP_BLOCK * PAGE_W
CMP_PAGES_PER_STEP = 32
STEP_PAGES = 64
STEP_ROWS = 16


def _rope_tables(pos, width):
    half = HEAD_DIM // 2
    inv_freq = ROPE_THETA ** (-2.0 * jnp.arange(half, dtype=jnp.float32) / HEAD_DIM)
    ang = pos.astype(jnp.float32)[:, None] * inv_freq[None, :]
    cos, sin = jnp.cos(ang), jnp.sin(ang)
    reps = width // HEAD_DIM
    return (jnp.tile(jnp.concatenate([cos, cos], axis=1), (1, reps)),
            jnp.tile(jnp.concatenate([-sin, sin], axis=1), (1, reps)))


def _rope_lanes(x, cos, sin_signed):
    n = x.shape[-1]
    lane = lax.broadcasted_iota(jnp.int32, x.shape, x.ndim - 1)
    first = (lane % HEAD_DIM) < HEAD_DIM // 2
    partner = jnp.where(first, pltpu.roll(x, n - HEAD_DIM // 2, x.ndim - 1), pltpu.roll(x, HEAD_DIM // 2, x.ndim - 1))
    return x * cos + partner * sin_signed


def _cmp_step_kernel(pt_ref, *refs):
    npg = CMP_PAGES_PER_STEP
    pages = refs[:npg]
    pe_ref, w1_ref, b1_ref, w2_ref, cos_ref, sin_ref, ck_ref, cv_ref = refs[npg:]
    x = jnp.concatenate([r[0] for r in pages], axis=0) + pe_ref[...]
    h = jnp.dot(x.astype(jnp.bfloat16), w1_ref[...], preferred_element_type=jnp.float32) + b1_ref[...]
    c = jnp.dot(jax.nn.gelu(h).astype(jnp.bfloat16), w2_ref[...], preferred_element_type=jnp.float32)
    ck_ref[0] = _rope_lanes(c[:, :NSA_KV_W], cos_ref[...], sin_ref[...])
    cv_ref[0] = c[:, NSA_KV_W:]


def _compress_weights(cw1, cb1, cw2, cpe):
    eye = jnp.eye(2, dtype=jnp.float32)
    w1r = cw1.reshape(2, CMP_BLOCK, HEAD_DIM, CMP_HIDDEN)
    w1 = jnp.einsum('ktdj,ka,gb->tkgdabj', w1r, eye, eye).reshape(CMP_FLAT, 4 * CMP_HIDDEN)
    b1 = jnp.broadcast_to(cb1[:, None, :], (2, 2, CMP_HIDDEN)).reshape(1, 4 * CMP_HIDDEN)
    w2 = jnp.einsum('kjd,ka,gb->kgjabd', cw2, eye, eye).reshape(4 * CMP_HIDDEN, PAGE_W)
    pe = jnp.broadcast_to(cpe.transpose(1, 0, 2)[:, :, None, :], (CMP_BLOCK, 2, 2, HEAD_DIM)).reshape(1, CMP_FLAT)
    return w1.astype(jnp.bfloat16), b1, w2.astype(jnp.bfloat16), pe


def _nsa_sample_compress(pool, layer_idx, page_table, cw1, cb1, cw2, cpe):
    n_pool, nl = pool.shape[:2]
    db, n_pages = page_table.shape
    npg = CMP_PAGES_PER_STEP
    nchunk = n_pages // npg
    nc = n_pages * CMP_PER_PAGE
    view = pool.reshape(n_pool * nl, CMP_PER_PAGE, CMP_FLAT)
    pt = (page_table * nl + layer_idx).reshape(-1).astype(jnp.int32)
    w1, b1, w2, pe = _compress_weights(cw1, cb1, cw2, cpe)
    cos, sin = _rope_tables((jnp.arange(nc) + 1) * CMP_BLOCK - 1, NSA_KV_W)
    rows = npg * CMP_PER_PAGE

    def page_map(k):
        return lambda b, c, pt_ref: (pt_ref[b * n_pages + c * npg + k], 0, 0)

    const = lambda a: pl.BlockSpec(a.shape, lambda b, c, pt_ref: (0,) * a.ndim, pipeline_mode=pl.Buffered(1))
    grid_spec = pltpu.PrefetchScalarGridSpec(
        num_scalar_prefetch=1, grid=(db, nchunk),
        in_specs=[pl.BlockSpec((1, CMP_PER_PAGE, CMP_FLAT), page_map(k)) for k in range(npg)]
        + [const(pe), const(w1), const(b1), const(w2),
           pl.BlockSpec((rows, NSA_KV_W), lambda b, c, pt_ref: (c, 0)),
           pl.BlockSpec((rows, NSA_KV_W), lambda b, c, pt_ref: (c, 0))],
        out_specs=[pl.BlockSpec((1, rows, NSA_KV_W), lambda b, c, pt_ref: (b, c, 0))] * 2)
    return pl.pallas_call(
        _cmp_step_kernel, grid_spec=grid_spec,
        out_shape=[jax.ShapeDtypeStruct((db, nc, NSA_KV_W), jnp.float32)] * 2,
        compiler_params=pltpu.CompilerParams(dimension_semantics=("arbitrary", "arbitrary"),
                                             vmem_limit_bytes=VMEM_LIMIT_BYTES),
        name="nsa_sample_compress",
    )(pt, *([view] * npg), pe, w1, b1, w2, cos, sin)


def _nt_dot(a, b):
    return lax.dot_general(a, b, (((1,), (1,)), ((), ())), preferred_element_type=jnp.float32)


def _nsa_step_kernel(pt_ref, *refs, cur, nsl):
    f32, bf16 = jnp.float32, jnp.bfloat16
    npg = STEP_PAGES
    q_ref, ck_ref, cv_ref = refs[:3]
    pages = refs[3:3 + npg]
    (win_ref, new_ref, gate_ref, exp_ref, triu_ref, o_ref, wout_ref,
     selt_ref, m_ref, l_ref, acc_ref, side_ref) = refs[3 + npg:]
    cc = pl.program_id(1)
    q16 = q_ref[0]
    qb = q16.astype(bf16)
    row16 = lax.broadcasted_iota(jnp.int32, (STEP_ROWS, 1), 0)

    def new_key_scores(krow):
        return jnp.sum(q16 * krow, axis=1, keepdims=True)

    @pl.when(cc == 0)
    def _():
        nch = ck_ref.shape[1] // 2
        halves = lambda r: jnp.concatenate([r[0, pl.ds(0, nch, stride=2), :], r[0, pl.ds(1, nch, stride=2), :]], axis=0)
        ck, cv = halves(ck_ref), halves(cv_ref)
        s = _nt_dot(qb, ck.astype(bf16))
        p = jnp.exp(s - jnp.max(s, axis=1, keepdims=True))
        pn = p / jnp.sum(p, axis=1, keepdims=True)
        o_cmp = jnp.dot(pn.astype(bf16), cv.astype(bf16), preferred_element_type=f32)

        rowp = lax.broadcasted_iota(jnp.int32, pn.shape, 0)
        row8 = lax.broadcasted_iota(jnp.int32, (8, nsl), 0)
        blk = lax.broadcasted_iota(jnp.int32, (8, nsl), 1)
        forced = (blk == 0) | (blk == cur) | (blk == cur - 1)
        key = jnp.full((8, nsl), -1, jnp.int32)
        for g in range(NSA_KV_HEADS):
            ig = jnp.sum(jnp.where((rowp >= g * NSA_GROUP) & (rowp < (g + 1) * NSA_GROUP), pn, 0.0),
                         axis=0, keepdims=True)
            ig = ig[:, :nch] + ig[:, nch:]
            ig = jnp.concatenate([ig, jnp.zeros((1, nsl - nch), f32)], axis=1)
            kg = jnp.where(blk[:1] > cur, -1,
                           jnp.where(forced[:1], lax.bitcast_convert_type(jnp.full((1, nsl), NSA_FORCE, f32), jnp.int32),
                                     lax.bitcast_convert_type(ig, jnp.int32)))
            key = jnp.where(row8 == g, kg, key)
        thr = jnp.zeros((8, 1), jnp.int32)
        for bit in range(30, -1, -1):
            cand = thr | (1 << bit)
            cnt = jnp.sum(jnp.where(key >= cand, 1.0, 0.0), axis=1, keepdims=True)
            thr = jnp.where(cnt >= NSA_TOPN, cand, thr)
        above = key > thr
        n_above = jnp.sum(jnp.where(above, 1.0, 0.0), axis=1, keepdims=True)
        tie = key == thr
        rank = jnp.dot(jnp.where(tie, 1.0, 0.0).astype(bf16), triu_ref[...], preferred_element_type=f32)
        sel = jnp.where(above | (tie & (rank <= NSA_TOPN - n_above)), 1.0, 0.0)
        selh = jnp.where(row16 < NSA_GROUP, sel[0:1], jnp.where(row16 < NSA_HEADS, sel[1:2], 0.0))
        for j in range(selt_ref.shape[0]):
            selt_ref[j] = selh[:, j * LANES:(j + 1) * LANES]

        m_ref[...] = new_key_scores(new_ref[0, 0:1, :])
        l_ref[...] = jnp.ones(l_ref.shape, f32)
        acc_ref[...] = jnp.broadcast_to(new_ref[0, 1:2, :], acc_ref.shape)

        win = win_ref[0]
        sw = _nt_dot(qb, win[:, :NSA_KV_W].astype(bf16))
        sn = new_key_scores(new_ref[0, 2:3, :])
        mw = jnp.maximum(jnp.max(sw, axis=1, keepdims=True), sn)
        pw, pnw = jnp.exp(sw - mw), jnp.exp(sn - mw)
        lw = jnp.sum(pw, axis=1, keepdims=True) + pnw
        o_win = (jnp.dot(pw.astype(bf16), win[:, NSA_KV_W:].astype(bf16), preferred_element_type=f32)
                 + pnw * new_ref[0, 3:4, :]) / lw
        gt = jax.nn.sigmoid(gate_ref[0])
        side_ref[...] = gt[:, 0:1] * o_cmp + gt[:, 2:3] * o_win
        rw = lax.broadcasted_iota(jnp.int32, win.shape, 0)
        newrow = jnp.concatenate([new_ref[0, 2:3, :], new_ref[0, 3:4, :]], axis=1)
        wout_ref[0] = jnp.where(rw == win.shape[0] - 1, newrow, pltpu.roll(win, win.shape[0] - 1, 0))

    kv = jnp.concatenate([r[0] for r in pages], axis=0)
    s = _nt_dot(qb, kv[:, :NSA_KV_W].astype(bf16))
    picked = jnp.dot(selt_ref[cc].astype(bf16), exp_ref[...], preferred_element_type=f32)
    s = s + (picked - 1.0) * (-NEG)
    m_old = m_ref[...]
    m_new = jnp.maximum(m_old, jnp.max(s, axis=1, keepdims=True))
    alpha = jnp.exp(m_old - m_new)
    p = jnp.exp(s - m_new)
    m_ref[...] = m_new
    l_ref[...] = l_ref[...] * alpha + jnp.sum(p, axis=1, keepdims=True)
    acc_ref[...] = acc_ref[...] * alpha + jnp.dot(p.astype(bf16), kv[:, NSA_KV_W:].astype(bf16),
                                                  preferred_element_type=f32)

    @pl.when(cc == pl.num_programs(1) - 1)
    def _():
        gt = jax.nn.sigmoid(gate_ref[0])
        o_ref[0] = side_ref[...] + gt[:, 1:2] * acc_ref[...] / l_ref[...]


def _nsa_sample_attention(qr, ck, cv, slc_pool, layer_idx, page_table, win_buf, newrows, gate):
    db, n_pages = page_table.shape
    n_pool, nl = slc_pool.shape[:2]
    past = n_pages * PAGE_SIZE
    cur = past // SEL_BLOCK
    nsl = -(-(cur + 1) // LANES) * LANES
    npg = STEP_PAGES
    nchunk = n_pages // npg
    keys = npg * PAGE_SIZE
    view = slc_pool.reshape(n_pool * nl, PAGE_SIZE, PAGE_W)
    pt = (page_table * nl + layer_idx).reshape(-1).astype(jnp.int32)
    f32 = jnp.float32
    hmask = (jnp.arange(NSA_HEADS)[:, None] // NSA_GROUP == jnp.arange(NSA_KV_HEADS)[None, :]).astype(f32)
    q16 = (qr * HEAD_DIM ** -0.5)[:, :, None, :] * hmask[None, :, :, None]
    q16 = jnp.pad(q16.reshape(db, NSA_HEADS, NSA_KV_W), ((0, 0), (0, STEP_ROWS - NSA_HEADS), (0, 0)))
    new8 = jnp.pad(newrows, ((0, 0), (0, 8 - newrows.shape[1]), (0, 0)))
    g16 = jnp.pad(gate.reshape(db, 3, NSA_HEADS).transpose(0, 2, 1),
                  ((0, 0), (0, STEP_ROWS - NSA_HEADS), (0, LANES - 3)))
    expand = (jnp.arange(LANES)[:, None] == jnp.arange(keys)[None, :] // SEL_BLOCK).astype(jnp.bfloat16)
    triu = (jnp.arange(nsl)[:, None] <= jnp.arange(nsl)[None, :]).astype(jnp.bfloat16)
    wlen = win_buf.shape[1]

    def page_map(k):
        return lambda b, c, pt_ref: (pt_ref[b * n_pages + c * npg + k], 0, 0)

    per_b = lambda shp: pl.BlockSpec((1,) + shp, lambda b, c, pt_ref: (b, 0, 0))
    const = lambda a: pl.BlockSpec(a.shape, lambda b, c, pt_ref: (0,) * a.ndim)
    grid_spec = pltpu.PrefetchScalarGridSpec(
        num_scalar_prefetch=1, grid=(db, nchunk),
        in_specs=[per_b((STEP_ROWS, NSA_KV_W)), per_b(ck.shape[1:]), per_b(cv.shape[1:])]
        + [pl.BlockSpec((1, PAGE_SIZE, PAGE_W), page_map(k)) for k in range(npg)]
        + [per_b((wlen, PAGE_W)), per_b((8, NSA_KV_W)), per_b((STEP_ROWS, LANES)), const(expand), const(triu)],
        out_specs=[per_b((STEP_ROWS, NSA_KV_W)), per_b((wlen, PAGE_W))],
        scratch_shapes=[pltpu.VMEM((nsl // LANES, STEP_ROWS, LANES), f32),
                        pltpu.VMEM((STEP_ROWS, 1), f32), pltpu.VMEM((STEP_ROWS, 1), f32),
                        pltpu.VMEM((STEP_ROWS, NSA_KV_W), f32), pltpu.VMEM((STEP_ROWS, NSA_KV_W), f32)])
    o16, wout = pl.pallas_call(
        functools.partial(_nsa_step_kernel, cur=cur, nsl=nsl), grid_spec=grid_spec,
        out_shape=[jax.ShapeDtypeStruct((db, STEP_ROWS, NSA_KV_W), f32),
                   jax.ShapeDtypeStruct((db, wlen, PAGE_W), f32)],
        compiler_params=pltpu.CompilerParams(dimension_semantics=("arbitrary", "arbitrary"),
                                             vmem_limit_bytes=VMEM_LIMIT_BYTES),
        name="nsa_sample_attention",
    )(pt, q16, ck, cv, *([view] * npg), win_buf, new8, g16, expand, triu)
    o = o16[:, :NSA_HEADS].reshape(db, NSA_HEADS, NSA_KV_HEADS, HEAD_DIM)
    o = jnp.take_along_axis(o, (jnp.arange(NSA_HEADS) // NSA_GROUP)[None, :, None, None], axis=2)
    return o.reshape(db, NSA_HEADS * HEAD_DIM), wout


CMP_STEP_LANES = CMP_PAGES_PER_STEP * CMP_PER_PAGE
CMP_FEATURE_GROUP = 16


def _cmp_lane_blocks(n_pages):
    lane = np.arange(n_pages * CMP_PER_PAGE)
    step, rem = lane // CMP_STEP_LANES, lane % CMP_STEP_LANES
    j, pl_ = rem // CMP_PAGES_PER_STEP, rem % CMP_PAGES_PER_STEP
    return (step * CMP_PAGES_PER_STEP + pl_) * CMP_PER_PAGE + j


def _cmp_step_t_kernel(pt_ref, *refs):
    f32 = jnp.float32
    npg = CMP_PAGES_PER_STEP
    pages = refs[:npg]
    pe_ref, w1_ref, b1_ref, w2_ref, cos_ref, sin_ref, ck_ref, cv_ref, slab_ref = refs[npg:]
    outs = (ck_ref, cv_ref)
    for k, r in enumerate(pages):
        for s in range(2 * NSA_KV_HEADS):
            slab_ref[s, k * HEAD_DIM:(k + 1) * HEAD_DIM, :] = r[0, s]
    for kv in range(2):
        h = jnp.zeros((NSA_KV_HEADS * npg, CMP_PER_PAGE * CMP_HIDDEN), f32)
        for dg in range(HEAD_DIM // CMP_FEATURE_GROUP):
            x = jnp.concatenate(
                [jnp.concatenate([slab_ref[2 * kv + g, pl.ds(d, npg, stride=HEAD_DIM), :]
                                  for g in range(NSA_KV_HEADS)], axis=0) + pe_ref[kv, d]
                 for d in range(dg * CMP_FEATURE_GROUP, (dg + 1) * CMP_FEATURE_GROUP)], axis=1)
            h = h + jnp.dot(x.astype(jnp.bfloat16), w1_ref[kv, dg], preferred_element_type=f32)
        act = jax.nn.gelu(h + b1_ref[kv]).astype(jnp.bfloat16)
        ct = _nt_dot(w2_ref[kv], act)
        tile = jnp.concatenate(
            [jnp.concatenate([ct[j * HEAD_DIM:(j + 1) * HEAD_DIM, g * npg:(g + 1) * npg] for j in range(CMP_PER_PAGE)],
                             axis=1) for g in range(NSA_KV_HEADS)], axis=0)
        if kv == 0:
            row = lax.broadcasted_iota(jnp.int32, tile.shape, 0)
            n = tile.shape[0]
            partner = jnp.where((row % HEAD_DIM) < HEAD_DIM // 2, pltpu.roll(tile, n - HEAD_DIM // 2, 0),
                                pltpu.roll(tile, HEAD_DIM // 2, 0))
            tile = tile * cos_ref[...] + partner * sin_ref[...]
        outs[kv][0] = tile


def _compress_weights_t(cw1, cb1, cw2, cpe):
    eye = jnp.eye(CMP_PER_PAGE, dtype=jnp.float32)
    w1r = cw1.reshape(2, CMP_BLOCK, HEAD_DIM, CMP_HIDDEN)
    w1 = jnp.einsum('ktdn,ja->kdjtan', w1r, eye).reshape(
        2, HEAD_DIM // CMP_FEATURE_GROUP, CMP_FEATURE_GROUP * PAGE_SIZE, CMP_PER_PAGE * CMP_HIDDEN)
    b1 = jnp.tile(cb1, (1, CMP_PER_PAGE))[:, None, :]
    w2 = jnp.einsum('knd,ja->kjdan', cw2, eye).reshape(2, CMP_PER_PAGE * HEAD_DIM, CMP_PER_PAGE * CMP_HIDDEN)
    pe = jnp.tile(cpe.transpose(0, 2, 1), (1, 1, CMP_PER_PAGE))[:, :, None, :]
    return w1.astype(jnp.bfloat16), b1, w2.astype(jnp.bfloat16), pe


def _nsa_sample_compress_t(pool, layer_idx, page_table, cw1, cb1, cw2, cpe):
    n_pool, nl = pool.shape[:2]
    db, n_pages = page_table.shape
    npg = CMP_PAGES_PER_STEP
    nchunk = n_pages // npg
    nc = n_pages * CMP_PER_PAGE
    view = jnp.transpose(pool, (0, 1, 3, 4, 5, 2)).reshape(n_pool * nl, 2 * NSA_KV_HEADS, HEAD_DIM, PAGE_SIZE)
    pt = (page_table * nl + layer_idx).reshape(-1).astype(jnp.int32)
    w1, b1, w2, pe = _compress_weights_t(cw1, cb1, cw2, cpe)
    pos = (jnp.asarray(_cmp_lane_blocks(n_pages)) + 1) * CMP_BLOCK - 1
    half = HEAD_DIM // 2
    inv_freq = ROPE_THETA ** (-2.0 * jnp.arange(half, dtype=jnp.float32) / HEAD_DIM)
    ang = inv_freq[:, None] * pos.astype(jnp.float32)[None, :]
    cos = jnp.tile(jnp.cos(ang), (2 * NSA_KV_HEADS, 1))
    sin = jnp.tile(jnp.concatenate([-jnp.sin(ang), jnp.sin(ang)], axis=0), (NSA_KV_HEADS, 1))

    def page_map(k):
        return lambda b, c, pt_ref: (pt_ref[b * n_pages + c * npg + k], 0, 0, 0)

    const = lambda a: pl.BlockSpec(a.shape, lambda b, c, pt_ref: (0,) * a.ndim, pipeline_mode=pl.Buffered(1))
    lanes_c = lambda: pl.BlockSpec((NSA_KV_W, CMP_STEP_LANES), lambda b, c, pt_ref: (0, c))
    grid_spec = pltpu.PrefetchScalarGridSpec(
        num_scalar_prefetch=1, grid=(db, nchunk),
        in_specs=[pl.BlockSpec((1, 2 * NSA_KV_HEADS, HEAD_DIM, PAGE_SIZE), page_map(k)) for k in range(npg)]
        + [const(pe), const(w1), const(b1), const(w2), lanes_c(), lanes_c()],
        out_specs=[pl.BlockSpec((1, NSA_KV_W, CMP_STEP_LANES), lambda b, c, pt_ref: (b, 0, c))] * 2,
        scratch_shapes=[pltpu.VMEM((2 * NSA_KV_HEADS, npg * HEAD_DIM, PAGE_SIZE), jnp.float32)])
    return pl.pallas_call(
        _cmp_step_t_kernel, grid_spec=grid_spec,
        out_shape=[jax.ShapeDtypeStruct((db, NSA_KV_W, nc), jnp.float32)] * 2,
        compiler_params=pltpu.CompilerParams(dimension_semantics=("arbitrary", "arbitrary"),
                                             vmem_limit_bytes=VMEM_LIMIT_BYTES),
        name="nsa_sample_compress",
    )(pt, *([view] * npg), pe, w1, b1, w2, cos, sin)


def _nsa_step_t_kernel(pt_ref, *refs, topn):
    f32, bf16 = jnp.float32, jnp.bfloat16
    npg = STEP_PAGES
    q_ref, ck_ref, cv_ref = refs[:3]
    pages = refs[3:3 + npg]
    (win_ref, newr_ref, newt_ref, gate_ref, blk_ref, exp_ref, o_ref, wout_ref,
     selt_ref, m_ref, l_ref, acc_ref, side_ref) = refs[3 + npg:]
    cc = pl.program_id(1)
    q16 = q_ref[0]
    qb = q16.astype(bf16)
    row16 = lax.broadcasted_iota(jnp.int32, (STEP_ROWS, 1), 0)

    def new_key_scores(krow):
        return jnp.sum(q16 * krow, axis=1, keepdims=True)

    @pl.when(cc == 0)
    def _():
        nc = ck_ref.shape[2]
        s = jnp.dot(qb, ck_ref[0].astype(bf16), preferred_element_type=f32)
        p = jnp.exp(s - jnp.max(s, axis=1, keepdims=True))
        pn = p / jnp.sum(p, axis=1, keepdims=True)
        o_cmp = _nt_dot(pn.astype(bf16), cv_ref[0].astype(bf16))

        rowp = lax.broadcasted_iota(jnp.int32, pn.shape, 0)
        row8 = lax.broadcasted_iota(jnp.int32, (8, nc), 0)
        blk = blk_ref[...]
        val = jnp.full((8, nc), -jnp.inf, f32)
        for g in range(NSA_KV_HEADS):
            ig = jnp.sum(jnp.where((rowp >= g * NSA_GROUP) & (rowp < (g + 1) * NSA_GROUP), pn, 0.0),
                         axis=0, keepdims=True)
            ig = ig + pltpu.roll(ig, nc - CMP_PAGES_PER_STEP, 1)
            vg = jnp.where(blk[:1] < 0, -jnp.inf, jnp.where(blk[1:2] > 0, NSA_FORCE, ig))
            val = jnp.where(row8 == g, vg, val)
        sblk = jnp.where(blk[:1] < 0, nc, blk[:1])
        sel = jnp.zeros((8, nc), f32)
        for _ in range(topn):
            top = jnp.max(val, axis=1, keepdims=True)
            pick = jnp.min(jnp.where(val == top, sblk, nc), axis=1, keepdims=True)
            hit = sblk == pick
            sel = jnp.where(hit, 1.0, sel)
            val = jnp.where(hit, -jnp.inf, val)
        selh = jnp.where(row16 < NSA_GROUP, sel[0:1], jnp.where(row16 < NSA_HEADS, sel[1:2], 0.0))
        wsel = selt_ref.shape[2]
        for j in range(selt_ref.shape[0]):
            selt_ref[j] = selh[:, j * wsel:(j + 1) * wsel]

        m_ref[...] = new_key_scores(newr_ref[0, 0:1, :])
        l_ref[...] = jnp.ones(l_ref.shape, f32)
        acc_ref[...] = jnp.broadcast_to(newr_ref[0, 1:2, :], acc_ref.shape)

        sw = jnp.dot(qb, win_ref[0, 0].astype(bf16), preferred_element_type=f32)
        sn = new_key_scores(newr_ref[0, 2:3, :])
        mw = jnp.maximum(jnp.max(sw, axis=1, keepdims=True), sn)
        pw, pnw = jnp.exp(sw - mw), jnp.exp(sn - mw)
        lw = jnp.sum(pw, axis=1, keepdims=True) + pnw
        o_win = (_nt_dot(pw.astype(bf16), win_ref[0, 1].astype(bf16)) + pnw * newr_ref[0, 3:4, :]) / lw
        gt = jax.nn.sigmoid(gate_ref[0])
        side_ref[...] = gt[:, 0:1] * o_cmp + gt[:, 2:3] * o_win
        wl = win_ref.shape[3]
        lane = lax.broadcasted_iota(jnp.int32, (NSA_KV_W, wl), 1)
        for kv in range(2):
            wout_ref[0, kv] = jnp.where(lane == wl - 1, newt_ref[0, :, 2 + kv:3 + kv],
                                        pltpu.roll(win_ref[0, kv], wl - 1, 1))

    kt = jnp.concatenate([r[0, 0] for r in pages], axis=1)
    vt = jnp.concatenate([r[0, 1] for r in pages], axis=1)
    s = jnp.dot(qb, kt.astype(bf16), preferred_element_type=f32)
    picked = jnp.dot(selt_ref[cc].astype(bf16), exp_ref[...], preferred_element_type=f32)
    s = s + (picked - 1.0) * (-NEG)
    m_old = m_ref[...]
    m_new = jnp.maximum(m_old, jnp.max(s, axis=1, keepdims=True))
    alpha = jnp.exp(m_old - m_new)
    p = jnp.exp(s - m_new)
    m_ref[...] = m_new
    l_ref[...] = l_ref[...] * alpha + jnp.sum(p, axis=1, keepdims=True)
    acc_ref[...] = acc_ref[...] * alpha + _nt_dot(p.astype(bf16), vt.astype(bf16))

    @pl.when(cc == pl.num_programs(1) - 1)
    def _():
        gt = jax.nn.sigmoid(gate_ref[0])
        o_ref[0] = side_ref[...] + gt[:, 1:2] * acc_ref[...] / l_ref[...]


def _nsa_sample_attention_t(qr, ckt, cvt, slc_pool, layer_idx, page_table, win_buf, newrows, gate):
    db, n_pages = page_table.shape
    n_pool, nl = slc_pool.shape[:2]
    wlen = win_buf.shape[1]
    nc = ckt.shape[2]
    past = n_pages * PAGE_SIZE
    cur = past // SEL_BLOCK
    npg = STEP_PAGES
    nchunk = n_pages // npg
    keys = npg * PAGE_SIZE
    wsel = npg * CMP_PER_PAGE
    f32, bf16 = jnp.float32, jnp.bfloat16
    view = jnp.transpose(slc_pool, (0, 1, 3, 4, 5, 2)).reshape(n_pool * nl, 2, NSA_KV_W, PAGE_SIZE)
    wint = jnp.transpose(win_buf, (0, 2, 3, 4, 1)).reshape(db, 2, NSA_KV_W, wlen)
    pt = (page_table * nl + layer_idx).reshape(-1).astype(jnp.int32)
    hmask = (jnp.arange(NSA_HEADS)[:, None] // NSA_GROUP == jnp.arange(NSA_KV_HEADS)[None, :]).astype(f32)
    q16 = (qr * HEAD_DIM ** -0.5)[:, :, None, :] * hmask[None, :, :, None]
    q16 = jnp.pad(q16.reshape(db, NSA_HEADS, NSA_KV_W), ((0, 0), (0, STEP_ROWS - NSA_HEADS), (0, 0)))
    newr = jnp.pad(newrows, ((0, 0), (0, 8 - newrows.shape[1]), (0, 0)))
    newt = jnp.pad(newrows.transpose(0, 2, 1), ((0, 0), (0, 0), (0, LANES - newrows.shape[1])))
    g16 = jnp.pad(gate.reshape(db, 3, NSA_HEADS).transpose(0, 2, 1),
                  ((0, 0), (0, STEP_ROWS - NSA_HEADS), (0, LANES - 3)))
    cblk = _cmp_lane_blocks(n_pages)
    jj = cblk % CMP_PER_PAGE
    sblk = np.where(jj % 2 == 0, cblk // 2, -1)
    forced = ((sblk == 0) | (sblk == cur - 1)).astype(np.int32)
    blk8 = np.zeros((8, nc), np.int32)
    blk8[0], blk8[1] = sblk, forced
    loc = np.arange(wsel)
    lstep, lrem = loc // CMP_STEP_LANES, loc % CMP_STEP_LANES
    lj, lpage = lrem // CMP_PAGES_PER_STEP, lstep * CMP_PAGES_PER_STEP + lrem % CMP_PAGES_PER_STEP
    kidx = np.arange(keys)
    expand = ((lj[:, None] % 2 == 0) & (kidx[None, :] // PAGE_SIZE == lpage[:, None])
              & ((kidx[None, :] % PAGE_SIZE) // SEL_BLOCK == lj[:, None] // 2)).astype(np.float32)
    topn = min(NSA_TOPN, cur + 1) - 1

    def page_map(k):
        return lambda b, c, pt_ref: (pt_ref[b * n_pages + c * npg + k], 0, 0, 0)

    per_b = lambda shp: pl.BlockSpec((1,) + shp, lambda b, c, pt_ref: (b,) + (0,) * len(shp))
    const = lambda a: pl.BlockSpec(a.shape, lambda b, c, pt_ref: (0,) * a.ndim)
    consts = (jnp.asarray(blk8), jnp.asarray(expand, bf16))
    grid_spec = pltpu.PrefetchScalarGridSpec(
        num_scalar_prefetch=1, grid=(db, nchunk),
        in_specs=[per_b((STEP_ROWS, NSA_KV_W)), per_b((NSA_KV_W, nc)), per_b((NSA_KV_W, nc))]
        + [pl.BlockSpec((1, 2, NSA_KV_W, PAGE_SIZE), page_map(k)) for k in range(npg)]
        + [per_b((2, NSA_KV_W, wlen)), per_b((8, NSA_KV_W)), per_b((NSA_KV_W, LANES)), per_b((STEP_ROWS, LANES))]
        + [const(a) for a in consts],
        out_specs=[per_b((STEP_ROWS, NSA_KV_W)), per_b((2, NSA_KV_W, wlen))],
        scratch_shapes=[pltpu.VMEM((nc // wsel, STEP_ROWS, wsel), f32),
                        pltpu.VMEM((STEP_ROWS, 1), f32), pltpu.VMEM((STEP_ROWS, 1), f32),
                        pltpu.VMEM((STEP_ROWS, NSA_KV_W), f32), pltpu.VMEM((STEP_ROWS, NSA_KV_W), f32)])
    o16, wout = pl.pallas_call(
        functools.partial(_nsa_step_t_kernel, topn=topn), grid_spec=grid_spec,
        out_shape=[jax.ShapeDtypeStruct((db, STEP_ROWS, NSA_KV_W), f32),
                   jax.ShapeDtypeStruct((db, 2, NSA_KV_W, wlen), f32)],
        compiler_params=pltpu.CompilerParams(dimension_semantics=("arbitrary", "arbitrary"),
                                             vmem_limit_bytes=VMEM_LIMIT_BYTES),
        name="nsa_sample_attention",
    )(pt, q16, ckt, cvt, *([view] * npg), wint, newr, newt, g16, *consts)
    o = o16[:, :NSA_HEADS].reshape(db, NSA_HEADS, NSA_KV_HEADS, HEAD_DIM)
    o = jnp.take_along_axis(o, (jnp.arange(NSA_HEADS) // NSA_GROUP)[None, :, None, None], axis=2)
    wout = jnp.transpose(wout.reshape(db, 2, NSA_KV_HEADS, HEAD_DIM, wlen), (0, 4, 1, 2, 3))
    return o.reshape(db, NSA_HEADS * HEAD_DIM), wout


DIL_ROW_CHUNK = 64


def _dil_step_kernel(q_ref, buf_ref, newt_ref, newr_ref, bias_ref, o_ref, out_ref, p_ref, pn_ref, den_ref):
    f32, bf16 = jnp.float32, jnp.bfloat16
    kv = pl.program_id(1)
    wlen = buf_ref.shape[3]
    nrow = buf_ref.shape[2]
    q16 = q_ref[0]

    @pl.when(kv == 0)
    def _():
        s = jnp.dot(q16.astype(bf16), buf_ref[0, 0].astype(bf16), preferred_element_type=f32)
        s_new = jnp.sum(q16 * newr_ref[0, 0:1, :], axis=1, keepdims=True)
        ms, es, ens, dens = [], [], [], []
        for g in range(len(DIL_GROUPS)):
            sg = s + bias_ref[g:g + 1, :]
            m = jnp.maximum(jnp.max(sg, axis=1, keepdims=True), s_new)
            e, en = jnp.exp(sg - m), jnp.exp(s_new - m)
            ms.append(m); es.append(e); ens.append(en)
            dens.append(jnp.sum(e, axis=1, keepdims=True) + en)
        m_all = functools.reduce(jnp.maximum, ms)
        ws = [jnp.exp(m - m_all) for m in ms]
        p_ref[...] = sum(w * e for w, e in zip(ws, es))
        pn_ref[...] = sum(w * en for w, en in zip(ws, ens))
        den_ref[...] = sum(w * d for w, d in zip(ws, dens))

    @pl.when(kv == 1)
    def _():
        r = _nt_dot(p_ref[...].astype(bf16), buf_ref[0, 0].astype(bf16))
        r = (r + pn_ref[...] * newr_ref[0, 1:2, :]) / den_ref[...]
        head = lax.broadcasted_iota(jnp.int32, r.shape, 1) // HEAD_DIM
        row = lax.broadcasted_iota(jnp.int32, r.shape, 0)
        o_ref[0] = jnp.broadcast_to(jnp.sum(jnp.where(head == row, r, 0.0), axis=0, keepdims=True), o_ref.shape[1:])

    lane = lax.broadcasted_iota(jnp.int32, (DIL_ROW_CHUNK, wlen), 1)
    for c in range(nrow // DIL_ROW_CHUNK):
        rs = slice(c * DIL_ROW_CHUNK, (c + 1) * DIL_ROW_CHUNK)
        col = jnp.where(kv == 0, newt_ref[0, rs, 0:1], newt_ref[0, rs, 1:2])
        out_ref[0, 0, rs, :] = jnp.where(lane == wlen - 1, col, pltpu.roll(buf_ref[0, 0, rs, :], wlen - 1, 1))


def _dil_sample_attention(qr, kr_new, v_new, buf):
    db, wlen = buf.shape[:2]
    f32 = jnp.float32
    buft = jnp.transpose(buf, (0, 2, 3, 4, 1)).reshape(db, 2, DIL_W, wlen)
    eye = jnp.eye(DIL_HEADS, dtype=f32)
    q16 = ((qr * HEAD_DIM ** -0.5)[:, :, None, :] * eye[None, :, :, None]).reshape(db, DIL_HEADS, DIL_W)
    newr = jnp.pad(jnp.stack([kr_new, v_new], axis=1), ((0, 0), (0, 6), (0, 0)))
    newt = jnp.pad(jnp.stack([kr_new, v_new], axis=2), ((0, 0), (0, 0), (0, LANES - 2)))
    back = wlen - jnp.arange(wlen)
    bias = jnp.stack([jnp.where((back % d == 0) & (back // d <= DIL_SPAN), 0.0, NEG) for _, d in DIL_GROUPS])
    bias = jnp.pad(bias, ((0, 8 - len(DIL_GROUPS)), (0, 0))).astype(f32)
    o, new_buf = pl.pallas_call(
        _dil_step_kernel,
        grid=(db, 2),
        in_specs=[pl.BlockSpec((1, DIL_HEADS, DIL_W), lambda b, k: (b, 0, 0)),
                  pl.BlockSpec((1, 1, DIL_W, wlen), lambda b, k: (b, k, 0, 0)),
                  pl.BlockSpec((1, DIL_W, LANES), lambda b, k: (b, 0, 0)),
                  pl.BlockSpec((1, 8, DIL_W), lambda b, k: (b, 0, 0)),
                  pl.BlockSpec((8, wlen), lambda b, k: (0, 0))],
        out_specs=[pl.BlockSpec((1, 8, DIL_W), lambda b, k: (b, 0, 0)),
                   pl.BlockSpec((1, 1, DIL_W, wlen), lambda b, k: (b, k, 0, 0))],
        out_shape=[jax.ShapeDtypeStruct((db, 8, DIL_W), f32),
                   jax.ShapeDtypeStruct((db, 2, DIL_W, wlen), f32)],
        scratch_shapes=[pltpu.VMEM((DIL_HEADS, wlen), f32), pltpu.VMEM((DIL_HEADS, 1), f32),
                        pltpu.VMEM((DIL_HEADS, 1), f32)],
        compiler_params=pltpu.CompilerParams(dimension_semantics=("arbitrary", "arbitrary"),
                                             vmem_limit_bytes=VMEM_LIMIT_BYTES),
        name="dil_sample",
    )(q16, buft, newt, newr, bias)
    new_buf = jnp.transpose(new_buf.reshape(db, 2, DIL_HEADS, HEAD_DIM, wlen), (0, 4, 1, 2, 3))
    return o[:, 0], new_buf


def _dil_band_kernel(q_ref, kp_ref, kc_ref, vp_ref, vc_ref, num_ref, st_ref):
    f32, bf16 = jnp.float32, jnp.bfloat16
    blk = DIL_BLOCK
    n = pl.program_id(0)
    i = lax.broadcasted_iota(jnp.int32, (blk, 2 * blk), 0)
    j = lax.broadcasted_iota(jnp.int32, (blk, 2 * blk), 1) - blk
    ok = (i - j >= 0) & (i - j <= DIL_SPAN) & (n * blk + j >= 0)
    bias = jnp.where(ok, 0.0, NEG)
    bias = jnp.concatenate([bias, bias], axis=0)
    lane = lax.broadcasted_iota(jnp.int32, (blk, LANES), 1)
    first = lane < HEAD_DIM
    stats = jnp.zeros((blk, LANES), f32)
    for p in range(DIL_HEADS // 2):
        cols = slice(p * LANES, (p + 1) * LANES)
        qp = q_ref[:, cols] * (HEAD_DIM ** -0.5)
        qst = jnp.concatenate([jnp.where(first, qp, 0.0), jnp.where(first, 0.0, qp)], axis=0).astype(bf16)
        kk = jnp.concatenate([kp_ref[:, cols], kc_ref[:, cols]], axis=0).astype(bf16)
        vv = jnp.concatenate([vp_ref[:, cols], vc_ref[:, cols]], axis=0).astype(bf16)
        s = _nt_dot(qst, kk) + bias
        m = jnp.max(s, axis=1, keepdims=True)
        e = jnp.exp(s - m)
        den = jnp.sum(e, axis=1, keepdims=True)
        nm = jnp.dot(e.astype(bf16), vv, preferred_element_type=f32)
        num_ref[:, cols] = jnp.where(first, nm[:blk], nm[blk:])
        for a in range(2):
            h = 2 * p + a
            stats = jnp.where(lane == h, m[a * blk:(a + 1) * blk], stats)
            stats = jnp.where(lane == DIL_HEADS + h, den[a * blk:(a + 1) * blk], stats)
    st_ref[...] = stats


def _dil_band_stats(qr, kr, v, d):
    l = qr.shape[0]
    assert l % (d * DIL_BLOCK) == 0
    nb = l // (d * DIL_BLOCK)
    view = lambda a: a.reshape(l // d, d * a.shape[1])
    blk = lambda w, prev: pl.BlockSpec((DIL_BLOCK, w),
                                       (lambda n, r: (jnp.maximum(n - 1, 0), r)) if prev else (lambda n, r: (n, r)))
    num, st = pl.pallas_call(
        _dil_band_kernel,
        grid=(nb, d),
        in_specs=[blk(DIL_W, False), blk(DIL_W, True), blk(DIL_W, False), blk(DIL_W, True), blk(DIL_W, False)],
        out_specs=[blk(DIL_W, False), blk(LANES, False)],
        out_shape=[jax.ShapeDtypeStruct((l // d, d * DIL_W), jnp.float32),
                   jax.ShapeDtypeStruct((l // d, d * LANES), jnp.float32)],
        compiler_params=pltpu.CompilerParams(dimension_semantics=("arbitrary", "arbitrary"),
                                             vmem_limit_bytes=VMEM_LIMIT_BYTES),
        name="dil_band_stats",
    )(view(qr), view(kr), view(kr), view(v), view(v))
    st = st.reshape(l, LANES)
    return st[:, :DIL_HEADS], st[:, DIL_HEADS:2 * DIL_HEADS], num.reshape(l, DIL_HEADS, HEAD_DIM)


GDN_PREP_ROWS = 512
GDN_TILE_CHUNKS = 4
GDN_PAIRS = GDN_HEADS // 2
GDN_A_LANE = 3 * NSA_HEADS
GDN_B_LANE = GDN_A_LANE + GDN_HEADS


def _hi_lo(x):
    hi = x.astype(jnp.bfloat16)
    return hi, (x - hi.astype(jnp.float32)).astype(jnp.bfloat16)


def _three_way(x):
    f32 = jnp.float32
    x1 = x.astype(jnp.bfloat16)
    r1 = x - x1.astype(f32)
    x2 = r1.astype(jnp.bfloat16)
    return x1, x2, (r1 - x2.astype(f32)).astype(jnp.bfloat16)


def _dot_select(x, sel):
    return sum(jnp.dot(piece, sel, preferred_element_type=jnp.float32) for piece in _three_way(x))


def _select_dot(sel, x):
    return sum(jnp.dot(sel, piece, preferred_element_type=jnp.float32) for piece in _three_way(x))


def _dot_hl(a, b):
    f32 = jnp.float32
    ah, al = _hi_lo(a)
    bh, bl = _hi_lo(b)
    return (jnp.dot(ah, bh, preferred_element_type=f32) + jnp.dot(ah, bl, preferred_element_type=f32)
            + jnp.dot(al, bh, preferred_element_type=f32))


def _gdn_prep_kernel(u_ref, sm_ref, cw_ref, prm_ref, ea_ref, eb_ref, eh_ref, q_ref, k_ref, v_ref, g_ref, b_ref,
                     carry_ref):
    f32 = jnp.float32
    i = pl.program_id(0)
    tl = u_ref.shape[0]

    @pl.when(i == 0)
    def _():
        carry_ref[...] = jnp.zeros(carry_ref.shape, f32)

    u = u_ref[...]
    prev = carry_ref[...]
    row = lax.broadcasted_iota(jnp.int32, u.shape, 0)

    def shifted(k):
        r = pltpu.roll(u, k, 0)
        for j in range(k):
            r = jnp.where(row == j, prev[8 - k + j:8 - k + j + 1], r)
        return r

    cw = cw_ref[...]
    c = cw[0:1] * shifted(3) + cw[1:2] * shifted(2) + cw[2:3] * shifted(1) + cw[3:4] * u
    carry_ref[...] = u[tl - 8:]
    c = c * jax.nn.sigmoid(c)
    eh = eh_ref[...]

    def l2n(x):
        return x * lax.rsqrt(_dot_select(x * x, eh) + NORM_EPS)

    q_ref[...] = l2n(c[:, :GDN_W]) * (HEAD_DIM ** -0.5)
    k_ref[...] = l2n(c[:, GDN_W:2 * GDN_W])
    v_ref[...] = c[:, 2 * GDN_W:]
    sm = sm_ref[...]
    x = sm + prm_ref[1:2]
    softplus = jnp.maximum(x, 0.0) + jnp.log(1.0 + jnp.exp(-jnp.abs(x)))
    g_ref[...] = _dot_select(-jnp.exp(prm_ref[0:1]) * softplus, ea_ref[...])
    b_ref[...] = _dot_select(jax.nn.sigmoid(sm), eb_ref[...])


def _gdn_chunk_kernel(q_ref, k_ref, v_ref, g_ref, b_ref, z_ref, nw_ref, lt_ref, eh_ref, o_ref, s_out_ref, s_ref):
    f32 = jnp.float32
    ch = GDN_CHUNK
    i = pl.program_id(0)

    @pl.when(i == 0)
    def _():
        s_ref[...] = jnp.zeros(s_ref.shape, f32)

    lane = lax.broadcasted_iota(jnp.int32, (ch, LANES), 1)
    first = lane < HEAD_DIM
    stack = lambda x: jnp.concatenate([jnp.where(first, x, 0.0), jnp.where(first, 0.0, x)], axis=0)
    r2 = lax.broadcasted_iota(jnp.int32, (2 * ch, 2 * ch), 0)
    c2 = lax.broadcasted_iota(jnp.int32, (2 * ch, 2 * ch), 1)
    same = (r2 // ch) == (c2 // ch)
    tri = same & (r2 % ch >= c2 % ch)
    strict = same & (r2 % ch > c2 % ch)
    eye = r2 == c2
    eye_f = jnp.where(eye, 1.0, 0.0)
    diag2 = lax.broadcasted_iota(jnp.int32, (ch, LANES), 0) == lane % HEAD_DIM
    lt = lt_ref[...]
    bf = lambda x: x.astype(jnp.bfloat16)
    dot = lambda a, b: jnp.dot(bf(a), bf(b), preferred_element_type=f32)

    blocks = [(c, p) for c in range(GDN_TILE_CHUNKS) for p in range(GDN_PAIRS)]
    ld = lambda ref, c, p: ref[c * ch:(c + 1) * ch, p * LANES:(p + 1) * LANES]
    gcs = [_select_dot(lt, ld(g_ref, c, p)) for c, p in blocks]
    amats, qks, rhs_u, rhs_w, qgs, kds, decs = [], [], [], [], [], [], []
    for (c, p), gc in zip(blocks, gcs):
        kk, qq, vv, bb = ld(k_ref, c, p), ld(q_ref, c, p), ld(v_ref, c, p), ld(b_ref, c, p)
        eg = jnp.exp(gc)
        g_end = gc[ch - 1:ch]
        kb = kk * bb
        col = jnp.concatenate([jnp.broadcast_to(gc[:, 0:1], (ch, LANES)),
                               jnp.broadcast_to(gc[:, HEAD_DIM:HEAD_DIM + 1], (ch, LANES))], axis=0)
        rowv = jnp.sum(jnp.where(diag2, gc, 0.0), axis=0, keepdims=True)
        gam = jnp.where(tri, jnp.exp(jnp.where(tri, col - rowv, 0.0)), 0.0)
        kst = stack(kk)
        amats.append(jnp.where(strict, _nt_dot(bf(stack(kb)), bf(kst)) * gam, 0.0))
        qks.append(jnp.where(tri, _nt_dot(bf(stack(qq)), bf(kst)) * gam, 0.0))
        rhs_u.append(stack(vv * bb))
        rhs_w.append(stack(kb * eg))
        qgs.append(stack(qq * eg))
        kds.append(stack(kk * jnp.exp(g_end - gc)))
        decs.append(jnp.sum(jnp.where(eye, jnp.exp(g_end), 0.0), axis=1, keepdims=True))
    xs = [eye_f - a for a in amats]
    pws = [_dot_hl(a, a) for a in amats]
    steps = GDN_CHUNK.bit_length() - 2
    for r in range(steps):
        xs = [x + _dot_hl(x, pw) for x, pw in zip(xs, pws)]
        if r < steps - 1:
            pws = [_dot_hl(pw, pw) for pw in pws]
    uus = [dot(x, u) for x, u in zip(xs, rhs_u)]
    wws = [dot(x, w) for x, w in zip(xs, rhs_w)]
    kdts = [kd.T for kd in kds]
    states = [s_ref[p] for p in range(GDN_PAIRS)]
    for c in range(GDN_TILE_CHUNKS):
        rs = slice(c * ch, (c + 1) * ch)
        outs = []
        for p in range(GDN_PAIRS):
            n = c * GDN_PAIRS + p
            s = states[p]
            v_new = uus[n] - dot(wws[n], s)
            o_st = dot(qgs[n], s) + dot(qks[n], v_new)
            states[p] = s * decs[n] + dot(kdts[n], v_new)
            outs.append(o_st[:ch] + o_st[ch:])
        o = jnp.concatenate(outs, axis=1)
        ms = _dot_select(o * o, eh_ref[...]) * (1.0 / HEAD_DIM)
        z = z_ref[rs, :]
        o_ref[rs, :] = o * lax.rsqrt(ms + NORM_EPS) * nw_ref[...] * (z * jax.nn.sigmoid(z))
    for p in range(GDN_PAIRS):
        s_ref[p] = states[p]

    @pl.when(i == pl.num_programs(0) - 1)
    def _():
        s_out_ref[...] = s_ref[...]


def _gdn_prompt(qkv, small, z, conv_w, a_log, dt_bias, norm_w):
    l = qkv.shape[0]
    f32, bf16 = jnp.float32, jnp.bfloat16
    w = GDN_W
    hh = jnp.arange(w) // HEAD_DIM
    expander = lambda base: (jnp.arange(LANES)[:, None] == base + hh[None, :]).astype(bf16)
    eh = (hh[:, None] == hh[None, :]).astype(bf16)
    cw8 = jnp.zeros((8, 3 * w), f32).at[:GDN_CONV].set(conv_w)
    prm = jnp.zeros((8, LANES), f32)
    prm = prm.at[0, GDN_A_LANE:GDN_A_LANE + GDN_HEADS].set(a_log).at[1, GDN_A_LANE:GDN_A_LANE + GDN_HEADS].set(dt_bias)
    tl = GDN_PREP_ROWS
    row = lambda wd: pl.BlockSpec((tl, wd), lambda i: (i, 0))
    const = lambda a: pl.BlockSpec(a.shape, lambda i: (0,) * a.ndim)
    ea, eb = expander(GDN_A_LANE), expander(GDN_B_LANE)
    q, k, v, g, b = pl.pallas_call(
        _gdn_prep_kernel,
        grid=(l // tl,),
        in_specs=[row(3 * w), row(LANES), const(cw8), const(prm), const(ea), const(eb), const(eh)],
        out_specs=[row(w)] * 5,
        out_shape=[jax.ShapeDtypeStruct((l, w), f32)] * 5,
        scratch_shapes=[pltpu.VMEM((8, 3 * w), f32)],
        compiler_params=pltpu.CompilerParams(dimension_semantics=("arbitrary",), vmem_limit_bytes=VMEM_LIMIT_BYTES),
        name="gdn_prep",
    )(qkv, small, cw8, prm, ea, eb, eh)
    tc = GDN_TILE_CHUNKS * GDN_CHUNK
    lt = (jnp.arange(GDN_CHUNK)[:, None] >= jnp.arange(GDN_CHUNK)[None, :]).astype(bf16)
    nw = jnp.tile(norm_w, GDN_HEADS).reshape(1, w)
    rowc = pl.BlockSpec((tc, w), lambda i: (i, 0))
    o, s_bd = pl.pallas_call(
        _gdn_chunk_kernel,
        grid=(l // tc,),
        in_specs=[rowc] * 6 + [const(nw), const(lt), const(eh)],
        out_specs=[rowc, pl.BlockSpec((GDN_PAIRS, LANES, LANES), lambda i: (0, 0, 0))],
        out_shape=[jax.ShapeDtypeStruct((l, w), f32), jax.ShapeDtypeStruct((GDN_PAIRS, LANES, LANES), f32)],
        scratch_shapes=[pltpu.VMEM((GDN_PAIRS, LANES, LANES), f32)],
        compiler_params=pltpu.CompilerParams(dimension_semantics=("arbitrary",), vmem_limit_bytes=VMEM_LIMIT_BYTES),
        name="gdn_chunk",
    )(q, k, v, g, b, z, nw, lt, eh)
    s4 = s_bd.reshape(GDN_PAIRS, 2, HEAD_DIM, 2, HEAD_DIM)
    s_fin = jnp.stack([s4[:, 0, :, 0], s4[:, 1, :, 1]], axis=1).reshape(GDN_HEADS, HEAD_DIM, HEAD_DIM)
    return o, s_fin


def _split_cols(h, widths):
    parts, start = [], 0
    for w in widths:
        parts.append(h[..., start:start + w])
        start += w
    return parts


def _even_widths():
    return (NSA_Q_W,) + (NSA_KV_W,) * 6 + (3 * NSA_HEADS, 3 * GDN_W, GDN_HEADS, GDN_HEADS, GDN_W)


def _rms_norm(x, w):
    return x * lax.rsqrt(jnp.mean(jnp.square(x), axis=-1, keepdims=True) + NORM_EPS) * w


def _l2_norm(x):
    return x * lax.rsqrt(jnp.sum(jnp.square(x), axis=-1, keepdims=True) + NORM_EPS)


def _rope(x, pos):
    half = HEAD_DIM // 2
    inv_freq = ROPE_THETA ** (-2.0 * jnp.arange(half, dtype=jnp.float32) / HEAD_DIM)
    ang = pos.astype(jnp.float32)[:, None] * inv_freq[None, :]
    cos, sin = jnp.cos(ang)[:, None, :], jnp.sin(ang)[:, None, :]
    xf = x.astype(jnp.float32)
    x1, x2 = xf[..., :half], xf[..., half:]
    return jnp.concatenate([x1 * cos - x2 * sin, x2 * cos + x1 * sin], axis=-1)


def _causal_dwconv(hist, u, w):
    width, s = w.shape[0], u.shape[1]
    ext = jnp.concatenate([hist.astype(u.dtype), u], axis=1)
    out = w[0] * ext[:, :s]
    for j in range(1, width):
        out = out + w[j] * ext[:, j:j + s]
    return out, ext[:, s:]


def _masked_softmax(s, mask):
    s = jnp.where(mask, s, -jnp.inf)
    m = jnp.max(s, axis=-1, keepdims=True)
    m = jnp.where(jnp.isfinite(m), m, 0.0)
    e = jnp.where(mask, jnp.exp(s - m), 0.0)
    den = jnp.sum(e, axis=-1, keepdims=True)
    return e / jnp.where(den > 0.0, den, 1.0)


def _gather_pages(pool, page_table, layer_idx):
    rows = pool[page_table, layer_idx]
    return rows.reshape(rows.shape[0], -1, *rows.shape[3:])


def _nsa_compress(rows, w1, b1, w2, pe):
    b, l, g, dh = rows.shape
    nc = l // CMP_BLOCK
    blk = rows[:, :nc * CMP_BLOCK].astype(jnp.float32).reshape(b, nc, CMP_BLOCK, g, dh) + pe[:, None, :]
    flat = blk.transpose(0, 1, 3, 2, 4).reshape(b, nc, g, CMP_BLOCK * dh)
    return jax.nn.gelu(flat @ w1 + b1) @ w2


def _nsa_compressed_kv(k_rows, v_rows, cw1, cb1, cw2, cpe):
    ck = _nsa_compress(k_rows, cw1[0], cb1[0], cw2[0], cpe[0])
    cv = _nsa_compress(v_rows, cw1[1], cb1[1], cw2[1], cpe[1])
    nc = ck.shape[1]
    ck = _rope(ck, (jnp.arange(nc) + 1) * CMP_BLOCK - 1)
    return ck, cv


def _nsa_attend(q, qpos, ck, cv, sk, sv, wk, wv, wpos):
    b, nq = q.shape[:2]
    scale = HEAD_DIM ** -0.5
    nc, ns = ck.shape[1], sk.shape[2]
    cend = (jnp.arange(nc) + 1) * CMP_BLOCK - 1
    s = jnp.einsum('bqghd,bcgd->bghqc', q, ck) * scale
    p_cmp = _masked_softmax(s, cend[None, :] <= qpos[:, None])
    o_cmp = jnp.einsum('bghqc,bcgd->bqghd', p_cmp, cv)
    imp = jnp.sum(p_cmp, axis=2)
    imp = jnp.pad(imp, ((0, 0), (0, 0), (0, 0), (0, ns * CMP_PER_SEL - nc)))
    imp = imp.reshape(b, NSA_KV_HEADS, nq, ns, CMP_PER_SEL).sum(-1)
    blk = jnp.arange(ns)[None, :]
    cur = (qpos // SEL_BLOCK)[:, None]
    forced = (blk == 0) | (blk == cur) | (blk == cur - 1)
    imp = jnp.where(blk <= cur, jnp.where(forced, NSA_FORCE, imp), -1.0)
    _, idx = lax.top_k(imp, min(NSA_TOPN, ns))
    n = idx.shape[-1]
    pick = jax.vmap(jax.vmap(lambda kb, ix: kb[ix]))
    ksel = pick(sk, idx).reshape(b, NSA_KV_HEADS, nq, n * SEL_BLOCK, HEAD_DIM)
    vsel = pick(sv, idx).reshape(b, NSA_KV_HEADS, nq, n * SEL_BLOCK, HEAD_DIM)
    kpos = (idx[..., None] * SEL_BLOCK + jnp.arange(SEL_BLOCK)).reshape(b, NSA_KV_HEADS, nq, n * SEL_BLOCK)
    s = jnp.einsum('bqghd,bgqkd->bghqk', q, ksel) * scale
    p = _masked_softmax(s, (kpos <= qpos[:, None])[:, :, None])
    o_slc = jnp.einsum('bghqk,bgqkd->bqghd', p, vsel)
    dist = qpos[:, None] - wpos[None, :]
    wmask = (dist >= 0) & (dist <= NSA_WINDOW) & (wpos[None, :] >= 0)
    s = jnp.einsum('bqghd,bkgd->bghqk', q, wk) * scale
    p = _masked_softmax(s, wmask)
    o_win = jnp.einsum('bghqk,bkgd->bqghd', p, wv)
    return o_cmp, o_slc, o_win


def _nsa_prompt(q, kc, vc, ks, vs, kw, vw, small, cw1, cb1, cw2, cpe):
    b, l = q.shape[:2]
    pos = jnp.arange(l)
    qr = _rope(q, pos)
    ck, cv = _nsa_compressed_kv(kc, vc, cw1, cb1, cw2, cpe)
    ksr = _rope(ks, pos)
    vsf = vs.astype(jnp.float32)
    kwr = _rope(kw, pos)
    vwf = vw.astype(jnp.float32)
    flat = lambda t: t.reshape(t.shape[1], -1)
    o_nsa = _nsa_prompt_attention(flat(qr), small, flat(ck), flat(cv), flat(ksr), flat(vsf), flat(kwr), flat(vwf))
    keep = min(NSA_WINDOW, l)
    rows_cmp = jnp.stack([kc, vc], axis=2)
    rows_slc = jnp.stack([ksr, vsf], axis=2)
    rows_win = jnp.stack([kwr[:, l - keep:], vwf[:, l - keep:]], axis=2)
    return o_nsa[None], rows_cmp, rows_slc, rows_win


def _nsa_sample(q, kc, vc, ks, vs, kw, vw, gate, cmp_pool, slc_pool, layer_idx, win_buf, page_table,
                cw1, cb1, cw2, cpe):
    db, s = q.shape[:2]
    past = page_table.shape[1] * PAGE_SIZE
    wb = win_buf.shape[1]
    assert s == 1 and wb == NSA_WINDOW and past >= wb and past % (STEP_PAGES * PAGE_SIZE) == 0
    qpos = past + jnp.arange(s)
    qr = _rope(q, qpos)
    ckt, cvt = _nsa_sample_compress_t(cmp_pool, layer_idx, page_table, cw1, cb1, cw2, cpe)
    ksr = _rope(ks, qpos)
    vsf = vs.astype(jnp.float32)
    kwr = _rope(kw, qpos)
    vwf = vw.astype(jnp.float32)
    newrows = jnp.stack([t.reshape(db, NSA_KV_W) for t in (ksr, vsf, kwr, vwf)], axis=1)
    o_nsa, rows_win = _nsa_sample_attention_t(qr[:, 0], ckt, cvt, slc_pool, layer_idx, page_table, win_buf,
                                              newrows, gate.reshape(db, -1))
    rows_cmp = jnp.stack([kc, vc], axis=2)
    rows_slc = jnp.stack([ksr, vsf], axis=2)
    return o_nsa[:, None], rows_cmp, rows_slc, rows_win


def _gdn_chunked(q, k, v, g, beta, s0):
    b, l, h, dk = q.shape
    dv = v.shape[-1]
    c = GDN_CHUNK
    nch = l // c
    r = lambda a: jnp.moveaxis(a.reshape(b, nch, c, h, *a.shape[3:]), 3, 2)
    q, k, v, g, beta = r(q), r(k), r(v), r(g), r(beta)
    gc = jnp.cumsum(g, axis=-1)
    ii = jnp.arange(c)
    tri = ii[:, None] >= ii[None, :]
    strict = ii[:, None] > ii[None, :]
    diff = gc[..., :, None] - gc[..., None, :]
    gamma = jnp.where(tri, jnp.exp(jnp.where(tri, diff, 0.0)), 0.0)
    kb = k * beta[..., None]
    a_mat = jnp.where(strict, jnp.einsum('bnhik,bnhjk->bnhij', kb, k) * gamma, 0.0)
    eye = jnp.eye(c, dtype=jnp.float32)
    t_inv = lax.linalg.triangular_solve(eye + a_mat, jnp.broadcast_to(eye, a_mat.shape),
                                        left_side=True, lower=True, unit_diagonal=True)
    u = t_inv @ (v * beta[..., None])
    w = t_inv @ (kb * jnp.exp(gc)[..., None])
    qk = jnp.where(tri, jnp.einsum('bnhik,bnhjk->bnhij', q, k) * gamma, 0.0)
    qg = q * jnp.exp(gc)[..., None]
    kd = k * jnp.exp(gc[..., -1:] - gc)[..., None]
    glast = jnp.exp(gc[..., -1])

    def step(state, xs):
        qg_c, kd_c, u_c, w_c, qk_c, gl_c = xs
        v_new = u_c - jnp.einsum('bhck,bhkv->bhcv', w_c, state)
        o = jnp.einsum('bhck,bhkv->bhcv', qg_c, state) + jnp.einsum('bhij,bhjv->bhiv', qk_c, v_new)
        state = state * gl_c[..., None, None] + jnp.einsum('bhck,bhcv->bhkv', kd_c, v_new)
        return state, o

    xs = tuple(jnp.moveaxis(a, 1, 0) for a in (qg, kd, u, w, qk, glast))
    s_fin, o = lax.scan(step, s0, xs)
    o = jnp.moveaxis(jnp.moveaxis(o, 0, 1), 2, 3).reshape(b, l, h, dv)
    return o, s_fin


def _gdn_recurrent(q, k, v, g, beta, s0):
    def step(state, xs):
        q_t, k_t, v_t, g_t, b_t = xs
        state = state * jnp.exp(g_t)[..., None, None]
        v_t = (v_t - jnp.einsum('bhk,bhkv->bhv', k_t, state)) * b_t[..., None]
        state = state + jnp.einsum('bhk,bhv->bhkv', k_t, v_t)
        return state, jnp.einsum('bhk,bhkv->bhv', q_t, state)

    xs = tuple(jnp.moveaxis(a, 1, 0) for a in (q, k, v, g, beta))
    s_fin, o = lax.scan(step, s0, xs)
    return jnp.moveaxis(o, 0, 1), s_fin


def _gdn_mix(qkv, a, bt, z, conv_hist, s0, conv_w, a_log, dt_bias, norm_w, chunked):
    b, s = qkv.shape[:2]
    c, new_hist = _causal_dwconv(conv_hist, qkv, conv_w)
    c = jax.nn.silu(c.astype(jnp.float32))
    q, k, v = [t.reshape(b, s, GDN_HEADS, HEAD_DIM) for t in jnp.split(c, 3, axis=-1)]
    q = _l2_norm(q) * HEAD_DIM ** -0.5
    k = _l2_norm(k)
    beta = jax.nn.sigmoid(bt.astype(jnp.float32))
    g = -jnp.exp(a_log) * jax.nn.softplus(a.astype(jnp.float32) + dt_bias)
    s0 = s0.astype(jnp.float32)
    if chunked:
        o, s_fin = _gdn_chunked(q, k, v, g, beta, s0)
    else:
        o, s_fin = _gdn_recurrent(q, k, v, g, beta, s0)
    o = _rms_norm(o, norm_w) * jax.nn.silu(z.astype(jnp.float32).reshape(b, s, GDN_HEADS, HEAD_DIM))
    return o.reshape(b, s, GDN_W), new_hist, s_fin


def _even_merge(o_cmp, o_slc, o_win, gate, o_gdn):
    b, s = gate.shape[:2]
    gt = jax.nn.sigmoid(gate.astype(jnp.float32)).reshape(b, s, 3, NSA_HEADS, 1)
    o_nsa = gt[:, :, 0] * o_cmp + gt[:, :, 1] * o_slc + gt[:, :, 2] * o_win
    return jnp.concatenate([o_nsa.reshape(b, s, NSA_Q_W), o_gdn], axis=-1)


def _proj(x, w):
    b, s, d = x.shape
    n = w.shape[1]
    npad = -(-n // LANES) * LANES
    wp = jnp.pad(w, ((0, 0), (0, npad - n)))
    return _matmul(x.reshape(b * s, d), wp)[:, :n].reshape(b, s, n)


def _even_prompt(x, w_in, cw1, cb1, cw2, cpe, conv_w, a_log, dt_bias, norm_w):
    b, l, _ = x.shape
    q, kc, vc, ks, vs, kw, vw, gate, qkv, a, bt, z = _split_cols(_proj(x, w_in), _even_widths())
    heads = lambda t: t.reshape(b, l, -1, HEAD_DIM)
    assert b == 1
    small = jnp.concatenate([gate, a, bt], axis=-1).reshape(l, -1)
    small = jnp.pad(small, ((0, 0), (0, LANES - small.shape[-1])))
    o_nsa, r_cmp, r_slc, r_win = _nsa_prompt(
        heads(q), heads(kc), heads(vc), heads(ks), heads(vs), heads(kw), heads(vw), small, cw1, cb1, cw2, cpe)
    o_gdn, s_fin = _gdn_prompt(qkv[0], small, z[0], conv_w, a_log, dt_bias, norm_w)
    conv_hist = qkv[:, l - (GDN_CONV - 1):]
    return jnp.concatenate([o_nsa, o_gdn[None]], axis=-1), r_cmp, r_slc, r_win, conv_hist, s_fin[None]


def _even_sample(x, cmp_pool, slc_pool, layer_idx, win_buf, conv_hist, s0, page_table,
                 w_in, cw1, cb1, cw2, cpe, conv_w, a_log, dt_bias, norm_w):
    b, s, _ = x.shape
    q, kc, vc, ks, vs, kw, vw, gate, qkv, a, bt, z = _split_cols(_proj(x, w_in), _even_widths())
    heads = lambda t: t.reshape(b, s, -1, HEAD_DIM)
    o_nsa, r_cmp, r_slc, r_win = _nsa_sample(
        heads(q), heads(kc), heads(vc), heads(ks), heads(vs), heads(kw), heads(vw), gate,
        cmp_pool, slc_pool, layer_idx, win_buf, page_table, cw1, cb1, cw2, cpe)
    o_gdn, new_hist, s_fin = _gdn_mix(qkv, a, bt, z, conv_hist, s0, conv_w, a_log, dt_bias, norm_w, False)
    return jnp.concatenate([o_nsa, o_gdn], axis=-1), r_cmp, r_slc, r_win, new_hist, s_fin


def _dilated_band_stats(q, k, v, d):
    b, l, h, dh = q.shape
    unit = d * DIL_BLOCK
    lp = -(-l // unit) * unit
    nb = lp // unit
    to_sub = lambda a: jnp.pad(a, ((0, 0), (0, lp - l), (0, 0), (0, 0))).reshape(b, nb, DIL_BLOCK, d, h, dh)
    qs, ks, vs = to_sub(q), to_sub(k), to_sub(v)
    prev = lambda a: jnp.concatenate([jnp.zeros_like(a[:, :1]), a[:, :-1]], axis=1)
    kk = jnp.concatenate([prev(ks), ks], axis=2)
    vv = jnp.concatenate([prev(vs), vs], axis=2)
    s = jnp.einsum('bnirhd,bnjrhd->bnrhij', qs, kk) * HEAD_DIM ** -0.5
    i = jnp.arange(DIL_BLOCK)
    j = jnp.arange(2 * DIL_BLOCK) - DIL_BLOCK
    dist = i[:, None] - j[None, :]
    sub_k = jnp.arange(nb)[:, None, None] * DIL_BLOCK + j[None, None, :]
    mask = (dist >= 0)[None] & (dist <= DIL_SPAN)[None] & (sub_k >= 0)
    s = jnp.where(mask[None, :, None, None], s, -jnp.inf)
    m = jnp.max(s, axis=-1)
    e = jnp.exp(s - m[..., None])
    den = jnp.sum(e, axis=-1)
    num = jnp.einsum('bnrhij,bnjrhd->bnrhid', e, vv)
    m = jnp.transpose(m, (0, 1, 4, 2, 3)).reshape(b, lp, h)[:, :l]
    den = jnp.transpose(den, (0, 1, 4, 2, 3)).reshape(b, lp, h)[:, :l]
    num = jnp.transpose(num, (0, 1, 4, 2, 3, 5)).reshape(b, lp, h, dh)[:, :l]
    return m, den, num


def _dilated_gather_stats(q, k_all, v_all, qpos, kpos0, d):
    kp = qpos[:, None] - jnp.arange(DIL_SPAN + 1)[None, :] * d
    idx = kp - kpos0
    valid = idx >= 0
    idxc = jnp.clip(idx, 0)
    kg, vg = k_all[:, idxc], v_all[:, idxc]
    s = jnp.einsum('bshd,bsmhd->bshm', q, kg) * HEAD_DIM ** -0.5
    s = jnp.where(valid[None, :, None, :], s, -jnp.inf)
    m = jnp.max(s, axis=-1)
    e = jnp.exp(s - m[..., None])
    return m, jnp.sum(e, axis=-1), jnp.einsum('bshm,bsmhd->bshd', e, vg)


def _combine_by_denominators(stats):
    m_all = stats[0][0]
    for m, _, _ in stats[1:]:
        m_all = jnp.maximum(m_all, m)
    num, den = None, None
    for m, dn, nm in stats:
        w = jnp.exp(m - m_all)
        num = w[..., None] * nm if num is None else num + w[..., None] * nm
        den = w * dn if den is None else den + w * dn
    return num / den[..., None]


def _dil_prompt(x, w_in):
    b, l, _ = x.shape
    q, k, v = [t.reshape(b, l, DIL_HEADS, HEAD_DIM) for t in jnp.split(_proj(x, w_in), 3, axis=-1)]
    pos = jnp.arange(l)
    qr, kr, vf = _rope(q, pos), _rope(k, pos), v.astype(jnp.float32)
    flat = lambda t: t.reshape(l, DIL_W)
    o = _combine_by_denominators([_dil_band_stats(flat(qr), flat(kr), flat(vf), d) for _, d in DIL_GROUPS])
    keep = min(DIL_MAX_WINDOW, l)
    buf = jnp.stack([kr[:, l - keep:], vf[:, l - keep:]], axis=2)
    return o.reshape(b, l, DIL_W), buf


def _dil_sample(x, buf, past, w_in):
    db, s, _ = x.shape
    q, k, v = [t.reshape(db, s, DIL_HEADS, HEAD_DIM) for t in jnp.split(_proj(x, w_in), 3, axis=-1)]
    assert s == 1 and buf.shape[1] == DIL_MAX_WINDOW <= past
    qpos = past + jnp.arange(s)
    qr, kr = _rope(q, qpos), _rope(k, qpos)
    o, new_buf = _dil_sample_attention(qr[:, 0], kr.reshape(db, DIL_W), v.reshape(db, DIL_W).astype(jnp.float32), buf)
    return o[:, None], new_buf


def kernel(x_prompt, x_sample, cache_nsa_cmp_kv, cache_nsa_slc_kv, state_nsa_win_kv, state_gdn_conv,
           state_gdn_S, state_dil_kv, state_ffn_conv, page_table, w_in_a, nsa_cmp_w1, nsa_cmp_b1, nsa_cmp_w2,
           nsa_cmp_pe, gdn_conv_w, gdn_A_log, gdn_dt_bias, gdn_norm_w, w_out_a, w_in_c, w_out_c,
           ln_mix_g, ln_mix_b, ffn_w_in, ffn_conv_w, ffn_conv_b, ffn_w_out, ln_ffn_g, ln_ffn_b):
    past = page_table.shape[1] * PAGE_SIZE
    bp, lp, d = x_prompt.shape
    bs, ls, _ = x_sample.shape
    assert bp == 1 and ls == 1
    xp, xs = x_prompt, x_sample
    cmp_p, cmp_s, slc_p, slc_s, win_p, win_s = [], [], [], [], [], []
    gconv_p, gconv_s, gstate_p, gstate_s = [], [], [], []
    dil_p, dil_s, ffn_p, ffn_s = [], [], [], []
    for layer in range(DEPTH):
        if layer % 2 == 0:
            la = layer // 2
            wa = (w_in_a[la], nsa_cmp_w1[la], nsa_cmp_b1[la], nsa_cmp_w2[la], nsa_cmp_pe[la],
                  gdn_conv_w[la], gdn_A_log[la], gdn_dt_bias[la], gdn_norm_w[la])
            mp, rc, rs, rw, hc, hs_ = _even_prompt(xp, *wa)
            cmp_p.append(rc); slc_p.append(rs); win_p.append(rw); gconv_p.append(hc); gstate_p.append(hs_)
            ms, rc, rs, rw, hc, hs_ = _even_sample(xs, cache_nsa_cmp_kv, cache_nsa_slc_kv, la,
                                                   state_nsa_win_kv[:, la], state_gdn_conv[:, la],
                                                   state_gdn_S[:, la], page_table, *wa)
            cmp_s.append(rc); slc_s.append(rs); win_s.append(rw); gconv_s.append(hc); gstate_s.append(hs_)
            w_out = w_out_a[la]
        else:
            lc = layer // 2
            mp, bpf = _dil_prompt(xp, w_in_c[lc])
            ms, bsf = _dil_sample(xs, state_dil_kv[:, lc], past, w_in_c[lc])
            dil_p.append(bpf); dil_s.append(bsf)
            w_out = w_out_c[lc]
        xp2 = _matmul_ln(mp.reshape(lp, -1), w_out, xp.reshape(lp, d), ln_mix_g[layer], ln_mix_b[layer])
        xs2 = _matmul_ln(ms.reshape(bs, -1), w_out, xs.reshape(bs, d), ln_mix_g[layer], ln_mix_b[layer])
        fargs = (ffn_w_in[layer], ffn_conv_w[layer], ffn_conv_b[layer], ffn_w_out[layer],
                 ln_ffn_g[layer], ln_ffn_b[layer])
        xp3, hp = _ffn_seq(xp2, *fargs)
        xs3, hs = _ffn_step(xs2, state_ffn_conv[:, layer], *fargs)
        xp, xs = xp3.reshape(1, lp, d), xs3.reshape(bs, 1, d)
        ffn_p.append(hp[None]); ffn_s.append(hs)

    def stk(lst):
        return jnp.stack(lst, axis=1)

    return (xp, xs, stk(cmp_p), stk(cmp_s), stk(slc_p), stk(slc_s), stk(win_p), stk(win_s),
            stk(gconv_p), stk(gconv_s), stk(gstate_p), stk(gstate_s), stk(dil_p), stk(dil_s),
            stk(ffn_p), stk(ffn_s))
```

```python
import functools
import math

import jax
import jax.numpy as jnp
from jax import lax
from jax.experimental import pallas as pl
from jax.experimental.pallas import tpu as pltpu
import numpy as np

D_MODEL = 1024
DEPTH = 2
PAGE_SIZE = 128
HEAD_DIM = 64
ROPE_THETA = 10000.0
NSA_HEADS = 8
NSA_KV_HEADS = 2
NSA_GROUP = NSA_HEADS // NSA_KV_HEADS
CMP_BLOCK = 32
SEL_BLOCK = 64
CMP_PER_SEL = SEL_BLOCK // CMP_BLOCK
NSA_TOPN = 16
NSA_WINDOW = 512
NSA_QBLOCK = 128
NSA_FORCE = 1.0e4
GDN_HEADS = 8
GDN_CONV = 4
GDN_CHUNK = 64
DIL_HEADS = 16
DIL_GROUPS = ((128, 1), (512, 4), (2048, 16))
DIL_SPAN = 128
DIL_BLOCK = 128
DIL_MAX_WINDOW = 2048
D_FF = 2816
FFN_CONV = 3
DEEPNORM_ALPHA = (2.0 * DEPTH) ** 0.25
LN_EPS = 1e-5
NORM_EPS = 1e-6
NSA_Q_W = NSA_HEADS * HEAD_DIM
NSA_KV_W = NSA_KV_HEADS * HEAD_DIM
GDN_W = GDN_HEADS * HEAD_DIM
DIL_W = DIL_HEADS * HEAD_DIM

LANES = 128
VMEM_LIMIT_BYTES = 56 * 1024 * 1024


def _layer_norm_rows(r, g, b):
    mu = jnp.mean(r, axis=-1, keepdims=True)
    d = r - mu
    var = jnp.mean(d * d, axis=-1, keepdims=True)
    return d * lax.rsqrt(var + LN_EPS) * g + b


def _mm_kernel(x_ref, w_ref, o_ref):
    o_ref[...] = jnp.dot(x_ref[...].astype(jnp.bfloat16), w_ref[...], preferred_element_type=jnp.float32)


def _mm_ln_kernel(x_ref, w_ref, res_ref, g_ref, b_ref, o_ref):
    acc = jnp.dot(x_ref[...].astype(jnp.bfloat16), w_ref[...], preferred_element_type=jnp.float32)
    o_ref[...] = _layer_norm_rows(DEEPNORM_ALPHA * res_ref[...] + acc, g_ref[...], b_ref[...])


def _row_tile(m):
    return 512 if m % 512 == 0 else m


def _matmul(x, w):
    m, k = x.shape
    n = w.shape[1]
    tm = _row_tile(m)
    tn = n
    for cand in (1152, 1024, 768, 512):
        if n % cand == 0:
            tn = cand
            break
    return pl.pallas_call(
        _mm_kernel,
        grid=(m // tm, n // tn),
        in_specs=[pl.BlockSpec((tm, k), lambda i, j: (i, 0)),
                  pl.BlockSpec((k, tn), lambda i, j: (0, j))],
        out_specs=pl.BlockSpec((tm, tn), lambda i, j: (i, j)),
        out_shape=jax.ShapeDtypeStruct((m, n), jnp.float32),
        compiler_params=pltpu.CompilerParams(dimension_semantics=("parallel", "arbitrary"),
                                             vmem_limit_bytes=VMEM_LIMIT_BYTES),
        name="matmul",
    )(x, w.astype(jnp.bfloat16))


def _matmul_ln(x, w, res, g, b):
    m, k = x.shape
    n = w.shape[1]
    tm = _row_tile(m)
    return pl.pallas_call(
        _mm_ln_kernel,
        grid=(m // tm,),
        in_specs=[pl.BlockSpec((tm, k), lambda i: (i, 0)),
                  pl.BlockSpec((k, n), lambda i: (0, 0)),
                  pl.BlockSpec((tm, n), lambda i: (i, 0)),
                  pl.BlockSpec((1, n), lambda i: (0, 0)),
                  pl.BlockSpec((1, n), lambda i: (0, 0))],
        out_specs=pl.BlockSpec((tm, n), lambda i: (i, 0)),
        out_shape=jax.ShapeDtypeStruct((m, n), jnp.float32),
        compiler_params=pltpu.CompilerParams(dimension_semantics=("arbitrary",),
                                             vmem_limit_bytes=VMEM_LIMIT_BYTES),
        name="matmul_ln",
    )(x, w.astype(jnp.bfloat16), res, g.reshape(1, n), b.reshape(1, n))


FFN_CHUNK = D_FF // 2
FFN_NCHUNK = D_FF // FFN_CHUNK


def _ffn_seq_kernel(x_ref, wa_ref, wg_ref, cwa_ref, cwg_ref, cba_ref, cbg_ref, wo_ref, lg_ref, lb_ref,
                    y_ref, ha_ref, hg_ref, acc_ref, carry_ref):
    i, j = pl.program_id(0), pl.program_id(1)
    tm = x_ref.shape[0]
    x = x_ref[...]
    xb = x.astype(jnp.bfloat16)

    @pl.when(i == 0)
    def _():
        carry_ref[j] = jnp.zeros(carry_ref.shape[1:], jnp.float32)

    def conv(u, cw_ref, cb_ref, slot):
        prev = carry_ref[j, slot]
        p2, p1 = prev[6:7], prev[7:8]
        row = lax.broadcasted_iota(jnp.int32, u.shape, 0)
        u1 = jnp.where(row == 0, p1, pltpu.roll(u, 1, 0))
        u2 = jnp.where(row == 0, p2, jnp.where(row == 1, p1, pltpu.roll(u, 2, 0)))
        carry_ref[j, slot] = u[tm - 8:]
        cw = cw_ref[...]
        return cw[0:1] * u2 + cw[1:2] * u1 + cw[2:3] * u + cb_ref[...]

    ua = jnp.dot(xb, wa_ref[...], preferred_element_type=jnp.float32)
    ug = jnp.dot(xb, wg_ref[...], preferred_element_type=jnp.float32)
    ha_ref[...] = ua[tm - 8:]
    hg_ref[...] = ug[tm - 8:]
    a = conv(ua, cwa_ref, cba_ref, 0)
    g = conv(ug, cwg_ref, cbg_ref, 1)
    h = (a * jax.nn.sigmoid(a) * g).astype(jnp.bfloat16)
    part = jnp.dot(h, wo_ref[...], preferred_element_type=jnp.float32)

    @pl.when(j == 0)
    def _():
        acc_ref[...] = part

    @pl.when(j > 0)
    def _():
        acc_ref[...] += part

    @pl.when(j == pl.num_programs(1) - 1)
    def _():
        y_ref[...] = _layer_norm_rows(DEEPNORM_ALPHA * x + acc_ref[...], lg_ref[...], lb_ref[...])


def _ffn_seq(x, w_in, conv_w, conv_b, w_out, ln_g, ln_b):
    l, d = x.shape
    tm = 512
    c, nc = FFN_CHUNK, FFN_NCHUNK
    w_in = w_in.astype(jnp.bfloat16)
    cw8 = jnp.zeros((8, 2 * D_FF), jnp.float32).at[:FFN_CONV].set(conv_w)
    cb = conv_b.reshape(1, 2 * D_FF)
    y, ha, hg = pl.pallas_call(
        _ffn_seq_kernel,
        grid=(l // tm, nc),
        in_specs=[pl.BlockSpec((tm, d), lambda i, j: (i, 0)),
                  pl.BlockSpec((d, c), lambda i, j: (0, j)),
                  pl.BlockSpec((d, c), lambda i, j: (0, j + nc)),
                  pl.BlockSpec((8, c), lambda i, j: (0, j)),
                  pl.BlockSpec((8, c), lambda i, j: (0, j + nc)),
                  pl.BlockSpec((1, c), lambda i, j: (0, j)),
                  pl.BlockSpec((1, c), lambda i, j: (0, j + nc)),
                  pl.BlockSpec((c, d), lambda i, j: (j, 0)),
                  pl.BlockSpec((1, d), lambda i, j: (0, 0)),
                  pl.BlockSpec((1, d), lambda i, j: (0, 0))],
        out_specs=[pl.BlockSpec((tm, d), lambda i, j: (i, 0)),
                   pl.BlockSpec((8, c), lambda i, j: (i, j)),
                   pl.BlockSpec((8, c), lambda i, j: (i, j))],
        out_shape=[jax.ShapeDtypeStruct((l, d), jnp.float32),
                   jax.ShapeDtypeStruct((l // tm * 8, D_FF), jnp.float32),
                   jax.ShapeDtypeStruct((l // tm * 8, D_FF), jnp.float32)],
        scratch_shapes=[pltpu.VMEM((tm, d), jnp.float32),
                        pltpu.VMEM((nc, 2, 8, c), jnp.float32)],
        compiler_params=pltpu.CompilerParams(dimension_semantics=("arbitrary", "arbitrary"),
                                             vmem_limit_bytes=VMEM_LIMIT_BYTES),
        name="ffn_seq",
    )(x, w_in, w_in, cw8, cw8, cb, cb, w_out.astype(jnp.bfloat16), ln_g.reshape(1, d), ln_b.reshape(1, d))
    hist = jnp.concatenate([ha[-(FFN_CONV - 1):], hg[-(FFN_CONV - 1):]], axis=-1)
    return y, hist


def _ffn_step_kernel(x_ref, h_ref, wa_ref, wg_ref, cwa_ref, cwg_ref, cba_ref, cbg_ref, wo_ref, lg_ref, lb_ref,
                     y_ref, ua_ref, ug_ref, acc_ref):
    j = pl.program_id(0)
    x = x_ref[...]
    xb = x.astype(jnp.bfloat16)
    ua = jnp.dot(xb, wa_ref[...], preferred_element_type=jnp.float32)
    ug = jnp.dot(xb, wg_ref[...], preferred_element_type=jnp.float32)
    ua_ref[...] = ua
    ug_ref[...] = ug
    cwa, cwg = cwa_ref[...], cwg_ref[...]
    a = cwa[0:1] * h_ref[0, 0] + cwa[1:2] * h_ref[1, 0] + cwa[2:3] * ua + cba_ref[...]
    g = cwg[0:1] * h_ref[0, 1] + cwg[1:2] * h_ref[1, 1] + cwg[2:3] * ug + cbg_ref[...]
    h = (a * jax.nn.sigmoid(a) * g).astype(jnp.bfloat16)
    part = jnp.dot(h, wo_ref[...], preferred_element_type=jnp.float32)

    @pl.when(j == 0)
    def _():
        acc_ref[...] = part

    @pl.when(j > 0)
    def _():
        acc_ref[...] += part

    @pl.when(j == pl.num_programs(0) - 1)
    def _():
        y_ref[...] = _layer_norm_rows(DEEPNORM_ALPHA * x + acc_ref[...], lg_ref[...], lb_ref[...])


def _ffn_step(x, hist, w_in, conv_w, conv_b, w_out, ln_g, ln_b):
    b, d = x.shape
    c, nc = FFN_CHUNK, FFN_NCHUNK
    w_in = w_in.astype(jnp.bfloat16)
    cw8 = jnp.zeros((8, 2 * D_FF), jnp.float32).at[:FFN_CONV].set(conv_w)
    cb = conv_b.reshape(1, 2 * D_FF)
    h4 = jnp.transpose(hist, (1, 0, 2)).reshape(2, b, 2, D_FF).transpose(0, 2, 1, 3)
    y, ua, ug = pl.pallas_call(
        _ffn_step_kernel,
        grid=(nc,),
        in_specs=[pl.BlockSpec((b, d), lambda j: (0, 0)),
                  pl.BlockSpec((2, 2, b, c), lambda j: (0, 0, 0, j)),
                  pl.BlockSpec((d, c), lambda j: (0, j)),
                  pl.BlockSpec((d, c), lambda j: (0, j + nc)),
                  pl.BlockSpec((8, c), lambda j: (0, j)),
                  pl.BlockSpec((8, c), lambda j: (0, j + nc)),
                  pl.BlockSpec((1, c), lambda j: (0, j)),
                  pl.BlockSpec((1, c), lambda j: (0, j + nc)),
                  pl.BlockSpec((c, d), lambda j: (j, 0)),
                  pl.BlockSpec((1, d), lambda j: (0, 0)),
                  pl.BlockSpec((1, d), lambda j: (0, 0))],
        out_specs=[pl.BlockSpec((b, d), lambda j: (0, 0)),
                   pl.BlockSpec((b, c), lambda j: (0, j)),
                   pl.BlockSpec((b, c), lambda j: (0, j))],
        out_shape=[jax.ShapeDtypeStruct((b, d), jnp.float32),
                   jax.ShapeDtypeStruct((b, D_FF), jnp.float32),
                   jax.ShapeDtypeStruct((b, D_FF), jnp.float32)],
        scratch_shapes=[pltpu.VMEM((b, d), jnp.float32)],
        compiler_params=pltpu.CompilerParams(dimension_semantics=("arbitrary",),
                                             vmem_limit_bytes=VMEM_LIMIT_BYTES),
        name="ffn_step",
    )(x, h4, w_in, w_in, cw8, cw8, cb, cb, w_out.astype(jnp.bfloat16), ln_g.reshape(1, d), ln_b.reshape(1, d))
    u = jnp.concatenate([ua, ug], axis=-1)
    return y, jnp.concatenate([hist[:, 1:], u[:, None]], axis=1)


NEG = -1e30
NSA_KT = 512
NSA_COLS = NSA_HEADS * NSA_QBLOCK
NSA_WSPAN = NSA_WINDOW + NSA_QBLOCK
LOG2E = 1.4426950408889634
NSA_VROWS = HEAD_DIM + 8


def _lane_tile(x, n):
    return jnp.concatenate([x] * n, axis=1)


def _nsa_prompt_kernel(q_ref, sm_ref, ck_ref, cvt_ref, ks_ref, vst_ref, kw_ref, vwt_ref, hot_ref, o_ref,
                       selb_ref, m_ref, acc_ref, *, ns):
    f32, bf16 = jnp.float32, jnp.bfloat16
    qb = NSA_QBLOCK
    i = pl.program_id(0)
    s0 = i * qb
    half = NSA_COLS // 2

    qt = (q_ref[...] * (HEAD_DIM ** -0.5 * LOG2E)).T
    zero = jnp.zeros((HEAD_DIM, qb), f32)
    top = jnp.concatenate([qt[h * HEAD_DIM:(h + 1) * HEAD_DIM] for h in range(NSA_GROUP)] + [zero] * NSA_GROUP, axis=1)
    bot = jnp.concatenate([zero] * NSA_GROUP + [qt[h * HEAD_DIM:(h + 1) * HEAD_DIM]
                                                for h in range(NSA_GROUP, NSA_HEADS)], axis=1)
    qbd = jnp.concatenate([top, bot], axis=0).astype(bf16)

    def pv(vt, p):
        pb = p.astype(bf16)
        rows = vt.shape[0] // NSA_KV_HEADS
        return [jnp.dot(vt[g * rows:(g + 1) * rows], pb[:, g * half:(g + 1) * half],
                        preferred_element_type=f32) for g in range(NSA_KV_HEADS)]

    nc = 2 * ns
    r = lax.broadcasted_iota(jnp.int32, (nc, qb), 0)
    lane = lax.broadcasted_iota(jnp.int32, (nc, qb), 1)
    cidx = jnp.where(r < ns, 2 * r, 2 * (r - ns) + 1)
    cbias = jnp.where((cidx + 1) * CMP_BLOCK - 1 <= s0 + lane, 0.0, NEG)
    sc = jnp.dot(ck_ref[...], qbd, preferred_element_type=f32) + _lane_tile(cbias, NSA_HEADS)
    m = jnp.max(sc, axis=0, keepdims=True)
    p = jnp.exp2(sc - m)
    pn = p * jnp.where(m > 0.5 * NEG, 1.0 / jnp.sum(p, axis=0, keepdims=True), 0.0)
    o_cmp = pv(cvt_ref[...], pn)

    blk = lax.broadcasted_iota(jnp.int32, (ns, qb), 0)
    qpos = s0 + lax.broadcasted_iota(jnp.int32, (ns, qb), 1)
    cur = qpos // SEL_BLOCK
    forced = (blk == 0) | (blk == cur) | (blk == cur - 1)
    for g in range(NSA_KV_HEADS):
        imp = pn[:, g * half:g * half + qb]
        for h in range(1, NSA_GROUP):
            imp = imp + pn[:, g * half + h * qb:g * half + (h + 1) * qb]
        imp = imp[:ns] + imp[ns:]
        val = jnp.where(blk > cur, -1.0, jnp.where(forced, NSA_FORCE, imp))
        bias = jnp.full((ns, qb), NEG, f32)
        for _ in range(min(NSA_TOPN, ns)):
            top = jnp.max(val, axis=0, keepdims=True)
            pick = jnp.min(jnp.where(val == top, blk, ns), axis=0, keepdims=True)
            hit = blk == pick
            bias = jnp.where(hit, 0.0, bias)
            val = jnp.where(hit, -jnp.inf, val)
        selb_ref[g] = bias

    m_ref[...] = jnp.full(m_ref.shape, NEG, f32)
    acc_ref[...] = jnp.zeros(acc_ref.shape, f32)
    per_tile = NSA_KT // SEL_BLOCK
    zpad = jnp.zeros((LANES - 16, NSA_COLS), bf16)

    def slc_tile(kt, causal):
        k0 = pl.multiple_of(kt * NSA_KT, NSA_KT)
        b0 = pl.multiple_of(kt * per_tile, per_tile)
        brow = jnp.concatenate([selb_ref[g, pl.ds(b0, per_tile), :] for g in range(NSA_KV_HEADS)
                                for _ in range(NSA_GROUP)], axis=1)
        brow = jnp.concatenate([brow, jnp.zeros((16 - per_tile, NSA_COLS), f32)], axis=0).astype(bf16)
        q_aug = jnp.concatenate([qbd, brow, zpad], axis=0)
        k_aug = jnp.concatenate([ks_ref[pl.ds(k0, NSA_KT), :], hot_ref[...]], axis=1)
        s = jnp.dot(k_aug, q_aug, preferred_element_type=f32)
        if causal:
            kpos = k0 + lax.broadcasted_iota(jnp.int32, (NSA_KT, qb), 0)
            qq = s0 + lax.broadcasted_iota(jnp.int32, (NSA_KT, qb), 1)
            s = s + _lane_tile(jnp.where(kpos <= qq, 0.0, NEG), NSA_HEADS)
        m_old = m_ref[...]
        m_new = jnp.maximum(m_old, jnp.max(s, axis=0, keepdims=True))
        alpha = jnp.exp2(m_old - m_new)
        p = jnp.exp2(s - m_new)
        m_ref[...] = m_new
        upd = pv(vst_ref[:, pl.ds(k0, NSA_KT)], p)
        for g in range(NSA_KV_HEADS):
            acc_ref[g] = acc_ref[g] * alpha[:, g * half:(g + 1) * half] + upd[g]

    kd = s0 // NSA_KT

    def body(j, carry):
        slc_tile(2 * j, False)
        slc_tile(2 * j + 1, False)
        return carry

    lax.fori_loop(0, kd // 2, body, 0)

    @pl.when(kd % 2 == 1)
    def _():
        slc_tile(kd - 1, False)

    slc_tile(kd, True)
    inv_slc = [1.0 / acc_ref[g, HEAD_DIM:HEAD_DIM + 1, :] for g in range(NSA_KV_HEADS)]

    w0 = pl.multiple_of(s0, qb)
    sw = jnp.dot(kw_ref[pl.ds(w0, NSA_WSPAN), :], qbd, preferred_element_type=f32)
    rr = lax.broadcasted_iota(jnp.int32, (NSA_WSPAN, qb), 0)
    qi = lax.broadcasted_iota(jnp.int32, (NSA_WSPAN, qb), 1)
    ok = (rr >= qi) & (rr <= qi + NSA_WINDOW) & (rr + s0 >= NSA_WINDOW)
    sw = sw + _lane_tile(jnp.where(ok, 0.0, NEG), NSA_HEADS)
    pw = jnp.exp2(sw - jnp.max(sw, axis=0, keepdims=True))
    o_win = pv(vwt_ref[:, pl.ds(w0, NSA_WSPAN)], pw)
    inv_win = [1.0 / o[HEAD_DIM:HEAD_DIM + 1] for o in o_win]

    gt = jax.nn.sigmoid(sm_ref[...].T)
    outs = []
    for h in range(NSA_HEADS):
        g, hg = divmod(h, NSA_GROUP)
        c0, c1 = hg * qb, (hg + 1) * qb
        g_cmp = gt[h:h + 1]
        g_slc = gt[NSA_HEADS + h:NSA_HEADS + h + 1] * inv_slc[g][:, c0:c1]
        g_win = gt[2 * NSA_HEADS + h:2 * NSA_HEADS + h + 1] * inv_win[g][:, c0:c1]
        outs.append(o_cmp[g][:, c0:c1] * g_cmp + acc_ref[g, :HEAD_DIM, c0:c1] * g_slc
                    + o_win[g][:HEAD_DIM, c0:c1] * g_win)
    o_ref[...] = jnp.concatenate(outs, axis=0).T


def _nsa_prompt_attention(qr, small, ck, cv, ksr, vs, kwr, vw):
    l = qr.shape[0]
    ns = l // SEL_BLOCK
    nc = 2 * ns
    bf16 = jnp.bfloat16
    perm = jnp.concatenate([jnp.arange(0, nc, 2), jnp.arange(1, nc, 2)])
    ckp = ck[perm].astype(bf16)
    cvt = cv[perm].T.astype(bf16)
    pad = jnp.zeros((NSA_WINDOW, NSA_KV_W), bf16)
    kwp = jnp.concatenate([pad, kwr.astype(bf16)], axis=0)
    def with_ones(vt):
        n = vt.shape[1]
        extra = jnp.concatenate([jnp.ones((1, n), bf16), jnp.zeros((NSA_VROWS - HEAD_DIM - 1, n), bf16)], axis=0)
        return jnp.concatenate([x for g in range(NSA_KV_HEADS) for x in (vt[g * HEAD_DIM:(g + 1) * HEAD_DIM], extra)], axis=0)

    vwt = with_ones(jnp.concatenate([pad, vw.astype(bf16)], axis=0).T)
    hot = (jnp.arange(NSA_KT)[:, None] // SEL_BLOCK == jnp.arange(LANES)[None, :]).astype(bf16)
    full = lambda a: pl.BlockSpec(a.shape, lambda i: (0,) * a.ndim)
    args = (qr, small, ckp, cvt, ksr.astype(bf16), with_ones(vs.T.astype(bf16)), kwp, vwt, hot)
    return pl.pallas_call(
        functools.partial(_nsa_prompt_kernel, ns=ns),
        grid=(l // NSA_QBLOCK,),
        in_specs=[pl.BlockSpec((NSA_QBLOCK, NSA_Q_W), lambda i: (i, 0)),
                  pl.BlockSpec((NSA_QBLOCK, LANES), lambda i: (i, 0))] + [full(a) for a in args[2:]],
        out_specs=pl.BlockSpec((NSA_QBLOCK, NSA_Q_W), lambda i: (i, 0)),
        out_shape=jax.ShapeDtypeStruct((l, NSA_Q_W), jnp.float32),
        scratch_shapes=[pltpu.VMEM((NSA_KV_HEADS, ns, NSA_QBLOCK), jnp.float32),
                        pltpu.VMEM((1, NSA_COLS), jnp.float32),
                        pltpu.VMEM((NSA_KV_HEADS, NSA_VROWS, NSA_COLS // 2), jnp.float32)],
        compiler_params=pltpu.CompilerParams(dimension_semantics=("arbitrary",),
                                             vmem_limit_bytes=VMEM_LIMIT_BYTES),
        name="nsa_prompt",
    )(*args)


PAGE_W = 2 * NSA_KV_W
CMP_HIDDEN = 2 * HEAD_DIM
CMP_PER_PAGE = PAGE_SIZE // CMP_BLOCK
CMP_FLAT = CMP_BLOCK * PAGE_W
CMP_PAGES_PER_STEP = 32
STEP_PAGES = 64
STEP_ROWS = 16


def _rope_tables(pos, width):
    half = HEAD_DIM // 2
    inv_freq = ROPE_THETA ** (-2.0 * jnp.arange(half, dtype=jnp.float32) / HEAD_DIM)
    ang = pos.astype(jnp.float32)[:, None] * inv_freq[None, :]
    cos, sin = jnp.cos(ang), jnp.sin(ang)
    reps = width // HEAD_DIM
    return (jnp.tile(jnp.concatenate([cos, cos], axis=1), (1, reps)),
            jnp.tile(jnp.concatenate([-sin, sin], axis=1), (1, reps)))


def _rope_lanes(x, cos, sin_signed):
    n = x.shape[-1]
    lane = lax.broadcasted_iota(jnp.int32, x.shape, x.ndim - 1)
    first = (lane % HEAD_DIM) < HEAD_DIM // 2
    partner = jnp.where(first, pltpu.roll(x, n - HEAD_DIM // 2, x.ndim - 1), pltpu.roll(x, HEAD_DIM // 2, x.ndim - 1))
    return x * cos + partner * sin_signed


def _cmp_step_kernel(pt_ref, *refs):
    npg = CMP_PAGES_PER_STEP
    pages = refs[:npg]
    pe_ref, w1_ref, b1_ref, w2_ref, cos_ref, sin_ref, ck_ref, cv_ref = refs[npg:]
    x = jnp.concatenate([r[0] for r in pages], axis=0) + pe_ref[...]
    h = jnp.dot(x.astype(jnp.bfloat16), w1_ref[...], preferred_element_type=jnp.float32) + b1_ref[...]
    c = jnp.dot(jax.nn.gelu(h).astype(jnp.bfloat16), w2_ref[...], preferred_element_type=jnp.float32)
    ck_ref[0] = _rope_lanes(c[:, :NSA_KV_W], cos_ref[...], sin_ref[...])
    cv_ref[0] = c[:, NSA_KV_W:]


def _compress_weights(cw1, cb1, cw2, cpe):
    eye = jnp.eye(2, dtype=jnp.float32)
    w1r = cw1.reshape(2, CMP_BLOCK, HEAD_DIM, CMP_HIDDEN)
    w1 = jnp.einsum('ktdj,ka,gb->tkgdabj', w1r, eye, eye).reshape(CMP_FLAT, 4 * CMP_HIDDEN)
    b1 = jnp.broadcast_to(cb1[:, None, :], (2, 2, CMP_HIDDEN)).reshape(1, 4 * CMP_HIDDEN)
    w2 = jnp.einsum('kjd,ka,gb->kgjabd', cw2, eye, eye).reshape(4 * CMP_HIDDEN, PAGE_W)
    pe = jnp.broadcast_to(cpe.transpose(1, 0, 2)[:, :, None, :], (CMP_BLOCK, 2, 2, HEAD_DIM)).reshape(1, CMP_FLAT)
    return w1.astype(jnp.bfloat16), b1, w2.astype(jnp.bfloat16), pe


def _nsa_sample_compress(pool, layer_idx, page_table, cw1, cb1, cw2, cpe):
    n_pool, nl = pool.shape[:2]
    db, n_pages = page_table.shape
    npg = CMP_PAGES_PER_STEP
    nchunk = n_pages // npg
    nc = n_pages * CMP_PER_PAGE
    view = pool.reshape(n_pool * nl, CMP_PER_PAGE, CMP_FLAT)
    pt = (page_table * nl + layer_idx).reshape(-1).astype(jnp.int32)
    w1, b1, w2, pe = _compress_weights(cw1, cb1, cw2, cpe)
    cos, sin = _rope_tables((jnp.arange(nc) + 1) * CMP_BLOCK - 1, NSA_KV_W)
    rows = npg * CMP_PER_PAGE

    def page_map(k):
        return lambda b, c, pt_ref: (pt_ref[b * n_pages + c * npg + k], 0, 0)

    const = lambda a: pl.BlockSpec(a.shape, lambda b, c, pt_ref: (0,) * a.ndim, pipeline_mode=pl.Buffered(1))
    grid_spec = pltpu.PrefetchScalarGridSpec(
        num_scalar_prefetch=1, grid=(db, nchunk),
        in_specs=[pl.BlockSpec((1, CMP_PER_PAGE, CMP_FLAT), page_map(k)) for k in range(npg)]
        + [const(pe), const(w1), const(b1), const(w2),
           pl.BlockSpec((rows, NSA_KV_W), lambda b, c, pt_ref: (c, 0)),
           pl.BlockSpec((rows, NSA_KV_W), lambda b, c, pt_ref: (c, 0))],
        out_specs=[pl.BlockSpec((1, rows, NSA_KV_W), lambda b, c, pt_ref: (b, c, 0))] * 2)
    return pl.pallas_call(
        _cmp_step_kernel, grid_spec=grid_spec,
        out_shape=[jax.ShapeDtypeStruct((db, nc, NSA_KV_W), jnp.float32)] * 2,
        compiler_params=pltpu.CompilerParams(dimension_semantics=("arbitrary", "arbitrary"),
                                             vmem_limit_bytes=VMEM_LIMIT_BYTES),
        name="nsa_sample_compress",
    )(pt, *([view] * npg), pe, w1, b1, w2, cos, sin)


def _nt_dot(a, b):
    return lax.dot_general(a, b, (((1,), (1,)), ((), ())), preferred_element_type=jnp.float32)


def _nsa_step_kernel(pt_ref, *refs, cur, nsl):
    f32, bf16 = jnp.float32, jnp.bfloat16
    npg = STEP_PAGES
    q_ref, ck_ref, cv_ref = refs[:3]
    pages = refs[3:3 + npg]
    (win_ref, new_ref, gate_ref, exp_ref, triu_ref, o_ref, wout_ref,
     selt_ref, m_ref, l_ref, acc_ref, side_ref) = refs[3 + npg:]
    cc = pl.program_id(1)
    q16 = q_ref[0]
    qb = q16.astype(bf16)
    row16 = lax.broadcasted_iota(jnp.int32, (STEP_ROWS, 1), 0)

    def new_key_scores(krow):
        return jnp.sum(q16 * krow, axis=1, keepdims=True)

    @pl.when(cc == 0)
    def _():
        nch = ck_ref.shape[1] // 2
        halves = lambda r: jnp.concatenate([r[0, pl.ds(0, nch, stride=2), :], r[0, pl.ds(1, nch, stride=2), :]], axis=0)
        ck, cv = halves(ck_ref), halves(cv_ref)
        s = _nt_dot(qb, ck.astype(bf16))
        p = jnp.exp(s - jnp.max(s, axis=1, keepdims=True))
        pn = p / jnp.sum(p, axis=1, keepdims=True)
        o_cmp = jnp.dot(pn.astype(bf16), cv.astype(bf16), preferred_element_type=f32)

        rowp = lax.broadcasted_iota(jnp.int32, pn.shape, 0)
        row8 = lax.broadcasted_iota(jnp.int32, (8, nsl), 0)
        blk = lax.broadcasted_iota(jnp.int32, (8, nsl), 1)
        forced = (blk == 0) | (blk == cur) | (blk == cur - 1)
        key = jnp.full((8, nsl), -1, jnp.int32)
        for g in range(NSA_KV_HEADS):
            ig = jnp.sum(jnp.where((rowp >= g * NSA_GROUP) & (rowp < (g + 1) * NSA_GROUP), pn, 0.0),
                         axis=0, keepdims=True)
            ig = ig[:, :nch] + ig[:, nch:]
            ig = jnp.concatenate([ig, jnp.zeros((1, nsl - nch), f32)], axis=1)
            kg = jnp.where(blk[:1] > cur, -1,
                           jnp.where(forced[:1], lax.bitcast_convert_type(jnp.full((1, nsl), NSA_FORCE, f32), jnp.int32),
                                     lax.bitcast_convert_type(ig, jnp.int32)))
            key = jnp.where(row8 == g, kg, key)
        thr = jnp.zeros((8, 1), jnp.int32)
        for bit in range(30, -1, -1):
            cand = thr | (1 << bit)
            cnt = jnp.sum(jnp.where(key >= cand, 1.0, 0.0), axis=1, keepdims=True)
            thr = jnp.where(cnt >= NSA_TOPN, cand, thr)
        above = key > thr
        n_above = jnp.sum(jnp.where(above, 1.0, 0.0), axis=1, keepdims=True)
        tie = key == thr
        rank = jnp.dot(jnp.where(tie, 1.0, 0.0).astype(bf16), triu_ref[...], preferred_element_type=f32)
        sel = jnp.where(above | (tie & (rank <= NSA_TOPN - n_above)), 1.0, 0.0)
        selh = jnp.where(row16 < NSA_GROUP, sel[0:1], jnp.where(row16 < NSA_HEADS, sel[1:2], 0.0))
        for j in range(selt_ref.shape[0]):
            selt_ref[j] = selh[:, j * LANES:(j + 1) * LANES]

        m_ref[...] = new_key_scores(new_ref[0, 0:1, :])
        l_ref[...] = jnp.ones(l_ref.shape, f32)
        acc_ref[...] = jnp.broadcast_to(new_ref[0, 1:2, :], acc_ref.shape)

        win = win_ref[0]
        sw = _nt_dot(qb, win[:, :NSA_KV_W].astype(bf16))
        sn = new_key_scores(new_ref[0, 2:3, :])
        mw = jnp.maximum(jnp.max(sw, axis=1, keepdims=True), sn)
        pw, pnw = jnp.exp(sw - mw), jnp.exp(sn - mw)
        lw = jnp.sum(pw, axis=1, keepdims=True) + pnw
        o_win = (jnp.dot(pw.astype(bf16), win[:, NSA_KV_W:].astype(bf16), preferred_element_type=f32)
                 + pnw * new_ref[0, 3:4, :]) / lw
        gt = jax.nn.sigmoid(gate_ref[0])
        side_ref[...] = gt[:, 0:1] * o_cmp + gt[:, 2:3] * o_win
        rw = lax.broadcasted_iota(jnp.int32, win.shape, 0)
        newrow = jnp.concatenate([new_ref[0, 2:3, :], new_ref[0, 3:4, :]], axis=1)
        wout_ref[0] = jnp.where(rw == win.shape[0] - 1, newrow, pltpu.roll(win, win.shape[0] - 1, 0))

    kv = jnp.concatenate([r[0] for r in pages], axis=0)
    s = _nt_dot(qb, kv[:, :NSA_KV_W].astype(bf16))
    picked = jnp.dot(selt_ref[cc].astype(bf16), exp_ref[...], preferred_element_type=f32)
    s = s + (picked - 1.0) * (-NEG)
    m_old = m_ref[...]
    m_new = jnp.maximum(m_old, jnp.max(s, axis=1, keepdims=True))
    alpha = jnp.exp(m_old - m_new)
    p = jnp.exp(s - m_new)
    m_ref[...] = m_new
    l_ref[...] = l_ref[...] * alpha + jnp.sum(p, axis=1, keepdims=True)
    acc_ref[...] = acc_ref[...] * alpha + jnp.dot(p.astype(bf16), kv[:, NSA_KV_W:].astype(bf16),
                                                  preferred_element_type=f32)

    @pl.when(cc == pl.num_programs(1) - 1)
    def _():
        gt = jax.nn.sigmoid(gate_ref[0])
        o_ref[0] = side_ref[...] + gt[:, 1:2] * acc_ref[...] / l_ref[...]


def _nsa_sample_attention(qr, ck, cv, slc_pool, layer_idx, page_table, win_buf, newrows, gate):
    db, n_pages = page_table.shape
    n_pool, nl = slc_pool.shape[:2]
    past = n_pages * PAGE_SIZE
    cur = past // SEL_BLOCK
    nsl = -(-(cur + 1) // LANES) * LANES
    npg = STEP_PAGES
    nchunk = n_pages // npg
    keys = npg * PAGE_SIZE
    view = slc_pool.reshape(n_pool * nl, PAGE_SIZE, PAGE_W)
    pt = (page_table * nl + layer_idx).reshape(-1).astype(jnp.int32)
    f32 = jnp.float32
    hmask = (jnp.arange(NSA_HEADS)[:, None] // NSA_GROUP == jnp.arange(NSA_KV_HEADS)[None, :]).astype(f32)
    q16 = (qr * HEAD_DIM ** -0.5)[:, :, None, :] * hmask[None, :, :, None]
    q16 = jnp.pad(q16.reshape(db, NSA_HEADS, NSA_KV_W), ((0, 0), (0, STEP_ROWS - NSA_HEADS), (0, 0)))
    new8 = jnp.pad(newrows, ((0, 0), (0, 8 - newrows.shape[1]), (0, 0)))
    g16 = jnp.pad(gate.reshape(db, 3, NSA_HEADS).transpose(0, 2, 1),
                  ((0, 0), (0, STEP_ROWS - NSA_HEADS), (0, LANES - 3)))
    expand = (jnp.arange(LANES)[:, None] == jnp.arange(keys)[None, :] // SEL_BLOCK).astype(jnp.bfloat16)
    triu = (jnp.arange(nsl)[:, None] <= jnp.arange(nsl)[None, :]).astype(jnp.bfloat16)
    wlen = win_buf.shape[1]

    def page_map(k):
        return lambda b, c, pt_ref: (pt_ref[b * n_pages + c * npg + k], 0, 0)

    per_b = lambda shp: pl.BlockSpec((1,) + shp, lambda b, c, pt_ref: (b, 0, 0))
    const = lambda a: pl.BlockSpec(a.shape, lambda b, c, pt_ref: (0,) * a.ndim)
    grid_spec = pltpu.PrefetchScalarGridSpec(
        num_scalar_prefetch=1, grid=(db, nchunk),
        in_specs=[per_b((STEP_ROWS, NSA_KV_W)), per_b(ck.shape[1:]), per_b(cv.shape[1:])]
        + [pl.BlockSpec((1, PAGE_SIZE, PAGE_W), page_map(k)) for k in range(npg)]
        + [per_b((wlen, PAGE_W)), per_b((8, NSA_KV_W)), per_b((STEP_ROWS, LANES)), const(expand), const(triu)],
        out_specs=[per_b((STEP_ROWS, NSA_KV_W)), per_b((wlen, PAGE_W))],
        scratch_shapes=[pltpu.VMEM((nsl // LANES, STEP_ROWS, LANES), f32),
                        pltpu.VMEM((STEP_ROWS, 1), f32), pltpu.VMEM((STEP_ROWS, 1), f32),
                        pltpu.VMEM((STEP_ROWS, NSA_KV_W), f32), pltpu.VMEM((STEP_ROWS, NSA_KV_W), f32)])
    o16, wout = pl.pallas_call(
        functools.partial(_nsa_step_kernel, cur=cur, nsl=nsl), grid_spec=grid_spec,
        out_shape=[jax.ShapeDtypeStruct((db, STEP_ROWS, NSA_KV_W), f32),
                   jax.ShapeDtypeStruct((db, wlen, PAGE_W), f32)],
        compiler_params=pltpu.CompilerParams(dimension_semantics=("arbitrary", "arbitrary"),
                                             vmem_limit_bytes=VMEM_LIMIT_BYTES),
        name="nsa_sample_attention",
    )(pt, q16, ck, cv, *([view] * npg), win_buf, new8, g16, expand, triu)
    o = o16[:, :NSA_HEADS].reshape(db, NSA_HEADS, NSA_KV_HEADS, HEAD_DIM)
    o = jnp.take_along_axis(o, (jnp.arange(NSA_HEADS) // NSA_GROUP)[None, :, None, None], axis=2)
    return o.reshape(db, NSA_HEADS * HEAD_DIM), wout


CMP_STEP_LANES = CMP_PAGES_PER_STEP * CMP_PER_PAGE
CMP_FEATURE_GROUP = 16


def _cmp_lane_blocks(n_pages):
    lane = np.arange(n_pages * CMP_PER_PAGE)
    step, rem = lane // CMP_STEP_LANES, lane % CMP_STEP_LANES
    j, pl_ = rem // CMP_PAGES_PER_STEP, rem % CMP_PAGES_PER_STEP
    return (step * CMP_PAGES_PER_STEP + pl_) * CMP_PER_PAGE + j


def _cmp_step_t_kernel(pt_ref, *refs):
    f32 = jnp.float32
    npg = CMP_PAGES_PER_STEP
    pages = refs[:npg]
    pe_ref, w1_ref, b1_ref, w2_ref, cos_ref, sin_ref, ck_ref, cv_ref, slab_ref = refs[npg:]
    outs = (ck_ref, cv_ref)
    for k, r in enumerate(pages):
        for s in range(2 * NSA_KV_HEADS):
            slab_ref[s, k * HEAD_DIM:(k + 1) * HEAD_DIM, :] = r[0, s]
    for kv in range(2):
        h = jnp.zeros((NSA_KV_HEADS * npg, CMP_PER_PAGE * CMP_HIDDEN), f32)
        for dg in range(HEAD_DIM // CMP_FEATURE_GROUP):
            x = jnp.concatenate(
                [jnp.concatenate([slab_ref[2 * kv + g, pl.ds(d, npg, stride=HEAD_DIM), :]
                                  for g in range(NSA_KV_HEADS)], axis=0) + pe_ref[kv, d]
                 for d in range(dg * CMP_FEATURE_GROUP, (dg + 1) * CMP_FEATURE_GROUP)], axis=1)
            h = h + jnp.dot(x.astype(jnp.bfloat16), w1_ref[kv, dg], preferred_element_type=f32)
        act = jax.nn.gelu(h + b1_ref[kv]).astype(jnp.bfloat16)
        ct = _nt_dot(w2_ref[kv], act)
        tile = jnp.concatenate(
            [jnp.concatenate([ct[j * HEAD_DIM:(j + 1) * HEAD_DIM, g * npg:(g + 1) * npg] for j in range(CMP_PER_PAGE)],
                             axis=1) for g in range(NSA_KV_HEADS)], axis=0)
        if kv == 0:
            row = lax.broadcasted_iota(jnp.int32, tile.shape, 0)
            n = tile.shape[0]
            partner = jnp.where((row % HEAD_DIM) < HEAD_DIM // 2, pltpu.roll(tile, n - HEAD_DIM // 2, 0),
                                pltpu.roll(tile, HEAD_DIM // 2, 0))
            tile = tile * cos_ref[...] + partner * sin_ref[...]
        outs[kv][0] = tile


def _compress_weights_t(cw1, cb1, cw2, cpe):
    eye = jnp.eye(CMP_PER_PAGE, dtype=jnp.float32)
    w1r = cw1.reshape(2, CMP_BLOCK, HEAD_DIM, CMP_HIDDEN)
    w1 = jnp.einsum('ktdn,ja->kdjtan', w1r, eye).reshape(
        2, HEAD_DIM // CMP_FEATURE_GROUP, CMP_FEATURE_GROUP * PAGE_SIZE, CMP_PER_PAGE * CMP_HIDDEN)
    b1 = jnp.tile(cb1, (1, CMP_PER_PAGE))[:, None, :]
    w2 = jnp.einsum('knd,ja->kjdan', cw2, eye).reshape(2, CMP_PER_PAGE * HEAD_DIM, CMP_PER_PAGE * CMP_HIDDEN)
    pe = jnp.tile(cpe.transpose(0, 2, 1), (1, 1, CMP_PER_PAGE))[:, :, None, :]
    return w1.astype(jnp.bfloat16), b1, w2.astype(jnp.bfloat16), pe


def _nsa_sample_compress_t(pool, layer_idx, page_table, cw1, cb1, cw2, cpe):
    n_pool, nl = pool.shape[:2]
    db, n_pages = page_table.shape
    npg = CMP_PAGES_PER_STEP
    nchunk = n_pages // npg
    nc = n_pages * CMP_PER_PAGE
    view = jnp.transpose(pool, (0, 1, 3, 4, 5, 2)).reshape(n_pool * nl, 2 * NSA_KV_HEADS, HEAD_DIM, PAGE_SIZE)
    pt = (page_table * nl + layer_idx).reshape(-1).astype(jnp.int32)
    w1, b1, w2, pe = _compress_weights_t(cw1, cb1, cw2, cpe)
    pos = (jnp.asarray(_cmp_lane_blocks(n_pages)) + 1) * CMP_BLOCK - 1
    half = HEAD_DIM // 2
    inv_freq = ROPE_THETA ** (-2.0 * jnp.arange(half, dtype=jnp.float32) / HEAD_DIM)
    ang = inv_freq[:, None] * pos.astype(jnp.float32)[None, :]
    cos = jnp.tile(jnp.cos(ang), (2 * NSA_KV_HEADS, 1))
    sin = jnp.tile(jnp.concatenate([-jnp.sin(ang), jnp.sin(ang)], axis=0), (NSA_KV_HEADS, 1))

    def page_map(k):
        return lambda b, c, pt_ref: (pt_ref[b * n_pages + c * npg + k], 0, 0, 0)

    const = lambda a: pl.BlockSpec(a.shape, lambda b, c, pt_ref: (0,) * a.ndim, pipeline_mode=pl.Buffered(1))
    lanes_c = lambda: pl.BlockSpec((NSA_KV_W, CMP_STEP_LANES), lambda b, c, pt_ref: (0, c))
    grid_spec = pltpu.PrefetchScalarGridSpec(
        num_scalar_prefetch=1, grid=(db, nchunk),
        in_specs=[pl.BlockSpec((1, 2 * NSA_KV_HEADS, HEAD_DIM, PAGE_SIZE), page_map(k)) for k in range(npg)]
        + [const(pe), const(w1), const(b1), const(w2), lanes_c(), lanes_c()],
        out_specs=[pl.BlockSpec((1, NSA_KV_W, CMP_STEP_LANES), lambda b, c, pt_ref: (b, 0, c))] * 2,
        scratch_shapes=[pltpu.VMEM((2 * NSA_KV_HEADS, npg * HEAD_DIM, PAGE_SIZE), jnp.float32)])
    return pl.pallas_call(
        _cmp_step_t_kernel, grid_spec=grid_spec,
        out_shape=[jax.ShapeDtypeStruct((db, NSA_KV_W, nc), jnp.float32)] * 2,
        compiler_params=pltpu.CompilerParams(dimension_semantics=("arbitrary", "arbitrary"),
                                             vmem_limit_bytes=VMEM_LIMIT_BYTES),
        name="nsa_sample_compress",
    )(pt, *([view] * npg), pe, w1, b1, w2, cos, sin)


def _nsa_step_t_kernel(pt_ref, *refs, topn):
    f32, bf16 = jnp.float32, jnp.bfloat16
    npg = STEP_PAGES
    q_ref, ck_ref, cv_ref = refs[:3]
    pages = refs[3:3 + npg]
    (win_ref, newr_ref, newt_ref, gate_ref, blk_ref, exp_ref, o_ref, wout_ref,
     selt_ref, m_ref, l_ref, acc_ref, side_ref) = refs[3 + npg:]
    cc = pl.program_id(1)
    q16 = q_ref[0]
    qb = q16.astype(bf16)
    row16 = lax.broadcasted_iota(jnp.int32, (STEP_ROWS, 1), 0)

    def new_key_scores(krow):
        return jnp.sum(q16 * krow, axis=1, keepdims=True)

    @pl.when(cc == 0)
    def _():
        nc = ck_ref.shape[2]
        s = jnp.dot(qb, ck_ref[0].astype(bf16), preferred_element_type=f32)
        p = jnp.exp(s - jnp.max(s, axis=1, keepdims=True))
        pn = p / jnp.sum(p, axis=1, keepdims=True)
        o_cmp = _nt_dot(pn.astype(bf16), cv_ref[0].astype(bf16))

        rowp = lax.broadcasted_iota(jnp.int32, pn.shape, 0)
        row8 = lax.broadcasted_iota(jnp.int32, (8, nc), 0)
        blk = blk_ref[...]
        val = jnp.full((8, nc), -jnp.inf, f32)
        for g in range(NSA_KV_HEADS):
            ig = jnp.sum(jnp.where((rowp >= g * NSA_GROUP) & (rowp < (g + 1) * NSA_GROUP), pn, 0.0),
                         axis=0, keepdims=True)
            ig = ig + pltpu.roll(ig, nc - CMP_PAGES_PER_STEP, 1)
            vg = jnp.where(blk[:1] < 0, -jnp.inf, jnp.where(blk[1:2] > 0, NSA_FORCE, ig))
            val = jnp.where(row8 == g, vg, val)
        sblk = jnp.where(blk[:1] < 0, nc, blk[:1])
        sel = jnp.zeros((8, nc), f32)
        for _ in range(topn):
            top = jnp.max(val, axis=1, keepdims=True)
            pick = jnp.min(jnp.where(val == top, sblk, nc), axis=1, keepdims=True)
            hit = sblk == pick
            sel = jnp.where(hit, 1.0, sel)
            val = jnp.where(hit, -jnp.inf, val)
        selh = jnp.where(row16 < NSA_GROUP, sel[0:1], jnp.where(row16 < NSA_HEADS, sel[1:2], 0.0))
        wsel = selt_ref.shape[2]
        for j in range(selt_ref.shape[0]):
            selt_ref[j] = selh[:, j * wsel:(j + 1) * wsel]

        m_ref[...] = new_key_scores(newr_ref[0, 0:1, :])
        l_ref[...] = jnp.ones(l_ref.shape, f32)
        acc_ref[...] = jnp.broadcast_to(newr_ref[0, 1:2, :], acc_ref.shape)

        sw = jnp.dot(qb, win_ref[0, 0].astype(bf16), preferred_element_type=f32)
        sn = new_key_scores(newr_ref[0, 2:3, :])
        mw = jnp.maximum(jnp.max(sw, axis=1, keepdims=True), sn)
        pw, pnw = jnp.exp(sw - mw), jnp.exp(sn - mw)
        lw = jnp.sum(pw, axis=1, keepdims=True) + pnw
        o_win = (_nt_dot(pw.astype(bf16), win_ref[0, 1].astype(bf16)) + pnw * newr_ref[0, 3:4, :]) / lw
        gt = jax.nn.sigmoid(gate_ref[0])
        side_ref[...] = gt[:, 0:1] * o_cmp + gt[:, 2:3] * o_win
        wl = win_ref.shape[3]
        lane = lax.broadcasted_iota(jnp.int32, (NSA_KV_W, wl), 1)
        for kv in range(2):
            wout_ref[0, kv] = jnp.where(lane == wl - 1, newt_ref[0, :, 2 + kv:3 + kv],
                                        pltpu.roll(win_ref[0, kv], wl - 1, 1))

    kt = jnp.concatenate([r[0, 0] for r in pages], axis=1)
    vt = jnp.concatenate([r[0, 1] for r in pages], axis=1)
    s = jnp.dot(qb, kt.astype(bf16), preferred_element_type=f32)
    picked = jnp.dot(selt_ref[cc].astype(bf16), exp_ref[...], preferred_element_type=f32)
    s = s + (picked - 1.0) * (-NEG)
    m_old = m_ref[...]
    m_new = jnp.maximum(m_old, jnp.max(s, axis=1, keepdims=True))
    alpha = jnp.exp(m_old - m_new)
    p = jnp.exp(s - m_new)
    m_ref[...] = m_new
    l_ref[...] = l_ref[...] * alpha + jnp.sum(p, axis=1, keepdims=True)
    acc_ref[...] = acc_ref[...] * alpha + _nt_dot(p.astype(bf16), vt.astype(bf16))

    @pl.when(cc == pl.num_programs(1) - 1)
    def _():
        gt = jax.nn.sigmoid(gate_ref[0])
        o_ref[0] = side_ref[...] + gt[:, 1:2] * acc_ref[...] / l_ref[...]


def _nsa_sample_attention_t(qr, ckt, cvt, slc_pool, layer_idx, page_table, win_buf, newrows, gate):
    db, n_pages = page_table.shape
    n_pool, nl = slc_pool.shape[:2]
    wlen = win_buf.shape[1]
    nc = ckt.shape[2]
    past = n_pages * PAGE_SIZE
    cur = past // SEL_BLOCK
    npg = STEP_PAGES
    nchunk = n_pages // npg
    keys = npg * PAGE_SIZE
    wsel = npg * CMP_PER_PAGE
    f32, bf16 = jnp.float32, jnp.bfloat16
    view = jnp.transpose(slc_pool, (0, 1, 3, 4, 5, 2)).reshape(n_pool * nl, 2, NSA_KV_W, PAGE_SIZE)
    wint = jnp.transpose(win_buf, (0, 2, 3, 4, 1)).reshape(db, 2, NSA_KV_W, wlen)
    pt = (page_table * nl + layer_idx).reshape(-1).astype(jnp.int32)
    hmask = (jnp.arange(NSA_HEADS)[:, None] // NSA_GROUP == jnp.arange(NSA_KV_HEADS)[None, :]).astype(f32)
    q16 = (qr * HEAD_DIM ** -0.5)[:, :, None, :] * hmask[None, :, :, None]
    q16 = jnp.pad(q16.reshape(db, NSA_HEADS, NSA_KV_W), ((0, 0), (0, STEP_ROWS - NSA_HEADS), (0, 0)))
    newr = jnp.pad(newrows, ((0, 0), (0, 8 - newrows.shape[1]), (0, 0)))
    newt = jnp.pad(newrows.transpose(0, 2, 1), ((0, 0), (0, 0), (0, LANES - newrows.shape[1])))
    g16 = jnp.pad(gate.reshape(db, 3, NSA_HEADS).transpose(0, 2, 1),
                  ((0, 0), (0, STEP_ROWS - NSA_HEADS), (0, LANES - 3)))
    cblk = _cmp_lane_blocks(n_pages)
    jj = cblk % CMP_PER_PAGE
    sblk = np.where(jj % 2 == 0, cblk // 2, -1)
    forced = ((sblk == 0) | (sblk == cur - 1)).astype(np.int32)
    blk8 = np.zeros((8, nc), np.int32)
    blk8[0], blk8[1] = sblk, forced
    loc = np.arange(wsel)
    lstep, lrem = loc // CMP_STEP_LANES, loc % CMP_STEP_LANES
    lj, lpage = lrem // CMP_PAGES_PER_STEP, lstep * CMP_PAGES_PER_STEP + lrem % CMP_PAGES_PER_STEP
    kidx = np.arange(keys)
    expand = ((lj[:, None] % 2 == 0) & (kidx[None, :] // PAGE_SIZE == lpage[:, None])
              & ((kidx[None, :] % PAGE_SIZE) // SEL_BLOCK == lj[:, None] // 2)).astype(np.float32)
    topn = min(NSA_TOPN, cur + 1) - 1

    def page_map(k):
        return lambda b, c, pt_ref: (pt_ref[b * n_pages + c * npg + k], 0, 0, 0)

    per_b = lambda shp: pl.BlockSpec((1,) + shp, lambda b, c, pt_ref: (b,) + (0,) * len(shp))
    const = lambda a: pl.BlockSpec(a.shape, lambda b, c, pt_ref: (0,) * a.ndim)
    consts = (jnp.asarray(blk8), jnp.asarray(expand, bf16))
    grid_spec = pltpu.PrefetchScalarGridSpec(
        num_scalar_prefetch=1, grid=(db, nchunk),
        in_specs=[per_b((STEP_ROWS, NSA_KV_W)), per_b((NSA_KV_W, nc)), per_b((NSA_KV_W, nc))]
        + [pl.BlockSpec((1, 2, NSA_KV_W, PAGE_SIZE), page_map(k)) for k in range(npg)]
        + [per_b((2, NSA_KV_W, wlen)), per_b((8, NSA_KV_W)), per_b((NSA_KV_W, LANES)), per_b((STEP_ROWS, LANES))]
        + [const(a) for a in consts],
        out_specs=[per_b((STEP_ROWS, NSA_KV_W)), per_b((2, NSA_KV_W, wlen))],
        scratch_shapes=[pltpu.VMEM((nc // wsel, STEP_ROWS, wsel), f32),
                        pltpu.VMEM((STEP_ROWS, 1), f32), pltpu.VMEM((STEP_ROWS, 1), f32),
                        pltpu.VMEM((STEP_ROWS, NSA_KV_W), f32), pltpu.VMEM((STEP_ROWS, NSA_KV_W), f32)])
    o16, wout = pl.pallas_call(
        functools.partial(_nsa_step_t_kernel, topn=topn), grid_spec=grid_spec,
        out_shape=[jax.ShapeDtypeStruct((db, STEP_ROWS, NSA_KV_W), f32),
                   jax.ShapeDtypeStruct((db, 2, NSA_KV_W, wlen), f32)],
        compiler_params=pltpu.CompilerParams(dimension_semantics=("arbitrary", "arbitrary"),
                                             vmem_limit_bytes=VMEM_LIMIT_BYTES),
        name="nsa_sample_attention",
    )(pt, q16, ckt, cvt, *([view] * npg), wint, newr, newt, g16, *consts)
    o = o16[:, :NSA_HEADS].reshape(db, NSA_HEADS, NSA_KV_HEADS, HEAD_DIM)
    o = jnp.take_along_axis(o, (jnp.arange(NSA_HEADS) // NSA_GROUP)[None, :, None, None], axis=2)
    wout = jnp.transpose(wout.reshape(db, 2, NSA_KV_HEADS, HEAD_DIM, wlen), (0, 4, 1, 2, 3))
    return o.reshape(db, NSA_HEADS * HEAD_DIM), wout


DIL_ROW_CHUNK = 64


def _dil_step_kernel(q_ref, buf_ref, newt_ref, newr_ref, bias_ref, o_ref, out_ref, p_ref, pn_ref, den_ref):
    f32, bf16 = jnp.float32, jnp.bfloat16
    kv = pl.program_id(1)
    wlen = buf_ref.shape[3]
    nrow = buf_ref.shape[2]
    q16 = q_ref[0]

    @pl.when(kv == 0)
    def _():
        s = jnp.dot(q16.astype(bf16), buf_ref[0, 0].astype(bf16), preferred_element_type=f32)
        s_new = jnp.sum(q16 * newr_ref[0, 0:1, :], axis=1, keepdims=True)
        ms, es, ens, dens = [], [], [], []
        for g in range(len(DIL_GROUPS)):
            sg = s + bias_ref[g:g + 1, :]
            m = jnp.maximum(jnp.max(sg, axis=1, keepdims=True), s_new)
            e, en = jnp.exp(sg - m), jnp.exp(s_new - m)
            ms.append(m); es.append(e); ens.append(en)
            dens.append(jnp.sum(e, axis=1, keepdims=True) + en)
        m_all = functools.reduce(jnp.maximum, ms)
        ws = [jnp.exp(m - m_all) for m in ms]
        p_ref[...] = sum(w * e for w, e in zip(ws, es))
        pn_ref[...] = sum(w * en for w, en in zip(ws, ens))
        den_ref[...] = sum(w * d for w, d in zip(ws, dens))

    @pl.when(kv == 1)
    def _():
        r = _nt_dot(p_ref[...].astype(bf16), buf_ref[0, 0].astype(bf16))
        r = (r + pn_ref[...] * newr_ref[0, 1:2, :]) / den_ref[...]
        head = lax.broadcasted_iota(jnp.int32, r.shape, 1) // HEAD_DIM
        row = lax.broadcasted_iota(jnp.int32, r.shape, 0)
        o_ref[0] = jnp.broadcast_to(jnp.sum(jnp.where(head == row, r, 0.0), axis=0, keepdims=True), o_ref.shape[1:])

    lane = lax.broadcasted_iota(jnp.int32, (DIL_ROW_CHUNK, wlen), 1)
    for c in range(nrow // DIL_ROW_CHUNK):
        rs = slice(c * DIL_ROW_CHUNK, (c + 1) * DIL_ROW_CHUNK)
        col = jnp.where(kv == 0, newt_ref[0, rs, 0:1], newt_ref[0, rs, 1:2])
        out_ref[0, 0, rs, :] = jnp.where(lane == wlen - 1, col, pltpu.roll(buf_ref[0, 0, rs, :], wlen - 1, 1))


def _dil_sample_attention(qr, kr_new, v_new, buf):
    db, wlen = buf.shape[:2]
    f32 = jnp.float32
    buft = jnp.transpose(buf, (0, 2, 3, 4, 1)).reshape(db, 2, DIL_W, wlen)
    eye = jnp.eye(DIL_HEADS, dtype=f32)
    q16 = ((qr * HEAD_DIM ** -0.5)[:, :, None, :] * eye[None, :, :, None]).reshape(db, DIL_HEADS, DIL_W)
    newr = jnp.pad(jnp.stack([kr_new, v_new], axis=1), ((0, 0), (0, 6), (0, 0)))
    newt = jnp.pad(jnp.stack([kr_new, v_new], axis=2), ((0, 0), (0, 0), (0, LANES - 2)))
    back = wlen - jnp.arange(wlen)
    bias = jnp.stack([jnp.where((back % d == 0) & (back // d <= DIL_SPAN), 0.0, NEG) for _, d in DIL_GROUPS])
    bias = jnp.pad(bias, ((0, 8 - len(DIL_GROUPS)), (0, 0))).astype(f32)
    o, new_buf = pl.pallas_call(
        _dil_step_kernel,
        grid=(db, 2),
        in_specs=[pl.BlockSpec((1, DIL_HEADS, DIL_W), lambda b, k: (b, 0, 0)),
                  pl.BlockSpec((1, 1, DIL_W, wlen), lambda b, k: (b, k, 0, 0)),
                  pl.BlockSpec((1, DIL_W, LANES), lambda b, k: (b, 0, 0)),
                  pl.BlockSpec((1, 8, DIL_W), lambda b, k: (b, 0, 0)),
                  pl.BlockSpec((8, wlen), lambda b, k: (0, 0))],
        out_specs=[pl.BlockSpec((1, 8, DIL_W), lambda b, k: (b, 0, 0)),
                   pl.BlockSpec((1, 1, DIL_W, wlen), lambda b, k: (b, k, 0, 0))],
        out_shape=[jax.ShapeDtypeStruct((db, 8, DIL_W), f32),
                   jax.ShapeDtypeStruct((db, 2, DIL_W, wlen), f32)],
        scratch_shapes=[pltpu.VMEM((DIL_HEADS, wlen), f32), pltpu.VMEM((DIL_HEADS, 1), f32),
                        pltpu.VMEM((DIL_HEADS, 1), f32)],
        compiler_params=pltpu.CompilerParams(dimension_semantics=("arbitrary", "arbitrary"),
                                             vmem_limit_bytes=VMEM_LIMIT_BYTES),
        name="dil_sample",
    )(q16, buft, newt, newr, bias)
    new_buf = jnp.transpose(new_buf.reshape(db, 2, DIL_HEADS, HEAD_DIM, wlen), (0, 4, 1, 2, 3))
    return o[:, 0], new_buf


def _dil_band_kernel(q_ref, kp_ref, kc_ref, vp_ref, vc_ref, num_ref, st_ref):
    f32, bf16 = jnp.float32, jnp.bfloat16
    blk = DIL_BLOCK
    n = pl.program_id(0)
    i = lax.broadcasted_iota(jnp.int32, (blk, 2 * blk), 0)
    j = lax.broadcasted_iota(jnp.int32, (blk, 2 * blk), 1) - blk
    ok = (i - j >= 0) & (i - j <= DIL_SPAN) & (n * blk + j >= 0)
    bias = jnp.where(ok, 0.0, NEG)
    bias = jnp.concatenate([bias, bias], axis=0)
    lane = lax.broadcasted_iota(jnp.int32, (blk, LANES), 1)
    first = lane < HEAD_DIM
    stats = jnp.zeros((blk, LANES), f32)
    for p in range(DIL_HEADS // 2):
        cols = slice(p * LANES, (p + 1) * LANES)
        qp = q_ref[:, cols] * (HEAD_DIM ** -0.5)
        qst = jnp.concatenate([jnp.where(first, qp, 0.0), jnp.where(first, 0.0, qp)], axis=0).astype(bf16)
        kk = jnp.concatenate([kp_ref[:, cols], kc_ref[:, cols]], axis=0).astype(bf16)
        vv = jnp.concatenate([vp_ref[:, cols], vc_ref[:, cols]], axis=0).astype(bf16)
        s = _nt_dot(qst, kk) + bias
        m = jnp.max(s, axis=1, keepdims=True)
        e = jnp.exp(s - m)
        den = jnp.sum(e, axis=1, keepdims=True)
        nm = jnp.dot(e.astype(bf16), vv, preferred_element_type=f32)
        num_ref[:, cols] = jnp.where(first, nm[:blk], nm[blk:])
        for a in range(2):
            h = 2 * p + a
            stats = jnp.where(lane == h, m[a * blk:(a + 1) * blk], stats)
            stats = jnp.where(lane == DIL_HEADS + h, den[a * blk:(a + 1) * blk], stats)
    st_ref[...] = stats


def _dil_band_stats(qr, kr, v, d):
    l = qr.shape[0]
    assert l % (d * DIL_BLOCK) == 0
    nb = l // (d * DIL_BLOCK)
    view = lambda a: a.reshape(l // d, d * a.shape[1])
    blk = lambda w, prev: pl.BlockSpec((DIL_BLOCK, w),
                                       (lambda n, r: (jnp.maximum(n - 1, 0), r)) if prev else (lambda n, r: (n, r)))
    num, st = pl.pallas_call(
        _dil_band_kernel,
        grid=(nb, d),
        in_specs=[blk(DIL_W, False), blk(DIL_W, True), blk(DIL_W, False), blk(DIL_W, True), blk(DIL_W, False)],
        out_specs=[blk(DIL_W, False), blk(LANES, False)],
        out_shape=[jax.ShapeDtypeStruct((l // d, d * DIL_W), jnp.float32),
                   jax.ShapeDtypeStruct((l // d, d * LANES), jnp.float32)],
        compiler_params=pltpu.CompilerParams(dimension_semantics=("arbitrary", "arbitrary"),
                                             vmem_limit_bytes=VMEM_LIMIT_BYTES),
        name="dil_band_stats",
    )(view(qr), view(kr), view(kr), view(v), view(v))
    st = st.reshape(l, LANES)
    return st[:, :DIL_HEADS], st[:, DIL_HEADS:2 * DIL_HEADS], num.reshape(l, DIL_HEADS, HEAD_DIM)


GDN_PREP_ROWS = 512
GDN_TILE_CHUNKS = 4
GDN_PAIRS = GDN_HEADS // 2
GDN_A_LANE = 3 * NSA_HEADS
GDN_B_LANE = GDN_A_LANE + GDN_HEADS


def _hi_lo(x):
    hi = x.astype(jnp.bfloat16)
    return hi, (x - hi.astype(jnp.float32)).astype(jnp.bfloat16)


def _three_way(x):
    f32 = jnp.float32
    x1 = x.astype(jnp.bfloat16)
    r1 = x - x1.astype(f32)
    x2 = r1.astype(jnp.bfloat16)
    return x1, x2, (r1 - x2.astype(f32)).astype(jnp.bfloat16)


def _dot_select(x, sel):
    return sum(jnp.dot(piece, sel, preferred_element_type=jnp.float32) for piece in _three_way(x))


def _select_dot(sel, x):
    return sum(jnp.dot(sel, piece, preferred_element_type=jnp.float32) for piece in _three_way(x))


def _dot_hl(a, b):
    f32 = jnp.float32
    ah, al = _hi_lo(a)
    bh, bl = _hi_lo(b)
    return (jnp.dot(ah, bh, preferred_element_type=f32) + jnp.dot(ah, bl, preferred_element_type=f32)
            + jnp.dot(al, bh, preferred_element_type=f32))


def _gdn_prep_kernel(u_ref, sm_ref, cw_ref, prm_ref, ea_ref, eb_ref, eh_ref, q_ref, k_ref, v_ref, g_ref, b_ref,
                     carry_ref):
    f32 = jnp.float32
    i = pl.program_id(0)
    tl = u_ref.shape[0]

    @pl.when(i == 0)
    def _():
        carry_ref[...] = jnp.zeros(carry_ref.shape, f32)

    u = u_ref[...]
    prev = carry_ref[...]
    row = lax.broadcasted_iota(jnp.int32, u.shape, 0)

    def shifted(k):
        r = pltpu.roll(u, k, 0)
        for j in range(k):
            r = jnp.where(row == j, prev[8 - k + j:8 - k + j + 1], r)
        return r

    cw = cw_ref[...]
    c = cw[0:1] * shifted(3) + cw[1:2] * shifted(2) + cw[2:3] * shifted(1) + cw[3:4] * u
    carry_ref[...] = u[tl - 8:]
    c = c * jax.nn.sigmoid(c)
    eh = eh_ref[...]

    def l2n(x):
        return x * lax.rsqrt(_dot_select(x * x, eh) + NORM_EPS)

    q_ref[...] = l2n(c[:, :GDN_W]) * (HEAD_DIM ** -0.5)
    k_ref[...] = l2n(c[:, GDN_W:2 * GDN_W])
    v_ref[...] = c[:, 2 * GDN_W:]
    sm = sm_ref[...]
    x = sm + prm_ref[1:2]
    softplus = jnp.maximum(x, 0.0) + jnp.log(1.0 + jnp.exp(-jnp.abs(x)))
    g_ref[...] = _dot_select(-jnp.exp(prm_ref[0:1]) * softplus, ea_ref[...])
    b_ref[...] = _dot_select(jax.nn.sigmoid(sm), eb_ref[...])


def _gdn_chunk_kernel(q_ref, k_ref, v_ref, g_ref, b_ref, z_ref, nw_ref, lt_ref, eh_ref, o_ref, s_out_ref, s_ref):
    f32 = jnp.float32
    ch = GDN_CHUNK
    i = pl.program_id(0)

    @pl.when(i == 0)
    def _():
        s_ref[...] = jnp.zeros(s_ref.shape, f32)

    lane = lax.broadcasted_iota(jnp.int32, (ch, LANES), 1)
    first = lane < HEAD_DIM
    stack = lambda x: jnp.concatenate([jnp.where(first, x, 0.0), jnp.where(first, 0.0, x)], axis=0)
    r2 = lax.broadcasted_iota(jnp.int32, (2 * ch, 2 * ch), 0)
    c2 = lax.broadcasted_iota(jnp.int32, (2 * ch, 2 * ch), 1)
    same = (r2 // ch) == (c2 // ch)
    tri = same & (r2 % ch >= c2 % ch)
    strict = same & (r2 % ch > c2 % ch)
    eye = r2 == c2
    eye_f = jnp.where(eye, 1.0, 0.0)
    diag2 = lax.broadcasted_iota(jnp.int32, (ch, LANES), 0) == lane % HEAD_DIM
    lt = lt_ref[...]
    bf = lambda x: x.astype(jnp.bfloat16)
    dot = lambda a, b: jnp.dot(bf(a), bf(b), preferred_element_type=f32)

    blocks = [(c, p) for c in range(GDN_TILE_CHUNKS) for p in range(GDN_PAIRS)]
    ld = lambda ref, c, p: ref[c * ch:(c + 1) * ch, p * LANES:(p + 1) * LANES]
    gcs = [_select_dot(lt, ld(g_ref, c, p)) for c, p in blocks]
    amats, qks, rhs_u, rhs_w, qgs, kds, decs = [], [], [], [], [], [], []
    for (c, p), gc in zip(blocks, gcs):
        kk, qq, vv, bb = ld(k_ref, c, p), ld(q_ref, c, p), ld(v_ref, c, p), ld(b_ref, c, p)
        eg = jnp.exp(gc)
        g_end = gc[ch - 1:ch]
        kb = kk * bb
        col = jnp.concatenate([jnp.broadcast_to(gc[:, 0:1], (ch, LANES)),
                               jnp.broadcast_to(gc[:, HEAD_DIM:HEAD_DIM + 1], (ch, LANES))], axis=0)
        rowv = jnp.sum(jnp.where(diag2, gc, 0.0), axis=0, keepdims=True)
        gam = jnp.where(tri, jnp.exp(jnp.where(tri, col - rowv, 0.0)), 0.0)
        kst = stack(kk)
        amats.append(jnp.where(strict, _nt_dot(bf(stack(kb)), bf(kst)) * gam, 0.0))
        qks.append(jnp.where(tri, _nt_dot(bf(stack(qq)), bf(kst)) * gam, 0.0))
        rhs_u.append(stack(vv * bb))
        rhs_w.append(stack(kb * eg))
        qgs.append(stack(qq * eg))
        kds.append(stack(kk * jnp.exp(g_end - gc)))
        decs.append(jnp.sum(jnp.where(eye, jnp.exp(g_end), 0.0), axis=1, keepdims=True))
    xs = [eye_f - a for a in amats]
    pws = [_dot_hl(a, a) for a in amats]
    steps = GDN_CHUNK.bit_length() - 2
    for r in range(steps):
        xs = [x + _dot_hl(x, pw) for x, pw in zip(xs, pws)]
        if r < steps - 1:
            pws = [_dot_hl(pw, pw) for pw in pws]
    uus = [dot(x, u) for x, u in zip(xs, rhs_u)]
    wws = [dot(x, w) for x, w in zip(xs, rhs_w)]
    kdts = [kd.T for kd in kds]
    states = [s_ref[p] for p in range(GDN_PAIRS)]
    for c in range(GDN_TILE_CHUNKS):
        rs = slice(c * ch, (c + 1) * ch)
        outs = []
        for p in range(GDN_PAIRS):
            n = c * GDN_PAIRS + p
            s = states[p]
            v_new = uus[n] - dot(wws[n], s)
            o_st = dot(qgs[n], s) + dot(qks[n], v_new)
            states[p] = s * decs[n] + dot(kdts[n], v_new)
            outs.append(o_st[:ch] + o_st[ch:])
        o = jnp.concatenate(outs, axis=1)
        ms = _dot_select(o * o, eh_ref[...]) * (1.0 / HEAD_DIM)
        z = z_ref[rs, :]
        o_ref[rs, :] = o * lax.rsqrt(ms + NORM_EPS) * nw_ref[...] * (z * jax.nn.sigmoid(z))
    for p in range(GDN_PAIRS):
        s_ref[p] = states[p]

    @pl.when(i == pl.num_programs(0) - 1)
    def _():
        s_out_ref[...] = s_ref[...]


def _gdn_prompt(qkv, small, z, conv_w, a_log, dt_bias, norm_w):
    l = qkv.shape[0]
    f32, bf16 = jnp.float32, jnp.bfloat16
    w = GDN_W
    hh = jnp.arange(w) // HEAD_DIM
    expander = lambda base: (jnp.arange(LANES)[:, None] == base + hh[None, :]).astype(bf16)
    eh = (hh[:, None] == hh[None, :]).astype(bf16)
    cw8 = jnp.zeros((8, 3 * w), f32).at[:GDN_CONV].set(conv_w)
    prm = jnp.zeros((8, LANES), f32)
    prm = prm.at[0, GDN_A_LANE:GDN_A_LANE + GDN_HEADS].set(a_log).at[1, GDN_A_LANE:GDN_A_LANE + GDN_HEADS].set(dt_bias)
    tl = GDN_PREP_ROWS
    row = lambda wd: pl.BlockSpec((tl, wd), lambda i: (i, 0))
    const = lambda a: pl.BlockSpec(a.shape, lambda i: (0,) * a.ndim)
    ea, eb = expander(GDN_A_LANE), expander(GDN_B_LANE)
    q, k, v, g, b = pl.pallas_call(
        _gdn_prep_kernel,
        grid=(l // tl,),
        in_specs=[row(3 * w), row(LANES), const(cw8), const(prm), const(ea), const(eb), const(eh)],
        out_specs=[row(w)] * 5,
        out_shape=[jax.ShapeDtypeStruct((l, w), f32)] * 5,
        scratch_shapes=[pltpu.VMEM((8, 3 * w), f32)],
        compiler_params=pltpu.CompilerParams(dimension_semantics=("arbitrary",), vmem_limit_bytes=VMEM_LIMIT_BYTES),
        name="gdn_prep",
    )(qkv, small, cw8, prm, ea, eb, eh)
    tc = GDN_TILE_CHUNKS * GDN_CHUNK
    lt = (jnp.arange(GDN_CHUNK)[:, None] >= jnp.arange(GDN_CHUNK)[None, :]).astype(bf16)
    nw = jnp.tile(norm_w, GDN_HEADS).reshape(1, w)
    rowc = pl.BlockSpec((tc, w), lambda i: (i, 0))
    o, s_bd = pl.pallas_call(
        _gdn_chunk_kernel,
        grid=(l // tc,),
        in_specs=[rowc] * 6 + [const(nw), const(lt), const(eh)],
        out_specs=[rowc, pl.BlockSpec((GDN_PAIRS, LANES, LANES), lambda i: (0, 0, 0))],
        out_shape=[jax.ShapeDtypeStruct((l, w), f32), jax.ShapeDtypeStruct((GDN_PAIRS, LANES, LANES), f32)],
        scratch_shapes=[pltpu.VMEM((GDN_PAIRS, LANES, LANES), f32)],
        compiler_params=pltpu.CompilerParams(dimension_semantics=("arbitrary",), vmem_limit_bytes=VMEM_LIMIT_BYTES),
        name="gdn_chunk",
    )(q, k, v, g, b, z, nw, lt, eh)
    s4 = s_bd.reshape(GDN_PAIRS, 2, HEAD_DIM, 2, HEAD_DIM)
    s_fin = jnp.stack([s4[:, 0, :, 0], s4[:, 1, :, 1]], axis=1).reshape(GDN_HEADS, HEAD_DIM, HEAD_DIM)
    return o, s_fin


def _proj_dil_kernel(x_ref, w_ref, cos_ref, sin_ref, bf_ref, kv_ref):
    acc = jnp.dot(x_ref[...].astype(jnp.bfloat16), w_ref[...], preferred_element_type=jnp.float32)
    reps = 2 * DIL_W // LANES
    qk = _rope_lanes(acc[:, :2 * DIL_W], _lane_tile(cos_ref[...], reps), _lane_tile(sin_ref[...], reps))
    v = acc[:, 2 * DIL_W:]
    bf_ref[...] = jnp.concatenate([qk, v], axis=1).astype(jnp.bfloat16)
    kv_ref[...] = jnp.concatenate([qk[:, DIL_W:], v], axis=1)


def _proj_dil(x, w_in):
    l, d = x.shape
    tm = _row_tile(l)
    cos, sin = _rope_tables(jnp.arange(l), LANES)
    return pl.pallas_call(
        _proj_dil_kernel,
        grid=(l // tm,),
        in_specs=[pl.BlockSpec((tm, d), lambda i: (i, 0)),
                  pl.BlockSpec((d, 3 * DIL_W), lambda i: (0, 0)),
                  pl.BlockSpec((tm, LANES), lambda i: (i, 0)),
                  pl.BlockSpec((tm, LANES), lambda i: (i, 0))],
        out_specs=[pl.BlockSpec((tm, 3 * DIL_W), lambda i: (i, 0)),
                   pl.BlockSpec((tm, 2 * DIL_W), lambda i: (i, 0))],
        out_shape=[jax.ShapeDtypeStruct((l, 3 * DIL_W), jnp.bfloat16),
                   jax.ShapeDtypeStruct((l, 2 * DIL_W), jnp.float32)],
        compiler_params=pltpu.CompilerParams(dimension_semantics=("arbitrary",), vmem_limit_bytes=VMEM_LIMIT_BYTES),
        name="proj_dil",
    )(x, w_in.astype(jnp.bfloat16), cos, sin)


def _dil_band_stats_packed(qkv, d):
    l = qkv.shape[0]
    assert l % (d * DIL_BLOCK) == 0
    nb = l // (d * DIL_BLOCK)
    view = qkv.reshape(l // d, d * 3 * DIL_W)

    def part(which, prev):
        return pl.BlockSpec((DIL_BLOCK, DIL_W), (lambda n, r: (jnp.maximum(n - 1, 0), 3 * r + which)) if prev
                            else (lambda n, r: (n, 3 * r + which)))

    out = lambda w: pl.BlockSpec((DIL_BLOCK, w), lambda n, r: (n, r))
    num, st = pl.pallas_call(
        _dil_band_kernel,
        grid=(nb, d),
        in_specs=[part(0, False), part(1, True), part(1, False), part(2, True), part(2, False)],
        out_specs=[out(DIL_W), out(LANES)],
        out_shape=[jax.ShapeDtypeStruct((l // d, d * DIL_W), jnp.float32),
                   jax.ShapeDtypeStruct((l // d, d * LANES), jnp.float32)],
        compiler_params=pltpu.CompilerParams(dimension_semantics=("arbitrary", "arbitrary"),
                                             vmem_limit_bytes=VMEM_LIMIT_BYTES),
        name="dil_band_stats",
    )(view, view, view, view, view)
    return num.reshape(l, DIL_W), st.reshape(l, LANES)


def _dil_merge_kernel(*refs):
    f32 = jnp.float32
    ng = len(DIL_GROUPS)
    nums, sts = refs[:ng], refs[ng:2 * ng]
    ex_ref, w_ref, res_ref, g_ref, b_ref, o_ref = refs[2 * ng:]
    st = [r[...] for r in sts]
    m_all = functools.reduce(jnp.maximum, st)
    ws = [jnp.exp(s - m_all) for s in st]
    den = sum(w * pltpu.roll(s, LANES - DIL_HEADS, 1) for w, s in zip(ws, st))
    head_lane = lax.broadcasted_iota(jnp.int32, den.shape, 1) < DIL_HEADS
    o = sum(_dot_select(jnp.where(head_lane, w / den, 0.0), ex_ref[...]) * n[...]
            for w, n in zip(ws, nums))
    acc = jnp.dot(o.astype(jnp.bfloat16), w_ref[...], preferred_element_type=f32)
    o_ref[...] = _layer_norm_rows(DEEPNORM_ALPHA * res_ref[...] + acc, g_ref[...], b_ref[...])


def _dil_merge_proj(nums, sts, w_out, res, g, b):
    l, n = res.shape
    tm = _row_tile(l)
    expand = (jnp.arange(LANES)[:, None] == jnp.arange(DIL_W)[None, :] // HEAD_DIM).astype(jnp.bfloat16)
    row = lambda w: pl.BlockSpec((tm, w), lambda i: (i, 0))
    const = lambda a: pl.BlockSpec(a.shape, lambda i: (0,) * a.ndim)
    wb = w_out.astype(jnp.bfloat16)
    g2, b2 = g.reshape(1, n), b.reshape(1, n)
    return pl.pallas_call(
        _dil_merge_kernel,
        grid=(l // tm,),
        in_specs=[row(DIL_W)] * len(nums) + [row(LANES)] * len(sts) + [const(expand), const(wb), row(n),
                                                                       const(g2), const(b2)],
        out_specs=row(n),
        out_shape=jax.ShapeDtypeStruct((l, n), jnp.float32),
        compiler_params=pltpu.CompilerParams(dimension_semantics=("arbitrary",), vmem_limit_bytes=VMEM_LIMIT_BYTES),
        name="dil_merge_proj",
    )(*nums, *sts, expand, wb, res, g2, b2)


def _split_cols(h, widths):
    parts, start = [], 0
    for w in widths:
        parts.append(h[..., start:start + w])
        start += w
    return parts


def _even_widths():
    return (NSA_Q_W,) + (NSA_KV_W,) * 6 + (3 * NSA_HEADS, 3 * GDN_W, GDN_HEADS, GDN_HEADS, GDN_W)


def _rms_norm(x, w):
    return x * lax.rsqrt(jnp.mean(jnp.square(x), axis=-1, keepdims=True) + NORM_EPS) * w


def _l2_norm(x):
    return x * lax.rsqrt(jnp.sum(jnp.square(x), axis=-1, keepdims=True) + NORM_EPS)


def _rope(x, pos):
    half = HEAD_DIM // 2
    inv_freq = ROPE_THETA ** (-2.0 * jnp.arange(half, dtype=jnp.float32) / HEAD_DIM)
    ang = pos.astype(jnp.float32)[:, None] * inv_freq[None, :]
    cos, sin = jnp.cos(ang)[:, None, :], jnp.sin(ang)[:, None, :]
    xf = x.astype(jnp.float32)
    x1, x2 = xf[..., :half], xf[..., half:]
    return jnp.concatenate([x1 * cos - x2 * sin, x2 * cos + x1 * sin], axis=-1)


def _causal_dwconv(hist, u, w):
    width, s = w.shape[0], u.shape[1]
    ext = jnp.concatenate([hist.astype(u.dtype), u], axis=1)
    out = w[0] * ext[:, :s]
    for j in range(1, width):
        out = out + w[j] * ext[:, j:j + s]
    return out, ext[:, s:]


def _masked_softmax(s, mask):
    s = jnp.where(mask, s, -jnp.inf)
    m = jnp.max(s, axis=-1, keepdims=True)
    m = jnp.where(jnp.isfinite(m), m, 0.0)
    e = jnp.where(mask, jnp.exp(s - m), 0.0)
    den = jnp.sum(e, axis=-1, keepdims=True)
    return e / jnp.where(den > 0.0, den, 1.0)


def _gather_pages(pool, page_table, layer_idx):
    rows = pool[page_table, layer_idx]
    return rows.reshape(rows.shape[0], -1, *rows.shape[3:])


def _nsa_compress(rows, w1, b1, w2, pe):
    b, l, g, dh = rows.shape
    nc = l // CMP_BLOCK
    blk = rows[:, :nc * CMP_BLOCK].astype(jnp.float32).reshape(b, nc, CMP_BLOCK, g, dh) + pe[:, None, :]
    flat = blk.transpose(0, 1, 3, 2, 4).reshape(b, nc, g, CMP_BLOCK * dh)
    return jax.nn.gelu(flat @ w1 + b1) @ w2


def _nsa_compressed_kv(k_rows, v_rows, cw1, cb1, cw2, cpe):
    ck = _nsa_compress(k_rows, cw1[0], cb1[0], cw2[0], cpe[0])
    cv = _nsa_compress(v_rows, cw1[1], cb1[1], cw2[1], cpe[1])
    nc = ck.shape[1]
    ck = _rope(ck, (jnp.arange(nc) + 1) * CMP_BLOCK - 1)
    return ck, cv


def _nsa_attend(q, qpos, ck, cv, sk, sv, wk, wv, wpos):
    b, nq = q.shape[:2]
    scale = HEAD_DIM ** -0.5
    nc, ns = ck.shape[1], sk.shape[2]
    cend = (jnp.arange(nc) + 1) * CMP_BLOCK - 1
    s = jnp.einsum('bqghd,bcgd->bghqc', q, ck) * scale
    p_cmp = _masked_softmax(s, cend[None, :] <= qpos[:, None])
    o_cmp = jnp.einsum('bghqc,bcgd->bqghd', p_cmp, cv)
    imp = jnp.sum(p_cmp, axis=2)
    imp = jnp.pad(imp, ((0, 0), (0, 0), (0, 0), (0, ns * CMP_PER_SEL - nc)))
    imp = imp.reshape(b, NSA_KV_HEADS, nq, ns, CMP_PER_SEL).sum(-1)
    blk = jnp.arange(ns)[None, :]
    cur = (qpos // SEL_BLOCK)[:, None]
    forced = (blk == 0) | (blk == cur) | (blk == cur - 1)
    imp = jnp.where(blk <= cur, jnp.where(forced, NSA_FORCE, imp), -1.0)
    _, idx = lax.top_k(imp, min(NSA_TOPN, ns))
    n = idx.shape[-1]
    pick = jax.vmap(jax.vmap(lambda kb, ix: kb[ix]))
    ksel = pick(sk, idx).reshape(b, NSA_KV_HEADS, nq, n * SEL_BLOCK, HEAD_DIM)
    vsel = pick(sv, idx).reshape(b, NSA_KV_HEADS, nq, n * SEL_BLOCK, HEAD_DIM)
    kpos = (idx[..., None] * SEL_BLOCK + jnp.arange(SEL_BLOCK)).reshape(b, NSA_KV_HEADS, nq, n * SEL_BLOCK)
    s = jnp.einsum('bqghd,bgqkd->bghqk', q, ksel) * scale
    p = _masked_softmax(s, (kpos <= qpos[:, None])[:, :, None])
    o_slc = jnp.einsum('bghqk,bgqkd->bqghd', p, vsel)
    dist = qpos[:, None] - wpos[None, :]
    wmask = (dist >= 0) & (dist <= NSA_WINDOW) & (wpos[None, :] >= 0)
    s = jnp.einsum('bqghd,bkgd->bghqk', q, wk) * scale
    p = _masked_softmax(s, wmask)
    o_win = jnp.einsum('bghqk,bkgd->bqghd', p, wv)
    return o_cmp, o_slc, o_win


def _rope_rows_kernel(x_ref, cos_ref, sin_ref, o_ref):
    reps = x_ref.shape[1] // LANES
    o_ref[...] = _rope_lanes(x_ref[...], _lane_tile(cos_ref[...], reps), _lane_tile(sin_ref[...], reps))


def _rope_rows(x):
    l, w = x.shape
    tm = _row_tile(l)
    cos, sin = _rope_tables(jnp.arange(l), LANES)
    return pl.pallas_call(
        _rope_rows_kernel,
        grid=(l // tm,),
        in_specs=[pl.BlockSpec((tm, w), lambda i: (i, 0)), pl.BlockSpec((tm, LANES), lambda i: (i, 0)),
                  pl.BlockSpec((tm, LANES), lambda i: (i, 0))],
        out_specs=pl.BlockSpec((tm, w), lambda i: (i, 0)),
        out_shape=jax.ShapeDtypeStruct((l, w), jnp.float32),
        compiler_params=pltpu.CompilerParams(dimension_semantics=("arbitrary",), vmem_limit_bytes=VMEM_LIMIT_BYTES),
        name="rope_rows",
    )(x, cos, sin)


def _nsa_prompt(q, kc, vc, ks, vs, kw, vw, small, cw1, cb1, cw2, cpe):
    b, l = q.shape[:2]
    flat = lambda t: t.reshape(t.shape[1], -1)
    roped = _rope_rows(jnp.concatenate([flat(q), flat(ks), flat(kw)], axis=1))
    qr = roped[:, :NSA_Q_W]
    ksr = roped[:, NSA_Q_W:NSA_Q_W + NSA_KV_W].reshape(ks.shape)
    kwr = roped[:, NSA_Q_W + NSA_KV_W:].reshape(kw.shape)
    ck, cv = _nsa_compressed_kv(kc, vc, cw1, cb1, cw2, cpe)
    vsf = vs.astype(jnp.float32)
    vwf = vw.astype(jnp.float32)
    o_nsa = _nsa_prompt_attention(qr, small, flat(ck), flat(cv), flat(ksr), flat(vsf), flat(kwr), flat(vwf))
    keep = min(NSA_WINDOW, l)
    rows_cmp = jnp.stack([kc, vc], axis=2)
    rows_slc = jnp.stack([ksr, vsf], axis=2)
    rows_win = jnp.stack([kwr[:, l - keep:], vwf[:, l - keep:]], axis=2)
    return o_nsa[None], rows_cmp, rows_slc, rows_win


def _nsa_sample(q, kc, vc, ks, vs, kw, vw, gate, cmp_pool, slc_pool, layer_idx, win_buf, page_table,
                cw1, cb1, cw2, cpe):
    db, s = q.shape[:2]
    past = page_table.shape[1] * PAGE_SIZE
    wb = win_buf.shape[1]
    assert s == 1 and wb == NSA_WINDOW and past >= wb and past % (STEP_PAGES * PAGE_SIZE) == 0
    qpos = past + jnp.arange(s)
    qr = _rope(q, qpos)
    ckt, cvt = _nsa_sample_compress_t(cmp_pool, layer_idx, page_table, cw1, cb1, cw2, cpe)
    ksr = _rope(ks, qpos)
    vsf = vs.astype(jnp.float32)
    kwr = _rope(kw, qpos)
    vwf = vw.astype(jnp.float32)
    newrows = jnp.stack([t.reshape(db, NSA_KV_W) for t in (ksr, vsf, kwr, vwf)], axis=1)
    o_nsa, rows_win = _nsa_sample_attention_t(qr[:, 0], ckt, cvt, slc_pool, layer_idx, page_table, win_buf,
                                              newrows, gate.reshape(db, -1))
    rows_cmp = jnp.stack([kc, vc], axis=2)
    rows_slc = jnp.stack([ksr, vsf], axis=2)
    return o_nsa[:, None], rows_cmp, rows_slc, rows_win


def _gdn_chunked(q, k, v, g, beta, s0):
    b, l, h, dk = q.shape
    dv = v.shape[-1]
    c = GDN_CHUNK
    nch = l // c
    r = lambda a: jnp.moveaxis(a.reshape(b, nch, c, h, *a.shape[3:]), 3, 2)
    q, k, v, g, beta = r(q), r(k), r(v), r(g), r(beta)
    gc = jnp.cumsum(g, axis=-1)
    ii = jnp.arange(c)
    tri = ii[:, None] >= ii[None, :]
    strict = ii[:, None] > ii[None, :]
    diff = gc[..., :, None] - gc[..., None, :]
    gamma = jnp.where(tri, jnp.exp(jnp.where(tri, diff, 0.0)), 0.0)
    kb = k * beta[..., None]
    a_mat = jnp.where(strict, jnp.einsum('bnhik,bnhjk->bnhij', kb, k) * gamma, 0.0)
    eye = jnp.eye(c, dtype=jnp.float32)
    t_inv = lax.linalg.triangular_solve(eye + a_mat, jnp.broadcast_to(eye, a_mat.shape),
                                        left_side=True, lower=True, unit_diagonal=True)
    u = t_inv @ (v * beta[..., None])
    w = t_inv @ (kb * jnp.exp(gc)[..., None])
    qk = jnp.where(tri, jnp.einsum('bnhik,bnhjk->bnhij', q, k) * gamma, 0.0)
    qg = q * jnp.exp(gc)[..., None]
    kd = k * jnp.exp(gc[..., -1:] - gc)[..., None]
    glast = jnp.exp(gc[..., -1])

    def step(state, xs):
        qg_c, kd_c, u_c, w_c, qk_c, gl_c = xs
        v_new = u_c - jnp.einsum('bhck,bhkv->bhcv', w_c, state)
        o = jnp.einsum('bhck,bhkv->bhcv', qg_c, state) + jnp.einsum('bhij,bhjv->bhiv', qk_c, v_new)
        state = state * gl_c[..., None, None] + jnp.einsum('bhck,bhcv->bhkv', kd_c, v_new)
        return state, o

    xs = tuple(jnp.moveaxis(a, 1, 0) for a in (qg, kd, u, w, qk, glast))
    s_fin, o = lax.scan(step, s0, xs)
    o = jnp.moveaxis(jnp.moveaxis(o, 0, 1), 2, 3).reshape(b, l, h, dv)
    return o, s_fin


def _gdn_recurrent(q, k, v, g, beta, s0):
    def step(state, xs):
        q_t, k_t, v_t, g_t, b_t = xs
        state = state * jnp.exp(g_t)[..., None, None]
        v_t = (v_t - jnp.einsum('bhk,bhkv->bhv', k_t, state)) * b_t[..., None]
        state = state + jnp.einsum('bhk,bhv->bhkv', k_t, v_t)
        return state, jnp.einsum('bhk,bhkv->bhv', q_t, state)

    xs = tuple(jnp.moveaxis(a, 1, 0) for a in (q, k, v, g, beta))
    s_fin, o = lax.scan(step, s0, xs)
    return jnp.moveaxis(o, 0, 1), s_fin


def _gdn_mix(qkv, a, bt, z, conv_hist, s0, conv_w, a_log, dt_bias, norm_w, chunked):
    b, s = qkv.shape[:2]
    c, new_hist = _causal_dwconv(conv_hist, qkv, conv_w)
    c = jax.nn.silu(c.astype(jnp.float32))
    q, k, v = [t.reshape(b, s, GDN_HEADS, HEAD_DIM) for t in jnp.split(c, 3, axis=-1)]
    q = _l2_norm(q) * HEAD_DIM ** -0.5
    k = _l2_norm(k)
    beta = jax.nn.sigmoid(bt.astype(jnp.float32))
    g = -jnp.exp(a_log) * jax.nn.softplus(a.astype(jnp.float32) + dt_bias)
    s0 = s0.astype(jnp.float32)
    if chunked:
        o, s_fin = _gdn_chunked(q, k, v, g, beta, s0)
    else:
        o, s_fin = _gdn_recurrent(q, k, v, g, beta, s0)
    o = _rms_norm(o, norm_w) * jax.nn.silu(z.astype(jnp.float32).reshape(b, s, GDN_HEADS, HEAD_DIM))
    return o.reshape(b, s, GDN_W), new_hist, s_fin


def _even_merge(o_cmp, o_slc, o_win, gate, o_gdn):
    b, s = gate.shape[:2]
    gt = jax.nn.sigmoid(gate.astype(jnp.float32)).reshape(b, s, 3, NSA_HEADS, 1)
    o_nsa = gt[:, :, 0] * o_cmp + gt[:, :, 1] * o_slc + gt[:, :, 2] * o_win
    return jnp.concatenate([o_nsa.reshape(b, s, NSA_Q_W), o_gdn], axis=-1)


def _proj(x, w):
    b, s, d = x.shape
    n = w.shape[1]
    npad = -(-n // LANES) * LANES
    wp = jnp.pad(w, ((0, 0), (0, npad - n)))
    return _matmul(x.reshape(b * s, d), wp).reshape(b, s, npad)


def _even_prompt(x, w_in, cw1, cb1, cw2, cpe, conv_w, a_log, dt_bias, norm_w):
    b, l, _ = x.shape
    q, kc, vc, ks, vs, kw, vw, gate, qkv, a, bt, z = _split_cols(_proj(x, w_in), _even_widths())
    heads = lambda t: t.reshape(b, l, -1, HEAD_DIM)
    assert b == 1
    small = jnp.concatenate([gate, a, bt], axis=-1).reshape(l, -1)
    small = jnp.pad(small, ((0, 0), (0, LANES - small.shape[-1])))
    o_nsa, r_cmp, r_slc, r_win = _nsa_prompt(
        heads(q), heads(kc), heads(vc), heads(ks), heads(vs), heads(kw), heads(vw), small, cw1, cb1, cw2, cpe)
    o_gdn, s_fin = _gdn_prompt(qkv[0], small, z[0], conv_w, a_log, dt_bias, norm_w)
    conv_hist = qkv[:, l - (GDN_CONV - 1):]
    return jnp.concatenate([o_nsa, o_gdn[None]], axis=-1), r_cmp, r_slc, r_win, conv_hist, s_fin[None]


def _even_sample(x, cmp_pool, slc_pool, layer_idx, win_buf, conv_hist, s0, page_table,
                 w_in, cw1, cb1, cw2, cpe, conv_w, a_log, dt_bias, norm_w):
    b, s, _ = x.shape
    q, kc, vc, ks, vs, kw, vw, gate, qkv, a, bt, z = _split_cols(_proj(x, w_in), _even_widths())
    heads = lambda t: t.reshape(b, s, -1, HEAD_DIM)
    o_nsa, r_cmp, r_slc, r_win = _nsa_sample(
        heads(q), heads(kc), heads(vc), heads(ks), heads(vs), heads(kw), heads(vw), gate,
        cmp_pool, slc_pool, layer_idx, win_buf, page_table, cw1, cb1, cw2, cpe)
    o_gdn, new_hist, s_fin = _gdn_mix(qkv, a, bt, z, conv_hist, s0, conv_w, a_log, dt_bias, norm_w, False)
    return jnp.concatenate([o_nsa, o_gdn], axis=-1), r_cmp, r_slc, r_win, new_hist, s_fin


def _dilated_band_stats(q, k, v, d):
    b, l, h, dh = q.shape
    unit = d * DIL_BLOCK
    lp = -(-l // unit) * unit
    nb = lp // unit
    to_sub = lambda a: jnp.pad(a, ((0, 0), (0, lp - l), (0, 0), (0, 0))).reshape(b, nb, DIL_BLOCK, d, h, dh)
    qs, ks, vs = to_sub(q), to_sub(k), to_sub(v)
    prev = lambda a: jnp.concatenate([jnp.zeros_like(a[:, :1]), a[:, :-1]], axis=1)
    kk = jnp.concatenate([prev(ks), ks], axis=2)
    vv = jnp.concatenate([prev(vs), vs], axis=2)
    s = jnp.einsum('bnirhd,bnjrhd->bnrhij', qs, kk) * HEAD_DIM ** -0.5
    i = jnp.arange(DIL_BLOCK)
    j = jnp.arange(2 * DIL_BLOCK) - DIL_BLOCK
    dist = i[:, None] - j[None, :]
    sub_k = jnp.arange(nb)[:, None, None] * DIL_BLOCK + j[None, None, :]
    mask = (dist >= 0)[None] & (dist <= DIL_SPAN)[None] & (sub_k >= 0)
    s = jnp.where(mask[None, :, None, None], s, -jnp.inf)
    m = jnp.max(s, axis=-1)
    e = jnp.exp(s - m[..., None])
    den = jnp.sum(e, axis=-1)
    num = jnp.einsum('bnrhij,bnjrhd->bnrhid', e, vv)
    m = jnp.transpose(m, (0, 1, 4, 2, 3)).reshape(b, lp, h)[:, :l]
    den = jnp.transpose(den, (0, 1, 4, 2, 3)).reshape(b, lp, h)[:, :l]
    num = jnp.transpose(num, (0, 1, 4, 2, 3, 5)).reshape(b, lp, h, dh)[:, :l]
    return m, den, num


def _dilated_gather_stats(q, k_all, v_all, qpos, kpos0, d):
    kp = qpos[:, None] - jnp.arange(DIL_SPAN + 1)[None, :] * d
    idx = kp - kpos0
    valid = idx >= 0
    idxc = jnp.clip(idx, 0)
    kg, vg = k_all[:, idxc], v_all[:, idxc]
    s = jnp.einsum('bshd,bsmhd->bshm', q, kg) * HEAD_DIM ** -0.5
    s = jnp.where(valid[None, :, None, :], s, -jnp.inf)
    m = jnp.max(s, axis=-1)
    e = jnp.exp(s - m[..., None])
    return m, jnp.sum(e, axis=-1), jnp.einsum('bshm,bsmhd->bshd', e, vg)


def _combine_by_denominators(stats):
    m_all = stats[0][0]
    for m, _, _ in stats[1:]:
        m_all = jnp.maximum(m_all, m)
    num, den = None, None
    for m, dn, nm in stats:
        w = jnp.exp(m - m_all)
        num = w[..., None] * nm if num is None else num + w[..., None] * nm
        den = w * dn if den is None else den + w * dn
    return num / den[..., None]


def _dil_prompt(x, w_in, w_out, g, b):
    bsz, l, _ = x.shape
    assert bsz == 1
    qkv, kv = _proj_dil(x[0], w_in)
    stats = [_dil_band_stats_packed(qkv, d) for _, d in DIL_GROUPS]
    y = _dil_merge_proj([n for n, _ in stats], [s for _, s in stats], w_out, x[0], g, b)
    keep = min(DIL_MAX_WINDOW, l)
    buf = kv[l - keep:].reshape(1, keep, 2, DIL_HEADS, HEAD_DIM)
    return y, buf


def _dil_sample(x, buf, past, w_in):
    db, s, _ = x.shape
    q, k, v = [t.reshape(db, s, DIL_HEADS, HEAD_DIM) for t in jnp.split(_proj(x, w_in), 3, axis=-1)]
    assert s == 1 and buf.shape[1] == DIL_MAX_WINDOW <= past
    qpos = past + jnp.arange(s)
    qr, kr = _rope(q, qpos), _rope(k, qpos)
    o, new_buf = _dil_sample_attention(qr[:, 0], kr.reshape(db, DIL_W), v.reshape(db, DIL_W).astype(jnp.float32), buf)
    return o[:, None], new_buf


def kernel(x_prompt, x_sample, cache_nsa_cmp_kv, cache_nsa_slc_kv, state_nsa_win_kv, state_gdn_conv,
           state_gdn_S, state_dil_kv, state_ffn_conv, page_table, w_in_a, nsa_cmp_w1, nsa_cmp_b1, nsa_cmp_w2,
           nsa_cmp_pe, gdn_conv_w, gdn_A_log, gdn_dt_bias, gdn_norm_w, w_out_a, w_in_c, w_out_c,
           ln_mix_g, ln_mix_b, ffn_w_in, ffn_conv_w, ffn_conv_b, ffn_w_out, ln_ffn_g, ln_ffn_b):
    past = page_table.shape[1] * PAGE_SIZE
    bp, lp, d = x_prompt.shape
    bs, ls, _ = x_sample.shape
    assert bp == 1 and ls == 1
    xp, xs = x_prompt, x_sample
    cmp_p, cmp_s, slc_p, slc_s, win_p, win_s = [], [], [], [], [], []
    gconv_p, gconv_s, gstate_p, gstate_s = [], [], [], []
    dil_p, dil_s, ffn_p, ffn_s = [], [], [], []
    for layer in range(DEPTH):
        if layer % 2 == 0:
            la = layer // 2
            wa = (w_in_a[la], nsa_cmp_w1[la], nsa_cmp_b1[la], nsa_cmp_w2[la], nsa_cmp_pe[la],
                  gdn_conv_w[la], gdn_A_log[la], gdn_dt_bias[la], gdn_norm_w[la])
            mp, rc, rs, rw, hc, hs_ = _even_prompt(xp, *wa)
            cmp_p.append(rc); slc_p.append(rs); win_p.append(rw); gconv_p.append(hc); gstate_p.append(hs_)
            ms, rc, rs, rw, hc, hs_ = _even_sample(xs, cache_nsa_cmp_kv, cache_nsa_slc_kv, la,
                                                   state_nsa_win_kv[:, la], state_gdn_conv[:, la],
                                                   state_gdn_S[:, la], page_table, *wa)
            cmp_s.append(rc); slc_s.append(rs); win_s.append(rw); gconv_s.append(hc); gstate_s.append(hs_)
            w_out = w_out_a[la]
        else:
            lc = layer // 2
            xp2, bpf = _dil_prompt(xp, w_in_c[lc], w_out_c[lc], ln_mix_g[layer], ln_mix_b[layer])
            ms, bsf = _dil_sample(xs, state_dil_kv[:, lc], past, w_in_c[lc])
            dil_p.append(bpf); dil_s.append(bsf)
            w_out = w_out_c[lc]
        if layer % 2 == 0:
            xp2 = _matmul_ln(mp.reshape(lp, -1), w_out, xp.reshape(lp, d), ln_mix_g[layer], ln_mix_b[layer])
        xs2 = _matmul_ln(ms.reshape(bs, -1), w_out, xs.reshape(bs, d), ln_mix_g[layer], ln_mix_b[layer])
        fargs = (ffn_w_in[layer], ffn_conv_w[layer], ffn_conv_b[layer], ffn_w_out[layer],
                 ln_ffn_g[layer], ln_ffn_b[layer])
        xp3, hp = _ffn_seq(xp2, *fargs)
        xs3, hs = _ffn_step(xs2, state_ffn_conv[:, layer], *fargs)
        xp, xs = xp3.reshape(1, lp, d), xs3.reshape(bs, 1, d)
        ffn_p.append(hp[None]); ffn_s.append(hs)

    def stk(lst):
        return jnp.stack(lst, axis=1)

    return (xp, xs, stk(cmp_p), stk(cmp_s), stk(slc_p), stk(slc_s), stk(win_p), stk(win_s),
            stk(gconv_p), stk(gconv_s), stk(gstate_p), stk(gstate_s), stk(dil_p), stk(dil_s),
            stk(ffn_p), stk(ffn_s))
```

```python
import functools

import jax
import jax.numpy as jnp
from jax import lax
from jax.experimental import pallas as pl
from jax.experimental.pallas import tpu as pltpu
import numpy as np

DEPTH = 2
PAGE_SIZE = 128
HEAD_DIM = 64
ROPE_THETA = 10000.0
NSA_HEADS = 8
NSA_KV_HEADS = 2
NSA_GROUP = NSA_HEADS // NSA_KV_HEADS
CMP_BLOCK = 32
SEL_BLOCK = 64
NSA_TOPN = 16
NSA_WINDOW = 512
NSA_QBLOCK = 128
NSA_FORCE = 1.0e4
GDN_HEADS = 8
GDN_CONV = 4
GDN_CHUNK = 64
DIL_HEADS = 16
DIL_GROUPS = ((128, 1), (512, 4), (2048, 16))
DIL_SPAN = 128
DIL_BLOCK = 128
DIL_MAX_WINDOW = 2048
D_FF = 2816
FFN_CONV = 3
DEEPNORM_ALPHA = (2.0 * DEPTH) ** 0.25
LN_EPS = 1e-5
NORM_EPS = 1e-6
NSA_Q_W = NSA_HEADS * HEAD_DIM
NSA_KV_W = NSA_KV_HEADS * HEAD_DIM
GDN_W = GDN_HEADS * HEAD_DIM
DIL_W = DIL_HEADS * HEAD_DIM

LANES = 128
VMEM_LIMIT_BYTES = 56 * 1024 * 1024


def _layer_norm_rows(r, g, b):
    mu = jnp.mean(r, axis=-1, keepdims=True)
    d = r - mu
    var = jnp.mean(d * d, axis=-1, keepdims=True)
    return d * lax.rsqrt(var + LN_EPS) * g + b


def _mm_kernel(x_ref, w_ref, o_ref):
    o_ref[...] = jnp.dot(x_ref[...].astype(jnp.bfloat16), w_ref[...], preferred_element_type=jnp.float32)


def _mm_ln_kernel(x_ref, w_ref, res_ref, g_ref, b_ref, o_ref):
    acc = jnp.dot(x_ref[...].astype(jnp.bfloat16), w_ref[...], preferred_element_type=jnp.float32)
    o_ref[...] = _layer_norm_rows(DEEPNORM_ALPHA * res_ref[...] + acc, g_ref[...], b_ref[...])


def _row_tile(m):
    return 512 if m % 512 == 0 else m


def _matmul(x, w):
    m, k = x.shape
    n = w.shape[1]
    tm = _row_tile(m)
    tn = n
    for cand in (1152, 1024, 768, 512):
        if n % cand == 0:
            tn = cand
            break
    return pl.pallas_call(
        _mm_kernel,
        grid=(m // tm, n // tn),
        in_specs=[pl.BlockSpec((tm, k), lambda i, j: (i, 0)),
                  pl.BlockSpec((k, tn), lambda i, j: (0, j))],
        out_specs=pl.BlockSpec((tm, tn), lambda i, j: (i, j)),
        out_shape=jax.ShapeDtypeStruct((m, n), jnp.float32),
        compiler_params=pltpu.CompilerParams(dimension_semantics=("parallel", "arbitrary"),
                                             vmem_limit_bytes=VMEM_LIMIT_BYTES),
        name="matmul",
    )(x, w.astype(jnp.bfloat16))


def _matmul_ln(x, w, res, g, b):
    m, k = x.shape
    n = w.shape[1]
    tm = _row_tile(m)
    return pl.pallas_call(
        _mm_ln_kernel,
        grid=(m // tm,),
        in_specs=[pl.BlockSpec((tm, k), lambda i: (i, 0)),
                  pl.BlockSpec((k, n), lambda i: (0, 0)),
                  pl.BlockSpec((tm, n), lambda i: (i, 0)),
                  pl.BlockSpec((1, n), lambda i: (0, 0)),
                  pl.BlockSpec((1, n), lambda i: (0, 0))],
        out_specs=pl.BlockSpec((tm, n), lambda i: (i, 0)),
        out_shape=jax.ShapeDtypeStruct((m, n), jnp.float32),
        compiler_params=pltpu.CompilerParams(dimension_semantics=("arbitrary",),
                                             vmem_limit_bytes=VMEM_LIMIT_BYTES),
        name="matmul_ln",
    )(x, w.astype(jnp.bfloat16), res, g.reshape(1, n), b.reshape(1, n))


FFN_CHUNK = D_FF // 2
FFN_NCHUNK = D_FF // FFN_CHUNK


def _ffn_seq_kernel(x_ref, wa_ref, wg_ref, cwa_ref, cwg_ref, cba_ref, cbg_ref, wo_ref, lg_ref, lb_ref,
                    y_ref, ha_ref, hg_ref, acc_ref, carry_ref):
    i, j = pl.program_id(0), pl.program_id(1)
    tm = x_ref.shape[0]
    x = x_ref[...]
    xb = x.astype(jnp.bfloat16)

    @pl.when(i == 0)
    def _():
        carry_ref[j] = jnp.zeros(carry_ref.shape[1:], jnp.float32)

    def conv(u, cw_ref, cb_ref, slot):
        prev = carry_ref[j, slot]
        p2, p1 = prev[6:7], prev[7:8]
        row = lax.broadcasted_iota(jnp.int32, u.shape, 0)
        u1 = jnp.where(row == 0, p1, pltpu.roll(u, 1, 0))
        u2 = jnp.where(row == 0, p2, jnp.where(row == 1, p1, pltpu.roll(u, 2, 0)))
        carry_ref[j, slot] = u[tm - 8:]
        cw = cw_ref[...]
        return cw[0:1] * u2 + cw[1:2] * u1 + cw[2:3] * u + cb_ref[...]

    ua = jnp.dot(xb, wa_ref[...], preferred_element_type=jnp.float32)
    ug = jnp.dot(xb, wg_ref[...], preferred_element_type=jnp.float32)
    ha_ref[...] = ua[tm - 8:]
    hg_ref[...] = ug[tm - 8:]
    a = conv(ua, cwa_ref, cba_ref, 0)
    g = conv(ug, cwg_ref, cbg_ref, 1)
    h = (a * jax.nn.sigmoid(a) * g).astype(jnp.bfloat16)
    part = jnp.dot(h, wo_ref[...], preferred_element_type=jnp.float32)

    @pl.when(j == 0)
    def _():
        acc_ref[...] = part

    @pl.when(j > 0)
    def _():
        acc_ref[...] += part

    @pl.when(j == pl.num_programs(1) - 1)
    def _():
        y_ref[...] = _layer_norm_rows(DEEPNORM_ALPHA * x + acc_ref[...], lg_ref[...], lb_ref[...])


def _ffn_seq(x, w_in, conv_w, conv_b, w_out, ln_g, ln_b):
    l, d = x.shape
    tm = 512
    c, nc = FFN_CHUNK, FFN_NCHUNK
    w_in = w_in.astype(jnp.bfloat16)
    cw8 = jnp.zeros((8, 2 * D_FF), jnp.float32).at[:FFN_CONV].set(conv_w)
    cb = conv_b.reshape(1, 2 * D_FF)
    y, ha, hg = pl.pallas_call(
        _ffn_seq_kernel,
        grid=(l // tm, nc),
        in_specs=[pl.BlockSpec((tm, d), lambda i, j: (i, 0)),
                  pl.BlockSpec((d, c), lambda i, j: (0, j)),
                  pl.BlockSpec((d, c), lambda i, j: (0, j + nc)),
                  pl.BlockSpec((8, c), lambda i, j: (0, j)),
                  pl.BlockSpec((8, c), lambda i, j: (0, j + nc)),
                  pl.BlockSpec((1, c), lambda i, j: (0, j)),
                  pl.BlockSpec((1, c), lambda i, j: (0, j + nc)),
                  pl.BlockSpec((c, d), lambda i, j: (j, 0)),
                  pl.BlockSpec((1, d), lambda i, j: (0, 0)),
                  pl.BlockSpec((1, d), lambda i, j: (0, 0))],
        out_specs=[pl.BlockSpec((tm, d), lambda i, j: (i, 0)),
                   pl.BlockSpec((8, c), lambda i, j: (i, j)),
                   pl.BlockSpec((8, c), lambda i, j: (i, j))],
        out_shape=[jax.ShapeDtypeStruct((l, d), jnp.float32),
                   jax.ShapeDtypeStruct((l // tm * 8, D_FF), jnp.float32),
                   jax.ShapeDtypeStruct((l // tm * 8, D_FF), jnp.float32)],
        scratch_shapes=[pltpu.VMEM((tm, d), jnp.float32),
                        pltpu.VMEM((nc, 2, 8, c), jnp.float32)],
        compiler_params=pltpu.CompilerParams(dimension_semantics=("arbitrary", "arbitrary"),
                                             vmem_limit_bytes=VMEM_LIMIT_BYTES),
        name="ffn_seq",
    )(x, w_in, w_in, cw8, cw8, cb, cb, w_out.astype(jnp.bfloat16), ln_g.reshape(1, d), ln_b.reshape(1, d))
    hist = jnp.concatenate([ha[-(FFN_CONV - 1):], hg[-(FFN_CONV - 1):]], axis=-1)
    return y, hist


def _ffn_step_kernel(x_ref, h_ref, wa_ref, wg_ref, cwa_ref, cwg_ref, cba_ref, cbg_ref, wo_ref, lg_ref, lb_ref,
                     y_ref, ua_ref, ug_ref, acc_ref):
    j = pl.program_id(0)
    x = x_ref[...]
    xb = x.astype(jnp.bfloat16)
    ua = jnp.dot(xb, wa_ref[...], preferred_element_type=jnp.float32)
    ug = jnp.dot(xb, wg_ref[...], preferred_element_type=jnp.float32)
    ua_ref[...] = ua
    ug_ref[...] = ug
    cwa, cwg = cwa_ref[...], cwg_ref[...]
    a = cwa[0:1] * h_ref[0, 0] + cwa[1:2] * h_ref[1, 0] + cwa[2:3] * ua + cba_ref[...]
    g = cwg[0:1] * h_ref[0, 1] + cwg[1:2] * h_ref[1, 1] + cwg[2:3] * ug + cbg_ref[...]
    h = (a * jax.nn.sigmoid(a) * g).astype(jnp.bfloat16)
    part = jnp.dot(h, wo_ref[...], preferred_element_type=jnp.float32)

    @pl.when(j == 0)
    def _():
        acc_ref[...] = part

    @pl.when(j > 0)
    def _():
        acc_ref[...] += part

    @pl.when(j == pl.num_programs(0) - 1)
    def _():
        y_ref[...] = _layer_norm_rows(DEEPNORM_ALPHA * x + acc_ref[...], lg_ref[...], lb_ref[...])


def _ffn_step(x, hist, w_in, conv_w, conv_b, w_out, ln_g, ln_b):
    b, d = x.shape
    c, nc = FFN_CHUNK, FFN_NCHUNK
    w_in = w_in.astype(jnp.bfloat16)
    cw8 = jnp.zeros((8, 2 * D_FF), jnp.float32).at[:FFN_CONV].set(conv_w)
    cb = conv_b.reshape(1, 2 * D_FF)
    h4 = jnp.transpose(hist, (1, 0, 2)).reshape(2, b, 2, D_FF).transpose(0, 2, 1, 3)
    y, ua, ug = pl.pallas_call(
        _ffn_step_kernel,
        grid=(nc,),
        in_specs=[pl.BlockSpec((b, d), lambda j: (0, 0)),
                  pl.BlockSpec((2, 2, b, c), lambda j: (0, 0, 0, j)),
                  pl.BlockSpec((d, c), lambda j: (0, j)),
                  pl.BlockSpec((d, c), lambda j: (0, j + nc)),
                  pl.BlockSpec((8, c), lambda j: (0, j)),
                  pl.BlockSpec((8, c), lambda j: (0, j + nc)),
                  pl.BlockSpec((1, c), lambda j: (0, j)),
                  pl.BlockSpec((1, c), lambda j: (0, j + nc)),
                  pl.BlockSpec((c, d), lambda j: (j, 0)),
                  pl.BlockSpec((1, d), lambda j: (0, 0)),
                  pl.BlockSpec((1, d), lambda j: (0, 0))],
        out_specs=[pl.BlockSpec((b, d), lambda j: (0, 0)),
                   pl.BlockSpec((b, c), lambda j: (0, j)),
                   pl.BlockSpec((b, c), lambda j: (0, j))],
        out_shape=[jax.ShapeDtypeStruct((b, d), jnp.float32),
                   jax.ShapeDtypeStruct((b, D_FF), jnp.float32),
                   jax.ShapeDtypeStruct((b, D_FF), jnp.float32)],
        scratch_shapes=[pltpu.VMEM((b, d), jnp.float32)],
        compiler_params=pltpu.CompilerParams(dimension_semantics=("arbitrary",),
                                             vmem_limit_bytes=VMEM_LIMIT_BYTES),
        name="ffn_step",
    )(x, h4, w_in, w_in, cw8, cw8, cb, cb, w_out.astype(jnp.bfloat16), ln_g.reshape(1, d), ln_b.reshape(1, d))
    u = jnp.concatenate([ua, ug], axis=-1)
    return y, jnp.concatenate([hist[:, 1:], u[:, None]], axis=1)


NEG = -1e30
NSA_KT = 512
NSA_COLS = NSA_HEADS * NSA_QBLOCK
NSA_WSPAN = NSA_WINDOW + NSA_QBLOCK
LOG2E = 1.4426950408889634
NSA_VROWS = HEAD_DIM + 8


def _lane_tile(x, n):
    return jnp.concatenate([x] * n, axis=1)


def _nsa_prompt_kernel(q_ref, sm_ref, ck_ref, cvt_ref, ks_ref, vst_ref, kw_ref, vwt_ref, hot_ref, o_ref,
                       selb_ref, m_ref, acc_ref, *, ns):
    f32, bf16 = jnp.float32, jnp.bfloat16
    qb = NSA_QBLOCK
    i = pl.program_id(0)
    s0 = i * qb
    half = NSA_COLS // 2

    qt = (q_ref[...] * (HEAD_DIM ** -0.5 * LOG2E)).T
    zero = jnp.zeros((HEAD_DIM, qb), f32)
    top = jnp.concatenate([qt[h * HEAD_DIM:(h + 1) * HEAD_DIM] for h in range(NSA_GROUP)] + [zero] * NSA_GROUP, axis=1)
    bot = jnp.concatenate([zero] * NSA_GROUP + [qt[h * HEAD_DIM:(h + 1) * HEAD_DIM]
                                                for h in range(NSA_GROUP, NSA_HEADS)], axis=1)
    qbd = jnp.concatenate([top, bot], axis=0).astype(bf16)

    def pv(vt, p):
        pb = p.astype(bf16)
        rows = vt.shape[0] // NSA_KV_HEADS
        return [jnp.dot(vt[g * rows:(g + 1) * rows], pb[:, g * half:(g + 1) * half],
                        preferred_element_type=f32) for g in range(NSA_KV_HEADS)]

    nc = 2 * ns
    r = lax.broadcasted_iota(jnp.int32, (nc, qb), 0)
    lane = lax.broadcasted_iota(jnp.int32, (nc, qb), 1)
    cidx = jnp.where(r < ns, 2 * r, 2 * (r - ns) + 1)
    cbias = jnp.where((cidx + 1) * CMP_BLOCK - 1 <= s0 + lane, 0.0, NEG)
    sc = jnp.dot(ck_ref[...], qbd, preferred_element_type=f32) + _lane_tile(cbias, NSA_HEADS)
    m = jnp.max(sc, axis=0, keepdims=True)
    p = jnp.exp2(sc - m)
    pn = p * jnp.where(m > 0.5 * NEG, 1.0 / jnp.sum(p, axis=0, keepdims=True), 0.0)
    o_cmp = pv(cvt_ref[...], pn)

    blk = lax.broadcasted_iota(jnp.int32, (ns, qb), 0)
    qpos = s0 + lax.broadcasted_iota(jnp.int32, (ns, qb), 1)
    cur = qpos // SEL_BLOCK
    forced = (blk == 0) | (blk == cur) | (blk == cur - 1)
    for g in range(NSA_KV_HEADS):
        imp = pn[:, g * half:g * half + qb]
        for h in range(1, NSA_GROUP):
            imp = imp + pn[:, g * half + h * qb:g * half + (h + 1) * qb]
        imp = imp[:ns] + imp[ns:]
        val = jnp.where(blk > cur, -1.0, jnp.where(forced, NSA_FORCE, imp))
        bias = jnp.full((ns, qb), NEG, f32)
        for _ in range(min(NSA_TOPN, ns)):
            top = jnp.max(val, axis=0, keepdims=True)
            pick = jnp.min(jnp.where(val == top, blk, ns), axis=0, keepdims=True)
            hit = blk == pick
            bias = jnp.where(hit, 0.0, bias)
            val = jnp.where(hit, -jnp.inf, val)
        selb_ref[g] = bias

    m_ref[...] = jnp.full(m_ref.shape, NEG, f32)
    acc_ref[...] = jnp.zeros(acc_ref.shape, f32)
    per_tile = NSA_KT // SEL_BLOCK
    zpad = jnp.zeros((LANES - 16, NSA_COLS), bf16)

    def slc_tile(kt, causal):
        k0 = pl.multiple_of(kt * NSA_KT, NSA_KT)
        b0 = pl.multiple_of(kt * per_tile, per_tile)
        brow = jnp.concatenate([selb_ref[g, pl.ds(b0, per_tile), :] for g in range(NSA_KV_HEADS)
                                for _ in range(NSA_GROUP)], axis=1)
        brow = jnp.concatenate([brow, jnp.zeros((16 - per_tile, NSA_COLS), f32)], axis=0).astype(bf16)
        q_aug = jnp.concatenate([qbd, brow, zpad], axis=0)
        k_aug = jnp.concatenate([ks_ref[pl.ds(k0, NSA_KT), :], hot_ref[...]], axis=1)
        s = jnp.dot(k_aug, q_aug, preferred_element_type=f32)
        if causal:
            kpos = k0 + lax.broadcasted_iota(jnp.int32, (NSA_KT, qb), 0)
            qq = s0 + lax.broadcasted_iota(jnp.int32, (NSA_KT, qb), 1)
            s = s + _lane_tile(jnp.where(kpos <= qq, 0.0, NEG), NSA_HEADS)
        m_old = m_ref[...]
        m_new = jnp.maximum(m_old, jnp.max(s, axis=0, keepdims=True))
        alpha = jnp.exp2(m_old - m_new)
        p = jnp.exp2(s - m_new)
        m_ref[...] = m_new
        upd = pv(vst_ref[:, pl.ds(k0, NSA_KT)], p)
        for g in range(NSA_KV_HEADS):
            acc_ref[g] = acc_ref[g] * alpha[:, g * half:(g + 1) * half] + upd[g]

    kd = s0 // NSA_KT

    def body(j, carry):
        slc_tile(2 * j, False)
        slc_tile(2 * j + 1, False)
        return carry

    lax.fori_loop(0, kd // 2, body, 0)

    @pl.when(kd % 2 == 1)
    def _():
        slc_tile(kd - 1, False)

    slc_tile(kd, True)
    inv_slc = [1.0 / acc_ref[g, HEAD_DIM:HEAD_DIM + 1, :] for g in range(NSA_KV_HEADS)]

    w0 = pl.multiple_of(s0, qb)
    sw = jnp.dot(kw_ref[pl.ds(w0, NSA_WSPAN), :], qbd, preferred_element_type=f32)
    rr = lax.broadcasted_iota(jnp.int32, (NSA_WSPAN, qb), 0)
    qi = lax.broadcasted_iota(jnp.int32, (NSA_WSPAN, qb), 1)
    ok = (rr >= qi) & (rr <= qi + NSA_WINDOW) & (rr + s0 >= NSA_WINDOW)
    sw = sw + _lane_tile(jnp.where(ok, 0.0, NEG), NSA_HEADS)
    pw = jnp.exp2(sw - jnp.max(sw, axis=0, keepdims=True))
    o_win = pv(vwt_ref[:, pl.ds(w0, NSA_WSPAN)], pw)
    inv_win = [1.0 / o[HEAD_DIM:HEAD_DIM + 1] for o in o_win]

    gt = jax.nn.sigmoid(sm_ref[...].T)
    outs = []
    for h in range(NSA_HEADS):
        g, hg = divmod(h, NSA_GROUP)
        c0, c1 = hg * qb, (hg + 1) * qb
        g_cmp = gt[h:h + 1]
        g_slc = gt[NSA_HEADS + h:NSA_HEADS + h + 1] * inv_slc[g][:, c0:c1]
        g_win = gt[2 * NSA_HEADS + h:2 * NSA_HEADS + h + 1] * inv_win[g][:, c0:c1]
        outs.append(o_cmp[g][:, c0:c1] * g_cmp + acc_ref[g, :HEAD_DIM, c0:c1] * g_slc
                    + o_win[g][:HEAD_DIM, c0:c1] * g_win)
    o_ref[...] = jnp.concatenate(outs, axis=0).T


def _nsa_prompt_attention(qr, small, ck, cv, ksr, vs, kwr, vw):
    l = qr.shape[0]
    ns = l // SEL_BLOCK
    nc = 2 * ns
    bf16 = jnp.bfloat16
    perm = jnp.concatenate([jnp.arange(0, nc, 2), jnp.arange(1, nc, 2)])
    ckp = ck[perm].astype(bf16)
    cvt = cv[perm].T.astype(bf16)
    pad = jnp.zeros((NSA_WINDOW, NSA_KV_W), bf16)
    kwp = jnp.concatenate([pad, kwr.astype(bf16)], axis=0)
    def with_ones(vt):
        n = vt.shape[1]
        extra = jnp.concatenate([jnp.ones((1, n), bf16), jnp.zeros((NSA_VROWS - HEAD_DIM - 1, n), bf16)], axis=0)
        return jnp.concatenate([x for g in range(NSA_KV_HEADS) for x in (vt[g * HEAD_DIM:(g + 1) * HEAD_DIM], extra)], axis=0)

    vwt = with_ones(jnp.concatenate([pad, vw.astype(bf16)], axis=0).T)
    hot = (jnp.arange(NSA_KT)[:, None] // SEL_BLOCK == jnp.arange(LANES)[None, :]).astype(bf16)
    full = lambda a: pl.BlockSpec(a.shape, lambda i: (0,) * a.ndim)
    args = (qr, small, ckp, cvt, ksr.astype(bf16), with_ones(vs.T.astype(bf16)), kwp, vwt, hot)
    return pl.pallas_call(
        functools.partial(_nsa_prompt_kernel, ns=ns),
        grid=(l // NSA_QBLOCK,),
        in_specs=[pl.BlockSpec((NSA_QBLOCK, NSA_Q_W), lambda i: (i, 0)),
                  pl.BlockSpec((NSA_QBLOCK, LANES), lambda i: (i, 0))] + [full(a) for a in args[2:]],
        out_specs=pl.BlockSpec((NSA_QBLOCK, NSA_Q_W), lambda i: (i, 0)),
        out_shape=jax.ShapeDtypeStruct((l, NSA_Q_W), jnp.float32),
        scratch_shapes=[pltpu.VMEM((NSA_KV_HEADS, ns, NSA_QBLOCK), jnp.float32),
                        pltpu.VMEM((1, NSA_COLS), jnp.float32),
                        pltpu.VMEM((NSA_KV_HEADS, NSA_VROWS, NSA_COLS // 2), jnp.float32)],
        compiler_params=pltpu.CompilerParams(dimension_semantics=("arbitrary",),
                                             vmem_limit_bytes=VMEM_LIMIT_BYTES),
        name="nsa_prompt",
    )(*args)


CMP_HIDDEN = 2 * HEAD_DIM
CMP_PER_PAGE = PAGE_SIZE // CMP_BLOCK
CMP_PAGES_PER_STEP = 64
STEP_PAGES = 64
STEP_ROWS = 16


def _rope_tables(pos, width):
    half = HEAD_DIM // 2
    inv_freq = ROPE_THETA ** (-2.0 * jnp.arange(half, dtype=jnp.float32) / HEAD_DIM)
    ang = pos.astype(jnp.float32)[:, None] * inv_freq[None, :]
    cos, sin = jnp.cos(ang), jnp.sin(ang)
    reps = width // HEAD_DIM
    return (jnp.tile(jnp.concatenate([cos, cos], axis=1), (1, reps)),
            jnp.tile(jnp.concatenate([-sin, sin], axis=1), (1, reps)))


def _rope_lanes(x, cos, sin_signed):
    n = x.shape[-1]
    lane = lax.broadcasted_iota(jnp.int32, x.shape, x.ndim - 1)
    first = (lane % HEAD_DIM) < HEAD_DIM // 2
    partner = jnp.where(first, pltpu.roll(x, n - HEAD_DIM // 2, x.ndim - 1), pltpu.roll(x, HEAD_DIM // 2, x.ndim - 1))
    return x * cos + partner * sin_signed


def _nt_dot(a, b):
    return lax.dot_general(a, b, (((1,), (1,)), ((), ())), preferred_element_type=jnp.float32)


CMP_STEP_LANES = CMP_PAGES_PER_STEP * CMP_PER_PAGE
CMP_FEATURE_GROUP = 16


def _cmp_lane_blocks(n_pages):
    lane = np.arange(n_pages * CMP_PER_PAGE)
    step, rem = lane // CMP_STEP_LANES, lane % CMP_STEP_LANES
    j, pl_ = rem // CMP_PAGES_PER_STEP, rem % CMP_PAGES_PER_STEP
    return (step * CMP_PAGES_PER_STEP + pl_) * CMP_PER_PAGE + j


def _cmp_step_kernel(pt_ref, *refs):
    f32 = jnp.float32
    npg = CMP_PAGES_PER_STEP
    pages = refs[:npg]
    pe_ref, w1_ref, b1_ref, w2_ref, cos_ref, sin_ref, ck_ref, cv_ref, slab_ref = refs[npg:]
    outs = (ck_ref, cv_ref)
    for k, r in enumerate(pages):
        for s in range(2 * NSA_KV_HEADS):
            slab_ref[s, k * HEAD_DIM:(k + 1) * HEAD_DIM, :] = r[0, s]
    for kv in range(2):
        h = jnp.zeros((NSA_KV_HEADS * npg, CMP_PER_PAGE * CMP_HIDDEN), f32)
        for dg in range(HEAD_DIM // CMP_FEATURE_GROUP):
            x = jnp.concatenate(
                [jnp.concatenate([slab_ref[2 * kv + g, pl.ds(d, npg, stride=HEAD_DIM), :]
                                  for g in range(NSA_KV_HEADS)], axis=0) + pe_ref[kv, d]
                 for d in range(dg * CMP_FEATURE_GROUP, (dg + 1) * CMP_FEATURE_GROUP)], axis=1)
            h = h + jnp.dot(x.astype(jnp.bfloat16), w1_ref[kv, dg], preferred_element_type=f32)
        act = jax.nn.gelu(h + b1_ref[kv]).astype(jnp.bfloat16)
        ct = _nt_dot(w2_ref[kv], act)
        tile = jnp.concatenate(
            [jnp.concatenate([ct[j * HEAD_DIM:(j + 1) * HEAD_DIM, g * npg:(g + 1) * npg] for j in range(CMP_PER_PAGE)],
                             axis=1) for g in range(NSA_KV_HEADS)], axis=0)
        if kv == 0:
            row = lax.broadcasted_iota(jnp.int32, tile.shape, 0)
            n = tile.shape[0]
            partner = jnp.where((row % HEAD_DIM) < HEAD_DIM // 2, pltpu.roll(tile, n - HEAD_DIM // 2, 0),
                                pltpu.roll(tile, HEAD_DIM // 2, 0))
            tile = tile * cos_ref[...] + partner * sin_ref[...]
        outs[kv][0] = tile


def _compress_weights(cw1, cb1, cw2, cpe):
    eye = jnp.eye(CMP_PER_PAGE, dtype=jnp.float32)
    w1r = cw1.reshape(2, CMP_BLOCK, HEAD_DIM, CMP_HIDDEN)
    w1 = jnp.einsum('ktdn,ja->kdjtan', w1r, eye).reshape(
        2, HEAD_DIM // CMP_FEATURE_GROUP, CMP_FEATURE_GROUP * PAGE_SIZE, CMP_PER_PAGE * CMP_HIDDEN)
    b1 = jnp.tile(cb1, (1, CMP_PER_PAGE))[:, None, :]
    w2 = jnp.einsum('knd,ja->kjdan', cw2, eye).reshape(2, CMP_PER_PAGE * HEAD_DIM, CMP_PER_PAGE * CMP_HIDDEN)
    pe = jnp.tile(cpe.transpose(0, 2, 1), (1, 1, CMP_PER_PAGE))[:, :, None, :]
    return w1.astype(jnp.bfloat16), b1, w2.astype(jnp.bfloat16), pe


def _nsa_sample_compress(pool, layer_idx, page_table, cw1, cb1, cw2, cpe):
    n_pool, nl = pool.shape[:2]
    db, n_pages = page_table.shape
    npg = CMP_PAGES_PER_STEP
    nchunk = n_pages // npg
    nc = n_pages * CMP_PER_PAGE
    view = jnp.transpose(pool, (0, 1, 3, 4, 5, 2)).reshape(n_pool * nl, 2 * NSA_KV_HEADS, HEAD_DIM, PAGE_SIZE)
    pt = (page_table * nl + layer_idx).reshape(-1).astype(jnp.int32)
    w1, b1, w2, pe = _compress_weights(cw1, cb1, cw2, cpe)
    pos = (jnp.asarray(_cmp_lane_blocks(n_pages)) + 1) * CMP_BLOCK - 1
    half = HEAD_DIM // 2
    inv_freq = ROPE_THETA ** (-2.0 * jnp.arange(half, dtype=jnp.float32) / HEAD_DIM)
    ang = inv_freq[:, None] * pos.astype(jnp.float32)[None, :]
    cos = jnp.tile(jnp.cos(ang), (2 * NSA_KV_HEADS, 1))
    sin = jnp.tile(jnp.concatenate([-jnp.sin(ang), jnp.sin(ang)], axis=0), (NSA_KV_HEADS, 1))

    def page_map(k):
        return lambda b, c, pt_ref: (pt_ref[b * n_pages + c * npg + k], 0, 0, 0)

    const = lambda a: pl.BlockSpec(a.shape, lambda b, c, pt_ref: (0,) * a.ndim, pipeline_mode=pl.Buffered(1))
    lanes_c = lambda: pl.BlockSpec((NSA_KV_W, CMP_STEP_LANES), lambda b, c, pt_ref: (0, c))
    grid_spec = pltpu.PrefetchScalarGridSpec(
        num_scalar_prefetch=1, grid=(db, nchunk),
        in_specs=[pl.BlockSpec((1, 2 * NSA_KV_HEADS, HEAD_DIM, PAGE_SIZE), page_map(k)) for k in range(npg)]
        + [const(pe), const(w1), const(b1), const(w2), lanes_c(), lanes_c()],
        out_specs=[pl.BlockSpec((1, NSA_KV_W, CMP_STEP_LANES), lambda b, c, pt_ref: (b, 0, c))] * 2,
        scratch_shapes=[pltpu.VMEM((2 * NSA_KV_HEADS, npg * HEAD_DIM, PAGE_SIZE), jnp.float32)])
    return pl.pallas_call(
        _cmp_step_kernel, grid_spec=grid_spec,
        out_shape=[jax.ShapeDtypeStruct((db, NSA_KV_W, nc), jnp.float32)] * 2,
        compiler_params=pltpu.CompilerParams(dimension_semantics=("arbitrary", "arbitrary"),
                                             vmem_limit_bytes=VMEM_LIMIT_BYTES),
        name="nsa_sample_compress",
    )(pt, *([view] * npg), pe, w1, b1, w2, cos, sin)


def _nsa_step_kernel(pt_ref, *refs, topn):
    f32, bf16 = jnp.float32, jnp.bfloat16
    npg = STEP_PAGES
    q_ref, ck_ref, cv_ref = refs[:3]
    pages = refs[3:3 + npg]
    (win_ref, newr_ref, newt_ref, gate_ref, blk_ref, exp_ref, o_ref, wout_ref,
     selt_ref, m_ref, l_ref, acc_ref, side_ref) = refs[3 + npg:]
    cc = pl.program_id(1)
    q16 = q_ref[0]
    qb = q16.astype(bf16)
    row16 = lax.broadcasted_iota(jnp.int32, (STEP_ROWS, 1), 0)

    def new_key_scores(krow):
        return jnp.sum(q16 * krow, axis=1, keepdims=True)

    @pl.when(cc == 0)
    def _():
        nc = ck_ref.shape[2]
        s = jnp.dot(qb, ck_ref[0].astype(bf16), preferred_element_type=f32)
        p = jnp.exp(s - jnp.max(s, axis=1, keepdims=True))
        pn = p / jnp.sum(p, axis=1, keepdims=True)
        o_cmp = _nt_dot(pn.astype(bf16), cv_ref[0].astype(bf16))

        rowp = lax.broadcasted_iota(jnp.int32, pn.shape, 0)
        row8 = lax.broadcasted_iota(jnp.int32, (8, nc), 0)
        blk = blk_ref[...]
        val = jnp.full((8, nc), -jnp.inf, f32)
        for g in range(NSA_KV_HEADS):
            ig = jnp.sum(jnp.where((rowp >= g * NSA_GROUP) & (rowp < (g + 1) * NSA_GROUP), pn, 0.0),
                         axis=0, keepdims=True)
            ig = ig + pltpu.roll(ig, nc - CMP_PAGES_PER_STEP, 1)
            vg = jnp.where(blk[:1] < 0, -jnp.inf, jnp.where(blk[1:2] > 0, NSA_FORCE, ig))
            val = jnp.where(row8 == g, vg, val)
        sblk = jnp.where(blk[:1] < 0, nc, blk[:1])
        sel = jnp.zeros((8, nc), f32)
        for _ in range(topn):
            top = jnp.max(val, axis=1, keepdims=True)
            pick = jnp.min(jnp.where(val == top, sblk, nc), axis=1, keepdims=True)
            hit = sblk == pick
            sel = jnp.where(hit, 1.0, sel)
            val = jnp.where(hit, -jnp.inf, val)
        selh = jnp.where(row16 < NSA_GROUP, sel[0:1], jnp.where(row16 < NSA_HEADS, sel[1:2], 0.0))
        wsel = selt_ref.shape[2]
        for j in range(selt_ref.shape[0]):
            selt_ref[j] = selh[:, j * wsel:(j + 1) * wsel]

        m_ref[...] = new_key_scores(newr_ref[0, 0:1, :])
        l_ref[...] = jnp.ones(l_ref.shape, f32)
        acc_ref[...] = jnp.broadcast_to(newr_ref[0, 1:2, :], acc_ref.shape)

        sw = jnp.dot(qb, win_ref[0, 0].astype(bf16), preferred_element_type=f32)
        sn = new_key_scores(newr_ref[0, 2:3, :])
        mw = jnp.maximum(jnp.max(sw, axis=1, keepdims=True), sn)
        pw, pnw = jnp.exp(sw - mw), jnp.exp(sn - mw)
        lw = jnp.sum(pw, axis=1, keepdims=True) + pnw
        o_win = (_nt_dot(pw.astype(bf16), win_ref[0, 1].astype(bf16)) + pnw * newr_ref[0, 3:4, :]) / lw
        gt = jax.nn.sigmoid(gate_ref[0])
        side_ref[...] = gt[:, 0:1] * o_cmp + gt[:, 2:3] * o_win
        wl = win_ref.shape[3]
        lane = lax.broadcasted_iota(jnp.int32, (NSA_KV_W, wl), 1)
        for kv in range(2):
            wout_ref[0, kv] = jnp.where(lane == wl - 1, newt_ref[0, :, 2 + kv:3 + kv],
                                        pltpu.roll(win_ref[0, kv], wl - 1, 1))

    kt = jnp.concatenate([r[0, 0] for r in pages], axis=1)
    vt = jnp.concatenate([r[0, 1] for r in pages], axis=1)
    s = jnp.dot(qb, kt.astype(bf16), preferred_element_type=f32)
    picked = jnp.dot(selt_ref[cc].astype(bf16), exp_ref[...], preferred_element_type=f32)
    s = s + (picked - 1.0) * (-NEG)
    m_old = m_ref[...]
    m_new = jnp.maximum(m_old, jnp.max(s, axis=1, keepdims=True))
    alpha = jnp.exp(m_old - m_new)
    p = jnp.exp(s - m_new)
    m_ref[...] = m_new
    l_ref[...] = l_ref[...] * alpha + jnp.sum(p, axis=1, keepdims=True)
    acc_ref[...] = acc_ref[...] * alpha + _nt_dot(p.astype(bf16), vt.astype(bf16))

    @pl.when(cc == pl.num_programs(1) - 1)
    def _():
        gt = jax.nn.sigmoid(gate_ref[0])
        o_ref[0] = side_ref[...] + gt[:, 1:2] * acc_ref[...] / l_ref[...]


def _nsa_sample_attention(qr, ckt, cvt, slc_pool, layer_idx, page_table, win_buf, newrows, gate):
    db, n_pages = page_table.shape
    n_pool, nl = slc_pool.shape[:2]
    wlen = win_buf.shape[1]
    nc = ckt.shape[2]
    past = n_pages * PAGE_SIZE
    cur = past // SEL_BLOCK
    npg = STEP_PAGES
    nchunk = n_pages // npg
    keys = npg * PAGE_SIZE
    wsel = npg * CMP_PER_PAGE
    f32, bf16 = jnp.float32, jnp.bfloat16
    view = jnp.transpose(slc_pool, (0, 1, 3, 4, 5, 2)).reshape(n_pool * nl, 2, NSA_KV_W, PAGE_SIZE)
    wint = jnp.transpose(win_buf, (0, 2, 3, 4, 1)).reshape(db, 2, NSA_KV_W, wlen)
    pt = (page_table * nl + layer_idx).reshape(-1).astype(jnp.int32)
    hmask = (jnp.arange(NSA_HEADS)[:, None] // NSA_GROUP == jnp.arange(NSA_KV_HEADS)[None, :]).astype(f32)
    q16 = (qr * HEAD_DIM ** -0.5)[:, :, None, :] * hmask[None, :, :, None]
    q16 = jnp.pad(q16.reshape(db, NSA_HEADS, NSA_KV_W), ((0, 0), (0, STEP_ROWS - NSA_HEADS), (0, 0)))
    newr = jnp.pad(newrows, ((0, 0), (0, 8 - newrows.shape[1]), (0, 0)))
    newt = jnp.pad(newrows.transpose(0, 2, 1), ((0, 0), (0, 0), (0, LANES - newrows.shape[1])))
    g16 = jnp.pad(gate.reshape(db, 3, NSA_HEADS).transpose(0, 2, 1),
                  ((0, 0), (0, STEP_ROWS - NSA_HEADS), (0, LANES - 3)))
    cblk = _cmp_lane_blocks(n_pages)
    jj = cblk % CMP_PER_PAGE
    sblk = np.where(jj % 2 == 0, cblk // 2, -1)
    forced = ((sblk == 0) | (sblk == cur - 1)).astype(np.int32)
    blk8 = np.zeros((8, nc), np.int32)
    blk8[0], blk8[1] = sblk, forced
    loc = np.arange(wsel)
    lstep, lrem = loc // CMP_STEP_LANES, loc % CMP_STEP_LANES
    lj, lpage = lrem // CMP_PAGES_PER_STEP, lstep * CMP_PAGES_PER_STEP + lrem % CMP_PAGES_PER_STEP
    kidx = np.arange(keys)
    expand = ((lj[:, None] % 2 == 0) & (kidx[None, :] // PAGE_SIZE == lpage[:, None])
              & ((kidx[None, :] % PAGE_SIZE) // SEL_BLOCK == lj[:, None] // 2)).astype(np.float32)
    topn = min(NSA_TOPN, cur + 1) - 1

    def page_map(k):
        return lambda b, c, pt_ref: (pt_ref[b * n_pages + c * npg + k], 0, 0, 0)

    per_b = lambda shp: pl.BlockSpec((1,) + shp, lambda b, c, pt_ref: (b,) + (0,) * len(shp))
    const = lambda a: pl.BlockSpec(a.shape, lambda b, c, pt_ref: (0,) * a.ndim)
    consts = (jnp.asarray(blk8), jnp.asarray(expand, bf16))
    grid_spec = pltpu.PrefetchScalarGridSpec(
        num_scalar_prefetch=1, grid=(db, nchunk),
        in_specs=[per_b((STEP_ROWS, NSA_KV_W)), per_b((NSA_KV_W, nc)), per_b((NSA_KV_W, nc))]
        + [pl.BlockSpec((1, 2, NSA_KV_W, PAGE_SIZE), page_map(k)) for k in range(npg)]
        + [per_b((2, NSA_KV_W, wlen)), per_b((8, NSA_KV_W)), per_b((NSA_KV_W, LANES)), per_b((STEP_ROWS, LANES))]
        + [const(a) for a in consts],
        out_specs=[per_b((STEP_ROWS, NSA_KV_W)), per_b((2, NSA_KV_W, wlen))],
        scratch_shapes=[pltpu.VMEM((nc // wsel, STEP_ROWS, wsel), f32),
                        pltpu.VMEM((STEP_ROWS, 1), f32), pltpu.VMEM((STEP_ROWS, 1), f32),
                        pltpu.VMEM((STEP_ROWS, NSA_KV_W), f32), pltpu.VMEM((STEP_ROWS, NSA_KV_W), f32)])
    o16, wout = pl.pallas_call(
        functools.partial(_nsa_step_kernel, topn=topn), grid_spec=grid_spec,
        out_shape=[jax.ShapeDtypeStruct((db, STEP_ROWS, NSA_KV_W), f32),
                   jax.ShapeDtypeStruct((db, 2, NSA_KV_W, wlen), f32)],
        compiler_params=pltpu.CompilerParams(dimension_semantics=("arbitrary", "arbitrary"),
                                             vmem_limit_bytes=VMEM_LIMIT_BYTES),
        name="nsa_sample_attention",
    )(pt, q16, ckt, cvt, *([view] * npg), wint, newr, newt, g16, *consts)
    o = o16[:, :NSA_HEADS].reshape(db, NSA_HEADS, NSA_KV_HEADS, HEAD_DIM)
    o = jnp.take_along_axis(o, (jnp.arange(NSA_HEADS) // NSA_GROUP)[None, :, None, None], axis=2)
    wout = jnp.transpose(wout.reshape(db, 2, NSA_KV_HEADS, HEAD_DIM, wlen), (0, 4, 1, 2, 3))
    return o.reshape(db, NSA_HEADS * HEAD_DIM), wout


DIL_ROW_CHUNK = 64


def _dil_step_kernel(q_ref, buf_ref, newt_ref, newr_ref, bias_ref, o_ref, out_ref, p_ref, pn_ref, den_ref):
    f32, bf16 = jnp.float32, jnp.bfloat16
    kv = pl.program_id(1)
    wlen = buf_ref.shape[3]
    nrow = buf_ref.shape[2]
    q16 = q_ref[0]

    @pl.when(kv == 0)
    def _():
        s = jnp.dot(q16.astype(bf16), buf_ref[0, 0].astype(bf16), preferred_element_type=f32)
        s_new = jnp.sum(q16 * newr_ref[0, 0:1, :], axis=1, keepdims=True)
        ms, es, ens, dens = [], [], [], []
        for g in range(len(DIL_GROUPS)):
            sg = s + bias_ref[g:g + 1, :]
            m = jnp.maximum(jnp.max(sg, axis=1, keepdims=True), s_new)
            e, en = jnp.exp(sg - m), jnp.exp(s_new - m)
            ms.append(m); es.append(e); ens.append(en)
            dens.append(jnp.sum(e, axis=1, keepdims=True) + en)
        m_all = functools.reduce(jnp.maximum, ms)
        ws = [jnp.exp(m - m_all) for m in ms]
        p_ref[...] = sum(w * e for w, e in zip(ws, es))
        pn_ref[...] = sum(w * en for w, en in zip(ws, ens))
        den_ref[...] = sum(w * d for w, d in zip(ws, dens))

    @pl.when(kv == 1)
    def _():
        r = _nt_dot(p_ref[...].astype(bf16), buf_ref[0, 0].astype(bf16))
        r = (r + pn_ref[...] * newr_ref[0, 1:2, :]) / den_ref[...]
        head = lax.broadcasted_iota(jnp.int32, r.shape, 1) // HEAD_DIM
        row = lax.broadcasted_iota(jnp.int32, r.shape, 0)
        o_ref[0] = jnp.broadcast_to(jnp.sum(jnp.where(head == row, r, 0.0), axis=0, keepdims=True), o_ref.shape[1:])

    lane = lax.broadcasted_iota(jnp.int32, (DIL_ROW_CHUNK, wlen), 1)
    for c in range(nrow // DIL_ROW_CHUNK):
        rs = slice(c * DIL_ROW_CHUNK, (c + 1) * DIL_ROW_CHUNK)
        col = jnp.where(kv == 0, newt_ref[0, rs, 0:1], newt_ref[0, rs, 1:2])
        out_ref[0, 0, rs, :] = jnp.where(lane == wlen - 1, col, pltpu.roll(buf_ref[0, 0, rs, :], wlen - 1, 1))


def _dil_sample_attention(qr, kr_new, v_new, buf):
    db, wlen = buf.shape[:2]
    f32 = jnp.float32
    buft = jnp.transpose(buf, (0, 2, 3, 4, 1)).reshape(db, 2, DIL_W, wlen)
    eye = jnp.eye(DIL_HEADS, dtype=f32)
    q16 = ((qr * HEAD_DIM ** -0.5)[:, :, None, :] * eye[None, :, :, None]).reshape(db, DIL_HEADS, DIL_W)
    newr = jnp.pad(jnp.stack([kr_new, v_new], axis=1), ((0, 0), (0, 6), (0, 0)))
    newt = jnp.pad(jnp.stack([kr_new, v_new], axis=2), ((0, 0), (0, 0), (0, LANES - 2)))
    back = wlen - jnp.arange(wlen)
    bias = jnp.stack([jnp.where((back % d == 0) & (back // d <= DIL_SPAN), 0.0, NEG) for _, d in DIL_GROUPS])
    bias = jnp.pad(bias, ((0, 8 - len(DIL_GROUPS)), (0, 0))).astype(f32)
    o, new_buf = pl.pallas_call(
        _dil_step_kernel,
        grid=(db, 2),
        in_specs=[pl.BlockSpec((1, DIL_HEADS, DIL_W), lambda b, k: (b, 0, 0)),
                  pl.BlockSpec((1, 1, DIL_W, wlen), lambda b, k: (b, k, 0, 0)),
                  pl.BlockSpec((1, DIL_W, LANES), lambda b, k: (b, 0, 0)),
                  pl.BlockSpec((1, 8, DIL_W), lambda b, k: (b, 0, 0)),
                  pl.BlockSpec((8, wlen), lambda b, k: (0, 0))],
        out_specs=[pl.BlockSpec((1, 8, DIL_W), lambda b, k: (b, 0, 0)),
                   pl.BlockSpec((1, 1, DIL_W, wlen), lambda b, k: (b, k, 0, 0))],
        out_shape=[jax.ShapeDtypeStruct((db, 8, DIL_W), f32),
                   jax.ShapeDtypeStruct((db, 2, DIL_W, wlen), f32)],
        scratch_shapes=[pltpu.VMEM((DIL_HEADS, wlen), f32), pltpu.VMEM((DIL_HEADS, 1), f32),
                        pltpu.VMEM((DIL_HEADS, 1), f32)],
        compiler_params=pltpu.CompilerParams(dimension_semantics=("arbitrary", "arbitrary"),
                                             vmem_limit_bytes=VMEM_LIMIT_BYTES),
        name="dil_sample",
    )(q16, buft, newt, newr, bias)
    new_buf = jnp.transpose(new_buf.reshape(db, 2, DIL_HEADS, HEAD_DIM, wlen), (0, 4, 1, 2, 3))
    return o[:, 0], new_buf


def _dil_band_kernel(q_ref, kp_ref, kc_ref, vp_ref, vc_ref, num_ref, st_ref):
    f32, bf16 = jnp.float32, jnp.bfloat16
    blk = DIL_BLOCK
    n = pl.program_id(0)
    i = lax.broadcasted_iota(jnp.int32, (blk, 2 * blk), 0)
    j = lax.broadcasted_iota(jnp.int32, (blk, 2 * blk), 1) - blk
    ok = (i - j >= 0) & (i - j <= DIL_SPAN) & (n * blk + j >= 0)
    bias = jnp.where(ok, 0.0, NEG)
    bias = jnp.concatenate([bias, bias], axis=0)
    lane = lax.broadcasted_iota(jnp.int32, (blk, LANES), 1)
    first = lane < HEAD_DIM
    stats = jnp.zeros((blk, LANES), f32)
    for p in range(DIL_HEADS // 2):
        cols = slice(p * LANES, (p + 1) * LANES)
        qp = q_ref[:, cols] * (HEAD_DIM ** -0.5)
        qst = jnp.concatenate([jnp.where(first, qp, 0.0), jnp.where(first, 0.0, qp)], axis=0).astype(bf16)
        kk = jnp.concatenate([kp_ref[:, cols], kc_ref[:, cols]], axis=0).astype(bf16)
        vv = jnp.concatenate([vp_ref[:, cols], vc_ref[:, cols]], axis=0).astype(bf16)
        s = _nt_dot(qst, kk) + bias
        m = jnp.max(s, axis=1, keepdims=True)
        e = jnp.exp(s - m)
        den = jnp.sum(e, axis=1, keepdims=True)
        nm = jnp.dot(e.astype(bf16), vv, preferred_element_type=f32)
        num_ref[:, cols] = jnp.where(first, nm[:blk], nm[blk:])
        for a in range(2):
            h = 2 * p + a
            stats = jnp.where(lane == h, m[a * blk:(a + 1) * blk], stats)
            stats = jnp.where(lane == DIL_HEADS + h, den[a * blk:(a + 1) * blk], stats)
    st_ref[...] = stats


GDN_PREP_ROWS = 512
GDN_TILE_CHUNKS = 4
GDN_PAIRS = GDN_HEADS // 2
GDN_A_LANE = 3 * NSA_HEADS
GDN_B_LANE = GDN_A_LANE + GDN_HEADS


def _hi_lo(x):
    hi = x.astype(jnp.bfloat16)
    return hi, (x - hi.astype(jnp.float32)).astype(jnp.bfloat16)


def _three_way(x):
    f32 = jnp.float32
    x1 = x.astype(jnp.bfloat16)
    r1 = x - x1.astype(f32)
    x2 = r1.astype(jnp.bfloat16)
    return x1, x2, (r1 - x2.astype(f32)).astype(jnp.bfloat16)


def _dot_select(x, sel):
    return sum(jnp.dot(piece, sel, preferred_element_type=jnp.float32) for piece in _three_way(x))


def _select_dot(sel, x):
    return sum(jnp.dot(sel, piece, preferred_element_type=jnp.float32) for piece in _three_way(x))


def _dot_hl(a, b):
    f32 = jnp.float32
    ah, al = _hi_lo(a)
    bh, bl = _hi_lo(b)
    return (jnp.dot(ah, bh, preferred_element_type=f32) + jnp.dot(ah, bl, preferred_element_type=f32)
            + jnp.dot(al, bh, preferred_element_type=f32))


def _gdn_prep_kernel(u_ref, sm_ref, cw_ref, prm_ref, ea_ref, eb_ref, eh_ref, q_ref, k_ref, v_ref, g_ref, b_ref,
                     carry_ref):
    f32 = jnp.float32
    i = pl.program_id(0)
    tl = u_ref.shape[0]

    @pl.when(i == 0)
    def _():
        carry_ref[...] = jnp.zeros(carry_ref.shape, f32)

    u = u_ref[...]
    prev = carry_ref[...]
    row = lax.broadcasted_iota(jnp.int32, u.shape, 0)

    def shifted(k):
        r = pltpu.roll(u, k, 0)
        for j in range(k):
            r = jnp.where(row == j, prev[8 - k + j:8 - k + j + 1], r)
        return r

    cw = cw_ref[...]
    c = cw[0:1] * shifted(3) + cw[1:2] * shifted(2) + cw[2:3] * shifted(1) + cw[3:4] * u
    carry_ref[...] = u[tl - 8:]
    c = c * jax.nn.sigmoid(c)
    eh = eh_ref[...]

    def l2n(x):
        return x * lax.rsqrt(_dot_select(x * x, eh) + NORM_EPS)

    q_ref[...] = l2n(c[:, :GDN_W]) * (HEAD_DIM ** -0.5)
    k_ref[...] = l2n(c[:, GDN_W:2 * GDN_W])
    v_ref[...] = c[:, 2 * GDN_W:]
    sm = sm_ref[...]
    x = sm + prm_ref[1:2]
    softplus = jnp.maximum(x, 0.0) + jnp.log(1.0 + jnp.exp(-jnp.abs(x)))
    g_ref[...] = _dot_select(-jnp.exp(prm_ref[0:1]) * softplus, ea_ref[...])
    b_ref[...] = _dot_select(jax.nn.sigmoid(sm), eb_ref[...])


def _gdn_chunk_kernel(q_ref, k_ref, v_ref, g_ref, b_ref, z_ref, nw_ref, lt_ref, eh_ref, o_ref, s_out_ref, s_ref):
    f32 = jnp.float32
    ch = GDN_CHUNK
    i = pl.program_id(0)

    @pl.when(i == 0)
    def _():
        s_ref[...] = jnp.zeros(s_ref.shape, f32)

    lane = lax.broadcasted_iota(jnp.int32, (ch, LANES), 1)
    first = lane < HEAD_DIM
    stack = lambda x: jnp.concatenate([jnp.where(first, x, 0.0), jnp.where(first, 0.0, x)], axis=0)
    r2 = lax.broadcasted_iota(jnp.int32, (2 * ch, 2 * ch), 0)
    c2 = lax.broadcasted_iota(jnp.int32, (2 * ch, 2 * ch), 1)
    same = (r2 // ch) == (c2 // ch)
    tri = same & (r2 % ch >= c2 % ch)
    strict = same & (r2 % ch > c2 % ch)
    eye = r2 == c2
    eye_f = jnp.where(eye, 1.0, 0.0)
    diag2 = lax.broadcasted_iota(jnp.int32, (ch, LANES), 0) == lane % HEAD_DIM
    lt = lt_ref[...]
    bf = lambda x: x.astype(jnp.bfloat16)
    dot = lambda a, b: jnp.dot(bf(a), bf(b), preferred_element_type=f32)

    blocks = [(c, p) for c in range(GDN_TILE_CHUNKS) for p in range(GDN_PAIRS)]
    ld = lambda ref, c, p: ref[c * ch:(c + 1) * ch, p * LANES:(p + 1) * LANES]
    gcs = [_select_dot(lt, ld(g_ref, c, p)) for c, p in blocks]
    amats, qks, rhs_u, rhs_w, qgs, kds, decs = [], [], [], [], [], [], []
    for (c, p), gc in zip(blocks, gcs):
        kk, qq, vv, bb = ld(k_ref, c, p), ld(q_ref, c, p), ld(v_ref, c, p), ld(b_ref, c, p)
        eg = jnp.exp(gc)
        g_end = gc[ch - 1:ch]
        kb = kk * bb
        col = jnp.concatenate([jnp.broadcast_to(gc[:, 0:1], (ch, LANES)),
                               jnp.broadcast_to(gc[:, HEAD_DIM:HEAD_DIM + 1], (ch, LANES))], axis=0)
        rowv = jnp.sum(jnp.where(diag2, gc, 0.0), axis=0, keepdims=True)
        gam = jnp.where(tri, jnp.exp(jnp.where(tri, col - rowv, 0.0)), 0.0)
        kst = stack(kk)
        amats.append(jnp.where(strict, _nt_dot(bf(stack(kb)), bf(kst)) * gam, 0.0))
        qks.append(jnp.where(tri, _nt_dot(bf(stack(qq)), bf(kst)) * gam, 0.0))
        rhs_u.append(stack(vv * bb))
        rhs_w.append(stack(kb * eg))
        qgs.append(stack(qq * eg))
        kds.append(stack(kk * jnp.exp(g_end - gc)))
        decs.append(jnp.sum(jnp.where(eye, jnp.exp(g_end), 0.0), axis=1, keepdims=True))
    xs = [eye_f - a for a in amats]
    pws = [_dot_hl(a, a) for a in amats]
    steps = GDN_CHUNK.bit_length() - 2
    for r in range(steps):
        xs = [x + _dot_hl(x, pw) for x, pw in zip(xs, pws)]
        if r < steps - 1:
            pws = [_dot_hl(pw, pw) for pw in pws]
    uus = [dot(x, u) for x, u in zip(xs, rhs_u)]
    wws = [dot(x, w) for x, w in zip(xs, rhs_w)]
    kdts = [kd.T for kd in kds]
    states = [s_ref[p] for p in range(GDN_PAIRS)]
    for c in range(GDN_TILE_CHUNKS):
        rs = slice(c * ch, (c + 1) * ch)
        outs = []
        for p in range(GDN_PAIRS):
            n = c * GDN_PAIRS + p
            s = states[p]
            v_new = uus[n] - dot(wws[n], s)
            o_st = dot(qgs[n], s) + dot(qks[n], v_new)
            states[p] = s * decs[n] + dot(kdts[n], v_new)
            outs.append(o_st[:ch] + o_st[ch:])
        o = jnp.concatenate(outs, axis=1)
        ms = _dot_select(o * o, eh_ref[...]) * (1.0 / HEAD_DIM)
        z = z_ref[rs, :]
        o_ref[rs, :] = o * lax.rsqrt(ms + NORM_EPS) * nw_ref[...] * (z * jax.nn.sigmoid(z))
    for p in range(GDN_PAIRS):
        s_ref[p] = states[p]

    @pl.when(i == pl.num_programs(0) - 1)
    def _():
        s_out_ref[...] = s_ref[...]


def _gdn_prompt(qkv, small, z, conv_w, a_log, dt_bias, norm_w):
    l = qkv.shape[0]
    f32, bf16 = jnp.float32, jnp.bfloat16
    w = GDN_W
    hh = jnp.arange(w) // HEAD_DIM
    expander = lambda base: (jnp.arange(LANES)[:, None] == base + hh[None, :]).astype(bf16)
    eh = (hh[:, None] == hh[None, :]).astype(bf16)
    cw8 = jnp.zeros((8, 3 * w), f32).at[:GDN_CONV].set(conv_w)
    prm = jnp.zeros((8, LANES), f32)
    prm = prm.at[0, GDN_A_LANE:GDN_A_LANE + GDN_HEADS].set(a_log).at[1, GDN_A_LANE:GDN_A_LANE + GDN_HEADS].set(dt_bias)
    tl = GDN_PREP_ROWS
    row = lambda wd: pl.BlockSpec((tl, wd), lambda i: (i, 0))
    const = lambda a: pl.BlockSpec(a.shape, lambda i: (0,) * a.ndim)
    ea, eb = expander(GDN_A_LANE), expander(GDN_B_LANE)
    q, k, v, g, b = pl.pallas_call(
        _gdn_prep_kernel,
        grid=(l // tl,),
        in_specs=[row(3 * w), row(LANES), const(cw8), const(prm), const(ea), const(eb), const(eh)],
        out_specs=[row(w)] * 5,
        out_shape=[jax.ShapeDtypeStruct((l, w), f32)] * 5,
        scratch_shapes=[pltpu.VMEM((8, 3 * w), f32)],
        compiler_params=pltpu.CompilerParams(dimension_semantics=("arbitrary",), vmem_limit_bytes=VMEM_LIMIT_BYTES),
        name="gdn_prep",
    )(qkv, small, cw8, prm, ea, eb, eh)
    tc = GDN_TILE_CHUNKS * GDN_CHUNK
    lt = (jnp.arange(GDN_CHUNK)[:, None] >= jnp.arange(GDN_CHUNK)[None, :]).astype(bf16)
    nw = jnp.tile(norm_w, GDN_HEADS).reshape(1, w)
    rowc = pl.BlockSpec((tc, w), lambda i: (i, 0))
    o, s_bd = pl.pallas_call(
        _gdn_chunk_kernel,
        grid=(l // tc,),
        in_specs=[rowc] * 6 + [const(nw), const(lt), const(eh)],
        out_specs=[rowc, pl.BlockSpec((GDN_PAIRS, LANES, LANES), lambda i: (0, 0, 0))],
        out_shape=[jax.ShapeDtypeStruct((l, w), f32), jax.ShapeDtypeStruct((GDN_PAIRS, LANES, LANES), f32)],
        scratch_shapes=[pltpu.VMEM((GDN_PAIRS, LANES, LANES), f32)],
        compiler_params=pltpu.CompilerParams(dimension_semantics=("arbitrary",), vmem_limit_bytes=VMEM_LIMIT_BYTES),
        name="gdn_chunk",
    )(q, k, v, g, b, z, nw, lt, eh)
    s4 = s_bd.reshape(GDN_PAIRS, 2, HEAD_DIM, 2, HEAD_DIM)
    s_fin = jnp.stack([s4[:, 0, :, 0], s4[:, 1, :, 1]], axis=1).reshape(GDN_HEADS, HEAD_DIM, HEAD_DIM)
    return o, s_fin


def _proj_dil_kernel(x_ref, w_ref, cos_ref, sin_ref, bf_ref, kv_ref):
    acc = jnp.dot(x_ref[...].astype(jnp.bfloat16), w_ref[...], preferred_element_type=jnp.float32)
    reps = 2 * DIL_W // LANES
    qk = _rope_lanes(acc[:, :2 * DIL_W], _lane_tile(cos_ref[...], reps), _lane_tile(sin_ref[...], reps))
    v = acc[:, 2 * DIL_W:]
    bf_ref[...] = jnp.concatenate([qk, v], axis=1).astype(jnp.bfloat16)
    kv_ref[...] = jnp.concatenate([qk[:, DIL_W:], v], axis=1)


def _proj_dil(x, w_in):
    l, d = x.shape
    tm = _row_tile(l)
    cos, sin = _rope_tables(jnp.arange(l), LANES)
    return pl.pallas_call(
        _proj_dil_kernel,
        grid=(l // tm,),
        in_specs=[pl.BlockSpec((tm, d), lambda i: (i, 0)),
                  pl.BlockSpec((d, 3 * DIL_W), lambda i: (0, 0)),
                  pl.BlockSpec((tm, LANES), lambda i: (i, 0)),
                  pl.BlockSpec((tm, LANES), lambda i: (i, 0))],
        out_specs=[pl.BlockSpec((tm, 3 * DIL_W), lambda i: (i, 0)),
                   pl.BlockSpec((tm, 2 * DIL_W), lambda i: (i, 0))],
        out_shape=[jax.ShapeDtypeStruct((l, 3 * DIL_W), jnp.bfloat16),
                   jax.ShapeDtypeStruct((l, 2 * DIL_W), jnp.float32)],
        compiler_params=pltpu.CompilerParams(dimension_semantics=("arbitrary",), vmem_limit_bytes=VMEM_LIMIT_BYTES),
        name="proj_dil",
    )(x, w_in.astype(jnp.bfloat16), cos, sin)


def _dil_band_stats(qkv, d):
    l = qkv.shape[0]
    assert l % (d * DIL_BLOCK) == 0
    nb = l // (d * DIL_BLOCK)
    view = qkv.reshape(l // d, d * 3 * DIL_W)

    def part(which, prev):
        return pl.BlockSpec((DIL_BLOCK, DIL_W), (lambda n, r: (jnp.maximum(n - 1, 0), 3 * r + which)) if prev
                            else (lambda n, r: (n, 3 * r + which)))

    out = lambda w: pl.BlockSpec((DIL_BLOCK, w), lambda n, r: (n, r))
    num, st = pl.pallas_call(
        _dil_band_kernel,
        grid=(nb, d),
        in_specs=[part(0, False), part(1, True), part(1, False), part(2, True), part(2, False)],
        out_specs=[out(DIL_W), out(LANES)],
        out_shape=[jax.ShapeDtypeStruct((l // d, d * DIL_W), jnp.float32),
                   jax.ShapeDtypeStruct((l // d, d * LANES), jnp.float32)],
        compiler_params=pltpu.CompilerParams(dimension_semantics=("arbitrary", "arbitrary"),
                                             vmem_limit_bytes=VMEM_LIMIT_BYTES),
        name="dil_band_stats",
    )(view, view, view, view, view)
    return num.reshape(l, DIL_W), st.reshape(l, LANES)


def _dil_merge_kernel(*refs):
    f32 = jnp.float32
    ng = len(DIL_GROUPS)
    nums, sts = refs[:ng], refs[ng:2 * ng]
    ex_ref, w_ref, res_ref, g_ref, b_ref, o_ref = refs[2 * ng:]
    st = [r[...] for r in sts]
    m_all = functools.reduce(jnp.maximum, st)
    ws = [jnp.exp(s - m_all) for s in st]
    den = sum(w * pltpu.roll(s, LANES - DIL_HEADS, 1) for w, s in zip(ws, st))
    head_lane = lax.broadcasted_iota(jnp.int32, den.shape, 1) < DIL_HEADS
    o = sum(_dot_select(jnp.where(head_lane, w / den, 0.0), ex_ref[...]) * n[...]
            for w, n in zip(ws, nums))
    acc = jnp.dot(o.astype(jnp.bfloat16), w_ref[...], preferred_element_type=f32)
    o_ref[...] = _layer_norm_rows(DEEPNORM_ALPHA * res_ref[...] + acc, g_ref[...], b_ref[...])


def _dil_merge_proj(nums, sts, w_out, res, g, b):
    l, n = res.shape
    tm = _row_tile(l)
    expand = (jnp.arange(LANES)[:, None] == jnp.arange(DIL_W)[None, :] // HEAD_DIM).astype(jnp.bfloat16)
    row = lambda w: pl.BlockSpec((tm, w), lambda i: (i, 0))
    const = lambda a: pl.BlockSpec(a.shape, lambda i: (0,) * a.ndim)
    wb = w_out.astype(jnp.bfloat16)
    g2, b2 = g.reshape(1, n), b.reshape(1, n)
    return pl.pallas_call(
        _dil_merge_kernel,
        grid=(l // tm,),
        in_specs=[row(DIL_W)] * len(nums) + [row(LANES)] * len(sts) + [const(expand), const(wb), row(n),
                                                                       const(g2), const(b2)],
        out_specs=row(n),
        out_shape=jax.ShapeDtypeStruct((l, n), jnp.float32),
        compiler_params=pltpu.CompilerParams(dimension_semantics=("arbitrary",), vmem_limit_bytes=VMEM_LIMIT_BYTES),
        name="dil_merge_proj",
    )(*nums, *sts, expand, wb, res, g2, b2)


def _split_cols(h, widths):
    parts, start = [], 0
    for w in widths:
        parts.append(h[..., start:start + w])
        start += w
    return parts


def _even_widths():
    return (NSA_Q_W,) + (NSA_KV_W,) * 6 + (3 * NSA_HEADS, 3 * GDN_W, GDN_HEADS, GDN_HEADS, GDN_W)


def _rms_norm(x, w):
    return x * lax.rsqrt(jnp.mean(jnp.square(x), axis=-1, keepdims=True) + NORM_EPS) * w


def _l2_norm(x):
    return x * lax.rsqrt(jnp.sum(jnp.square(x), axis=-1, keepdims=True) + NORM_EPS)


def _rope(x, pos):
    half = HEAD_DIM // 2
    inv_freq = ROPE_THETA ** (-2.0 * jnp.arange(half, dtype=jnp.float32) / HEAD_DIM)
    ang = pos.astype(jnp.float32)[:, None] * inv_freq[None, :]
    cos, sin = jnp.cos(ang)[:, None, :], jnp.sin(ang)[:, None, :]
    xf = x.astype(jnp.float32)
    x1, x2 = xf[..., :half], xf[..., half:]
    return jnp.concatenate([x1 * cos - x2 * sin, x2 * cos + x1 * sin], axis=-1)


def _causal_dwconv(hist, u, w):
    width, s = w.shape[0], u.shape[1]
    ext = jnp.concatenate([hist.astype(u.dtype), u], axis=1)
    out = w[0] * ext[:, :s]
    for j in range(1, width):
        out = out + w[j] * ext[:, j:j + s]
    return out, ext[:, s:]


def _nsa_compress(rows, w1, b1, w2, pe):
    b, l, g, dh = rows.shape
    nc = l // CMP_BLOCK
    blk = rows[:, :nc * CMP_BLOCK].astype(jnp.float32).reshape(b, nc, CMP_BLOCK, g, dh) + pe[:, None, :]
    flat = blk.transpose(0, 1, 3, 2, 4).reshape(b, nc, g, CMP_BLOCK * dh)
    return jax.nn.gelu(flat @ w1 + b1) @ w2


def _nsa_compressed_kv(k_rows, v_rows, cw1, cb1, cw2, cpe):
    ck = _nsa_compress(k_rows, cw1[0], cb1[0], cw2[0], cpe[0])
    cv = _nsa_compress(v_rows, cw1[1], cb1[1], cw2[1], cpe[1])
    nc = ck.shape[1]
    ck = _rope(ck, (jnp.arange(nc) + 1) * CMP_BLOCK - 1)
    return ck, cv


def _rope_rows_kernel(x_ref, cos_ref, sin_ref, o_ref):
    reps = x_ref.shape[1] // LANES
    o_ref[...] = _rope_lanes(x_ref[...], _lane_tile(cos_ref[...], reps), _lane_tile(sin_ref[...], reps))


def _rope_rows(x):
    l, w = x.shape
    tm = _row_tile(l)
    cos, sin = _rope_tables(jnp.arange(l), LANES)
    return pl.pallas_call(
        _rope_rows_kernel,
        grid=(l // tm,),
        in_specs=[pl.BlockSpec((tm, w), lambda i: (i, 0)), pl.BlockSpec((tm, LANES), lambda i: (i, 0)),
                  pl.BlockSpec((tm, LANES), lambda i: (i, 0))],
        out_specs=pl.BlockSpec((tm, w), lambda i: (i, 0)),
        out_shape=jax.ShapeDtypeStruct((l, w), jnp.float32),
        compiler_params=pltpu.CompilerParams(dimension_semantics=("arbitrary",), vmem_limit_bytes=VMEM_LIMIT_BYTES),
        name="rope_rows",
    )(x, cos, sin)


def _nsa_prompt(q, kc, vc, ks, vs, kw, vw, small, cw1, cb1, cw2, cpe):
    b, l = q.shape[:2]
    flat = lambda t: t.reshape(t.shape[1], -1)
    roped = _rope_rows(jnp.concatenate([flat(q), flat(ks), flat(kw)], axis=1))
    qr = roped[:, :NSA_Q_W]
    ksr = roped[:, NSA_Q_W:NSA_Q_W + NSA_KV_W].reshape(ks.shape)
    kwr = roped[:, NSA_Q_W + NSA_KV_W:].reshape(kw.shape)
    ck, cv = _nsa_compressed_kv(kc, vc, cw1, cb1, cw2, cpe)
    vsf = vs.astype(jnp.float32)
    vwf = vw.astype(jnp.float32)
    o_nsa = _nsa_prompt_attention(qr, small, flat(ck), flat(cv), flat(ksr), flat(vsf), flat(kwr), flat(vwf))
    keep = min(NSA_WINDOW, l)
    rows_cmp = jnp.stack([kc, vc], axis=2)
    rows_slc = jnp.stack([ksr, vsf], axis=2)
    rows_win = jnp.stack([kwr[:, l - keep:], vwf[:, l - keep:]], axis=2)
    return o_nsa[None], rows_cmp, rows_slc, rows_win


def _nsa_sample(q, kc, vc, ks, vs, kw, vw, gate, cmp_pool, slc_pool, layer_idx, win_buf, page_table,
                cw1, cb1, cw2, cpe):
    db, s = q.shape[:2]
    past = page_table.shape[1] * PAGE_SIZE
    wb = win_buf.shape[1]
    assert s == 1 and wb == NSA_WINDOW and past >= wb and past % (STEP_PAGES * PAGE_SIZE) == 0
    qpos = past + jnp.arange(s)
    qr = _rope(q, qpos)
    ckt, cvt = _nsa_sample_compress(cmp_pool, layer_idx, page_table, cw1, cb1, cw2, cpe)
    ksr = _rope(ks, qpos)
    vsf = vs.astype(jnp.float32)
    kwr = _rope(kw, qpos)
    vwf = vw.astype(jnp.float32)
    newrows = jnp.stack([t.reshape(db, NSA_KV_W) for t in (ksr, vsf, kwr, vwf)], axis=1)
    o_nsa, rows_win = _nsa_sample_attention(qr[:, 0], ckt, cvt, slc_pool, layer_idx, page_table, win_buf,
                                              newrows, gate.reshape(db, -1))
    rows_cmp = jnp.stack([kc, vc], axis=2)
    rows_slc = jnp.stack([ksr, vsf], axis=2)
    return o_nsa[:, None], rows_cmp, rows_slc, rows_win


def _gdn_recurrent(q, k, v, g, beta, s0):
    def step(state, xs):
        q_t, k_t, v_t, g_t, b_t = xs
        state = state * jnp.exp(g_t)[..., None, None]
        v_t = (v_t - jnp.einsum('bhk,bhkv->bhv', k_t, state)) * b_t[..., None]
        state = state + jnp.einsum('bhk,bhv->bhkv', k_t, v_t)
        return state, jnp.einsum('bhk,bhkv->bhv', q_t, state)

    xs = tuple(jnp.moveaxis(a, 1, 0) for a in (q, k, v, g, beta))
    s_fin, o = lax.scan(step, s0, xs)
    return jnp.moveaxis(o, 0, 1), s_fin


def _gdn_step(qkv, a, bt, z, conv_hist, s0, conv_w, a_log, dt_bias, norm_w):
    b, s = qkv.shape[:2]
    c, new_hist = _causal_dwconv(conv_hist, qkv, conv_w)
    c = jax.nn.silu(c.astype(jnp.float32))
    q, k, v = [t.reshape(b, s, GDN_HEADS, HEAD_DIM) for t in jnp.split(c, 3, axis=-1)]
    q = _l2_norm(q) * HEAD_DIM ** -0.5
    k = _l2_norm(k)
    beta = jax.nn.sigmoid(bt.astype(jnp.float32))
    g = -jnp.exp(a_log) * jax.nn.softplus(a.astype(jnp.float32) + dt_bias)
    o, s_fin = _gdn_recurrent(q, k, v, g, beta, s0.astype(jnp.float32))
    o = _rms_norm(o, norm_w) * jax.nn.silu(z.astype(jnp.float32).reshape(b, s, GDN_HEADS, HEAD_DIM))
    return o.reshape(b, s, GDN_W), new_hist, s_fin


def _proj(x, w):
    b, s, d = x.shape
    n = w.shape[1]
    npad = -(-n // LANES) * LANES
    wp = jnp.pad(w, ((0, 0), (0, npad - n)))
    return _matmul(x.reshape(b * s, d), wp).reshape(b, s, npad)


def _even_prompt(x, w_in, cw1, cb1, cw2, cpe, conv_w, a_log, dt_bias, norm_w):
    b, l, _ = x.shape
    q, kc, vc, ks, vs, kw, vw, gate, qkv, a, bt, z = _split_cols(_proj(x, w_in), _even_widths())
    heads = lambda t: t.reshape(b, l, -1, HEAD_DIM)
    assert b == 1
    small = jnp.concatenate([gate, a, bt], axis=-1).reshape(l, -1)
    small = jnp.pad(small, ((0, 0), (0, LANES - small.shape[-1])))
    o_nsa, r_cmp, r_slc, r_win = _nsa_prompt(
        heads(q), heads(kc), heads(vc), heads(ks), heads(vs), heads(kw), heads(vw), small, cw1, cb1, cw2, cpe)
    o_gdn, s_fin = _gdn_prompt(qkv[0], small, z[0], conv_w, a_log, dt_bias, norm_w)
    conv_hist = qkv[:, l - (GDN_CONV - 1):]
    return jnp.concatenate([o_nsa, o_gdn[None]], axis=-1), r_cmp, r_slc, r_win, conv_hist, s_fin[None]


def _even_sample(x, cmp_pool, slc_pool, layer_idx, win_buf, conv_hist, s0, page_table,
                 w_in, cw1, cb1, cw2, cpe, conv_w, a_log, dt_bias, norm_w):
    b, s, _ = x.shape
    q, kc, vc, ks, vs, kw, vw, gate, qkv, a, bt, z = _split_cols(_proj(x, w_in), _even_widths())
    heads = lambda t: t.reshape(b, s, -1, HEAD_DIM)
    o_nsa, r_cmp, r_slc, r_win = _nsa_sample(
        heads(q), heads(kc), heads(vc), heads(ks), heads(vs), heads(kw), heads(vw), gate,
        cmp_pool, slc_pool, layer_idx, win_buf, page_table, cw1, cb1, cw2, cpe)
    o_gdn, new_hist, s_fin = _gdn_step(qkv, a, bt, z, conv_hist, s0, conv_w, a_log, dt_bias, norm_w)
    return jnp.concatenate([o_nsa, o_gdn], axis=-1), r_cmp, r_slc, r_win, new_hist, s_fin


def _dil_prompt(x, w_in, w_out, g, b):
    bsz, l, _ = x.shape
    assert bsz == 1
    qkv, kv = _proj_dil(x[0], w_in)
    stats = [_dil_band_stats(qkv, d) for _, d in DIL_GROUPS]
    y = _dil_merge_proj([n for n, _ in stats], [s for _, s in stats], w_out, x[0], g, b)
    keep = min(DIL_MAX_WINDOW, l)
    buf = kv[l - keep:].reshape(1, keep, 2, DIL_HEADS, HEAD_DIM)
    return y, buf


def _dil_sample(x, buf, past, w_in):
    db, s, _ = x.shape
    q, k, v = [t.reshape(db, s, DIL_HEADS, HEAD_DIM) for t in jnp.split(_proj(x, w_in), 3, axis=-1)]
    assert s == 1 and buf.shape[1] == DIL_MAX_WINDOW <= past
    qpos = past + jnp.arange(s)
    qr, kr = _rope(q, qpos), _rope(k, qpos)
    o, new_buf = _dil_sample_attention(qr[:, 0], kr.reshape(db, DIL_W), v.reshape(db, DIL_W).astype(jnp.float32), buf)
    return o[:, None], new_buf


def kernel(x_prompt, x_sample, cache_nsa_cmp_kv, cache_nsa_slc_kv, state_nsa_win_kv, state_gdn_conv,
           state_gdn_S, state_dil_kv, state_ffn_conv, page_table, w_in_a, nsa_cmp_w1, nsa_cmp_b1, nsa_cmp_w2,
           nsa_cmp_pe, gdn_conv_w, gdn_A_log, gdn_dt_bias, gdn_norm_w, w_out_a, w_in_c, w_out_c,
           ln_mix_g, ln_mix_b, ffn_w_in, ffn_conv_w, ffn_conv_b, ffn_w_out, ln_ffn_g, ln_ffn_b):
    past = page_table.shape[1] * PAGE_SIZE
    bp, lp, d = x_prompt.shape
    bs, ls, _ = x_sample.shape
    assert bp == 1 and ls == 1
    xp, xs = x_prompt, x_sample
    cmp_p, cmp_s, slc_p, slc_s, win_p, win_s = [], [], [], [], [], []
    gconv_p, gconv_s, gstate_p, gstate_s = [], [], [], []
    dil_p, dil_s, ffn_p, ffn_s = [], [], [], []
    for layer in range(DEPTH):
        if layer % 2 == 0:
            la = layer // 2
            wa = (w_in_a[la], nsa_cmp_w1[la], nsa_cmp_b1[la], nsa_cmp_w2[la], nsa_cmp_pe[la],
                  gdn_conv_w[la], gdn_A_log[la], gdn_dt_bias[la], gdn_norm_w[la])
            mp, rc, rs, rw, hc, hs_ = _even_prompt(xp, *wa)
            cmp_p.append(rc); slc_p.append(rs); win_p.append(rw); gconv_p.append(hc); gstate_p.append(hs_)
            ms, rc, rs, rw, hc, hs_ = _even_sample(xs, cache_nsa_cmp_kv, cache_nsa_slc_kv, la,
                                                   state_nsa_win_kv[:, la], state_gdn_conv[:, la],
                                                   state_gdn_S[:, la], page_table, *wa)
            cmp_s.append(rc); slc_s.append(rs); win_s.append(rw); gconv_s.append(hc); gstate_s.append(hs_)
            w_out = w_out_a[la]
        else:
            lc = layer // 2
            xp2, bpf = _dil_prompt(xp, w_in_c[lc], w_out_c[lc], ln_mix_g[layer], ln_mix_b[layer])
            ms, bsf = _dil_sample(xs, state_dil_kv[:, lc], past, w_in_c[lc])
            dil_p.append(bpf); dil_s.append(bsf)
            w_out = w_out_c[lc]
        if layer % 2 == 0:
            xp2 = _matmul_ln(mp.reshape(lp, -1), w_out, xp.reshape(lp, d), ln_mix_g[layer], ln_mix_b[layer])
        xs2 = _matmul_ln(ms.reshape(bs, -1), w_out, xs.reshape(bs, d), ln_mix_g[layer], ln_mix_b[layer])
        fargs = (ffn_w_in[layer], ffn_conv_w[layer], ffn_conv_b[layer], ffn_w_out[layer],
                 ln_ffn_g[layer], ln_ffn_b[layer])
        xp3, hp = _ffn_seq(xp2, *fargs)
        xs3, hs = _ffn_step(xs2, state_ffn_conv[:, layer], *fargs)
        xp, xs = xp3.reshape(1, lp, d), xs3.reshape(bs, 1, d)
        ffn_p.append(hp[None]); ffn_s.append(hs)

    def stk(lst):
        return jnp.stack(lst, axis=1)

    return (xp, xs, stk(cmp_p), stk(cmp_s), stk(slc_p), stk(slc_s), stk(win_p), stk(win_s),
            stk(gconv_p), stk(gconv_s), stk(gstate_p), stk(gstate_s), stk(dil_p), stk(dil_s),
            stk(ffn_p), stk(ffn_s))
```

```python
import functools

import jax
import jax.numpy as jnp
from jax import lax
from jax.experimental import pallas as pl
from jax.experimental.pallas import tpu as pltpu
import numpy as np

DEPTH = 2
PAGE_SIZE = 128
HEAD_DIM = 64
ROPE_THETA = 10000.0
NSA_HEADS = 8
NSA_KV_HEADS = 2
NSA_GROUP = NSA_HEADS // NSA_KV_HEADS
CMP_BLOCK = 32
SEL_BLOCK = 64
NSA_TOPN = 16
NSA_WINDOW = 512
NSA_QBLOCK = 128
NSA_FORCE = 1.0e4
GDN_HEADS = 8
GDN_CONV = 4
GDN_CHUNK = 64
DIL_HEADS = 16
DIL_GROUPS = ((128, 1), (512, 4), (2048, 16))
DIL_SPAN = 128
DIL_BLOCK = 128
DIL_MAX_WINDOW = 2048
D_FF = 2816
FFN_CONV = 3
DEEPNORM_ALPHA = (2.0 * DEPTH) ** 0.25
LN_EPS = 1e-5
NORM_EPS = 1e-6
NSA_Q_W = NSA_HEADS * HEAD_DIM
NSA_KV_W = NSA_KV_HEADS * HEAD_DIM
GDN_W = GDN_HEADS * HEAD_DIM
DIL_W = DIL_HEADS * HEAD_DIM

LANES = 128
VMEM_LIMIT_BYTES = 56 * 1024 * 1024


def _layer_norm_rows(r, g, b):
    mu = jnp.mean(r, axis=-1, keepdims=True)
    d = r - mu
    var = jnp.mean(d * d, axis=-1, keepdims=True)
    return d * lax.rsqrt(var + LN_EPS) * g + b


def _mm_kernel(x_ref, w_ref, o_ref):
    o_ref[...] = jnp.dot(x_ref[...].astype(jnp.bfloat16), w_ref[...], preferred_element_type=jnp.float32)


def _mm_ln_kernel(x_ref, w_ref, res_ref, g_ref, b_ref, o_ref):
    acc = jnp.dot(x_ref[...].astype(jnp.bfloat16), w_ref[...], preferred_element_type=jnp.float32)
    o_ref[...] = _layer_norm_rows(DEEPNORM_ALPHA * res_ref[...] + acc, g_ref[...], b_ref[...])


def _row_tile(m):
    return 512 if m % 512 == 0 else m


def _matmul(x, w):
    m, k = x.shape
    n = w.shape[1]
    tm = _row_tile(m)
    tn = n
    for cand in (1152, 1024, 768, 512):
        if n % cand == 0:
            tn = cand
            break
    return pl.pallas_call(
        _mm_kernel,
        grid=(m // tm, n // tn),
        in_specs=[pl.BlockSpec((tm, k), lambda i, j: (i, 0)),
                  pl.BlockSpec((k, tn), lambda i, j: (0, j))],
        out_specs=pl.BlockSpec((tm, tn), lambda i, j: (i, j)),
        out_shape=jax.ShapeDtypeStruct((m, n), jnp.float32),
        compiler_params=pltpu.CompilerParams(dimension_semantics=("parallel", "arbitrary"),
                                             vmem_limit_bytes=VMEM_LIMIT_BYTES),
        name="matmul",
    )(x, w.astype(jnp.bfloat16))


def _matmul_ln(x, w, res, g, b):
    m, k = x.shape
    n = w.shape[1]
    tm = _row_tile(m)
    return pl.pallas_call(
        _mm_ln_kernel,
        grid=(m // tm,),
        in_specs=[pl.BlockSpec((tm, k), lambda i: (i, 0)),
                  pl.BlockSpec((k, n), lambda i: (0, 0)),
                  pl.BlockSpec((tm, n), lambda i: (i, 0)),
                  pl.BlockSpec((1, n), lambda i: (0, 0)),
                  pl.BlockSpec((1, n), lambda i: (0, 0))],
        out_specs=pl.BlockSpec((tm, n), lambda i: (i, 0)),
        out_shape=jax.ShapeDtypeStruct((m, n), jnp.float32),
        compiler_params=pltpu.CompilerParams(dimension_semantics=("arbitrary",),
                                             vmem_limit_bytes=VMEM_LIMIT_BYTES),
        name="matmul_ln",
    )(x, w.astype(jnp.bfloat16), res, g.reshape(1, n), b.reshape(1, n))


FFN_CHUNK = D_FF // 2
FFN_NCHUNK = D_FF // FFN_CHUNK


def _ffn_seq_kernel(x_ref, wi_ref, cw_ref, cb_ref, wo_ref, lg_ref, lb_ref, y_ref, hist_ref, carry_ref):
    f32, bf16 = jnp.float32, jnp.bfloat16
    i = pl.program_id(0)
    tm = x_ref.shape[0]
    x = x_ref[...]
    xb = x.astype(bf16)

    @pl.when(i == 0)
    def _():
        carry_ref[...] = jnp.zeros(carry_ref.shape, f32)

    row = lax.broadcasted_iota(jnp.int32, (8, FFN_CHUNK), 0)

    def conv_half(cols):
        u = jnp.dot(xb, wi_ref[:, cols], preferred_element_type=f32)
        prev = carry_ref[:, cols]
        p2, p1 = prev[6:7], prev[7:8]
        r1, r2 = pltpu.roll(u, 1, 0), pltpu.roll(u, 2, 0)
        u1 = jnp.concatenate([jnp.where(row == 0, p1, r1[:8]), r1[8:]], axis=0)
        u2 = jnp.concatenate([jnp.where(row == 0, p2, jnp.where(row == 1, p1, r2[:8])), r2[8:]], axis=0)
        carry_ref[:, cols] = u[tm - 8:]
        hist_ref[:, cols] = u[tm - 8:]
        cw = cw_ref[:, cols]
        return cw[0:1] * u2 + cw[1:2] * u1 + cw[2:3] * u + cb_ref[:, cols]

    acc = None
    for j in range(FFN_NCHUNK):
        a = conv_half(slice(j * FFN_CHUNK, (j + 1) * FFN_CHUNK))
        g = conv_half(slice(D_FF + j * FFN_CHUNK, D_FF + (j + 1) * FFN_CHUNK))
        h = (a * jax.nn.sigmoid(a) * g).astype(bf16)
        part = jnp.dot(h, wo_ref[j * FFN_CHUNK:(j + 1) * FFN_CHUNK, :], preferred_element_type=f32)
        acc = part if acc is None else acc + part
    y_ref[...] = _layer_norm_rows(DEEPNORM_ALPHA * x + acc, lg_ref[...], lb_ref[...])


def _ffn_seq(x, w_in, conv_w, conv_b, w_out, ln_g, ln_b):
    l, d = x.shape
    tm = _row_tile(l)
    cw8 = jnp.zeros((8, 2 * D_FF), jnp.float32).at[:FFN_CONV].set(conv_w)
    cb = conv_b.reshape(1, 2 * D_FF)
    const = lambda a: pl.BlockSpec(a.shape, lambda i: (0,) * a.ndim, pipeline_mode=pl.Buffered(1))
    wi, wo = w_in.astype(jnp.bfloat16), w_out.astype(jnp.bfloat16)
    g2, b2 = ln_g.reshape(1, d), ln_b.reshape(1, d)
    y, hist = pl.pallas_call(
        _ffn_seq_kernel,
        grid=(l // tm,),
        in_specs=[pl.BlockSpec((tm, d), lambda i: (i, 0)), const(wi), const(cw8), const(cb), const(wo),
                  const(g2), const(b2)],
        out_specs=[pl.BlockSpec((tm, d), lambda i: (i, 0)), pl.BlockSpec((8, 2 * D_FF), lambda i: (i, 0))],
        out_shape=[jax.ShapeDtypeStruct((l, d), jnp.float32),
                   jax.ShapeDtypeStruct((l // tm * 8, 2 * D_FF), jnp.float32)],
        scratch_shapes=[pltpu.VMEM((8, 2 * D_FF), jnp.float32)],
        compiler_params=pltpu.CompilerParams(dimension_semantics=("arbitrary",), vmem_limit_bytes=VMEM_LIMIT_BYTES),
        name="ffn_seq",
    )(x, wi, cw8, cb, wo, g2, b2)
    return y, hist[-(FFN_CONV - 1):]


def _ffn_step_kernel(x_ref, h_ref, wa_ref, wg_ref, cwa_ref, cwg_ref, cba_ref, cbg_ref, wo_ref, lg_ref, lb_ref,
                     y_ref, ua_ref, ug_ref, acc_ref):
    j = pl.program_id(0)
    x = x_ref[...]
    xb = x.astype(jnp.bfloat16)
    ua = jnp.dot(xb, wa_ref[...], preferred_element_type=jnp.float32)
    ug = jnp.dot(xb, wg_ref[...], preferred_element_type=jnp.float32)
    ua_ref[...] = ua
    ug_ref[...] = ug
    cwa, cwg = cwa_ref[...], cwg_ref[...]
    a = cwa[0:1] * h_ref[0, 0] + cwa[1:2] * h_ref[1, 0] + cwa[2:3] * ua + cba_ref[...]
    g = cwg[0:1] * h_ref[0, 1] + cwg[1:2] * h_ref[1, 1] + cwg[2:3] * ug + cbg_ref[...]
    h = (a * jax.nn.sigmoid(a) * g).astype(jnp.bfloat16)
    part = jnp.dot(h, wo_ref[...], preferred_element_type=jnp.float32)

    @pl.when(j == 0)
    def _():
        acc_ref[...] = part

    @pl.when(j > 0)
    def _():
        acc_ref[...] += part

    @pl.when(j == pl.num_programs(0) - 1)
    def _():
        y_ref[...] = _layer_norm_rows(DEEPNORM_ALPHA * x + acc_ref[...], lg_ref[...], lb_ref[...])


def _ffn_step(x, hist, w_in, conv_w, conv_b, w_out, ln_g, ln_b):
    b, d = x.shape
    c, nc = FFN_CHUNK, FFN_NCHUNK
    w_in = w_in.astype(jnp.bfloat16)
    cw8 = jnp.zeros((8, 2 * D_FF), jnp.float32).at[:FFN_CONV].set(conv_w)
    cb = conv_b.reshape(1, 2 * D_FF)
    h4 = jnp.transpose(hist, (1, 0, 2)).reshape(2, b, 2, D_FF).transpose(0, 2, 1, 3)
    y, ua, ug = pl.pallas_call(
        _ffn_step_kernel,
        grid=(nc,),
        in_specs=[pl.BlockSpec((b, d), lambda j: (0, 0)),
                  pl.BlockSpec((2, 2, b, c), lambda j: (0, 0, 0, j)),
                  pl.BlockSpec((d, c), lambda j: (0, j)),
                  pl.BlockSpec((d, c), lambda j: (0, j + nc)),
                  pl.BlockSpec((8, c), lambda j: (0, j)),
                  pl.BlockSpec((8, c), lambda j: (0, j + nc)),
                  pl.BlockSpec((1, c), lambda j: (0, j)),
                  pl.BlockSpec((1, c), lambda j: (0, j + nc)),
                  pl.BlockSpec((c, d), lambda j: (j, 0)),
                  pl.BlockSpec((1, d), lambda j: (0, 0)),
                  pl.BlockSpec((1, d), lambda j: (0, 0))],
        out_specs=[pl.BlockSpec((b, d), lambda j: (0, 0)),
                   pl.BlockSpec((b, c), lambda j: (0, j)),
                   pl.BlockSpec((b, c), lambda j: (0, j))],
        out_shape=[jax.ShapeDtypeStruct((b, d), jnp.float32),
                   jax.ShapeDtypeStruct((b, D_FF), jnp.float32),
                   jax.ShapeDtypeStruct((b, D_FF), jnp.float32)],
        scratch_shapes=[pltpu.VMEM((b, d), jnp.float32)],
        compiler_params=pltpu.CompilerParams(dimension_semantics=("arbitrary",),
                                             vmem_limit_bytes=VMEM_LIMIT_BYTES),
        name="ffn_step",
    )(x, h4, w_in, w_in, cw8, cw8, cb, cb, w_out.astype(jnp.bfloat16), ln_g.reshape(1, d), ln_b.reshape(1, d))
    u = jnp.concatenate([ua, ug], axis=-1)
    return y, jnp.concatenate([hist[:, 1:], u[:, None]], axis=1)


NEG = -1e30
NSA_KT = 512
NSA_COLS = NSA_HEADS * NSA_QBLOCK
NSA_WSPAN = NSA_WINDOW + NSA_QBLOCK
LOG2E = 1.4426950408889634
NSA_VROWS = HEAD_DIM + 8


def _lane_tile(x, n):
    return jnp.concatenate([x] * n, axis=1)


def _nsa_prompt_kernel(q_ref, sm_ref, ck_ref, cvt_ref, ks_ref, vst_ref, kw_ref, vwt_ref, hot_ref, o_ref,
                       selb_ref, m_ref, acc_ref, *, ns):
    f32, bf16 = jnp.float32, jnp.bfloat16
    qb = NSA_QBLOCK
    i = pl.program_id(0)
    s0 = i * qb
    half = NSA_COLS // 2

    qt = (q_ref[...] * (HEAD_DIM ** -0.5 * LOG2E)).T
    zero = jnp.zeros((HEAD_DIM, qb), f32)
    top = jnp.concatenate([qt[h * HEAD_DIM:(h + 1) * HEAD_DIM] for h in range(NSA_GROUP)] + [zero] * NSA_GROUP, axis=1)
    bot = jnp.concatenate([zero] * NSA_GROUP + [qt[h * HEAD_DIM:(h + 1) * HEAD_DIM]
                                                for h in range(NSA_GROUP, NSA_HEADS)], axis=1)
    qbd = jnp.concatenate([top, bot], axis=0).astype(bf16)

    def pv(vt, p):
        pb = p.astype(bf16)
        rows = vt.shape[0] // NSA_KV_HEADS
        return [jnp.dot(vt[g * rows:(g + 1) * rows], pb[:, g * half:(g + 1) * half],
                        preferred_element_type=f32) for g in range(NSA_KV_HEADS)]

    nc = 2 * ns
    r = lax.broadcasted_iota(jnp.int32, (nc, qb), 0)
    lane = lax.broadcasted_iota(jnp.int32, (nc, qb), 1)
    cidx = jnp.where(r < ns, 2 * r, 2 * (r - ns) + 1)
    cbias = jnp.where((cidx + 1) * CMP_BLOCK - 1 <= s0 + lane, 0.0, NEG)
    sc = jnp.dot(ck_ref[...], qbd, preferred_element_type=f32) + _lane_tile(cbias, NSA_HEADS)
    m = jnp.max(sc, axis=0, keepdims=True)
    p = jnp.exp2(sc - m)
    pn = p * jnp.where(m > 0.5 * NEG, 1.0 / jnp.sum(p, axis=0, keepdims=True), 0.0)
    o_cmp = pv(cvt_ref[...], pn)

    blk = lax.broadcasted_iota(jnp.int32, (ns, qb), 0)
    qpos = s0 + lax.broadcasted_iota(jnp.int32, (ns, qb), 1)
    cur = qpos // SEL_BLOCK
    forced = (blk == 0) | (blk == cur) | (blk == cur - 1)
    for g in range(NSA_KV_HEADS):
        imp = pn[:, g * half:g * half + qb]
        for h in range(1, NSA_GROUP):
            imp = imp + pn[:, g * half + h * qb:g * half + (h + 1) * qb]
        imp = imp[:ns] + imp[ns:]
        val = jnp.where(blk > cur, -1.0, jnp.where(forced, NSA_FORCE, imp))
        bias = jnp.full((ns, qb), NEG, f32)
        for _ in range(min(NSA_TOPN, ns)):
            top = jnp.max(val, axis=0, keepdims=True)
            pick = jnp.min(jnp.where(val == top, blk, ns), axis=0, keepdims=True)
            hit = blk == pick
            bias = jnp.where(hit, 0.0, bias)
            val = jnp.where(hit, -jnp.inf, val)
        selb_ref[g] = bias

    m_ref[...] = jnp.full(m_ref.shape, NEG, f32)
    acc_ref[...] = jnp.zeros(acc_ref.shape, f32)
    per_tile = NSA_KT // SEL_BLOCK
    zpad = jnp.zeros((LANES - 16, NSA_COLS), bf16)

    def slc_tile(kt, causal):
        k0 = pl.multiple_of(kt * NSA_KT, NSA_KT)
        b0 = pl.multiple_of(kt * per_tile, per_tile)
        brow = jnp.concatenate([selb_ref[g, pl.ds(b0, per_tile), :] for g in range(NSA_KV_HEADS)
                                for _ in range(NSA_GROUP)], axis=1)
        brow = jnp.concatenate([brow, jnp.zeros((16 - per_tile, NSA_COLS), f32)], axis=0).astype(bf16)
        q_aug = jnp.concatenate([qbd, brow, zpad], axis=0)
        k_aug = jnp.concatenate([ks_ref[pl.ds(k0, NSA_KT), :], hot_ref[...]], axis=1)
        s = jnp.dot(k_aug, q_aug, preferred_element_type=f32)
        if causal:
            kpos = k0 + lax.broadcasted_iota(jnp.int32, (NSA_KT, qb), 0)
            qq = s0 + lax.broadcasted_iota(jnp.int32, (NSA_KT, qb), 1)
            s = s + _lane_tile(jnp.where(kpos <= qq, 0.0, NEG), NSA_HEADS)
        m_old = m_ref[...]
        m_new = jnp.maximum(m_old, jnp.max(s, axis=0, keepdims=True))
        alpha = jnp.exp2(m_old - m_new)
        p = jnp.exp2(s - m_new)
        m_ref[...] = m_new
        upd = pv(vst_ref[:, pl.ds(k0, NSA_KT)], p)
        for g in range(NSA_KV_HEADS):
            acc_ref[g] = acc_ref[g] * alpha[:, g * half:(g + 1) * half] + upd[g]

    kd = s0 // NSA_KT

    def body(j, carry):
        slc_tile(2 * j, False)
        slc_tile(2 * j + 1, False)
        return carry

    lax.fori_loop(0, kd // 2, body, 0)

    @pl.when(kd % 2 == 1)
    def _():
        slc_tile(kd - 1, False)

    slc_tile(kd, True)
    inv_slc = [1.0 / acc_ref[g, HEAD_DIM:HEAD_DIM + 1, :] for g in range(NSA_KV_HEADS)]

    w0 = pl.multiple_of(s0, qb)
    sw = jnp.dot(kw_ref[pl.ds(w0, NSA_WSPAN), :], qbd, preferred_element_type=f32)
    rr = lax.broadcasted_iota(jnp.int32, (NSA_WSPAN, qb), 0)
    qi = lax.broadcasted_iota(jnp.int32, (NSA_WSPAN, qb), 1)
    ok = (rr >= qi) & (rr <= qi + NSA_WINDOW) & (rr + s0 >= NSA_WINDOW)
    sw = sw + _lane_tile(jnp.where(ok, 0.0, NEG), NSA_HEADS)
    pw = jnp.exp2(sw - jnp.max(sw, axis=0, keepdims=True))
    o_win = pv(vwt_ref[:, pl.ds(w0, NSA_WSPAN)], pw)
    inv_win = [1.0 / o[HEAD_DIM:HEAD_DIM + 1] for o in o_win]

    gt = jax.nn.sigmoid(sm_ref[...].T)
    outs = []
    for h in range(NSA_HEADS):
        g, hg = divmod(h, NSA_GROUP)
        c0, c1 = hg * qb, (hg + 1) * qb
        g_cmp = gt[h:h + 1]
        g_slc = gt[NSA_HEADS + h:NSA_HEADS + h + 1] * inv_slc[g][:, c0:c1]
        g_win = gt[2 * NSA_HEADS + h:2 * NSA_HEADS + h + 1] * inv_win[g][:, c0:c1]
        outs.append(o_cmp[g][:, c0:c1] * g_cmp + acc_ref[g, :HEAD_DIM, c0:c1] * g_slc
                    + o_win[g][:HEAD_DIM, c0:c1] * g_win)
    o_ref[...] = jnp.concatenate(outs, axis=0).T


def _nsa_prompt_attention(qr, small, ck, cv, ksr, vs, kwr, vw):
    l = qr.shape[0]
    ns = l // SEL_BLOCK
    nc = 2 * ns
    bf16 = jnp.bfloat16
    perm = jnp.concatenate([jnp.arange(0, nc, 2), jnp.arange(1, nc, 2)])
    ckp = ck[perm].astype(bf16)
    cvt = cv[perm].T.astype(bf16)
    pad = jnp.zeros((NSA_WINDOW, NSA_KV_W), bf16)
    kwp = jnp.concatenate([pad, kwr.astype(bf16)], axis=0)
    def with_ones(vt):
        n = vt.shape[1]
        extra = jnp.concatenate([jnp.ones((1, n), bf16), jnp.zeros((NSA_VROWS - HEAD_DIM - 1, n), bf16)], axis=0)
        return jnp.concatenate([x for g in range(NSA_KV_HEADS) for x in (vt[g * HEAD_DIM:(g + 1) * HEAD_DIM], extra)], axis=0)

    vwt = with_ones(jnp.concatenate([pad, vw.astype(bf16)], axis=0).T)
    hot = (jnp.arange(NSA_KT)[:, None] // SEL_BLOCK == jnp.arange(LANES)[None, :]).astype(bf16)
    full = lambda a: pl.BlockSpec(a.shape, lambda i: (0,) * a.ndim)
    args = (qr, small, ckp, cvt, ksr.astype(bf16), with_ones(vs.T.astype(bf16)), kwp, vwt, hot)
    return pl.pallas_call(
        functools.partial(_nsa_prompt_kernel, ns=ns),
        grid=(l // NSA_QBLOCK,),
        in_specs=[pl.BlockSpec((NSA_QBLOCK, NSA_Q_W), lambda i: (i, 0)),
                  pl.BlockSpec((NSA_QBLOCK, LANES), lambda i: (i, 0))] + [full(a) for a in args[2:]],
        out_specs=pl.BlockSpec((NSA_QBLOCK, NSA_Q_W), lambda i: (i, 0)),
        out_shape=jax.ShapeDtypeStruct((l, NSA_Q_W), jnp.float32),
        scratch_shapes=[pltpu.VMEM((NSA_KV_HEADS, ns, NSA_QBLOCK), jnp.float32),
                        pltpu.VMEM((1, NSA_COLS), jnp.float32),
                        pltpu.VMEM((NSA_KV_HEADS, NSA_VROWS, NSA_COLS // 2), jnp.float32)],
        compiler_params=pltpu.CompilerParams(dimension_semantics=("arbitrary",),
                                             vmem_limit_bytes=VMEM_LIMIT_BYTES),
        name="nsa_prompt",
    )(*args)


CMP_HIDDEN = 2 * HEAD_DIM
CMP_PER_PAGE = PAGE_SIZE // CMP_BLOCK
CMP_PAGES_PER_STEP = 64
STEP_PAGES = 64
STEP_ROWS = 16


def _rope_tables(pos, width):
    half = HEAD_DIM // 2
    inv_freq = ROPE_THETA ** (-2.0 * jnp.arange(half, dtype=jnp.float32) / HEAD_DIM)
    ang = pos.astype(jnp.float32)[:, None] * inv_freq[None, :]
    cos, sin = jnp.cos(ang), jnp.sin(ang)
    reps = width // HEAD_DIM
    return (jnp.tile(jnp.concatenate([cos, cos], axis=1), (1, reps)),
            jnp.tile(jnp.concatenate([-sin, sin], axis=1), (1, reps)))


def _rope_lanes(x, cos, sin_signed):
    n = x.shape[-1]
    lane = lax.broadcasted_iota(jnp.int32, x.shape, x.ndim - 1)
    first = (lane % HEAD_DIM) < HEAD_DIM // 2
    partner = jnp.where(first, pltpu.roll(x, n - HEAD_DIM // 2, x.ndim - 1), pltpu.roll(x, HEAD_DIM // 2, x.ndim - 1))
    return x * cos + partner * sin_signed


def _nt_dot(a, b):
    return lax.dot_general(a, b, (((1,), (1,)), ((), ())), preferred_element_type=jnp.float32)


CMP_STEP_LANES = CMP_PAGES_PER_STEP * CMP_PER_PAGE
CMP_FEATURE_GROUP = 16


def _cmp_lane_blocks(n_pages):
    lane = np.arange(n_pages * CMP_PER_PAGE)
    step, rem = lane // CMP_STEP_LANES, lane % CMP_STEP_LANES
    j, pl_ = rem // CMP_PAGES_PER_STEP, rem % CMP_PAGES_PER_STEP
    return (step * CMP_PAGES_PER_STEP + pl_) * CMP_PER_PAGE + j


def _cmp_step_kernel(pt_ref, *refs):
    f32 = jnp.float32
    npg = CMP_PAGES_PER_STEP
    pages = refs[:npg]
    pe_ref, w1_ref, b1_ref, w2_ref, cos_ref, sin_ref, ck_ref, cv_ref, slab_ref = refs[npg:]
    outs = (ck_ref, cv_ref)
    for k, r in enumerate(pages):
        for s in range(2 * NSA_KV_HEADS):
            slab_ref[s, k * HEAD_DIM:(k + 1) * HEAD_DIM, :] = r[0, s]
    for kv in range(2):
        h = jnp.zeros((NSA_KV_HEADS * npg, CMP_PER_PAGE * CMP_HIDDEN), f32)
        for dg in range(HEAD_DIM // CMP_FEATURE_GROUP):
            x = jnp.concatenate(
                [jnp.concatenate([slab_ref[2 * kv + g, pl.ds(d, npg, stride=HEAD_DIM), :]
                                  for g in range(NSA_KV_HEADS)], axis=0) + pe_ref[kv, d]
                 for d in range(dg * CMP_FEATURE_GROUP, (dg + 1) * CMP_FEATURE_GROUP)], axis=1)
            h = h + jnp.dot(x.astype(jnp.bfloat16), w1_ref[kv, dg], preferred_element_type=f32)
        act = jax.nn.gelu(h + b1_ref[kv]).astype(jnp.bfloat16)
        ct = _nt_dot(w2_ref[kv], act)
        tile = jnp.concatenate(
            [jnp.concatenate([ct[j * HEAD_DIM:(j + 1) * HEAD_DIM, g * npg:(g + 1) * npg] for j in range(CMP_PER_PAGE)],
                             axis=1) for g in range(NSA_KV_HEADS)], axis=0)
        if kv == 0:
            row = lax.broadcasted_iota(jnp.int32, tile.shape, 0)
            n = tile.shape[0]
            partner = jnp.where((row % HEAD_DIM) < HEAD_DIM // 2, pltpu.roll(tile, n - HEAD_DIM // 2, 0),
                                pltpu.roll(tile, HEAD_DIM // 2, 0))
            tile = tile * cos_ref[...] + partner * sin_ref[...]
        outs[kv][0] = tile


def _compress_weights(cw1, cb1, cw2, cpe):
    eye = jnp.eye(CMP_PER_PAGE, dtype=jnp.float32)
    w1r = cw1.reshape(2, CMP_BLOCK, HEAD_DIM, CMP_HIDDEN)
    w1 = jnp.einsum('ktdn,ja->kdjtan', w1r, eye).reshape(
        2, HEAD_DIM // CMP_FEATURE_GROUP, CMP_FEATURE_GROUP * PAGE_SIZE, CMP_PER_PAGE * CMP_HIDDEN)
    b1 = jnp.tile(cb1, (1, CMP_PER_PAGE))[:, None, :]
    w2 = jnp.einsum('knd,ja->kjdan', cw2, eye).reshape(2, CMP_PER_PAGE * HEAD_DIM, CMP_PER_PAGE * CMP_HIDDEN)
    pe = jnp.tile(cpe.transpose(0, 2, 1), (1, 1, CMP_PER_PAGE))[:, :, None, :]
    return w1.astype(jnp.bfloat16), b1, w2.astype(jnp.bfloat16), pe


def _nsa_sample_compress(pool, layer_idx, page_table, cw1, cb1, cw2, cpe):
    n_pool, nl = pool.shape[:2]
    db, n_pages = page_table.shape
    npg = CMP_PAGES_PER_STEP
    nchunk = n_pages // npg
    nc = n_pages * CMP_PER_PAGE
    view = jnp.transpose(pool, (0, 1, 3, 4, 5, 2)).reshape(n_pool * nl, 2 * NSA_KV_HEADS, HEAD_DIM, PAGE_SIZE)
    pt = (page_table * nl + layer_idx).reshape(-1).astype(jnp.int32)
    w1, b1, w2, pe = _compress_weights(cw1, cb1, cw2, cpe)
    pos = (jnp.asarray(_cmp_lane_blocks(n_pages)) + 1) * CMP_BLOCK - 1
    half = HEAD_DIM // 2
    inv_freq = ROPE_THETA ** (-2.0 * jnp.arange(half, dtype=jnp.float32) / HEAD_DIM)
    ang = inv_freq[:, None] * pos.astype(jnp.float32)[None, :]
    cos = jnp.tile(jnp.cos(ang), (2 * NSA_KV_HEADS, 1))
    sin = jnp.tile(jnp.concatenate([-jnp.sin(ang), jnp.sin(ang)], axis=0), (NSA_KV_HEADS, 1))

    def page_map(k):
        return lambda b, c, pt_ref: (pt_ref[b * n_pages + c * npg + k], 0, 0, 0)

    const = lambda a: pl.BlockSpec(a.shape, lambda b, c, pt_ref: (0,) * a.ndim, pipeline_mode=pl.Buffered(1))
    lanes_c = lambda: pl.BlockSpec((NSA_KV_W, CMP_STEP_LANES), lambda b, c, pt_ref: (0, c))
    grid_spec = pltpu.PrefetchScalarGridSpec(
        num_scalar_prefetch=1, grid=(db, nchunk),
        in_specs=[pl.BlockSpec((1, 2 * NSA_KV_HEADS, HEAD_DIM, PAGE_SIZE), page_map(k)) for k in range(npg)]
        + [const(pe), const(w1), const(b1), const(w2), lanes_c(), lanes_c()],
        out_specs=[pl.BlockSpec((1, NSA_KV_W, CMP_STEP_LANES), lambda b, c, pt_ref: (b, 0, c))] * 2,
        scratch_shapes=[pltpu.VMEM((2 * NSA_KV_HEADS, npg * HEAD_DIM, PAGE_SIZE), jnp.float32)])
    return pl.pallas_call(
        _cmp_step_kernel, grid_spec=grid_spec,
        out_shape=[jax.ShapeDtypeStruct((db, NSA_KV_W, nc), jnp.float32)] * 2,
        compiler_params=pltpu.CompilerParams(dimension_semantics=("arbitrary", "arbitrary"),
                                             vmem_limit_bytes=VMEM_LIMIT_BYTES),
        name="nsa_sample_compress",
    )(pt, *([view] * npg), pe, w1, b1, w2, cos, sin)


def _nsa_step_kernel(pt_ref, *refs, topn):
    f32, bf16 = jnp.float32, jnp.bfloat16
    npg = STEP_PAGES
    q_ref, ck_ref, cv_ref = refs[:3]
    pages = refs[3:3 + npg]
    (win_ref, newr_ref, newt_ref, gate_ref, blk_ref, exp_ref, o_ref, wout_ref,
     selt_ref, m_ref, l_ref, acc_ref, side_ref) = refs[3 + npg:]
    cc = pl.program_id(1)
    q16 = q_ref[0]
    qb = q16.astype(bf16)
    row16 = lax.broadcasted_iota(jnp.int32, (STEP_ROWS, 1), 0)

    def new_key_scores(krow):
        return jnp.sum(q16 * krow, axis=1, keepdims=True)

    @pl.when(cc == 0)
    def _():
        nc = ck_ref.shape[2]
        s = jnp.dot(qb, ck_ref[0].astype(bf16), preferred_element_type=f32)
        p = jnp.exp(s - jnp.max(s, axis=1, keepdims=True))
        pn = p / jnp.sum(p, axis=1, keepdims=True)
        o_cmp = _nt_dot(pn.astype(bf16), cv_ref[0].astype(bf16))

        rowp = lax.broadcasted_iota(jnp.int32, pn.shape, 0)
        row8 = lax.broadcasted_iota(jnp.int32, (8, nc), 0)
        blk = blk_ref[...]
        val = jnp.full((8, nc), -jnp.inf, f32)
        for g in range(NSA_KV_HEADS):
            ig = jnp.sum(jnp.where((rowp >= g * NSA_GROUP) & (rowp < (g + 1) * NSA_GROUP), pn, 0.0),
                         axis=0, keepdims=True)
            ig = ig + pltpu.roll(ig, nc - CMP_PAGES_PER_STEP, 1)
            vg = jnp.where(blk[:1] < 0, -jnp.inf, jnp.where(blk[1:2] > 0, NSA_FORCE, ig))
            val = jnp.where(row8 == g, vg, val)
        sblk = jnp.where(blk[:1] < 0, nc, blk[:1])
        sel = jnp.zeros((8, nc), f32)
        for _ in range(topn):
            top = jnp.max(val, axis=1, keepdims=True)
            pick = jnp.min(jnp.where(val == top, sblk, nc), axis=1, keepdims=True)
            hit = sblk == pick
            sel = jnp.where(hit, 1.0, sel)
            val = jnp.where(hit, -jnp.inf, val)
        selh = jnp.where(row16 < NSA_GROUP, sel[0:1], jnp.where(row16 < NSA_HEADS, sel[1:2], 0.0))
        wsel = selt_ref.shape[2]
        for j in range(selt_ref.shape[0]):
            selt_ref[j] = selh[:, j * wsel:(j + 1) * wsel]

        m_ref[...] = new_key_scores(newr_ref[0, 0:1, :])
        l_ref[...] = jnp.ones(l_ref.shape, f32)
        acc_ref[...] = jnp.broadcast_to(newr_ref[0, 1:2, :], acc_ref.shape)

        sw = jnp.dot(qb, win_ref[0, 0].astype(bf16), preferred_element_type=f32)
        sn = new_key_scores(newr_ref[0, 2:3, :])
        mw = jnp.maximum(jnp.max(sw, axis=1, keepdims=True), sn)
        pw, pnw = jnp.exp(sw - mw), jnp.exp(sn - mw)
        lw = jnp.sum(pw, axis=1, keepdims=True) + pnw
        o_win = (_nt_dot(pw.astype(bf16), win_ref[0, 1].astype(bf16)) + pnw * newr_ref[0, 3:4, :]) / lw
        gt = jax.nn.sigmoid(gate_ref[0])
        side_ref[...] = gt[:, 0:1] * o_cmp + gt[:, 2:3] * o_win
        wl = win_ref.shape[3]
        lane = lax.broadcasted_iota(jnp.int32, (NSA_KV_W, wl), 1)
        for kv in range(2):
            wout_ref[0, kv] = jnp.where(lane == wl - 1, newt_ref[0, :, 2 + kv:3 + kv],
                                        pltpu.roll(win_ref[0, kv], wl - 1, 1))

    kt = jnp.concatenate([r[0, 0] for r in pages], axis=1)
    vt = jnp.concatenate([r[0, 1] for r in pages], axis=1)
    s = jnp.dot(qb, kt.astype(bf16), preferred_element_type=f32)
    picked = jnp.dot(selt_ref[cc].astype(bf16), exp_ref[...], preferred_element_type=f32)
    s = s + (picked - 1.0) * (-NEG)
    m_old = m_ref[...]
    m_new = jnp.maximum(m_old, jnp.max(s, axis=1, keepdims=True))
    alpha = jnp.exp(m_old - m_new)
    p = jnp.exp(s - m_new)
    m_ref[...] = m_new
    l_ref[...] = l_ref[...] * alpha + jnp.sum(p, axis=1, keepdims=True)
    acc_ref[...] = acc_ref[...] * alpha + _nt_dot(p.astype(bf16), vt.astype(bf16))

    @pl.when(cc == pl.num_programs(1) - 1)
    def _():
        gt = jax.nn.sigmoid(gate_ref[0])
        o_ref[0] = side_ref[...] + gt[:, 1:2] * acc_ref[...] / l_ref[...]


def _nsa_sample_attention(qr, ckt, cvt, slc_pool, layer_idx, page_table, win_buf, newrows, gate):
    db, n_pages = page_table.shape
    n_pool, nl = slc_pool.shape[:2]
    wlen = win_buf.shape[1]
    nc = ckt.shape[2]
    past = n_pages * PAGE_SIZE
    cur = past // SEL_BLOCK
    npg = STEP_PAGES
    nchunk = n_pages // npg
    keys = npg * PAGE_SIZE
    wsel = npg * CMP_PER_PAGE
    f32, bf16 = jnp.float32, jnp.bfloat16
    view = jnp.transpose(slc_pool, (0, 1, 3, 4, 5, 2)).reshape(n_pool * nl, 2, NSA_KV_W, PAGE_SIZE)
    wint = jnp.transpose(win_buf, (0, 2, 3, 4, 1)).reshape(db, 2, NSA_KV_W, wlen)
    pt = (page_table * nl + layer_idx).reshape(-1).astype(jnp.int32)
    hmask = (jnp.arange(NSA_HEADS)[:, None] // NSA_GROUP == jnp.arange(NSA_KV_HEADS)[None, :]).astype(f32)
    q16 = (qr * HEAD_DIM ** -0.5)[:, :, None, :] * hmask[None, :, :, None]
    q16 = jnp.pad(q16.reshape(db, NSA_HEADS, NSA_KV_W), ((0, 0), (0, STEP_ROWS - NSA_HEADS), (0, 0)))
    newr = jnp.pad(newrows, ((0, 0), (0, 8 - newrows.shape[1]), (0, 0)))
    newt = jnp.pad(newrows.transpose(0, 2, 1), ((0, 0), (0, 0), (0, LANES - newrows.shape[1])))
    g16 = jnp.pad(gate.reshape(db, 3, NSA_HEADS).transpose(0, 2, 1),
                  ((0, 0), (0, STEP_ROWS - NSA_HEADS), (0, LANES - 3)))
    cblk = _cmp_lane_blocks(n_pages)
    jj = cblk % CMP_PER_PAGE
    sblk = np.where(jj % 2 == 0, cblk // 2, -1)
    forced = ((sblk == 0) | (sblk == cur - 1)).astype(np.int32)
    blk8 = np.zeros((8, nc), np.int32)
    blk8[0], blk8[1] = sblk, forced
    loc = np.arange(wsel)
    lstep, lrem = loc // CMP_STEP_LANES, loc % CMP_STEP_LANES
    lj, lpage = lrem // CMP_PAGES_PER_STEP, lstep * CMP_PAGES_PER_STEP + lrem % CMP_PAGES_PER_STEP
    kidx = np.arange(keys)
    expand = ((lj[:, None] % 2 == 0) & (kidx[None, :] // PAGE_SIZE == lpage[:, None])
              & ((kidx[None, :] % PAGE_SIZE) // SEL_BLOCK == lj[:, None] // 2)).astype(np.float32)
    topn = min(NSA_TOPN, cur + 1) - 1

    def page_map(k):
        return lambda b, c, pt_ref: (pt_ref[b * n_pages + c * npg + k], 0, 0, 0)

    per_b = lambda shp: pl.BlockSpec((1,) + shp, lambda b, c, pt_ref: (b,) + (0,) * len(shp))
    const = lambda a: pl.BlockSpec(a.shape, lambda b, c, pt_ref: (0,) * a.ndim)
    consts = (jnp.asarray(blk8), jnp.asarray(expand, bf16))
    grid_spec = pltpu.PrefetchScalarGridSpec(
        num_scalar_prefetch=1, grid=(db, nchunk),
        in_specs=[per_b((STEP_ROWS, NSA_KV_W)), per_b((NSA_KV_W, nc)), per_b((NSA_KV_W, nc))]
        + [pl.BlockSpec((1, 2, NSA_KV_W, PAGE_SIZE), page_map(k)) for k in range(npg)]
        + [per_b((2, NSA_KV_W, wlen)), per_b((8, NSA_KV_W)), per_b((NSA_KV_W, LANES)), per_b((STEP_ROWS, LANES))]
        + [const(a) for a in consts],
        out_specs=[per_b((STEP_ROWS, NSA_KV_W)), per_b((2, NSA_KV_W, wlen))],
        scratch_shapes=[pltpu.VMEM((nc // wsel, STEP_ROWS, wsel), f32),
                        pltpu.VMEM((STEP_ROWS, 1), f32), pltpu.VMEM((STEP_ROWS, 1), f32),
                        pltpu.VMEM((STEP_ROWS, NSA_KV_W), f32), pltpu.VMEM((STEP_ROWS, NSA_KV_W), f32)])
    o16, wout = pl.pallas_call(
        functools.partial(_nsa_step_kernel, topn=topn), grid_spec=grid_spec,
        out_shape=[jax.ShapeDtypeStruct((db, STEP_ROWS, NSA_KV_W), f32),
                   jax.ShapeDtypeStruct((db, 2, NSA_KV_W, wlen), f32)],
        compiler_params=pltpu.CompilerParams(dimension_semantics=("arbitrary", "arbitrary"),
                                             vmem_limit_bytes=VMEM_LIMIT_BYTES),
        name="nsa_sample_attention",
    )(pt, q16, ckt, cvt, *([view] * npg), wint, newr, newt, g16, *consts)
    o = o16[:, :NSA_HEADS].reshape(db, NSA_HEADS, NSA_KV_HEADS, HEAD_DIM)
    o = jnp.take_along_axis(o, (jnp.arange(NSA_HEADS) // NSA_GROUP)[None, :, None, None], axis=2)
    wout = jnp.transpose(wout.reshape(db, 2, NSA_KV_HEADS, HEAD_DIM, wlen), (0, 4, 1, 2, 3))
    return o.reshape(db, NSA_HEADS * HEAD_DIM), wout


DIL_ROW_CHUNK = 64


def _dil_step_kernel(q_ref, buf_ref, newt_ref, newr_ref, bias_ref, o_ref, out_ref, p_ref, pn_ref, den_ref):
    f32, bf16 = jnp.float32, jnp.bfloat16
    kv = pl.program_id(1)
    wlen = buf_ref.shape[3]
    nrow = buf_ref.shape[2]
    q16 = q_ref[0]

    @pl.when(kv == 0)
    def _():
        s = jnp.dot(q16.astype(bf16), buf_ref[0, 0].astype(bf16), preferred_element_type=f32)
        s_new = jnp.sum(q16 * newr_ref[0, 0:1, :], axis=1, keepdims=True)
        ms, es, ens, dens = [], [], [], []
        for g in range(len(DIL_GROUPS)):
            sg = s + bias_ref[g:g + 1, :]
            m = jnp.maximum(jnp.max(sg, axis=1, keepdims=True), s_new)
            e, en = jnp.exp(sg - m), jnp.exp(s_new - m)
            ms.append(m); es.append(e); ens.append(en)
            dens.append(jnp.sum(e, axis=1, keepdims=True) + en)
        m_all = functools.reduce(jnp.maximum, ms)
        ws = [jnp.exp(m - m_all) for m in ms]
        p_ref[...] = sum(w * e for w, e in zip(ws, es))
        pn_ref[...] = sum(w * en for w, en in zip(ws, ens))
        den_ref[...] = sum(w * d for w, d in zip(ws, dens))

    @pl.when(kv == 1)
    def _():
        r = _nt_dot(p_ref[...].astype(bf16), buf_ref[0, 0].astype(bf16))
        r = (r + pn_ref[...] * newr_ref[0, 1:2, :]) / den_ref[...]
        head = lax.broadcasted_iota(jnp.int32, r.shape, 1) // HEAD_DIM
        row = lax.broadcasted_iota(jnp.int32, r.shape, 0)
        o_ref[0] = jnp.broadcast_to(jnp.sum(jnp.where(head == row, r, 0.0), axis=0, keepdims=True), o_ref.shape[1:])

    lane = lax.broadcasted_iota(jnp.int32, (DIL_ROW_CHUNK, wlen), 1)
    for c in range(nrow // DIL_ROW_CHUNK):
        rs = slice(c * DIL_ROW_CHUNK, (c + 1) * DIL_ROW_CHUNK)
        col = jnp.where(kv == 0, newt_ref[0, rs, 0:1], newt_ref[0, rs, 1:2])
        out_ref[0, 0, rs, :] = jnp.where(lane == wlen - 1, col, pltpu.roll(buf_ref[0, 0, rs, :], wlen - 1, 1))


def _dil_sample_attention(qr, kr_new, v_new, buf):
    db, wlen = buf.shape[:2]
    f32 = jnp.float32
    buft = jnp.transpose(buf, (0, 2, 3, 4, 1)).reshape(db, 2, DIL_W, wlen)
    eye = jnp.eye(DIL_HEADS, dtype=f32)
    q16 = ((qr * HEAD_DIM ** -0.5)[:, :, None, :] * eye[None, :, :, None]).reshape(db, DIL_HEADS, DIL_W)
    newr = jnp.pad(jnp.stack([kr_new, v_new], axis=1), ((0, 0), (0, 6), (0, 0)))
    newt = jnp.pad(jnp.stack([kr_new, v_new], axis=2), ((0, 0), (0, 0), (0, LANES - 2)))
    back = wlen - jnp.arange(wlen)
    bias = jnp.stack([jnp.where((back % d == 0) & (back // d <= DIL_SPAN), 0.0, NEG) for _, d in DIL_GROUPS])
    bias = jnp.pad(bias, ((0, 8 - len(DIL_GROUPS)), (0, 0))).astype(f32)
    o, new_buf = pl.pallas_call(
        _dil_step_kernel,
        grid=(db, 2),
        in_specs=[pl.BlockSpec((1, DIL_HEADS, DIL_W), lambda b, k: (b, 0, 0)),
                  pl.BlockSpec((1, 1, DIL_W, wlen), lambda b, k: (b, k, 0, 0)),
                  pl.BlockSpec((1, DIL_W, LANES), lambda b, k: (b, 0, 0)),
                  pl.BlockSpec((1, 8, DIL_W), lambda b, k: (b, 0, 0)),
                  pl.BlockSpec((8, wlen), lambda b, k: (0, 0))],
        out_specs=[pl.BlockSpec((1, 8, DIL_W), lambda b, k: (b, 0, 0)),
                   pl.BlockSpec((1, 1, DIL_W, wlen), lambda b, k: (b, k, 0, 0))],
        out_shape=[jax.ShapeDtypeStruct((db, 8, DIL_W), f32),
                   jax.ShapeDtypeStruct((db, 2, DIL_W, wlen), f32)],
        scratch_shapes=[pltpu.VMEM((DIL_HEADS, wlen), f32), pltpu.VMEM((DIL_HEADS, 1), f32),
                        pltpu.VMEM((DIL_HEADS, 1), f32)],
        compiler_params=pltpu.CompilerParams(dimension_semantics=("arbitrary", "arbitrary"),
                                             vmem_limit_bytes=VMEM_LIMIT_BYTES),
        name="dil_sample",
    )(q16, buft, newt, newr, bias)
    new_buf = jnp.transpose(new_buf.reshape(db, 2, DIL_HEADS, HEAD_DIM, wlen), (0, 4, 1, 2, 3))
    return o[:, 0], new_buf


def _dil_band_kernel(q_ref, kp_ref, kc_ref, vp_ref, vc_ref, num_ref, st_ref):
    f32, bf16 = jnp.float32, jnp.bfloat16
    blk = DIL_BLOCK
    n = pl.program_id(0)
    i = lax.broadcasted_iota(jnp.int32, (blk, 2 * blk), 0)
    j = lax.broadcasted_iota(jnp.int32, (blk, 2 * blk), 1) - blk
    ok = (i - j >= 0) & (i - j <= DIL_SPAN) & (n * blk + j >= 0)
    bias = jnp.where(ok, 0.0, NEG)
    bias = jnp.concatenate([bias, bias], axis=0)
    lane = lax.broadcasted_iota(jnp.int32, (blk, LANES), 1)
    first = lane < HEAD_DIM
    stats = jnp.zeros((blk, LANES), f32)
    for p in range(DIL_HEADS // 2):
        cols = slice(p * LANES, (p + 1) * LANES)
        qp = q_ref[:, cols] * (HEAD_DIM ** -0.5)
        qst = jnp.concatenate([jnp.where(first, qp, 0.0), jnp.where(first, 0.0, qp)], axis=0).astype(bf16)
        kk = jnp.concatenate([kp_ref[:, cols], kc_ref[:, cols]], axis=0).astype(bf16)
        vv = jnp.concatenate([vp_ref[:, cols], vc_ref[:, cols]], axis=0).astype(bf16)
        s = _nt_dot(qst, kk) + bias
        m = jnp.max(s, axis=1, keepdims=True)
        e = jnp.exp(s - m)
        den = jnp.sum(e, axis=1, keepdims=True)
        nm = jnp.dot(e.astype(bf16), vv, preferred_element_type=f32)
        num_ref[:, cols] = jnp.where(first, nm[:blk], nm[blk:])
        for a in range(2):
            h = 2 * p + a
            stats = jnp.where(lane == h, m[a * blk:(a + 1) * blk], stats)
            stats = jnp.where(lane == DIL_HEADS + h, den[a * blk:(a + 1) * blk], stats)
    st_ref[...] = stats


GDN_PREP_ROWS = 512
GDN_TILE_CHUNKS = 4
GDN_PAIRS = GDN_HEADS // 2
GDN_A_LANE = 3 * NSA_HEADS
GDN_B_LANE = GDN_A_LANE + GDN_HEADS


def _hi_lo(x):
    hi = x.astype(jnp.bfloat16)
    return hi, (x - hi.astype(jnp.float32)).astype(jnp.bfloat16)


def _three_way(x):
    f32 = jnp.float32
    x1 = x.astype(jnp.bfloat16)
    r1 = x - x1.astype(f32)
    x2 = r1.astype(jnp.bfloat16)
    return x1, x2, (r1 - x2.astype(f32)).astype(jnp.bfloat16)


def _dot_select(x, sel):
    return sum(jnp.dot(piece, sel, preferred_element_type=jnp.float32) for piece in _three_way(x))


def _select_dot(sel, x):
    return sum(jnp.dot(sel, piece, preferred_element_type=jnp.float32) for piece in _three_way(x))


def _dot_hl(a, b):
    f32 = jnp.float32
    ah, al = _hi_lo(a)
    bh, bl = _hi_lo(b)
    return (jnp.dot(ah, bh, preferred_element_type=f32) + jnp.dot(ah, bl, preferred_element_type=f32)
            + jnp.dot(al, bh, preferred_element_type=f32))


def _gdn_prep_kernel(u_ref, sm_ref, cw_ref, prm_ref, ea_ref, eb_ref, eh_ref, q_ref, k_ref, v_ref, g_ref, b_ref,
                     carry_ref):
    f32 = jnp.float32
    i = pl.program_id(0)
    tl = u_ref.shape[0]

    @pl.when(i == 0)
    def _():
        carry_ref[...] = jnp.zeros(carry_ref.shape, f32)

    u = u_ref[...]
    prev = carry_ref[...]
    row = lax.broadcasted_iota(jnp.int32, u.shape, 0)

    def shifted(k):
        r = pltpu.roll(u, k, 0)
        for j in range(k):
            r = jnp.where(row == j, prev[8 - k + j:8 - k + j + 1], r)
        return r

    cw = cw_ref[...]
    c = cw[0:1] * shifted(3) + cw[1:2] * shifted(2) + cw[2:3] * shifted(1) + cw[3:4] * u
    carry_ref[...] = u[tl - 8:]
    c = c * jax.nn.sigmoid(c)
    eh = eh_ref[...]

    def l2n(x):
        return x * lax.rsqrt(_dot_select(x * x, eh) + NORM_EPS)

    q_ref[...] = l2n(c[:, :GDN_W]) * (HEAD_DIM ** -0.5)
    k_ref[...] = l2n(c[:, GDN_W:2 * GDN_W])
    v_ref[...] = c[:, 2 * GDN_W:]
    sm = sm_ref[...]
    x = sm + prm_ref[1:2]
    softplus = jnp.maximum(x, 0.0) + jnp.log(1.0 + jnp.exp(-jnp.abs(x)))
    g_ref[...] = _dot_select(-jnp.exp(prm_ref[0:1]) * softplus, ea_ref[...])
    b_ref[...] = _dot_select(jax.nn.sigmoid(sm), eb_ref[...])


def _gdn_chunk_kernel(q_ref, k_ref, v_ref, g_ref, b_ref, z_ref, nw_ref, lt_ref, eh_ref, o_ref, s_out_ref, s_ref):
    f32 = jnp.float32
    ch = GDN_CHUNK
    i = pl.program_id(0)

    @pl.when(i == 0)
    def _():
        s_ref[...] = jnp.zeros(s_ref.shape, f32)

    lane = lax.broadcasted_iota(jnp.int32, (ch, LANES), 1)
    first = lane < HEAD_DIM
    stack = lambda x: jnp.concatenate([jnp.where(first, x, 0.0), jnp.where(first, 0.0, x)], axis=0)
    r2 = lax.broadcasted_iota(jnp.int32, (2 * ch, 2 * ch), 0)
    c2 = lax.broadcasted_iota(jnp.int32, (2 * ch, 2 * ch), 1)
    same = (r2 // ch) == (c2 // ch)
    tri = same & (r2 % ch >= c2 % ch)
    strict = same & (r2 % ch > c2 % ch)
    eye = r2 == c2
    eye_f = jnp.where(eye, 1.0, 0.0)
    diag2 = lax.broadcasted_iota(jnp.int32, (ch, LANES), 0) == lane % HEAD_DIM
    lt = lt_ref[...]
    bf = lambda x: x.astype(jnp.bfloat16)
    dot = lambda a, b: jnp.dot(bf(a), bf(b), preferred_element_type=f32)

    blocks = [(c, p) for c in range(GDN_TILE_CHUNKS) for p in range(GDN_PAIRS)]
    ld = lambda ref, c, p: ref[c * ch:(c + 1) * ch, p * LANES:(p + 1) * LANES]
    gcs = [_select_dot(lt, ld(g_ref, c, p)) for c, p in blocks]
    amats, qks, rhs_u, rhs_w, qgs, kds, decs = [], [], [], [], [], [], []
    for (c, p), gc in zip(blocks, gcs):
        kk, qq, vv, bb = ld(k_ref, c, p), ld(q_ref, c, p), ld(v_ref, c, p), ld(b_ref, c, p)
        eg = jnp.exp(gc)
        g_end = gc[ch - 1:ch]
        kb = kk * bb
        col = jnp.concatenate([jnp.broadcast_to(gc[:, 0:1], (ch, LANES)),
                               jnp.broadcast_to(gc[:, HEAD_DIM:HEAD_DIM + 1], (ch, LANES))], axis=0)
        rowv = jnp.sum(jnp.where(diag2, gc, 0.0), axis=0, keepdims=True)
        gam = jnp.where(tri, jnp.exp(jnp.where(tri, col - rowv, 0.0)), 0.0)
        kst = stack(kk)
        amats.append(jnp.where(strict, _nt_dot(bf(stack(kb)), bf(kst)) * gam, 0.0))
        qks.append(jnp.where(tri, _nt_dot(bf(stack(qq)), bf(kst)) * gam, 0.0))
        rhs_u.append(stack(vv * bb))
        rhs_w.append(stack(kb * eg))
        qgs.append(stack(qq * eg))
        kds.append(stack(kk * jnp.exp(g_end - gc)))
        decs.append(jnp.sum(jnp.where(eye, jnp.exp(g_end), 0.0), axis=1, keepdims=True))
    xs = [eye_f - a for a in amats]
    pws = [_dot_hl(a, a) for a in amats]
    steps = GDN_CHUNK.bit_length() - 2
    for r in range(steps):
        xs = [x + _dot_hl(x, pw) for x, pw in zip(xs, pws)]
        if r < steps - 1:
            pws = [_dot_hl(pw, pw) for pw in pws]
    uus = [dot(x, u) for x, u in zip(xs, rhs_u)]
    wws = [dot(x, w) for x, w in zip(xs, rhs_w)]
    kdts = [kd.T for kd in kds]
    states = [s_ref[p] for p in range(GDN_PAIRS)]
    for c in range(GDN_TILE_CHUNKS):
        rs = slice(c * ch, (c + 1) * ch)
        outs = []
        for p in range(GDN_PAIRS):
            n = c * GDN_PAIRS + p
            s = states[p]
            v_new = uus[n] - dot(wws[n], s)
            o_st = dot(qgs[n], s) + dot(qks[n], v_new)
            states[p] = s * decs[n] + dot(kdts[n], v_new)
            outs.append(o_st[:ch] + o_st[ch:])
        o = jnp.concatenate(outs, axis=1)
        ms = _dot_select(o * o, eh_ref[...]) * (1.0 / HEAD_DIM)
        z = z_ref[rs, :]
        o_ref[rs, :] = o * lax.rsqrt(ms + NORM_EPS) * nw_ref[...] * (z * jax.nn.sigmoid(z))
    for p in range(GDN_PAIRS):
        s_ref[p] = states[p]

    @pl.when(i == pl.num_programs(0) - 1)
    def _():
        s_out_ref[...] = s_ref[...]


def _gdn_prompt(qkv, small, z, conv_w, a_log, dt_bias, norm_w):
    l = qkv.shape[0]
    f32, bf16 = jnp.float32, jnp.bfloat16
    w = GDN_W
    hh = jnp.arange(w) // HEAD_DIM
    expander = lambda base: (jnp.arange(LANES)[:, None] == base + hh[None, :]).astype(bf16)
    eh = (hh[:, None] == hh[None, :]).astype(bf16)
    cw8 = jnp.zeros((8, 3 * w), f32).at[:GDN_CONV].set(conv_w)
    prm = jnp.zeros((8, LANES), f32)
    prm = prm.at[0, GDN_A_LANE:GDN_A_LANE + GDN_HEADS].set(a_log).at[1, GDN_A_LANE:GDN_A_LANE + GDN_HEADS].set(dt_bias)
    tl = GDN_PREP_ROWS
    row = lambda wd: pl.BlockSpec((tl, wd), lambda i: (i, 0))
    const = lambda a: pl.BlockSpec(a.shape, lambda i: (0,) * a.ndim)
    ea, eb = expander(GDN_A_LANE), expander(GDN_B_LANE)
    q, k, v, g, b = pl.pallas_call(
        _gdn_prep_kernel,
        grid=(l // tl,),
        in_specs=[row(3 * w), row(LANES), const(cw8), const(prm), const(ea), const(eb), const(eh)],
        out_specs=[row(w)] * 5,
        out_shape=[jax.ShapeDtypeStruct((l, w), f32)] * 5,
        scratch_shapes=[pltpu.VMEM((8, 3 * w), f32)],
        compiler_params=pltpu.CompilerParams(dimension_semantics=("arbitrary",), vmem_limit_bytes=VMEM_LIMIT_BYTES),
        name="gdn_prep",
    )(qkv, small, cw8, prm, ea, eb, eh)
    tc = GDN_TILE_CHUNKS * GDN_CHUNK
    lt = (jnp.arange(GDN_CHUNK)[:, None] >= jnp.arange(GDN_CHUNK)[None, :]).astype(bf16)
    nw = jnp.tile(norm_w, GDN_HEADS).reshape(1, w)
    rowc = pl.BlockSpec((tc, w), lambda i: (i, 0))
    o, s_bd = pl.pallas_call(
        _gdn_chunk_kernel,
        grid=(l // tc,),
        in_specs=[rowc] * 6 + [const(nw), const(lt), const(eh)],
        out_specs=[rowc, pl.BlockSpec((GDN_PAIRS, LANES, LANES), lambda i: (0, 0, 0))],
        out_shape=[jax.ShapeDtypeStruct((l, w), f32), jax.ShapeDtypeStruct((GDN_PAIRS, LANES, LANES), f32)],
        scratch_shapes=[pltpu.VMEM((GDN_PAIRS, LANES, LANES), f32)],
        compiler_params=pltpu.CompilerParams(dimension_semantics=("arbitrary",), vmem_limit_bytes=VMEM_LIMIT_BYTES),
        name="gdn_chunk",
    )(q, k, v, g, b, z, nw, lt, eh)
    s4 = s_bd.reshape(GDN_PAIRS, 2, HEAD_DIM, 2, HEAD_DIM)
    s_fin = jnp.stack([s4[:, 0, :, 0], s4[:, 1, :, 1]], axis=1).reshape(GDN_HEADS, HEAD_DIM, HEAD_DIM)
    return o, s_fin


def _proj_dil_kernel(x_ref, w_ref, cos_ref, sin_ref, bf_ref, kv_ref):
    acc = jnp.dot(x_ref[...].astype(jnp.bfloat16), w_ref[...], preferred_element_type=jnp.float32)
    reps = 2 * DIL_W // LANES
    qk = _rope_lanes(acc[:, :2 * DIL_W], _lane_tile(cos_ref[...], reps), _lane_tile(sin_ref[...], reps))
    v = acc[:, 2 * DIL_W:]
    bf_ref[...] = jnp.concatenate([qk, v], axis=1).astype(jnp.bfloat16)
    kv_ref[...] = jnp.concatenate([qk[:, DIL_W:], v], axis=1)


def _proj_dil(x, w_in):
    l, d = x.shape
    tm = _row_tile(l)
    cos, sin = _rope_tables(jnp.arange(l), LANES)
    return pl.pallas_call(
        _proj_dil_kernel,
        grid=(l // tm,),
        in_specs=[pl.BlockSpec((tm, d), lambda i: (i, 0)),
                  pl.BlockSpec((d, 3 * DIL_W), lambda i: (0, 0)),
                  pl.BlockSpec((tm, LANES), lambda i: (i, 0)),
                  pl.BlockSpec((tm, LANES), lambda i: (i, 0))],
        out_specs=[pl.BlockSpec((tm, 3 * DIL_W), lambda i: (i, 0)),
                   pl.BlockSpec((tm, 2 * DIL_W), lambda i: (i, 0))],
        out_shape=[jax.ShapeDtypeStruct((l, 3 * DIL_W), jnp.bfloat16),
                   jax.ShapeDtypeStruct((l, 2 * DIL_W), jnp.float32)],
        compiler_params=pltpu.CompilerParams(dimension_semantics=("arbitrary",), vmem_limit_bytes=VMEM_LIMIT_BYTES),
        name="proj_dil",
    )(x, w_in.astype(jnp.bfloat16), cos, sin)


def _dil_band_stats(qkv, d):
    l = qkv.shape[0]
    assert l % (d * DIL_BLOCK) == 0
    nb = l // (d * DIL_BLOCK)
    view = qkv.reshape(l // d, d * 3 * DIL_W)

    def part(which, prev):
        return pl.BlockSpec((DIL_BLOCK, DIL_W), (lambda n, r: (jnp.maximum(n - 1, 0), 3 * r + which)) if prev
                            else (lambda n, r: (n, 3 * r + which)))

    out = lambda w: pl.BlockSpec((DIL_BLOCK, w), lambda n, r: (n, r))
    num, st = pl.pallas_call(
        _dil_band_kernel,
        grid=(nb, d),
        in_specs=[part(0, False), part(1, True), part(1, False), part(2, True), part(2, False)],
        out_specs=[out(DIL_W), out(LANES)],
        out_shape=[jax.ShapeDtypeStruct((l // d, d * DIL_W), jnp.float32),
                   jax.ShapeDtypeStruct((l // d, d * LANES), jnp.float32)],
        compiler_params=pltpu.CompilerParams(dimension_semantics=("arbitrary", "arbitrary"),
                                             vmem_limit_bytes=VMEM_LIMIT_BYTES),
        name="dil_band_stats",
    )(view, view, view, view, view)
    return num.reshape(l, DIL_W), st.reshape(l, LANES)


def _dil_merge_kernel(*refs):
    f32 = jnp.float32
    ng = len(DIL_GROUPS)
    nums, sts = refs[:ng], refs[ng:2 * ng]
    ex_ref, w_ref, res_ref, g_ref, b_ref, o_ref = refs[2 * ng:]
    st = [r[...] for r in sts]
    m_all = functools.reduce(jnp.maximum, st)
    ws = [jnp.exp(s - m_all) for s in st]
    den = sum(w * pltpu.roll(s, LANES - DIL_HEADS, 1) for w, s in zip(ws, st))
    head_lane = lax.broadcasted_iota(jnp.int32, den.shape, 1) < DIL_HEADS
    o = sum(_dot_select(jnp.where(head_lane, w / den, 0.0), ex_ref[...]) * n[...]
            for w, n in zip(ws, nums))
    acc = jnp.dot(o.astype(jnp.bfloat16), w_ref[...], preferred_element_type=f32)
    o_ref[...] = _layer_norm_rows(DEEPNORM_ALPHA * res_ref[...] + acc, g_ref[...], b_ref[...])


def _dil_merge_proj(nums, sts, w_out, res, g, b):
    l, n = res.shape
    tm = _row_tile(l)
    expand = (jnp.arange(LANES)[:, None] == jnp.arange(DIL_W)[None, :] // HEAD_DIM).astype(jnp.bfloat16)
    row = lambda w: pl.BlockSpec((tm, w), lambda i: (i, 0))
    const = lambda a: pl.BlockSpec(a.shape, lambda i: (0,) * a.ndim)
    wb = w_out.astype(jnp.bfloat16)
    g2, b2 = g.reshape(1, n), b.reshape(1, n)
    return pl.pallas_call(
        _dil_merge_kernel,
        grid=(l // tm,),
        in_specs=[row(DIL_W)] * len(nums) + [row(LANES)] * len(sts) + [const(expand), const(wb), row(n),
                                                                       const(g2), const(b2)],
        out_specs=row(n),
        out_shape=jax.ShapeDtypeStruct((l, n), jnp.float32),
        compiler_params=pltpu.CompilerParams(dimension_semantics=("arbitrary",), vmem_limit_bytes=VMEM_LIMIT_BYTES),
        name="dil_merge_proj",
    )(*nums, *sts, expand, wb, res, g2, b2)


def _split_cols(h, widths):
    parts, start = [], 0
    for w in widths:
        parts.append(h[..., start:start + w])
        start += w
    return parts


def _even_widths():
    return (NSA_Q_W,) + (NSA_KV_W,) * 6 + (3 * NSA_HEADS, 3 * GDN_W, GDN_HEADS, GDN_HEADS, GDN_W)


def _rms_norm(x, w):
    return x * lax.rsqrt(jnp.mean(jnp.square(x), axis=-1, keepdims=True) + NORM_EPS) * w


def _l2_norm(x):
    return x * lax.rsqrt(jnp.sum(jnp.square(x), axis=-1, keepdims=True) + NORM_EPS)


def _rope(x, pos):
    half = HEAD_DIM // 2
    inv_freq = ROPE_THETA ** (-2.0 * jnp.arange(half, dtype=jnp.float32) / HEAD_DIM)
    ang = pos.astype(jnp.float32)[:, None] * inv_freq[None, :]
    cos, sin = jnp.cos(ang)[:, None, :], jnp.sin(ang)[:, None, :]
    xf = x.astype(jnp.float32)
    x1, x2 = xf[..., :half], xf[..., half:]
    return jnp.concatenate([x1 * cos - x2 * sin, x2 * cos + x1 * sin], axis=-1)


def _causal_dwconv(hist, u, w):
    width, s = w.shape[0], u.shape[1]
    ext = jnp.concatenate([hist.astype(u.dtype), u], axis=1)
    out = w[0] * ext[:, :s]
    for j in range(1, width):
        out = out + w[j] * ext[:, j:j + s]
    return out, ext[:, s:]


def _nsa_compress(rows, w1, b1, w2, pe):
    b, l, g, dh = rows.shape
    nc = l // CMP_BLOCK
    blk = rows[:, :nc * CMP_BLOCK].astype(jnp.float32).reshape(b, nc, CMP_BLOCK, g, dh) + pe[:, None, :]
    flat = blk.transpose(0, 1, 3, 2, 4).reshape(b, nc, g, CMP_BLOCK * dh)
    return jax.nn.gelu(flat @ w1 + b1) @ w2


def _nsa_compressed_kv(k_rows, v_rows, cw1, cb1, cw2, cpe):
    ck = _nsa_compress(k_rows, cw1[0], cb1[0], cw2[0], cpe[0])
    cv = _nsa_compress(v_rows, cw1[1], cb1[1], cw2[1], cpe[1])
    nc = ck.shape[1]
    ck = _rope(ck, (jnp.arange(nc) + 1) * CMP_BLOCK - 1)
    return ck, cv


def _rope_rows_kernel(x_ref, cos_ref, sin_ref, o_ref):
    reps = x_ref.shape[1] // LANES
    o_ref[...] = _rope_lanes(x_ref[...], _lane_tile(cos_ref[...], reps), _lane_tile(sin_ref[...], reps))


def _rope_rows(x):
    l, w = x.shape
    tm = _row_tile(l)
    cos, sin = _rope_tables(jnp.arange(l), LANES)
    return pl.pallas_call(
        _rope_rows_kernel,
        grid=(l // tm,),
        in_specs=[pl.BlockSpec((tm, w), lambda i: (i, 0)), pl.BlockSpec((tm, LANES), lambda i: (i, 0)),
                  pl.BlockSpec((tm, LANES), lambda i: (i, 0))],
        out_specs=pl.BlockSpec((tm, w), lambda i: (i, 0)),
        out_shape=jax.ShapeDtypeStruct((l, w), jnp.float32),
        compiler_params=pltpu.CompilerParams(dimension_semantics=("arbitrary",), vmem_limit_bytes=VMEM_LIMIT_BYTES),
        name="rope_rows",
    )(x, cos, sin)


def _nsa_prompt(q, kc, vc, ks, vs, kw, vw, small, cw1, cb1, cw2, cpe):
    b, l = q.shape[:2]
    flat = lambda t: t.reshape(t.shape[1], -1)
    roped = _rope_rows(jnp.concatenate([flat(q), flat(ks), flat(kw)], axis=1))
    qr = roped[:, :NSA_Q_W]
    ksr = roped[:, NSA_Q_W:NSA_Q_W + NSA_KV_W].reshape(ks.shape)
    kwr = roped[:, NSA_Q_W + NSA_KV_W:].reshape(kw.shape)
    ck, cv = _nsa_compressed_kv(kc, vc, cw1, cb1, cw2, cpe)
    vsf = vs.astype(jnp.float32)
    vwf = vw.astype(jnp.float32)
    o_nsa = _nsa_prompt_attention(qr, small, flat(ck), flat(cv), flat(ksr), flat(vsf), flat(kwr), flat(vwf))
    keep = min(NSA_WINDOW, l)
    rows_cmp = jnp.stack([kc, vc], axis=2)
    rows_slc = jnp.stack([ksr, vsf], axis=2)
    rows_win = jnp.stack([kwr[:, l - keep:], vwf[:, l - keep:]], axis=2)
    return o_nsa[None], rows_cmp, rows_slc, rows_win


def _nsa_sample(q, kc, vc, ks, vs, kw, vw, gate, cmp_pool, slc_pool, layer_idx, win_buf, page_table,
                cw1, cb1, cw2, cpe):
    db, s = q.shape[:2]
    past = page_table.shape[1] * PAGE_SIZE
    wb = win_buf.shape[1]
    assert s == 1 and wb == NSA_WINDOW and past >= wb and past % (STEP_PAGES * PAGE_SIZE) == 0
    qpos = past + jnp.arange(s)
    qr = _rope(q, qpos)
    ckt, cvt = _nsa_sample_compress(cmp_pool, layer_idx, page_table, cw1, cb1, cw2, cpe)
    ksr = _rope(ks, qpos)
    vsf = vs.astype(jnp.float32)
    kwr = _rope(kw, qpos)
    vwf = vw.astype(jnp.float32)
    newrows = jnp.stack([t.reshape(db, NSA_KV_W) for t in (ksr, vsf, kwr, vwf)], axis=1)
    o_nsa, rows_win = _nsa_sample_attention(qr[:, 0], ckt, cvt, slc_pool, layer_idx, page_table, win_buf,
                                              newrows, gate.reshape(db, -1))
    rows_cmp = jnp.stack([kc, vc], axis=2)
    rows_slc = jnp.stack([ksr, vsf], axis=2)
    return o_nsa[:, None], rows_cmp, rows_slc, rows_win


def _gdn_recurrent(q, k, v, g, beta, s0):
    def step(state, xs):
        q_t, k_t, v_t, g_t, b_t = xs
        state = state * jnp.exp(g_t)[..., None, None]
        v_t = (v_t - jnp.einsum('bhk,bhkv->bhv', k_t, state)) * b_t[..., None]
        state = state + jnp.einsum('bhk,bhv->bhkv', k_t, v_t)
        return state, jnp.einsum('bhk,bhkv->bhv', q_t, state)

    xs = tuple(jnp.moveaxis(a, 1, 0) for a in (q, k, v, g, beta))
    s_fin, o = lax.scan(step, s0, xs)
    return jnp.moveaxis(o, 0, 1), s_fin


def _gdn_step(qkv, a, bt, z, conv_hist, s0, conv_w, a_log, dt_bias, norm_w):
    b, s = qkv.shape[:2]
    c, new_hist = _causal_dwconv(conv_hist, qkv, conv_w)
    c = jax.nn.silu(c.astype(jnp.float32))
    q, k, v = [t.reshape(b, s, GDN_HEADS, HEAD_DIM) for t in jnp.split(c, 3, axis=-1)]
    q = _l2_norm(q) * HEAD_DIM ** -0.5
    k = _l2_norm(k)
    beta = jax.nn.sigmoid(bt.astype(jnp.float32))
    g = -jnp.exp(a_log) * jax.nn.softplus(a.astype(jnp.float32) + dt_bias)
    o, s_fin = _gdn_recurrent(q, k, v, g, beta, s0.astype(jnp.float32))
    o = _rms_norm(o, norm_w) * jax.nn.silu(z.astype(jnp.float32).reshape(b, s, GDN_HEADS, HEAD_DIM))
    return o.reshape(b, s, GDN_W), new_hist, s_fin


def _proj(x, w):
    b, s, d = x.shape
    n = w.shape[1]
    npad = -(-n // LANES) * LANES
    wp = jnp.pad(w, ((0, 0), (0, npad - n)))
    return _matmul(x.reshape(b * s, d), wp).reshape(b, s, npad)


def _even_prompt(x, w_in, cw1, cb1, cw2, cpe, conv_w, a_log, dt_bias, norm_w):
    b, l, _ = x.shape
    q, kc, vc, ks, vs, kw, vw, gate, qkv, a, bt, z = _split_cols(_proj(x, w_in), _even_widths())
    heads = lambda t: t.reshape(b, l, -1, HEAD_DIM)
    assert b == 1
    small = jnp.concatenate([gate, a, bt], axis=-1).reshape(l, -1)
    small = jnp.pad(small, ((0, 0), (0, LANES - small.shape[-1])))
    o_nsa, r_cmp, r_slc, r_win = _nsa_prompt(
        heads(q), heads(kc), heads(vc), heads(ks), heads(vs), heads(kw), heads(vw), small, cw1, cb1, cw2, cpe)
    o_gdn, s_fin = _gdn_prompt(qkv[0], small, z[0], conv_w, a_log, dt_bias, norm_w)
    conv_hist = qkv[:, l - (GDN_CONV - 1):]
    return jnp.concatenate([o_nsa, o_gdn[None]], axis=-1), r_cmp, r_slc, r_win, conv_hist, s_fin[None]


def _even_sample(x, cmp_pool, slc_pool, layer_idx, win_buf, conv_hist, s0, page_table,
                 w_in, cw1, cb1, cw2, cpe, conv_w, a_log, dt_bias, norm_w):
    b, s, _ = x.shape
    q, kc, vc, ks, vs, kw, vw, gate, qkv, a, bt, z = _split_cols(_proj(x, w_in), _even_widths())
    heads = lambda t: t.reshape(b, s, -1, HEAD_DIM)
    o_nsa, r_cmp, r_slc, r_win = _nsa_sample(
        heads(q), heads(kc), heads(vc), heads(ks), heads(vs), heads(kw), heads(vw), gate,
        cmp_pool, slc_pool, layer_idx, win_buf, page_table, cw1, cb1, cw2, cpe)
    o_gdn, new_hist, s_fin = _gdn_step(qkv, a, bt, z, conv_hist, s0, conv_w, a_log, dt_bias, norm_w)
    return jnp.concatenate([o_nsa, o_gdn], axis=-1), r_cmp, r_slc, r_win, new_hist, s_fin


def _dil_prompt(x, w_in, w_out, g, b):
    bsz, l, _ = x.shape
    assert bsz == 1
    qkv, kv = _proj_dil(x[0], w_in)
    stats = [_dil_band_stats(qkv, d) for _, d in DIL_GROUPS]
    y = _dil_merge_proj([n for n, _ in stats], [s for _, s in stats], w_out, x[0], g, b)
    keep = min(DIL_MAX_WINDOW, l)
    buf = kv[l - keep:].reshape(1, keep, 2, DIL_HEADS, HEAD_DIM)
    return y, buf


def _dil_sample(x, buf, past, w_in):
    db, s, _ = x.shape
    q, k, v = [t.reshape(db, s, DIL_HEADS, HEAD_DIM) for t in jnp.split(_proj(x, w_in), 3, axis=-1)]
    assert s == 1 and buf.shape[1] == DIL_MAX_WINDOW <= past
    qpos = past + jnp.arange(s)
    qr, kr = _rope(q, qpos), _rope(k, qpos)
    o, new_buf = _dil_sample_attention(qr[:, 0], kr.reshape(db, DIL_W), v.reshape(db, DIL_W).astype(jnp.float32), buf)
    return o[:, None], new_buf


def kernel(x_prompt, x_sample, cache_nsa_cmp_kv, cache_nsa_slc_kv, state_nsa_win_kv, state_gdn_conv,
           state_gdn_S, state_dil_kv, state_ffn_conv, page_table, w_in_a, nsa_cmp_w1, nsa_cmp_b1, nsa_cmp_w2,
           nsa_cmp_pe, gdn_conv_w, gdn_A_log, gdn_dt_bias, gdn_norm_w, w_out_a, w_in_c, w_out_c,
           ln_mix_g, ln_mix_b, ffn_w_in, ffn_conv_w, ffn_conv_b, ffn_w_out, ln_ffn_g, ln_ffn_b):
    past = page_table.shape[1] * PAGE_SIZE
    bp, lp, d = x_prompt.shape
    bs, ls, _ = x_sample.shape
    assert bp == 1 and ls == 1
    xp, xs = x_prompt, x_sample
    cmp_p, cmp_s, slc_p, slc_s, win_p, win_s = [], [], [], [], [], []
    gconv_p, gconv_s, gstate_p, gstate_s = [], [], [], []
    dil_p, dil_s, ffn_p, ffn_s = [], [], [], []
    for layer in range(DEPTH):
        if layer % 2 == 0:
            la = layer // 2
            wa = (w_in_a[la], nsa_cmp_w1[la], nsa_cmp_b1[la], nsa_cmp_w2[la], nsa_cmp_pe[la],
                  gdn_conv_w[la], gdn_A_log[la], gdn_dt_bias[la], gdn_norm_w[la])
            mp, rc, rs, rw, hc, hs_ = _even_prompt(xp, *wa)
            cmp_p.append(rc); slc_p.append(rs); win_p.append(rw); gconv_p.append(hc); gstate_p.append(hs_)
            ms, rc, rs, rw, hc, hs_ = _even_sample(xs, cache_nsa_cmp_kv, cache_nsa_slc_kv, la,
                                                   state_nsa_win_kv[:, la], state_gdn_conv[:, la],
                                                   state_gdn_S[:, la], page_table, *wa)
            cmp_s.append(rc); slc_s.append(rs); win_s.append(rw); gconv_s.append(hc); gstate_s.append(hs_)
            w_out = w_out_a[la]
        else:
            lc = layer // 2
            xp2, bpf = _dil_prompt(xp, w_in_c[lc], w_out_c[lc], ln_mix_g[layer], ln_mix_b[layer])
            ms, bsf = _dil_sample(xs, state_dil_kv[:, lc], past, w_in_c[lc])
            dil_p.append(bpf); dil_s.append(bsf)
            w_out = w_out_c[lc]
        if layer % 2 == 0:
            xp2 = _matmul_ln(mp.reshape(lp, -1), w_out, xp.reshape(lp, d), ln_mix_g[layer], ln_mix_b[layer])
        xs2 = _matmul_ln(ms.reshape(bs, -1), w_out, xs.reshape(bs, d), ln_mix_g[layer], ln_mix_b[layer])
        fargs = (ffn_w_in[layer], ffn_conv_w[layer], ffn_conv_b[layer], ffn_w_out[layer],
                 ln_ffn_g[layer], ln_ffn_b[layer])
        xp3, hp = _ffn_seq(xp2, *fargs)
        xs3, hs = _ffn_step(xs2, state_ffn_conv[:, layer], *fargs)
        xp, xs = xp3.reshape(1, lp, d), xs3.reshape(bs, 1, d)
        ffn_p.append(hp[None]); ffn_s.append(hs)

    def stk(lst):
        return jnp.stack(lst, axis=1)

    return (xp, xs, stk(cmp_p), stk(cmp_s), stk(slc_p), stk(slc_s), stk(win_p), stk(win_s),
            stk(gconv_p), stk(gconv_s), stk(gstate_p), stk(gstate_s), stk(dil_p), stk(dil_s),
            stk(ffn_p), stk(ffn_s))
```

```python
import functools

import jax
import jax.numpy as jnp
from jax import lax
from jax.experimental import pallas as pl
from jax.experimental.pallas import tpu as pltpu
import numpy as np

DEPTH = 2
PAGE_SIZE = 128
HEAD_DIM = 64
ROPE_THETA = 10000.0
NSA_HEADS = 8
NSA_KV_HEADS = 2
NSA_GROUP = NSA_HEADS // NSA_KV_HEADS
CMP_BLOCK = 32
SEL_BLOCK = 64
NSA_TOPN = 16
NSA_WINDOW = 512
NSA_QBLOCK = 128
NSA_FORCE = 1.0e4
GDN_HEADS = 8
GDN_CONV = 4
GDN_CHUNK = 64
DIL_HEADS = 16
DIL_GROUPS = ((128, 1), (512, 4), (2048, 16))
DIL_SPAN = 128
DIL_BLOCK = 128
DIL_MAX_WINDOW = 2048
D_FF = 2816
FFN_CONV = 3
DEEPNORM_ALPHA = (2.0 * DEPTH) ** 0.25
LN_EPS = 1e-5
NORM_EPS = 1e-6
NSA_Q_W = NSA_HEADS * HEAD_DIM
NSA_KV_W = NSA_KV_HEADS * HEAD_DIM
GDN_W = GDN_HEADS * HEAD_DIM
DIL_W = DIL_HEADS * HEAD_DIM

LANES = 128
VMEM_LIMIT_BYTES = 56 * 1024 * 1024


def _layer_norm_rows(r, g, b):
    mu = jnp.mean(r, axis=-1, keepdims=True)
    d = r - mu
    var = jnp.mean(d * d, axis=-1, keepdims=True)
    return d * lax.rsqrt(var + LN_EPS) * g + b


def _mm_kernel(x_ref, w_ref, o_ref):
    o_ref[...] = jnp.dot(x_ref[...].astype(jnp.bfloat16), w_ref[...], preferred_element_type=jnp.float32)


def _mm_ln_kernel(x_ref, w_ref, res_ref, g_ref, b_ref, o_ref):
    acc = jnp.dot(x_ref[...].astype(jnp.bfloat16), w_ref[...], preferred_element_type=jnp.float32)
    o_ref[...] = _layer_norm_rows(DEEPNORM_ALPHA * res_ref[...] + acc, g_ref[...], b_ref[...])


def _row_tile(m):
    return 512 if m % 512 == 0 else m


def _matmul(x, w):
    m, k = x.shape
    n = w.shape[1]
    tm = _row_tile(m)
    tn = n
    return pl.pallas_call(
        _mm_kernel,
        grid=(m // tm, n // tn),
        in_specs=[pl.BlockSpec((tm, k), lambda i, j: (i, 0)),
                  pl.BlockSpec((k, tn), lambda i, j: (0, j))],
        out_specs=pl.BlockSpec((tm, tn), lambda i, j: (i, j)),
        out_shape=jax.ShapeDtypeStruct((m, n), jnp.float32),
        compiler_params=pltpu.CompilerParams(dimension_semantics=("parallel", "arbitrary"),
                                             vmem_limit_bytes=VMEM_LIMIT_BYTES),
        name="matmul",
    )(x, w.astype(jnp.bfloat16))


def _matmul_ln(x, w, res, g, b):
    m, k = x.shape
    n = w.shape[1]
    tm = _row_tile(m)
    return pl.pallas_call(
        _mm_ln_kernel,
        grid=(m // tm,),
        in_specs=[pl.BlockSpec((tm, k), lambda i: (i, 0)),
                  pl.BlockSpec((k, n), lambda i: (0, 0)),
                  pl.BlockSpec((tm, n), lambda i: (i, 0)),
                  pl.BlockSpec((1, n), lambda i: (0, 0)),
                  pl.BlockSpec((1, n), lambda i: (0, 0))],
        out_specs=pl.BlockSpec((tm, n), lambda i: (i, 0)),
        out_shape=jax.ShapeDtypeStruct((m, n), jnp.float32),
        compiler_params=pltpu.CompilerParams(dimension_semantics=("arbitrary",),
                                             vmem_limit_bytes=VMEM_LIMIT_BYTES),
        name="matmul_ln",
    )(x, w.astype(jnp.bfloat16), res, g.reshape(1, n), b.reshape(1, n))


FFN_CHUNK = D_FF // 2
FFN_NCHUNK = D_FF // FFN_CHUNK


def _ffn_seq_kernel(x_ref, wi_ref, cw_ref, cb_ref, wo_ref, lg_ref, lb_ref, y_ref, hist_ref, carry_ref):
    f32, bf16 = jnp.float32, jnp.bfloat16
    i = pl.program_id(0)
    tm = x_ref.shape[0]
    x = x_ref[...]
    xb = x.astype(bf16)

    @pl.when(i == 0)
    def _():
        carry_ref[...] = jnp.zeros(carry_ref.shape, f32)

    row = lax.broadcasted_iota(jnp.int32, (8, FFN_CHUNK), 0)

    def conv_half(cols):
        u = jnp.dot(xb, wi_ref[:, cols], preferred_element_type=f32)
        prev = carry_ref[:, cols]
        p2, p1 = prev[6:7], prev[7:8]
        r1, r2 = pltpu.roll(u, 1, 0), pltpu.roll(u, 2, 0)
        u1 = jnp.concatenate([jnp.where(row == 0, p1, r1[:8]), r1[8:]], axis=0)
        u2 = jnp.concatenate([jnp.where(row == 0, p2, jnp.where(row == 1, p1, r2[:8])), r2[8:]], axis=0)
        carry_ref[:, cols] = u[tm - 8:]
        hist_ref[:, cols] = u[tm - 8:]
        cw = cw_ref[:, cols]
        return cw[0:1] * u2 + cw[1:2] * u1 + cw[2:3] * u + cb_ref[:, cols]

    acc = None
    for j in range(FFN_NCHUNK):
        a = conv_half(slice(j * FFN_CHUNK, (j + 1) * FFN_CHUNK))
        g = conv_half(slice(D_FF + j * FFN_CHUNK, D_FF + (j + 1) * FFN_CHUNK))
        h = (a * jax.nn.sigmoid(a) * g).astype(bf16)
        part = jnp.dot(h, wo_ref[j * FFN_CHUNK:(j + 1) * FFN_CHUNK, :], preferred_element_type=f32)
        acc = part if acc is None else acc + part
    y_ref[...] = _layer_norm_rows(DEEPNORM_ALPHA * x + acc, lg_ref[...], lb_ref[...])


def _ffn_seq(x, w_in, conv_w, conv_b, w_out, ln_g, ln_b):
    l, d = x.shape
    tm = _row_tile(l)
    cw8 = jnp.zeros((8, 2 * D_FF), jnp.float32).at[:FFN_CONV].set(conv_w)
    cb = conv_b.reshape(1, 2 * D_FF)
    const = lambda a: pl.BlockSpec(a.shape, lambda i: (0,) * a.ndim, pipeline_mode=pl.Buffered(1))
    wi, wo = w_in.astype(jnp.bfloat16), w_out.astype(jnp.bfloat16)
    g2, b2 = ln_g.reshape(1, d), ln_b.reshape(1, d)
    y, hist = pl.pallas_call(
        _ffn_seq_kernel,
        grid=(l // tm,),
        in_specs=[pl.BlockSpec((tm, d), lambda i: (i, 0)), const(wi), const(cw8), const(cb), const(wo),
                  const(g2), const(b2)],
        out_specs=[pl.BlockSpec((tm, d), lambda i: (i, 0)), pl.BlockSpec((8, 2 * D_FF), lambda i: (i, 0))],
        out_shape=[jax.ShapeDtypeStruct((l, d), jnp.float32),
                   jax.ShapeDtypeStruct((l // tm * 8, 2 * D_FF), jnp.float32)],
        scratch_shapes=[pltpu.VMEM((8, 2 * D_FF), jnp.float32)],
        compiler_params=pltpu.CompilerParams(dimension_semantics=("arbitrary",), vmem_limit_bytes=VMEM_LIMIT_BYTES),
        name="ffn_seq",
    )(x, wi, cw8, cb, wo, g2, b2)
    return y, hist[-(FFN_CONV - 1):]


def _ffn_step_kernel(x_ref, h_ref, wa_ref, wg_ref, cwa_ref, cwg_ref, cba_ref, cbg_ref, wo_ref, lg_ref, lb_ref,
                     y_ref, ua_ref, ug_ref, acc_ref):
    j = pl.program_id(0)
    x = x_ref[...]
    xb = x.astype(jnp.bfloat16)
    ua = jnp.dot(xb, wa_ref[...], preferred_element_type=jnp.float32)
    ug = jnp.dot(xb, wg_ref[...], preferred_element_type=jnp.float32)
    ua_ref[...] = ua
    ug_ref[...] = ug
    cwa, cwg = cwa_ref[...], cwg_ref[...]
    a = cwa[0:1] * h_ref[0, 0] + cwa[1:2] * h_ref[1, 0] + cwa[2:3] * ua + cba_ref[...]
    g = cwg[0:1] * h_ref[0, 1] + cwg[1:2] * h_ref[1, 1] + cwg[2:3] * ug + cbg_ref[...]
    h = (a * jax.nn.sigmoid(a) * g).astype(jnp.bfloat16)
    part = jnp.dot(h, wo_ref[...], preferred_element_type=jnp.float32)

    @pl.when(j == 0)
    def _():
        acc_ref[...] = part

    @pl.when(j > 0)
    def _():
        acc_ref[...] += part

    @pl.when(j == pl.num_programs(0) - 1)
    def _():
        y_ref[...] = _layer_norm_rows(DEEPNORM_ALPHA * x + acc_ref[...], lg_ref[...], lb_ref[...])


def _ffn_step(x, hist, w_in, conv_w, conv_b, w_out, ln_g, ln_b):
    b, d = x.shape
    c, nc = FFN_CHUNK, FFN_NCHUNK
    w_in = w_in.astype(jnp.bfloat16)
    cw8 = jnp.zeros((8, 2 * D_FF), jnp.float32).at[:FFN_CONV].set(conv_w)
    cb = conv_b.reshape(1, 2 * D_FF)
    h4 = jnp.transpose(hist, (1, 0, 2)).reshape(2, b, 2, D_FF).transpose(0, 2, 1, 3)
    y, ua, ug = pl.pallas_call(
        _ffn_step_kernel,
        grid=(nc,),
        in_specs=[pl.BlockSpec((b, d), lambda j: (0, 0)),
                  pl.BlockSpec((2, 2, b, c), lambda j: (0, 0, 0, j)),
                  pl.BlockSpec((d, c), lambda j: (0, j)),
                  pl.BlockSpec((d, c), lambda j: (0, j + nc)),
                  pl.BlockSpec((8, c), lambda j: (0, j)),
                  pl.BlockSpec((8, c), lambda j: (0, j + nc)),
                  pl.BlockSpec((1, c), lambda j: (0, j)),
                  pl.BlockSpec((1, c), lambda j: (0, j + nc)),
                  pl.BlockSpec((c, d), lambda j: (j, 0)),
                  pl.BlockSpec((1, d), lambda j: (0, 0)),
                  pl.BlockSpec((1, d), lambda j: (0, 0))],
        out_specs=[pl.BlockSpec((b, d), lambda j: (0, 0)),
                   pl.BlockSpec((b, c), lambda j: (0, j)),
                   pl.BlockSpec((b, c), lambda j: (0, j))],
        out_shape=[jax.ShapeDtypeStruct((b, d), jnp.float32),
                   jax.ShapeDtypeStruct((b, D_FF), jnp.float32),
                   jax.ShapeDtypeStruct((b, D_FF), jnp.float32)],
        scratch_shapes=[pltpu.VMEM((b, d), jnp.float32)],
        compiler_params=pltpu.CompilerParams(dimension_semantics=("arbitrary",),
                                             vmem_limit_bytes=VMEM_LIMIT_BYTES),
        name="ffn_step",
    )(x, h4, w_in, w_in, cw8, cw8, cb, cb, w_out.astype(jnp.bfloat16), ln_g.reshape(1, d), ln_b.reshape(1, d))
    u = jnp.concatenate([ua, ug], axis=-1)
    return y, jnp.concatenate([hist[:, 1:], u[:, None]], axis=1)


NEG = -1e30
NSA_KT = 512
NSA_COLS = NSA_HEADS * NSA_QBLOCK
NSA_WSPAN = NSA_WINDOW + NSA_QBLOCK
LOG2E = 1.4426950408889634
NSA_VROWS = HEAD_DIM + 8


def _lane_tile(x, n):
    return jnp.concatenate([x] * n, axis=1)


def _nsa_prompt_kernel(q_ref, sm_ref, ck_ref, cvt_ref, ks_ref, vst_ref, kw_ref, vwt_ref, hot_ref, o_ref,
                       selb_ref, m_ref, acc_ref, *, ns):
    f32, bf16 = jnp.float32, jnp.bfloat16
    qb = NSA_QBLOCK
    i = pl.program_id(0)
    s0 = i * qb
    half = NSA_COLS // 2

    qt = (q_ref[...] * (HEAD_DIM ** -0.5 * LOG2E)).T
    zero = jnp.zeros((HEAD_DIM, qb), f32)
    top = jnp.concatenate([qt[h * HEAD_DIM:(h + 1) * HEAD_DIM] for h in range(NSA_GROUP)] + [zero] * NSA_GROUP, axis=1)
    bot = jnp.concatenate([zero] * NSA_GROUP + [qt[h * HEAD_DIM:(h + 1) * HEAD_DIM]
                                                for h in range(NSA_GROUP, NSA_HEADS)], axis=1)
    qbd = jnp.concatenate([top, bot], axis=0).astype(bf16)

    def pv(vt, p):
        pb = p.astype(bf16)
        rows = vt.shape[0] // NSA_KV_HEADS
        return [jnp.dot(vt[g * rows:(g + 1) * rows], pb[:, g * half:(g + 1) * half],
                        preferred_element_type=f32) for g in range(NSA_KV_HEADS)]

    nc = 2 * ns
    r = lax.broadcasted_iota(jnp.int32, (nc, qb), 0)
    lane = lax.broadcasted_iota(jnp.int32, (nc, qb), 1)
    cidx = jnp.where(r < ns, 2 * r, 2 * (r - ns) + 1)
    cbias = jnp.where((cidx + 1) * CMP_BLOCK - 1 <= s0 + lane, 0.0, NEG)
    sc = jnp.dot(ck_ref[...], qbd, preferred_element_type=f32) + _lane_tile(cbias, NSA_HEADS)
    m = jnp.max(sc, axis=0, keepdims=True)
    p = jnp.exp2(sc - m)
    pn = p * jnp.where(m > 0.5 * NEG, 1.0 / jnp.sum(p, axis=0, keepdims=True), 0.0)
    o_cmp = pv(cvt_ref[...], pn)

    blk = lax.broadcasted_iota(jnp.int32, (ns, qb), 0)
    qpos = s0 + lax.broadcasted_iota(jnp.int32, (ns, qb), 1)
    cur = qpos // SEL_BLOCK
    forced = (blk == 0) | (blk == cur) | (blk == cur - 1)
    for g in range(NSA_KV_HEADS):
        imp = pn[:, g * half:g * half + qb]
        for h in range(1, NSA_GROUP):
            imp = imp + pn[:, g * half + h * qb:g * half + (h + 1) * qb]
        imp = imp[:ns] + imp[ns:]
        val = jnp.where(blk > cur, -1.0, jnp.where(forced, -jnp.inf, imp))
        bias = jnp.where(forced & (blk <= cur), 0.0, NEG)
        for _ in range(NSA_TOPN - 3):
            top = jnp.max(val, axis=0, keepdims=True)
            pick = jnp.min(jnp.where(val == top, blk, ns), axis=0, keepdims=True)
            hit = blk == pick
            bias = jnp.where(hit, 0.0, bias)
            val = jnp.where(hit, -jnp.inf, val)
        selb_ref[g] = bias

    m_ref[...] = jnp.full(m_ref.shape, NEG, f32)
    acc_ref[...] = jnp.zeros(acc_ref.shape, f32)
    per_tile = NSA_KT // SEL_BLOCK
    zpad = jnp.zeros((LANES - 16, NSA_COLS), bf16)

    def slc_tile(kt, causal):
        k0 = pl.multiple_of(kt * NSA_KT, NSA_KT)
        b0 = pl.multiple_of(kt * per_tile, per_tile)
        brow = jnp.concatenate([selb_ref[g, pl.ds(b0, per_tile), :] for g in range(NSA_KV_HEADS)
                                for _ in range(NSA_GROUP)], axis=1)
        brow = jnp.concatenate([brow, jnp.zeros((16 - per_tile, NSA_COLS), f32)], axis=0).astype(bf16)
        q_aug = jnp.concatenate([qbd, brow, zpad], axis=0)
        k_aug = jnp.concatenate([ks_ref[pl.ds(k0, NSA_KT), :], hot_ref[...]], axis=1)
        s = jnp.dot(k_aug, q_aug, preferred_element_type=f32)
        if causal:
            kpos = k0 + lax.broadcasted_iota(jnp.int32, (NSA_KT, qb), 0)
            qq = s0 + lax.broadcasted_iota(jnp.int32, (NSA_KT, qb), 1)
            s = s + _lane_tile(jnp.where(kpos <= qq, 0.0, NEG), NSA_HEADS)
        m_old = m_ref[...]
        m_new = jnp.maximum(m_old, jnp.max(s, axis=0, keepdims=True))
        alpha = jnp.exp2(m_old - m_new)
        p = jnp.exp2(s - m_new)
        m_ref[...] = m_new
        upd = pv(vst_ref[:, pl.ds(k0, NSA_KT)], p)
        for g in range(NSA_KV_HEADS):
            acc_ref[g] = acc_ref[g] * alpha[:, g * half:(g + 1) * half] + upd[g]

    kd = s0 // NSA_KT

    def body(j, carry):
        slc_tile(2 * j, False)
        slc_tile(2 * j + 1, False)
        return carry

    lax.fori_loop(0, kd // 2, body, 0)

    @pl.when(kd % 2 == 1)
    def _():
        slc_tile(kd - 1, False)

    slc_tile(kd, True)
    inv_slc = [1.0 / acc_ref[g, HEAD_DIM:HEAD_DIM + 1, :] for g in range(NSA_KV_HEADS)]

    w0 = pl.multiple_of(s0, qb)
    sw = jnp.dot(kw_ref[pl.ds(w0, NSA_WSPAN), :], qbd, preferred_element_type=f32)
    rr = lax.broadcasted_iota(jnp.int32, (NSA_WSPAN, qb), 0)
    qi = lax.broadcasted_iota(jnp.int32, (NSA_WSPAN, qb), 1)
    ok = (rr >= qi) & (rr <= qi + NSA_WINDOW) & (rr + s0 >= NSA_WINDOW)
    sw = sw + _lane_tile(jnp.where(ok, 0.0, NEG), NSA_HEADS)
    pw = jnp.exp2(sw - jnp.max(sw, axis=0, keepdims=True))
    o_win = pv(vwt_ref[:, pl.ds(w0, NSA_WSPAN)], pw)
    inv_win = [1.0 / o[HEAD_DIM:HEAD_DIM + 1] for o in o_win]

    gt = jax.nn.sigmoid(sm_ref[...].T)
    outs = []
    for h in range(NSA_HEADS):
        g, hg = divmod(h, NSA_GROUP)
        c0, c1 = hg * qb, (hg + 1) * qb
        g_cmp = gt[h:h + 1]
        g_slc = gt[NSA_HEADS + h:NSA_HEADS + h + 1] * inv_slc[g][:, c0:c1]
        g_win = gt[2 * NSA_HEADS + h:2 * NSA_HEADS + h + 1] * inv_win[g][:, c0:c1]
        outs.append(o_cmp[g][:, c0:c1] * g_cmp + acc_ref[g, :HEAD_DIM, c0:c1] * g_slc
                    + o_win[g][:HEAD_DIM, c0:c1] * g_win)
    o_ref[...] = jnp.concatenate(outs, axis=0).T


def _nsa_prompt_attention(qr, small, ck, cv, ksr, vs, kwr, vw):
    l = qr.shape[0]
    ns = l // SEL_BLOCK
    assert ns >= NSA_TOPN and l % NSA_KT == 0
    nc = 2 * ns
    bf16 = jnp.bfloat16
    perm =jnp.concatenate([jnp.arange(0, nc, 2), jnp.arange(1, nc, 2)])
    ckp = ck[perm].astype(bf16)
    cvt = cv[perm].T.astype(bf16)
    pad = jnp.zeros((NSA_WINDOW, NSA_KV_W), bf16)
    kwp = jnp.concatenate([pad, kwr.astype(bf16)], axis=0)
    def with_ones(vt):
        n = vt.shape[1]
        extra = jnp.concatenate([jnp.ones((1, n), bf16), jnp.zeros((NSA_VROWS - HEAD_DIM - 1, n), bf16)], axis=0)
        return jnp.concatenate([x for g in range(NSA_KV_HEADS) for x in (vt[g * HEAD_DIM:(g + 1) * HEAD_DIM], extra)], axis=0)

    vwt = with_ones(jnp.concatenate([pad, vw.astype(bf16)], axis=0).T)
    hot = (jnp.arange(NSA_KT)[:, None] // SEL_BLOCK == jnp.arange(LANES)[None, :]).astype(bf16)
    full = lambda a: pl.BlockSpec(a.shape, lambda i: (0,) * a.ndim)
    args = (qr, small, ckp, cvt, ksr.astype(bf16), with_ones(vs.T.astype(bf16)), kwp, vwt, hot)
    return pl.pallas_call(
        functools.partial(_nsa_prompt_kernel, ns=ns),
        grid=(l // NSA_QBLOCK,),
        in_specs=[pl.BlockSpec((NSA_QBLOCK, NSA_Q_W), lambda i: (i, 0)),
                  pl.BlockSpec((NSA_QBLOCK, LANES), lambda i: (i, 0))] + [full(a) for a in args[2:]],
        out_specs=pl.BlockSpec((NSA_QBLOCK, NSA_Q_W), lambda i: (i, 0)),
        out_shape=jax.ShapeDtypeStruct((l, NSA_Q_W), jnp.float32),
        scratch_shapes=[pltpu.VMEM((NSA_KV_HEADS, ns, NSA_QBLOCK), jnp.float32),
                        pltpu.VMEM((1, NSA_COLS), jnp.float32),
                        pltpu.VMEM((NSA_KV_HEADS, NSA_VROWS, NSA_COLS // 2), jnp.float32)],
        compiler_params=pltpu.CompilerParams(dimension_semantics=("arbitrary",),
                                             vmem_limit_bytes=VMEM_LIMIT_BYTES),
        name="nsa_prompt",
    )(*args)


CMP_HIDDEN = 2 * HEAD_DIM
CMP_PER_PAGE = PAGE_SIZE // CMP_BLOCK
CMP_PAGES_PER_STEP = 64
STEP_PAGES = 64
STEP_ROWS = 16


def _rope_tables(pos, width):
    half = HEAD_DIM // 2
    inv_freq = ROPE_THETA ** (-2.0 * jnp.arange(half, dtype=jnp.float32) / HEAD_DIM)
    ang = pos.astype(jnp.float32)[:, None] * inv_freq[None, :]
    cos, sin = jnp.cos(ang), jnp.sin(ang)
    reps = width // HEAD_DIM
    return (jnp.tile(jnp.concatenate([cos, cos], axis=1), (1, reps)),
            jnp.tile(jnp.concatenate([-sin, sin], axis=1), (1, reps)))


def _rope_lanes(x, cos, sin_signed):
    n = x.shape[-1]
    lane = lax.broadcasted_iota(jnp.int32, x.shape, x.ndim - 1)
    first = (lane % HEAD_DIM) < HEAD_DIM // 2
    partner = jnp.where(first, pltpu.roll(x, n - HEAD_DIM // 2, x.ndim - 1), pltpu.roll(x, HEAD_DIM // 2, x.ndim - 1))
    return x * cos + partner * sin_signed


def _nt_dot(a, b):
    return lax.dot_general(a, b, (((1,), (1,)), ((), ())), preferred_element_type=jnp.float32)


CMP_STEP_LANES = CMP_PAGES_PER_STEP * CMP_PER_PAGE
CMP_FEATURE_GROUP = 16


def _cmp_lane_blocks(n_pages):
    lane = np.arange(n_pages * CMP_PER_PAGE)
    step, rem = lane // CMP_STEP_LANES, lane % CMP_STEP_LANES
    j, pl_ = rem // CMP_PAGES_PER_STEP, rem % CMP_PAGES_PER_STEP
    return (step * CMP_PAGES_PER_STEP + pl_) * CMP_PER_PAGE + j


def _cmp_step_kernel(pt_ref, *refs):
    f32 = jnp.float32
    npg = CMP_PAGES_PER_STEP
    pages = refs[:npg]
    pe_ref, w1_ref, b1_ref, w2_ref, cos_ref, sin_ref, ck_ref, cv_ref, slab_ref = refs[npg:]
    outs = (ck_ref, cv_ref)
    for k, r in enumerate(pages):
        for s in range(2 * NSA_KV_HEADS):
            slab_ref[s, k * HEAD_DIM:(k + 1) * HEAD_DIM, :] = r[0, s]
    for kv in range(2):
        h = jnp.zeros((NSA_KV_HEADS * npg, CMP_PER_PAGE * CMP_HIDDEN), f32)
        for dg in range(HEAD_DIM // CMP_FEATURE_GROUP):
            x = jnp.concatenate(
                [jnp.concatenate([slab_ref[2 * kv + g, pl.ds(d, npg, stride=HEAD_DIM), :]
                                  for g in range(NSA_KV_HEADS)], axis=0) + pe_ref[kv, d]
                 for d in range(dg * CMP_FEATURE_GROUP, (dg + 1) * CMP_FEATURE_GROUP)], axis=1)
            h = h + jnp.dot(x.astype(jnp.bfloat16), w1_ref[kv, dg], preferred_element_type=f32)
        act = jax.nn.gelu(h + b1_ref[kv]).astype(jnp.bfloat16)
        ct = _nt_dot(w2_ref[kv], act)
        tile = jnp.concatenate(
            [jnp.concatenate([ct[j * HEAD_DIM:(j + 1) * HEAD_DIM, g * npg:(g + 1) * npg] for j in range(CMP_PER_PAGE)],
                             axis=1) for g in range(NSA_KV_HEADS)], axis=0)
        if kv == 0:
            row = lax.broadcasted_iota(jnp.int32, tile.shape, 0)
            n = tile.shape[0]
            partner = jnp.where((row % HEAD_DIM) < HEAD_DIM // 2, pltpu.roll(tile, n - HEAD_DIM // 2, 0),
                                pltpu.roll(tile, HEAD_DIM // 2, 0))
            tile = tile * cos_ref[...] + partner * sin_ref[...]
        outs[kv][0] = tile


def _compress_weights(cw1, cb1, cw2, cpe):
    eye = jnp.eye(CMP_PER_PAGE, dtype=jnp.float32)
    w1r = cw1.reshape(2, CMP_BLOCK, HEAD_DIM, CMP_HIDDEN)
    w1 = jnp.einsum('ktdn,ja->kdjtan', w1r, eye).reshape(
        2, HEAD_DIM // CMP_FEATURE_GROUP, CMP_FEATURE_GROUP * PAGE_SIZE, CMP_PER_PAGE * CMP_HIDDEN)
    b1 = jnp.tile(cb1, (1, CMP_PER_PAGE))[:, None, :]
    w2 = jnp.einsum('knd,ja->kjdan', cw2, eye).reshape(2, CMP_PER_PAGE * HEAD_DIM, CMP_PER_PAGE * CMP_HIDDEN)
    pe = jnp.tile(cpe.transpose(0, 2, 1), (1, 1, CMP_PER_PAGE))[:, :, None, :]
    return w1.astype(jnp.bfloat16), b1, w2.astype(jnp.bfloat16), pe


def _nsa_sample_compress(pool, layer_idx, page_table, cw1, cb1, cw2, cpe):
    n_pool, nl = pool.shape[:2]
    db, n_pages = page_table.shape
    npg = CMP_PAGES_PER_STEP
    nchunk = n_pages // npg
    nc = n_pages * CMP_PER_PAGE
    view = jnp.transpose(pool, (0, 1, 3, 4, 5, 2)).reshape(n_pool * nl, 2 * NSA_KV_HEADS, HEAD_DIM, PAGE_SIZE)
    pt = (page_table * nl + layer_idx).reshape(-1).astype(jnp.int32)
    w1, b1, w2, pe = _compress_weights(cw1, cb1, cw2, cpe)
    pos = (jnp.asarray(_cmp_lane_blocks(n_pages)) + 1) * CMP_BLOCK - 1
    half = HEAD_DIM // 2
    inv_freq = ROPE_THETA ** (-2.0 * jnp.arange(half, dtype=jnp.float32) / HEAD_DIM)
    ang = inv_freq[:, None] * pos.astype(jnp.float32)[None, :]
    cos = jnp.tile(jnp.cos(ang), (2 * NSA_KV_HEADS, 1))
    sin = jnp.tile(jnp.concatenate([-jnp.sin(ang), jnp.sin(ang)], axis=0), (NSA_KV_HEADS, 1))

    def page_map(k):
        return lambda b, c, pt_ref: (pt_ref[b * n_pages + c * npg + k], 0, 0, 0)

    const = lambda a: pl.BlockSpec(a.shape, lambda b, c, pt_ref: (0,) * a.ndim, pipeline_mode=pl.Buffered(1))
    lanes_c = lambda: pl.BlockSpec((NSA_KV_W, CMP_STEP_LANES), lambda b, c, pt_ref: (0, c))
    grid_spec = pltpu.PrefetchScalarGridSpec(
        num_scalar_prefetch=1, grid=(db, nchunk),
        in_specs=[pl.BlockSpec((1, 2 * NSA_KV_HEADS, HEAD_DIM, PAGE_SIZE), page_map(k)) for k in range(npg)]
        + [const(pe), const(w1), const(b1), const(w2), lanes_c(), lanes_c()],
        out_specs=[pl.BlockSpec((1, NSA_KV_W, CMP_STEP_LANES), lambda b, c, pt_ref: (b, 0, c))] * 2,
        scratch_shapes=[pltpu.VMEM((2 * NSA_KV_HEADS, npg * HEAD_DIM, PAGE_SIZE), jnp.float32)])
    return pl.pallas_call(
        _cmp_step_kernel, grid_spec=grid_spec,
        out_shape=[jax.ShapeDtypeStruct((db, NSA_KV_W, nc), jnp.float32)] * 2,
        compiler_params=pltpu.CompilerParams(dimension_semantics=("arbitrary", "arbitrary"),
                                             vmem_limit_bytes=VMEM_LIMIT_BYTES),
        name="nsa_sample_compress",
    )(pt, *([view] * npg), pe, w1, b1, w2, cos, sin)


def _nsa_step_kernel(pt_ref, *refs, topn):
    f32, bf16 = jnp.float32, jnp.bfloat16
    npg = STEP_PAGES
    q_ref, ck_ref, cv_ref = refs[:3]
    pages = refs[3:3 + npg]
    (win_ref, newr_ref, newt_ref, gate_ref, blk_ref, exp_ref, o_ref, wout_ref,
     selt_ref, m_ref, l_ref, acc_ref, side_ref) = refs[3 + npg:]
    cc = pl.program_id(1)
    q16 = q_ref[0]
    qb = q16.astype(bf16)
    row16 = lax.broadcasted_iota(jnp.int32, (STEP_ROWS, 1), 0)

    def new_key_scores(krow):
        return jnp.sum(q16 * krow, axis=1, keepdims=True)

    @pl.when(cc == 0)
    def _():
        nc = ck_ref.shape[2]
        s = jnp.dot(qb, ck_ref[0].astype(bf16), preferred_element_type=f32)
        p = jnp.exp(s - jnp.max(s, axis=1, keepdims=True))
        pn = p / jnp.sum(p, axis=1, keepdims=True)
        o_cmp = _nt_dot(pn.astype(bf16), cv_ref[0].astype(bf16))

        rowp = lax.broadcasted_iota(jnp.int32, pn.shape, 0)
        row8 = lax.broadcasted_iota(jnp.int32, (8, nc), 0)
        blk = blk_ref[...]
        val = jnp.full((8, nc), -jnp.inf, f32)
        for g in range(NSA_KV_HEADS):
            ig = jnp.sum(jnp.where((rowp >= g * NSA_GROUP) & (rowp < (g + 1) * NSA_GROUP), pn, 0.0),
                         axis=0, keepdims=True)
            ig = ig + pltpu.roll(ig, nc - CMP_PAGES_PER_STEP, 1)
            vg = jnp.where(blk[:1] < 0, -jnp.inf, jnp.where(blk[1:2] > 0, NSA_FORCE, ig))
            val = jnp.where(row8 == g, vg, val)
        sblk = jnp.where(blk[:1] < 0, nc, blk[:1])
        sel = jnp.zeros((8, nc), f32)
        for _ in range(topn):
            top = jnp.max(val, axis=1, keepdims=True)
            pick = jnp.min(jnp.where(val == top, sblk, nc), axis=1, keepdims=True)
            hit = sblk == pick
            sel = jnp.where(hit, 1.0, sel)
            val = jnp.where(hit, -jnp.inf, val)
        selh = jnp.where(row16 < NSA_GROUP, sel[0:1], jnp.where(row16 < NSA_HEADS, sel[1:2], 0.0))
        wsel = selt_ref.shape[2]
        for j in range(selt_ref.shape[0]):
            selt_ref[j] = selh[:, j * wsel:(j + 1) * wsel]

        m_ref[...] = new_key_scores(newr_ref[0, 0:1, :])
        l_ref[...] = jnp.ones(l_ref.shape, f32)
        acc_ref[...] = jnp.broadcast_to(newr_ref[0, 1:2, :], acc_ref.shape)

        sw = jnp.dot(qb, win_ref[0, 0].astype(bf16), preferred_element_type=f32)
        sn = new_key_scores(newr_ref[0, 2:3, :])
        mw = jnp.maximum(jnp.max(sw, axis=1, keepdims=True), sn)
        pw, pnw = jnp.exp(sw - mw), jnp.exp(sn - mw)
        lw = jnp.sum(pw, axis=1, keepdims=True) + pnw
        o_win = (_nt_dot(pw.astype(bf16), win_ref[0, 1].astype(bf16)) + pnw * newr_ref[0, 3:4, :]) / lw
        gt = jax.nn.sigmoid(gate_ref[0])
        side_ref[...] = gt[:, 0:1] * o_cmp + gt[:, 2:3] * o_win
        wl = win_ref.shape[3]
        lane = lax.broadcasted_iota(jnp.int32, (NSA_KV_W, wl), 1)
        for kv in range(2):
            wout_ref[0, kv] = jnp.where(lane == wl - 1, newt_ref[0, :, 2 + kv:3 + kv],
                                        pltpu.roll(win_ref[0, kv], wl - 1, 1))

    kt = jnp.concatenate([r[0, 0] for r in pages], axis=1)
    vt = jnp.concatenate([r[0, 1] for r in pages], axis=1)
    s = jnp.dot(qb, kt.astype(bf16), preferred_element_type=f32)
    picked = jnp.dot(selt_ref[cc].astype(bf16), exp_ref[...], preferred_element_type=f32)
    s = s + (picked - 1.0) * (-NEG)
    m_old = m_ref[...]
    m_new = jnp.maximum(m_old, jnp.max(s, axis=1, keepdims=True))
    alpha = jnp.exp(m_old - m_new)
    p = jnp.exp(s - m_new)
    m_ref[...] = m_new
    l_ref[...] = l_ref[...] * alpha + jnp.sum(p, axis=1, keepdims=True)
    acc_ref[...] = acc_ref[...] * alpha + _nt_dot(p.astype(bf16), vt.astype(bf16))

    @pl.when(cc == pl.num_programs(1) - 1)
    def _():
        gt = jax.nn.sigmoid(gate_ref[0])
        o_ref[0] = side_ref[...] + gt[:, 1:2] * acc_ref[...] / l_ref[...]


def _nsa_sample_attention(qr, ckt, cvt, slc_pool, layer_idx, page_table, win_buf, newrows, gate):
    db, n_pages = page_table.shape
    n_pool, nl = slc_pool.shape[:2]
    wlen = win_buf.shape[1]
    nc = ckt.shape[2]
    past = n_pages * PAGE_SIZE
    cur = past // SEL_BLOCK
    npg = STEP_PAGES
    nchunk = n_pages // npg
    keys = npg * PAGE_SIZE
    wsel = npg * CMP_PER_PAGE
    f32, bf16 = jnp.float32, jnp.bfloat16
    view = jnp.transpose(slc_pool, (0, 1, 3, 4, 5, 2)).reshape(n_pool * nl, 2, NSA_KV_W, PAGE_SIZE)
    wint = jnp.transpose(win_buf, (0, 2, 3, 4, 1)).reshape(db, 2, NSA_KV_W, wlen)
    pt = (page_table * nl + layer_idx).reshape(-1).astype(jnp.int32)
    hmask = (jnp.arange(NSA_HEADS)[:, None] // NSA_GROUP == jnp.arange(NSA_KV_HEADS)[None, :]).astype(f32)
    q16 = (qr * HEAD_DIM ** -0.5)[:, :, None, :] * hmask[None, :, :, None]
    q16 = jnp.pad(q16.reshape(db, NSA_HEADS, NSA_KV_W), ((0, 0), (0, STEP_ROWS - NSA_HEADS), (0, 0)))
    newr = jnp.pad(newrows, ((0, 0), (0, 8 - newrows.shape[1]), (0, 0)))
    newt = jnp.pad(newrows.transpose(0, 2, 1), ((0, 0), (0, 0), (0, LANES - newrows.shape[1])))
    g16 = jnp.pad(gate.reshape(db, 3, NSA_HEADS).transpose(0, 2, 1),
                  ((0, 0), (0, STEP_ROWS - NSA_HEADS), (0, LANES - 3)))
    cblk = _cmp_lane_blocks(n_pages)
    jj = cblk % CMP_PER_PAGE
    sblk = np.where(jj % 2 == 0, cblk // 2, -1)
    forced = ((sblk == 0) | (sblk == cur - 1)).astype(np.int32)
    blk8 = np.zeros((8, nc), np.int32)
    blk8[0], blk8[1] = sblk, forced
    loc = np.arange(wsel)
    lstep, lrem = loc // CMP_STEP_LANES, loc % CMP_STEP_LANES
    lj, lpage = lrem // CMP_PAGES_PER_STEP, lstep * CMP_PAGES_PER_STEP + lrem % CMP_PAGES_PER_STEP
    kidx = np.arange(keys)
    expand = ((lj[:, None] % 2 == 0) & (kidx[None, :] // PAGE_SIZE == lpage[:, None])
              & ((kidx[None, :] % PAGE_SIZE) // SEL_BLOCK == lj[:, None] // 2)).astype(np.float32)
    topn = min(NSA_TOPN, cur + 1) - 1

    def page_map(k):
        return lambda b, c, pt_ref: (pt_ref[b * n_pages + c * npg + k], 0, 0, 0)

    per_b = lambda shp: pl.BlockSpec((1,) + shp, lambda b, c, pt_ref: (b,) + (0,) * len(shp))
    const = lambda a: pl.BlockSpec(a.shape, lambda b, c, pt_ref: (0,) * a.ndim)
    consts = (jnp.asarray(blk8), jnp.asarray(expand, bf16))
    grid_spec = pltpu.PrefetchScalarGridSpec(
        num_scalar_prefetch=1, grid=(db, nchunk),
        in_specs=[per_b((STEP_ROWS, NSA_KV_W)), per_b((NSA_KV_W, nc)), per_b((NSA_KV_W, nc))]
        + [pl.BlockSpec((1, 2, NSA_KV_W, PAGE_SIZE), page_map(k)) for k in range(npg)]
        + [per_b((2, NSA_KV_W, wlen)), per_b((8, NSA_KV_W)), per_b((NSA_KV_W, LANES)), per_b((STEP_ROWS, LANES))]
        + [const(a) for a in consts],
        out_specs=[per_b((STEP_ROWS, NSA_KV_W)), per_b((2, NSA_KV_W, wlen))],
        scratch_shapes=[pltpu.VMEM((nc // wsel, STEP_ROWS, wsel), f32),
                        pltpu.VMEM((STEP_ROWS, 1), f32), pltpu.VMEM((STEP_ROWS, 1), f32),
                        pltpu.VMEM((STEP_ROWS, NSA_KV_W), f32), pltpu.VMEM((STEP_ROWS, NSA_KV_W), f32)])
    o16, wout = pl.pallas_call(
        functools.partial(_nsa_step_kernel, topn=topn), grid_spec=grid_spec,
        out_shape=[jax.ShapeDtypeStruct((db, STEP_ROWS, NSA_KV_W), f32),
                   jax.ShapeDtypeStruct((db, 2, NSA_KV_W, wlen), f32)],
        compiler_params=pltpu.CompilerParams(dimension_semantics=("arbitrary", "arbitrary"),
                                             vmem_limit_bytes=VMEM_LIMIT_BYTES),
        name="nsa_sample_attention",
    )(pt, q16, ckt, cvt, *([view] * npg), wint, newr, newt, g16, *consts)
    o = o16[:, :NSA_HEADS].reshape(db, NSA_HEADS, NSA_KV_HEADS, HEAD_DIM)
    o = jnp.take_along_axis(o, (jnp.arange(NSA_HEADS) // NSA_GROUP)[None, :, None, None], axis=2)
    wout = jnp.transpose(wout.reshape(db, 2, NSA_KV_HEADS, HEAD_DIM, wlen), (0, 4, 1, 2, 3))
    return o.reshape(db, NSA_HEADS * HEAD_DIM), wout


DIL_ROW_CHUNK = 64


def _dil_step_kernel(q_ref, buf_ref, newt_ref, newr_ref, bias_ref, o_ref, out_ref, p_ref, pn_ref, den_ref):
    f32, bf16 = jnp.float32, jnp.bfloat16
    kv = pl.program_id(1)
    wlen = buf_ref.shape[3]
    nrow = buf_ref.shape[2]
    q16 = q_ref[0]

    @pl.when(kv == 0)
    def _():
        s = jnp.dot(q16.astype(bf16), buf_ref[0, 0].astype(bf16), preferred_element_type=f32)
        s_new = jnp.sum(q16 * newr_ref[0, 0:1, :], axis=1, keepdims=True)
        ms, es, ens, dens = [], [], [], []
        for g in range(len(DIL_GROUPS)):
            sg = s + bias_ref[g:g + 1, :]
            m = jnp.maximum(jnp.max(sg, axis=1, keepdims=True), s_new)
            e, en = jnp.exp(sg - m), jnp.exp(s_new - m)
            ms.append(m); es.append(e); ens.append(en)
            dens.append(jnp.sum(e, axis=1, keepdims=True) + en)
        m_all = functools.reduce(jnp.maximum, ms)
        ws = [jnp.exp(m - m_all) for m in ms]
        p_ref[...] = sum(w * e for w, e in zip(ws, es))
        pn_ref[...] = sum(w * en for w, en in zip(ws, ens))
        den_ref[...] = sum(w * d for w, d in zip(ws, dens))

    @pl.when(kv == 1)
    def _():
        r = _nt_dot(p_ref[...].astype(bf16), buf_ref[0, 0].astype(bf16))
        r = (r + pn_ref[...] * newr_ref[0, 1:2, :]) / den_ref[...]
        head = lax.broadcasted_iota(jnp.int32, r.shape, 1) // HEAD_DIM
        row = lax.broadcasted_iota(jnp.int32, r.shape, 0)
        o_ref[0] = jnp.broadcast_to(jnp.sum(jnp.where(head == row, r, 0.0), axis=0, keepdims=True), o_ref.shape[1:])

    lane = lax.broadcasted_iota(jnp.int32, (DIL_ROW_CHUNK, wlen), 1)
    for c in range(nrow // DIL_ROW_CHUNK):
        rs = slice(c * DIL_ROW_CHUNK, (c + 1) * DIL_ROW_CHUNK)
        col = jnp.where(kv == 0, newt_ref[0, rs, 0:1], newt_ref[0, rs, 1:2])
        out_ref[0, 0, rs, :] = jnp.where(lane == wlen - 1, col, pltpu.roll(buf_ref[0, 0, rs, :], wlen - 1, 1))


def _dil_sample_attention(qr, kr_new, v_new, buf):
    db, wlen = buf.shape[:2]
    f32 = jnp.float32
    buft = jnp.transpose(buf, (0, 2, 3, 4, 1)).reshape(db, 2, DIL_W, wlen)
    eye = jnp.eye(DIL_HEADS, dtype=f32)
    q16 = ((qr * HEAD_DIM ** -0.5)[:, :, None, :] * eye[None, :, :, None]).reshape(db, DIL_HEADS, DIL_W)
    newr = jnp.pad(jnp.stack([kr_new, v_new], axis=1), ((0, 0), (0, 6), (0, 0)))
    newt = jnp.pad(jnp.stack([kr_new, v_new], axis=2), ((0, 0), (0, 0), (0, LANES - 2)))
    back = wlen - jnp.arange(wlen)
    bias = jnp.stack([jnp.where((back % d == 0) & (back // d <= DIL_SPAN), 0.0, NEG) for _, d in DIL_GROUPS])
    bias = jnp.pad(bias, ((0, 8 - len(DIL_GROUPS)), (0, 0))).astype(f32)
    o, new_buf = pl.pallas_call(
        _dil_step_kernel,
        grid=(db, 2),
        in_specs=[pl.BlockSpec((1, DIL_HEADS, DIL_W), lambda b, k: (b, 0, 0)),
                  pl.BlockSpec((1, 1, DIL_W, wlen), lambda b, k: (b, k, 0, 0)),
                  pl.BlockSpec((1, DIL_W, LANES), lambda b, k: (b, 0, 0)),
                  pl.BlockSpec((1, 8, DIL_W), lambda b, k: (b, 0, 0)),
                  pl.BlockSpec((8, wlen), lambda b, k: (0, 0))],
        out_specs=[pl.BlockSpec((1, 8, DIL_W), lambda b, k: (b, 0, 0)),
                   pl.BlockSpec((1, 1, DIL_W, wlen), lambda b, k: (b, k, 0, 0))],
        out_shape=[jax.ShapeDtypeStruct((db, 8, DIL_W), f32),
                   jax.ShapeDtypeStruct((db, 2, DIL_W, wlen), f32)],
        scratch_shapes=[pltpu.VMEM((DIL_HEADS, wlen), f32), pltpu.VMEM((DIL_HEADS, 1), f32),
                        pltpu.VMEM((DIL_HEADS, 1), f32)],
        compiler_params=pltpu.CompilerParams(dimension_semantics=("arbitrary", "arbitrary"),
                                             vmem_limit_bytes=VMEM_LIMIT_BYTES),
        name="dil_sample",
    )(q16, buft, newt, newr, bias)
    new_buf = jnp.transpose(new_buf.reshape(db, 2, DIL_HEADS, HEAD_DIM, wlen), (0, 4, 1, 2, 3))
    return o[:, 0], new_buf


def _dil_band_kernel(q_ref, kp_ref, kc_ref, vp_ref, vc_ref, num_ref, st_ref):
    f32, bf16 = jnp.float32, jnp.bfloat16
    blk = DIL_BLOCK
    n = pl.program_id(0)
    i = lax.broadcasted_iota(jnp.int32, (blk, 2 * blk), 0)
    j = lax.broadcasted_iota(jnp.int32, (blk, 2 * blk), 1) - blk
    ok = (i - j >= 0) & (i - j <= DIL_SPAN) & (n * blk + j >= 0)
    bias = jnp.where(ok, 0.0, NEG)
    bias = jnp.concatenate([bias, bias], axis=0)
    lane = lax.broadcasted_iota(jnp.int32, (blk, LANES), 1)
    first = lane < HEAD_DIM
    stats = jnp.zeros((blk, LANES), f32)
    for p in range(DIL_HEADS // 2):
        cols = slice(p * LANES, (p + 1) * LANES)
        qp = q_ref[:, cols] * (HEAD_DIM ** -0.5)
        qst = jnp.concatenate([jnp.where(first, qp, 0.0), jnp.where(first, 0.0, qp)], axis=0).astype(bf16)
        kk = jnp.concatenate([kp_ref[:, cols], kc_ref[:, cols]], axis=0).astype(bf16)
        vv = jnp.concatenate([vp_ref[:, cols], vc_ref[:, cols]], axis=0).astype(bf16)
        s = _nt_dot(qst, kk) + bias
        m = jnp.max(s, axis=1, keepdims=True)
        e = jnp.exp(s - m)
        den = jnp.sum(e, axis=1, keepdims=True)
        nm = jnp.dot(e.astype(bf16), vv, preferred_element_type=f32)
        num_ref[:, cols] = jnp.where(first, nm[:blk], nm[blk:])
        for a in range(2):
            h = 2 * p + a
            stats = jnp.where(lane == h, m[a * blk:(a + 1) * blk], stats)
            stats = jnp.where(lane == DIL_HEADS + h, den[a * blk:(a + 1) * blk], stats)
    st_ref[...] = stats


GDN_PREP_ROWS = 512
GDN_TILE_CHUNKS = 4
GDN_PAIRS = GDN_HEADS // 2
GDN_A_LANE = 3 * NSA_HEADS
GDN_B_LANE = GDN_A_LANE + GDN_HEADS


def _hi_lo(x):
    hi = x.astype(jnp.bfloat16)
    return hi, (x - hi.astype(jnp.float32)).astype(jnp.bfloat16)


def _three_way(x):
    f32 = jnp.float32
    x1 = x.astype(jnp.bfloat16)
    r1 = x - x1.astype(f32)
    x2 = r1.astype(jnp.bfloat16)
    return x1, x2, (r1 - x2.astype(f32)).astype(jnp.bfloat16)


def _dot_select(x, sel):
    return sum(jnp.dot(piece, sel, preferred_element_type=jnp.float32) for piece in _three_way(x))


def _select_dot(sel, x):
    return sum(jnp.dot(sel, piece, preferred_element_type=jnp.float32) for piece in _three_way(x))


def _dot_hl(a, b):
    f32 = jnp.float32
    ah, al = _hi_lo(a)
    bh, bl = _hi_lo(b)
    return (jnp.dot(ah, bh, preferred_element_type=f32) + jnp.dot(ah, bl, preferred_element_type=f32)
            + jnp.dot(al, bh, preferred_element_type=f32))


def _gdn_prep_kernel(u_ref, sm_ref, cw_ref, prm_ref, ea_ref, eb_ref, eh_ref, q_ref, k_ref, v_ref, g_ref, b_ref,
                     carry_ref):
    f32 = jnp.float32
    i = pl.program_id(0)
    tl = u_ref.shape[0]

    @pl.when(i == 0)
    def _():
        carry_ref[...] = jnp.zeros(carry_ref.shape, f32)

    u = u_ref[...]
    prev = carry_ref[...]
    row = lax.broadcasted_iota(jnp.int32, u.shape, 0)

    def shifted(k):
        r = pltpu.roll(u, k, 0)
        for j in range(k):
            r = jnp.where(row == j, prev[8 - k + j:8 - k + j + 1], r)
        return r

    cw = cw_ref[...]
    c = cw[0:1] * shifted(3) + cw[1:2] * shifted(2) + cw[2:3] * shifted(1) + cw[3:4] * u
    carry_ref[...] = u[tl - 8:]
    c = c * jax.nn.sigmoid(c)
    eh = eh_ref[...]

    def l2n(x):
        return x * lax.rsqrt(_dot_select(x * x, eh) + NORM_EPS)

    q_ref[...] = l2n(c[:, :GDN_W]) * (HEAD_DIM ** -0.5)
    k_ref[...] = l2n(c[:, GDN_W:2 * GDN_W])
    v_ref[...] = c[:, 2 * GDN_W:]
    sm = sm_ref[...]
    x = sm + prm_ref[1:2]
    softplus = jnp.maximum(x, 0.0) + jnp.log(1.0 + jnp.exp(-jnp.abs(x)))
    g_ref[...] = _dot_select(-jnp.exp(prm_ref[0:1]) * softplus, ea_ref[...])
    b_ref[...] = _dot_select(jax.nn.sigmoid(sm), eb_ref[...])


def _gdn_chunk_kernel(q_ref, k_ref, v_ref, g_ref, b_ref, z_ref, nw_ref, lt_ref, eh_ref, o_ref, s_out_ref, s_ref):
    f32 = jnp.float32
    ch = GDN_CHUNK
    i = pl.program_id(0)

    @pl.when(i == 0)
    def _():
        s_ref[...] = jnp.zeros(s_ref.shape, f32)

    lane = lax.broadcasted_iota(jnp.int32, (ch, LANES), 1)
    first = lane < HEAD_DIM
    stack = lambda x: jnp.concatenate([jnp.where(first, x, 0.0), jnp.where(first, 0.0, x)], axis=0)
    r2 = lax.broadcasted_iota(jnp.int32, (2 * ch, 2 * ch), 0)
    c2 = lax.broadcasted_iota(jnp.int32, (2 * ch, 2 * ch), 1)
    same = (r2 // ch) == (c2 // ch)
    tri = same & (r2 % ch >= c2 % ch)
    strict = same & (r2 % ch > c2 % ch)
    eye = r2 == c2
    eye_f = jnp.where(eye, 1.0, 0.0)
    diag2 = lax.broadcasted_iota(jnp.int32, (ch, LANES), 0) == lane % HEAD_DIM
    lt = lt_ref[...]
    bf = lambda x: x.astype(jnp.bfloat16)
    dot = lambda a, b: jnp.dot(bf(a), bf(b), preferred_element_type=f32)

    blocks = [(c, p) for c in range(GDN_TILE_CHUNKS) for p in range(GDN_PAIRS)]
    ld = lambda ref, c, p: ref[c * ch:(c + 1) * ch, p * LANES:(p + 1) * LANES]
    gcs = [_select_dot(lt, ld(g_ref, c, p)) for c, p in blocks]
    amats, qks, rhs_u, rhs_w, qgs, kds, decs = [], [], [], [], [], [], []
    for (c, p), gc in zip(blocks, gcs):
        kk, qq, vv, bb = ld(k_ref, c, p), ld(q_ref, c, p), ld(v_ref, c, p), ld(b_ref, c, p)
        eg = jnp.exp(gc)
        g_end = gc[ch - 1:ch]
        kb = kk * bb
        col = jnp.concatenate([jnp.broadcast_to(gc[:, 0:1], (ch, LANES)),
                               jnp.broadcast_to(gc[:, HEAD_DIM:HEAD_DIM + 1], (ch, LANES))], axis=0)
        rowv = jnp.sum(jnp.where(diag2, gc, 0.0), axis=0, keepdims=True)
        gam = jnp.where(tri, jnp.exp(jnp.where(tri, col - rowv, 0.0)), 0.0)
        kst = stack(kk)
        amats.append(jnp.where(strict, _nt_dot(bf(stack(kb)), bf(kst)) * gam, 0.0))
        qks.append(jnp.where(tri, _nt_dot(bf(stack(qq)), bf(kst)) * gam, 0.0))
        rhs_u.append(stack(vv * bb))
        rhs_w.append(stack(kb * eg))
        qgs.append(stack(qq * eg))
        kds.append(stack(kk * jnp.exp(g_end - gc)))
        decs.append(jnp.sum(jnp.where(eye, jnp.exp(g_end), 0.0), axis=1, keepdims=True))
    xs = [eye_f - a for a in amats]
    pws = [_dot_hl(a, a) for a in amats]
    steps = GDN_CHUNK.bit_length() - 2
    for r in range(steps):
        xs = [x + _dot_hl(x, pw) for x, pw in zip(xs, pws)]
        if r < steps - 1:
            pws = [_dot_hl(pw, pw) for pw in pws]
    uus = [dot(x, u) for x, u in zip(xs, rhs_u)]
    wws = [dot(x, w) for x, w in zip(xs, rhs_w)]
    kdts = [kd.T for kd in kds]
    states = [s_ref[p] for p in range(GDN_PAIRS)]
    for c in range(GDN_TILE_CHUNKS):
        rs = slice(c * ch, (c + 1) * ch)
        outs = []
        for p in range(GDN_PAIRS):
            n = c * GDN_PAIRS + p
            s = states[p]
            v_new = uus[n] - dot(wws[n], s)
            o_st = dot(qgs[n], s) + dot(qks[n], v_new)
            states[p] = s * decs[n] + dot(kdts[n], v_new)
            outs.append(o_st[:ch] + o_st[ch:])
        o = jnp.concatenate(outs, axis=1)
        ms = _dot_select(o * o, eh_ref[...]) * (1.0 / HEAD_DIM)
        z = z_ref[rs, :]
        o_ref[rs, :] = o * lax.rsqrt(ms + NORM_EPS) * nw_ref[...] * (z * jax.nn.sigmoid(z))
    for p in range(GDN_PAIRS):
        s_ref[p] = states[p]

    @pl.when(i == pl.num_programs(0) - 1)
    def _():
        s_out_ref[...] = s_ref[...]


def _gdn_prompt(qkv, small, z, conv_w, a_log, dt_bias, norm_w):
    l = qkv.shape[0]
    f32, bf16 = jnp.float32, jnp.bfloat16
    w = GDN_W
    hh = jnp.arange(w) // HEAD_DIM
    expander = lambda base: (jnp.arange(LANES)[:, None] == base + hh[None, :]).astype(bf16)
    eh = (hh[:, None] == hh[None, :]).astype(bf16)
    cw8 = jnp.zeros((8, 3 * w), f32).at[:GDN_CONV].set(conv_w)
    prm = jnp.zeros((8, LANES), f32)
    prm = prm.at[0, GDN_A_LANE:GDN_A_LANE + GDN_HEADS].set(a_log).at[1, GDN_A_LANE:GDN_A_LANE + GDN_HEADS].set(dt_bias)
    tl = GDN_PREP_ROWS
    row = lambda wd: pl.BlockSpec((tl, wd), lambda i: (i, 0))
    const = lambda a: pl.BlockSpec(a.shape, lambda i: (0,) * a.ndim)
    ea, eb = expander(GDN_A_LANE), expander(GDN_B_LANE)
    q, k, v, g, b = pl.pallas_call(
        _gdn_prep_kernel,
        grid=(l // tl,),
        in_specs=[row(3 * w), row(LANES), const(cw8), const(prm), const(ea), const(eb), const(eh)],
        out_specs=[row(w)] * 5,
        out_shape=[jax.ShapeDtypeStruct((l, w), f32)] * 5,
        scratch_shapes=[pltpu.VMEM((8, 3 * w), f32)],
        compiler_params=pltpu.CompilerParams(dimension_semantics=("arbitrary",), vmem_limit_bytes=VMEM_LIMIT_BYTES),
        name="gdn_prep",
    )(qkv, small, cw8, prm, ea, eb, eh)
    tc = GDN_TILE_CHUNKS * GDN_CHUNK
    lt = (jnp.arange(GDN_CHUNK)[:, None] >= jnp.arange(GDN_CHUNK)[None, :]).astype(bf16)
    nw = jnp.tile(norm_w, GDN_HEADS).reshape(1, w)
    rowc = pl.BlockSpec((tc, w), lambda i: (i, 0))
    o, s_bd = pl.pallas_call(
        _gdn_chunk_kernel,
        grid=(l // tc,),
        in_specs=[rowc] * 6 + [const(nw), const(lt), const(eh)],
        out_specs=[rowc, pl.BlockSpec((GDN_PAIRS, LANES, LANES), lambda i: (0, 0, 0))],
        out_shape=[jax.ShapeDtypeStruct((l, w), f32), jax.ShapeDtypeStruct((GDN_PAIRS, LANES, LANES), f32)],
        scratch_shapes=[pltpu.VMEM((GDN_PAIRS, LANES, LANES), f32)],
        compiler_params=pltpu.CompilerParams(dimension_semantics=("arbitrary",), vmem_limit_bytes=VMEM_LIMIT_BYTES),
        name="gdn_chunk",
    )(q, k, v, g, b, z, nw, lt, eh)
    s4 = s_bd.reshape(GDN_PAIRS, 2, HEAD_DIM, 2, HEAD_DIM)
    s_fin = jnp.stack([s4[:, 0, :, 0], s4[:, 1, :, 1]], axis=1).reshape(GDN_HEADS, HEAD_DIM, HEAD_DIM)
    return o, s_fin


def _proj_dil_kernel(x_ref, w_ref, cos_ref, sin_ref, bf_ref, kv_ref):
    acc = jnp.dot(x_ref[...].astype(jnp.bfloat16), w_ref[...], preferred_element_type=jnp.float32)
    reps = 2 * DIL_W // LANES
    qk = _rope_lanes(acc[:, :2 * DIL_W], _lane_tile(cos_ref[...], reps), _lane_tile(sin_ref[...], reps))
    v = acc[:, 2 * DIL_W:]
    bf_ref[...] = jnp.concatenate([qk, v], axis=1).astype(jnp.bfloat16)
    kv_ref[...] = jnp.concatenate([qk[:, DIL_W:], v], axis=1)


def _proj_dil(x, w_in):
    l, d = x.shape
    tm = _row_tile(l)
    cos, sin = _rope_tables(jnp.arange(l), LANES)
    return pl.pallas_call(
        _proj_dil_kernel,
        grid=(l // tm,),
        in_specs=[pl.BlockSpec((tm, d), lambda i: (i, 0)),
                  pl.BlockSpec((d, 3 * DIL_W), lambda i: (0, 0)),
                  pl.BlockSpec((tm, LANES), lambda i: (i, 0)),
                  pl.BlockSpec((tm, LANES), lambda i: (i, 0))],
        out_specs=[pl.BlockSpec((tm, 3 * DIL_W), lambda i: (i, 0)),
                   pl.BlockSpec((tm, 2 * DIL_W), lambda i: (i, 0))],
        out_shape=[jax.ShapeDtypeStruct((l, 3 * DIL_W), jnp.bfloat16),
                   jax.ShapeDtypeStruct((l, 2 * DIL_W), jnp.float32)],
        compiler_params=pltpu.CompilerParams(dimension_semantics=("arbitrary",), vmem_limit_bytes=VMEM_LIMIT_BYTES),
        name="proj_dil",
    )(x, w_in.astype(jnp.bfloat16), cos, sin)


def _dil_band_stats(qkv, d):
    l = qkv.shape[0]
    assert l % (d * DIL_BLOCK) == 0
    nb = l // (d * DIL_BLOCK)
    view = qkv.reshape(l // d, d * 3 * DIL_W)

    def part(which, prev):
        return pl.BlockSpec((DIL_BLOCK, DIL_W), (lambda n, r: (jnp.maximum(n - 1, 0), 3 * r + which)) if prev
                            else (lambda n, r: (n, 3 * r + which)))

    out = lambda w: pl.BlockSpec((DIL_BLOCK, w), lambda n, r: (n, r))
    num, st = pl.pallas_call(
        _dil_band_kernel,
        grid=(nb, d),
        in_specs=[part(0, False), part(1, True), part(1, False), part(2, True), part(2, False)],
        out_specs=[out(DIL_W), out(LANES)],
        out_shape=[jax.ShapeDtypeStruct((l // d, d * DIL_W), jnp.float32),
                   jax.ShapeDtypeStruct((l // d, d * LANES), jnp.float32)],
        compiler_params=pltpu.CompilerParams(dimension_semantics=("arbitrary", "arbitrary"),
                                             vmem_limit_bytes=VMEM_LIMIT_BYTES),
        name="dil_band_stats",
    )(view, view, view, view, view)
    return num.reshape(l, DIL_W), st.reshape(l, LANES)


def _dil_merge_kernel(*refs):
    f32 = jnp.float32
    ng = len(DIL_GROUPS)
    nums, sts = refs[:ng], refs[ng:2 * ng]
    ex_ref, w_ref, res_ref, g_ref, b_ref, o_ref = refs[2 * ng:]
    st = [r[...] for r in sts]
    m_all = functools.reduce(jnp.maximum, st)
    ws = [jnp.exp(s - m_all) for s in st]
    den = sum(w * pltpu.roll(s, LANES - DIL_HEADS, 1) for w, s in zip(ws, st))
    head_lane = lax.broadcasted_iota(jnp.int32, den.shape, 1) < DIL_HEADS
    o = sum(_dot_select(jnp.where(head_lane, w / den, 0.0), ex_ref[...]) * n[...]
            for w, n in zip(ws, nums))
    acc = jnp.dot(o.astype(jnp.bfloat16), w_ref[...], preferred_element_type=f32)
    o_ref[...] = _layer_norm_rows(DEEPNORM_ALPHA * res_ref[...] + acc, g_ref[...], b_ref[...])


def _dil_merge_proj(nums, sts, w_out, res, g, b):
    l, n = res.shape
    tm = _row_tile(l)
    expand = (jnp.arange(LANES)[:, None] == jnp.arange(DIL_W)[None, :] // HEAD_DIM).astype(jnp.bfloat16)
    row = lambda w: pl.BlockSpec((tm, w), lambda i: (i, 0))
    const = lambda a: pl.BlockSpec(a.shape, lambda i: (0,) * a.ndim)
    wb = w_out.astype(jnp.bfloat16)
    g2, b2 = g.reshape(1, n), b.reshape(1, n)
    return pl.pallas_call(
        _dil_merge_kernel,
        grid=(l // tm,),
        in_specs=[row(DIL_W)] * len(nums) + [row(LANES)] * len(sts) + [const(expand), const(wb), row(n),
                                                                       const(g2), const(b2)],
        out_specs=row(n),
        out_shape=jax.ShapeDtypeStruct((l, n), jnp.float32),
        compiler_params=pltpu.CompilerParams(dimension_semantics=("arbitrary",), vmem_limit_bytes=VMEM_LIMIT_BYTES),
        name="dil_merge_proj",
    )(*nums, *sts, expand, wb, res, g2, b2)


def _split_cols(h, widths):
    parts, start = [], 0
    for w in widths:
        parts.append(h[..., start:start + w])
        start += w
    return parts


def _even_widths():
    return (NSA_Q_W,) + (NSA_KV_W,) * 6 + (3 * NSA_HEADS, 3 * GDN_W, GDN_HEADS, GDN_HEADS, GDN_W)


def _rms_norm(x, w):
    return x * lax.rsqrt(jnp.mean(jnp.square(x), axis=-1, keepdims=True) + NORM_EPS) * w


def _l2_norm(x):
    return x * lax.rsqrt(jnp.sum(jnp.square(x), axis=-1, keepdims=True) + NORM_EPS)


def _rope(x, pos):
    half = HEAD_DIM // 2
    inv_freq = ROPE_THETA ** (-2.0 * jnp.arange(half, dtype=jnp.float32) / HEAD_DIM)
    ang = pos.astype(jnp.float32)[:, None] * inv_freq[None, :]
    cos, sin = jnp.cos(ang)[:, None, :], jnp.sin(ang)[:, None, :]
    xf = x.astype(jnp.float32)
    x1, x2 = xf[..., :half], xf[..., half:]
    return jnp.concatenate([x1 * cos - x2 * sin, x2 * cos + x1 * sin], axis=-1)


def _causal_dwconv(hist, u, w):
    width, s = w.shape[0], u.shape[1]
    ext = jnp.concatenate([hist.astype(u.dtype), u], axis=1)
    out = w[0] * ext[:, :s]
    for j in range(1, width):
        out = out + w[j] * ext[:, j:j + s]
    return out, ext[:, s:]


def _nsa_compress(rows, w1, b1, w2, pe):
    b, l, g, dh = rows.shape
    nc = l // CMP_BLOCK
    blk = rows[:, :nc * CMP_BLOCK].astype(jnp.float32).reshape(b, nc, CMP_BLOCK, g, dh) + pe[:, None, :]
    flat = blk.transpose(0, 1, 3, 2, 4).reshape(b, nc, g, CMP_BLOCK * dh)
    return jax.nn.gelu(flat @ w1 + b1) @ w2


def _nsa_compressed_kv(k_rows, v_rows, cw1, cb1, cw2, cpe):
    ck = _nsa_compress(k_rows, cw1[0], cb1[0], cw2[0], cpe[0])
    cv = _nsa_compress(v_rows, cw1[1], cb1[1], cw2[1], cpe[1])
    nc = ck.shape[1]
    ck = _rope(ck, (jnp.arange(nc) + 1) * CMP_BLOCK - 1)
    return ck, cv


def _rope_rows_kernel(x_ref, cos_ref, sin_ref, o_ref):
    reps = x_ref.shape[1] // LANES
    o_ref[...] = _rope_lanes(x_ref[...], _lane_tile(cos_ref[...], reps), _lane_tile(sin_ref[...], reps))


def _rope_rows(x):
    l, w = x.shape
    tm = _row_tile(l)
    cos, sin = _rope_tables(jnp.arange(l), LANES)
    return pl.pallas_call(
        _rope_rows_kernel,
        grid=(l // tm,),
        in_specs=[pl.BlockSpec((tm, w), lambda i: (i, 0)), pl.BlockSpec((tm, LANES), lambda i: (i, 0)),
                  pl.BlockSpec((tm, LANES), lambda i: (i, 0))],
        out_specs=pl.BlockSpec((tm, w), lambda i: (i, 0)),
        out_shape=jax.ShapeDtypeStruct((l, w), jnp.float32),
        compiler_params=pltpu.CompilerParams(dimension_semantics=("arbitrary",), vmem_limit_bytes=VMEM_LIMIT_BYTES),
        name="rope_rows",
    )(x, cos, sin)


def _nsa_prompt(q, kc, vc, ks, vs, kw, vw, small, cw1, cb1, cw2, cpe):
    b, l = q.shape[:2]
    flat = lambda t: t.reshape(t.shape[1], -1)
    roped = _rope_rows(jnp.concatenate([flat(q), flat(ks), flat(kw)], axis=1))
    qr = roped[:, :NSA_Q_W]
    ksr = roped[:, NSA_Q_W:NSA_Q_W + NSA_KV_W].reshape(ks.shape)
    kwr = roped[:, NSA_Q_W + NSA_KV_W:].reshape(kw.shape)
    ck, cv = _nsa_compressed_kv(kc, vc, cw1, cb1, cw2, cpe)
    vsf = vs.astype(jnp.float32)
    vwf = vw.astype(jnp.float32)
    o_nsa = _nsa_prompt_attention(qr, small, flat(ck), flat(cv), flat(ksr), flat(vsf), flat(kwr), flat(vwf))
    keep = min(NSA_WINDOW, l)
    rows_cmp = jnp.stack([kc, vc], axis=2)
    rows_slc = jnp.stack([ksr, vsf], axis=2)
    rows_win = jnp.stack([kwr[:, l - keep:], vwf[:, l - keep:]], axis=2)
    return o_nsa[None], rows_cmp, rows_slc, rows_win


def _nsa_sample(q, kc, vc, ks, vs, kw, vw, gate, cmp_pool, slc_pool, layer_idx, win_buf, page_table,
                cw1, cb1, cw2, cpe):
    db, s = q.shape[:2]
    past = page_table.shape[1] * PAGE_SIZE
    wb = win_buf.shape[1]
    assert s == 1 and wb == NSA_WINDOW and past >= wb and past % (STEP_PAGES * PAGE_SIZE) == 0
    qpos = past + jnp.arange(s)
    qr = _rope(q, qpos)
    ckt, cvt = _nsa_sample_compress(cmp_pool, layer_idx, page_table, cw1, cb1, cw2, cpe)
    ksr = _rope(ks, qpos)
    vsf = vs.astype(jnp.float32)
    kwr = _rope(kw, qpos)
    vwf = vw.astype(jnp.float32)
    newrows = jnp.stack([t.reshape(db, NSA_KV_W) for t in (ksr, vsf, kwr, vwf)], axis=1)
    o_nsa, rows_win = _nsa_sample_attention(qr[:, 0], ckt, cvt, slc_pool, layer_idx, page_table, win_buf,
                                              newrows, gate.reshape(db, -1))
    rows_cmp = jnp.stack([kc, vc], axis=2)
    rows_slc = jnp.stack([ksr, vsf], axis=2)
    return o_nsa[:, None], rows_cmp, rows_slc, rows_win


def _gdn_recurrent(q, k, v, g, beta, s0):
    def step(state, xs):
        q_t, k_t, v_t, g_t, b_t = xs
        state = state * jnp.exp(g_t)[..., None, None]
        v_t = (v_t - jnp.einsum('bhk,bhkv->bhv', k_t, state)) * b_t[..., None]
        state = state + jnp.einsum('bhk,bhv->bhkv', k_t, v_t)
        return state, jnp.einsum('bhk,bhkv->bhv', q_t, state)

    xs = tuple(jnp.moveaxis(a, 1, 0) for a in (q, k, v, g, beta))
    s_fin, o = lax.scan(step, s0, xs)
    return jnp.moveaxis(o, 0, 1), s_fin


def _gdn_step(qkv, a, bt, z, conv_hist, s0, conv_w, a_log, dt_bias, norm_w):
    b, s = qkv.shape[:2]
    c, new_hist = _causal_dwconv(conv_hist, qkv, conv_w)
    c = jax.nn.silu(c.astype(jnp.float32))
    q, k, v = [t.reshape(b, s, GDN_HEADS, HEAD_DIM) for t in jnp.split(c, 3, axis=-1)]
    q = _l2_norm(q) * HEAD_DIM ** -0.5
    k = _l2_norm(k)
    beta = jax.nn.sigmoid(bt.astype(jnp.float32))
    g = -jnp.exp(a_log) * jax.nn.softplus(a.astype(jnp.float32) + dt_bias)
    o, s_fin = _gdn_recurrent(q, k, v, g, beta, s0.astype(jnp.float32))
    o = _rms_norm(o, norm_w) * jax.nn.silu(z.astype(jnp.float32).reshape(b, s, GDN_HEADS, HEAD_DIM))
    return o.reshape(b, s, GDN_W), new_hist, s_fin


def _proj(x, w):
    b, s, d = x.shape
    n = w.shape[1]
    npad = -(-n // LANES) * LANES
    wp = jnp.pad(w, ((0, 0), (0, npad - n)))
    return _matmul(x.reshape(b * s, d), wp).reshape(b, s, npad)


def _even_prompt(x, w_in, cw1, cb1, cw2, cpe, conv_w, a_log, dt_bias, norm_w):
    b, l, _ = x.shape
    q, kc, vc, ks, vs, kw, vw, gate, qkv, a, bt, z = _split_cols(_proj(x, w_in), _even_widths())
    heads = lambda t: t.reshape(b, l, -1, HEAD_DIM)
    assert b == 1
    small = jnp.concatenate([gate, a, bt], axis=-1).reshape(l, -1)
    small = jnp.pad(small, ((0, 0), (0, LANES - small.shape[-1])))
    o_nsa, r_cmp, r_slc, r_win = _nsa_prompt(
        heads(q), heads(kc), heads(vc), heads(ks), heads(vs), heads(kw), heads(vw), small, cw1, cb1, cw2, cpe)
    o_gdn, s_fin = _gdn_prompt(qkv[0], small, z[0], conv_w, a_log, dt_bias, norm_w)
    conv_hist = qkv[:, l - (GDN_CONV - 1):]
    return jnp.concatenate([o_nsa, o_gdn[None]], axis=-1), r_cmp, r_slc, r_win, conv_hist, s_fin[None]


def _even_sample(x, cmp_pool, slc_pool, layer_idx, win_buf, conv_hist, s0, page_table,
                 w_in, cw1, cb1, cw2, cpe, conv_w, a_log, dt_bias, norm_w):
    b, s, _ = x.shape
    q, kc, vc, ks, vs, kw, vw, gate, qkv, a, bt, z = _split_cols(_proj(x, w_in), _even_widths())
    heads = lambda t: t.reshape(b, s, -1, HEAD_DIM)
    o_nsa, r_cmp, r_slc, r_win = _nsa_sample(
        heads(q), heads(kc), heads(vc), heads(ks), heads(vs), heads(kw), heads(vw), gate,
        cmp_pool, slc_pool, layer_idx, win_buf, page_table, cw1, cb1, cw2, cpe)
    o_gdn, new_hist, s_fin = _gdn_step(qkv, a, bt, z, conv_hist, s0, conv_w, a_log, dt_bias, norm_w)
    return jnp.concatenate([o_nsa, o_gdn], axis=-1), r_cmp, r_slc, r_win, new_hist, s_fin


def _dil_prompt(x, w_in, w_out, g, b):
    bsz, l, _ = x.shape
    assert bsz == 1
    qkv, kv = _proj_dil(x[0], w_in)
    stats = [_dil_band_stats(qkv, d) for _, d in DIL_GROUPS]
    y = _dil_merge_proj([n for n, _ in stats], [s for _, s in stats], w_out, x[0], g, b)
    keep = min(DIL_MAX_WINDOW, l)
    buf = kv[l - keep:].reshape(1, keep, 2, DIL_HEADS, HEAD_DIM)
    return y, buf


def _dil_sample(x, buf, past, w_in):
    db, s, _ = x.shape
    q, k, v = [t.reshape(db, s, DIL_HEADS, HEAD_DIM) for t in jnp.split(_proj(x, w_in), 3, axis=-1)]
    assert s == 1 and buf.shape[1] == DIL_MAX_WINDOW <= past
    qpos = past + jnp.arange(s)
    qr, kr = _rope(q, qpos), _rope(k, qpos)
    o, new_buf = _dil_sample_attention(qr[:, 0], kr.reshape(db, DIL_W), v.reshape(db, DIL_W).astype(jnp.float32), buf)
    return o[:, None], new_buf


def kernel(x_prompt, x_sample, cache_nsa_cmp_kv, cache_nsa_slc_kv, state_nsa_win_kv, state_gdn_conv,
           state_gdn_S, state_dil_kv, state_ffn_conv, page_table, w_in_a, nsa_cmp_w1, nsa_cmp_b1, nsa_cmp_w2,
           nsa_cmp_pe, gdn_conv_w, gdn_A_log, gdn_dt_bias, gdn_norm_w, w_out_a, w_in_c, w_out_c,
           ln_mix_g, ln_mix_b, ffn_w_in, ffn_conv_w, ffn_conv_b, ffn_w_out, ln_ffn_g, ln_ffn_b):
    past = page_table.shape[1] * PAGE_SIZE
    bp, lp, d = x_prompt.shape
    bs, ls, _ = x_sample.shape
    assert bp == 1 and ls == 1
    xp, xs = x_prompt, x_sample
    cmp_p, cmp_s, slc_p, slc_s, win_p, win_s = [], [], [], [], [], []
    gconv_p, gconv_s, gstate_p, gstate_s = [], [], [], []
    dil_p, dil_s, ffn_p, ffn_s = [], [], [], []
    for layer in range(DEPTH):
        if layer % 2 == 0:
            la = layer // 2
            wa = (w_in_a[la], nsa_cmp_w1[la], nsa_cmp_b1[la], nsa_cmp_w2[la], nsa_cmp_pe[la],
                  gdn_conv_w[la], gdn_A_log[la], gdn_dt_bias[la], gdn_norm_w[la])
            mp, rc, rs, rw, hc, hs_ = _even_prompt(xp, *wa)
            cmp_p.append(rc); slc_p.append(rs); win_p.append(rw); gconv_p.append(hc); gstate_p.append(hs_)
            ms, rc, rs, rw, hc, hs_ = _even_sample(xs, cache_nsa_cmp_kv, cache_nsa_slc_kv, la,
                                                   state_nsa_win_kv[:, la], state_gdn_conv[:, la],
                                                   state_gdn_S[:, la], page_table, *wa)
            cmp_s.append(rc); slc_s.append(rs); win_s.append(rw); gconv_s.append(hc); gstate_s.append(hs_)
            w_out = w_out_a[la]
        else:
            lc = layer // 2
            xp2, bpf = _dil_prompt(xp, w_in_c[lc], w_out_c[lc], ln_mix_g[layer], ln_mix_b[layer])
            ms, bsf = _dil_sample(xs, state_dil_kv[:, lc], past, w_in_c[lc])
            dil_p.append(bpf); dil_s.append(bsf)
            w_out = w_out_c[lc]
        if layer % 2 == 0:
            xp2 = _matmul_ln(mp.reshape(lp, -1), w_out, xp.reshape(lp, d), ln_mix_g[layer], ln_mix_b[layer])
        xs2 = _matmul_ln(ms.reshape(bs, -1), w_out, xs.reshape(bs, d), ln_mix_g[layer], ln_mix_b[layer])
        fargs = (ffn_w_in[layer], ffn_conv_w[layer], ffn_conv_b[layer], ffn_w_out[layer],
                 ln_ffn_g[layer], ln_ffn_b[layer])
        xp3, hp = _ffn_seq(xp2, *fargs)
        xs3, hs = _ffn_step(xs2, state_ffn_conv[:, layer], *fargs)
        xp, xs = xp3.reshape(1, lp, d), xs3.reshape(bs, 1, d)
        ffn_p.append(hp[None]); ffn_s.append(hs)

    def stk(lst):
        return jnp.stack(lst, axis=1)

    return (xp, xs, stk(cmp_p), stk(cmp_s), stk(slc_p), stk(slc_s), stk(win_p), stk(win_s),
            stk(gconv_p), stk(gconv_s), stk(gstate_p), stk(gstate_s), stk(dil_p), stk(dil_s),
            stk(ffn_p), stk(ffn_s))
```

```python
import functools

import jax
import jax.numpy as jnp
from jax import lax
from jax.experimental import pallas as pl
from jax.experimental.pallas import tpu as pltpu
import numpy as np

DEPTH = 2
PAGE_SIZE = 128
HEAD_DIM = 64
ROPE_THETA = 10000.0
NSA_HEADS = 8
NSA_KV_HEADS = 2
NSA_GROUP = NSA_HEADS // NSA_KV_HEADS
CMP_BLOCK = 32
SEL_BLOCK = 64
NSA_TOPN = 16
NSA_WINDOW = 512
NSA_QBLOCK = 128
NSA_FORCE = 1.0e4
GDN_HEADS = 8
GDN_CONV = 4
GDN_CHUNK = 64
DIL_HEADS = 16
DIL_GROUPS = ((128, 1), (512, 4), (2048, 16))
DIL_SPAN = 128
DIL_BLOCK = 128
DIL_MAX_WINDOW = 2048
D_FF = 2816
FFN_CONV = 3
DEEPNORM_ALPHA = (2.0 * DEPTH) ** 0.25
LN_EPS = 1e-5
NORM_EPS = 1e-6
NSA_Q_W = NSA_HEADS * HEAD_DIM
NSA_KV_W = NSA_KV_HEADS * HEAD_DIM
GDN_W = GDN_HEADS * HEAD_DIM
DIL_W = DIL_HEADS * HEAD_DIM

LANES = 128
VMEM_LIMIT_BYTES = 56 * 1024 * 1024


def _layer_norm_rows(r, g, b):
    mu = jnp.mean(r, axis=-1, keepdims=True)
    d = r - mu
    var = jnp.mean(d * d, axis=-1, keepdims=True)
    return d * lax.rsqrt(var + LN_EPS) * g + b


def _mm_kernel(x_ref, w_ref, o_ref):
    o_ref[...] = jnp.dot(x_ref[...].astype(jnp.bfloat16), w_ref[...], preferred_element_type=jnp.float32)


def _mm_ln_kernel(*refs):
    n = (len(refs) - 4) // 2
    res_ref, g_ref, b_ref, o_ref = refs[2 * n:]
    acc = sum(jnp.dot(x[...].astype(jnp.bfloat16), w[...], preferred_element_type=jnp.float32)
              for x, w in zip(refs[:n], refs[n:2 * n]))
    o_ref[...] = _layer_norm_rows(DEEPNORM_ALPHA * res_ref[...] + acc, g_ref[...], b_ref[...])


def _row_tile(m):
    return 512 if m % 512 == 0 else m


def _matmul(x, w):
    m, k = x.shape
    n = w.shape[1]
    tm = _row_tile(m)
    tn = n
    return pl.pallas_call(
        _mm_kernel,
        grid=(m // tm, n // tn),
        in_specs=[pl.BlockSpec((tm, k), lambda i, j: (i, 0)),
                  pl.BlockSpec((k, tn), lambda i, j: (0, j))],
        out_specs=pl.BlockSpec((tm, tn), lambda i, j: (i, j)),
        out_shape=jax.ShapeDtypeStruct((m, n), jnp.float32),
        compiler_params=pltpu.CompilerParams(dimension_semantics=("parallel", "arbitrary"),
                                             vmem_limit_bytes=VMEM_LIMIT_BYTES),
        name="matmul",
    )(x, w.astype(jnp.bfloat16))


def _matmul_ln(xs, w, res, g, b):
    m, n = res.shape
    tm = _row_tile(m)
    wb = w.astype(jnp.bfloat16)
    ws, start = [], 0
    for x in xs:
        ws.append(wb[start:start + x.shape[1]])
        start += x.shape[1]
    assert start == w.shape[0]
    row = lambda a: pl.BlockSpec((tm, a.shape[1]), lambda i: (i, 0))
    const = lambda a: pl.BlockSpec(a.shape, lambda i: (0, 0))
    g2, b2 = g.reshape(1, n), b.reshape(1, n)
    return pl.pallas_call(
        _mm_ln_kernel,
        grid=(m // tm,),
        in_specs=[row(x) for x in xs] + [const(wi) for wi in ws] + [row(res), const(g2), const(b2)],
        out_specs=row(res),
        out_shape=jax.ShapeDtypeStruct((m, n), jnp.float32),
        compiler_params=pltpu.CompilerParams(dimension_semantics=("arbitrary",),
                                             vmem_limit_bytes=VMEM_LIMIT_BYTES),
        name="matmul_ln",
    )(*xs, *ws, res, g2, b2)


FFN_CHUNK = D_FF // 2
FFN_NCHUNK = D_FF // FFN_CHUNK


def _ffn_seq_kernel(x_ref, wi_ref, cw_ref, cb_ref, wo_ref, lg_ref, lb_ref, y_ref, hist_ref, carry_ref):
    f32, bf16 = jnp.float32, jnp.bfloat16
    i = pl.program_id(0)
    tm = x_ref.shape[0]
    x = x_ref[...]
    xb = x.astype(bf16)

    @pl.when(i == 0)
    def _():
        carry_ref[...] = jnp.zeros(carry_ref.shape, f32)

    row = lax.broadcasted_iota(jnp.int32, (8, FFN_CHUNK), 0)

    def conv_half(cols):
        u = jnp.dot(xb, wi_ref[:, cols], preferred_element_type=f32)
        prev = carry_ref[:, cols]
        p2, p1 = prev[6:7], prev[7:8]
        r1, r2 = pltpu.roll(u, 1, 0), pltpu.roll(u, 2, 0)
        u1 = jnp.concatenate([jnp.where(row == 0, p1, r1[:8]), r1[8:]], axis=0)
        u2 = jnp.concatenate([jnp.where(row == 0, p2, jnp.where(row == 1, p1, r2[:8])), r2[8:]], axis=0)
        carry_ref[:, cols] = u[tm - 8:]
        hist_ref[:, cols] = u[tm - 8:]
        cw = cw_ref[:, cols]
        return cw[0:1] * u2 + cw[1:2] * u1 + cw[2:3] * u + cb_ref[:, cols]

    acc = None
    for j in range(FFN_NCHUNK):
        a = conv_half(slice(j * FFN_CHUNK, (j + 1) * FFN_CHUNK))
        g = conv_half(slice(D_FF + j * FFN_CHUNK, D_FF + (j + 1) * FFN_CHUNK))
        h = (a * jax.nn.sigmoid(a) * g).astype(bf16)
        part = jnp.dot(h, wo_ref[j * FFN_CHUNK:(j + 1) * FFN_CHUNK, :], preferred_element_type=f32)
        acc = part if acc is None else acc + part
    y_ref[...] = _layer_norm_rows(DEEPNORM_ALPHA * x + acc, lg_ref[...], lb_ref[...])


def _ffn_seq(x, w_in, conv_w, conv_b, w_out, ln_g, ln_b):
    l, d = x.shape
    tm = _row_tile(l)
    cw8 = jnp.zeros((8, 2 * D_FF), jnp.float32).at[:FFN_CONV].set(conv_w)
    cb = conv_b.reshape(1, 2 * D_FF)
    const = lambda a: pl.BlockSpec(a.shape, lambda i: (0,) * a.ndim, pipeline_mode=pl.Buffered(1))
    wi, wo = w_in.astype(jnp.bfloat16), w_out.astype(jnp.bfloat16)
    g2, b2 = ln_g.reshape(1, d), ln_b.reshape(1, d)
    y, hist = pl.pallas_call(
        _ffn_seq_kernel,
        grid=(l // tm,),
        in_specs=[pl.BlockSpec((tm, d), lambda i: (i, 0)), const(wi), const(cw8), const(cb), const(wo),
                  const(g2), const(b2)],
        out_specs=[pl.BlockSpec((tm, d), lambda i: (i, 0)), pl.BlockSpec((8, 2 * D_FF), lambda i: (i, 0))],
        out_shape=[jax.ShapeDtypeStruct((l, d), jnp.float32),
                   jax.ShapeDtypeStruct((l // tm * 8, 2 * D_FF), jnp.float32)],
        scratch_shapes=[pltpu.VMEM((8, 2 * D_FF), jnp.float32)],
        compiler_params=pltpu.CompilerParams(dimension_semantics=("arbitrary",), vmem_limit_bytes=VMEM_LIMIT_BYTES),
        name="ffn_seq",
    )(x, wi, cw8, cb, wo, g2, b2)
    return y, hist[-(FFN_CONV - 1):]


def _ffn_step_kernel(x_ref, h_ref, wa_ref, wg_ref, cwa_ref, cwg_ref, cba_ref, cbg_ref, wo_ref, lg_ref, lb_ref,
                     y_ref, ua_ref, ug_ref, acc_ref):
    j = pl.program_id(0)
    x = x_ref[...]
    xb = x.astype(jnp.bfloat16)
    ua = jnp.dot(xb, wa_ref[...], preferred_element_type=jnp.float32)
    ug = jnp.dot(xb, wg_ref[...], preferred_element_type=jnp.float32)
    ua_ref[...] = ua
    ug_ref[...] = ug
    cwa, cwg = cwa_ref[...], cwg_ref[...]
    a = cwa[0:1] * h_ref[0, 0] + cwa[1:2] * h_ref[1, 0] + cwa[2:3] * ua + cba_ref[...]
    g = cwg[0:1] * h_ref[0, 1] + cwg[1:2] * h_ref[1, 1] + cwg[2:3] * ug + cbg_ref[...]
    h = (a * jax.nn.sigmoid(a) * g).astype(jnp.bfloat16)
    part = jnp.dot(h, wo_ref[...], preferred_element_type=jnp.float32)

    @pl.when(j == 0)
    def _():
        acc_ref[...] = part

    @pl.when(j > 0)
    def _():
        acc_ref[...] += part

    @pl.when(j == pl.num_programs(0) - 1)
    def _():
        y_ref[...] = _layer_norm_rows(DEEPNORM_ALPHA * x + acc_ref[...], lg_ref[...], lb_ref[...])


def _ffn_step(x, hist, w_in, conv_w, conv_b, w_out, ln_g, ln_b):
    b, d = x.shape
    c, nc = FFN_CHUNK, FFN_NCHUNK
    w_in = w_in.astype(jnp.bfloat16)
    cw8 = jnp.zeros((8, 2 * D_FF), jnp.float32).at[:FFN_CONV].set(conv_w)
    cb = conv_b.reshape(1, 2 * D_FF)
    h4 = jnp.transpose(hist, (1, 0, 2)).reshape(2, b, 2, D_FF).transpose(0, 2, 1, 3)
    y, ua, ug = pl.pallas_call(
        _ffn_step_kernel,
        grid=(nc,),
        in_specs=[pl.BlockSpec((b, d), lambda j: (0, 0)),
                  pl.BlockSpec((2, 2, b, c), lambda j: (0, 0, 0, j)),
                  pl.BlockSpec((d, c), lambda j: (0, j)),
                  pl.BlockSpec((d, c), lambda j: (0, j + nc)),
                  pl.BlockSpec((8, c), lambda j: (0, j)),
                  pl.BlockSpec((8, c), lambda j: (0, j + nc)),
                  pl.BlockSpec((1, c), lambda j: (0, j)),
                  pl.BlockSpec((1, c), lambda j: (0, j + nc)),
                  pl.BlockSpec((c, d), lambda j: (j, 0)),
                  pl.BlockSpec((1, d), lambda j: (0, 0)),
                  pl.BlockSpec((1, d), lambda j: (0, 0))],
        out_specs=[pl.BlockSpec((b, d), lambda j: (0, 0)),
                   pl.BlockSpec((b, c), lambda j: (0, j)),
                   pl.BlockSpec((b, c), lambda j: (0, j))],
        out_shape=[jax.ShapeDtypeStruct((b, d), jnp.float32),
                   jax.ShapeDtypeStruct((b, D_FF), jnp.float32),
                   jax.ShapeDtypeStruct((b, D_FF), jnp.float32)],
        scratch_shapes=[pltpu.VMEM((b, d), jnp.float32)],
        compiler_params=pltpu.CompilerParams(dimension_semantics=("arbitrary",),
                                             vmem_limit_bytes=VMEM_LIMIT_BYTES),
        name="ffn_step",
    )(x, h4, w_in, w_in, cw8, cw8, cb, cb, w_out.astype(jnp.bfloat16), ln_g.reshape(1, d), ln_b.reshape(1, d))
    u = jnp.concatenate([ua, ug], axis=-1)
    return y, jnp.concatenate([hist[:, 1:], u[:, None]], axis=1)


NEG = -1e30
NSA_KT = 512
NSA_COLS = NSA_HEADS * NSA_QBLOCK
NSA_WSPAN = NSA_WINDOW + NSA_QBLOCK
LOG2E = 1.4426950408889634
NSA_VROWS = HEAD_DIM + 8


def _lane_tile(x, n):
    return jnp.concatenate([x] * n, axis=1)


def _nsa_prompt_kernel(q_ref, sm_ref, ck_ref, cvt_ref, ks_ref, vst_ref, kw_ref, vwt_ref, hot_ref, o_ref,
                       selb_ref, m_ref, acc_ref, *, ns):
    f32, bf16 = jnp.float32, jnp.bfloat16
    qb = NSA_QBLOCK
    i = pl.program_id(0)
    s0 = i * qb
    half = NSA_COLS // 2

    qt = (q_ref[...] * (HEAD_DIM ** -0.5 * LOG2E)).T
    zero = jnp.zeros((HEAD_DIM, qb), f32)
    top = jnp.concatenate([qt[h * HEAD_DIM:(h + 1) * HEAD_DIM] for h in range(NSA_GROUP)] + [zero] * NSA_GROUP, axis=1)
    bot = jnp.concatenate([zero] * NSA_GROUP + [qt[h * HEAD_DIM:(h + 1) * HEAD_DIM]
                                                for h in range(NSA_GROUP, NSA_HEADS)], axis=1)
    qbd = jnp.concatenate([top, bot], axis=0).astype(bf16)

    def pv(vt, p):
        pb = p.astype(bf16)
        rows = vt.shape[0] // NSA_KV_HEADS
        return [jnp.dot(vt[g * rows:(g + 1) * rows], pb[:, g * half:(g + 1) * half],
                        preferred_element_type=f32) for g in range(NSA_KV_HEADS)]

    nc = 2 * ns
    r = lax.broadcasted_iota(jnp.int32, (nc, qb), 0)
    lane = lax.broadcasted_iota(jnp.int32, (nc, qb), 1)
    cidx = jnp.where(r < ns, 2 * r, 2 * (r - ns) + 1)
    cbias = jnp.where((cidx + 1) * CMP_BLOCK - 1 <= s0 + lane, 0.0, NEG)
    sc = jnp.dot(ck_ref[...], qbd, preferred_element_type=f32) + _lane_tile(cbias, NSA_HEADS)
    m = jnp.max(sc, axis=0, keepdims=True)
    p = jnp.exp2(sc - m)
    pn = p * jnp.where(m > 0.5 * NEG, 1.0 / jnp.sum(p, axis=0, keepdims=True), 0.0)
    o_cmp = pv(cvt_ref[...], pn)

    blk = lax.broadcasted_iota(jnp.int32, (ns, qb), 0)
    qpos = s0 + lax.broadcasted_iota(jnp.int32, (ns, qb), 1)
    cur = qpos // SEL_BLOCK
    forced = (blk == 0) | (blk == cur) | (blk == cur - 1)
    for g in range(NSA_KV_HEADS):
        imp = pn[:, g * half:g * half + qb]
        for h in range(1, NSA_GROUP):
            imp = imp + pn[:, g * half + h * qb:g * half + (h + 1) * qb]
        imp = imp[:ns] + imp[ns:]
        val = jnp.where(blk > cur, -1.0, jnp.where(forced, -jnp.inf, imp))
        bias = jnp.where(forced & (blk <= cur), 0.0, NEG)
        for _ in range(NSA_TOPN - 3):
            top = jnp.max(val, axis=0, keepdims=True)
            pick = jnp.min(jnp.where(val == top, blk, ns), axis=0, keepdims=True)
            hit = blk == pick
            bias = jnp.where(hit, 0.0, bias)
            val = jnp.where(hit, -jnp.inf, val)
        selb_ref[g] = bias

    m_ref[...] = jnp.full(m_ref.shape, NEG, f32)
    acc_ref[...] = jnp.zeros(acc_ref.shape, f32)
    per_tile = NSA_KT // SEL_BLOCK
    zpad = jnp.zeros((LANES - 16, NSA_COLS), bf16)

    def slc_tile(kt, causal):
        k0 = pl.multiple_of(kt * NSA_KT, NSA_KT)
        b0 = pl.multiple_of(kt * per_tile, per_tile)
        brow = jnp.concatenate([selb_ref[g, pl.ds(b0, per_tile), :] for g in range(NSA_KV_HEADS)
                                for _ in range(NSA_GROUP)], axis=1)
        brow = jnp.concatenate([brow, jnp.zeros((16 - per_tile, NSA_COLS), f32)], axis=0).astype(bf16)
        q_aug = jnp.concatenate([qbd, brow, zpad], axis=0)
        k_aug = jnp.concatenate([ks_ref[pl.ds(k0, NSA_KT), :], hot_ref[...]], axis=1)
        s = jnp.dot(k_aug, q_aug, preferred_element_type=f32)
        if causal:
            kpos = k0 + lax.broadcasted_iota(jnp.int32, (NSA_KT, qb), 0)
            qq = s0 + lax.broadcasted_iota(jnp.int32, (NSA_KT, qb), 1)
            s = s + _lane_tile(jnp.where(kpos <= qq, 0.0, NEG), NSA_HEADS)
        m_old = m_ref[...]
        m_new = jnp.maximum(m_old, jnp.max(s, axis=0, keepdims=True))
        alpha = jnp.exp2(m_old - m_new)
        p = jnp.exp2(s - m_new)
        m_ref[...] = m_new
        upd = pv(vst_ref[:, pl.ds(k0, NSA_KT)], p)
        for g in range(NSA_KV_HEADS):
            acc_ref[g] = acc_ref[g] * alpha[:, g * half:(g + 1) * half] + upd[g]

    kd = s0 // NSA_KT

    def body(j, carry):
        slc_tile(2 * j, False)
        slc_tile(2 * j + 1, False)
        return carry

    lax.fori_loop(0, kd // 2, body, 0)

    @pl.when(kd % 2 == 1)
    def _():
        slc_tile(kd - 1, False)

    slc_tile(kd, True)
    inv_slc = [1.0 / acc_ref[g, HEAD_DIM:HEAD_DIM + 1, :] for g in range(NSA_KV_HEADS)]

    w0 = pl.multiple_of(s0, qb)
    sw = jnp.dot(kw_ref[pl.ds(w0, NSA_WSPAN), :], qbd, preferred_element_type=f32)
    rr = lax.broadcasted_iota(jnp.int32, (NSA_WSPAN, qb), 0)
    qi = lax.broadcasted_iota(jnp.int32, (NSA_WSPAN, qb), 1)
    ok = (rr >= qi) & (rr <= qi + NSA_WINDOW) & (rr + s0 >= NSA_WINDOW)
    sw = sw + _lane_tile(jnp.where(ok, 0.0, NEG), NSA_HEADS)
    pw = jnp.exp2(sw - jnp.max(sw, axis=0, keepdims=True))
    o_win = pv(vwt_ref[:, pl.ds(w0, NSA_WSPAN)], pw)
    inv_win = [1.0 / o[HEAD_DIM:HEAD_DIM + 1] for o in o_win]

    gt = jax.nn.sigmoid(sm_ref[...].T)
    outs = []
    for h in range(NSA_HEADS):
        g, hg = divmod(h, NSA_GROUP)
        c0, c1 = hg * qb, (hg + 1) * qb
        g_cmp = gt[h:h + 1]
        g_slc = gt[NSA_HEADS + h:NSA_HEADS + h + 1] * inv_slc[g][:, c0:c1]
        g_win = gt[2 * NSA_HEADS + h:2 * NSA_HEADS + h + 1] * inv_win[g][:, c0:c1]
        outs.append(o_cmp[g][:, c0:c1] * g_cmp + acc_ref[g, :HEAD_DIM, c0:c1] * g_slc
                    + o_win[g][:HEAD_DIM, c0:c1] * g_win)
    o_ref[...] = jnp.concatenate(outs, axis=0).T


def _nsa_prompt_attention(qr, small, ck, cv, ksr, vs, kwr, vw):
    l = qr.shape[0]
    ns = l // SEL_BLOCK
    assert ns >= NSA_TOPN and l % NSA_KT == 0
    nc = 2 * ns
    bf16 = jnp.bfloat16
    perm =jnp.concatenate([jnp.arange(0, nc, 2), jnp.arange(1, nc, 2)])
    ckp = ck[perm].astype(bf16)
    cvt = cv[perm].T.astype(bf16)
    pad = jnp.zeros((NSA_WINDOW, NSA_KV_W), bf16)
    kwp = jnp.concatenate([pad, kwr.astype(bf16)], axis=0)
    def with_ones(vt):
        n = vt.shape[1]
        extra = jnp.concatenate([jnp.ones((1, n), bf16), jnp.zeros((NSA_VROWS - HEAD_DIM - 1, n), bf16)], axis=0)
        return jnp.concatenate([x for g in range(NSA_KV_HEADS) for x in (vt[g * HEAD_DIM:(g + 1) * HEAD_DIM], extra)], axis=0)

    vwt = with_ones(jnp.concatenate([pad, vw.astype(bf16)], axis=0).T)
    hot = (jnp.arange(NSA_KT)[:, None] // SEL_BLOCK == jnp.arange(LANES)[None, :]).astype(bf16)
    full = lambda a: pl.BlockSpec(a.shape, lambda i: (0,) * a.ndim)
    args = (qr, small, ckp, cvt, ksr.astype(bf16), with_ones(vs.T.astype(bf16)), kwp, vwt, hot)
    return pl.pallas_call(
        functools.partial(_nsa_prompt_kernel, ns=ns),
        grid=(l // NSA_QBLOCK,),
        in_specs=[pl.BlockSpec((NSA_QBLOCK, NSA_Q_W), lambda i: (i, 0)),
                  pl.BlockSpec((NSA_QBLOCK, LANES), lambda i: (i, 0))] + [full(a) for a in args[2:]],
        out_specs=pl.BlockSpec((NSA_QBLOCK, NSA_Q_W), lambda i: (i, 0)),
        out_shape=jax.ShapeDtypeStruct((l, NSA_Q_W), jnp.float32),
        scratch_shapes=[pltpu.VMEM((NSA_KV_HEADS, ns, NSA_QBLOCK), jnp.float32),
                        pltpu.VMEM((1, NSA_COLS), jnp.float32),
                        pltpu.VMEM((NSA_KV_HEADS, NSA_VROWS, NSA_COLS // 2), jnp.float32)],
        compiler_params=pltpu.CompilerParams(dimension_semantics=("arbitrary",),
                                             vmem_limit_bytes=VMEM_LIMIT_BYTES),
        name="nsa_prompt",
    )(*args)


CMP_HIDDEN = 2 * HEAD_DIM
CMP_PER_PAGE = PAGE_SIZE // CMP_BLOCK
CMP_PAGES_PER_STEP = 64
STEP_PAGES = 64
STEP_ROWS = 16


def _rope_tables(pos, width):
    half = HEAD_DIM // 2
    inv_freq = ROPE_THETA ** (-2.0 * jnp.arange(half, dtype=jnp.float32) / HEAD_DIM)
    ang = pos.astype(jnp.float32)[:, None] * inv_freq[None, :]
    cos, sin = jnp.cos(ang), jnp.sin(ang)
    reps = width // HEAD_DIM
    return (jnp.tile(jnp.concatenate([cos, cos], axis=1), (1, reps)),
            jnp.tile(jnp.concatenate([-sin, sin], axis=1), (1, reps)))


def _rope_lanes(x, cos, sin_signed):
    n = x.shape[-1]
    lane = lax.broadcasted_iota(jnp.int32, x.shape, x.ndim - 1)
    first = (lane % HEAD_DIM) < HEAD_DIM // 2
    partner = jnp.where(first, pltpu.roll(x, n - HEAD_DIM // 2, x.ndim - 1), pltpu.roll(x, HEAD_DIM // 2, x.ndim - 1))
    return x * cos + partner * sin_signed


def _nt_dot(a, b):
    return lax.dot_general(a, b, (((1,), (1,)), ((), ())), preferred_element_type=jnp.float32)


CMP_STEP_LANES = CMP_PAGES_PER_STEP * CMP_PER_PAGE
CMP_FEATURE_GROUP = 16


def _cmp_lane_blocks(n_pages):
    lane = np.arange(n_pages * CMP_PER_PAGE)
    step, rem = lane // CMP_STEP_LANES, lane % CMP_STEP_LANES
    j, pl_ = rem // CMP_PAGES_PER_STEP, rem % CMP_PAGES_PER_STEP
    return (step * CMP_PAGES_PER_STEP + pl_) * CMP_PER_PAGE + j


def _cmp_step_kernel(pt_ref, *refs):
    f32 = jnp.float32
    npg = CMP_PAGES_PER_STEP
    pages = refs[:npg]
    pe_ref, w1_ref, b1_ref, w2_ref, cos_ref, sin_ref, ck_ref, cv_ref, slab_ref = refs[npg:]
    outs = (ck_ref, cv_ref)
    for k, r in enumerate(pages):
        for s in range(2 * NSA_KV_HEADS):
            slab_ref[s, k * HEAD_DIM:(k + 1) * HEAD_DIM, :] = r[0, s]
    for kv in range(2):
        h = jnp.zeros((NSA_KV_HEADS * npg, CMP_PER_PAGE * CMP_HIDDEN), f32)
        for dg in range(HEAD_DIM // CMP_FEATURE_GROUP):
            x = jnp.concatenate(
                [jnp.concatenate([slab_ref[2 * kv + g, pl.ds(d, npg, stride=HEAD_DIM), :]
                                  for g in range(NSA_KV_HEADS)], axis=0) + pe_ref[kv, d]
                 for d in range(dg * CMP_FEATURE_GROUP, (dg + 1) * CMP_FEATURE_GROUP)], axis=1)
            h = h + jnp.dot(x.astype(jnp.bfloat16), w1_ref[kv, dg], preferred_element_type=f32)
        act = jax.nn.gelu(h + b1_ref[kv]).astype(jnp.bfloat16)
        ct = _nt_dot(w2_ref[kv], act)
        tile = jnp.concatenate(
            [jnp.concatenate([ct[j * HEAD_DIM:(j + 1) * HEAD_DIM, g * npg:(g + 1) * npg] for j in range(CMP_PER_PAGE)],
                             axis=1) for g in range(NSA_KV_HEADS)], axis=0)
        if kv == 0:
            row = lax.broadcasted_iota(jnp.int32, tile.shape, 0)
            n = tile.shape[0]
            partner = jnp.where((row % HEAD_DIM) < HEAD_DIM // 2, pltpu.roll(tile, n - HEAD_DIM // 2, 0),
                                pltpu.roll(tile, HEAD_DIM // 2, 0))
            tile = tile * cos_ref[...] + partner * sin_ref[...]
        outs[kv][0] = tile


def _compress_weights(cw1, cb1, cw2, cpe):
    eye = jnp.eye(CMP_PER_PAGE, dtype=jnp.float32)
    w1r = cw1.reshape(2, CMP_BLOCK, HEAD_DIM, CMP_HIDDEN)
    w1 = jnp.einsum('ktdn,ja->kdjtan', w1r, eye).reshape(
        2, HEAD_DIM // CMP_FEATURE_GROUP, CMP_FEATURE_GROUP * PAGE_SIZE, CMP_PER_PAGE * CMP_HIDDEN)
    b1 = jnp.tile(cb1, (1, CMP_PER_PAGE))[:, None, :]
    w2 = jnp.einsum('knd,ja->kjdan', cw2, eye).reshape(2, CMP_PER_PAGE * HEAD_DIM, CMP_PER_PAGE * CMP_HIDDEN)
    pe = jnp.tile(cpe.transpose(0, 2, 1), (1, 1, CMP_PER_PAGE))[:, :, None, :]
    return w1.astype(jnp.bfloat16), b1, w2.astype(jnp.bfloat16), pe


def _nsa_sample_compress(pool, layer_idx, page_table, cw1, cb1, cw2, cpe):
    n_pool, nl = pool.shape[:2]
    db, n_pages = page_table.shape
    npg = CMP_PAGES_PER_STEP
    nchunk = n_pages // npg
    nc = n_pages * CMP_PER_PAGE
    view = jnp.transpose(pool, (0, 1, 3, 4, 5, 2)).reshape(n_pool * nl, 2 * NSA_KV_HEADS, HEAD_DIM, PAGE_SIZE)
    pt = (page_table * nl + layer_idx).reshape(-1).astype(jnp.int32)
    w1, b1, w2, pe = _compress_weights(cw1, cb1, cw2, cpe)
    pos = (jnp.asarray(_cmp_lane_blocks(n_pages)) + 1) * CMP_BLOCK - 1
    half = HEAD_DIM // 2
    inv_freq = ROPE_THETA ** (-2.0 * jnp.arange(half, dtype=jnp.float32) / HEAD_DIM)
    ang = inv_freq[:, None] * pos.astype(jnp.float32)[None, :]
    cos = jnp.tile(jnp.cos(ang), (2 * NSA_KV_HEADS, 1))
    sin = jnp.tile(jnp.concatenate([-jnp.sin(ang), jnp.sin(ang)], axis=0), (NSA_KV_HEADS, 1))

    def page_map(k):
        return lambda b, c, pt_ref: (pt_ref[b * n_pages + c * npg + k], 0, 0, 0)

    const = lambda a: pl.BlockSpec(a.shape, lambda b, c, pt_ref: (0,) * a.ndim, pipeline_mode=pl.Buffered(1))
    lanes_c = lambda: pl.BlockSpec((NSA_KV_W, CMP_STEP_LANES), lambda b, c, pt_ref: (0, c))
    grid_spec = pltpu.PrefetchScalarGridSpec(
        num_scalar_prefetch=1, grid=(db, nchunk),
        in_specs=[pl.BlockSpec((1, 2 * NSA_KV_HEADS, HEAD_DIM, PAGE_SIZE), page_map(k)) for k in range(npg)]
        + [const(pe), const(w1), const(b1), const(w2), lanes_c(), lanes_c()],
        out_specs=[pl.BlockSpec((1, NSA_KV_W, CMP_STEP_LANES), lambda b, c, pt_ref: (b, 0, c))] * 2,
        scratch_shapes=[pltpu.VMEM((2 * NSA_KV_HEADS, npg * HEAD_DIM, PAGE_SIZE), jnp.float32)])
    return pl.pallas_call(
        _cmp_step_kernel, grid_spec=grid_spec,
        out_shape=[jax.ShapeDtypeStruct((db, NSA_KV_W, nc), jnp.float32)] * 2,
        compiler_params=pltpu.CompilerParams(dimension_semantics=("arbitrary", "arbitrary"),
                                             vmem_limit_bytes=VMEM_LIMIT_BYTES),
        name="nsa_sample_compress",
    )(pt, *([view] * npg), pe, w1, b1, w2, cos, sin)


def _nsa_step_kernel(pt_ref, *refs, topn):
    f32, bf16 = jnp.float32, jnp.bfloat16
    npg = STEP_PAGES
    q_ref, ck_ref, cv_ref = refs[:3]
    pages = refs[3:3 + npg]
    (win_ref, newr_ref, newt_ref, gate_ref, blk_ref, exp_ref, o_ref, wout_ref,
     selt_ref, m_ref, l_ref, acc_ref, side_ref) = refs[3 + npg:]
    cc = pl.program_id(1)
    q16 = q_ref[0]
    qb = q16.astype(bf16)
    row16 = lax.broadcasted_iota(jnp.int32, (STEP_ROWS, 1), 0)

    def new_key_scores(krow):
        return jnp.sum(q16 * krow, axis=1, keepdims=True)

    @pl.when(cc == 0)
    def _():
        nc = ck_ref.shape[2]
        s = jnp.dot(qb, ck_ref[0].astype(bf16), preferred_element_type=f32)
        p = jnp.exp(s - jnp.max(s, axis=1, keepdims=True))
        pn = p / jnp.sum(p, axis=1, keepdims=True)
        o_cmp = _nt_dot(pn.astype(bf16), cv_ref[0].astype(bf16))

        rowp = lax.broadcasted_iota(jnp.int32, pn.shape, 0)
        row8 = lax.broadcasted_iota(jnp.int32, (8, nc), 0)
        blk = blk_ref[...]
        val = jnp.full((8, nc), -jnp.inf, f32)
        for g in range(NSA_KV_HEADS):
            ig = jnp.sum(jnp.where((rowp >= g * NSA_GROUP) & (rowp < (g + 1) * NSA_GROUP), pn, 0.0),
                         axis=0, keepdims=True)
            ig = ig + pltpu.roll(ig, nc - CMP_PAGES_PER_STEP, 1)
            vg = jnp.where(blk[:1] < 0, -jnp.inf, jnp.where(blk[1:2] > 0, NSA_FORCE, ig))
            val = jnp.where(row8 == g, vg, val)
        sblk = jnp.where(blk[:1] < 0, nc, blk[:1])
        sel = jnp.zeros((8, nc), f32)
        for _ in range(topn):
            top = jnp.max(val, axis=1, keepdims=True)
            pick = jnp.min(jnp.where(val == top, sblk, nc), axis=1, keepdims=True)
            hit = sblk == pick
            sel = jnp.where(hit, 1.0, sel)
            val = jnp.where(hit, -jnp.inf, val)
        selh = jnp.where(row16 < NSA_GROUP, sel[0:1], jnp.where(row16 < NSA_HEADS, sel[1:2], 0.0))
        wsel = selt_ref.shape[2]
        for j in range(selt_ref.shape[0]):
            selt_ref[j] = selh[:, j * wsel:(j + 1) * wsel]

        m_ref[...] = new_key_scores(newr_ref[0, 0:1, :])
        l_ref[...] = jnp.ones(l_ref.shape, f32)
        acc_ref[...] = jnp.broadcast_to(newr_ref[0, 1:2, :], acc_ref.shape)

        sw = jnp.dot(qb, win_ref[0, 0].astype(bf16), preferred_element_type=f32)
        sn = new_key_scores(newr_ref[0, 2:3, :])
        mw = jnp.maximum(jnp.max(sw, axis=1, keepdims=True), sn)
        pw, pnw = jnp.exp(sw - mw), jnp.exp(sn - mw)
        lw = jnp.sum(pw, axis=1, keepdims=True) + pnw
        o_win = (_nt_dot(pw.astype(bf16), win_ref[0, 1].astype(bf16)) + pnw * newr_ref[0, 3:4, :]) / lw
        gt = jax.nn.sigmoid(gate_ref[0])
        side_ref[...] = gt[:, 0:1] * o_cmp + gt[:, 2:3] * o_win
        wl = win_ref.shape[3]
        lane = lax.broadcasted_iota(jnp.int32, (NSA_KV_W, wl), 1)
        for kv in range(2):
            wout_ref[0, kv] = jnp.where(lane == wl - 1, newt_ref[0, :, 2 + kv:3 + kv],
                                        pltpu.roll(win_ref[0, kv], wl - 1, 1))

    kt = jnp.concatenate([r[0, 0] for r in pages], axis=1)
    vt = jnp.concatenate([r[0, 1] for r in pages], axis=1)
    s = jnp.dot(qb, kt.astype(bf16), preferred_element_type=f32)
    picked = jnp.dot(selt_ref[cc].astype(bf16), exp_ref[...], preferred_element_type=f32)
    s = s + (picked - 1.0) * (-NEG)
    m_old = m_ref[...]
    m_new = jnp.maximum(m_old, jnp.max(s, axis=1, keepdims=True))
    alpha = jnp.exp(m_old - m_new)
    p = jnp.exp(s - m_new)
    m_ref[...] = m_new
    l_ref[...] = l_ref[...] * alpha + jnp.sum(p, axis=1, keepdims=True)
    acc_ref[...] = acc_ref[...] * alpha + _nt_dot(p.astype(bf16), vt.astype(bf16))

    @pl.when(cc == pl.num_programs(1) - 1)
    def _():
        gt = jax.nn.sigmoid(gate_ref[0])
        o_ref[0] = side_ref[...] + gt[:, 1:2] * acc_ref[...] / l_ref[...]


def _nsa_sample_attention(qr, ckt, cvt, slc_pool, layer_idx, page_table, win_buf, newrows, gate):
    db, n_pages = page_table.shape
    n_pool, nl = slc_pool.shape[:2]
    wlen = win_buf.shape[1]
    nc = ckt.shape[2]
    past = n_pages * PAGE_SIZE
    cur = past // SEL_BLOCK
    npg = STEP_PAGES
    nchunk = n_pages // npg
    keys = npg * PAGE_SIZE
    wsel = npg * CMP_PER_PAGE
    f32, bf16 = jnp.float32, jnp.bfloat16
    view = jnp.transpose(slc_pool, (0, 1, 3, 4, 5, 2)).reshape(n_pool * nl, 2, NSA_KV_W, PAGE_SIZE)
    wint = jnp.transpose(win_buf, (0, 2, 3, 4, 1)).reshape(db, 2, NSA_KV_W, wlen)
    pt = (page_table * nl + layer_idx).reshape(-1).astype(jnp.int32)
    hmask = (jnp.arange(NSA_HEADS)[:, None] // NSA_GROUP == jnp.arange(NSA_KV_HEADS)[None, :]).astype(f32)
    q16 = (qr * HEAD_DIM ** -0.5)[:, :, None, :] * hmask[None, :, :, None]
    q16 = jnp.pad(q16.reshape(db, NSA_HEADS, NSA_KV_W), ((0, 0), (0, STEP_ROWS - NSA_HEADS), (0, 0)))
    newr = jnp.pad(newrows, ((0, 0), (0, 8 - newrows.shape[1]), (0, 0)))
    newt = jnp.pad(newrows.transpose(0, 2, 1), ((0, 0), (0, 0), (0, LANES - newrows.shape[1])))
    g16 = jnp.pad(gate.reshape(db, 3, NSA_HEADS).transpose(0, 2, 1),
                  ((0, 0), (0, STEP_ROWS - NSA_HEADS), (0, LANES - 3)))
    cblk = _cmp_lane_blocks(n_pages)
    jj = cblk % CMP_PER_PAGE
    sblk = np.where(jj % 2 == 0, cblk // 2, -1)
    forced = ((sblk == 0) | (sblk == cur - 1)).astype(np.int32)
    blk8 = np.zeros((8, nc), np.int32)
    blk8[0], blk8[1] = sblk, forced
    loc = np.arange(wsel)
    lstep, lrem = loc // CMP_STEP_LANES, loc % CMP_STEP_LANES
    lj, lpage = lrem // CMP_PAGES_PER_STEP, lstep * CMP_PAGES_PER_STEP + lrem % CMP_PAGES_PER_STEP
    kidx = np.arange(keys)
    expand = ((lj[:, None] % 2 == 0) & (kidx[None, :] // PAGE_SIZE == lpage[:, None])
              & ((kidx[None, :] % PAGE_SIZE) // SEL_BLOCK == lj[:, None] // 2)).astype(np.float32)
    topn = min(NSA_TOPN, cur + 1) - 1

    def page_map(k):
        return lambda b, c, pt_ref: (pt_ref[b * n_pages + c * npg + k], 0, 0, 0)

    per_b = lambda shp: pl.BlockSpec((1,) + shp, lambda b, c, pt_ref: (b,) + (0,) * len(shp))
    const = lambda a: pl.BlockSpec(a.shape, lambda b, c, pt_ref: (0,) * a.ndim)
    consts = (jnp.asarray(blk8), jnp.asarray(expand, bf16))
    grid_spec = pltpu.PrefetchScalarGridSpec(
        num_scalar_prefetch=1, grid=(db, nchunk),
        in_specs=[per_b((STEP_ROWS, NSA_KV_W)), per_b((NSA_KV_W, nc)), per_b((NSA_KV_W, nc))]
        + [pl.BlockSpec((1, 2, NSA_KV_W, PAGE_SIZE), page_map(k)) for k in range(npg)]
        + [per_b((2, NSA_KV_W, wlen)), per_b((8, NSA_KV_W)), per_b((NSA_KV_W, LANES)), per_b((STEP_ROWS, LANES))]
        + [const(a) for a in consts],
        out_specs=[per_b((STEP_ROWS, NSA_KV_W)), per_b((2, NSA_KV_W, wlen))],
        scratch_shapes=[pltpu.VMEM((nc // wsel, STEP_ROWS, wsel), f32),
                        pltpu.VMEM((STEP_ROWS, 1), f32), pltpu.VMEM((STEP_ROWS, 1), f32),
                        pltpu.VMEM((STEP_ROWS, NSA_KV_W), f32), pltpu.VMEM((STEP_ROWS, NSA_KV_W), f32)])
    o16, wout = pl.pallas_call(
        functools.partial(_nsa_step_kernel, topn=topn), grid_spec=grid_spec,
        out_shape=[jax.ShapeDtypeStruct((db, STEP_ROWS, NSA_KV_W), f32),
                   jax.ShapeDtypeStruct((db, 2, NSA_KV_W, wlen), f32)],
        compiler_params=pltpu.CompilerParams(dimension_semantics=("arbitrary", "arbitrary"),
                                             vmem_limit_bytes=VMEM_LIMIT_BYTES),
        name="nsa_sample_attention",
    )(pt, q16, ckt, cvt, *([view] * npg), wint, newr, newt, g16, *consts)
    o = o16[:, :NSA_HEADS].reshape(db, NSA_HEADS, NSA_KV_HEADS, HEAD_DIM)
    o = jnp.take_along_axis(o, (jnp.arange(NSA_HEADS) // NSA_GROUP)[None, :, None, None], axis=2)
    wout = jnp.transpose(wout.reshape(db, 2, NSA_KV_HEADS, HEAD_DIM, wlen), (0, 4, 1, 2, 3))
    return o.reshape(db, NSA_HEADS * HEAD_DIM), wout


DIL_ROW_CHUNK = 64


def _dil_step_kernel(q_ref, buf_ref, newt_ref, newr_ref, bias_ref, o_ref, out_ref, p_ref, pn_ref, den_ref):
    f32, bf16 = jnp.float32, jnp.bfloat16
    kv = pl.program_id(1)
    wlen = buf_ref.shape[3]
    nrow = buf_ref.shape[2]
    q16 = q_ref[0]

    @pl.when(kv == 0)
    def _():
        s = jnp.dot(q16.astype(bf16), buf_ref[0, 0].astype(bf16), preferred_element_type=f32)
        s_new = jnp.sum(q16 * newr_ref[0, 0:1, :], axis=1, keepdims=True)
        ms, es, ens, dens = [], [], [], []
        for g in range(len(DIL_GROUPS)):
            sg = s + bias_ref[g:g + 1, :]
            m = jnp.maximum(jnp.max(sg, axis=1, keepdims=True), s_new)
            e, en = jnp.exp(sg - m), jnp.exp(s_new - m)
            ms.append(m); es.append(e); ens.append(en)
            dens.append(jnp.sum(e, axis=1, keepdims=True) + en)
        m_all = functools.reduce(jnp.maximum, ms)
        ws = [jnp.exp(m - m_all) for m in ms]
        p_ref[...] = sum(w * e for w, e in zip(ws, es))
        pn_ref[...] = sum(w * en for w, en in zip(ws, ens))
        den_ref[...] = sum(w * d for w, d in zip(ws, dens))

    @pl.when(kv == 1)
    def _():
        r = _nt_dot(p_ref[...].astype(bf16), buf_ref[0, 0].astype(bf16))
        r = (r + pn_ref[...] * newr_ref[0, 1:2, :]) / den_ref[...]
        head = lax.broadcasted_iota(jnp.int32, r.shape, 1) // HEAD_DIM
        row = lax.broadcasted_iota(jnp.int32, r.shape, 0)
        o_ref[0] = jnp.broadcast_to(jnp.sum(jnp.where(head == row, r, 0.0), axis=0, keepdims=True), o_ref.shape[1:])

    lane = lax.broadcasted_iota(jnp.int32, (DIL_ROW_CHUNK, wlen), 1)
    for c in range(nrow // DIL_ROW_CHUNK):
        rs = slice(c * DIL_ROW_CHUNK, (c + 1) * DIL_ROW_CHUNK)
        col = jnp.where(kv == 0, newt_ref[0, rs, 0:1], newt_ref[0, rs, 1:2])
        out_ref[0, 0, rs, :] = jnp.where(lane == wlen - 1, col, pltpu.roll(buf_ref[0, 0, rs, :], wlen - 1, 1))


def _dil_sample_attention(qr, kr_new, v_new, buf):
    db, wlen = buf.shape[:2]
    f32 = jnp.float32
    buft = jnp.transpose(buf, (0, 2, 3, 4, 1)).reshape(db, 2, DIL_W, wlen)
    eye = jnp.eye(DIL_HEADS, dtype=f32)
    q16 = ((qr * HEAD_DIM ** -0.5)[:, :, None, :] * eye[None, :, :, None]).reshape(db, DIL_HEADS, DIL_W)
    newr = jnp.pad(jnp.stack([kr_new, v_new], axis=1), ((0, 0), (0, 6), (0, 0)))
    newt = jnp.pad(jnp.stack([kr_new, v_new], axis=2), ((0, 0), (0, 0), (0, LANES - 2)))
    back = wlen - jnp.arange(wlen)
    bias = jnp.stack([jnp.where((back % d == 0) & (back // d <= DIL_SPAN), 0.0, NEG) for _, d in DIL_GROUPS])
    bias = jnp.pad(bias, ((0, 8 - len(DIL_GROUPS)), (0, 0))).astype(f32)
    o, new_buf = pl.pallas_call(
        _dil_step_kernel,
        grid=(db, 2),
        in_specs=[pl.BlockSpec((1, DIL_HEADS, DIL_W), lambda b, k: (b, 0, 0)),
                  pl.BlockSpec((1, 1, DIL_W, wlen), lambda b, k: (b, k, 0, 0)),
                  pl.BlockSpec((1, DIL_W, LANES), lambda b, k: (b, 0, 0)),
                  pl.BlockSpec((1, 8, DIL_W), lambda b, k: (b, 0, 0)),
                  pl.BlockSpec((8, wlen), lambda b, k: (0, 0))],
        out_specs=[pl.BlockSpec((1, 8, DIL_W), lambda b, k: (b, 0, 0)),
                   pl.BlockSpec((1, 1, DIL_W, wlen), lambda b, k: (b, k, 0, 0))],
        out_shape=[jax.ShapeDtypeStruct((db, 8, DIL_W), f32),
                   jax.ShapeDtypeStruct((db, 2, DIL_W, wlen), f32)],
        scratch_shapes=[pltpu.VMEM((DIL_HEADS, wlen), f32), pltpu.VMEM((DIL_HEADS, 1), f32),
                        pltpu.VMEM((DIL_HEADS, 1), f32)],
        compiler_params=pltpu.CompilerParams(dimension_semantics=("arbitrary", "arbitrary"),
                                             vmem_limit_bytes=VMEM_LIMIT_BYTES),
        name="dil_sample",
    )(q16, buft, newt, newr, bias)
    new_buf = jnp.transpose(new_buf.reshape(db, 2, DIL_HEADS, HEAD_DIM, wlen), (0, 4, 1, 2, 3))
    return o[:, 0], new_buf


def _dil_band_kernel(q_ref, kp_ref, kc_ref, vp_ref, vc_ref, num_ref, st_ref):
    f32, bf16 = jnp.float32, jnp.bfloat16
    blk = DIL_BLOCK
    n = pl.program_id(0)
    i = lax.broadcasted_iota(jnp.int32, (blk, 2 * blk), 0)
    j = lax.broadcasted_iota(jnp.int32, (blk, 2 * blk), 1) - blk
    ok = (i - j >= 0) & (i - j <= DIL_SPAN) & (n * blk + j >= 0)
    bias = jnp.where(ok, 0.0, NEG)
    bias = jnp.concatenate([bias, bias], axis=0)
    lane = lax.broadcasted_iota(jnp.int32, (blk, LANES), 1)
    first = lane < HEAD_DIM
    stats = jnp.zeros((blk, LANES), f32)
    for p in range(DIL_HEADS // 2):
        cols = slice(p * LANES, (p + 1) * LANES)
        qp = q_ref[:, cols] * (HEAD_DIM ** -0.5)
        qst = jnp.concatenate([jnp.where(first, qp, 0.0), jnp.where(first, 0.0, qp)], axis=0).astype(bf16)
        kk = jnp.concatenate([kp_ref[:, cols], kc_ref[:, cols]], axis=0).astype(bf16)
        vv = jnp.concatenate([vp_ref[:, cols], vc_ref[:, cols]], axis=0).astype(bf16)
        s = _nt_dot(qst, kk) + bias
        m = jnp.max(s, axis=1, keepdims=True)
        e = jnp.exp(s - m)
        den = jnp.sum(e, axis=1, keepdims=True)
        nm = jnp.dot(e.astype(bf16), vv, preferred_element_type=f32)
        num_ref[:, cols] = jnp.where(first, nm[:blk], nm[blk:])
        for a in range(2):
            h = 2 * p + a
            stats = jnp.where(lane == h, m[a * blk:(a + 1) * blk], stats)
            stats = jnp.where(lane == DIL_HEADS + h, den[a * blk:(a + 1) * blk], stats)
    st_ref[...] = stats


GDN_PREP_ROWS = 512
GDN_TILE_CHUNKS = 4
GDN_PAIRS = GDN_HEADS // 2
GDN_A_LANE = 3 * NSA_HEADS
GDN_B_LANE = GDN_A_LANE + GDN_HEADS


def _hi_lo(x):
    hi = x.astype(jnp.bfloat16)
    return hi, (x - hi.astype(jnp.float32)).astype(jnp.bfloat16)


def _three_way(x):
    f32 = jnp.float32
    x1 = x.astype(jnp.bfloat16)
    r1 = x - x1.astype(f32)
    x2 = r1.astype(jnp.bfloat16)
    return x1, x2, (r1 - x2.astype(f32)).astype(jnp.bfloat16)


def _dot_select(x, sel):
    return sum(jnp.dot(piece, sel, preferred_element_type=jnp.float32) for piece in _three_way(x))


def _select_dot(sel, x):
    return sum(jnp.dot(sel, piece, preferred_element_type=jnp.float32) for piece in _three_way(x))


def _dot_hl(a, b):
    f32 = jnp.float32
    ah, al = _hi_lo(a)
    bh, bl = _hi_lo(b)
    return (jnp.dot(ah, bh, preferred_element_type=f32) + jnp.dot(ah, bl, preferred_element_type=f32)
            + jnp.dot(al, bh, preferred_element_type=f32))


def _gdn_prep_kernel(u_ref, sm_ref, cw_ref, prm_ref, ea_ref, eb_ref, eh_ref, q_ref, k_ref, v_ref, g_ref, b_ref,
                     carry_ref):
    f32 = jnp.float32
    i = pl.program_id(0)
    tl = u_ref.shape[0]

    @pl.when(i == 0)
    def _():
        carry_ref[...] = jnp.zeros(carry_ref.shape, f32)

    u = u_ref[...]
    prev = carry_ref[...]
    row = lax.broadcasted_iota(jnp.int32, u.shape, 0)

    def shifted(k):
        r = pltpu.roll(u, k, 0)
        for j in range(k):
            r = jnp.where(row == j, prev[8 - k + j:8 - k + j + 1], r)
        return r

    cw = cw_ref[...]
    c = cw[0:1] * shifted(3) + cw[1:2] * shifted(2) + cw[2:3] * shifted(1) + cw[3:4] * u
    carry_ref[...] = u[tl - 8:]
    c = c * jax.nn.sigmoid(c)
    eh = eh_ref[...]

    def l2n(x):
        return x * lax.rsqrt(_dot_select(x * x, eh) + NORM_EPS)

    q_ref[...] = l2n(c[:, :GDN_W]) * (HEAD_DIM ** -0.5)
    k_ref[...] = l2n(c[:, GDN_W:2 * GDN_W])
    v_ref[...] = c[:, 2 * GDN_W:]
    sm = sm_ref[...]
    x = sm + prm_ref[1:2]
    softplus = jnp.maximum(x, 0.0) + jnp.log(1.0 + jnp.exp(-jnp.abs(x)))
    g_ref[...] = _dot_select(-jnp.exp(prm_ref[0:1]) * softplus, ea_ref[...])
    b_ref[...] = _dot_select(jax.nn.sigmoid(sm), eb_ref[...])


def _gdn_chunk_kernel(q_ref, k_ref, v_ref, g_ref, b_ref, z_ref, nw_ref, lt_ref, eh_ref, o_ref, s_out_ref, s_ref):
    f32 = jnp.float32
    ch = GDN_CHUNK
    i = pl.program_id(0)

    @pl.when(i == 0)
    def _():
        s_ref[...] = jnp.zeros(s_ref.shape, f32)

    lane = lax.broadcasted_iota(jnp.int32, (ch, LANES), 1)
    first = lane < HEAD_DIM
    stack = lambda x: jnp.concatenate([jnp.where(first, x, 0.0), jnp.where(first, 0.0, x)], axis=0)
    r2 = lax.broadcasted_iota(jnp.int32, (2 * ch, 2 * ch), 0)
    c2 = lax.broadcasted_iota(jnp.int32, (2 * ch, 2 * ch), 1)
    same = (r2 // ch) == (c2 // ch)
    tri = same & (r2 % ch >= c2 % ch)
    strict = same & (r2 % ch > c2 % ch)
    eye = r2 == c2
    eye_f = jnp.where(eye, 1.0, 0.0)
    diag2 = lax.broadcasted_iota(jnp.int32, (ch, LANES), 0) == lane % HEAD_DIM
    lt = lt_ref[...]
    bf = lambda x: x.astype(jnp.bfloat16)
    dot = lambda a, b: jnp.dot(bf(a), bf(b), preferred_element_type=f32)

    blocks = [(c, p) for c in range(GDN_TILE_CHUNKS) for p in range(GDN_PAIRS)]
    ld = lambda ref, c, p: ref[c * ch:(c + 1) * ch, p * LANES:(p + 1) * LANES]
    gcs = [_select_dot(lt, ld(g_ref, c, p)) for c, p in blocks]
    amats, qks, rhs_u, rhs_w, qgs, kds, decs = [], [], [], [], [], [], []
    for (c, p), gc in zip(blocks, gcs):
        kk, qq, vv, bb = ld(k_ref, c, p), ld(q_ref, c, p), ld(v_ref, c, p), ld(b_ref, c, p)
        eg = jnp.exp(gc)
        g_end = gc[ch - 1:ch]
        kb = kk * bb
        col = jnp.concatenate([jnp.broadcast_to(gc[:, 0:1], (ch, LANES)),
                               jnp.broadcast_to(gc[:, HEAD_DIM:HEAD_DIM + 1], (ch, LANES))], axis=0)
        rowv = jnp.sum(jnp.where(diag2, gc, 0.0), axis=0, keepdims=True)
        gam = jnp.where(tri, jnp.exp(jnp.where(tri, col - rowv, 0.0)), 0.0)
        kst = stack(kk)
        amats.append(jnp.where(strict, _nt_dot(bf(stack(kb)), bf(kst)) * gam, 0.0))
        qks.append(jnp.where(tri, _nt_dot(bf(stack(qq)), bf(kst)) * gam, 0.0))
        rhs_u.append(stack(vv * bb))
        rhs_w.append(stack(kb * eg))
        qgs.append(stack(qq * eg))
        kds.append(stack(kk * jnp.exp(g_end - gc)))
        decs.append(jnp.sum(jnp.where(eye, jnp.exp(g_end), 0.0), axis=1, keepdims=True))
    xs = [eye_f - a for a in amats]
    pws = [_dot_hl(a, a) for a in amats]
    steps = GDN_CHUNK.bit_length() - 2
    for r in range(steps):
        xs = [x + _dot_hl(x, pw) for x, pw in zip(xs, pws)]
        if r < steps - 1:
            pws = [_dot_hl(pw, pw) for pw in pws]
    uus = [dot(x, u) for x, u in zip(xs, rhs_u)]
    wws = [dot(x, w) for x, w in zip(xs, rhs_w)]
    kdts = [kd.T for kd in kds]
    states = [s_ref[p] for p in range(GDN_PAIRS)]
    for c in range(GDN_TILE_CHUNKS):
        rs = slice(c * ch, (c + 1) * ch)
        outs = []
        for p in range(GDN_PAIRS):
            n = c * GDN_PAIRS + p
            s = states[p]
            v_new = uus[n] - dot(wws[n], s)
            o_st = dot(qgs[n], s) + dot(qks[n], v_new)
            states[p] = s * decs[n] + dot(kdts[n], v_new)
            outs.append(o_st[:ch] + o_st[ch:])
        o = jnp.concatenate(outs, axis=1)
        ms = _dot_select(o * o, eh_ref[...]) * (1.0 / HEAD_DIM)
        z = z_ref[rs, :]
        o_ref[rs, :] = o * lax.rsqrt(ms + NORM_EPS) * nw_ref[...] * (z * jax.nn.sigmoid(z))
    for p in range(GDN_PAIRS):
        s_ref[p] = states[p]

    @pl.when(i == pl.num_programs(0) - 1)
    def _():
        s_out_ref[...] = s_ref[...]


def _gdn_prompt(qkv, small, z, conv_w, a_log, dt_bias, norm_w):
    l = qkv.shape[0]
    f32, bf16 = jnp.float32, jnp.bfloat16
    w = GDN_W
    hh = jnp.arange(w) // HEAD_DIM
    expander = lambda base: (jnp.arange(LANES)[:, None] == base + hh[None, :]).astype(bf16)
    eh = (hh[:, None] == hh[None, :]).astype(bf16)
    cw8 = jnp.zeros((8, 3 * w), f32).at[:GDN_CONV].set(conv_w)
    prm = jnp.zeros((8, LANES), f32)
    prm = prm.at[0, GDN_A_LANE:GDN_A_LANE + GDN_HEADS].set(a_log).at[1, GDN_A_LANE:GDN_A_LANE + GDN_HEADS].set(dt_bias)
    tl = GDN_PREP_ROWS
    row = lambda wd: pl.BlockSpec((tl, wd), lambda i: (i, 0))
    const = lambda a: pl.BlockSpec(a.shape, lambda i: (0,) * a.ndim)
    ea, eb = expander(GDN_A_LANE), expander(GDN_B_LANE)
    q, k, v, g, b = pl.pallas_call(
        _gdn_prep_kernel,
        grid=(l // tl,),
        in_specs=[row(3 * w), row(LANES), const(cw8), const(prm), const(ea), const(eb), const(eh)],
        out_specs=[row(w)] * 5,
        out_shape=[jax.ShapeDtypeStruct((l, w), f32)] * 5,
        scratch_shapes=[pltpu.VMEM((8, 3 * w), f32)],
        compiler_params=pltpu.CompilerParams(dimension_semantics=("arbitrary",), vmem_limit_bytes=VMEM_LIMIT_BYTES),
        name="gdn_prep",
    )(qkv, small, cw8, prm, ea, eb, eh)
    tc = GDN_TILE_CHUNKS * GDN_CHUNK
    lt = (jnp.arange(GDN_CHUNK)[:, None] >= jnp.arange(GDN_CHUNK)[None, :]).astype(bf16)
    nw = jnp.tile(norm_w, GDN_HEADS).reshape(1, w)
    rowc = pl.BlockSpec((tc, w), lambda i: (i, 0))
    o, s_bd = pl.pallas_call(
        _gdn_chunk_kernel,
        grid=(l // tc,),
        in_specs=[rowc] * 6 + [const(nw), const(lt), const(eh)],
        out_specs=[rowc, pl.BlockSpec((GDN_PAIRS, LANES, LANES), lambda i: (0, 0, 0))],
        out_shape=[jax.ShapeDtypeStruct((l, w), f32), jax.ShapeDtypeStruct((GDN_PAIRS, LANES, LANES), f32)],
        scratch_shapes=[pltpu.VMEM((GDN_PAIRS, LANES, LANES), f32)],
        compiler_params=pltpu.CompilerParams(dimension_semantics=("arbitrary",), vmem_limit_bytes=VMEM_LIMIT_BYTES),
        name="gdn_chunk",
    )(q, k, v, g, b, z, nw, lt, eh)
    s4 = s_bd.reshape(GDN_PAIRS, 2, HEAD_DIM, 2, HEAD_DIM)
    s_fin = jnp.stack([s4[:, 0, :, 0], s4[:, 1, :, 1]], axis=1).reshape(GDN_HEADS, HEAD_DIM, HEAD_DIM)
    return o, s_fin


def _proj_dil_kernel(x_ref, w_ref, cos_ref, sin_ref, bf_ref, kv_ref):
    acc = jnp.dot(x_ref[...].astype(jnp.bfloat16), w_ref[...], preferred_element_type=jnp.float32)
    reps = 2 * DIL_W // LANES
    qk = _rope_lanes(acc[:, :2 * DIL_W], _lane_tile(cos_ref[...], reps), _lane_tile(sin_ref[...], reps))
    v = acc[:, 2 * DIL_W:]
    bf_ref[...] = jnp.concatenate([qk, v], axis=1).astype(jnp.bfloat16)
    kv_ref[...] = jnp.concatenate([qk[:, DIL_W:], v], axis=1)


def _proj_dil(x, w_in):
    l, d = x.shape
    tm = _row_tile(l)
    cos, sin = _rope_tables(jnp.arange(l), LANES)
    return pl.pallas_call(
        _proj_dil_kernel,
        grid=(l // tm,),
        in_specs=[pl.BlockSpec((tm, d), lambda i: (i, 0)),
                  pl.BlockSpec((d, 3 * DIL_W), lambda i: (0, 0)),
                  pl.BlockSpec((tm, LANES), lambda i: (i, 0)),
                  pl.BlockSpec((tm, LANES), lambda i: (i, 0))],
        out_specs=[pl.BlockSpec((tm, 3 * DIL_W), lambda i: (i, 0)),
                   pl.BlockSpec((tm, 2 * DIL_W), lambda i: (i, 0))],
        out_shape=[jax.ShapeDtypeStruct((l, 3 * DIL_W), jnp.bfloat16),
                   jax.ShapeDtypeStruct((l, 2 * DIL_W), jnp.float32)],
        compiler_params=pltpu.CompilerParams(dimension_semantics=("arbitrary",), vmem_limit_bytes=VMEM_LIMIT_BYTES),
        name="proj_dil",
    )(x, w_in.astype(jnp.bfloat16), cos, sin)


def _dil_band_stats(qkv, d):
    l = qkv.shape[0]
    assert l % (d * DIL_BLOCK) == 0
    nb = l // (d * DIL_BLOCK)
    view = qkv.reshape(l // d, d * 3 * DIL_W)

    def part(which, prev):
        return pl.BlockSpec((DIL_BLOCK, DIL_W), (lambda n, r: (jnp.maximum(n - 1, 0), 3 * r + which)) if prev
                            else (lambda n, r: (n, 3 * r + which)))

    out = lambda w: pl.BlockSpec((DIL_BLOCK, w), lambda n, r: (n, r))
    num, st = pl.pallas_call(
        _dil_band_kernel,
        grid=(nb, d),
        in_specs=[part(0, False), part(1, True), part(1, False), part(2, True), part(2, False)],
        out_specs=[out(DIL_W), out(LANES)],
        out_shape=[jax.ShapeDtypeStruct((l // d, d * DIL_W), jnp.float32),
                   jax.ShapeDtypeStruct((l // d, d * LANES), jnp.float32)],
        compiler_params=pltpu.CompilerParams(dimension_semantics=("arbitrary", "arbitrary"),
                                             vmem_limit_bytes=VMEM_LIMIT_BYTES),
        name="dil_band_stats",
    )(view, view, view, view, view)
    return num.reshape(l, DIL_W), st.reshape(l, LANES)


def _dil_merge_kernel(*refs):
    f32 = jnp.float32
    ng = len(DIL_GROUPS)
    nums, sts = refs[:ng], refs[ng:2 * ng]
    ex_ref, w_ref, res_ref, g_ref, b_ref, o_ref = refs[2 * ng:]
    st = [r[...] for r in sts]
    m_all = functools.reduce(jnp.maximum, st)
    ws = [jnp.exp(s - m_all) for s in st]
    den = sum(w * pltpu.roll(s, LANES - DIL_HEADS, 1) for w, s in zip(ws, st))
    head_lane = lax.broadcasted_iota(jnp.int32, den.shape, 1) < DIL_HEADS
    o = sum(_dot_select(jnp.where(head_lane, w / den, 0.0), ex_ref[...]) * n[...]
            for w, n in zip(ws, nums))
    acc = jnp.dot(o.astype(jnp.bfloat16), w_ref[...], preferred_element_type=f32)
    o_ref[...] = _layer_norm_rows(DEEPNORM_ALPHA * res_ref[...] + acc, g_ref[...], b_ref[...])


def _dil_merge_proj(nums, sts, w_out, res, g, b):
    l, n = res.shape
    tm = _row_tile(l)
    expand = (jnp.arange(LANES)[:, None] == jnp.arange(DIL_W)[None, :] // HEAD_DIM).astype(jnp.bfloat16)
    row = lambda w: pl.BlockSpec((tm, w), lambda i: (i, 0))
    const = lambda a: pl.BlockSpec(a.shape, lambda i: (0,) * a.ndim)
    wb = w_out.astype(jnp.bfloat16)
    g2, b2 = g.reshape(1, n), b.reshape(1, n)
    return pl.pallas_call(
        _dil_merge_kernel,
        grid=(l // tm,),
        in_specs=[row(DIL_W)] * len(nums) + [row(LANES)] * len(sts) + [const(expand), const(wb), row(n),
                                                                       const(g2), const(b2)],
        out_specs=row(n),
        out_shape=jax.ShapeDtypeStruct((l, n), jnp.float32),
        compiler_params=pltpu.CompilerParams(dimension_semantics=("arbitrary",), vmem_limit_bytes=VMEM_LIMIT_BYTES),
        name="dil_merge_proj",
    )(*nums, *sts, expand, wb, res, g2, b2)


def _split_cols(h, widths):
    parts, start = [], 0
    for w in widths:
        parts.append(h[..., start:start + w])
        start += w
    return parts


def _even_widths():
    return (NSA_Q_W,) + (NSA_KV_W,) * 6 + (3 * NSA_HEADS, 3 * GDN_W, GDN_HEADS, GDN_HEADS, GDN_W)


def _rms_norm(x, w):
    return x * lax.rsqrt(jnp.mean(jnp.square(x), axis=-1, keepdims=True) + NORM_EPS) * w


def _l2_norm(x):
    return x * lax.rsqrt(jnp.sum(jnp.square(x), axis=-1, keepdims=True) + NORM_EPS)


def _rope(x, pos):
    half = HEAD_DIM // 2
    inv_freq = ROPE_THETA ** (-2.0 * jnp.arange(half, dtype=jnp.float32) / HEAD_DIM)
    ang = pos.astype(jnp.float32)[:, None] * inv_freq[None, :]
    cos, sin = jnp.cos(ang)[:, None, :], jnp.sin(ang)[:, None, :]
    xf = x.astype(jnp.float32)
    x1, x2 = xf[..., :half], xf[..., half:]
    return jnp.concatenate([x1 * cos - x2 * sin, x2 * cos + x1 * sin], axis=-1)


def _causal_dwconv(hist, u, w):
    width, s = w.shape[0], u.shape[1]
    ext = jnp.concatenate([hist.astype(u.dtype), u], axis=1)
    out = w[0] * ext[:, :s]
    for j in range(1, width):
        out = out + w[j] * ext[:, j:j + s]
    return out, ext[:, s:]


def _nsa_compress(rows, w1, b1, w2, pe):
    b, l, g, dh = rows.shape
    nc = l // CMP_BLOCK
    blk = rows[:, :nc * CMP_BLOCK].astype(jnp.float32).reshape(b, nc, CMP_BLOCK, g, dh) + pe[:, None, :]
    flat = blk.transpose(0, 1, 3, 2, 4).reshape(b, nc, g, CMP_BLOCK * dh)
    return jax.nn.gelu(flat @ w1 + b1) @ w2


def _nsa_compressed_kv(k_rows, v_rows, cw1, cb1, cw2, cpe):
    ck = _nsa_compress(k_rows, cw1[0], cb1[0], cw2[0], cpe[0])
    cv = _nsa_compress(v_rows, cw1[1], cb1[1], cw2[1], cpe[1])
    nc = ck.shape[1]
    ck = _rope(ck, (jnp.arange(nc) + 1) * CMP_BLOCK - 1)
    return ck, cv


def _rope_rows_kernel(*refs):
    cos_ref, sin_ref = refs[:2]
    n = (len(refs) - 2) // 2
    for x_ref, o_ref in zip(refs[2:2 + n], refs[2 + n:]):
        reps = x_ref.shape[1] // LANES
        o_ref[...] = _rope_lanes(x_ref[...], _lane_tile(cos_ref[...], reps), _lane_tile(sin_ref[...], reps))


def _rope_rows(xs):
    l = xs[0].shape[0]
    tm = _row_tile(l)
    cos, sin = _rope_tables(jnp.arange(l), LANES)
    row = lambda w: pl.BlockSpec((tm, w), lambda i: (i, 0))
    return pl.pallas_call(
        _rope_rows_kernel,
        grid=(l // tm,),
        in_specs=[row(LANES), row(LANES)] + [row(x.shape[1]) for x in xs],
        out_specs=[row(x.shape[1]) for x in xs],
        out_shape=[jax.ShapeDtypeStruct(x.shape, jnp.float32) for x in xs],
        compiler_params=pltpu.CompilerParams(dimension_semantics=("arbitrary",), vmem_limit_bytes=VMEM_LIMIT_BYTES),
        name="rope_rows",
    )(cos, sin, *xs)


def _nsa_prompt(q, kc, vc, ks, vs, kw, vw, small, cw1, cb1, cw2, cpe):
    b, l = q.shape[:2]
    flat = lambda t: t.reshape(t.shape[1], -1)
    qr, ksr, kwr = _rope_rows([flat(q), flat(ks), flat(kw)])
    ksr, kwr = ksr.reshape(ks.shape), kwr.reshape(kw.shape)
    ck, cv = _nsa_compressed_kv(kc, vc, cw1, cb1, cw2, cpe)
    vsf = vs.astype(jnp.float32)
    vwf = vw.astype(jnp.float32)
    o_nsa = _nsa_prompt_attention(qr, small, flat(ck), flat(cv), flat(ksr), flat(vsf), flat(kwr), flat(vwf))
    keep = min(NSA_WINDOW, l)
    rows_cmp = jnp.stack([kc, vc], axis=2)
    rows_slc = jnp.stack([ksr, vsf], axis=2)
    rows_win = jnp.stack([kwr[:, l - keep:], vwf[:, l - keep:]], axis=2)
    return o_nsa[None], rows_cmp, rows_slc, rows_win


def _nsa_sample(q, kc, vc, ks, vs, kw, vw, gate, cmp_pool, slc_pool, layer_idx, win_buf, page_table,
                cw1, cb1, cw2, cpe):
    db, s = q.shape[:2]
    past = page_table.shape[1] * PAGE_SIZE
    wb = win_buf.shape[1]
    assert s == 1 and wb == NSA_WINDOW and past >= wb and past % (STEP_PAGES * PAGE_SIZE) == 0
    qpos = past + jnp.arange(s)
    qr = _rope(q, qpos)
    ckt, cvt = _nsa_sample_compress(cmp_pool, layer_idx, page_table, cw1, cb1, cw2, cpe)
    ksr = _rope(ks, qpos)
    vsf = vs.astype(jnp.float32)
    kwr = _rope(kw, qpos)
    vwf = vw.astype(jnp.float32)
    newrows = jnp.stack([t.reshape(db, NSA_KV_W) for t in (ksr, vsf, kwr, vwf)], axis=1)
    o_nsa, rows_win = _nsa_sample_attention(qr[:, 0], ckt, cvt, slc_pool, layer_idx, page_table, win_buf,
                                              newrows, gate.reshape(db, -1))
    rows_cmp = jnp.stack([kc, vc], axis=2)
    rows_slc = jnp.stack([ksr, vsf], axis=2)
    return o_nsa[:, None], rows_cmp, rows_slc, rows_win


def _gdn_recurrent(q, k, v, g, beta, s0):
    def step(state, xs):
        q_t, k_t, v_t, g_t, b_t = xs
        state = state * jnp.exp(g_t)[..., None, None]
        v_t = (v_t - jnp.einsum('bhk,bhkv->bhv', k_t, state)) * b_t[..., None]
        state = state + jnp.einsum('bhk,bhv->bhkv', k_t, v_t)
        return state, jnp.einsum('bhk,bhkv->bhv', q_t, state)

    xs = tuple(jnp.moveaxis(a, 1, 0) for a in (q, k, v, g, beta))
    s_fin, o = lax.scan(step, s0, xs)
    return jnp.moveaxis(o, 0, 1), s_fin


def _gdn_step(qkv, a, bt, z, conv_hist, s0, conv_w, a_log, dt_bias, norm_w):
    b, s = qkv.shape[:2]
    c, new_hist = _causal_dwconv(conv_hist, qkv, conv_w)
    c = jax.nn.silu(c.astype(jnp.float32))
    q, k, v = [t.reshape(b, s, GDN_HEADS, HEAD_DIM) for t in jnp.split(c, 3, axis=-1)]
    q = _l2_norm(q) * HEAD_DIM ** -0.5
    k = _l2_norm(k)
    beta = jax.nn.sigmoid(bt.astype(jnp.float32))
    g = -jnp.exp(a_log) * jax.nn.softplus(a.astype(jnp.float32) + dt_bias)
    o, s_fin = _gdn_recurrent(q, k, v, g, beta, s0.astype(jnp.float32))
    o = _rms_norm(o, norm_w) * jax.nn.silu(z.astype(jnp.float32).reshape(b, s, GDN_HEADS, HEAD_DIM))
    return o.reshape(b, s, GDN_W), new_hist, s_fin


def _proj(x, w):
    b, s, d = x.shape
    n = w.shape[1]
    npad = -(-n // LANES) * LANES
    wp = jnp.pad(w, ((0, 0), (0, npad - n)))
    return _matmul(x.reshape(b * s, d), wp).reshape(b, s, npad)


def _even_prompt(x, w_in, cw1, cb1, cw2, cpe, conv_w, a_log, dt_bias, norm_w):
    b, l, _ = x.shape
    q, kc, vc, ks, vs, kw, vw, gate, qkv, a, bt, z = _split_cols(_proj(x, w_in), _even_widths())
    heads = lambda t: t.reshape(b, l, -1, HEAD_DIM)
    assert b == 1
    small = jnp.concatenate([gate, a, bt], axis=-1).reshape(l, -1)
    small = jnp.pad(small, ((0, 0), (0, LANES - small.shape[-1])))
    o_nsa, r_cmp, r_slc, r_win = _nsa_prompt(
        heads(q), heads(kc), heads(vc), heads(ks), heads(vs), heads(kw), heads(vw), small, cw1, cb1, cw2, cpe)
    o_gdn, s_fin = _gdn_prompt(qkv[0], small, z[0], conv_w, a_log, dt_bias, norm_w)
    conv_hist = qkv[:, l - (GDN_CONV - 1):]
    return [o_nsa[0], o_gdn], r_cmp, r_slc, r_win, conv_hist, s_fin[None]


def _even_sample(x, cmp_pool, slc_pool, layer_idx, win_buf, conv_hist, s0, page_table,
                 w_in, cw1, cb1, cw2, cpe, conv_w, a_log, dt_bias, norm_w):
    b, s, _ = x.shape
    q, kc, vc, ks, vs, kw, vw, gate, qkv, a, bt, z = _split_cols(_proj(x, w_in), _even_widths())
    heads = lambda t: t.reshape(b, s, -1, HEAD_DIM)
    o_nsa, r_cmp, r_slc, r_win = _nsa_sample(
        heads(q), heads(kc), heads(vc), heads(ks), heads(vs), heads(kw), heads(vw), gate,
        cmp_pool, slc_pool, layer_idx, win_buf, page_table, cw1, cb1, cw2, cpe)
    o_gdn, new_hist, s_fin = _gdn_step(qkv, a, bt, z, conv_hist, s0, conv_w, a_log, dt_bias, norm_w)
    return [o_nsa[:, 0], o_gdn[:, 0]], r_cmp, r_slc, r_win, new_hist, s_fin


def _dil_prompt(x, w_in, w_out, g, b):
    bsz, l, _ = x.shape
    assert bsz == 1
    qkv, kv = _proj_dil(x[0], w_in)
    stats = [_dil_band_stats(qkv, d) for _, d in DIL_GROUPS]
    y = _dil_merge_proj([n for n, _ in stats], [s for _, s in stats], w_out, x[0], g, b)
    keep = min(DIL_MAX_WINDOW, l)
    buf = kv[l - keep:].reshape(1, keep, 2, DIL_HEADS, HEAD_DIM)
    return y, buf


def _dil_sample(x, buf, past, w_in):
    db, s, _ = x.shape
    q, k, v = [t.reshape(db, s, DIL_HEADS, HEAD_DIM) for t in jnp.split(_proj(x, w_in), 3, axis=-1)]
    assert s == 1 and buf.shape[1] == DIL_MAX_WINDOW <= past
    qpos = past + jnp.arange(s)
    qr, kr = _rope(q, qpos), _rope(k, qpos)
    o, new_buf = _dil_sample_attention(qr[:, 0], kr.reshape(db, DIL_W), v.reshape(db, DIL_W).astype(jnp.float32), buf)
    return o[:, None], new_buf


def kernel(x_prompt, x_sample, cache_nsa_cmp_kv, cache_nsa_slc_kv, state_nsa_win_kv, state_gdn_conv,
           state_gdn_S, state_dil_kv, state_ffn_conv, page_table, w_in_a, nsa_cmp_w1, nsa_cmp_b1, nsa_cmp_w2,
           nsa_cmp_pe, gdn_conv_w, gdn_A_log, gdn_dt_bias, gdn_norm_w, w_out_a, w_in_c, w_out_c,
           ln_mix_g, ln_mix_b, ffn_w_in, ffn_conv_w, ffn_conv_b, ffn_w_out, ln_ffn_g, ln_ffn_b):
    past = page_table.shape[1] * PAGE_SIZE
    bp, lp, d = x_prompt.shape
    bs, ls, _ = x_sample.shape
    assert bp == 1 and ls == 1
    xp, xs = x_prompt, x_sample
    cmp_p, cmp_s, slc_p, slc_s, win_p, win_s = [], [], [], [], [], []
    gconv_p, gconv_s, gstate_p, gstate_s = [], [], [], []
    dil_p, dil_s, ffn_p, ffn_s = [], [], [], []
    for layer in range(DEPTH):
        if layer % 2 == 0:
            la = layer // 2
            wa = (w_in_a[la], nsa_cmp_w1[la], nsa_cmp_b1[la], nsa_cmp_w2[la], nsa_cmp_pe[la],
                  gdn_conv_w[la], gdn_A_log[la], gdn_dt_bias[la], gdn_norm_w[la])
            mp, rc, rs, rw, hc, hs_ = _even_prompt(xp, *wa)
            cmp_p.append(rc); slc_p.append(rs); win_p.append(rw); gconv_p.append(hc); gstate_p.append(hs_)
            ms, rc, rs, rw, hc, hs_ = _even_sample(xs, cache_nsa_cmp_kv, cache_nsa_slc_kv, la,
                                                   state_nsa_win_kv[:, la], state_gdn_conv[:, la],
                                                   state_gdn_S[:, la], page_table, *wa)
            cmp_s.append(rc); slc_s.append(rs); win_s.append(rw); gconv_s.append(hc); gstate_s.append(hs_)
            w_out = w_out_a[la]
        else:
            lc = layer // 2
            xp2, bpf = _dil_prompt(xp, w_in_c[lc], w_out_c[lc], ln_mix_g[layer], ln_mix_b[layer])
            o_dil, bsf = _dil_sample(xs, state_dil_kv[:, lc], past, w_in_c[lc])
            ms = [o_dil.reshape(bs, -1)]
            dil_p.append(bpf); dil_s.append(bsf)
            w_out = w_out_c[lc]
        if layer % 2 == 0:
            xp2 = _matmul_ln(mp, w_out, xp.reshape(lp, d), ln_mix_g[layer], ln_mix_b[layer])
        xs2 = _matmul_ln(ms, w_out, xs.reshape(bs, d), ln_mix_g[layer], ln_mix_b[layer])
        fargs = (ffn_w_in[layer], ffn_conv_w[layer], ffn_conv_b[layer], ffn_w_out[layer],
                 ln_ffn_g[layer], ln_ffn_b[layer])
        xp3, hp = _ffn_seq(xp2, *fargs)
        xs3, hs = _ffn_step(xs2, state_ffn_conv[:, layer], *fargs)
        xp, xs = xp3.reshape(1, lp, d), xs3.reshape(bs, 1, d)
        ffn_p.append(hp[None]); ffn_s.append(hs)

    def stk(lst):
        return jnp.stack(lst, axis=1)

    return (xp, xs, stk(cmp_p), stk(cmp_s), stk(slc_p), stk(slc_s), stk(win_p), stk(win_s),
            stk(gconv_p), stk(gconv_s), stk(gstate_p), stk(gstate_s), stk(dil_p), stk(dil_s),
            stk(ffn_p), stk(ffn_s))
```

```python
import functools

import jax
import jax.numpy as jnp
from jax import lax
from jax.experimental import pallas as pl
from jax.experimental.pallas import tpu as pltpu
import numpy as np

DEPTH = 2
PAGE_SIZE = 128
HEAD_DIM = 64
ROPE_THETA = 10000.0
NSA_HEADS = 8
NSA_KV_HEADS = 2
NSA_GROUP = NSA_HEADS // NSA_KV_HEADS
CMP_BLOCK = 32
SEL_BLOCK = 64
NSA_TOPN = 16
NSA_WINDOW = 512
NSA_QBLOCK = 128
NSA_FORCE = 1.0e4
GDN_HEADS = 8
GDN_CONV = 4
GDN_CHUNK = 64
DIL_HEADS = 16
DIL_GROUPS = ((128, 1), (512, 4), (2048, 16))
DIL_SPAN = 128
DIL_BLOCK = 128
DIL_MAX_WINDOW = 2048
D_FF = 2816
FFN_CONV = 3
DEEPNORM_ALPHA = (2.0 * DEPTH) ** 0.25
LN_EPS = 1e-5
NORM_EPS = 1e-6
NSA_Q_W = NSA_HEADS * HEAD_DIM
NSA_KV_W = NSA_KV_HEADS * HEAD_DIM
GDN_W = GDN_HEADS * HEAD_DIM
DIL_W = DIL_HEADS * HEAD_DIM

LANES = 128
VMEM_LIMIT_BYTES = 56 * 1024 * 1024


def _layer_norm_rows(r, g, b):
    mu = jnp.mean(r, axis=-1, keepdims=True)
    d = r - mu
    var = jnp.mean(d * d, axis=-1, keepdims=True)
    return d * lax.rsqrt(var + LN_EPS) * g + b


def _mm_kernel(x_ref, w_ref, o_ref):
    o_ref[...] = jnp.dot(x_ref[...].astype(jnp.bfloat16), w_ref[...], preferred_element_type=jnp.float32)


def _mm_ln_kernel(*refs):
    n = (len(refs) - 4) // 2
    res_ref, g_ref, b_ref, o_ref = refs[2 * n:]
    acc = sum(jnp.dot(x[...].astype(jnp.bfloat16), w[...], preferred_element_type=jnp.float32)
              for x, w in zip(refs[:n], refs[n:2 * n]))
    o_ref[...] = _layer_norm_rows(DEEPNORM_ALPHA * res_ref[...] + acc, g_ref[...], b_ref[...])


def _row_tile(m):
    return 512 if m % 512 == 0 else m


def _matmul(x, w):
    m, k = x.shape
    n = w.shape[1]
    tm = _row_tile(m)
    tn = n
    return pl.pallas_call(
        _mm_kernel,
        grid=(m // tm, n // tn),
        in_specs=[pl.BlockSpec((tm, k), lambda i, j: (i, 0)),
                  pl.BlockSpec((k, tn), lambda i, j: (0, j))],
        out_specs=pl.BlockSpec((tm, tn), lambda i, j: (i, j)),
        out_shape=jax.ShapeDtypeStruct((m, n), jnp.float32),
        compiler_params=pltpu.CompilerParams(dimension_semantics=("parallel", "arbitrary"),
                                             vmem_limit_bytes=VMEM_LIMIT_BYTES),
        name="matmul",
    )(x, w.astype(jnp.bfloat16))


def _matmul_ln(xs, w, res, g, b):
    m, n = res.shape
    tm = _row_tile(m)
    wb = w.astype(jnp.bfloat16)
    ws, start = [], 0
    for x in xs:
        ws.append(wb[start:start + x.shape[1]])
        start += x.shape[1]
    assert start == w.shape[0]
    row = lambda a: pl.BlockSpec((tm, a.shape[1]), lambda i: (i, 0))
    const = lambda a: pl.BlockSpec(a.shape, lambda i: (0, 0))
    g2, b2 = g.reshape(1, n), b.reshape(1, n)
    return pl.pallas_call(
        _mm_ln_kernel,
        grid=(m // tm,),
        in_specs=[row(x) for x in xs] + [const(wi) for wi in ws] + [row(res), const(g2), const(b2)],
        out_specs=row(res),
        out_shape=jax.ShapeDtypeStruct((m, n), jnp.float32),
        compiler_params=pltpu.CompilerParams(dimension_semantics=("arbitrary",),
                                             vmem_limit_bytes=VMEM_LIMIT_BYTES),
        name="matmul_ln",
    )(*xs, *ws, res, g2, b2)


FFN_CHUNK = D_FF // 2
FFN_NCHUNK = D_FF // FFN_CHUNK


def _ffn_seq_kernel(x_ref, wi_ref, cw_ref, cb_ref, wo_ref, lg_ref, lb_ref, y_ref, hist_ref, carry_ref):
    f32, bf16 = jnp.float32, jnp.bfloat16
    i = pl.program_id(0)
    tm = x_ref.shape[0]
    x = x_ref[...]
    xb = x.astype(bf16)

    @pl.when(i == 0)
    def _():
        carry_ref[...] = jnp.zeros(carry_ref.shape, f32)

    row = lax.broadcasted_iota(jnp.int32, (8, FFN_CHUNK), 0)

    def conv_half(cols):
        u = jnp.dot(xb, wi_ref[:, cols], preferred_element_type=f32)
        prev = carry_ref[:, cols]
        p2, p1 = prev[6:7], prev[7:8]
        r1, r2 = pltpu.roll(u, 1, 0), pltpu.roll(u, 2, 0)
        u1 = jnp.concatenate([jnp.where(row == 0, p1, r1[:8]), r1[8:]], axis=0)
        u2 = jnp.concatenate([jnp.where(row == 0, p2, jnp.where(row == 1, p1, r2[:8])), r2[8:]], axis=0)
        carry_ref[:, cols] = u[tm - 8:]
        hist_ref[:, cols] = u[tm - 8:]
        cw = cw_ref[:, cols]
        return cw[0:1] * u2 + cw[1:2] * u1 + cw[2:3] * u + cb_ref[:, cols]

    acc = None
    for j in range(FFN_NCHUNK):
        a = conv_half(slice(j * FFN_CHUNK, (j + 1) * FFN_CHUNK))
        g = conv_half(slice(D_FF + j * FFN_CHUNK, D_FF + (j + 1) * FFN_CHUNK))
        h = (a * jax.nn.sigmoid(a) * g).astype(bf16)
        part = jnp.dot(h, wo_ref[j * FFN_CHUNK:(j + 1) * FFN_CHUNK, :], preferred_element_type=f32)
        acc = part if acc is None else acc + part
    y_ref[...] = _layer_norm_rows(DEEPNORM_ALPHA * x + acc, lg_ref[...], lb_ref[...])


def _ffn_seq(x, w_in, conv_w, conv_b, w_out, ln_g, ln_b):
    l, d = x.shape
    tm = _row_tile(l)
    cw8 = jnp.zeros((8, 2 * D_FF), jnp.float32).at[:FFN_CONV].set(conv_w)
    cb = conv_b.reshape(1, 2 * D_FF)
    const = lambda a: pl.BlockSpec(a.shape, lambda i: (0,) * a.ndim, pipeline_mode=pl.Buffered(1))
    wi, wo = w_in.astype(jnp.bfloat16), w_out.astype(jnp.bfloat16)
    g2, b2 = ln_g.reshape(1, d), ln_b.reshape(1, d)
    y, hist = pl.pallas_call(
        _ffn_seq_kernel,
        grid=(l // tm,),
        in_specs=[pl.BlockSpec((tm, d), lambda i: (i, 0)), const(wi), const(cw8), const(cb), const(wo),
                  const(g2), const(b2)],
        out_specs=[pl.BlockSpec((tm, d), lambda i: (i, 0)), pl.BlockSpec((8, 2 * D_FF), lambda i: (i, 0))],
        out_shape=[jax.ShapeDtypeStruct((l, d), jnp.float32),
                   jax.ShapeDtypeStruct((l // tm * 8, 2 * D_FF), jnp.float32)],
        scratch_shapes=[pltpu.VMEM((8, 2 * D_FF), jnp.float32)],
        compiler_params=pltpu.CompilerParams(dimension_semantics=("arbitrary",), vmem_limit_bytes=VMEM_LIMIT_BYTES),
        name="ffn_seq",
    )(x, wi, cw8, cb, wo, g2, b2)
    return y, hist[-(FFN_CONV - 1):]


def _ffn_step_kernel(x_ref, h_ref, wa_ref, wg_ref, cwa_ref, cwg_ref, cba_ref, cbg_ref, wo_ref, lg_ref, lb_ref,
                     y_ref, ua_ref, ug_ref, acc_ref):
    j = pl.program_id(0)
    x = x_ref[...]
    xb = x.astype(jnp.bfloat16)
    ua = jnp.dot(xb, wa_ref[...], preferred_element_type=jnp.float32)
    ug = jnp.dot(xb, wg_ref[...], preferred_element_type=jnp.float32)
    ua_ref[...] = ua
    ug_ref[...] = ug
    cwa, cwg = cwa_ref[...], cwg_ref[...]
    a = cwa[0:1] * h_ref[0, 0] + cwa[1:2] * h_ref[1, 0] + cwa[2:3] * ua + cba_ref[...]
    g = cwg[0:1] * h_ref[0, 1] + cwg[1:2] * h_ref[1, 1] + cwg[2:3] * ug + cbg_ref[...]
    h = (a * jax.nn.sigmoid(a) * g).astype(jnp.bfloat16)
    part = jnp.dot(h, wo_ref[...], preferred_element_type=jnp.float32)

    @pl.when(j == 0)
    def _():
        acc_ref[...] = part

    @pl.when(j > 0)
    def _():
        acc_ref[...] += part

    @pl.when(j == pl.num_programs(0) - 1)
    def _():
        y_ref[...] = _layer_norm_rows(DEEPNORM_ALPHA * x + acc_ref[...], lg_ref[...], lb_ref[...])


def _ffn_step(x, hist, w_in, conv_w, conv_b, w_out, ln_g, ln_b):
    b, d = x.shape
    c, nc = FFN_CHUNK, FFN_NCHUNK
    w_in = w_in.astype(jnp.bfloat16)
    cw8 = jnp.zeros((8, 2 * D_FF), jnp.float32).at[:FFN_CONV].set(conv_w)
    cb = conv_b.reshape(1, 2 * D_FF)
    h4 = jnp.transpose(hist, (1, 0, 2)).reshape(2, b, 2, D_FF).transpose(0, 2, 1, 3)
    y, ua, ug = pl.pallas_call(
        _ffn_step_kernel,
        grid=(nc,),
        in_specs=[pl.BlockSpec((b, d), lambda j: (0, 0)),
                  pl.BlockSpec((2, 2, b, c), lambda j: (0, 0, 0, j)),
                  pl.BlockSpec((d, c), lambda j: (0, j)),
                  pl.BlockSpec((d, c), lambda j: (0, j + nc)),
                  pl.BlockSpec((8, c), lambda j: (0, j)),
                  pl.BlockSpec((8, c), lambda j: (0, j + nc)),
                  pl.BlockSpec((1, c), lambda j: (0, j)),
                  pl.BlockSpec((1, c), lambda j: (0, j + nc)),
                  pl.BlockSpec((c, d), lambda j: (j, 0)),
                  pl.BlockSpec((1, d), lambda j: (0, 0)),
                  pl.BlockSpec((1, d), lambda j: (0, 0))],
        out_specs=[pl.BlockSpec((b, d), lambda j: (0, 0)),
                   pl.BlockSpec((b, c), lambda j: (0, j)),
                   pl.BlockSpec((b, c), lambda j: (0, j))],
        out_shape=[jax.ShapeDtypeStruct((b, d), jnp.float32),
                   jax.ShapeDtypeStruct((b, D_FF), jnp.float32),
                   jax.ShapeDtypeStruct((b, D_FF), jnp.float32)],
        scratch_shapes=[pltpu.VMEM((b, d), jnp.float32)],
        compiler_params=pltpu.CompilerParams(dimension_semantics=("arbitrary",),
                                             vmem_limit_bytes=VMEM_LIMIT_BYTES),
        name="ffn_step",
    )(x, h4, w_in, w_in, cw8, cw8, cb, cb, w_out.astype(jnp.bfloat16), ln_g.reshape(1, d), ln_b.reshape(1, d))
    u = jnp.concatenate([ua, ug], axis=-1)
    return y, jnp.concatenate([hist[:, 1:], u[:, None]], axis=1)


NEG = -1e30
NSA_KT = 512
NSA_COLS = NSA_HEADS * NSA_QBLOCK
NSA_WSPAN = NSA_WINDOW + NSA_QBLOCK
LOG2E = 1.4426950408889634
NSA_VROWS = HEAD_DIM + 8


def _lane_tile(x, n):
    return jnp.concatenate([x] * n, axis=1)


def _nsa_prompt_kernel(q_ref, sm_ref, ck_ref, cvt_ref, ks_ref, vst_ref, kw_ref, vwt_ref, hot_ref, o_ref,
                       selb_ref, m_ref, acc_ref, *, ns):
    f32, bf16 = jnp.float32, jnp.bfloat16
    qb = NSA_QBLOCK
    i = pl.program_id(0)
    s0 = i * qb
    half = NSA_COLS // 2

    qt = (q_ref[...] * (HEAD_DIM ** -0.5 * LOG2E)).T
    zero = jnp.zeros((HEAD_DIM, qb), f32)
    top = jnp.concatenate([qt[h * HEAD_DIM:(h + 1) * HEAD_DIM] for h in range(NSA_GROUP)] + [zero] * NSA_GROUP, axis=1)
    bot = jnp.concatenate([zero] * NSA_GROUP + [qt[h * HEAD_DIM:(h + 1) * HEAD_DIM]
                                                for h in range(NSA_GROUP, NSA_HEADS)], axis=1)
    qbd = jnp.concatenate([top, bot], axis=0).astype(bf16)

    def pv(vt, p):
        pb = p.astype(bf16)
        rows = vt.shape[0] // NSA_KV_HEADS
        return [jnp.dot(vt[g * rows:(g + 1) * rows], pb[:, g * half:(g + 1) * half],
                        preferred_element_type=f32) for g in range(NSA_KV_HEADS)]

    nc = 2 * ns
    r = lax.broadcasted_iota(jnp.int32, (nc, qb), 0)
    lane = lax.broadcasted_iota(jnp.int32, (nc, qb), 1)
    cidx = jnp.where(r < ns, 2 * r, 2 * (r - ns) + 1)
    cbias = jnp.where((cidx + 1) * CMP_BLOCK - 1 <= s0 + lane, 0.0, NEG)
    sc = jnp.dot(ck_ref[...], qbd, preferred_element_type=f32) + _lane_tile(cbias, NSA_HEADS)
    m = jnp.max(sc, axis=0, keepdims=True)
    p = jnp.exp2(sc - m)
    pn = p * jnp.where(m > 0.5 * NEG, 1.0 / jnp.sum(p, axis=0, keepdims=True), 0.0)
    o_cmp = pv(cvt_ref[...], pn)

    blk = lax.broadcasted_iota(jnp.int32, (ns, qb), 0)
    qpos = s0 + lax.broadcasted_iota(jnp.int32, (ns, qb), 1)
    cur = qpos // SEL_BLOCK
    forced = (blk == 0) | (blk == cur) | (blk == cur - 1)
    for g in range(NSA_KV_HEADS):
        imp = pn[:, g * half:g * half + qb]
        for h in range(1, NSA_GROUP):
            imp = imp + pn[:, g * half + h * qb:g * half + (h + 1) * qb]
        imp = imp[:ns] + imp[ns:]
        val = jnp.where(blk > cur, -1.0, jnp.where(forced, -jnp.inf, imp))
        bias = jnp.where(forced & (blk <= cur), 0.0, NEG)
        for _ in range(NSA_TOPN - 3):
            top = jnp.max(val, axis=0, keepdims=True)
            pick = jnp.min(jnp.where(val == top, blk, ns), axis=0, keepdims=True)
            hit = blk == pick
            bias = jnp.where(hit, 0.0, bias)
            val = jnp.where(hit, -jnp.inf, val)
        selb_ref[g] = bias

    m_ref[...] = jnp.full(m_ref.shape, NEG, f32)
    acc_ref[...] = jnp.zeros(acc_ref.shape, f32)
    per_tile = NSA_KT // SEL_BLOCK
    zpad = jnp.zeros((LANES - 16, NSA_COLS), bf16)

    def slc_tile(kt, causal):
        k0 = pl.multiple_of(kt * NSA_KT, NSA_KT)
        b0 = pl.multiple_of(kt * per_tile, per_tile)
        brow = jnp.concatenate([selb_ref[g, pl.ds(b0, per_tile), :] for g in range(NSA_KV_HEADS)
                                for _ in range(NSA_GROUP)], axis=1)
        brow = jnp.concatenate([brow, jnp.zeros((16 - per_tile, NSA_COLS), f32)], axis=0).astype(bf16)
        q_aug = jnp.concatenate([qbd, brow, zpad], axis=0)
        k_aug = jnp.concatenate([ks_ref[pl.ds(k0, NSA_KT), :], hot_ref[...]], axis=1)
        s = jnp.dot(k_aug, q_aug, preferred_element_type=f32)
        if causal:
            kpos = k0 + lax.broadcasted_iota(jnp.int32, (NSA_KT, qb), 0)
            qq = s0 + lax.broadcasted_iota(jnp.int32, (NSA_KT, qb), 1)
            s = s + _lane_tile(jnp.where(kpos <= qq, 0.0, NEG), NSA_HEADS)
        m_old = m_ref[...]
        m_new = jnp.maximum(m_old, jnp.max(s, axis=0, keepdims=True))
        alpha = jnp.exp2(m_old - m_new)
        p = jnp.exp2(s - m_new)
        m_ref[...] = m_new
        upd = pv(vst_ref[:, pl.ds(k0, NSA_KT)], p)
        for g in range(NSA_KV_HEADS):
            acc_ref[g] = acc_ref[g] * alpha[:, g * half:(g + 1) * half] + upd[g]

    kd = s0 // NSA_KT

    def body(j, carry):
        slc_tile(2 * j, False)
        slc_tile(2 * j + 1, False)
        return carry

    lax.fori_loop(0, kd // 2, body, 0)

    @pl.when(kd % 2 == 1)
    def _():
        slc_tile(kd - 1, False)

    slc_tile(kd, True)
    inv_slc = [1.0 / acc_ref[g, HEAD_DIM:HEAD_DIM + 1, :] for g in range(NSA_KV_HEADS)]

    w0 = pl.multiple_of(s0, qb)
    sw = jnp.dot(kw_ref[pl.ds(w0, NSA_WSPAN), :], qbd, preferred_element_type=f32)
    rr = lax.broadcasted_iota(jnp.int32, (NSA_WSPAN, qb), 0)
    qi = lax.broadcasted_iota(jnp.int32, (NSA_WSPAN, qb), 1)
    ok = (rr >= qi) & (rr <= qi + NSA_WINDOW) & (rr + s0 >= NSA_WINDOW)
    sw = sw + _lane_tile(jnp.where(ok, 0.0, NEG), NSA_HEADS)
    pw = jnp.exp2(sw - jnp.max(sw, axis=0, keepdims=True))
    o_win = pv(vwt_ref[:, pl.ds(w0, NSA_WSPAN)], pw)
    inv_win = [1.0 / o[HEAD_DIM:HEAD_DIM + 1] for o in o_win]

    gt = jax.nn.sigmoid(sm_ref[...].T)
    outs = []
    for h in range(NSA_HEADS):
        g, hg = divmod(h, NSA_GROUP)
        c0, c1 = hg * qb, (hg + 1) * qb
        g_cmp = gt[h:h + 1]
        g_slc = gt[NSA_HEADS + h:NSA_HEADS + h + 1] * inv_slc[g][:, c0:c1]
        g_win = gt[2 * NSA_HEADS + h:2 * NSA_HEADS + h + 1] * inv_win[g][:, c0:c1]
        outs.append(o_cmp[g][:, c0:c1] * g_cmp + acc_ref[g, :HEAD_DIM, c0:c1] * g_slc
                    + o_win[g][:HEAD_DIM, c0:c1] * g_win)
    o_ref[...] = jnp.concatenate(outs, axis=0).T


def _nsa_prompt_attention(qr, small, ck, cv, ksr, vs, kwr, vw):
    l = qr.shape[0]
    ns = l // SEL_BLOCK
    assert ns >= NSA_TOPN and l % NSA_KT == 0
    nc = 2 * ns
    bf16 = jnp.bfloat16
    perm =jnp.concatenate([jnp.arange(0, nc, 2), jnp.arange(1, nc, 2)])
    ckp = ck[perm].astype(bf16)
    cvt = cv[perm].T.astype(bf16)
    pad = jnp.zeros((NSA_WINDOW, NSA_KV_W), bf16)
    kwp = jnp.concatenate([pad, kwr.astype(bf16)], axis=0)
    def with_ones(vt):
        n = vt.shape[1]
        extra = jnp.concatenate([jnp.ones((1, n), bf16), jnp.zeros((NSA_VROWS - HEAD_DIM - 1, n), bf16)], axis=0)
        return jnp.concatenate([x for g in range(NSA_KV_HEADS) for x in (vt[g * HEAD_DIM:(g + 1) * HEAD_DIM], extra)], axis=0)

    vwt = with_ones(jnp.concatenate([pad, vw.astype(bf16)], axis=0).T)
    hot = (jnp.arange(NSA_KT)[:, None] // SEL_BLOCK == jnp.arange(LANES)[None, :]).astype(bf16)
    full = lambda a: pl.BlockSpec(a.shape, lambda i: (0,) * a.ndim)
    args = (qr, small, ckp, cvt, ksr.astype(bf16), with_ones(vs.T.astype(bf16)), kwp, vwt, hot)
    return pl.pallas_call(
        functools.partial(_nsa_prompt_kernel, ns=ns),
        grid=(l // NSA_QBLOCK,),
        in_specs=[pl.BlockSpec((NSA_QBLOCK, NSA_Q_W), lambda i: (i, 0)),
                  pl.BlockSpec((NSA_QBLOCK, LANES), lambda i: (i, 0))] + [full(a) for a in args[2:]],
        out_specs=pl.BlockSpec((NSA_QBLOCK, NSA_Q_W), lambda i: (i, 0)),
        out_shape=jax.ShapeDtypeStruct((l, NSA_Q_W), jnp.float32),
        scratch_shapes=[pltpu.VMEM((NSA_KV_HEADS, ns, NSA_QBLOCK), jnp.float32),
                        pltpu.VMEM((1, NSA_COLS), jnp.float32),
                        pltpu.VMEM((NSA_KV_HEADS, NSA_VROWS, NSA_COLS // 2), jnp.float32)],
        compiler_params=pltpu.CompilerParams(dimension_semantics=("arbitrary",),
                                             vmem_limit_bytes=VMEM_LIMIT_BYTES),
        name="nsa_prompt",
    )(*args)


CMP_HIDDEN = 2 * HEAD_DIM
CMP_PER_PAGE = PAGE_SIZE // CMP_BLOCK
CMP_PAGES_PER_STEP = 64
STEP_PAGES = 64
STEP_ROWS = 16


def _rope_tables(pos, width):
    half = HEAD_DIM // 2
    inv_freq = ROPE_THETA ** (-2.0 * jnp.arange(half, dtype=jnp.float32) / HEAD_DIM)
    ang = pos.astype(jnp.float32)[:, None] * inv_freq[None, :]
    cos, sin = jnp.cos(ang), jnp.sin(ang)
    reps = width // HEAD_DIM
    return (jnp.tile(jnp.concatenate([cos, cos], axis=1), (1, reps)),
            jnp.tile(jnp.concatenate([-sin, sin], axis=1), (1, reps)))


def _rope_lanes(x, cos, sin_signed):
    n = x.shape[-1]
    lane = lax.broadcasted_iota(jnp.int32, x.shape, x.ndim - 1)
    first = (lane % HEAD_DIM) < HEAD_DIM // 2
    partner = jnp.where(first, pltpu.roll(x, n - HEAD_DIM // 2, x.ndim - 1), pltpu.roll(x, HEAD_DIM // 2, x.ndim - 1))
    return x * cos + partner * sin_signed


def _nt_dot(a, b):
    return lax.dot_general(a, b, (((1,), (1,)), ((), ())), preferred_element_type=jnp.float32)


CMP_STEP_LANES = CMP_PAGES_PER_STEP * CMP_PER_PAGE
CMP_FEATURE_GROUP = 16


def _cmp_lane_blocks(n_pages):
    lane = np.arange(n_pages * CMP_PER_PAGE)
    step, rem = lane // CMP_STEP_LANES, lane % CMP_STEP_LANES
    j, pl_ = rem // CMP_PAGES_PER_STEP, rem % CMP_PAGES_PER_STEP
    return (step * CMP_PAGES_PER_STEP + pl_) * CMP_PER_PAGE + j


def _cmp_step_kernel(pt_ref, *refs):
    f32 = jnp.float32
    npg = CMP_PAGES_PER_STEP
    pages = refs[:npg]
    pe_ref, w1_ref, b1_ref, w2_ref, cos_ref, sin_ref, ck_ref, cv_ref, slab_ref = refs[npg:]
    outs = (ck_ref, cv_ref)
    for k, r in enumerate(pages):
        for s in range(2 * NSA_KV_HEADS):
            slab_ref[s, k * HEAD_DIM:(k + 1) * HEAD_DIM, :] = r[0, s]
    for kv in range(2):
        h = jnp.zeros((NSA_KV_HEADS * npg, CMP_PER_PAGE * CMP_HIDDEN), f32)
        xt = [jnp.swapaxes(slab_ref[2 * kv + g].reshape(npg, HEAD_DIM, PAGE_SIZE), 0, 1) for g in range(NSA_KV_HEADS)]
        for dg in range(HEAD_DIM // CMP_FEATURE_GROUP):
            x = jnp.concatenate(
                [jnp.concatenate([xt[g][d] for g in range(NSA_KV_HEADS)], axis=0) + pe_ref[kv, d]
                 for d in range(dg * CMP_FEATURE_GROUP, (dg + 1) * CMP_FEATURE_GROUP)], axis=1)
            h = h + jnp.dot(x.astype(jnp.bfloat16), w1_ref[kv, dg], preferred_element_type=f32)
        act = jax.nn.gelu(h + b1_ref[kv]).astype(jnp.bfloat16)
        ct = _nt_dot(w2_ref[kv], act)
        tile = jnp.concatenate(
            [jnp.concatenate([ct[j * HEAD_DIM:(j + 1) * HEAD_DIM, g * npg:(g + 1) * npg] for j in range(CMP_PER_PAGE)],
                             axis=1) for g in range(NSA_KV_HEADS)], axis=0)
        if kv == 0:
            row = lax.broadcasted_iota(jnp.int32, tile.shape, 0)
            n = tile.shape[0]
            partner = jnp.where((row % HEAD_DIM) < HEAD_DIM // 2, pltpu.roll(tile, n - HEAD_DIM // 2, 0),
                                pltpu.roll(tile, HEAD_DIM // 2, 0))
            tile = tile * cos_ref[...] + partner * sin_ref[...]
        outs[kv][0] = tile


def _compress_weights(cw1, cb1, cw2, cpe):
    eye = jnp.eye(CMP_PER_PAGE, dtype=jnp.float32)
    w1r = cw1.reshape(2, CMP_BLOCK, HEAD_DIM, CMP_HIDDEN)
    w1 = jnp.einsum('ktdn,ja->kdjtan', w1r, eye).reshape(
        2, HEAD_DIM // CMP_FEATURE_GROUP, CMP_FEATURE_GROUP * PAGE_SIZE, CMP_PER_PAGE * CMP_HIDDEN)
    b1 = jnp.tile(cb1, (1, CMP_PER_PAGE))[:, None, :]
    w2 = jnp.einsum('knd,ja->kjdan', cw2, eye).reshape(2, CMP_PER_PAGE * HEAD_DIM, CMP_PER_PAGE * CMP_HIDDEN)
    pe = jnp.tile(cpe.transpose(0, 2, 1), (1, 1, CMP_PER_PAGE))[:, :, None, :]
    return w1.astype(jnp.bfloat16), b1, w2.astype(jnp.bfloat16), pe


def _nsa_sample_compress(pool, layer_idx, page_table, cw1, cb1, cw2, cpe):
    n_pool, nl = pool.shape[:2]
    db, n_pages = page_table.shape
    npg = CMP_PAGES_PER_STEP
    nchunk = n_pages // npg
    nc = n_pages * CMP_PER_PAGE
    view = jnp.transpose(pool, (0, 1, 3, 4, 5, 2)).reshape(n_pool * nl, 2 * NSA_KV_HEADS, HEAD_DIM, PAGE_SIZE)
    pt = (page_table * nl + layer_idx).reshape(-1).astype(jnp.int32)
    w1, b1, w2, pe = _compress_weights(cw1, cb1, cw2, cpe)
    pos = (jnp.asarray(_cmp_lane_blocks(n_pages)) + 1) * CMP_BLOCK - 1
    half = HEAD_DIM // 2
    inv_freq = ROPE_THETA ** (-2.0 * jnp.arange(half, dtype=jnp.float32) / HEAD_DIM)
    ang = inv_freq[:, None] * pos.astype(jnp.float32)[None, :]
    cos = jnp.tile(jnp.cos(ang), (2 * NSA_KV_HEADS, 1))
    sin = jnp.tile(jnp.concatenate([-jnp.sin(ang), jnp.sin(ang)], axis=0), (NSA_KV_HEADS, 1))

    def page_map(k):
        return lambda b, c, pt_ref: (pt_ref[b * n_pages + c * npg + k], 0, 0, 0)

    const = lambda a: pl.BlockSpec(a.shape, lambda b, c, pt_ref: (0,) * a.ndim, pipeline_mode=pl.Buffered(1))
    lanes_c = lambda: pl.BlockSpec((NSA_KV_W, CMP_STEP_LANES), lambda b, c, pt_ref: (0, c))
    grid_spec = pltpu.PrefetchScalarGridSpec(
        num_scalar_prefetch=1, grid=(db, nchunk),
        in_specs=[pl.BlockSpec((1, 2 * NSA_KV_HEADS, HEAD_DIM, PAGE_SIZE), page_map(k)) for k in range(npg)]
        + [const(pe), const(w1), const(b1), const(w2), lanes_c(), lanes_c()],
        out_specs=[pl.BlockSpec((1, NSA_KV_W, CMP_STEP_LANES), lambda b, c, pt_ref: (b, 0, c))] * 2,
        scratch_shapes=[pltpu.VMEM((2 * NSA_KV_HEADS, npg * HEAD_DIM, PAGE_SIZE), jnp.float32)])
    return pl.pallas_call(
        _cmp_step_kernel, grid_spec=grid_spec,
        out_shape=[jax.ShapeDtypeStruct((db, NSA_KV_W, nc), jnp.float32)] * 2,
        compiler_params=pltpu.CompilerParams(dimension_semantics=("arbitrary", "arbitrary"),
                                             vmem_limit_bytes=VMEM_LIMIT_BYTES),
        name="nsa_sample_compress",
    )(pt, *([view] * npg), pe, w1, b1, w2, cos, sin)


def _nsa_step_kernel(pt_ref, *refs, topn):
    f32, bf16 = jnp.float32, jnp.bfloat16
    npg = STEP_PAGES
    q_ref, ck_ref, cv_ref = refs[:3]
    pages = refs[3:3 + npg]
    (win_ref, newr_ref, newt_ref, gate_ref, blk_ref, exp_ref, o_ref, wout_ref,
     selt_ref, m_ref, l_ref, acc_ref, side_ref) = refs[3 + npg:]
    cc = pl.program_id(1)
    q16 = q_ref[0]
    qb = q16.astype(bf16)
    row16 = lax.broadcasted_iota(jnp.int32, (STEP_ROWS, 1), 0)

    def new_key_scores(krow):
        return jnp.sum(q16 * krow, axis=1, keepdims=True)

    @pl.when(cc == 0)
    def _():
        nc = ck_ref.shape[2]
        s = jnp.dot(qb, ck_ref[0].astype(bf16), preferred_element_type=f32)
        p = jnp.exp(s - jnp.max(s, axis=1, keepdims=True))
        pn = p / jnp.sum(p, axis=1, keepdims=True)
        o_cmp = _nt_dot(pn.astype(bf16), cv_ref[0].astype(bf16))

        rowp = lax.broadcasted_iota(jnp.int32, pn.shape, 0)
        row8 = lax.broadcasted_iota(jnp.int32, (8, nc), 0)
        blk = blk_ref[...]
        val = jnp.full((8, nc), -jnp.inf, f32)
        for g in range(NSA_KV_HEADS):
            ig = jnp.sum(jnp.where((rowp >= g * NSA_GROUP) & (rowp < (g + 1) * NSA_GROUP), pn, 0.0),
                         axis=0, keepdims=True)
            ig = ig + pltpu.roll(ig, nc - CMP_PAGES_PER_STEP, 1)
            vg = jnp.where(blk[:1] < 0, -jnp.inf, jnp.where(blk[1:2] > 0, NSA_FORCE, ig))
            val = jnp.where(row8 == g, vg, val)
        sblk = jnp.where(blk[:1] < 0, nc, blk[:1])
        sel = jnp.zeros((8, nc), f32)
        for _ in range(topn):
            top = jnp.max(val, axis=1, keepdims=True)
            pick = jnp.min(jnp.where(val == top, sblk, nc), axis=1, keepdims=True)
            hit = sblk == pick
            sel = jnp.where(hit, 1.0, sel)
            val = jnp.where(hit, -jnp.inf, val)
        selh = jnp.where(row16 < NSA_GROUP, sel[0:1], jnp.where(row16 < NSA_HEADS, sel[1:2], 0.0))
        wsel = selt_ref.shape[2]
        for j in range(selt_ref.shape[0]):
            selt_ref[j] = selh[:, j * wsel:(j + 1) * wsel]

        m_ref[...] = new_key_scores(newr_ref[0, 0:1, :])
        l_ref[...] = jnp.ones(l_ref.shape, f32)
        acc_ref[...] = jnp.broadcast_to(newr_ref[0, 1:2, :], acc_ref.shape)

        sw = jnp.dot(qb, win_ref[0, 0].astype(bf16), preferred_element_type=f32)
        sn = new_key_scores(newr_ref[0, 2:3, :])
        mw = jnp.maximum(jnp.max(sw, axis=1, keepdims=True), sn)
        pw, pnw = jnp.exp(sw - mw), jnp.exp(sn - mw)
        lw = jnp.sum(pw, axis=1, keepdims=True) + pnw
        o_win = (_nt_dot(pw.astype(bf16), win_ref[0, 1].astype(bf16)) + pnw * newr_ref[0, 3:4, :]) / lw
        gt = jax.nn.sigmoid(gate_ref[0])
        side_ref[...] = gt[:, 0:1] * o_cmp + gt[:, 2:3] * o_win
        wl = win_ref.shape[3]
        lane = lax.broadcasted_iota(jnp.int32, (NSA_KV_W, wl), 1)
        for kv in range(2):
            wout_ref[0, kv] = jnp.where(lane == wl - 1, newt_ref[0, :, 2 + kv:3 + kv],
                                        pltpu.roll(win_ref[0, kv], wl - 1, 1))

    kt = jnp.concatenate([r[0, 0] for r in pages], axis=1)
    vt = jnp.concatenate([r[0, 1] for r in pages], axis=1)
    s = jnp.dot(qb, kt.astype(bf16), preferred_element_type=f32)
    picked = jnp.dot(selt_ref[cc].astype(bf16), exp_ref[...], preferred_element_type=f32)
    s = s + (picked - 1.0) * (-NEG)
    m_old = m_ref[...]
    m_new = jnp.maximum(m_old, jnp.max(s, axis=1, keepdims=True))
    alpha = jnp.exp(m_old - m_new)
    p = jnp.exp(s - m_new)
    m_ref[...] = m_new
    l_ref[...] = l_ref[...] * alpha + jnp.sum(p, axis=1, keepdims=True)
    acc_ref[...] = acc_ref[...] * alpha + _nt_dot(p.astype(bf16), vt.astype(bf16))

    @pl.when(cc == pl.num_programs(1) - 1)
    def _():
        gt = jax.nn.sigmoid(gate_ref[0])
        o_ref[0] = side_ref[...] + gt[:, 1:2] * acc_ref[...] / l_ref[...]


def _nsa_sample_attention(qr, ckt, cvt, slc_pool, layer_idx, page_table, win_buf, newrows, gate):
    db, n_pages = page_table.shape
    n_pool, nl = slc_pool.shape[:2]
    wlen = win_buf.shape[1]
    nc = ckt.shape[2]
    past = n_pages * PAGE_SIZE
    cur = past // SEL_BLOCK
    npg = STEP_PAGES
    nchunk = n_pages // npg
    keys = npg * PAGE_SIZE
    wsel = npg * CMP_PER_PAGE
    f32, bf16 = jnp.float32, jnp.bfloat16
    view = jnp.transpose(slc_pool, (0, 1, 3, 4, 5, 2)).reshape(n_pool * nl, 2, NSA_KV_W, PAGE_SIZE)
    wint = jnp.transpose(win_buf, (0, 2, 3, 4, 1)).reshape(db, 2, NSA_KV_W, wlen)
    pt = (page_table * nl + layer_idx).reshape(-1).astype(jnp.int32)
    hmask = (jnp.arange(NSA_HEADS)[:, None] // NSA_GROUP == jnp.arange(NSA_KV_HEADS)[None, :]).astype(f32)
    q16 = (qr * HEAD_DIM ** -0.5)[:, :, None, :] * hmask[None, :, :, None]
    q16 = jnp.pad(q16.reshape(db, NSA_HEADS, NSA_KV_W), ((0, 0), (0, STEP_ROWS - NSA_HEADS), (0, 0)))
    newr = jnp.pad(newrows, ((0, 0), (0, 8 - newrows.shape[1]), (0, 0)))
    newt = jnp.pad(newrows.transpose(0, 2, 1), ((0, 0), (0, 0), (0, LANES - newrows.shape[1])))
    g16 = jnp.pad(gate.reshape(db, 3, NSA_HEADS).transpose(0, 2, 1),
                  ((0, 0), (0, STEP_ROWS - NSA_HEADS), (0, LANES - 3)))
    cblk = _cmp_lane_blocks(n_pages)
    jj = cblk % CMP_PER_PAGE
    sblk = np.where(jj % 2 == 0, cblk // 2, -1)
    forced = ((sblk == 0) | (sblk == cur - 1)).astype(np.int32)
    blk8 = np.zeros((8, nc), np.int32)
    blk8[0], blk8[1] = sblk, forced
    loc = np.arange(wsel)
    lstep, lrem = loc // CMP_STEP_LANES, loc % CMP_STEP_LANES
    lj, lpage = lrem // CMP_PAGES_PER_STEP, lstep * CMP_PAGES_PER_STEP + lrem % CMP_PAGES_PER_STEP
    kidx = np.arange(keys)
    expand = ((lj[:, None] % 2 == 0) & (kidx[None, :] // PAGE_SIZE == lpage[:, None])
              & ((kidx[None, :] % PAGE_SIZE) // SEL_BLOCK == lj[:, None] // 2)).astype(np.float32)
    topn = min(NSA_TOPN, cur + 1) - 1

    def page_map(k):
        return lambda b, c, pt_ref: (pt_ref[b * n_pages + c * npg + k], 0, 0, 0)

    per_b = lambda shp: pl.BlockSpec((1,) + shp, lambda b, c, pt_ref: (b,) + (0,) * len(shp))
    const = lambda a: pl.BlockSpec(a.shape, lambda b, c, pt_ref: (0,) * a.ndim)
    consts = (jnp.asarray(blk8), jnp.asarray(expand, bf16))
    grid_spec = pltpu.PrefetchScalarGridSpec(
        num_scalar_prefetch=1, grid=(db, nchunk),
        in_specs=[per_b((STEP_ROWS, NSA_KV_W)), per_b((NSA_KV_W, nc)), per_b((NSA_KV_W, nc))]
        + [pl.BlockSpec((1, 2, NSA_KV_W, PAGE_SIZE), page_map(k)) for k in range(npg)]
        + [per_b((2, NSA_KV_W, wlen)), per_b((8, NSA_KV_W)), per_b((NSA_KV_W, LANES)), per_b((STEP_ROWS, LANES))]
        + [const(a) for a in consts],
        out_specs=[per_b((STEP_ROWS, NSA_KV_W)), per_b((2, NSA_KV_W, wlen))],
        scratch_shapes=[pltpu.VMEM((nc // wsel, STEP_ROWS, wsel), f32),
                        pltpu.VMEM((STEP_ROWS, 1), f32), pltpu.VMEM((STEP_ROWS, 1), f32),
                        pltpu.VMEM((STEP_ROWS, NSA_KV_W), f32), pltpu.VMEM((STEP_ROWS, NSA_KV_W), f32)])
    o16, wout = pl.pallas_call(
        functools.partial(_nsa_step_kernel, topn=topn), grid_spec=grid_spec,
        out_shape=[jax.ShapeDtypeStruct((db, STEP_ROWS, NSA_KV_W), f32),
                   jax.ShapeDtypeStruct((db, 2, NSA_KV_W, wlen), f32)],
        compiler_params=pltpu.CompilerParams(dimension_semantics=("arbitrary", "arbitrary"),
                                             vmem_limit_bytes=VMEM_LIMIT_BYTES),
        name="nsa_sample_attention",
    )(pt, q16, ckt, cvt, *([view] * npg), wint, newr, newt, g16, *consts)
    o = o16[:, :NSA_HEADS].reshape(db, NSA_HEADS, NSA_KV_HEADS, HEAD_DIM)
    o = jnp.take_along_axis(o, (jnp.arange(NSA_HEADS) // NSA_GROUP)[None, :, None, None], axis=2)
    wout = jnp.transpose(wout.reshape(db, 2, NSA_KV_HEADS, HEAD_DIM, wlen), (0, 4, 1, 2, 3))
    return o.reshape(db, NSA_HEADS * HEAD_DIM), wout


DIL_ROW_CHUNK = 64


def _dil_step_kernel(q_ref, buf_ref, newt_ref, newr_ref, bias_ref, o_ref, out_ref, p_ref, pn_ref, den_ref):
    f32, bf16 = jnp.float32, jnp.bfloat16
    kv = pl.program_id(1)
    wlen = buf_ref.shape[3]
    nrow = buf_ref.shape[2]
    q16 = q_ref[0]

    @pl.when(kv == 0)
    def _():
        s = jnp.dot(q16.astype(bf16), buf_ref[0, 0].astype(bf16), preferred_element_type=f32)
        s_new = jnp.sum(q16 * newr_ref[0, 0:1, :], axis=1, keepdims=True)
        ms, es, ens, dens = [], [], [], []
        for g in range(len(DIL_GROUPS)):
            sg = s + bias_ref[g:g + 1, :]
            m = jnp.maximum(jnp.max(sg, axis=1, keepdims=True), s_new)
            e, en = jnp.exp(sg - m), jnp.exp(s_new - m)
            ms.append(m); es.append(e); ens.append(en)
            dens.append(jnp.sum(e, axis=1, keepdims=True) + en)
        m_all = functools.reduce(jnp.maximum, ms)
        ws = [jnp.exp(m - m_all) for m in ms]
        p_ref[...] = sum(w * e for w, e in zip(ws, es))
        pn_ref[...] = sum(w * en for w, en in zip(ws, ens))
        den_ref[...] = sum(w * d for w, d in zip(ws, dens))

    @pl.when(kv == 1)
    def _():
        r = _nt_dot(p_ref[...].astype(bf16), buf_ref[0, 0].astype(bf16))
        r = (r + pn_ref[...] * newr_ref[0, 1:2, :]) / den_ref[...]
        head = lax.broadcasted_iota(jnp.int32, r.shape, 1) // HEAD_DIM
        row = lax.broadcasted_iota(jnp.int32, r.shape, 0)
        o_ref[0] = jnp.broadcast_to(jnp.sum(jnp.where(head == row, r, 0.0), axis=0, keepdims=True), o_ref.shape[1:])

    lane = lax.broadcasted_iota(jnp.int32, (DIL_ROW_CHUNK, wlen), 1)
    for c in range(nrow // DIL_ROW_CHUNK):
        rs = slice(c * DIL_ROW_CHUNK, (c + 1) * DIL_ROW_CHUNK)
        col = jnp.where(kv == 0, newt_ref[0, rs, 0:1], newt_ref[0, rs, 1:2])
        out_ref[0, 0, rs, :] = jnp.where(lane == wlen - 1, col, pltpu.roll(buf_ref[0, 0, rs, :], wlen - 1, 1))


def _dil_sample_attention(qr, kr_new, v_new, buf):
    db, wlen = buf.shape[:2]
    f32 = jnp.float32
    buft = jnp.transpose(buf, (0, 2, 3, 4, 1)).reshape(db, 2, DIL_W, wlen)
    eye = jnp.eye(DIL_HEADS, dtype=f32)
    q16 = ((qr * HEAD_DIM ** -0.5)[:, :, None, :] * eye[None, :, :, None]).reshape(db, DIL_HEADS, DIL_W)
    newr = jnp.pad(jnp.stack([kr_new, v_new], axis=1), ((0, 0), (0, 6), (0, 0)))
    newt = jnp.pad(jnp.stack([kr_new, v_new], axis=2), ((0, 0), (0, 0), (0, LANES - 2)))
    back = wlen - jnp.arange(wlen)
    bias = jnp.stack([jnp.where((back % d == 0) & (back // d <= DIL_SPAN), 0.0, NEG) for _, d in DIL_GROUPS])
    bias = jnp.pad(bias, ((0, 8 - len(DIL_GROUPS)), (0, 0))).astype(f32)
    o, new_buf = pl.pallas_call(
        _dil_step_kernel,
        grid=(db, 2),
        in_specs=[pl.BlockSpec((1, DIL_HEADS, DIL_W), lambda b, k: (b, 0, 0)),
                  pl.BlockSpec((1, 1, DIL_W, wlen), lambda b, k: (b, k, 0, 0)),
                  pl.BlockSpec((1, DIL_W, LANES), lambda b, k: (b, 0, 0)),
                  pl.BlockSpec((1, 8, DIL_W), lambda b, k: (b, 0, 0)),
                  pl.BlockSpec((8, wlen), lambda b, k: (0, 0))],
        out_specs=[pl.BlockSpec((1, 8, DIL_W), lambda b, k: (b, 0, 0)),
                   pl.BlockSpec((1, 1, DIL_W, wlen), lambda b, k: (b, k, 0, 0))],
        out_shape=[jax.ShapeDtypeStruct((db, 8, DIL_W), f32),
                   jax.ShapeDtypeStruct((db, 2, DIL_W, wlen), f32)],
        scratch_shapes=[pltpu.VMEM((DIL_HEADS, wlen), f32), pltpu.VMEM((DIL_HEADS, 1), f32),
                        pltpu.VMEM((DIL_HEADS, 1), f32)],
        compiler_params=pltpu.CompilerParams(dimension_semantics=("arbitrary", "arbitrary"),
                                             vmem_limit_bytes=VMEM_LIMIT_BYTES),
        name="dil_sample",
    )(q16, buft, newt, newr, bias)
    new_buf = jnp.transpose(new_buf.reshape(db, 2, DIL_HEADS, HEAD_DIM, wlen), (0, 4, 1, 2, 3))
    return o[:, 0], new_buf


def _dil_band_kernel(q_ref, kp_ref, kc_ref, vp_ref, vc_ref, num_ref, st_ref):
    f32, bf16 = jnp.float32, jnp.bfloat16
    blk = DIL_BLOCK
    n = pl.program_id(0)
    i = lax.broadcasted_iota(jnp.int32, (blk, 2 * blk), 0)
    j = lax.broadcasted_iota(jnp.int32, (blk, 2 * blk), 1) - blk
    ok = (i - j >= 0) & (i - j <= DIL_SPAN) & (n * blk + j >= 0)
    bias = jnp.where(ok, 0.0, NEG)
    bias = jnp.concatenate([bias, bias], axis=0)
    lane = lax.broadcasted_iota(jnp.int32, (blk, LANES), 1)
    first = lane < HEAD_DIM
    stats = jnp.zeros((blk, LANES), f32)
    for p in range(DIL_HEADS // 2):
        cols = slice(p * LANES, (p + 1) * LANES)
        qp = q_ref[:, cols] * (HEAD_DIM ** -0.5)
        qst = jnp.concatenate([jnp.where(first, qp, 0.0), jnp.where(first, 0.0, qp)], axis=0).astype(bf16)
        kk = jnp.concatenate([kp_ref[:, cols], kc_ref[:, cols]], axis=0).astype(bf16)
        vv = jnp.concatenate([vp_ref[:, cols], vc_ref[:, cols]], axis=0).astype(bf16)
        s = _nt_dot(qst, kk) + bias
        m = jnp.max(s, axis=1, keepdims=True)
        e = jnp.exp(s - m)
        den = jnp.sum(e, axis=1, keepdims=True)
        nm = jnp.dot(e.astype(bf16), vv, preferred_element_type=f32)
        num_ref[:, cols] = jnp.where(first, nm[:blk], nm[blk:])
        for a in range(2):
            h = 2 * p + a
            stats = jnp.where(lane == h, m[a * blk:(a + 1) * blk], stats)
            stats = jnp.where(lane == DIL_HEADS + h, den[a * blk:(a + 1) * blk], stats)
    st_ref[...] = stats


GDN_PREP_ROWS = 512
GDN_TILE_CHUNKS = 4
GDN_PAIRS = GDN_HEADS // 2
GDN_A_LANE = 3 * NSA_HEADS
GDN_B_LANE = GDN_A_LANE + GDN_HEADS


def _hi_lo(x):
    hi = x.astype(jnp.bfloat16)
    return hi, (x - hi.astype(jnp.float32)).astype(jnp.bfloat16)


def _three_way(x):
    f32 = jnp.float32
    x1 = x.astype(jnp.bfloat16)
    r1 = x - x1.astype(f32)
    x2 = r1.astype(jnp.bfloat16)
    return x1, x2, (r1 - x2.astype(f32)).astype(jnp.bfloat16)


def _dot_select(x, sel):
    return sum(jnp.dot(piece, sel, preferred_element_type=jnp.float32) for piece in _three_way(x))


def _select_dot(sel, x):
    return sum(jnp.dot(sel, piece, preferred_element_type=jnp.float32) for piece in _three_way(x))


def _dot_hl(a, b):
    f32 = jnp.float32
    ah, al = _hi_lo(a)
    bh, bl = _hi_lo(b)
    return (jnp.dot(ah, bh, preferred_element_type=f32) + jnp.dot(ah, bl, preferred_element_type=f32)
            + jnp.dot(al, bh, preferred_element_type=f32))


def _gdn_prep_kernel(u_ref, sm_ref, cw_ref, prm_ref, ea_ref, eb_ref, eh_ref, q_ref, k_ref, v_ref, g_ref, b_ref,
                     carry_ref):
    f32 = jnp.float32
    i = pl.program_id(0)
    tl = u_ref.shape[0]

    @pl.when(i == 0)
    def _():
        carry_ref[...] = jnp.zeros(carry_ref.shape, f32)

    u = u_ref[...]
    prev = carry_ref[...]
    row = lax.broadcasted_iota(jnp.int32, u.shape, 0)

    def shifted(k):
        r = pltpu.roll(u, k, 0)
        for j in range(k):
            r = jnp.where(row == j, prev[8 - k + j:8 - k + j + 1], r)
        return r

    cw = cw_ref[...]
    c = cw[0:1] * shifted(3) + cw[1:2] * shifted(2) + cw[2:3] * shifted(1) + cw[3:4] * u
    carry_ref[...] = u[tl - 8:]
    c = c * jax.nn.sigmoid(c)
    eh = eh_ref[...]

    def l2n(x):
        return x * lax.rsqrt(_dot_select(x * x, eh) + NORM_EPS)

    q_ref[...] = l2n(c[:, :GDN_W]) * (HEAD_DIM ** -0.5)
    k_ref[...] = l2n(c[:, GDN_W:2 * GDN_W])
    v_ref[...] = c[:, 2 * GDN_W:]
    sm = sm_ref[...]
    x = sm + prm_ref[1:2]
    softplus = jnp.maximum(x, 0.0) + jnp.log(1.0 + jnp.exp(-jnp.abs(x)))
    g_ref[...] = _dot_select(-jnp.exp(prm_ref[0:1]) * softplus, ea_ref[...])
    b_ref[...] = _dot_select(jax.nn.sigmoid(sm), eb_ref[...])


def _gdn_chunk_kernel(q_ref, k_ref, v_ref, g_ref, b_ref, z_ref, nw_ref, lt_ref, eh_ref, o_ref, s_out_ref, s_ref):
    f32 = jnp.float32
    ch = GDN_CHUNK
    i = pl.program_id(0)

    @pl.when(i == 0)
    def _():
        s_ref[...] = jnp.zeros(s_ref.shape, f32)

    lane = lax.broadcasted_iota(jnp.int32, (ch, LANES), 1)
    first = lane < HEAD_DIM
    stack = lambda x: jnp.concatenate([jnp.where(first, x, 0.0), jnp.where(first, 0.0, x)], axis=0)
    r2 = lax.broadcasted_iota(jnp.int32, (2 * ch, 2 * ch), 0)
    c2 = lax.broadcasted_iota(jnp.int32, (2 * ch, 2 * ch), 1)
    same = (r2 // ch) == (c2 // ch)
    tri = same & (r2 % ch >= c2 % ch)
    strict = same & (r2 % ch > c2 % ch)
    eye = r2 == c2
    eye_f = jnp.where(eye, 1.0, 0.0)
    diag2 = lax.broadcasted_iota(jnp.int32, (ch, LANES), 0) == lane % HEAD_DIM
    lt = lt_ref[...]
    bf = lambda x: x.astype(jnp.bfloat16)
    dot = lambda a, b: jnp.dot(bf(a), bf(b), preferred_element_type=f32)

    blocks = [(c, p) for c in range(GDN_TILE_CHUNKS) for p in range(GDN_PAIRS)]
    ld = lambda ref, c, p: ref[c * ch:(c + 1) * ch, p * LANES:(p + 1) * LANES]
    gcs = [_select_dot(lt, ld(g_ref, c, p)) for c, p in blocks]
    amats, qks, rhs_u, rhs_w, qgs, kds, decs = [], [], [], [], [], [], []
    for (c, p), gc in zip(blocks, gcs):
        kk, qq, vv, bb = ld(k_ref, c, p), ld(q_ref, c, p), ld(v_ref, c, p), ld(b_ref, c, p)
        eg = jnp.exp(gc)
        g_end = gc[ch - 1:ch]
        kb = kk * bb
        col = jnp.concatenate([jnp.broadcast_to(gc[:, 0:1], (ch, LANES)),
                               jnp.broadcast_to(gc[:, HEAD_DIM:HEAD_DIM + 1], (ch, LANES))], axis=0)
        rowv = jnp.sum(jnp.where(diag2, gc, 0.0), axis=0, keepdims=True)
        gam = jnp.where(tri, jnp.exp(jnp.where(tri, col - rowv, 0.0)), 0.0)
        kst = stack(kk)
        amats.append(jnp.where(strict, _nt_dot(bf(stack(kb)), bf(kst)) * gam, 0.0))
        qks.append(jnp.where(tri, _nt_dot(bf(stack(qq)), bf(kst)) * gam, 0.0))
        rhs_u.append(stack(vv * bb))
        rhs_w.append(stack(kb * eg))
        qgs.append(stack(qq * eg))
        kds.append(stack(kk * jnp.exp(g_end - gc)))
        decs.append(jnp.sum(jnp.where(eye, jnp.exp(g_end), 0.0), axis=1, keepdims=True))
    xs = [eye_f - a for a in amats]
    pws = [_dot_hl(a, a) for a in amats]
    steps = GDN_CHUNK.bit_length() - 2
    for r in range(steps):
        xs = [x + _dot_hl(x, pw) for x, pw in zip(xs, pws)]
        if r < steps - 1:
            pws = [_dot_hl(pw, pw) for pw in pws]
    uus = [dot(x, u) for x, u in zip(xs, rhs_u)]
    wws = [dot(x, w) for x, w in zip(xs, rhs_w)]
    kdts = [kd.T for kd in kds]
    states = [s_ref[p] for p in range(GDN_PAIRS)]
    for c in range(GDN_TILE_CHUNKS):
        rs = slice(c * ch, (c + 1) * ch)
        outs = []
        for p in range(GDN_PAIRS):
            n = c * GDN_PAIRS + p
            s = states[p]
            v_new = uus[n] - dot(wws[n], s)
            o_st = dot(qgs[n], s) + dot(qks[n], v_new)
            states[p] = s * decs[n] + dot(kdts[n], v_new)
            outs.append(o_st[:ch] + o_st[ch:])
        o = jnp.concatenate(outs, axis=1)
        ms = _dot_select(o * o, eh_ref[...]) * (1.0 / HEAD_DIM)
        z = z_ref[rs, :]
        o_ref[rs, :] = o * lax.rsqrt(ms + NORM_EPS) * nw_ref[...] * (z * jax.nn.sigmoid(z))
    for p in range(GDN_PAIRS):
        s_ref[p] = states[p]

    @pl.when(i == pl.num_programs(0) - 1)
    def _():
        s_out_ref[...] = s_ref[...]


def _gdn_prompt(qkv, small, z, conv_w, a_log, dt_bias, norm_w):
    l = qkv.shape[0]
    f32, bf16 = jnp.float32, jnp.bfloat16
    w = GDN_W
    hh = jnp.arange(w) // HEAD_DIM
    expander = lambda base: (jnp.arange(LANES)[:, None] == base + hh[None, :]).astype(bf16)
    eh = (hh[:, None] == hh[None, :]).astype(bf16)
    cw8 = jnp.zeros((8, 3 * w), f32).at[:GDN_CONV].set(conv_w)
    prm = jnp.zeros((8, LANES), f32)
    prm = prm.at[0, GDN_A_LANE:GDN_A_LANE + GDN_HEADS].set(a_log).at[1, GDN_A_LANE:GDN_A_LANE + GDN_HEADS].set(dt_bias)
    tl = GDN_PREP_ROWS
    row = lambda wd: pl.BlockSpec((tl, wd), lambda i: (i, 0))
    const = lambda a: pl.BlockSpec(a.shape, lambda i: (0,) * a.ndim)
    ea, eb = expander(GDN_A_LANE), expander(GDN_B_LANE)
    q, k, v, g, b = pl.pallas_call(
        _gdn_prep_kernel,
        grid=(l // tl,),
        in_specs=[row(3 * w), row(LANES), const(cw8), const(prm), const(ea), const(eb), const(eh)],
        out_specs=[row(w)] * 5,
        out_shape=[jax.ShapeDtypeStruct((l, w), f32)] * 5,
        scratch_shapes=[pltpu.VMEM((8, 3 * w), f32)],
        compiler_params=pltpu.CompilerParams(dimension_semantics=("arbitrary",), vmem_limit_bytes=VMEM_LIMIT_BYTES),
        name="gdn_prep",
    )(qkv, small, cw8, prm, ea, eb, eh)
    tc = GDN_TILE_CHUNKS * GDN_CHUNK
    lt = (jnp.arange(GDN_CHUNK)[:, None] >= jnp.arange(GDN_CHUNK)[None, :]).astype(bf16)
    nw = jnp.tile(norm_w, GDN_HEADS).reshape(1, w)
    rowc = pl.BlockSpec((tc, w), lambda i: (i, 0))
    o, s_bd = pl.pallas_call(
        _gdn_chunk_kernel,
        grid=(l // tc,),
        in_specs=[rowc] * 6 + [const(nw), const(lt), const(eh)],
        out_specs=[rowc, pl.BlockSpec((GDN_PAIRS, LANES, LANES), lambda i: (0, 0, 0))],
        out_shape=[jax.ShapeDtypeStruct((l, w), f32), jax.ShapeDtypeStruct((GDN_PAIRS, LANES, LANES), f32)],
        scratch_shapes=[pltpu.VMEM((GDN_PAIRS, LANES, LANES), f32)],
        compiler_params=pltpu.CompilerParams(dimension_semantics=("arbitrary",), vmem_limit_bytes=VMEM_LIMIT_BYTES),
        name="gdn_chunk",
    )(q, k, v, g, b, z, nw, lt, eh)
    s4 = s_bd.reshape(GDN_PAIRS, 2, HEAD_DIM, 2, HEAD_DIM)
    s_fin = jnp.stack([s4[:, 0, :, 0], s4[:, 1, :, 1]], axis=1).reshape(GDN_HEADS, HEAD_DIM, HEAD_DIM)
    return o, s_fin


def _proj_dil_kernel(x_ref, w_ref, cos_ref, sin_ref, bf_ref, kv_ref):
    acc = jnp.dot(x_ref[...].astype(jnp.bfloat16), w_ref[...], preferred_element_type=jnp.float32)
    reps = 2 * DIL_W // LANES
    qk = _rope_lanes(acc[:, :2 * DIL_W], _lane_tile(cos_ref[...], reps), _lane_tile(sin_ref[...], reps))
    v = acc[:, 2 * DIL_W:]
    bf_ref[...] = jnp.concatenate([qk, v], axis=1).astype(jnp.bfloat16)
    kv_ref[...] = jnp.concatenate([qk[:, DIL_W:], v], axis=1)


def _proj_dil(x, w_in):
    l, d = x.shape
    tm = _row_tile(l)
    cos, sin = _rope_tables(jnp.arange(l), LANES)
    return pl.pallas_call(
        _proj_dil_kernel,
        grid=(l // tm,),
        in_specs=[pl.BlockSpec((tm, d), lambda i: (i, 0)),
                  pl.BlockSpec((d, 3 * DIL_W), lambda i: (0, 0)),
                  pl.BlockSpec((tm, LANES), lambda i: (i, 0)),
                  pl.BlockSpec((tm, LANES), lambda i: (i, 0))],
        out_specs=[pl.BlockSpec((tm, 3 * DIL_W), lambda i: (i, 0)),
                   pl.BlockSpec((tm, 2 * DIL_W), lambda i: (i, 0))],
        out_shape=[jax.ShapeDtypeStruct((l, 3 * DIL_W), jnp.bfloat16),
                   jax.ShapeDtypeStruct((l, 2 * DIL_W), jnp.float32)],
        compiler_params=pltpu.CompilerParams(dimension_semantics=("arbitrary",), vmem_limit_bytes=VMEM_LIMIT_BYTES),
        name="proj_dil",
    )(x, w_in.astype(jnp.bfloat16), cos, sin)


def _dil_band_stats(qkv, d):
    l = qkv.shape[0]
    assert l % (d * DIL_BLOCK) == 0
    nb = l // (d * DIL_BLOCK)
    view = qkv.reshape(l // d, d * 3 * DIL_W)

    def part(which, prev):
        return pl.BlockSpec((DIL_BLOCK, DIL_W), (lambda n, r: (jnp.maximum(n - 1, 0), 3 * r + which)) if prev
                            else (lambda n, r: (n, 3 * r + which)))

    out = lambda w: pl.BlockSpec((DIL_BLOCK, w), lambda n, r: (n, r))
    num, st = pl.pallas_call(
        _dil_band_kernel,
        grid=(nb, d),
        in_specs=[part(0, False), part(1, True), part(1, False), part(2, True), part(2, False)],
        out_specs=[out(DIL_W), out(LANES)],
        out_shape=[jax.ShapeDtypeStruct((l // d, d * DIL_W), jnp.float32),
                   jax.ShapeDtypeStruct((l // d, d * LANES), jnp.float32)],
        compiler_params=pltpu.CompilerParams(dimension_semantics=("arbitrary", "arbitrary"),
                                             vmem_limit_bytes=VMEM_LIMIT_BYTES),
        name="dil_band_stats",
    )(view, view, view, view, view)
    return num.reshape(l, DIL_W), st.reshape(l, LANES)


def _dil_merge_kernel(*refs):
    f32 = jnp.float32
    ng = len(DIL_GROUPS)
    nums, sts = refs[:ng], refs[ng:2 * ng]
    ex_ref, w_ref, res_ref, g_ref, b_ref, o_ref = refs[2 * ng:]
    st = [r[...] for r in sts]
    m_all = functools.reduce(jnp.maximum, st)
    ws = [jnp.exp(s - m_all) for s in st]
    den = sum(w * pltpu.roll(s, LANES - DIL_HEADS, 1) for w, s in zip(ws, st))
    head_lane = lax.broadcasted_iota(jnp.int32, den.shape, 1) < DIL_HEADS
    o = sum(_dot_select(jnp.where(head_lane, w / den, 0.0), ex_ref[...]) * n[...]
            for w, n in zip(ws, nums))
    acc = jnp.dot(o.astype(jnp.bfloat16), w_ref[...], preferred_element_type=f32)
    o_ref[...] = _layer_norm_rows(DEEPNORM_ALPHA * res_ref[...] + acc, g_ref[...], b_ref[...])


def _dil_merge_proj(nums, sts, w_out, res, g, b):
    l, n = res.shape
    tm = _row_tile(l)
    expand = (jnp.arange(LANES)[:, None] == jnp.arange(DIL_W)[None, :] // HEAD_DIM).astype(jnp.bfloat16)
    row = lambda w: pl.BlockSpec((tm, w), lambda i: (i, 0))
    const = lambda a: pl.BlockSpec(a.shape, lambda i: (0,) * a.ndim)
    wb = w_out.astype(jnp.bfloat16)
    g2, b2 = g.reshape(1, n), b.reshape(1, n)
    return pl.pallas_call(
        _dil_merge_kernel,
        grid=(l // tm,),
        in_specs=[row(DIL_W)] * len(nums) + [row(LANES)] * len(sts) + [const(expand), const(wb), row(n),
                                                                       const(g2), const(b2)],
        out_specs=row(n),
        out_shape=jax.ShapeDtypeStruct((l, n), jnp.float32),
        compiler_params=pltpu.CompilerParams(dimension_semantics=("arbitrary",), vmem_limit_bytes=VMEM_LIMIT_BYTES),
        name="dil_merge_proj",
    )(*nums, *sts, expand, wb, res, g2, b2)


def _split_cols(h, widths):
    parts, start = [], 0
    for w in widths:
        parts.append(h[..., start:start + w])
        start += w
    return parts


def _even_widths():
    return (NSA_Q_W,) + (NSA_KV_W,) * 6 + (3 * NSA_HEADS, 3 * GDN_W, GDN_HEADS, GDN_HEADS, GDN_W)


def _rms_norm(x, w):
    return x * lax.rsqrt(jnp.mean(jnp.square(x), axis=-1, keepdims=True) + NORM_EPS) * w


def _l2_norm(x):
    return x * lax.rsqrt(jnp.sum(jnp.square(x), axis=-1, keepdims=True) + NORM_EPS)


def _rope(x, pos):
    half = HEAD_DIM // 2
    inv_freq = ROPE_THETA ** (-2.0 * jnp.arange(half, dtype=jnp.float32) / HEAD_DIM)
    ang = pos.astype(jnp.float32)[:, None] * inv_freq[None, :]
    cos, sin = jnp.cos(ang)[:, None, :], jnp.sin(ang)[:, None, :]
    xf = x.astype(jnp.float32)
    x1, x2 = xf[..., :half], xf[..., half:]
    return jnp.concatenate([x1 * cos - x2 * sin, x2 * cos + x1 * sin], axis=-1)


def _causal_dwconv(hist, u, w):
    width, s = w.shape[0], u.shape[1]
    ext = jnp.concatenate([hist.astype(u.dtype), u], axis=1)
    out = w[0] * ext[:, :s]
    for j in range(1, width):
        out = out + w[j] * ext[:, j:j + s]
    return out, ext[:, s:]


def _nsa_compress(rows, w1, b1, w2, pe):
    b, l, g, dh = rows.shape
    nc = l // CMP_BLOCK
    blk = rows[:, :nc * CMP_BLOCK].astype(jnp.float32).reshape(b, nc, CMP_BLOCK, g, dh) + pe[:, None, :]
    flat = blk.transpose(0, 1, 3, 2, 4).reshape(b, nc, g, CMP_BLOCK * dh)
    return jax.nn.gelu(flat @ w1 + b1) @ w2


def _nsa_compressed_kv(k_rows, v_rows, cw1, cb1, cw2, cpe):
    ck = _nsa_compress(k_rows, cw1[0], cb1[0], cw2[0], cpe[0])
    cv = _nsa_compress(v_rows, cw1[1], cb1[1], cw2[1], cpe[1])
    nc = ck.shape[1]
    ck = _rope(ck, (jnp.arange(nc) + 1) * CMP_BLOCK - 1)
    return ck, cv


def _rope_rows_kernel(*refs):
    cos_ref, sin_ref = refs[:2]
    n = (len(refs) - 2) // 2
    for x_ref, o_ref in zip(refs[2:2 + n], refs[2 + n:]):
        reps = x_ref.shape[1] // LANES
        o_ref[...] = _rope_lanes(x_ref[...], _lane_tile(cos_ref[...], reps), _lane_tile(sin_ref[...], reps))


def _rope_rows(xs):
    l = xs[0].shape[0]
    tm = _row_tile(l)
    cos, sin = _rope_tables(jnp.arange(l), LANES)
    row = lambda w: pl.BlockSpec((tm, w), lambda i: (i, 0))
    return pl.pallas_call(
        _rope_rows_kernel,
        grid=(l // tm,),
        in_specs=[row(LANES), row(LANES)] + [row(x.shape[1]) for x in xs],
        out_specs=[row(x.shape[1]) for x in xs],
        out_shape=[jax.ShapeDtypeStruct(x.shape, jnp.float32) for x in xs],
        compiler_params=pltpu.CompilerParams(dimension_semantics=("arbitrary",), vmem_limit_bytes=VMEM_LIMIT_BYTES),
        name="rope_rows",
    )(cos, sin, *xs)


def _nsa_prompt(q, kc, vc, ks, vs, kw, vw, small, cw1, cb1, cw2, cpe):
    b, l = q.shape[:2]
    flat = lambda t: t.reshape(t.shape[1], -1)
    qr, ksr, kwr = _rope_rows([flat(q), flat(ks), flat(kw)])
    ksr, kwr = ksr.reshape(ks.shape), kwr.reshape(kw.shape)
    ck, cv = _nsa_compressed_kv(kc, vc, cw1, cb1, cw2, cpe)
    vsf = vs.astype(jnp.float32)
    vwf = vw.astype(jnp.float32)
    o_nsa = _nsa_prompt_attention(qr, small, flat(ck), flat(cv), flat(ksr), flat(vsf), flat(kwr), flat(vwf))
    keep = min(NSA_WINDOW, l)
    rows_cmp = jnp.stack([kc, vc], axis=2)
    rows_slc = jnp.stack([ksr, vsf], axis=2)
    rows_win = jnp.stack([kwr[:, l - keep:], vwf[:, l - keep:]], axis=2)
    return o_nsa[None], rows_cmp, rows_slc, rows_win


def _nsa_sample(q, kc, vc, ks, vs, kw, vw, gate, cmp_pool, slc_pool, layer_idx, win_buf, page_table,
                cw1, cb1, cw2, cpe):
    db, s = q.shape[:2]
    past = page_table.shape[1] * PAGE_SIZE
    wb = win_buf.shape[1]
    assert s == 1 and wb == NSA_WINDOW and past >= wb and past % (STEP_PAGES * PAGE_SIZE) == 0
    qpos = past + jnp.arange(s)
    qr = _rope(q, qpos)
    ckt, cvt = _nsa_sample_compress(cmp_pool, layer_idx, page_table, cw1, cb1, cw2, cpe)
    ksr = _rope(ks, qpos)
    vsf = vs.astype(jnp.float32)
    kwr = _rope(kw, qpos)
    vwf = vw.astype(jnp.float32)
    newrows = jnp.stack([t.reshape(db, NSA_KV_W) for t in (ksr, vsf, kwr, vwf)], axis=1)
    o_nsa, rows_win = _nsa_sample_attention(qr[:, 0], ckt, cvt, slc_pool, layer_idx, page_table, win_buf,
                                              newrows, gate.reshape(db, -1))
    rows_cmp = jnp.stack([kc, vc], axis=2)
    rows_slc = jnp.stack([ksr, vsf], axis=2)
    return o_nsa[:, None], rows_cmp, rows_slc, rows_win


def _gdn_recurrent(q, k, v, g, beta, s0):
    def step(state, xs):
        q_t, k_t, v_t, g_t, b_t = xs
        state = state * jnp.exp(g_t)[..., None, None]
        v_t = (v_t - jnp.einsum('bhk,bhkv->bhv', k_t, state)) * b_t[..., None]
        state = state + jnp.einsum('bhk,bhv->bhkv', k_t, v_t)
        return state, jnp.einsum('bhk,bhkv->bhv', q_t, state)

    xs = tuple(jnp.moveaxis(a, 1, 0) for a in (q, k, v, g, beta))
    s_fin, o = lax.scan(step, s0, xs)
    return jnp.moveaxis(o, 0, 1), s_fin


def _gdn_step(qkv, a, bt, z, conv_hist, s0, conv_w, a_log, dt_bias, norm_w):
    b, s = qkv.shape[:2]
    c, new_hist = _causal_dwconv(conv_hist, qkv, conv_w)
    c = jax.nn.silu(c.astype(jnp.float32))
    q, k, v = [t.reshape(b, s, GDN_HEADS, HEAD_DIM) for t in jnp.split(c, 3, axis=-1)]
    q = _l2_norm(q) * HEAD_DIM ** -0.5
    k = _l2_norm(k)
    beta = jax.nn.sigmoid(bt.astype(jnp.float32))
    g = -jnp.exp(a_log) * jax.nn.softplus(a.astype(jnp.float32) + dt_bias)
    o, s_fin = _gdn_recurrent(q, k, v, g, beta, s0.astype(jnp.float32))
    o = _rms_norm(o, norm_w) * jax.nn.silu(z.astype(jnp.float32).reshape(b, s, GDN_HEADS, HEAD_DIM))
    return o.reshape(b, s, GDN_W), new_hist, s_fin


def _proj(x, w):
    b, s, d = x.shape
    n = w.shape[1]
    npad = -(-n // LANES) * LANES
    wp = jnp.pad(w, ((0, 0), (0, npad - n)))
    return _matmul(x.reshape(b * s, d), wp).reshape(b, s, npad)


def _even_prompt(x, w_in, cw1, cb1, cw2, cpe, conv_w, a_log, dt_bias, norm_w):
    b, l, _ = x.shape
    q, kc, vc, ks, vs, kw, vw, gate, qkv, a, bt, z = _split_cols(_proj(x, w_in), _even_widths())
    heads = lambda t: t.reshape(b, l, -1, HEAD_DIM)
    assert b == 1
    small = jnp.concatenate([gate, a, bt], axis=-1).reshape(l, -1)
    small = jnp.pad(small, ((0, 0), (0, LANES - small.shape[-1])))
    o_nsa, r_cmp, r_slc, r_win = _nsa_prompt(
        heads(q), heads(kc), heads(vc), heads(ks), heads(vs), heads(kw), heads(vw), small, cw1, cb1, cw2, cpe)
    o_gdn, s_fin = _gdn_prompt(qkv[0], small, z[0], conv_w, a_log, dt_bias, norm_w)
    conv_hist = qkv[:, l - (GDN_CONV - 1):]
    return [o_nsa[0], o_gdn], r_cmp, r_slc, r_win, conv_hist, s_fin[None]


def _even_sample(x, cmp_pool, slc_pool, layer_idx, win_buf, conv_hist, s0, page_table,
                 w_in, cw1, cb1, cw2, cpe, conv_w, a_log, dt_bias, norm_w):
    b, s, _ = x.shape
    q, kc, vc, ks, vs, kw, vw, gate, qkv, a, bt, z = _split_cols(_proj(x, w_in), _even_widths())
    heads = lambda t: t.reshape(b, s, -1, HEAD_DIM)
    o_nsa, r_cmp, r_slc, r_win = _nsa_sample(
        heads(q), heads(kc), heads(vc), heads(ks), heads(vs), heads(kw), heads(vw), gate,
        cmp_pool, slc_pool, layer_idx, win_buf, page_table, cw1, cb1, cw2, cpe)
    o_gdn, new_hist, s_fin = _gdn_step(qkv, a, bt, z, conv_hist, s0, conv_w, a_log, dt_bias, norm_w)
    return [o_nsa[:, 0], o_gdn[:, 0]], r_cmp, r_slc, r_win, new_hist, s_fin


def _dil_prompt(x, w_in, w_out, g, b):
    bsz, l, _ = x.shape
    assert bsz == 1
    qkv, kv = _proj_dil(x[0], w_in)
    stats = [_dil_band_stats(qkv, d) for _, d in DIL_GROUPS]
    y = _dil_merge_proj([n for n, _ in stats], [s for _, s in stats], w_out, x[0], g, b)
    keep = min(DIL_MAX_WINDOW, l)
    buf = kv[l - keep:].reshape(1, keep, 2, DIL_HEADS, HEAD_DIM)
    return y, buf


def _dil_sample(x, buf, past, w_in):
    db, s, _ = x.shape
    q, k, v = [t.reshape(db, s, DIL_HEADS, HEAD_DIM) for t in jnp.split(_proj(x, w_in), 3, axis=-1)]
    assert s == 1 and buf.shape[1] == DIL_MAX_WINDOW <= past
    qpos = past + jnp.arange(s)
    qr, kr = _rope(q, qpos), _rope(k, qpos)
    o, new_buf = _dil_sample_attention(qr[:, 0], kr.reshape(db, DIL_W), v.reshape(db, DIL_W).astype(jnp.float32), buf)
    return o[:, None], new_buf


def kernel(x_prompt, x_sample, cache_nsa_cmp_kv, cache_nsa_slc_kv, state_nsa_win_kv, state_gdn_conv,
           state_gdn_S, state_dil_kv, state_ffn_conv, page_table, w_in_a, nsa_cmp_w1, nsa_cmp_b1, nsa_cmp_w2,
           nsa_cmp_pe, gdn_conv_w, gdn_A_log, gdn_dt_bias, gdn_norm_w, w_out_a, w_in_c, w_out_c,
           ln_mix_g, ln_mix_b, ffn_w_in, ffn_conv_w, ffn_conv_b, ffn_w_out, ln_ffn_g, ln_ffn_b):
    past = page_table.shape[1] * PAGE_SIZE
    bp, lp, d = x_prompt.shape
    bs, ls, _ = x_sample.shape
    assert bp == 1 and ls == 1
    xp, xs = x_prompt, x_sample
    cmp_p, cmp_s, slc_p, slc_s, win_p, win_s = [], [], [], [], [], []
    gconv_p, gconv_s, gstate_p, gstate_s = [], [], [], []
    dil_p, dil_s, ffn_p, ffn_s = [], [], [], []
    for layer in range(DEPTH):
        if layer % 2 == 0:
            la = layer // 2
            wa = (w_in_a[la], nsa_cmp_w1[la], nsa_cmp_b1[la], nsa_cmp_w2[la], nsa_cmp_pe[la],
                  gdn_conv_w[la], gdn_A_log[la], gdn_dt_bias[la], gdn_norm_w[la])
            mp, rc, rs, rw, hc, hs_ = _even_prompt(xp, *wa)
            cmp_p.append(rc); slc_p.append(rs); win_p.append(rw); gconv_p.append(hc); gstate_p.append(hs_)
            ms, rc, rs, rw, hc, hs_ = _even_sample(xs, cache_nsa_cmp_kv, cache_nsa_slc_kv, la,
                                                   state_nsa_win_kv[:, la], state_gdn_conv[:, la],
                                                   state_gdn_S[:, la], page_table, *wa)
            cmp_s.append(rc); slc_s.append(rs); win_s.append(rw); gconv_s.append(hc); gstate_s.append(hs_)
            w_out = w_out_a[la]
        else:
            lc = layer // 2
            xp2, bpf = _dil_prompt(xp, w_in_c[lc], w_out_c[lc], ln_mix_g[layer], ln_mix_b[layer])
            o_dil, bsf = _dil_sample(xs, state_dil_kv[:, lc], past, w_in_c[lc])
            ms = [o_dil.reshape(bs, -1)]
            dil_p.append(bpf); dil_s.append(bsf)
            w_out = w_out_c[lc]
        if layer % 2 == 0:
            xp2 = _matmul_ln(mp, w_out, xp.reshape(lp, d), ln_mix_g[layer], ln_mix_b[layer])
        xs2 = _matmul_ln(ms, w_out, xs.reshape(bs, d), ln_mix_g[layer], ln_mix_b[layer])
        fargs = (ffn_w_in[layer], ffn_conv_w[layer], ffn_conv_b[layer], ffn_w_out[layer],
                 ln_ffn_g[layer], ln_ffn_b[layer])
        xp3, hp = _ffn_seq(xp2, *fargs)
        xs3, hs = _ffn_step(xs2, state_ffn_conv[:, layer], *fargs)
        xp, xs = xp3.reshape(1, lp, d), xs3.reshape(bs, 1, d)
        ffn_p.append(hp[None]); ffn_s.append(hs)

    def stk(lst):
        return jnp.stack(lst, axis=1)

    return (xp, xs, stk(cmp_p), stk(cmp_s), stk(slc_p), stk(slc_s), stk(win_p), stk(win_s),
            stk(gconv_p), stk(gconv_s), stk(gstate_p), stk(gstate_s), stk(dil_p), stk(dil_s),
            stk(ffn_p), stk(ffn_s))
```

```python
import functools

import jax
import jax.numpy as jnp
from jax import lax
from jax.experimental import pallas as pl
from jax.experimental.pallas import tpu as pltpu
import numpy as np

DEPTH = 2
PAGE_SIZE = 128
HEAD_DIM = 64
ROPE_THETA = 10000.0
NSA_HEADS = 8
NSA_KV_HEADS = 2
NSA_GROUP = NSA_HEADS // NSA_KV_HEADS
CMP_BLOCK = 32
SEL_BLOCK = 64
NSA_TOPN = 16
NSA_WINDOW = 512
NSA_QBLOCK = 128
NSA_FORCE = 1.0e4
GDN_HEADS = 8
GDN_CONV = 4
GDN_CHUNK = 64
DIL_HEADS = 16
DIL_GROUPS = ((128, 1), (512, 4), (2048, 16))
DIL_SPAN = 128
DIL_BLOCK = 128
DIL_MAX_WINDOW = 2048
D_FF = 2816
FFN_CONV = 3
DEEPNORM_ALPHA = (2.0 * DEPTH) ** 0.25
LN_EPS = 1e-5
NORM_EPS = 1e-6
NSA_Q_W = NSA_HEADS * HEAD_DIM
NSA_KV_W = NSA_KV_HEADS * HEAD_DIM
GDN_W = GDN_HEADS * HEAD_DIM
DIL_W = DIL_HEADS * HEAD_DIM

LANES = 128
VMEM_LIMIT_BYTES = 56 * 1024 * 1024


def _layer_norm_rows(r, g, b):
    mu = jnp.mean(r, axis=-1, keepdims=True)
    d = r - mu
    var = jnp.mean(d * d, axis=-1, keepdims=True)
    return d * lax.rsqrt(var + LN_EPS) * g + b


def _mm_kernel(x_ref, w_ref, o_ref):
    o_ref[...] = jnp.dot(x_ref[...].astype(jnp.bfloat16), w_ref[...], preferred_element_type=jnp.float32)


def _mm_ln_kernel(*refs):
    n = (len(refs) - 4) // 2
    res_ref, g_ref, b_ref, o_ref = refs[2 * n:]
    acc = sum(jnp.dot(x[...].astype(jnp.bfloat16), w[...], preferred_element_type=jnp.float32)
              for x, w in zip(refs[:n], refs[n:2 * n]))
    o_ref[...] = _layer_norm_rows(DEEPNORM_ALPHA * res_ref[...] + acc, g_ref[...], b_ref[...])


def _row_tile(m):
    return 512 if m % 512 == 0 else m


def _matmul(x, w):
    m, k = x.shape
    n = w.shape[1]
    tm = _row_tile(m)
    tn = n
    return pl.pallas_call(
        _mm_kernel,
        grid=(m // tm, n // tn),
        in_specs=[pl.BlockSpec((tm, k), lambda i, j: (i, 0)),
                  pl.BlockSpec((k, tn), lambda i, j: (0, j))],
        out_specs=pl.BlockSpec((tm, tn), lambda i, j: (i, j)),
        out_shape=jax.ShapeDtypeStruct((m, n), jnp.float32),
        compiler_params=pltpu.CompilerParams(dimension_semantics=("parallel", "arbitrary"),
                                             vmem_limit_bytes=VMEM_LIMIT_BYTES),
        name="matmul",
    )(x, w.astype(jnp.bfloat16))


def _matmul_ln(xs, w, res, g, b):
    m, n = res.shape
    tm = _row_tile(m)
    wb = w.astype(jnp.bfloat16)
    ws, start = [], 0
    for x in xs:
        ws.append(wb[start:start + x.shape[1]])
        start += x.shape[1]
    assert start == w.shape[0]
    row = lambda a: pl.BlockSpec((tm, a.shape[1]), lambda i: (i, 0))
    const = lambda a: pl.BlockSpec(a.shape, lambda i: (0, 0))
    g2, b2 = g.reshape(1, n), b.reshape(1, n)
    return pl.pallas_call(
        _mm_ln_kernel,
        grid=(m // tm,),
        in_specs=[row(x) for x in xs] + [const(wi) for wi in ws] + [row(res), const(g2), const(b2)],
        out_specs=row(res),
        out_shape=jax.ShapeDtypeStruct((m, n), jnp.float32),
        compiler_params=pltpu.CompilerParams(dimension_semantics=("arbitrary",),
                                             vmem_limit_bytes=VMEM_LIMIT_BYTES),
        name="matmul_ln",
    )(*xs, *ws, res, g2, b2)


FFN_CHUNK = D_FF
FFN_NCHUNK = D_FF // FFN_CHUNK


def _ffn_seq_kernel(x_ref, wi_ref, cw_ref, cb_ref, wo_ref, lg_ref, lb_ref, y_ref, hist_ref, carry_ref):
    f32, bf16 = jnp.float32, jnp.bfloat16
    i = pl.program_id(0)
    tm = x_ref.shape[0]
    x = x_ref[...]
    xb = x.astype(bf16)

    @pl.when(i == 0)
    def _():
        carry_ref[...] = jnp.zeros(carry_ref.shape, f32)

    row = lax.broadcasted_iota(jnp.int32, (8, FFN_CHUNK), 0)

    def conv_half(cols):
        u = jnp.dot(xb, wi_ref[:, cols], preferred_element_type=f32)
        prev = carry_ref[:, cols]
        p2, p1 = prev[6:7], prev[7:8]
        r1, r2 = pltpu.roll(u, 1, 0), pltpu.roll(u, 2, 0)
        u1 = jnp.concatenate([jnp.where(row == 0, p1, r1[:8]), r1[8:]], axis=0)
        u2 = jnp.concatenate([jnp.where(row == 0, p2, jnp.where(row == 1, p1, r2[:8])), r2[8:]], axis=0)
        carry_ref[:, cols] = u[tm - 8:]
        hist_ref[:, cols] = u[tm - 8:]
        cw = cw_ref[:, cols]
        return cw[0:1] * u2 + cw[1:2] * u1 + cw[2:3] * u + cb_ref[:, cols]

    acc = None
    for j in range(FFN_NCHUNK):
        a = conv_half(slice(j * FFN_CHUNK, (j + 1) * FFN_CHUNK))
        g = conv_half(slice(D_FF + j * FFN_CHUNK, D_FF + (j + 1) * FFN_CHUNK))
        h = (a * jax.nn.sigmoid(a) * g).astype(bf16)
        part = jnp.dot(h, wo_ref[j * FFN_CHUNK:(j + 1) * FFN_CHUNK, :], preferred_element_type=f32)
        acc = part if acc is None else acc + part
    y_ref[...] = _layer_norm_rows(DEEPNORM_ALPHA * x + acc, lg_ref[...], lb_ref[...])


def _ffn_seq(x, w_in, conv_w, conv_b, w_out, ln_g, ln_b):
    l, d = x.shape
    tm = _row_tile(l)
    cw8 = jnp.zeros((8, 2 * D_FF), jnp.float32).at[:FFN_CONV].set(conv_w)
    cb = conv_b.reshape(1, 2 * D_FF)
    const = lambda a: pl.BlockSpec(a.shape, lambda i: (0,) * a.ndim, pipeline_mode=pl.Buffered(1))
    wi, wo = w_in.astype(jnp.bfloat16), w_out.astype(jnp.bfloat16)
    g2, b2 = ln_g.reshape(1, d), ln_b.reshape(1, d)
    y, hist = pl.pallas_call(
        _ffn_seq_kernel,
        grid=(l // tm,),
        in_specs=[pl.BlockSpec((tm, d), lambda i: (i, 0)), const(wi), const(cw8), const(cb), const(wo),
                  const(g2), const(b2)],
        out_specs=[pl.BlockSpec((tm, d), lambda i: (i, 0)), pl.BlockSpec((8, 2 * D_FF), lambda i: (i, 0))],
        out_shape=[jax.ShapeDtypeStruct((l, d), jnp.float32),
                   jax.ShapeDtypeStruct((l // tm * 8, 2 * D_FF), jnp.float32)],
        scratch_shapes=[pltpu.VMEM((8, 2 * D_FF), jnp.float32)],
        compiler_params=pltpu.CompilerParams(dimension_semantics=("arbitrary",), vmem_limit_bytes=VMEM_LIMIT_BYTES),
        name="ffn_seq",
    )(x, wi, cw8, cb, wo, g2, b2)
    return y, hist[-(FFN_CONV - 1):]


def _ffn_step_kernel(x_ref, h_ref, wa_ref, wg_ref, cwa_ref, cwg_ref, cba_ref, cbg_ref, wo_ref, lg_ref, lb_ref,
                     y_ref, ua_ref, ug_ref, acc_ref):
    j = pl.program_id(0)
    x = x_ref[...]
    xb = x.astype(jnp.bfloat16)
    ua = jnp.dot(xb, wa_ref[...], preferred_element_type=jnp.float32)
    ug = jnp.dot(xb, wg_ref[...], preferred_element_type=jnp.float32)
    ua_ref[...] = ua
    ug_ref[...] = ug
    cwa, cwg = cwa_ref[...], cwg_ref[...]
    a = cwa[0:1] * h_ref[0, 0] + cwa[1:2] * h_ref[1, 0] + cwa[2:3] * ua + cba_ref[...]
    g = cwg[0:1] * h_ref[0, 1] + cwg[1:2] * h_ref[1, 1] + cwg[2:3] * ug + cbg_ref[...]
    h = (a * jax.nn.sigmoid(a) * g).astype(jnp.bfloat16)
    part = jnp.dot(h, wo_ref[...], preferred_element_type=jnp.float32)

    @pl.when(j == 0)
    def _():
        acc_ref[...] = part

    @pl.when(j > 0)
    def _():
        acc_ref[...] += part

    @pl.when(j == pl.num_programs(0) - 1)
    def _():
        y_ref[...] = _layer_norm_rows(DEEPNORM_ALPHA * x + acc_ref[...], lg_ref[...], lb_ref[...])


def _ffn_step(x, hist, w_in, conv_w, conv_b, w_out, ln_g, ln_b):
    b, d = x.shape
    c, nc = FFN_CHUNK, FFN_NCHUNK
    w_in = w_in.astype(jnp.bfloat16)
    cw8 = jnp.zeros((8, 2 * D_FF), jnp.float32).at[:FFN_CONV].set(conv_w)
    cb = conv_b.reshape(1, 2 * D_FF)
    h4 = jnp.transpose(hist, (1, 0, 2)).reshape(2, b, 2, D_FF).transpose(0, 2, 1, 3)
    y, ua, ug = pl.pallas_call(
        _ffn_step_kernel,
        grid=(nc,),
        in_specs=[pl.BlockSpec((b, d), lambda j: (0, 0)),
                  pl.BlockSpec((2, 2, b, c), lambda j: (0, 0, 0, j)),
                  pl.BlockSpec((d, c), lambda j: (0, j)),
                  pl.BlockSpec((d, c), lambda j: (0, j + nc)),
                  pl.BlockSpec((8, c), lambda j: (0, j)),
                  pl.BlockSpec((8, c), lambda j: (0, j + nc)),
                  pl.BlockSpec((1, c), lambda j: (0, j)),
                  pl.BlockSpec((1, c), lambda j: (0, j + nc)),
                  pl.BlockSpec((c, d), lambda j: (j, 0)),
                  pl.BlockSpec((1, d), lambda j: (0, 0)),
                  pl.BlockSpec((1, d), lambda j: (0, 0))],
        out_specs=[pl.BlockSpec((b, d), lambda j: (0, 0)),
                   pl.BlockSpec((b, c), lambda j: (0, j)),
                   pl.BlockSpec((b, c), lambda j: (0, j))],
        out_shape=[jax.ShapeDtypeStruct((b, d), jnp.float32),
                   jax.ShapeDtypeStruct((b, D_FF), jnp.float32),
                   jax.ShapeDtypeStruct((b, D_FF), jnp.float32)],
        scratch_shapes=[pltpu.VMEM((b, d), jnp.float32)],
        compiler_params=pltpu.CompilerParams(dimension_semantics=("arbitrary",),
                                             vmem_limit_bytes=VMEM_LIMIT_BYTES),
        name="ffn_step",
    )(x, h4, w_in, w_in, cw8, cw8, cb, cb, w_out.astype(jnp.bfloat16), ln_g.reshape(1, d), ln_b.reshape(1, d))
    u = jnp.concatenate([ua, ug], axis=-1)
    return y, jnp.concatenate([hist[:, 1:], u[:, None]], axis=1)


NEG = -1e30
NSA_KT = 512
NSA_COLS = NSA_HEADS * NSA_QBLOCK
NSA_WSPAN = NSA_WINDOW + NSA_QBLOCK
LOG2E = 1.4426950408889634
NSA_VROWS = HEAD_DIM + 8


def _lane_tile(x, n):
    return jnp.concatenate([x] * n, axis=1)


def _nsa_prompt_kernel(q_ref, sm_ref, ck_ref, cvt_ref, ks_ref, vst_ref, kw_ref, vwt_ref, hot_ref, o_ref,
                       selb_ref, m_ref, acc_ref, *, ns):
    f32, bf16 = jnp.float32, jnp.bfloat16
    qb = NSA_QBLOCK
    i = pl.program_id(0)
    s0 = i * qb
    half = NSA_COLS // 2

    qt = (q_ref[...] * (HEAD_DIM ** -0.5 * LOG2E)).T
    zero = jnp.zeros((HEAD_DIM, qb), f32)
    top = jnp.concatenate([qt[h * HEAD_DIM:(h + 1) * HEAD_DIM] for h in range(NSA_GROUP)] + [zero] * NSA_GROUP, axis=1)
    bot = jnp.concatenate([zero] * NSA_GROUP + [qt[h * HEAD_DIM:(h + 1) * HEAD_DIM]
                                                for h in range(NSA_GROUP, NSA_HEADS)], axis=1)
    qbd = jnp.concatenate([top, bot], axis=0).astype(bf16)

    def pv(vt, p):
        pb = p.astype(bf16)
        rows = vt.shape[0] // NSA_KV_HEADS
        return [jnp.dot(vt[g * rows:(g + 1) * rows], pb[:, g * half:(g + 1) * half],
                        preferred_element_type=f32) for g in range(NSA_KV_HEADS)]

    nc = 2 * ns
    r = lax.broadcasted_iota(jnp.int32, (nc, qb), 0)
    lane = lax.broadcasted_iota(jnp.int32, (nc, qb), 1)
    cidx = jnp.where(r < ns, 2 * r, 2 * (r - ns) + 1)
    cbias = jnp.where((cidx + 1) * CMP_BLOCK - 1 <= s0 + lane, 0.0, NEG)
    sc = jnp.dot(ck_ref[...], qbd, preferred_element_type=f32) + _lane_tile(cbias, NSA_HEADS)
    m = jnp.max(sc, axis=0, keepdims=True)
    p = jnp.exp2(sc - m)
    pn = p * jnp.where(m > 0.5 * NEG, 1.0 / jnp.sum(p, axis=0, keepdims=True), 0.0)
    o_cmp = pv(cvt_ref[...], pn)

    blk = lax.broadcasted_iota(jnp.int32, (ns, qb), 0)
    qpos = s0 + lax.broadcasted_iota(jnp.int32, (ns, qb), 1)
    cur = qpos // SEL_BLOCK
    forced = (blk == 0) | (blk == cur) | (blk == cur - 1)
    for g in range(NSA_KV_HEADS):
        imp = pn[:, g * half:g * half + qb]
        for h in range(1, NSA_GROUP):
            imp = imp + pn[:, g * half + h * qb:g * half + (h + 1) * qb]
        imp = imp[:ns] + imp[ns:]
        val = jnp.where(blk > cur, -1.0, jnp.where(forced, -jnp.inf, imp))
        bias = jnp.where(forced & (blk <= cur), 0.0, NEG)
        for _ in range(NSA_TOPN - 3):
            top = jnp.max(val, axis=0, keepdims=True)
            pick = jnp.min(jnp.where(val == top, blk, ns), axis=0, keepdims=True)
            hit = blk == pick
            bias = jnp.where(hit, 0.0, bias)
            val = jnp.where(hit, -jnp.inf, val)
        selb_ref[g] = bias

    m_ref[...] = jnp.full(m_ref.shape, NEG, f32)
    acc_ref[...] = jnp.zeros(acc_ref.shape, f32)
    per_tile = NSA_KT // SEL_BLOCK
    zpad = jnp.zeros((LANES - 16, NSA_COLS), bf16)

    def slc_tile(kt, causal):
        k0 = pl.multiple_of(kt * NSA_KT, NSA_KT)
        b0 = pl.multiple_of(kt * per_tile, per_tile)
        brow = jnp.concatenate([selb_ref[g, pl.ds(b0, per_tile), :] for g in range(NSA_KV_HEADS)
                                for _ in range(NSA_GROUP)], axis=1)
        brow = jnp.concatenate([brow, jnp.zeros((16 - per_tile, NSA_COLS), f32)], axis=0).astype(bf16)
        q_aug = jnp.concatenate([qbd, brow, zpad], axis=0)
        k_aug = jnp.concatenate([ks_ref[pl.ds(k0, NSA_KT), :], hot_ref[...]], axis=1)
        s = jnp.dot(k_aug, q_aug, preferred_element_type=f32)
        if causal:
            kpos = k0 + lax.broadcasted_iota(jnp.int32, (NSA_KT, qb), 0)
            qq = s0 + lax.broadcasted_iota(jnp.int32, (NSA_KT, qb), 1)
            s = s + _lane_tile(jnp.where(kpos <= qq, 0.0, NEG), NSA_HEADS)
        m_old = m_ref[...]
        m_new = jnp.maximum(m_old, jnp.max(s, axis=0, keepdims=True))
        alpha = jnp.exp2(m_old - m_new)
        p = jnp.exp2(s - m_new)
        m_ref[...] = m_new
        upd = pv(vst_ref[:, pl.ds(k0, NSA_KT)], p)
        for g in range(NSA_KV_HEADS):
            acc_ref[g] = acc_ref[g] * alpha[:, g * half:(g + 1) * half] + upd[g]

    kd = s0 // NSA_KT

    def body(j, carry):
        slc_tile(2 * j, False)
        slc_tile(2 * j + 1, False)
        return carry

    lax.fori_loop(0, kd // 2, body, 0)

    @pl.when(kd % 2 == 1)
    def _():
        slc_tile(kd - 1, False)

    slc_tile(kd, True)
    inv_slc = [1.0 / acc_ref[g, HEAD_DIM:HEAD_DIM + 1, :] for g in range(NSA_KV_HEADS)]

    w0 = pl.multiple_of(s0, qb)
    sw = jnp.dot(kw_ref[pl.ds(w0, NSA_WSPAN), :], qbd, preferred_element_type=f32)
    rr = lax.broadcasted_iota(jnp.int32, (NSA_WSPAN, qb), 0)
    qi = lax.broadcasted_iota(jnp.int32, (NSA_WSPAN, qb), 1)
    ok = (rr >= qi) & (rr <= qi + NSA_WINDOW) & (rr + s0 >= NSA_WINDOW)
    sw = sw + _lane_tile(jnp.where(ok, 0.0, NEG), NSA_HEADS)
    pw = jnp.exp2(sw - jnp.max(sw, axis=0, keepdims=True))
    o_win = pv(vwt_ref[:, pl.ds(w0, NSA_WSPAN)], pw)
    inv_win = [1.0 / o[HEAD_DIM:HEAD_DIM + 1] for o in o_win]

    gt = jax.nn.sigmoid(sm_ref[...].T)
    outs = []
    for h in range(NSA_HEADS):
        g, hg = divmod(h, NSA_GROUP)
        c0, c1 = hg * qb, (hg + 1) * qb
        g_cmp = gt[h:h + 1]
        g_slc = gt[NSA_HEADS + h:NSA_HEADS + h + 1] * inv_slc[g][:, c0:c1]
        g_win = gt[2 * NSA_HEADS + h:2 * NSA_HEADS + h + 1] * inv_win[g][:, c0:c1]
        outs.append(o_cmp[g][:, c0:c1] * g_cmp + acc_ref[g, :HEAD_DIM, c0:c1] * g_slc
                    + o_win[g][:HEAD_DIM, c0:c1] * g_win)
    o_ref[...] = jnp.concatenate(outs, axis=0).T


def _nsa_prompt_attention(qr, small, ck, cv, ksr, vs, kwr, vw):
    l = qr.shape[0]
    ns = l // SEL_BLOCK
    assert ns >= NSA_TOPN and l % NSA_KT == 0
    nc = 2 * ns
    bf16 = jnp.bfloat16
    perm =jnp.concatenate([jnp.arange(0, nc, 2), jnp.arange(1, nc, 2)])
    ckp = ck[perm].astype(bf16)
    cvt = cv[perm].T.astype(bf16)
    pad = jnp.zeros((NSA_WINDOW, NSA_KV_W), bf16)
    kwp = jnp.concatenate([pad, kwr.astype(bf16)], axis=0)
    def with_ones(vt):
        n = vt.shape[1]
        extra = jnp.concatenate([jnp.ones((1, n), bf16), jnp.zeros((NSA_VROWS - HEAD_DIM - 1, n), bf16)], axis=0)
        return jnp.concatenate([x for g in range(NSA_KV_HEADS) for x in (vt[g * HEAD_DIM:(g + 1) * HEAD_DIM], extra)], axis=0)

    vwt = with_ones(jnp.concatenate([pad, vw.astype(bf16)], axis=0).T)
    hot = (jnp.arange(NSA_KT)[:, None] // SEL_BLOCK == jnp.arange(LANES)[None, :]).astype(bf16)
    full = lambda a: pl.BlockSpec(a.shape, lambda i: (0,) * a.ndim)
    args = (qr, small, ckp, cvt, ksr.astype(bf16), with_ones(vs.T.astype(bf16)), kwp, vwt, hot)
    return pl.pallas_call(
        functools.partial(_nsa_prompt_kernel, ns=ns),
        grid=(l // NSA_QBLOCK,),
        in_specs=[pl.BlockSpec((NSA_QBLOCK, NSA_Q_W), lambda i: (i, 0)),
                  pl.BlockSpec((NSA_QBLOCK, LANES), lambda i: (i, 0))] + [full(a) for a in args[2:]],
        out_specs=pl.BlockSpec((NSA_QBLOCK, NSA_Q_W), lambda i: (i, 0)),
        out_shape=jax.ShapeDtypeStruct((l, NSA_Q_W), jnp.float32),
        scratch_shapes=[pltpu.VMEM((NSA_KV_HEADS, ns, NSA_QBLOCK), jnp.float32),
                        pltpu.VMEM((1, NSA_COLS), jnp.float32),
                        pltpu.VMEM((NSA_KV_HEADS, NSA_VROWS, NSA_COLS // 2), jnp.float32)],
        compiler_params=pltpu.CompilerParams(dimension_semantics=("arbitrary",),
                                             vmem_limit_bytes=VMEM_LIMIT_BYTES),
        name="nsa_prompt",
    )(*args)


CMP_HIDDEN = 2 * HEAD_DIM
CMP_PER_PAGE = PAGE_SIZE // CMP_BLOCK
CMP_PAGES_PER_STEP = 64
STEP_PAGES = 64
STEP_ROWS = 16


def _rope_tables(pos, width):
    half = HEAD_DIM // 2
    inv_freq = ROPE_THETA ** (-2.0 * jnp.arange(half, dtype=jnp.float32) / HEAD_DIM)
    ang = pos.astype(jnp.float32)[:, None] * inv_freq[None, :]
    cos, sin = jnp.cos(ang), jnp.sin(ang)
    reps = width // HEAD_DIM
    return (jnp.tile(jnp.concatenate([cos, cos], axis=1), (1, reps)),
            jnp.tile(jnp.concatenate([-sin, sin], axis=1), (1, reps)))


def _rope_lanes(x, cos, sin_signed):
    n = x.shape[-1]
    lane = lax.broadcasted_iota(jnp.int32, x.shape, x.ndim - 1)
    first = (lane % HEAD_DIM) < HEAD_DIM // 2
    partner = jnp.where(first, pltpu.roll(x, n - HEAD_DIM // 2, x.ndim - 1), pltpu.roll(x, HEAD_DIM // 2, x.ndim - 1))
    return x * cos + partner * sin_signed


def _nt_dot(a, b):
    return lax.dot_general(a, b, (((1,), (1,)), ((), ())), preferred_element_type=jnp.float32)


CMP_STEP_LANES = CMP_PAGES_PER_STEP * CMP_PER_PAGE
CMP_FEATURE_GROUP = 16


def _cmp_lane_blocks(n_pages):
    lane = np.arange(n_pages * CMP_PER_PAGE)
    step, rem = lane // CMP_STEP_LANES, lane % CMP_STEP_LANES
    j, pl_ = rem // CMP_PAGES_PER_STEP, rem % CMP_PAGES_PER_STEP
    return (step * CMP_PAGES_PER_STEP + pl_) * CMP_PER_PAGE + j


def _cmp_step_kernel(pt_ref, *refs):
    f32 = jnp.float32
    npg = CMP_PAGES_PER_STEP
    pages = refs[:npg]
    pe_ref, w1_ref, b1_ref, w2_ref, cos_ref, sin_ref, ck_ref, cv_ref, slab_ref = refs[npg:]
    outs = (ck_ref, cv_ref)
    for k, r in enumerate(pages):
        for s in range(2 * NSA_KV_HEADS):
            slab_ref[s, k * HEAD_DIM:(k + 1) * HEAD_DIM, :] = r[0, s]
    for kv in range(2):
        h = jnp.zeros((NSA_KV_HEADS * npg, CMP_PER_PAGE * CMP_HIDDEN), f32)
        xt = [jnp.swapaxes(slab_ref[2 * kv + g].reshape(npg, HEAD_DIM, PAGE_SIZE), 0, 1) for g in range(NSA_KV_HEADS)]
        for dg in range(HEAD_DIM // CMP_FEATURE_GROUP):
            x = jnp.concatenate(
                [jnp.concatenate([xt[g][d] for g in range(NSA_KV_HEADS)], axis=0) + pe_ref[kv, d]
                 for d in range(dg * CMP_FEATURE_GROUP, (dg + 1) * CMP_FEATURE_GROUP)], axis=1)
            h = h + jnp.dot(x.astype(jnp.bfloat16), w1_ref[kv, dg], preferred_element_type=f32)
        act = jax.nn.gelu(h + b1_ref[kv]).astype(jnp.bfloat16)
        ct = _nt_dot(w2_ref[kv], act)
        tile = jnp.concatenate(
            [jnp.concatenate([ct[j * HEAD_DIM:(j + 1) * HEAD_DIM, g * npg:(g + 1) * npg] for j in range(CMP_PER_PAGE)],
                             axis=1) for g in range(NSA_KV_HEADS)], axis=0)
        if kv == 0:
            row = lax.broadcasted_iota(jnp.int32, tile.shape, 0)
            n = tile.shape[0]
            partner = jnp.where((row % HEAD_DIM) < HEAD_DIM // 2, pltpu.roll(tile, n - HEAD_DIM // 2, 0),
                                pltpu.roll(tile, HEAD_DIM // 2, 0))
            tile = tile * cos_ref[...] + partner * sin_ref[...]
        outs[kv][0] = tile


def _compress_weights(cw1, cb1, cw2, cpe):
    eye = jnp.eye(CMP_PER_PAGE, dtype=jnp.float32)
    w1r = cw1.reshape(2, CMP_BLOCK, HEAD_DIM, CMP_HIDDEN)
    w1 = jnp.einsum('ktdn,ja->kdjtan', w1r, eye).reshape(
        2, HEAD_DIM // CMP_FEATURE_GROUP, CMP_FEATURE_GROUP * PAGE_SIZE, CMP_PER_PAGE * CMP_HIDDEN)
    b1 = jnp.tile(cb1, (1, CMP_PER_PAGE))[:, None, :]
    w2 = jnp.einsum('knd,ja->kjdan', cw2, eye).reshape(2, CMP_PER_PAGE * HEAD_DIM, CMP_PER_PAGE * CMP_HIDDEN)
    pe = jnp.tile(cpe.transpose(0, 2, 1), (1, 1, CMP_PER_PAGE))[:, :, None, :]
    return w1.astype(jnp.bfloat16), b1, w2.astype(jnp.bfloat16), pe


def _nsa_sample_compress(pool, layer_idx, page_table, cw1, cb1, cw2, cpe):
    n_pool, nl = pool.shape[:2]
    db, n_pages = page_table.shape
    npg = CMP_PAGES_PER_STEP
    nchunk = n_pages // npg
    nc = n_pages * CMP_PER_PAGE
    view = jnp.transpose(pool, (0, 1, 3, 4, 5, 2)).reshape(n_pool * nl, 2 * NSA_KV_HEADS, HEAD_DIM, PAGE_SIZE)
    pt = (page_table * nl + layer_idx).reshape(-1).astype(jnp.int32)
    w1, b1, w2, pe = _compress_weights(cw1, cb1, cw2, cpe)
    pos = (jnp.asarray(_cmp_lane_blocks(n_pages)) + 1) * CMP_BLOCK - 1
    half = HEAD_DIM // 2
    inv_freq = ROPE_THETA ** (-2.0 * jnp.arange(half, dtype=jnp.float32) / HEAD_DIM)
    ang = inv_freq[:, None] * pos.astype(jnp.float32)[None, :]
    cos = jnp.tile(jnp.cos(ang), (2 * NSA_KV_HEADS, 1))
    sin = jnp.tile(jnp.concatenate([-jnp.sin(ang), jnp.sin(ang)], axis=0), (NSA_KV_HEADS, 1))

    def page_map(k):
        return lambda b, c, pt_ref: (pt_ref[b * n_pages + c * npg + k], 0, 0, 0)

    const = lambda a: pl.BlockSpec(a.shape, lambda b, c, pt_ref: (0,) * a.ndim, pipeline_mode=pl.Buffered(1))
    lanes_c = lambda: pl.BlockSpec((NSA_KV_W, CMP_STEP_LANES), lambda b, c, pt_ref: (0, c))
    grid_spec = pltpu.PrefetchScalarGridSpec(
        num_scalar_prefetch=1, grid=(db, nchunk),
        in_specs=[pl.BlockSpec((1, 2 * NSA_KV_HEADS, HEAD_DIM, PAGE_SIZE), page_map(k)) for k in range(npg)]
        + [const(pe), const(w1), const(b1), const(w2), lanes_c(), lanes_c()],
        out_specs=[pl.BlockSpec((1, NSA_KV_W, CMP_STEP_LANES), lambda b, c, pt_ref: (b, 0, c))] * 2,
        scratch_shapes=[pltpu.VMEM((2 * NSA_KV_HEADS, npg * HEAD_DIM, PAGE_SIZE), jnp.float32)])
    return pl.pallas_call(
        _cmp_step_kernel, grid_spec=grid_spec,
        out_shape=[jax.ShapeDtypeStruct((db, NSA_KV_W, nc), jnp.float32)] * 2,
        compiler_params=pltpu.CompilerParams(dimension_semantics=("arbitrary", "arbitrary"),
                                             vmem_limit_bytes=VMEM_LIMIT_BYTES),
        name="nsa_sample_compress",
    )(pt, *([view] * npg), pe, w1, b1, w2, cos, sin)


def _nsa_step_kernel(pt_ref, *refs, topn):
    f32, bf16 = jnp.float32, jnp.bfloat16
    npg = STEP_PAGES
    q_ref, ck_ref, cv_ref = refs[:3]
    pages = refs[3:3 + npg]
    (win_ref, newr_ref, newt_ref, gate_ref, blk_ref, exp_ref, o_ref, wout_ref,
     selt_ref, m_ref, l_ref, acc_ref, side_ref) = refs[3 + npg:]
    cc = pl.program_id(1)
    q16 = q_ref[0]
    qb = q16.astype(bf16)
    row16 = lax.broadcasted_iota(jnp.int32, (STEP_ROWS, 1), 0)

    def new_key_scores(krow):
        return jnp.sum(q16 * krow, axis=1, keepdims=True)

    @pl.when(cc == 0)
    def _():
        nc = ck_ref.shape[2]
        s = jnp.dot(qb, ck_ref[0].astype(bf16), preferred_element_type=f32)
        p = jnp.exp(s - jnp.max(s, axis=1, keepdims=True))
        pn = p / jnp.sum(p, axis=1, keepdims=True)
        o_cmp = _nt_dot(pn.astype(bf16), cv_ref[0].astype(bf16))

        rowp = lax.broadcasted_iota(jnp.int32, pn.shape, 0)
        row8 = lax.broadcasted_iota(jnp.int32, (8, nc), 0)
        blk = blk_ref[...]
        val = jnp.full((8, nc), -jnp.inf, f32)
        for g in range(NSA_KV_HEADS):
            ig = jnp.sum(jnp.where((rowp >= g * NSA_GROUP) & (rowp < (g + 1) * NSA_GROUP), pn, 0.0),
                         axis=0, keepdims=True)
            ig = ig + pltpu.roll(ig, nc - CMP_PAGES_PER_STEP, 1)
            vg = jnp.where(blk[:1] < 0, -jnp.inf, jnp.where(blk[1:2] > 0, NSA_FORCE, ig))
            val = jnp.where(row8 == g, vg, val)
        sblk = jnp.where(blk[:1] < 0, nc, blk[:1])
        sel = jnp.zeros((8, nc), f32)
        for _ in range(topn):
            top = jnp.max(val, axis=1, keepdims=True)
            pick = jnp.min(jnp.where(val == top, sblk, nc), axis=1, keepdims=True)
            hit = sblk == pick
            sel = jnp.where(hit, 1.0, sel)
            val = jnp.where(hit, -jnp.inf, val)
        selh = jnp.where(row16 < NSA_GROUP, sel[0:1], jnp.where(row16 < NSA_HEADS, sel[1:2], 0.0))
        wsel = selt_ref.shape[2]
        for j in range(selt_ref.shape[0]):
            selt_ref[j] = selh[:, j * wsel:(j + 1) * wsel]

        m_ref[...] = new_key_scores(newr_ref[0, 0:1, :])
        l_ref[...] = jnp.ones(l_ref.shape, f32)
        acc_ref[...] = jnp.broadcast_to(newr_ref[0, 1:2, :], acc_ref.shape)

        sw = jnp.dot(qb, win_ref[0, 0].astype(bf16), preferred_element_type=f32)
        sn = new_key_scores(newr_ref[0, 2:3, :])
        mw = jnp.maximum(jnp.max(sw, axis=1, keepdims=True), sn)
        pw, pnw = jnp.exp(sw - mw), jnp.exp(sn - mw)
        lw = jnp.sum(pw, axis=1, keepdims=True) + pnw
        o_win = (_nt_dot(pw.astype(bf16), win_ref[0, 1].astype(bf16)) + pnw * newr_ref[0, 3:4, :]) / lw
        gt = jax.nn.sigmoid(gate_ref[0])
        side_ref[...] = gt[:, 0:1] * o_cmp + gt[:, 2:3] * o_win
        wl = win_ref.shape[3]
        lane = lax.broadcasted_iota(jnp.int32, (NSA_KV_W, wl), 1)
        for kv in range(2):
            wout_ref[0, kv] = jnp.where(lane == wl - 1, newt_ref[0, :, 2 + kv:3 + kv],
                                        pltpu.roll(win_ref[0, kv], wl - 1, 1))

    kt = jnp.concatenate([r[0, 0] for r in pages], axis=1)
    vt = jnp.concatenate([r[0, 1] for r in pages], axis=1)
    s = jnp.dot(qb, kt.astype(bf16), preferred_element_type=f32)
    picked = jnp.dot(selt_ref[cc].astype(bf16), exp_ref[...], preferred_element_type=f32)
    s = s + (picked - 1.0) * (-NEG)
    m_old = m_ref[...]
    m_new = jnp.maximum(m_old, jnp.max(s, axis=1, keepdims=True))
    alpha = jnp.exp(m_old - m_new)
    p = jnp.exp(s - m_new)
    m_ref[...] = m_new
    l_ref[...] = l_ref[...] * alpha + jnp.sum(p, axis=1, keepdims=True)
    acc_ref[...] = acc_ref[...] * alpha + _nt_dot(p.astype(bf16), vt.astype(bf16))

    @pl.when(cc == pl.num_programs(1) - 1)
    def _():
        gt = jax.nn.sigmoid(gate_ref[0])
        o_ref[0] = side_ref[...] + gt[:, 1:2] * acc_ref[...] / l_ref[...]


def _nsa_sample_attention(qr, ckt, cvt, slc_pool, layer_idx, page_table, win_buf, newrows, gate):
    db, n_pages = page_table.shape
    n_pool, nl = slc_pool.shape[:2]
    wlen = win_buf.shape[1]
    nc = ckt.shape[2]
    past = n_pages * PAGE_SIZE
    cur = past // SEL_BLOCK
    npg = STEP_PAGES
    nchunk = n_pages // npg
    keys = npg * PAGE_SIZE
    wsel = npg * CMP_PER_PAGE
    f32, bf16 = jnp.float32, jnp.bfloat16
    view = jnp.transpose(slc_pool, (0, 1, 3, 4, 5, 2)).reshape(n_pool * nl, 2, NSA_KV_W, PAGE_SIZE)
    wint = jnp.transpose(win_buf, (0, 2, 3, 4, 1)).reshape(db, 2, NSA_KV_W, wlen)
    pt = (page_table * nl + layer_idx).reshape(-1).astype(jnp.int32)
    hmask = (jnp.arange(NSA_HEADS)[:, None] // NSA_GROUP == jnp.arange(NSA_KV_HEADS)[None, :]).astype(f32)
    q16 = (qr * HEAD_DIM ** -0.5)[:, :, None, :] * hmask[None, :, :, None]
    q16 = jnp.pad(q16.reshape(db, NSA_HEADS, NSA_KV_W), ((0, 0), (0, STEP_ROWS - NSA_HEADS), (0, 0)))
    newr = jnp.pad(newrows, ((0, 0), (0, 8 - newrows.shape[1]), (0, 0)))
    newt = jnp.pad(newrows.transpose(0, 2, 1), ((0, 0), (0, 0), (0, LANES - newrows.shape[1])))
    g16 = jnp.pad(gate.reshape(db, 3, NSA_HEADS).transpose(0, 2, 1),
                  ((0, 0), (0, STEP_ROWS - NSA_HEADS), (0, LANES - 3)))
    cblk = _cmp_lane_blocks(n_pages)
    jj = cblk % CMP_PER_PAGE
    sblk = np.where(jj % 2 == 0, cblk // 2, -1)
    forced = ((sblk == 0) | (sblk == cur - 1)).astype(np.int32)
    blk8 = np.zeros((8, nc), np.int32)
    blk8[0], blk8[1] = sblk, forced
    loc = np.arange(wsel)
    lstep, lrem = loc // CMP_STEP_LANES, loc % CMP_STEP_LANES
    lj, lpage = lrem // CMP_PAGES_PER_STEP, lstep * CMP_PAGES_PER_STEP + lrem % CMP_PAGES_PER_STEP
    kidx = np.arange(keys)
    expand = ((lj[:, None] % 2 == 0) & (kidx[None, :] // PAGE_SIZE == lpage[:, None])
              & ((kidx[None, :] % PAGE_SIZE) // SEL_BLOCK == lj[:, None] // 2)).astype(np.float32)
    topn = min(NSA_TOPN, cur + 1) - 1

    def page_map(k):
        return lambda b, c, pt_ref: (pt_ref[b * n_pages + c * npg + k], 0, 0, 0)

    per_b = lambda shp: pl.BlockSpec((1,) + shp, lambda b, c, pt_ref: (b,) + (0,) * len(shp))
    const = lambda a: pl.BlockSpec(a.shape, lambda b, c, pt_ref: (0,) * a.ndim)
    consts = (jnp.asarray(blk8), jnp.asarray(expand, bf16))
    grid_spec = pltpu.PrefetchScalarGridSpec(
        num_scalar_prefetch=1, grid=(db, nchunk),
        in_specs=[per_b((STEP_ROWS, NSA_KV_W)), per_b((NSA_KV_W, nc)), per_b((NSA_KV_W, nc))]
        + [pl.BlockSpec((1, 2, NSA_KV_W, PAGE_SIZE), page_map(k)) for k in range(npg)]
        + [per_b((2, NSA_KV_W, wlen)), per_b((8, NSA_KV_W)), per_b((NSA_KV_W, LANES)), per_b((STEP_ROWS, LANES))]
        + [const(a) for a in consts],
        out_specs=[per_b((STEP_ROWS, NSA_KV_W)), per_b((2, NSA_KV_W, wlen))],
        scratch_shapes=[pltpu.VMEM((nc // wsel, STEP_ROWS, wsel), f32),
                        pltpu.VMEM((STEP_ROWS, 1), f32), pltpu.VMEM((STEP_ROWS, 1), f32),
                        pltpu.VMEM((STEP_ROWS, NSA_KV_W), f32), pltpu.VMEM((STEP_ROWS, NSA_KV_W), f32)])
    o16, wout = pl.pallas_call(
        functools.partial(_nsa_step_kernel, topn=topn), grid_spec=grid_spec,
        out_shape=[jax.ShapeDtypeStruct((db, STEP_ROWS, NSA_KV_W), f32),
                   jax.ShapeDtypeStruct((db, 2, NSA_KV_W, wlen), f32)],
        compiler_params=pltpu.CompilerParams(dimension_semantics=("arbitrary", "arbitrary"),
                                             vmem_limit_bytes=VMEM_LIMIT_BYTES),
        name="nsa_sample_attention",
    )(pt, q16, ckt, cvt, *([view] * npg), wint, newr, newt, g16, *consts)
    o = o16[:, :NSA_HEADS].reshape(db, NSA_HEADS, NSA_KV_HEADS, HEAD_DIM)
    o = jnp.take_along_axis(o, (jnp.arange(NSA_HEADS) // NSA_GROUP)[None, :, None, None], axis=2)
    wout = jnp.transpose(wout.reshape(db, 2, NSA_KV_HEADS, HEAD_DIM, wlen), (0, 4, 1, 2, 3))
    return o.reshape(db, NSA_HEADS * HEAD_DIM), wout


DIL_ROW_CHUNK = 64


def _dil_step_kernel(q_ref, buf_ref, newt_ref, newr_ref, bias_ref, o_ref, out_ref, p_ref, pn_ref, den_ref):
    f32, bf16 = jnp.float32, jnp.bfloat16
    kv = pl.program_id(1)
    wlen = buf_ref.shape[3]
    nrow = buf_ref.shape[2]
    q16 = q_ref[0]

    @pl.when(kv == 0)
    def _():
        s = jnp.dot(q16.astype(bf16), buf_ref[0, 0].astype(bf16), preferred_element_type=f32)
        s_new = jnp.sum(q16 * newr_ref[0, 0:1, :], axis=1, keepdims=True)
        ms, es, ens, dens = [], [], [], []
        for g in range(len(DIL_GROUPS)):
            sg = s + bias_ref[g:g + 1, :]
            m = jnp.maximum(jnp.max(sg, axis=1, keepdims=True), s_new)
            e, en = jnp.exp(sg - m), jnp.exp(s_new - m)
            ms.append(m); es.append(e); ens.append(en)
            dens.append(jnp.sum(e, axis=1, keepdims=True) + en)
        m_all = functools.reduce(jnp.maximum, ms)
        ws = [jnp.exp(m - m_all) for m in ms]
        p_ref[...] = sum(w * e for w, e in zip(ws, es))
        pn_ref[...] = sum(w * en for w, en in zip(ws, ens))
        den_ref[...] = sum(w * d for w, d in zip(ws, dens))

    @pl.when(kv == 1)
    def _():
        r = _nt_dot(p_ref[...].astype(bf16), buf_ref[0, 0].astype(bf16))
        r = (r + pn_ref[...] * newr_ref[0, 1:2, :]) / den_ref[...]
        head = lax.broadcasted_iota(jnp.int32, r.shape, 1) // HEAD_DIM
        row = lax.broadcasted_iota(jnp.int32, r.shape, 0)
        o_ref[0] = jnp.broadcast_to(jnp.sum(jnp.where(head == row, r, 0.0), axis=0, keepdims=True), o_ref.shape[1:])

    lane = lax.broadcasted_iota(jnp.int32, (DIL_ROW_CHUNK, wlen), 1)
    for c in range(nrow // DIL_ROW_CHUNK):
        rs = slice(c * DIL_ROW_CHUNK, (c + 1) * DIL_ROW_CHUNK)
        col = jnp.where(kv == 0, newt_ref[0, rs, 0:1], newt_ref[0, rs, 1:2])
        out_ref[0, 0, rs, :] = jnp.where(lane == wlen - 1, col, pltpu.roll(buf_ref[0, 0, rs, :], wlen - 1, 1))


def _dil_sample_attention(qr, kr_new, v_new, buf):
    db, wlen = buf.shape[:2]
    f32 = jnp.float32
    buft = jnp.transpose(buf, (0, 2, 3, 4, 1)).reshape(db, 2, DIL_W, wlen)
    eye = jnp.eye(DIL_HEADS, dtype=f32)
    q16 = ((qr * HEAD_DIM ** -0.5)[:, :, None, :] * eye[None, :, :, None]).reshape(db, DIL_HEADS, DIL_W)
    newr = jnp.pad(jnp.stack([kr_new, v_new], axis=1), ((0, 0), (0, 6), (0, 0)))
    newt = jnp.pad(jnp.stack([kr_new, v_new], axis=2), ((0, 0), (0, 0), (0, LANES - 2)))
    back = wlen - jnp.arange(wlen)
    bias = jnp.stack([jnp.where((back % d == 0) & (back // d <= DIL_SPAN), 0.0, NEG) for _, d in DIL_GROUPS])
    bias = jnp.pad(bias, ((0, 8 - len(DIL_GROUPS)), (0, 0))).astype(f32)
    o, new_buf = pl.pallas_call(
        _dil_step_kernel,
        grid=(db, 2),
        in_specs=[pl.BlockSpec((1, DIL_HEADS, DIL_W), lambda b, k: (b, 0, 0)),
                  pl.BlockSpec((1, 1, DIL_W, wlen), lambda b, k: (b, k, 0, 0)),
                  pl.BlockSpec((1, DIL_W, LANES), lambda b, k: (b, 0, 0)),
                  pl.BlockSpec((1, 8, DIL_W), lambda b, k: (b, 0, 0)),
                  pl.BlockSpec((8, wlen), lambda b, k: (0, 0))],
        out_specs=[pl.BlockSpec((1, 8, DIL_W), lambda b, k: (b, 0, 0)),
                   pl.BlockSpec((1, 1, DIL_W, wlen), lambda b, k: (b, k, 0, 0))],
        out_shape=[jax.ShapeDtypeStruct((db, 8, DIL_W), f32),
                   jax.ShapeDtypeStruct((db, 2, DIL_W, wlen), f32)],
        scratch_shapes=[pltpu.VMEM((DIL_HEADS, wlen), f32), pltpu.VMEM((DIL_HEADS, 1), f32),
                        pltpu.VMEM((DIL_HEADS, 1), f32)],
        compiler_params=pltpu.CompilerParams(dimension_semantics=("arbitrary", "arbitrary"),
                                             vmem_limit_bytes=VMEM_LIMIT_BYTES),
        name="dil_sample",
    )(q16, buft, newt, newr, bias)
    new_buf = jnp.transpose(new_buf.reshape(db, 2, DIL_HEADS, HEAD_DIM, wlen), (0, 4, 1, 2, 3))
    return o[:, 0], new_buf


def _dil_band_kernel(q_ref, kp_ref, kc_ref, vp_ref, vc_ref, num_ref, st_ref):
    f32, bf16 = jnp.float32, jnp.bfloat16
    blk = DIL_BLOCK
    n = pl.program_id(0)
    i = lax.broadcasted_iota(jnp.int32, (blk, 2 * blk), 0)
    j = lax.broadcasted_iota(jnp.int32, (blk, 2 * blk), 1) - blk
    ok = (i - j >= 0) & (i - j <= DIL_SPAN) & (n * blk + j >= 0)
    bias = jnp.where(ok, 0.0, NEG)
    bias = jnp.concatenate([bias, bias], axis=0)
    lane = lax.broadcasted_iota(jnp.int32, (blk, LANES), 1)
    first = lane < HEAD_DIM
    stats = jnp.zeros((blk, LANES), f32)
    for p in range(DIL_HEADS // 2):
        cols = slice(p * LANES, (p + 1) * LANES)
        qp = q_ref[:, cols] * (HEAD_DIM ** -0.5)
        qst = jnp.concatenate([jnp.where(first, qp, 0.0), jnp.where(first, 0.0, qp)], axis=0).astype(bf16)
        kk = jnp.concatenate([kp_ref[:, cols], kc_ref[:, cols]], axis=0).astype(bf16)
        vv = jnp.concatenate([vp_ref[:, cols], vc_ref[:, cols]], axis=0).astype(bf16)
        s = _nt_dot(qst, kk) + bias
        m = jnp.max(s, axis=1, keepdims=True)
        e = jnp.exp(s - m)
        den = jnp.sum(e, axis=1, keepdims=True)
        nm = jnp.dot(e.astype(bf16), vv, preferred_element_type=f32)
        num_ref[:, cols] = jnp.where(first, nm[:blk], nm[blk:])
        for a in range(2):
            h = 2 * p + a
            stats = jnp.where(lane == h, m[a * blk:(a + 1) * blk], stats)
            stats = jnp.where(lane == DIL_HEADS + h, den[a * blk:(a + 1) * blk], stats)
    st_ref[...] = stats


GDN_PREP_ROWS = 512
GDN_TILE_CHUNKS = 4
GDN_PAIRS = GDN_HEADS // 2
GDN_A_LANE = 3 * NSA_HEADS
GDN_B_LANE = GDN_A_LANE + GDN_HEADS


def _hi_lo(x):
    hi = x.astype(jnp.bfloat16)
    return hi, (x - hi.astype(jnp.float32)).astype(jnp.bfloat16)


def _three_way(x):
    f32 = jnp.float32
    x1 = x.astype(jnp.bfloat16)
    r1 = x - x1.astype(f32)
    x2 = r1.astype(jnp.bfloat16)
    return x1, x2, (r1 - x2.astype(f32)).astype(jnp.bfloat16)


def _dot_select(x, sel):
    return sum(jnp.dot(piece, sel, preferred_element_type=jnp.float32) for piece in _three_way(x))


def _select_dot(sel, x):
    return sum(jnp.dot(sel, piece, preferred_element_type=jnp.float32) for piece in _three_way(x))


def _dot_hl(a, b):
    f32 = jnp.float32
    ah, al = _hi_lo(a)
    bh, bl = _hi_lo(b)
    return (jnp.dot(ah, bh, preferred_element_type=f32) + jnp.dot(ah, bl, preferred_element_type=f32)
            + jnp.dot(al, bh, preferred_element_type=f32))


def _gdn_prep_kernel(u_ref, sm_ref, cw_ref, prm_ref, ea_ref, eb_ref, eh_ref, q_ref, k_ref, v_ref, g_ref, b_ref,
                     carry_ref):
    f32 = jnp.float32
    i = pl.program_id(0)
    tl = u_ref.shape[0]

    @pl.when(i == 0)
    def _():
        carry_ref[...] = jnp.zeros(carry_ref.shape, f32)

    u = u_ref[...]
    prev = carry_ref[...]
    row = lax.broadcasted_iota(jnp.int32, u.shape, 0)

    def shifted(k):
        r = pltpu.roll(u, k, 0)
        for j in range(k):
            r = jnp.where(row == j, prev[8 - k + j:8 - k + j + 1], r)
        return r

    cw = cw_ref[...]
    c = cw[0:1] * shifted(3) + cw[1:2] * shifted(2) + cw[2:3] * shifted(1) + cw[3:4] * u
    carry_ref[...] = u[tl - 8:]
    c = c * jax.nn.sigmoid(c)
    eh = eh_ref[...]

    def l2n(x):
        return x * lax.rsqrt(_dot_select(x * x, eh) + NORM_EPS)

    q_ref[...] = l2n(c[:, :GDN_W]) * (HEAD_DIM ** -0.5)
    k_ref[...] = l2n(c[:, GDN_W:2 * GDN_W])
    v_ref[...] = c[:, 2 * GDN_W:]
    sm = sm_ref[...]
    x = sm + prm_ref[1:2]
    softplus = jnp.maximum(x, 0.0) + jnp.log(1.0 + jnp.exp(-jnp.abs(x)))
    g_ref[...] = _dot_select(-jnp.exp(prm_ref[0:1]) * softplus, ea_ref[...])
    b_ref[...] = _dot_select(jax.nn.sigmoid(sm), eb_ref[...])


def _gdn_chunk_kernel(q_ref, k_ref, v_ref, g_ref, b_ref, z_ref, nw_ref, lt_ref, eh_ref, o_ref, s_out_ref, s_ref):
    f32 = jnp.float32
    ch = GDN_CHUNK
    i = pl.program_id(0)

    @pl.when(i == 0)
    def _():
        s_ref[...] = jnp.zeros(s_ref.shape, f32)

    lane = lax.broadcasted_iota(jnp.int32, (ch, LANES), 1)
    first = lane < HEAD_DIM
    stack = lambda x: jnp.concatenate([jnp.where(first, x, 0.0), jnp.where(first, 0.0, x)], axis=0)
    r2 = lax.broadcasted_iota(jnp.int32, (2 * ch, 2 * ch), 0)
    c2 = lax.broadcasted_iota(jnp.int32, (2 * ch, 2 * ch), 1)
    same = (r2 // ch) == (c2 // ch)
    tri = same & (r2 % ch >= c2 % ch)
    strict = same & (r2 % ch > c2 % ch)
    eye = r2 == c2
    eye_f = jnp.where(eye, 1.0, 0.0)
    diag2 = lax.broadcasted_iota(jnp.int32, (ch, LANES), 0) == lane % HEAD_DIM
    lt = lt_ref[...]
    bf = lambda x: x.astype(jnp.bfloat16)
    dot = lambda a, b: jnp.dot(bf(a), bf(b), preferred_element_type=f32)

    blocks = [(c, p) for c in range(GDN_TILE_CHUNKS) for p in range(GDN_PAIRS)]
    ld = lambda ref, c, p: ref[c * ch:(c + 1) * ch, p * LANES:(p + 1) * LANES]
    gcs = [_select_dot(lt, ld(g_ref, c, p)) for c, p in blocks]
    amats, qks, rhs_u, rhs_w, qgs, kds, decs = [], [], [], [], [], [], []
    for (c, p), gc in zip(blocks, gcs):
        kk, qq, vv, bb = ld(k_ref, c, p), ld(q_ref, c, p), ld(v_ref, c, p), ld(b_ref, c, p)
        eg = jnp.exp(gc)
        g_end = gc[ch - 1:ch]
        kb = kk * bb
        col = jnp.concatenate([jnp.broadcast_to(gc[:, 0:1], (ch, LANES)),
                               jnp.broadcast_to(gc[:, HEAD_DIM:HEAD_DIM + 1], (ch, LANES))], axis=0)
        rowv = jnp.sum(jnp.where(diag2, gc, 0.0), axis=0, keepdims=True)
        gam = jnp.where(tri, jnp.exp(jnp.where(tri, col - rowv, 0.0)), 0.0)
        kst = stack(kk)
        amats.append(jnp.where(strict, _nt_dot(bf(stack(kb)), bf(kst)) * gam, 0.0))
        qks.append(jnp.where(tri, _nt_dot(bf(stack(qq)), bf(kst)) * gam, 0.0))
        rhs_u.append(stack(vv * bb))
        rhs_w.append(stack(kb * eg))
        qgs.append(stack(qq * eg))
        kds.append(stack(kk * jnp.exp(g_end - gc)))
        decs.append(jnp.sum(jnp.where(eye, jnp.exp(g_end), 0.0), axis=1, keepdims=True))
    xs = [eye_f - a for a in amats]
    pws = [_dot_hl(a, a) for a in amats]
    steps = GDN_CHUNK.bit_length() - 2
    for r in range(steps):
        xs = [x + _dot_hl(x, pw) for x, pw in zip(xs, pws)]
        if r < steps - 1:
            pws = [_dot_hl(pw, pw) for pw in pws]
    uus = [dot(x, u) for x, u in zip(xs, rhs_u)]
    wws = [dot(x, w) for x, w in zip(xs, rhs_w)]
    kdts = [kd.T for kd in kds]
    states = [s_ref[p] for p in range(GDN_PAIRS)]
    for c in range(GDN_TILE_CHUNKS):
        rs = slice(c * ch, (c + 1) * ch)
        outs = []
        for p in range(GDN_PAIRS):
            n = c * GDN_PAIRS + p
            s = states[p]
            v_new = uus[n] - dot(wws[n], s)
            o_st = dot(qgs[n], s) + dot(qks[n], v_new)
            states[p] = s * decs[n] + dot(kdts[n], v_new)
            outs.append(o_st[:ch] + o_st[ch:])
        o = jnp.concatenate(outs, axis=1)
        ms = _dot_select(o * o, eh_ref[...]) * (1.0 / HEAD_DIM)
        z = z_ref[rs, :]
        o_ref[rs, :] = o * lax.rsqrt(ms + NORM_EPS) * nw_ref[...] * (z * jax.nn.sigmoid(z))
    for p in range(GDN_PAIRS):
        s_ref[p] = states[p]

    @pl.when(i == pl.num_programs(0) - 1)
    def _():
        s_out_ref[...] = s_ref[...]


def _gdn_prompt(qkv, small, z, conv_w, a_log, dt_bias, norm_w):
    l = qkv.shape[0]
    f32, bf16 = jnp.float32, jnp.bfloat16
    w = GDN_W
    hh = jnp.arange(w) // HEAD_DIM
    expander = lambda base: (jnp.arange(LANES)[:, None] == base + hh[None, :]).astype(bf16)
    eh = (hh[:, None] == hh[None, :]).astype(bf16)
    cw8 = jnp.zeros((8, 3 * w), f32).at[:GDN_CONV].set(conv_w)
    prm = jnp.zeros((8, LANES), f32)
    prm = prm.at[0, GDN_A_LANE:GDN_A_LANE + GDN_HEADS].set(a_log).at[1, GDN_A_LANE:GDN_A_LANE + GDN_HEADS].set(dt_bias)
    tl = GDN_PREP_ROWS
    row = lambda wd: pl.BlockSpec((tl, wd), lambda i: (i, 0))
    const = lambda a: pl.BlockSpec(a.shape, lambda i: (0,) * a.ndim)
    ea, eb = expander(GDN_A_LANE), expander(GDN_B_LANE)
    q, k, v, g, b = pl.pallas_call(
        _gdn_prep_kernel,
        grid=(l // tl,),
        in_specs=[row(3 * w), row(LANES), const(cw8), const(prm), const(ea), const(eb), const(eh)],
        out_specs=[row(w)] * 5,
        out_shape=[jax.ShapeDtypeStruct((l, w), f32)] * 5,
        scratch_shapes=[pltpu.VMEM((8, 3 * w), f32)],
        compiler_params=pltpu.CompilerParams(dimension_semantics=("arbitrary",), vmem_limit_bytes=VMEM_LIMIT_BYTES),
        name="gdn_prep",
    )(qkv, small, cw8, prm, ea, eb, eh)
    tc = GDN_TILE_CHUNKS * GDN_CHUNK
    lt = (jnp.arange(GDN_CHUNK)[:, None] >= jnp.arange(GDN_CHUNK)[None, :]).astype(bf16)
    nw = jnp.tile(norm_w, GDN_HEADS).reshape(1, w)
    rowc = pl.BlockSpec((tc, w), lambda i: (i, 0))
    o, s_bd = pl.pallas_call(
        _gdn_chunk_kernel,
        grid=(l // tc,),
        in_specs=[rowc] * 6 + [const(nw), const(lt), const(eh)],
        out_specs=[rowc, pl.BlockSpec((GDN_PAIRS, LANES, LANES), lambda i: (0, 0, 0))],
        out_shape=[jax.ShapeDtypeStruct((l, w), f32), jax.ShapeDtypeStruct((GDN_PAIRS, LANES, LANES), f32)],
        scratch_shapes=[pltpu.VMEM((GDN_PAIRS, LANES, LANES), f32)],
        compiler_params=pltpu.CompilerParams(dimension_semantics=("arbitrary",), vmem_limit_bytes=VMEM_LIMIT_BYTES),
        name="gdn_chunk",
    )(q, k, v, g, b, z, nw, lt, eh)
    s4 = s_bd.reshape(GDN_PAIRS, 2, HEAD_DIM, 2, HEAD_DIM)
    s_fin = jnp.stack([s4[:, 0, :, 0], s4[:, 1, :, 1]], axis=1).reshape(GDN_HEADS, HEAD_DIM, HEAD_DIM)
    return o, s_fin


def _proj_dil_kernel(x_ref, w_ref, cos_ref, sin_ref, bf_ref, kv_ref):
    acc = jnp.dot(x_ref[...].astype(jnp.bfloat16), w_ref[...], preferred_element_type=jnp.float32)
    reps = 2 * DIL_W // LANES
    qk = _rope_lanes(acc[:, :2 * DIL_W], _lane_tile(cos_ref[...], reps), _lane_tile(sin_ref[...], reps))
    v = acc[:, 2 * DIL_W:]
    bf_ref[...] = jnp.concatenate([qk, v], axis=1).astype(jnp.bfloat16)
    kv_ref[...] = jnp.concatenate([qk[:, DIL_W:], v], axis=1)


def _proj_dil(x, w_in):
    l, d = x.shape
    tm = _row_tile(l)
    cos, sin = _rope_tables(jnp.arange(l), LANES)
    return pl.pallas_call(
        _proj_dil_kernel,
        grid=(l // tm,),
        in_specs=[pl.BlockSpec((tm, d), lambda i: (i, 0)),
                  pl.BlockSpec((d, 3 * DIL_W), lambda i: (0, 0)),
                  pl.BlockSpec((tm, LANES), lambda i: (i, 0)),
                  pl.BlockSpec((tm, LANES), lambda i: (i, 0))],
        out_specs=[pl.BlockSpec((tm, 3 * DIL_W), lambda i: (i, 0)),
                   pl.BlockSpec((tm, 2 * DIL_W), lambda i: (i, 0))],
        out_shape=[jax.ShapeDtypeStruct((l, 3 * DIL_W), jnp.bfloat16),
                   jax.ShapeDtypeStruct((l, 2 * DIL_W), jnp.float32)],
        compiler_params=pltpu.CompilerParams(dimension_semantics=("arbitrary",), vmem_limit_bytes=VMEM_LIMIT_BYTES),
        name="proj_dil",
    )(x, w_in.astype(jnp.bfloat16), cos, sin)


def _dil_band_stats(qkv, d):
    l = qkv.shape[0]
    assert l % (d * DIL_BLOCK) == 0
    nb = l // (d * DIL_BLOCK)
    view = qkv.reshape(l // d, d * 3 * DIL_W)

    def part(which, prev):
        return pl.BlockSpec((DIL_BLOCK, DIL_W), (lambda n, r: (jnp.maximum(n - 1, 0), 3 * r + which)) if prev
                            else (lambda n, r: (n, 3 * r + which)))

    out = lambda w: pl.BlockSpec((DIL_BLOCK, w), lambda n, r: (n, r))
    num, st = pl.pallas_call(
        _dil_band_kernel,
        grid=(nb, d),
        in_specs=[part(0, False), part(1, True), part(1, False), part(2, True), part(2, False)],
        out_specs=[out(DIL_W), out(LANES)],
        out_shape=[jax.ShapeDtypeStruct((l // d, d * DIL_W), jnp.float32),
                   jax.ShapeDtypeStruct((l // d, d * LANES), jnp.float32)],
        compiler_params=pltpu.CompilerParams(dimension_semantics=("arbitrary", "arbitrary"),
                                             vmem_limit_bytes=VMEM_LIMIT_BYTES),
        name="dil_band_stats",
    )(view, view, view, view, view)
    return num.reshape(l, DIL_W), st.reshape(l, LANES)


def _dil_merge_kernel(*refs):
    f32 = jnp.float32
    ng = len(DIL_GROUPS)
    nums, sts = refs[:ng], refs[ng:2 * ng]
    ex_ref, w_ref, res_ref, g_ref, b_ref, o_ref = refs[2 * ng:]
    st = [r[...] for r in sts]
    m_all = functools.reduce(jnp.maximum, st)
    ws = [jnp.exp(s - m_all) for s in st]
    den = sum(w * pltpu.roll(s, LANES - DIL_HEADS, 1) for w, s in zip(ws, st))
    head_lane = lax.broadcasted_iota(jnp.int32, den.shape, 1) < DIL_HEADS
    o = sum(_dot_select(jnp.where(head_lane, w / den, 0.0), ex_ref[...]) * n[...]
            for w, n in zip(ws, nums))
    acc = jnp.dot(o.astype(jnp.bfloat16), w_ref[...], preferred_element_type=f32)
    o_ref[...] = _layer_norm_rows(DEEPNORM_ALPHA * res_ref[...] + acc, g_ref[...], b_ref[...])


def _dil_merge_proj(nums, sts, w_out, res, g, b):
    l, n = res.shape
    tm = _row_tile(l)
    expand = (jnp.arange(LANES)[:, None] == jnp.arange(DIL_W)[None, :] // HEAD_DIM).astype(jnp.bfloat16)
    row = lambda w: pl.BlockSpec((tm, w), lambda i: (i, 0))
    const = lambda a: pl.BlockSpec(a.shape, lambda i: (0,) * a.ndim)
    wb = w_out.astype(jnp.bfloat16)
    g2, b2 = g.reshape(1, n), b.reshape(1, n)
    return pl.pallas_call(
        _dil_merge_kernel,
        grid=(l // tm,),
        in_specs=[row(DIL_W)] * len(nums) + [row(LANES)] * len(sts) + [const(expand), const(wb), row(n),
                                                                       const(g2), const(b2)],
        out_specs=row(n),
        out_shape=jax.ShapeDtypeStruct((l, n), jnp.float32),
        compiler_params=pltpu.CompilerParams(dimension_semantics=("arbitrary",), vmem_limit_bytes=VMEM_LIMIT_BYTES),
        name="dil_merge_proj",
    )(*nums, *sts, expand, wb, res, g2, b2)


def _split_cols(h, widths):
    parts, start = [], 0
    for w in widths:
        parts.append(h[..., start:start + w])
        start += w
    return parts


def _even_widths():
    return (NSA_Q_W,) + (NSA_KV_W,) * 6 + (3 * NSA_HEADS, 3 * GDN_W, GDN_HEADS, GDN_HEADS, GDN_W)


def _rms_norm(x, w):
    return x * lax.rsqrt(jnp.mean(jnp.square(x), axis=-1, keepdims=True) + NORM_EPS) * w


def _l2_norm(x):
    return x * lax.rsqrt(jnp.sum(jnp.square(x), axis=-1, keepdims=True) + NORM_EPS)


def _rope(x, pos):
    half = HEAD_DIM // 2
    inv_freq = ROPE_THETA ** (-2.0 * jnp.arange(half, dtype=jnp.float32) / HEAD_DIM)
    ang = pos.astype(jnp.float32)[:, None] * inv_freq[None, :]
    cos, sin = jnp.cos(ang)[:, None, :], jnp.sin(ang)[:, None, :]
    xf = x.astype(jnp.float32)
    x1, x2 = xf[..., :half], xf[..., half:]
    return jnp.concatenate([x1 * cos - x2 * sin, x2 * cos + x1 * sin], axis=-1)


def _causal_dwconv(hist, u, w):
    width, s = w.shape[0], u.shape[1]
    ext = jnp.concatenate([hist.astype(u.dtype), u], axis=1)
    out = w[0] * ext[:, :s]
    for j in range(1, width):
        out = out + w[j] * ext[:, j:j + s]
    return out, ext[:, s:]


def _nsa_compress(rows, w1, b1, w2, pe):
    b, l, g, dh = rows.shape
    nc = l // CMP_BLOCK
    blk = rows[:, :nc * CMP_BLOCK].astype(jnp.float32).reshape(b, nc, CMP_BLOCK, g, dh) + pe[:, None, :]
    flat = blk.transpose(0, 1, 3, 2, 4).reshape(b, nc, g, CMP_BLOCK * dh)
    return jax.nn.gelu(flat @ w1 + b1) @ w2


def _nsa_compressed_kv(k_rows, v_rows, cw1, cb1, cw2, cpe):
    ck = _nsa_compress(k_rows, cw1[0], cb1[0], cw2[0], cpe[0])
    cv = _nsa_compress(v_rows, cw1[1], cb1[1], cw2[1], cpe[1])
    nc = ck.shape[1]
    ck = _rope(ck, (jnp.arange(nc) + 1) * CMP_BLOCK - 1)
    return ck, cv


def _rope_rows_kernel(*refs):
    cos_ref, sin_ref = refs[:2]
    n = (len(refs) - 2) // 2
    for x_ref, o_ref in zip(refs[2:2 + n], refs[2 + n:]):
        reps = x_ref.shape[1] // LANES
        o_ref[...] = _rope_lanes(x_ref[...], _lane_tile(cos_ref[...], reps), _lane_tile(sin_ref[...], reps))


def _rope_rows(xs):
    l = xs[0].shape[0]
    tm = _row_tile(l)
    cos, sin = _rope_tables(jnp.arange(l), LANES)
    row = lambda w: pl.BlockSpec((tm, w), lambda i: (i, 0))
    return pl.pallas_call(
        _rope_rows_kernel,
        grid=(l // tm,),
        in_specs=[row(LANES), row(LANES)] + [row(x.shape[1]) for x in xs],
        out_specs=[row(x.shape[1]) for x in xs],
        out_shape=[jax.ShapeDtypeStruct(x.shape, jnp.float32) for x in xs],
        compiler_params=pltpu.CompilerParams(dimension_semantics=("arbitrary",), vmem_limit_bytes=VMEM_LIMIT_BYTES),
        name="rope_rows",
    )(cos, sin, *xs)


def _nsa_prompt(q, kc, vc, ks, vs, kw, vw, small, cw1, cb1, cw2, cpe):
    b, l = q.shape[:2]
    flat = lambda t: t.reshape(t.shape[1], -1)
    qr, ksr, kwr = _rope_rows([flat(q), flat(ks), flat(kw)])
    ksr, kwr = ksr.reshape(ks.shape), kwr.reshape(kw.shape)
    ck, cv = _nsa_compressed_kv(kc, vc, cw1, cb1, cw2, cpe)
    vsf = vs.astype(jnp.float32)
    vwf = vw.astype(jnp.float32)
    o_nsa = _nsa_prompt_attention(qr, small, flat(ck), flat(cv), flat(ksr), flat(vsf), flat(kwr), flat(vwf))
    keep = min(NSA_WINDOW, l)
    rows_cmp = jnp.stack([kc, vc], axis=2)
    rows_slc = jnp.stack([ksr, vsf], axis=2)
    rows_win = jnp.stack([kwr[:, l - keep:], vwf[:, l - keep:]], axis=2)
    return o_nsa[None], rows_cmp, rows_slc, rows_win


def _nsa_sample(q, kc, vc, ks, vs, kw, vw, gate, cmp_pool, slc_pool, layer_idx, win_buf, page_table,
                cw1, cb1, cw2, cpe):
    db, s = q.shape[:2]
    past = page_table.shape[1] * PAGE_SIZE
    wb = win_buf.shape[1]
    assert s == 1 and wb == NSA_WINDOW and past >= wb and past % (STEP_PAGES * PAGE_SIZE) == 0
    qpos = past + jnp.arange(s)
    qr = _rope(q, qpos)
    ckt, cvt = _nsa_sample_compress(cmp_pool, layer_idx, page_table, cw1, cb1, cw2, cpe)
    ksr = _rope(ks, qpos)
    vsf = vs.astype(jnp.float32)
    kwr = _rope(kw, qpos)
    vwf = vw.astype(jnp.float32)
    newrows = jnp.stack([t.reshape(db, NSA_KV_W) for t in (ksr, vsf, kwr, vwf)], axis=1)
    o_nsa, rows_win = _nsa_sample_attention(qr[:, 0], ckt, cvt, slc_pool, layer_idx, page_table, win_buf,
                                              newrows, gate.reshape(db, -1))
    rows_cmp = jnp.stack([kc, vc], axis=2)
    rows_slc = jnp.stack([ksr, vsf], axis=2)
    return o_nsa[:, None], rows_cmp, rows_slc, rows_win


def _gdn_recurrent(q, k, v, g, beta, s0):
    def step(state, xs):
        q_t, k_t, v_t, g_t, b_t = xs
        state = state * jnp.exp(g_t)[..., None, None]
        v_t = (v_t - jnp.einsum('bhk,bhkv->bhv', k_t, state)) * b_t[..., None]
        state = state + jnp.einsum('bhk,bhv->bhkv', k_t, v_t)
        return state, jnp.einsum('bhk,bhkv->bhv', q_t, state)

    xs = tuple(jnp.moveaxis(a, 1, 0) for a in (q, k, v, g, beta))
    s_fin, o = lax.scan(step, s0, xs)
    return jnp.moveaxis(o, 0, 1), s_fin


def _gdn_step(qkv, a, bt, z, conv_hist, s0, conv_w, a_log, dt_bias, norm_w):
    b, s = qkv.shape[:2]
    c, new_hist = _causal_dwconv(conv_hist, qkv, conv_w)
    c = jax.nn.silu(c.astype(jnp.float32))
    q, k, v = [t.reshape(b, s, GDN_HEADS, HEAD_DIM) for t in jnp.split(c, 3, axis=-1)]
    q = _l2_norm(q) * HEAD_DIM ** -0.5
    k = _l2_norm(k)
    beta = jax.nn.sigmoid(bt.astype(jnp.float32))
    g = -jnp.exp(a_log) * jax.nn.softplus(a.astype(jnp.float32) + dt_bias)
    o, s_fin = _gdn_recurrent(q, k, v, g, beta, s0.astype(jnp.float32))
    o = _rms_norm(o, norm_w) * jax.nn.silu(z.astype(jnp.float32).reshape(b, s, GDN_HEADS, HEAD_DIM))
    return o.reshape(b, s, GDN_W), new_hist, s_fin


def _proj(x, w):
    b, s, d = x.shape
    n = w.shape[1]
    npad = -(-n // LANES) * LANES
    wp = jnp.pad(w, ((0, 0), (0, npad - n)))
    return _matmul(x.reshape(b * s, d), wp).reshape(b, s, npad)


def _even_prompt(x, w_in, cw1, cb1, cw2, cpe, conv_w, a_log, dt_bias, norm_w):
    b, l, _ = x.shape
    q, kc, vc, ks, vs, kw, vw, gate, qkv, a, bt, z = _split_cols(_proj(x, w_in), _even_widths())
    heads = lambda t: t.reshape(b, l, -1, HEAD_DIM)
    assert b == 1
    small = jnp.concatenate([gate, a, bt], axis=-1).reshape(l, -1)
    small = jnp.pad(small, ((0, 0), (0, LANES - small.shape[-1])))
    o_nsa, r_cmp, r_slc, r_win = _nsa_prompt(
        heads(q), heads(kc), heads(vc), heads(ks), heads(vs), heads(kw), heads(vw), small, cw1, cb1, cw2, cpe)
    o_gdn, s_fin = _gdn_prompt(qkv[0], small, z[0], conv_w, a_log, dt_bias, norm_w)
    conv_hist = qkv[:, l - (GDN_CONV - 1):]
    return [o_nsa[0], o_gdn], r_cmp, r_slc, r_win, conv_hist, s_fin[None]


def _even_sample(x, cmp_pool, slc_pool, layer_idx, win_buf, conv_hist, s0, page_table,
                 w_in, cw1, cb1, cw2, cpe, conv_w, a_log, dt_bias, norm_w):
    b, s, _ = x.shape
    q, kc, vc, ks, vs, kw, vw, gate, qkv, a, bt, z = _split_cols(_proj(x, w_in), _even_widths())
    heads = lambda t: t.reshape(b, s, -1, HEAD_DIM)
    o_nsa, r_cmp, r_slc, r_win = _nsa_sample(
        heads(q), heads(kc), heads(vc), heads(ks), heads(vs), heads(kw), heads(vw), gate,
        cmp_pool, slc_pool, layer_idx, win_buf, page_table, cw1, cb1, cw2, cpe)
    o_gdn, new_hist, s_fin = _gdn_step(qkv, a, bt, z, conv_hist, s0, conv_w, a_log, dt_bias, norm_w)
    return [o_nsa[:, 0], o_gdn[:, 0]], r_cmp, r_slc, r_win, new_hist, s_fin


def _dil_prompt(x, w_in, w_out, g, b):
    bsz, l, _ = x.shape
    assert bsz == 1
    qkv, kv = _proj_dil(x[0], w_in)
    stats = [_dil_band_stats(qkv, d) for _, d in DIL_GROUPS]
    y = _dil_merge_proj([n for n, _ in stats], [s for _, s in stats], w_out, x[0], g, b)
    keep = min(DIL_MAX_WINDOW, l)
    buf = kv[l - keep:].reshape(1, keep, 2, DIL_HEADS, HEAD_DIM)
    return y, buf


def _dil_sample(x, buf, past, w_in):
    db, s, _ = x.shape
    q, k, v = [t.reshape(db, s, DIL_HEADS, HEAD_DIM) for t in jnp.split(_proj(x, w_in), 3, axis=-1)]
    assert s == 1 and buf.shape[1] == DIL_MAX_WINDOW <= past
    qpos = past + jnp.arange(s)
    qr, kr = _rope(q, qpos), _rope(k, qpos)
    o, new_buf = _dil_sample_attention(qr[:, 0], kr.reshape(db, DIL_W), v.reshape(db, DIL_W).astype(jnp.float32), buf)
    return o[:, None], new_buf


def kernel(x_prompt, x_sample, cache_nsa_cmp_kv, cache_nsa_slc_kv, state_nsa_win_kv, state_gdn_conv,
           state_gdn_S, state_dil_kv, state_ffn_conv, page_table, w_in_a, nsa_cmp_w1, nsa_cmp_b1, nsa_cmp_w2,
           nsa_cmp_pe, gdn_conv_w, gdn_A_log, gdn_dt_bias, gdn_norm_w, w_out_a, w_in_c, w_out_c,
           ln_mix_g, ln_mix_b, ffn_w_in, ffn_conv_w, ffn_conv_b, ffn_w_out, ln_ffn_g, ln_ffn_b):
    past = page_table.shape[1] * PAGE_SIZE
    bp, lp, d = x_prompt.shape
    bs, ls, _ = x_sample.shape
    assert bp == 1 and ls == 1
    xp, xs = x_prompt, x_sample
    cmp_p, cmp_s, slc_p, slc_s, win_p, win_s = [], [], [], [], [], []
    gconv_p, gconv_s, gstate_p, gstate_s = [], [], [], []
    dil_p, dil_s, ffn_p, ffn_s = [], [], [], []
    for layer in range(DEPTH):
        if layer % 2 == 0:
            la = layer // 2
            wa = (w_in_a[la], nsa_cmp_w1[la], nsa_cmp_b1[la], nsa_cmp_w2[la], nsa_cmp_pe[la],
                  gdn_conv_w[la], gdn_A_log[la], gdn_dt_bias[la], gdn_norm_w[la])
            mp, rc, rs, rw, hc, hs_ = _even_prompt(xp, *wa)
            cmp_p.append(rc); slc_p.append(rs); win_p.append(rw); gconv_p.append(hc); gstate_p.append(hs_)
            ms, rc, rs, rw, hc, hs_ = _even_sample(xs, cache_nsa_cmp_kv, cache_nsa_slc_kv, la,
                                                   state_nsa_win_kv[:, la], state_gdn_conv[:, la],
                                                   state_gdn_S[:, la], page_table, *wa)
            cmp_s.append(rc); slc_s.append(rs); win_s.append(rw); gconv_s.append(hc); gstate_s.append(hs_)
            w_out = w_out_a[la]
        else:
            lc = layer // 2
            xp2, bpf = _dil_prompt(xp, w_in_c[lc], w_out_c[lc], ln_mix_g[layer], ln_mix_b[layer])
            o_dil, bsf = _dil_sample(xs, state_dil_kv[:, lc], past, w_in_c[lc])
            ms = [o_dil.reshape(bs, -1)]
            dil_p.append(bpf); dil_s.append(bsf)
            w_out = w_out_c[lc]
        if layer % 2 == 0:
            xp2 = _matmul_ln(mp, w_out, xp.reshape(lp, d), ln_mix_g[layer], ln_mix_b[layer])
        xs2 = _matmul_ln(ms, w_out, xs.reshape(bs, d), ln_mix_g[layer], ln_mix_b[layer])
        fargs = (ffn_w_in[layer], ffn_conv_w[layer], ffn_conv_b[layer], ffn_w_out[layer],
                 ln_ffn_g[layer], ln_ffn_b[layer])
        xp3, hp = _ffn_seq(xp2, *fargs)
        xs3, hs = _ffn_step(xs2, state_ffn_conv[:, layer], *fargs)
        xp, xs = xp3.reshape(1, lp, d), xs3.reshape(bs, 1, d)
        ffn_p.append(hp[None]); ffn_s.append(hs)

    def stk(lst):
        return jnp.stack(lst, axis=1)

    return (xp, xs, stk(cmp_p), stk(cmp_s), stk(slc_p), stk(slc_s), stk(win_p), stk(win_s),
            stk(gconv_p), stk(gconv_s), stk(gstate_p), stk(gstate_s), stk(dil_p), stk(dil_s),
            stk(ffn_p), stk(ffn_s))
```

```python
import functools

import jax
import jax.numpy as jnp
from jax import lax
from jax.experimental import pallas as pl
from jax.experimental.pallas import tpu as pltpu
import numpy as np

DEPTH = 2
PAGE_SIZE = 128
HEAD_DIM = 64
ROPE_THETA = 10000.0
NSA_HEADS = 8
NSA_KV_HEADS = 2
NSA_GROUP = NSA_HEADS // NSA_KV_HEADS
CMP_BLOCK = 32
SEL_BLOCK = 64
NSA_TOPN = 16
NSA_WINDOW = 512
NSA_QBLOCK = 128
NSA_FORCE = 1.0e4
GDN_HEADS = 8
GDN_CONV = 4
GDN_CHUNK = 64
DIL_HEADS = 16
DIL_GROUPS = ((128, 1), (512, 4), (2048, 16))
DIL_SPAN = 128
DIL_BLOCK = 128
DIL_MAX_WINDOW = 2048
D_FF = 2816
FFN_CONV = 3
DEEPNORM_ALPHA = (2.0 * DEPTH) ** 0.25
LN_EPS = 1e-5
NORM_EPS = 1e-6
NSA_Q_W = NSA_HEADS * HEAD_DIM
NSA_KV_W = NSA_KV_HEADS * HEAD_DIM
GDN_W = GDN_HEADS * HEAD_DIM
DIL_W = DIL_HEADS * HEAD_DIM

LANES = 128
VMEM_LIMIT_BYTES = 56 * 1024 * 1024


def _layer_norm_rows(r, g, b):
    mu = jnp.mean(r, axis=-1, keepdims=True)
    d = r - mu
    var = jnp.mean(d * d, axis=-1, keepdims=True)
    return d * lax.rsqrt(var + LN_EPS) * g + b


def _mm_kernel(x_ref, w_ref, o_ref):
    o_ref[...] = jnp.dot(x_ref[...].astype(jnp.bfloat16), w_ref[...], preferred_element_type=jnp.float32)


def _mm_ln_kernel(*refs):
    n = (len(refs) - 4) // 2
    res_ref, g_ref, b_ref, o_ref = refs[2 * n:]
    acc = sum(jnp.dot(x[...].astype(jnp.bfloat16), w[...], preferred_element_type=jnp.float32)
              for x, w in zip(refs[:n], refs[n:2 * n]))
    o_ref[...] = _layer_norm_rows(DEEPNORM_ALPHA * res_ref[...] + acc, g_ref[...], b_ref[...])


def _row_tile(m):
    return 512 if m % 512 == 0 else m


def _matmul(x, w):
    m, k = x.shape
    n = w.shape[1]
    tm = _row_tile(m)
    tn = n
    return pl.pallas_call(
        _mm_kernel,
        grid=(m // tm, n // tn),
        in_specs=[pl.BlockSpec((tm, k), lambda i, j: (i, 0)),
                  pl.BlockSpec((k, tn), lambda i, j: (0, j))],
        out_specs=pl.BlockSpec((tm, tn), lambda i, j: (i, j)),
        out_shape=jax.ShapeDtypeStruct((m, n), jnp.float32),
        compiler_params=pltpu.CompilerParams(dimension_semantics=("parallel", "arbitrary"),
                                             vmem_limit_bytes=VMEM_LIMIT_BYTES),
        name="matmul",
    )(x, w.astype(jnp.bfloat16))


def _matmul_ln(xs, w, res, g, b):
    m, n = res.shape
    tm = _row_tile(m)
    wb = w.astype(jnp.bfloat16)
    ws, start = [], 0
    for x in xs:
        ws.append(wb[start:start + x.shape[1]])
        start += x.shape[1]
    assert start == w.shape[0]
    row = lambda a: pl.BlockSpec((tm, a.shape[1]), lambda i: (i, 0))
    const = lambda a: pl.BlockSpec(a.shape, lambda i: (0, 0))
    g2, b2 = g.reshape(1, n), b.reshape(1, n)
    return pl.pallas_call(
        _mm_ln_kernel,
        grid=(m // tm,),
        in_specs=[row(x) for x in xs] + [const(wi) for wi in ws] + [row(res), const(g2), const(b2)],
        out_specs=row(res),
        out_shape=jax.ShapeDtypeStruct((m, n), jnp.float32),
        compiler_params=pltpu.CompilerParams(dimension_semantics=("arbitrary",),
                                             vmem_limit_bytes=VMEM_LIMIT_BYTES),
        name="matmul_ln",
    )(*xs, *ws, res, g2, b2)


FFN_CHUNK = D_FF
FFN_NCHUNK = D_FF // FFN_CHUNK


def _ffn_seq_kernel(x_ref, wi_ref, cw_ref, cb_ref, wo_ref, lg_ref, lb_ref, y_ref, hist_ref, carry_ref):
    f32, bf16 = jnp.float32, jnp.bfloat16
    i = pl.program_id(0)
    tm = x_ref.shape[0]
    x = x_ref[...]
    xb = x.astype(bf16)

    @pl.when(i == 0)
    def _():
        carry_ref[...] = jnp.zeros(carry_ref.shape, f32)

    row = lax.broadcasted_iota(jnp.int32, (8, FFN_CHUNK), 0)

    def conv_half(cols):
        u = jnp.dot(xb, wi_ref[:, cols], preferred_element_type=f32)
        prev = carry_ref[:, cols]
        p2, p1 = prev[6:7], prev[7:8]
        r1, r2 = pltpu.roll(u, 1, 0), pltpu.roll(u, 2, 0)
        u1 = jnp.concatenate([jnp.where(row == 0, p1, r1[:8]), r1[8:]], axis=0)
        u2 = jnp.concatenate([jnp.where(row == 0, p2, jnp.where(row == 1, p1, r2[:8])), r2[8:]], axis=0)
        carry_ref[:, cols] = u[tm - 8:]
        hist_ref[:, cols] = u[tm - 8:]
        cw = cw_ref[:, cols]
        return cw[0:1] * u2 + cw[1:2] * u1 + cw[2:3] * u + cb_ref[:, cols]

    acc = None
    for j in range(FFN_NCHUNK):
        a = conv_half(slice(j * FFN_CHUNK, (j + 1) * FFN_CHUNK))
        g = conv_half(slice(D_FF + j * FFN_CHUNK, D_FF + (j + 1) * FFN_CHUNK))
        h = (a * jax.nn.sigmoid(a) * g).astype(bf16)
        part = jnp.dot(h, wo_ref[j * FFN_CHUNK:(j + 1) * FFN_CHUNK, :], preferred_element_type=f32)
        acc = part if acc is None else acc + part
    y_ref[...] = _layer_norm_rows(DEEPNORM_ALPHA * x + acc, lg_ref[...], lb_ref[...])


def _ffn_seq(x, w_in, conv_w, conv_b, w_out, ln_g, ln_b):
    l, d = x.shape
    tm = _row_tile(l)
    cw8 = jnp.zeros((8, 2 * D_FF), jnp.float32).at[:FFN_CONV].set(conv_w)
    cb = conv_b.reshape(1, 2 * D_FF)
    const = lambda a: pl.BlockSpec(a.shape, lambda i: (0,) * a.ndim, pipeline_mode=pl.Buffered(1))
    wi, wo = w_in.astype(jnp.bfloat16), w_out.astype(jnp.bfloat16)
    g2, b2 = ln_g.reshape(1, d), ln_b.reshape(1, d)
    y, hist = pl.pallas_call(
        _ffn_seq_kernel,
        grid=(l // tm,),
        in_specs=[pl.BlockSpec((tm, d), lambda i: (i, 0)), const(wi), const(cw8), const(cb), const(wo),
                  const(g2), const(b2)],
        out_specs=[pl.BlockSpec((tm, d), lambda i: (i, 0)), pl.BlockSpec((8, 2 * D_FF), lambda i: (i, 0))],
        out_shape=[jax.ShapeDtypeStruct((l, d), jnp.float32),
                   jax.ShapeDtypeStruct((l // tm * 8, 2 * D_FF), jnp.float32)],
        scratch_shapes=[pltpu.VMEM((8, 2 * D_FF), jnp.float32)],
        compiler_params=pltpu.CompilerParams(dimension_semantics=("arbitrary",), vmem_limit_bytes=VMEM_LIMIT_BYTES),
        name="ffn_seq",
    )(x, wi, cw8, cb, wo, g2, b2)
    return y, hist[-(FFN_CONV - 1):]


def _ffn_step_kernel(x_ref, h_ref, wa_ref, wg_ref, cwa_ref, cwg_ref, cba_ref, cbg_ref, wo_ref, lg_ref, lb_ref,
                     y_ref, ua_ref, ug_ref, acc_ref):
    j = pl.program_id(0)
    x = x_ref[...]
    xb = x.astype(jnp.bfloat16)
    ua = jnp.dot(xb, wa_ref[...], preferred_element_type=jnp.float32)
    ug = jnp.dot(xb, wg_ref[...], preferred_element_type=jnp.float32)
    ua_ref[...] = ua
    ug_ref[...] = ug
    cwa, cwg = cwa_ref[...], cwg_ref[...]
    a = cwa[0:1] * h_ref[0, 0] + cwa[1:2] * h_ref[1, 0] + cwa[2:3] * ua + cba_ref[...]
    g = cwg[0:1] * h_ref[0, 1] + cwg[1:2] * h_ref[1, 1] + cwg[2:3] * ug + cbg_ref[...]
    h = (a * jax.nn.sigmoid(a) * g).astype(jnp.bfloat16)
    part = jnp.dot(h, wo_ref[...], preferred_element_type=jnp.float32)

    @pl.when(j == 0)
    def _():
        acc_ref[...] = part

    @pl.when(j > 0)
    def _():
        acc_ref[...] += part

    @pl.when(j == pl.num_programs(0) - 1)
    def _():
        y_ref[...] = _layer_norm_rows(DEEPNORM_ALPHA * x + acc_ref[...], lg_ref[...], lb_ref[...])


def _ffn_step(x, hist, w_in, conv_w, conv_b, w_out, ln_g, ln_b):
    b, d = x.shape
    c, nc = FFN_CHUNK, FFN_NCHUNK
    w_in = w_in.astype(jnp.bfloat16)
    cw8 = jnp.zeros((8, 2 * D_FF), jnp.float32).at[:FFN_CONV].set(conv_w)
    cb = conv_b.reshape(1, 2 * D_FF)
    h4 = jnp.transpose(hist, (1, 0, 2)).reshape(2, b, 2, D_FF).transpose(0, 2, 1, 3)
    y, ua, ug = pl.pallas_call(
        _ffn_step_kernel,
        grid=(nc,),
        in_specs=[pl.BlockSpec((b, d), lambda j: (0, 0)),
                  pl.BlockSpec((2, 2, b, c), lambda j: (0, 0, 0, j)),
                  pl.BlockSpec((d, c), lambda j: (0, j)),
                  pl.BlockSpec((d, c), lambda j: (0, j + nc)),
                  pl.BlockSpec((8, c), lambda j: (0, j)),
                  pl.BlockSpec((8, c), lambda j: (0, j + nc)),
                  pl.BlockSpec((1, c), lambda j: (0, j)),
                  pl.BlockSpec((1, c), lambda j: (0, j + nc)),
                  pl.BlockSpec((c, d), lambda j: (j, 0)),
                  pl.BlockSpec((1, d), lambda j: (0, 0)),
                  pl.BlockSpec((1, d), lambda j: (0, 0))],
        out_specs=[pl.BlockSpec((b, d), lambda j: (0, 0)),
                   pl.BlockSpec((b, c), lambda j: (0, j)),
                   pl.BlockSpec((b, c), lambda j: (0, j))],
        out_shape=[jax.ShapeDtypeStruct((b, d), jnp.float32),
                   jax.ShapeDtypeStruct((b, D_FF), jnp.float32),
                   jax.ShapeDtypeStruct((b, D_FF), jnp.float32)],
        scratch_shapes=[pltpu.VMEM((b, d), jnp.float32)],
        compiler_params=pltpu.CompilerParams(dimension_semantics=("arbitrary",),
                                             vmem_limit_bytes=VMEM_LIMIT_BYTES),
        name="ffn_step",
    )(x, h4, w_in, w_in, cw8, cw8, cb, cb, w_out.astype(jnp.bfloat16), ln_g.reshape(1, d), ln_b.reshape(1, d))
    u = jnp.concatenate([ua, ug], axis=-1)
    return y, jnp.concatenate([hist[:, 1:], u[:, None]], axis=1)


NEG = -1e30
NSA_KT = 512
NSA_COLS = NSA_HEADS * NSA_QBLOCK
NSA_WSPAN = NSA_WINDOW + NSA_QBLOCK
LOG2E = 1.4426950408889634
NSA_VROWS = HEAD_DIM + 8


def _lane_tile(x, n):
    return jnp.concatenate([x] * n, axis=1)


def _nsa_prompt_kernel(q_ref, sm_ref, ck_ref, cvt_ref, ks_ref, vst_ref, kw_ref, vwt_ref, hot_ref, o_ref,
                       selb_ref, m_ref, acc_ref, *, ns):
    f32, bf16 = jnp.float32, jnp.bfloat16
    qb = NSA_QBLOCK
    i = pl.program_id(0)
    s0 = i * qb
    half = NSA_COLS // 2

    qt = (q_ref[...] * (HEAD_DIM ** -0.5 * LOG2E)).T
    zero = jnp.zeros((HEAD_DIM, qb), f32)
    top = jnp.concatenate([qt[h * HEAD_DIM:(h + 1) * HEAD_DIM] for h in range(NSA_GROUP)] + [zero] * NSA_GROUP, axis=1)
    bot = jnp.concatenate([zero] * NSA_GROUP + [qt[h * HEAD_DIM:(h + 1) * HEAD_DIM]
                                                for h in range(NSA_GROUP, NSA_HEADS)], axis=1)
    qbd = jnp.concatenate([top, bot], axis=0).astype(bf16)

    def pv(vt, p):
        pb = p.astype(bf16)
        rows = vt.shape[0] // NSA_KV_HEADS
        return [jnp.dot(vt[g * rows:(g + 1) * rows], pb[:, g * half:(g + 1) * half],
                        preferred_element_type=f32) for g in range(NSA_KV_HEADS)]

    nc = 2 * ns
    r = lax.broadcasted_iota(jnp.int32, (nc, qb), 0)
    lane = lax.broadcasted_iota(jnp.int32, (nc, qb), 1)
    cidx = jnp.where(r < ns, 2 * r, 2 * (r - ns) + 1)
    cbias = jnp.where((cidx + 1) * CMP_BLOCK - 1 <= s0 + lane, 0.0, NEG)
    sc = jnp.dot(ck_ref[...], qbd, preferred_element_type=f32) + _lane_tile(cbias, NSA_HEADS)
    m = jnp.max(sc, axis=0, keepdims=True)
    p = jnp.exp2(sc - m)
    pn = p * jnp.where(m > 0.5 * NEG, 1.0 / jnp.sum(p, axis=0, keepdims=True), 0.0)
    o_cmp = pv(cvt_ref[...], pn)

    blk = lax.broadcasted_iota(jnp.int32, (ns, qb), 0)
    qpos = s0 + lax.broadcasted_iota(jnp.int32, (ns, qb), 1)
    cur = qpos // SEL_BLOCK
    forced = (blk == 0) | (blk == cur) | (blk == cur - 1)
    for g in range(NSA_KV_HEADS):
        imp = pn[:, g * half:g * half + qb]
        for h in range(1, NSA_GROUP):
            imp = imp + pn[:, g * half + h * qb:g * half + (h + 1) * qb]
        imp = imp[:ns] + imp[ns:]
        val = jnp.where(blk > cur, -1.0, jnp.where(forced, -jnp.inf, imp))
        bias = jnp.where(forced & (blk <= cur), 0.0, NEG)
        for _ in range(NSA_TOPN - 3):
            top = jnp.max(val, axis=0, keepdims=True)
            pick = jnp.min(jnp.where(val == top, blk, ns), axis=0, keepdims=True)
            hit = blk == pick
            bias = jnp.where(hit, 0.0, bias)
            val = jnp.where(hit, -jnp.inf, val)
        selb_ref[g] = bias

    m_ref[...] = jnp.full(m_ref.shape, NEG, f32)
    acc_ref[...] = jnp.zeros(acc_ref.shape, f32)
    per_tile = NSA_KT // SEL_BLOCK
    zpad = jnp.zeros((LANES - 16, NSA_COLS), bf16)

    def slc_tile(kt, causal):
        k0 = pl.multiple_of(kt * NSA_KT, NSA_KT)
        b0 = pl.multiple_of(kt * per_tile, per_tile)
        brow = jnp.concatenate([selb_ref[g, pl.ds(b0, per_tile), :] for g in range(NSA_KV_HEADS)
                                for _ in range(NSA_GROUP)], axis=1)
        brow = jnp.concatenate([brow, jnp.zeros((16 - per_tile, NSA_COLS), f32)], axis=0).astype(bf16)
        q_aug = jnp.concatenate([qbd, brow, zpad], axis=0)
        k_aug = jnp.concatenate([ks_ref[pl.ds(k0, NSA_KT), :], hot_ref[...]], axis=1)
        s = jnp.dot(k_aug, q_aug, preferred_element_type=f32)
        if causal:
            kpos = k0 + lax.broadcasted_iota(jnp.int32, (NSA_KT, qb), 0)
            qq = s0 + lax.broadcasted_iota(jnp.int32, (NSA_KT, qb), 1)
            s = s + _lane_tile(jnp.where(kpos <= qq, 0.0, NEG), NSA_HEADS)
        m_old = m_ref[...]
        m_new = jnp.maximum(m_old, jnp.max(s, axis=0, keepdims=True))
        alpha = jnp.exp2(m_old - m_new)
        p = jnp.exp2(s - m_new)
        m_ref[...] = m_new
        upd = pv(vst_ref[:, pl.ds(k0, NSA_KT)], p)
        for g in range(NSA_KV_HEADS):
            acc_ref[g] = acc_ref[g] * alpha[:, g * half:(g + 1) * half] + upd[g]

    kd = s0 // NSA_KT

    def body(j, carry):
        slc_tile(2 * j, False)
        slc_tile(2 * j + 1, False)
        return carry

    lax.fori_loop(0, kd // 2, body, 0)

    @pl.when(kd % 2 == 1)
    def _():
        slc_tile(kd - 1, False)

    slc_tile(kd, True)
    inv_slc = [1.0 / acc_ref[g, HEAD_DIM:HEAD_DIM + 1, :] for g in range(NSA_KV_HEADS)]

    w0 = pl.multiple_of(s0, qb)
    sw = jnp.dot(kw_ref[pl.ds(w0, NSA_WSPAN), :], qbd, preferred_element_type=f32)
    rr = lax.broadcasted_iota(jnp.int32, (NSA_WSPAN, qb), 0)
    qi = lax.broadcasted_iota(jnp.int32, (NSA_WSPAN, qb), 1)
    ok = (rr >= qi) & (rr <= qi + NSA_WINDOW) & (rr + s0 >= NSA_WINDOW)
    sw = sw + _lane_tile(jnp.where(ok, 0.0, NEG), NSA_HEADS)
    pw = jnp.exp2(sw - jnp.max(sw, axis=0, keepdims=True))
    o_win = pv(vwt_ref[:, pl.ds(w0, NSA_WSPAN)], pw)
    inv_win = [1.0 / o[HEAD_DIM:HEAD_DIM + 1] for o in o_win]

    gt = jax.nn.sigmoid(sm_ref[...].T)
    outs = []
    for h in range(NSA_HEADS):
        g, hg = divmod(h, NSA_GROUP)
        c0, c1 = hg * qb, (hg + 1) * qb
        g_cmp = gt[h:h + 1]
        g_slc = gt[NSA_HEADS + h:NSA_HEADS + h + 1] * inv_slc[g][:, c0:c1]
        g_win = gt[2 * NSA_HEADS + h:2 * NSA_HEADS + h + 1] * inv_win[g][:, c0:c1]
        outs.append(o_cmp[g][:, c0:c1] * g_cmp + acc_ref[g, :HEAD_DIM, c0:c1] * g_slc
                    + o_win[g][:HEAD_DIM, c0:c1] * g_win)
    o_ref[...] = jnp.concatenate(outs, axis=0).T


def _nsa_prompt_attention(qr, small, ck, cv, ksr, vs, kwr, vw):
    l = qr.shape[0]
    ns = l // SEL_BLOCK
    assert ns >= NSA_TOPN and l % NSA_KT == 0
    nc = 2 * ns
    bf16 = jnp.bfloat16
    perm =jnp.concatenate([jnp.arange(0, nc, 2), jnp.arange(1, nc, 2)])
    ckp = ck[perm].astype(bf16)
    cvt = cv[perm].T.astype(bf16)
    pad = jnp.zeros((NSA_WINDOW, NSA_KV_W), bf16)
    kwp = jnp.concatenate([pad, kwr.astype(bf16)], axis=0)
    def with_ones(vt):
        n = vt.shape[1]
        extra = jnp.concatenate([jnp.ones((1, n), bf16), jnp.zeros((NSA_VROWS - HEAD_DIM - 1, n), bf16)], axis=0)
        return jnp.concatenate([x for g in range(NSA_KV_HEADS) for x in (vt[g * HEAD_DIM:(g + 1) * HEAD_DIM], extra)], axis=0)

    vwt = with_ones(jnp.concatenate([pad, vw.astype(bf16)], axis=0).T)
    hot = (jnp.arange(NSA_KT)[:, None] // SEL_BLOCK == jnp.arange(LANES)[None, :]).astype(bf16)
    full = lambda a: pl.BlockSpec(a.shape, lambda i: (0,) * a.ndim)
    args = (qr, small, ckp, cvt, ksr.astype(bf16), with_ones(vs.T.astype(bf16)), kwp, vwt, hot)
    return pl.pallas_call(
        functools.partial(_nsa_prompt_kernel, ns=ns),
        grid=(l // NSA_QBLOCK,),
        in_specs=[pl.BlockSpec((NSA_QBLOCK, NSA_Q_W), lambda i: (i, 0)),
                  pl.BlockSpec((NSA_QBLOCK, LANES), lambda i: (i, 0))] + [full(a) for a in args[2:]],
        out_specs=pl.BlockSpec((NSA_QBLOCK, NSA_Q_W), lambda i: (i, 0)),
        out_shape=jax.ShapeDtypeStruct((l, NSA_Q_W), jnp.float32),
        scratch_shapes=[pltpu.VMEM((NSA_KV_HEADS, ns, NSA_QBLOCK), jnp.float32),
                        pltpu.VMEM((1, NSA_COLS), jnp.float32),
                        pltpu.VMEM((NSA_KV_HEADS, NSA_VROWS, NSA_COLS // 2), jnp.float32)],
        compiler_params=pltpu.CompilerParams(dimension_semantics=("arbitrary",),
                                             vmem_limit_bytes=VMEM_LIMIT_BYTES),
        name="nsa_prompt",
    )(*args)


CMP_HIDDEN = 2 * HEAD_DIM
CMP_PER_PAGE = PAGE_SIZE // CMP_BLOCK
CMP_PAGES_PER_STEP = 64
STEP_PAGES = 64
STEP_ROWS = 16


def _rope_tables(pos, width):
    half = HEAD_DIM // 2
    inv_freq = ROPE_THETA ** (-2.0 * jnp.arange(half, dtype=jnp.float32) / HEAD_DIM)
    ang = pos.astype(jnp.float32)[:, None] * inv_freq[None, :]
    cos, sin = jnp.cos(ang), jnp.sin(ang)
    reps = width // HEAD_DIM
    return (jnp.tile(jnp.concatenate([cos, cos], axis=1), (1, reps)),
            jnp.tile(jnp.concatenate([-sin, sin], axis=1), (1, reps)))


def _rope_lanes(x, cos, sin_signed):
    n = x.shape[-1]
    lane = lax.broadcasted_iota(jnp.int32, x.shape, x.ndim - 1)
    first = (lane % HEAD_DIM) < HEAD_DIM // 2
    partner = jnp.where(first, pltpu.roll(x, n - HEAD_DIM // 2, x.ndim - 1), pltpu.roll(x, HEAD_DIM // 2, x.ndim - 1))
    return x * cos + partner * sin_signed


def _nt_dot(a, b):
    return lax.dot_general(a, b, (((1,), (1,)), ((), ())), preferred_element_type=jnp.float32)


CMP_STEP_LANES = CMP_PAGES_PER_STEP * CMP_PER_PAGE
CMP_FEATURE_GROUP = 16


def _cmp_lane_blocks(n_pages):
    lane = np.arange(n_pages * CMP_PER_PAGE)
    step, rem = lane // CMP_STEP_LANES, lane % CMP_STEP_LANES
    j, pl_ = rem // CMP_PAGES_PER_STEP, rem % CMP_PAGES_PER_STEP
    return (step * CMP_PAGES_PER_STEP + pl_) * CMP_PER_PAGE + j


def _cmp_step_kernel(pt_ref, *refs):
    f32 = jnp.float32
    npg = CMP_PAGES_PER_STEP
    pages = refs[:npg]
    pe_ref, w1_ref, b1_ref, w2_ref, cos_ref, sin_ref, ck_ref, cv_ref, slab_ref = refs[npg:]
    outs = (ck_ref, cv_ref)
    for k, r in enumerate(pages):
        for s in range(2 * NSA_KV_HEADS):
            slab_ref[s, k * HEAD_DIM:(k + 1) * HEAD_DIM, :] = r[0, s]
    for kv in range(2):
        h = jnp.zeros((NSA_KV_HEADS * npg, CMP_PER_PAGE * CMP_HIDDEN), f32)
        xt = [jnp.swapaxes(slab_ref[2 * kv + g].reshape(npg, HEAD_DIM, PAGE_SIZE), 0, 1) for g in range(NSA_KV_HEADS)]
        for dg in range(HEAD_DIM // CMP_FEATURE_GROUP):
            x = jnp.concatenate(
                [jnp.concatenate([xt[g][d] for g in range(NSA_KV_HEADS)], axis=0) + pe_ref[kv, d]
                 for d in range(dg * CMP_FEATURE_GROUP, (dg + 1) * CMP_FEATURE_GROUP)], axis=1)
            h = h + jnp.dot(x.astype(jnp.bfloat16), w1_ref[kv, dg], preferred_element_type=f32)
        act = jax.nn.gelu(h + b1_ref[kv]).astype(jnp.bfloat16)
        ct = _nt_dot(w2_ref[kv], act)
        tile = jnp.concatenate(
            [jnp.concatenate([ct[j * HEAD_DIM:(j + 1) * HEAD_DIM, g * npg:(g + 1) * npg] for j in range(CMP_PER_PAGE)],
                             axis=1) for g in range(NSA_KV_HEADS)], axis=0)
        if kv == 0:
            row = lax.broadcasted_iota(jnp.int32, tile.shape, 0)
            n = tile.shape[0]
            partner = jnp.where((row % HEAD_DIM) < HEAD_DIM // 2, pltpu.roll(tile, n - HEAD_DIM // 2, 0),
                                pltpu.roll(tile, HEAD_DIM // 2, 0))
            tile = tile * cos_ref[...] + partner * sin_ref[...]
        outs[kv][0] = tile


def _compress_weights(cw1, cb1, cw2, cpe):
    eye = jnp.eye(CMP_PER_PAGE, dtype=jnp.float32)
    w1r = cw1.reshape(2, CMP_BLOCK, HEAD_DIM, CMP_HIDDEN)
    w1 = jnp.einsum('ktdn,ja->kdjtan', w1r, eye).reshape(
        2, HEAD_DIM // CMP_FEATURE_GROUP, CMP_FEATURE_GROUP * PAGE_SIZE, CMP_PER_PAGE * CMP_HIDDEN)
    b1 = jnp.tile(cb1, (1, CMP_PER_PAGE))[:, None, :]
    w2 = jnp.einsum('knd,ja->kjdan', cw2, eye).reshape(2, CMP_PER_PAGE * HEAD_DIM, CMP_PER_PAGE * CMP_HIDDEN)
    pe = jnp.tile(cpe.transpose(0, 2, 1), (1, 1, CMP_PER_PAGE))[:, :, None, :]
    return w1.astype(jnp.bfloat16), b1, w2.astype(jnp.bfloat16), pe


def _nsa_sample_compress(pool, layer_idx, page_table, cw1, cb1, cw2, cpe):
    n_pool, nl = pool.shape[:2]
    db, n_pages = page_table.shape
    npg = CMP_PAGES_PER_STEP
    nchunk = n_pages // npg
    nc = n_pages * CMP_PER_PAGE
    view = jnp.transpose(pool, (0, 1, 3, 4, 5, 2)).reshape(n_pool * nl, 2 * NSA_KV_HEADS, HEAD_DIM, PAGE_SIZE)
    pt = (page_table * nl + layer_idx).reshape(-1).astype(jnp.int32)
    w1, b1, w2, pe = _compress_weights(cw1, cb1, cw2, cpe)
    pos = (jnp.asarray(_cmp_lane_blocks(n_pages)) + 1) * CMP_BLOCK - 1
    half = HEAD_DIM // 2
    inv_freq = ROPE_THETA ** (-2.0 * jnp.arange(half, dtype=jnp.float32) / HEAD_DIM)
    ang = inv_freq[:, None] * pos.astype(jnp.float32)[None, :]
    cos = jnp.tile(jnp.cos(ang), (2 * NSA_KV_HEADS, 1))
    sin = jnp.tile(jnp.concatenate([-jnp.sin(ang), jnp.sin(ang)], axis=0), (NSA_KV_HEADS, 1))

    def page_map(k):
        return lambda b, c, pt_ref: (pt_ref[b * n_pages + c * npg + k], 0, 0, 0)

    const = lambda a: pl.BlockSpec(a.shape, lambda b, c, pt_ref: (0,) * a.ndim, pipeline_mode=pl.Buffered(1))
    lanes_c = lambda: pl.BlockSpec((NSA_KV_W, CMP_STEP_LANES), lambda b, c, pt_ref: (0, c))
    grid_spec = pltpu.PrefetchScalarGridSpec(
        num_scalar_prefetch=1, grid=(db, nchunk),
        in_specs=[pl.BlockSpec((1, 2 * NSA_KV_HEADS, HEAD_DIM, PAGE_SIZE), page_map(k)) for k in range(npg)]
        + [const(pe), const(w1), const(b1), const(w2), lanes_c(), lanes_c()],
        out_specs=[pl.BlockSpec((1, NSA_KV_W, CMP_STEP_LANES), lambda b, c, pt_ref: (b, 0, c))] * 2,
        scratch_shapes=[pltpu.VMEM((2 * NSA_KV_HEADS, npg * HEAD_DIM, PAGE_SIZE), jnp.float32)])
    return pl.pallas_call(
        _cmp_step_kernel, grid_spec=grid_spec,
        out_shape=[jax.ShapeDtypeStruct((db, NSA_KV_W, nc), jnp.float32)] * 2,
        compiler_params=pltpu.CompilerParams(dimension_semantics=("arbitrary", "arbitrary"),
                                             vmem_limit_bytes=VMEM_LIMIT_BYTES),
        name="nsa_sample_compress",
    )(pt, *([view] * npg), pe, w1, b1, w2, cos, sin)


def _nsa_step_kernel(pt_ref, *refs, topn):
    f32, bf16 = jnp.float32, jnp.bfloat16
    npg = STEP_PAGES
    q_ref, ck_ref, cv_ref = refs[:3]
    pages = refs[3:3 + npg]
    (win_ref, newr_ref, newt_ref, gate_ref, blk_ref, exp_ref, o_ref, wout_ref,
     selt_ref, m_ref, l_ref, acc_ref, side_ref) = refs[3 + npg:]
    cc = pl.program_id(1)
    q16 = q_ref[0]
    qb = q16.astype(bf16)
    row16 = lax.broadcasted_iota(jnp.int32, (STEP_ROWS, 1), 0)

    def new_key_scores(krow):
        return jnp.sum(q16 * krow, axis=1, keepdims=True)

    @pl.when(cc == 0)
    def _():
        nc = ck_ref.shape[2]
        s = jnp.dot(qb, ck_ref[0].astype(bf16), preferred_element_type=f32)
        p = jnp.exp(s - jnp.max(s, axis=1, keepdims=True))
        pn = p / jnp.sum(p, axis=1, keepdims=True)
        o_cmp = _nt_dot(pn.astype(bf16), cv_ref[0].astype(bf16))

        rowp = lax.broadcasted_iota(jnp.int32, pn.shape, 0)
        row8 = lax.broadcasted_iota(jnp.int32, (8, nc), 0)
        blk = blk_ref[...]
        val = jnp.full((8, nc), -jnp.inf, f32)
        for g in range(NSA_KV_HEADS):
            ig = jnp.sum(jnp.where((rowp >= g * NSA_GROUP) & (rowp < (g + 1) * NSA_GROUP), pn, 0.0),
                         axis=0, keepdims=True)
            ig = ig + pltpu.roll(ig, nc - CMP_PAGES_PER_STEP, 1)
            vg = jnp.where(blk[:1] < 0, -jnp.inf, jnp.where(blk[1:2] > 0, NSA_FORCE, ig))
            val = jnp.where(row8 == g, vg, val)
        sblk = jnp.where(blk[:1] < 0, nc, blk[:1])
        sel = jnp.zeros((8, nc), f32)
        for _ in range(topn):
            top = jnp.max(val, axis=1, keepdims=True)
            pick = jnp.min(jnp.where(val == top, sblk, nc), axis=1, keepdims=True)
            hit = sblk == pick
            sel = jnp.where(hit, 1.0, sel)
            val = jnp.where(hit, -jnp.inf, val)
        selh = jnp.where(row16 < NSA_GROUP, sel[0:1], jnp.where(row16 < NSA_HEADS, sel[1:2], 0.0))
        wsel = selt_ref.shape[2]
        for j in range(selt_ref.shape[0]):
            selt_ref[j] = selh[:, j * wsel:(j + 1) * wsel]

        m_ref[...] = new_key_scores(newr_ref[0, 0:1, :])
        l_ref[...] = jnp.ones(l_ref.shape, f32)
        acc_ref[...] = jnp.broadcast_to(newr_ref[0, 1:2, :], acc_ref.shape)

        sw = jnp.dot(qb, win_ref[0, 0].astype(bf16), preferred_element_type=f32)
        sn = new_key_scores(newr_ref[0, 2:3, :])
        mw = jnp.maximum(jnp.max(sw, axis=1, keepdims=True), sn)
        pw, pnw = jnp.exp(sw - mw), jnp.exp(sn - mw)
        lw = jnp.sum(pw, axis=1, keepdims=True) + pnw
        o_win = (_nt_dot(pw.astype(bf16), win_ref[0, 1].astype(bf16)) + pnw * newr_ref[0, 3:4, :]) / lw
        gt = jax.nn.sigmoid(gate_ref[0])
        side_ref[...] = gt[:, 0:1] * o_cmp + gt[:, 2:3] * o_win
        wl = win_ref.shape[3]
        lane = lax.broadcasted_iota(jnp.int32, (NSA_KV_W, wl), 1)
        for kv in range(2):
            wout_ref[0, kv] = jnp.where(lane == wl - 1, newt_ref[0, :, 2 + kv:3 + kv],
                                        pltpu.roll(win_ref[0, kv], wl - 1, 1))

    kt = jnp.concatenate([r[0, 0] for r in pages], axis=1)
    vt = jnp.concatenate([r[0, 1] for r in pages], axis=1)
    s = jnp.dot(qb, kt.astype(bf16), preferred_element_type=f32)
    picked = jnp.dot(selt_ref[cc].astype(bf16), exp_ref[...], preferred_element_type=f32)
    s = s + (picked - 1.0) * (-NEG)
    m_old = m_ref[...]
    m_new = jnp.maximum(m_old, jnp.max(s, axis=1, keepdims=True))
    alpha = jnp.exp(m_old - m_new)
    p = jnp.exp(s - m_new)
    m_ref[...] = m_new
    l_ref[...] = l_ref[...] * alpha + jnp.sum(p, axis=1, keepdims=True)
    acc_ref[...] = acc_ref[...] * alpha + _nt_dot(p.astype(bf16), vt.astype(bf16))

    @pl.when(cc == pl.num_programs(1) - 1)
    def _():
        gt = jax.nn.sigmoid(gate_ref[0])
        o_ref[0] = side_ref[...] + gt[:, 1:2] * acc_ref[...] / l_ref[...]


def _nsa_sample_attention(qr, ckt, cvt, slc_pool, layer_idx, page_table, win_buf, newrows, gate):
    db, n_pages = page_table.shape
    n_pool, nl = slc_pool.shape[:2]
    wlen = win_buf.shape[1]
    nc = ckt.shape[2]
    past = n_pages * PAGE_SIZE
    cur = past // SEL_BLOCK
    npg = STEP_PAGES
    nchunk = n_pages // npg
    keys = npg * PAGE_SIZE
    wsel = npg * CMP_PER_PAGE
    f32, bf16 = jnp.float32, jnp.bfloat16
    view = jnp.transpose(slc_pool, (0, 1, 3, 4, 5, 2)).reshape(n_pool * nl, 2, NSA_KV_W, PAGE_SIZE)
    wint = jnp.transpose(win_buf, (0, 2, 3, 4, 1)).reshape(db, 2, NSA_KV_W, wlen)
    pt = (page_table * nl + layer_idx).reshape(-1).astype(jnp.int32)
    hmask = (jnp.arange(NSA_HEADS)[:, None] // NSA_GROUP == jnp.arange(NSA_KV_HEADS)[None, :]).astype(f32)
    q16 = (qr * HEAD_DIM ** -0.5)[:, :, None, :] * hmask[None, :, :, None]
    q16 = jnp.pad(q16.reshape(db, NSA_HEADS, NSA_KV_W), ((0, 0), (0, STEP_ROWS - NSA_HEADS), (0, 0)))
    newr = jnp.pad(newrows, ((0, 0), (0, 8 - newrows.shape[1]), (0, 0)))
    newt = jnp.pad(newrows.transpose(0, 2, 1), ((0, 0), (0, 0), (0, LANES - newrows.shape[1])))
    g16 = jnp.pad(gate.reshape(db, 3, NSA_HEADS).transpose(0, 2, 1),
                  ((0, 0), (0, STEP_ROWS - NSA_HEADS), (0, LANES - 3)))
    cblk = _cmp_lane_blocks(n_pages)
    jj = cblk % CMP_PER_PAGE
    sblk = np.where(jj % 2 == 0, cblk // 2, -1)
    forced = ((sblk == 0) | (sblk == cur - 1)).astype(np.int32)
    blk8 = np.zeros((8, nc), np.int32)
    blk8[0], blk8[1] = sblk, forced
    loc = np.arange(wsel)
    lstep, lrem = loc // CMP_STEP_LANES, loc % CMP_STEP_LANES
    lj, lpage = lrem // CMP_PAGES_PER_STEP, lstep * CMP_PAGES_PER_STEP + lrem % CMP_PAGES_PER_STEP
    kidx = np.arange(keys)
    expand = ((lj[:, None] % 2 == 0) & (kidx[None, :] // PAGE_SIZE == lpage[:, None])
              & ((kidx[None, :] % PAGE_SIZE) // SEL_BLOCK == lj[:, None] // 2)).astype(np.float32)
    topn = min(NSA_TOPN, cur + 1) - 1

    def page_map(k):
        return lambda b, c, pt_ref: (pt_ref[b * n_pages + c * npg + k], 0, 0, 0)

    per_b = lambda shp: pl.BlockSpec((1,) + shp, lambda b, c, pt_ref: (b,) + (0,) * len(shp))
    const = lambda a: pl.BlockSpec(a.shape, lambda b, c, pt_ref: (0,) * a.ndim)
    consts = (jnp.asarray(blk8), jnp.asarray(expand, bf16))
    grid_spec = pltpu.PrefetchScalarGridSpec(
        num_scalar_prefetch=1, grid=(db, nchunk),
        in_specs=[per_b((STEP_ROWS, NSA_KV_W)), per_b((NSA_KV_W, nc)), per_b((NSA_KV_W, nc))]
        + [pl.BlockSpec((1, 2, NSA_KV_W, PAGE_SIZE), page_map(k)) for k in range(npg)]
        + [per_b((2, NSA_KV_W, wlen)), per_b((8, NSA_KV_W)), per_b((NSA_KV_W, LANES)), per_b((STEP_ROWS, LANES))]
        + [const(a) for a in consts],
        out_specs=[per_b((STEP_ROWS, NSA_KV_W)), per_b((2, NSA_KV_W, wlen))],
        scratch_shapes=[pltpu.VMEM((nc // wsel, STEP_ROWS, wsel), f32),
                        pltpu.VMEM((STEP_ROWS, 1), f32), pltpu.VMEM((STEP_ROWS, 1), f32),
                        pltpu.VMEM((STEP_ROWS, NSA_KV_W), f32), pltpu.VMEM((STEP_ROWS, NSA_KV_W), f32)])
    o16, wout = pl.pallas_call(
        functools.partial(_nsa_step_kernel, topn=topn), grid_spec=grid_spec,
        out_shape=[jax.ShapeDtypeStruct((db, STEP_ROWS, NSA_KV_W), f32),
                   jax.ShapeDtypeStruct((db, 2, NSA_KV_W, wlen), f32)],
        compiler_params=pltpu.CompilerParams(dimension_semantics=("arbitrary", "arbitrary"),
                                             vmem_limit_bytes=VMEM_LIMIT_BYTES),
        name="nsa_sample_attention",
    )(pt, q16, ckt, cvt, *([view] * npg), wint, newr, newt, g16, *consts)
    o = o16[:, :NSA_HEADS].reshape(db, NSA_HEADS, NSA_KV_HEADS, HEAD_DIM)
    o = jnp.take_along_axis(o, (jnp.arange(NSA_HEADS) // NSA_GROUP)[None, :, None, None], axis=2)
    wout = jnp.transpose(wout.reshape(db, 2, NSA_KV_HEADS, HEAD_DIM, wlen), (0, 4, 1, 2, 3))
    return o.reshape(db, NSA_HEADS * HEAD_DIM), wout


DIL_ROW_CHUNK = 64


def _dil_step_kernel(q_ref, buf_ref, newt_ref, newr_ref, bias_ref, o_ref, out_ref, p_ref, pn_ref, den_ref):
    f32, bf16 = jnp.float32, jnp.bfloat16
    kv = pl.program_id(1)
    wlen = buf_ref.shape[3]
    nrow = buf_ref.shape[2]
    q16 = q_ref[0]

    @pl.when(kv == 0)
    def _():
        s = jnp.dot(q16.astype(bf16), buf_ref[0, 0].astype(bf16), preferred_element_type=f32)
        s_new = jnp.sum(q16 * newr_ref[0, 0:1, :], axis=1, keepdims=True)
        ms, es, ens, dens = [], [], [], []
        for g in range(len(DIL_GROUPS)):
            sg = s + bias_ref[g:g + 1, :]
            m = jnp.maximum(jnp.max(sg, axis=1, keepdims=True), s_new)
            e, en = jnp.exp(sg - m), jnp.exp(s_new - m)
            ms.append(m); es.append(e); ens.append(en)
            dens.append(jnp.sum(e, axis=1, keepdims=True) + en)
        m_all = functools.reduce(jnp.maximum, ms)
        ws = [jnp.exp(m - m_all) for m in ms]
        p_ref[...] = sum(w * e for w, e in zip(ws, es))
        pn_ref[...] = sum(w * en for w, en in zip(ws, ens))
        den_ref[...] = sum(w * d for w, d in zip(ws, dens))

    @pl.when(kv == 1)
    def _():
        r = _nt_dot(p_ref[...].astype(bf16), buf_ref[0, 0].astype(bf16))
        r = (r + pn_ref[...] * newr_ref[0, 1:2, :]) / den_ref[...]
        head = lax.broadcasted_iota(jnp.int32, r.shape, 1) // HEAD_DIM
        row = lax.broadcasted_iota(jnp.int32, r.shape, 0)
        o_ref[0] = jnp.broadcast_to(jnp.sum(jnp.where(head == row, r, 0.0), axis=0, keepdims=True), o_ref.shape[1:])

    lane = lax.broadcasted_iota(jnp.int32, (DIL_ROW_CHUNK, wlen), 1)
    for c in range(nrow // DIL_ROW_CHUNK):
        rs = slice(c * DIL_ROW_CHUNK, (c + 1) * DIL_ROW_CHUNK)
        col = jnp.where(kv == 0, newt_ref[0, rs, 0:1], newt_ref[0, rs, 1:2])
        out_ref[0, 0, rs, :] = jnp.where(lane == wlen - 1, col, pltpu.roll(buf_ref[0, 0, rs, :], wlen - 1, 1))


def _dil_sample_attention(qr, kr_new, v_new, buf):
    db, wlen = buf.shape[:2]
    f32 = jnp.float32
    buft = jnp.transpose(buf, (0, 2, 3, 4, 1)).reshape(db, 2, DIL_W, wlen)
    eye = jnp.eye(DIL_HEADS, dtype=f32)
    q16 = ((qr * HEAD_DIM ** -0.5)[:, :, None, :] * eye[None, :, :, None]).reshape(db, DIL_HEADS, DIL_W)
    newr = jnp.pad(jnp.stack([kr_new, v_new], axis=1), ((0, 0), (0, 6), (0, 0)))
    newt = jnp.pad(jnp.stack([kr_new, v_new], axis=2), ((0, 0), (0, 0), (0, LANES - 2)))
    back = wlen - jnp.arange(wlen)
    bias = jnp.stack([jnp.where((back % d == 0) & (back // d <= DIL_SPAN), 0.0, NEG) for _, d in DIL_GROUPS])
    bias = jnp.pad(bias, ((0, 8 - len(DIL_GROUPS)), (0, 0))).astype(f32)
    o, new_buf = pl.pallas_call(
        _dil_step_kernel,
        grid=(db, 2),
        in_specs=[pl.BlockSpec((1, DIL_HEADS, DIL_W), lambda b, k: (b, 0, 0)),
                  pl.BlockSpec((1, 1, DIL_W, wlen), lambda b, k: (b, k, 0, 0)),
                  pl.BlockSpec((1, DIL_W, LANES), lambda b, k: (b, 0, 0)),
                  pl.BlockSpec((1, 8, DIL_W), lambda b, k: (b, 0, 0)),
                  pl.BlockSpec((8, wlen), lambda b, k: (0, 0))],
        out_specs=[pl.BlockSpec((1, 8, DIL_W), lambda b, k: (b, 0, 0)),
                   pl.BlockSpec((1, 1, DIL_W, wlen), lambda b, k: (b, k, 0, 0))],
        out_shape=[jax.ShapeDtypeStruct((db, 8, DIL_W), f32),
                   jax.ShapeDtypeStruct((db, 2, DIL_W, wlen), f32)],
        scratch_shapes=[pltpu.VMEM((DIL_HEADS, wlen), f32), pltpu.VMEM((DIL_HEADS, 1), f32),
                        pltpu.VMEM((DIL_HEADS, 1), f32)],
        compiler_params=pltpu.CompilerParams(dimension_semantics=("arbitrary", "arbitrary"),
                                             vmem_limit_bytes=VMEM_LIMIT_BYTES),
        name="dil_sample",
    )(q16, buft, newt, newr, bias)
    new_buf = jnp.transpose(new_buf.reshape(db, 2, DIL_HEADS, HEAD_DIM, wlen), (0, 4, 1, 2, 3))
    return o[:, 0], new_buf


def _dil_band_kernel(q_ref, kp_ref, kc_ref, vp_ref, vc_ref, num_ref, st_ref):
    f32, bf16 = jnp.float32, jnp.bfloat16
    blk = DIL_BLOCK
    n = pl.program_id(0)
    i = lax.broadcasted_iota(jnp.int32, (blk, 2 * blk), 0)
    j = lax.broadcasted_iota(jnp.int32, (blk, 2 * blk), 1) - blk
    ok = (i - j >= 0) & (i - j <= DIL_SPAN) & (n * blk + j >= 0)
    bias = jnp.where(ok, 0.0, NEG)
    bias = jnp.concatenate([bias, bias], axis=0)
    lane = lax.broadcasted_iota(jnp.int32, (blk, LANES), 1)
    first = lane < HEAD_DIM
    stats = jnp.zeros((blk, LANES), f32)
    for p in range(DIL_HEADS // 2):
        cols = slice(p * LANES, (p + 1) * LANES)
        qp = q_ref[:, cols] * (HEAD_DIM ** -0.5)
        qst = jnp.concatenate([jnp.where(first, qp, 0.0), jnp.where(first, 0.0, qp)], axis=0).astype(bf16)
        kk = jnp.concatenate([kp_ref[:, cols], kc_ref[:, cols]], axis=0).astype(bf16)
        vv = jnp.concatenate([vp_ref[:, cols], vc_ref[:, cols]], axis=0).astype(bf16)
        s = _nt_dot(qst, kk) + bias
        m = jnp.max(s, axis=1, keepdims=True)
        e = jnp.exp(s - m)
        den = jnp.sum(e, axis=1, keepdims=True)
        nm = jnp.dot(e.astype(bf16), vv, preferred_element_type=f32)
        num_ref[:, cols] = jnp.where(first, nm[:blk], nm[blk:])
        for a in range(2):
            h = 2 * p + a
            stats = jnp.where(lane == h, m[a * blk:(a + 1) * blk], stats)
            stats = jnp.where(lane == DIL_HEADS + h, den[a * blk:(a + 1) * blk], stats)
    st_ref[...] = stats


GDN_PREP_ROWS = 512
GDN_TILE_CHUNKS = 4
GDN_PAIRS = GDN_HEADS // 2
GDN_A_LANE = 3 * NSA_HEADS
GDN_B_LANE = GDN_A_LANE + GDN_HEADS


def _hi_lo(x):
    hi = x.astype(jnp.bfloat16)
    return hi, (x - hi.astype(jnp.float32)).astype(jnp.bfloat16)


def _three_way(x):
    f32 = jnp.float32
    x1 = x.astype(jnp.bfloat16)
    r1 = x - x1.astype(f32)
    x2 = r1.astype(jnp.bfloat16)
    return x1, x2, (r1 - x2.astype(f32)).astype(jnp.bfloat16)


def _dot_select(x, sel):
    return sum(jnp.dot(piece, sel, preferred_element_type=jnp.float32) for piece in _three_way(x))


def _select_dot(sel, x):
    return sum(jnp.dot(sel, piece, preferred_element_type=jnp.float32) for piece in _three_way(x))


def _dot_hl(a, b):
    f32 = jnp.float32
    ah, al = _hi_lo(a)
    bh, bl = _hi_lo(b)
    return (jnp.dot(ah, bh, preferred_element_type=f32) + jnp.dot(ah, bl, preferred_element_type=f32)
            + jnp.dot(al, bh, preferred_element_type=f32))


def _gdn_prep_kernel(u_ref, sm_ref, cw_ref, prm_ref, ea_ref, eb_ref, eh_ref, q_ref, k_ref, v_ref, g_ref, b_ref,
                     carry_ref):
    f32 = jnp.float32
    i = pl.program_id(0)
    tl = u_ref.shape[0]

    @pl.when(i == 0)
    def _():
        carry_ref[...] = jnp.zeros(carry_ref.shape, f32)

    u = u_ref[...]
    prev = carry_ref[...]
    row = lax.broadcasted_iota(jnp.int32, (8, u.shape[1]), 0)

    def shifted(k):
        r = pltpu.roll(u, k, 0)
        head = r[:8]
        for j in range(k):
            head = jnp.where(row == j, prev[8 - k + j:8 - k + j + 1], head)
        return jnp.concatenate([head, r[8:]], axis=0)

    cw = cw_ref[...]
    c = cw[0:1] * shifted(3) + cw[1:2] * shifted(2) + cw[2:3] * shifted(1) + cw[3:4] * u
    carry_ref[...] = u[tl - 8:]
    c = c * jax.nn.sigmoid(c)
    eh = eh_ref[...]

    def l2n(x):
        return x * lax.rsqrt(_dot_select(x * x, eh) + NORM_EPS)

    q_ref[...] = l2n(c[:, :GDN_W]) * (HEAD_DIM ** -0.5)
    k_ref[...] = l2n(c[:, GDN_W:2 * GDN_W])
    v_ref[...] = c[:, 2 * GDN_W:]
    sm = sm_ref[...]
    x = sm + prm_ref[1:2]
    softplus = jnp.maximum(x, 0.0) + jnp.log(1.0 + jnp.exp(-jnp.abs(x)))
    g_ref[...] = _dot_select(-jnp.exp(prm_ref[0:1]) * softplus, ea_ref[...])
    b_ref[...] = _dot_select(jax.nn.sigmoid(sm), eb_ref[...])


def _gdn_chunk_kernel(q_ref, k_ref, v_ref, g_ref, b_ref, z_ref, nw_ref, lt_ref, eh_ref, o_ref, s_out_ref, s_ref):
    f32 = jnp.float32
    ch = GDN_CHUNK
    i = pl.program_id(0)

    @pl.when(i == 0)
    def _():
        s_ref[...] = jnp.zeros(s_ref.shape, f32)

    lane = lax.broadcasted_iota(jnp.int32, (ch, LANES), 1)
    first = lane < HEAD_DIM
    stack = lambda x: jnp.concatenate([jnp.where(first, x, 0.0), jnp.where(first, 0.0, x)], axis=0)
    r2 = lax.broadcasted_iota(jnp.int32, (2 * ch, 2 * ch), 0)
    c2 = lax.broadcasted_iota(jnp.int32, (2 * ch, 2 * ch), 1)
    same = (r2 // ch) == (c2 // ch)
    tri = same & (r2 % ch >= c2 % ch)
    strict = same & (r2 % ch > c2 % ch)
    eye = r2 == c2
    eye_f = jnp.where(eye, 1.0, 0.0)
    diag2 = lax.broadcasted_iota(jnp.int32, (ch, LANES), 0) == lane % HEAD_DIM
    lt = lt_ref[...]
    bf = lambda x: x.astype(jnp.bfloat16)
    dot = lambda a, b: jnp.dot(bf(a), bf(b), preferred_element_type=f32)

    blocks = [(c, p) for c in range(GDN_TILE_CHUNKS) for p in range(GDN_PAIRS)]
    ld = lambda ref, c, p: ref[c * ch:(c + 1) * ch, p * LANES:(p + 1) * LANES]
    gcs = [_select_dot(lt, ld(g_ref, c, p)) for c, p in blocks]
    amats, qks, rhs_u, rhs_w, qgs, kds, decs = [], [], [], [], [], [], []
    for (c, p), gc in zip(blocks, gcs):
        kk, qq, vv, bb = ld(k_ref, c, p), ld(q_ref, c, p), ld(v_ref, c, p), ld(b_ref, c, p)
        eg = jnp.exp(gc)
        g_end = gc[ch - 1:ch]
        kb = kk * bb
        col = jnp.concatenate([jnp.broadcast_to(gc[:, 0:1], (ch, LANES)),
                               jnp.broadcast_to(gc[:, HEAD_DIM:HEAD_DIM + 1], (ch, LANES))], axis=0)
        rowv = jnp.sum(jnp.where(diag2, gc, 0.0), axis=0, keepdims=True)
        gam = jnp.where(tri, jnp.exp(jnp.where(tri, col - rowv, 0.0)), 0.0)
        kst = stack(kk)
        amats.append(jnp.where(strict, _nt_dot(bf(stack(kb)), bf(kst)) * gam, 0.0))
        qks.append(jnp.where(tri, _nt_dot(bf(stack(qq)), bf(kst)) * gam, 0.0))
        rhs_u.append(stack(vv * bb))
        rhs_w.append(stack(kb * eg))
        qgs.append(stack(qq * eg))
        kds.append(stack(kk * jnp.exp(g_end - gc)))
        decs.append(jnp.sum(jnp.where(eye, jnp.exp(g_end), 0.0), axis=1, keepdims=True))
    xs = [eye_f - a for a in amats]
    pws = [_dot_hl(a, a) for a in amats]
    steps = GDN_CHUNK.bit_length() - 2
    for r in range(steps):
        xs = [x + _dot_hl(x, pw) for x, pw in zip(xs, pws)]
        if r < steps - 1:
            pws = [_dot_hl(pw, pw) for pw in pws]
    uus = [dot(x, u) for x, u in zip(xs, rhs_u)]
    wws = [dot(x, w) for x, w in zip(xs, rhs_w)]
    kdts = [kd.T for kd in kds]
    states = [s_ref[p] for p in range(GDN_PAIRS)]
    for c in range(GDN_TILE_CHUNKS):
        rs = slice(c * ch, (c + 1) * ch)
        outs = []
        for p in range(GDN_PAIRS):
            n = c * GDN_PAIRS + p
            s = states[p]
            v_new = uus[n] - dot(wws[n], s)
            o_st = dot(qgs[n], s) + dot(qks[n], v_new)
            states[p] = s * decs[n] + dot(kdts[n], v_new)
            outs.append(o_st[:ch] + o_st[ch:])
        o = jnp.concatenate(outs, axis=1)
        ms = _dot_select(o * o, eh_ref[...]) * (1.0 / HEAD_DIM)
        z = z_ref[rs, :]
        o_ref[rs, :] = o * lax.rsqrt(ms + NORM_EPS) * nw_ref[...] * (z * jax.nn.sigmoid(z))
    for p in range(GDN_PAIRS):
        s_ref[p] = states[p]

    @pl.when(i == pl.num_programs(0) - 1)
    def _():
        s_out_ref[...] = s_ref[...]


def _gdn_prompt(qkv, small, z, conv_w, a_log, dt_bias, norm_w):
    l = qkv.shape[0]
    f32, bf16 = jnp.float32, jnp.bfloat16
    w = GDN_W
    hh = jnp.arange(w) // HEAD_DIM
    expander = lambda base: (jnp.arange(LANES)[:, None] == base + hh[None, :]).astype(bf16)
    eh = (hh[:, None] == hh[None, :]).astype(bf16)
    cw8 = jnp.zeros((8, 3 * w), f32).at[:GDN_CONV].set(conv_w)
    prm = jnp.zeros((8, LANES), f32)
    prm = prm.at[0, GDN_A_LANE:GDN_A_LANE + GDN_HEADS].set(a_log).at[1, GDN_A_LANE:GDN_A_LANE + GDN_HEADS].set(dt_bias)
    tl = GDN_PREP_ROWS
    row = lambda wd: pl.BlockSpec((tl, wd), lambda i: (i, 0))
    const = lambda a: pl.BlockSpec(a.shape, lambda i: (0,) * a.ndim)
    ea, eb = expander(GDN_A_LANE), expander(GDN_B_LANE)
    q, k, v, g, b = pl.pallas_call(
        _gdn_prep_kernel,
        grid=(l // tl,),
        in_specs=[row(3 * w), row(LANES), const(cw8), const(prm), const(ea), const(eb), const(eh)],
        out_specs=[row(w)] * 5,
        out_shape=[jax.ShapeDtypeStruct((l, w), f32)] * 5,
        scratch_shapes=[pltpu.VMEM((8, 3 * w), f32)],
        compiler_params=pltpu.CompilerParams(dimension_semantics=("arbitrary",), vmem_limit_bytes=VMEM_LIMIT_BYTES),
        name="gdn_prep",
    )(qkv, small, cw8, prm, ea, eb, eh)
    tc = GDN_TILE_CHUNKS * GDN_CHUNK
    lt = (jnp.arange(GDN_CHUNK)[:, None] >= jnp.arange(GDN_CHUNK)[None, :]).astype(bf16)
    nw = jnp.tile(norm_w, GDN_HEADS).reshape(1, w)
    rowc = pl.BlockSpec((tc, w), lambda i: (i, 0))
    o, s_bd = pl.pallas_call(
        _gdn_chunk_kernel,
        grid=(l // tc,),
        in_specs=[rowc] * 6 + [const(nw), const(lt), const(eh)],
        out_specs=[rowc, pl.BlockSpec((GDN_PAIRS, LANES, LANES), lambda i: (0, 0, 0))],
        out_shape=[jax.ShapeDtypeStruct((l, w), f32), jax.ShapeDtypeStruct((GDN_PAIRS, LANES, LANES), f32)],
        scratch_shapes=[pltpu.VMEM((GDN_PAIRS, LANES, LANES), f32)],
        compiler_params=pltpu.CompilerParams(dimension_semantics=("arbitrary",), vmem_limit_bytes=VMEM_LIMIT_BYTES),
        name="gdn_chunk",
    )(q, k, v, g, b, z, nw, lt, eh)
    s4 = s_bd.reshape(GDN_PAIRS, 2, HEAD_DIM, 2, HEAD_DIM)
    s_fin = jnp.stack([s4[:, 0, :, 0], s4[:, 1, :, 1]], axis=1).reshape(GDN_HEADS, HEAD_DIM, HEAD_DIM)
    return o, s_fin


def _proj_dil_kernel(x_ref, w_ref, cos_ref, sin_ref, bf_ref, kv_ref):
    acc = jnp.dot(x_ref[...].astype(jnp.bfloat16), w_ref[...], preferred_element_type=jnp.float32)
    reps = 2 * DIL_W // LANES
    qk = _rope_lanes(acc[:, :2 * DIL_W], _lane_tile(cos_ref[...], reps), _lane_tile(sin_ref[...], reps))
    v = acc[:, 2 * DIL_W:]
    bf_ref[...] = jnp.concatenate([qk, v], axis=1).astype(jnp.bfloat16)
    kv_ref[...] = jnp.concatenate([qk[:, DIL_W:], v], axis=1)


def _proj_dil(x, w_in):
    l, d = x.shape
    tm = _row_tile(l)
    cos, sin = _rope_tables(jnp.arange(l), LANES)
    return pl.pallas_call(
        _proj_dil_kernel,
        grid=(l // tm,),
        in_specs=[pl.BlockSpec((tm, d), lambda i: (i, 0)),
                  pl.BlockSpec((d, 3 * DIL_W), lambda i: (0, 0)),
                  pl.BlockSpec((tm, LANES), lambda i: (i, 0)),
                  pl.BlockSpec((tm, LANES), lambda i: (i, 0))],
        out_specs=[pl.BlockSpec((tm, 3 * DIL_W), lambda i: (i, 0)),
                   pl.BlockSpec((tm, 2 * DIL_W), lambda i: (i, 0))],
        out_shape=[jax.ShapeDtypeStruct((l, 3 * DIL_W), jnp.bfloat16),
                   jax.ShapeDtypeStruct((l, 2 * DIL_W), jnp.float32)],
        compiler_params=pltpu.CompilerParams(dimension_semantics=("arbitrary",), vmem_limit_bytes=VMEM_LIMIT_BYTES),
        name="proj_dil",
    )(x, w_in.astype(jnp.bfloat16), cos, sin)


def _dil_band_stats(qkv, d):
    l = qkv.shape[0]
    assert l % (d * DIL_BLOCK) == 0
    nb = l // (d * DIL_BLOCK)
    view = qkv.reshape(l // d, d * 3 * DIL_W)

    def part(which, prev):
        return pl.BlockSpec((DIL_BLOCK, DIL_W), (lambda n, r: (jnp.maximum(n - 1, 0), 3 * r + which)) if prev
                            else (lambda n, r: (n, 3 * r + which)))

    out = lambda w: pl.BlockSpec((DIL_BLOCK, w), lambda n, r: (n, r))
    num, st = pl.pallas_call(
        _dil_band_kernel,
        grid=(nb, d),
        in_specs=[part(0, False), part(1, True), part(1, False), part(2, True), part(2, False)],
        out_specs=[out(DIL_W), out(LANES)],
        out_shape=[jax.ShapeDtypeStruct((l // d, d * DIL_W), jnp.float32),
                   jax.ShapeDtypeStruct((l // d, d * LANES), jnp.float32)],
        compiler_params=pltpu.CompilerParams(dimension_semantics=("arbitrary", "arbitrary"),
                                             vmem_limit_bytes=VMEM_LIMIT_BYTES),
        name="dil_band_stats",
    )(view, view, view, view, view)
    return num.reshape(l, DIL_W), st.reshape(l, LANES)


def _dil_merge_kernel(*refs):
    f32 = jnp.float32
    ng = len(DIL_GROUPS)
    nums, sts = refs[:ng], refs[ng:2 * ng]
    ex_ref, w_ref, res_ref, g_ref, b_ref, o_ref = refs[2 * ng:]
    st = [r[...] for r in sts]
    m_all = functools.reduce(jnp.maximum, st)
    ws = [jnp.exp(s - m_all) for s in st]
    den = sum(w * pltpu.roll(s, LANES - DIL_HEADS, 1) for w, s in zip(ws, st))
    head_lane = lax.broadcasted_iota(jnp.int32, den.shape, 1) < DIL_HEADS
    o = sum(_dot_select(jnp.where(head_lane, w / den, 0.0), ex_ref[...]) * n[...]
            for w, n in zip(ws, nums))
    acc = jnp.dot(o.astype(jnp.bfloat16), w_ref[...], preferred_element_type=f32)
    o_ref[...] = _layer_norm_rows(DEEPNORM_ALPHA * res_ref[...] + acc, g_ref[...], b_ref[...])


def _dil_merge_proj(nums, sts, w_out, res, g, b):
    l, n = res.shape
    tm = _row_tile(l)
    expand = (jnp.arange(LANES)[:, None] == jnp.arange(DIL_W)[None, :] // HEAD_DIM).astype(jnp.bfloat16)
    row = lambda w: pl.BlockSpec((tm, w), lambda i: (i, 0))
    const = lambda a: pl.BlockSpec(a.shape, lambda i: (0,) * a.ndim)
    wb = w_out.astype(jnp.bfloat16)
    g2, b2 = g.reshape(1, n), b.reshape(1, n)
    return pl.pallas_call(
        _dil_merge_kernel,
        grid=(l // tm,),
        in_specs=[row(DIL_W)] * len(nums) + [row(LANES)] * len(sts) + [const(expand), const(wb), row(n),
                                                                       const(g2), const(b2)],
        out_specs=row(n),
        out_shape=jax.ShapeDtypeStruct((l, n), jnp.float32),
        compiler_params=pltpu.CompilerParams(dimension_semantics=("arbitrary",), vmem_limit_bytes=VMEM_LIMIT_BYTES),
        name="dil_merge_proj",
    )(*nums, *sts, expand, wb, res, g2, b2)


def _split_cols(h, widths):
    parts, start = [], 0
    for w in widths:
        parts.append(h[..., start:start + w])
        start += w
    return parts


def _even_widths():
    return (NSA_Q_W,) + (NSA_KV_W,) * 6 + (3 * NSA_HEADS, 3 * GDN_W, GDN_HEADS, GDN_HEADS, GDN_W)


def _rms_norm(x, w):
    return x * lax.rsqrt(jnp.mean(jnp.square(x), axis=-1, keepdims=True) + NORM_EPS) * w


def _l2_norm(x):
    return x * lax.rsqrt(jnp.sum(jnp.square(x), axis=-1, keepdims=True) + NORM_EPS)


def _rope(x, pos):
    half = HEAD_DIM // 2
    inv_freq = ROPE_THETA ** (-2.0 * jnp.arange(half, dtype=jnp.float32) / HEAD_DIM)
    ang = pos.astype(jnp.float32)[:, None] * inv_freq[None, :]
    cos, sin = jnp.cos(ang)[:, None, :], jnp.sin(ang)[:, None, :]
    xf = x.astype(jnp.float32)
    x1, x2 = xf[..., :half], xf[..., half:]
    return jnp.concatenate([x1 * cos - x2 * sin, x2 * cos + x1 * sin], axis=-1)


def _causal_dwconv(hist, u, w):
    width, s = w.shape[0], u.shape[1]
    ext = jnp.concatenate([hist.astype(u.dtype), u], axis=1)
    out = w[0] * ext[:, :s]
    for j in range(1, width):
        out = out + w[j] * ext[:, j:j + s]
    return out, ext[:, s:]


def _nsa_compress(rows, w1, b1, w2, pe):
    b, l, g, dh = rows.shape
    nc = l // CMP_BLOCK
    blk = rows[:, :nc * CMP_BLOCK].astype(jnp.float32).reshape(b, nc, CMP_BLOCK, g, dh) + pe[:, None, :]
    flat = blk.transpose(0, 1, 3, 2, 4).reshape(b, nc, g, CMP_BLOCK * dh)
    return jax.nn.gelu(flat @ w1 + b1) @ w2


def _nsa_compressed_kv(k_rows, v_rows, cw1, cb1, cw2, cpe):
    ck = _nsa_compress(k_rows, cw1[0], cb1[0], cw2[0], cpe[0])
    cv = _nsa_compress(v_rows, cw1[1], cb1[1], cw2[1], cpe[1])
    nc = ck.shape[1]
    ck = _rope(ck, (jnp.arange(nc) + 1) * CMP_BLOCK - 1)
    return ck, cv


def _rope_rows_kernel(*refs):
    cos_ref, sin_ref = refs[:2]
    n = (len(refs) - 2) // 2
    for x_ref, o_ref in zip(refs[2:2 + n], refs[2 + n:]):
        reps = x_ref.shape[1] // LANES
        o_ref[...] = _rope_lanes(x_ref[...], _lane_tile(cos_ref[...], reps), _lane_tile(sin_ref[...], reps))


def _rope_rows(xs):
    l = xs[0].shape[0]
    tm = _row_tile(l)
    cos, sin = _rope_tables(jnp.arange(l), LANES)
    row = lambda w: pl.BlockSpec((tm, w), lambda i: (i, 0))
    return pl.pallas_call(
        _rope_rows_kernel,
        grid=(l // tm,),
        in_specs=[row(LANES), row(LANES)] + [row(x.shape[1]) for x in xs],
        out_specs=[row(x.shape[1]) for x in xs],
        out_shape=[jax.ShapeDtypeStruct(x.shape, jnp.float32) for x in xs],
        compiler_params=pltpu.CompilerParams(dimension_semantics=("arbitrary",), vmem_limit_bytes=VMEM_LIMIT_BYTES),
        name="rope_rows",
    )(cos, sin, *xs)


def _nsa_prompt(q, kc, vc, ks, vs, kw, vw, small, cw1, cb1, cw2, cpe):
    b, l = q.shape[:2]
    flat = lambda t: t.reshape(t.shape[1], -1)
    qr, ksr, kwr = _rope_rows([flat(q), flat(ks), flat(kw)])
    ksr, kwr = ksr.reshape(ks.shape), kwr.reshape(kw.shape)
    ck, cv = _nsa_compressed_kv(kc, vc, cw1, cb1, cw2, cpe)
    vsf = vs.astype(jnp.float32)
    vwf = vw.astype(jnp.float32)
    o_nsa = _nsa_prompt_attention(qr, small, flat(ck), flat(cv), flat(ksr), flat(vsf), flat(kwr), flat(vwf))
    keep = min(NSA_WINDOW, l)
    rows_cmp = jnp.stack([kc, vc], axis=2)
    rows_slc = jnp.stack([ksr, vsf], axis=2)
    rows_win = jnp.stack([kwr[:, l - keep:], vwf[:, l - keep:]], axis=2)
    return o_nsa[None], rows_cmp, rows_slc, rows_win


def _nsa_sample(q, kc, vc, ks, vs, kw, vw, gate, cmp_pool, slc_pool, layer_idx, win_buf, page_table,
                cw1, cb1, cw2, cpe):
    db, s = q.shape[:2]
    past = page_table.shape[1] * PAGE_SIZE
    wb = win_buf.shape[1]
    assert s == 1 and wb == NSA_WINDOW and past >= wb and past % (STEP_PAGES * PAGE_SIZE) == 0
    qpos = past + jnp.arange(s)
    qr = _rope(q, qpos)
    ckt, cvt = _nsa_sample_compress(cmp_pool, layer_idx, page_table, cw1, cb1, cw2, cpe)
    ksr = _rope(ks, qpos)
    vsf = vs.astype(jnp.float32)
    kwr = _rope(kw, qpos)
    vwf = vw.astype(jnp.float32)
    newrows = jnp.stack([t.reshape(db, NSA_KV_W) for t in (ksr, vsf, kwr, vwf)], axis=1)
    o_nsa, rows_win = _nsa_sample_attention(qr[:, 0], ckt, cvt, slc_pool, layer_idx, page_table, win_buf,
                                              newrows, gate.reshape(db, -1))
    rows_cmp = jnp.stack([kc, vc], axis=2)
    rows_slc = jnp.stack([ksr, vsf], axis=2)
    return o_nsa[:, None], rows_cmp, rows_slc, rows_win


def _gdn_recurrent(q, k, v, g, beta, s0):
    def step(state, xs):
        q_t, k_t, v_t, g_t, b_t = xs
        state = state * jnp.exp(g_t)[..., None, None]
        v_t = (v_t - jnp.einsum('bhk,bhkv->bhv', k_t, state)) * b_t[..., None]
        state = state + jnp.einsum('bhk,bhv->bhkv', k_t, v_t)
        return state, jnp.einsum('bhk,bhkv->bhv', q_t, state)

    xs = tuple(jnp.moveaxis(a, 1, 0) for a in (q, k, v, g, beta))
    s_fin, o = lax.scan(step, s0, xs)
    return jnp.moveaxis(o, 0, 1), s_fin


def _gdn_step(qkv, a, bt, z, conv_hist, s0, conv_w, a_log, dt_bias, norm_w):
    b, s = qkv.shape[:2]
    c, new_hist = _causal_dwconv(conv_hist, qkv, conv_w)
    c = jax.nn.silu(c.astype(jnp.float32))
    q, k, v = [t.reshape(b, s, GDN_HEADS, HEAD_DIM) for t in jnp.split(c, 3, axis=-1)]
    q = _l2_norm(q) * HEAD_DIM ** -0.5
    k = _l2_norm(k)
    beta = jax.nn.sigmoid(bt.astype(jnp.float32))
    g = -jnp.exp(a_log) * jax.nn.softplus(a.astype(jnp.float32) + dt_bias)
    o, s_fin = _gdn_recurrent(q, k, v, g, beta, s0.astype(jnp.float32))
    o = _rms_norm(o, norm_w) * jax.nn.silu(z.astype(jnp.float32).reshape(b, s, GDN_HEADS, HEAD_DIM))
    return o.reshape(b, s, GDN_W), new_hist, s_fin


def _proj(x, w):
    b, s, d = x.shape
    n = w.shape[1]
    npad = -(-n // LANES) * LANES
    wp = jnp.pad(w, ((0, 0), (0, npad - n)))
    return _matmul(x.reshape(b * s, d), wp).reshape(b, s, npad)


def _even_prompt(x, w_in, cw1, cb1, cw2, cpe, conv_w, a_log, dt_bias, norm_w):
    b, l, _ = x.shape
    q, kc, vc, ks, vs, kw, vw, gate, qkv, a, bt, z = _split_cols(_proj(x, w_in), _even_widths())
    heads = lambda t: t.reshape(b, l, -1, HEAD_DIM)
    assert b == 1
    small = jnp.concatenate([gate, a, bt], axis=-1).reshape(l, -1)
    small = jnp.pad(small, ((0, 0), (0, LANES - small.shape[-1])))
    o_nsa, r_cmp, r_slc, r_win = _nsa_prompt(
        heads(q), heads(kc), heads(vc), heads(ks), heads(vs), heads(kw), heads(vw), small, cw1, cb1, cw2, cpe)
    o_gdn, s_fin = _gdn_prompt(qkv[0], small, z[0], conv_w, a_log, dt_bias, norm_w)
    conv_hist = qkv[:, l - (GDN_CONV - 1):]
    return [o_nsa[0], o_gdn], r_cmp, r_slc, r_win, conv_hist, s_fin[None]


def _even_sample(x, cmp_pool, slc_pool, layer_idx, win_buf, conv_hist, s0, page_table,
                 w_in, cw1, cb1, cw2, cpe, conv_w, a_log, dt_bias, norm_w):
    b, s, _ = x.shape
    q, kc, vc, ks, vs, kw, vw, gate, qkv, a, bt, z = _split_cols(_proj(x, w_in), _even_widths())
    heads = lambda t: t.reshape(b, s, -1, HEAD_DIM)
    o_nsa, r_cmp, r_slc, r_win = _nsa_sample(
        heads(q), heads(kc), heads(vc), heads(ks), heads(vs), heads(kw), heads(vw), gate,
        cmp_pool, slc_pool, layer_idx, win_buf, page_table, cw1, cb1, cw2, cpe)
    o_gdn, new_hist, s_fin = _gdn_step(qkv, a, bt, z, conv_hist, s0, conv_w, a_log, dt_bias, norm_w)
    return [o_nsa[:, 0], o_gdn[:, 0]], r_cmp, r_slc, r_win, new_hist, s_fin


def _dil_prompt(x, w_in, w_out, g, b):
    bsz, l, _ = x.shape
    assert bsz == 1
    qkv, kv = _proj_dil(x[0], w_in)
    stats = [_dil_band_stats(qkv, d) for _, d in DIL_GROUPS]
    y = _dil_merge_proj([n for n, _ in stats], [s for _, s in stats], w_out, x[0], g, b)
    keep = min(DIL_MAX_WINDOW, l)
    buf = kv[l - keep:].reshape(1, keep, 2, DIL_HEADS, HEAD_DIM)
    return y, buf


def _dil_sample(x, buf, past, w_in):
    db, s, _ = x.shape
    q, k, v = [t.reshape(db, s, DIL_HEADS, HEAD_DIM) for t in jnp.split(_proj(x, w_in), 3, axis=-1)]
    assert s == 1 and buf.shape[1] == DIL_MAX_WINDOW <= past
    qpos = past + jnp.arange(s)
    qr, kr = _rope(q, qpos), _rope(k, qpos)
    o, new_buf = _dil_sample_attention(qr[:, 0], kr.reshape(db, DIL_W), v.reshape(db, DIL_W).astype(jnp.float32), buf)
    return o[:, None], new_buf


def kernel(x_prompt, x_sample, cache_nsa_cmp_kv, cache_nsa_slc_kv, state_nsa_win_kv, state_gdn_conv,
           state_gdn_S, state_dil_kv, state_ffn_conv, page_table, w_in_a, nsa_cmp_w1, nsa_cmp_b1, nsa_cmp_w2,
           nsa_cmp_pe, gdn_conv_w, gdn_A_log, gdn_dt_bias, gdn_norm_w, w_out_a, w_in_c, w_out_c,
           ln_mix_g, ln_mix_b, ffn_w_in, ffn_conv_w, ffn_conv_b, ffn_w_out, ln_ffn_g, ln_ffn_b):
    past = page_table.shape[1] * PAGE_SIZE
    bp, lp, d = x_prompt.shape
    bs, ls, _ = x_sample.shape
    assert bp == 1 and ls == 1
    xp, xs = x_prompt, x_sample
    cmp_p, cmp_s, slc_p, slc_s, win_p, win_s = [], [], [], [], [], []
    gconv_p, gconv_s, gstate_p, gstate_s = [], [], [], []
    dil_p, dil_s, ffn_p, ffn_s = [], [], [], []
    for layer in range(DEPTH):
        if layer % 2 == 0:
            la = layer // 2
            wa = (w_in_a[la], nsa_cmp_w1[la], nsa_cmp_b1[la], nsa_cmp_w2[la], nsa_cmp_pe[la],
                  gdn_conv_w[la], gdn_A_log[la], gdn_dt_bias[la], gdn_norm_w[la])
            mp, rc, rs, rw, hc, hs_ = _even_prompt(xp, *wa)
            cmp_p.append(rc); slc_p.append(rs); win_p.append(rw); gconv_p.append(hc); gstate_p.append(hs_)
            ms, rc, rs, rw, hc, hs_ = _even_sample(xs, cache_nsa_cmp_kv, cache_nsa_slc_kv, la,
                                                   state_nsa_win_kv[:, la], state_gdn_conv[:, la],
                                                   state_gdn_S[:, la], page_table, *wa)
            cmp_s.append(rc); slc_s.append(rs); win_s.append(rw); gconv_s.append(hc); gstate_s.append(hs_)
            w_out = w_out_a[la]
        else:
            lc = layer // 2
            xp2, bpf = _dil_prompt(xp, w_in_c[lc], w_out_c[lc], ln_mix_g[layer], ln_mix_b[layer])
            o_dil, bsf = _dil_sample(xs, state_dil_kv[:, lc], past, w_in_c[lc])
            ms = [o_dil.reshape(bs, -1)]
            dil_p.append(bpf); dil_s.append(bsf)
            w_out = w_out_c[lc]
        if layer % 2 == 0:
            xp2 = _matmul_ln(mp, w_out, xp.reshape(lp, d), ln_mix_g[layer], ln_mix_b[layer])
        xs2 = _matmul_ln(ms, w_out, xs.reshape(bs, d), ln_mix_g[layer], ln_mix_b[layer])
        fargs = (ffn_w_in[layer], ffn_conv_w[layer], ffn_conv_b[layer], ffn_w_out[layer],
                 ln_ffn_g[layer], ln_ffn_b[layer])
        xp3, hp = _ffn_seq(xp2, *fargs)
        xs3, hs = _ffn_step(xs2, state_ffn_conv[:, layer], *fargs)
        xp, xs = xp3.reshape(1, lp, d), xs3.reshape(bs, 1, d)
        ffn_p.append(hp[None]); ffn_s.append(hs)

    def stk(lst):
        return jnp.stack(lst, axis=1)

    return (xp, xs, stk(cmp_p), stk(cmp_s), stk(slc_p), stk(slc_s), stk(win_p), stk(win_s),
            stk(gconv_p), stk(gconv_s), stk(gstate_p), stk(gstate_s), stk(dil_p), stk(dil_s),
            stk(ffn_p), stk(ffn_s))
```
